```python
import jax, jax.numpy as jnp
from jax import lax
import numpy as np

D_MODEL = 2048
BATCH = 8
SEQ = 4096
DEPTH = 1

CHUNK = 64

GLA_HEADS = 4
GLA_DK = 128
GLA_DV = 256
GLA_QK = GLA_HEADS * GLA_DK
GLA_V = GLA_HEADS * GLA_DV
GLA_LORA = 16
GLA_TAU = 16.0

RWKV_HEADS = 16
RWKV_HD = 64
RWKV_W = RWKV_HEADS * RWKV_HD
DECAY_LORA = 96
AAA_LORA = 96
GATE_LORA = 256
GN_EPS = 64e-5

N_BRANCH = 2
D_FF = 5632
NORM_EPS = 1e-6

GLA_SPLITS = (GLA_QK, GLA_QK, GLA_V, GLA_V, GLA_LORA)
GLA_IN = 2 * GLA_QK + 2 * GLA_V + GLA_LORA
RWKV_SPLITS = (RWKV_W, RWKV_W, RWKV_W, DECAY_LORA, AAA_LORA, GATE_LORA)
RWKV_IN = 3 * RWKV_W + DECAY_LORA + AAA_LORA + GATE_LORA
D_IN = GLA_IN + RWKV_IN + N_BRANCH * D_MODEL
BRANCH_IN = GLA_V + RWKV_W

kernel_name = "hybrid_gla_rwkv7_macaron_block"


def _split(p, sizes):
    out, off = [], 0
    for s in sizes:
        out.append(p[..., off:off + s])
        off += s
    return out


def rmsnorm(x, g):
    xf = x.astype(jnp.float32)
    y = xf * lax.rsqrt(jnp.mean(xf * xf, axis=-1, keepdims=True) + NORM_EPS)
    return (y * g.astype(jnp.float32)).astype(x.dtype)


def swiglu(h, wg, wu, wd):
    return (jax.nn.silu(h @ wg) * (h @ wu)) @ wd


def token_shift(p, mu):
    prev = jnp.pad(p, ((0, 0), (1, 0), (0, 0)))[:, :-1]
    return p + mu * (prev - p)


def gla_branch(q, k, v, r, a_down, w_a2, b_a, gn_w):
    B, S, _ = q.shape
    nc = S // CHUNK
    f32 = jnp.float32
    log_alpha = jax.nn.log_sigmoid(a_down.astype(f32) @ w_a2.astype(f32) + b_a.astype(f32)) / GLA_TAU
    shp = (B, nc, CHUNK, GLA_HEADS, GLA_DK)
    qf = q.astype(f32).reshape(shp) * (GLA_DK ** -0.5)
    kf = k.astype(f32).reshape(shp)
    vf = v.astype(f32).reshape(B, nc, CHUNK, GLA_HEADS, GLA_DV)
    cum = jnp.cumsum(log_alpha.reshape(shp), axis=2)
    total = cum[:, :, -1]
    kdec = kf * jnp.exp(total[:, :, None] - cum)
    u = jnp.einsum('bnchk,bnchv->nbhkv', kdec, vf)

    def step(state, inp):
        lt, uc = inp
        state = jnp.exp(lt)[..., None] * state + uc
        return state, state

    s0 = jnp.zeros((B, GLA_HEADS, GLA_DK, GLA_DV), f32)
    _, states = lax.scan(step, s0, (jnp.moveaxis(total, 1, 0), u))
    o = jnp.einsum('bnchk,nbhkv->bnchv', qf, states)
    o = o * lax.rsqrt(jnp.mean(o * o, axis=-1, keepdims=True) + NORM_EPS) * gn_w.astype(f32)
    o = o.reshape(B, S, GLA_V) * jax.nn.silu(r.astype(f32))
    return o.astype(q.dtype)


def rwkv7_branch(r, k, v, wd, ad, gd, w0, w_w2, a0, w_a2, w_g2, k_k, k_a, r_k, lnx_w, lnx_b):
    B, S, _ = r.shape
    f32 = jnp.float32
    r, k, v = r.astype(f32), k.astype(f32), v.astype(f32)
    w_raw = w0.astype(f32) + jnp.tanh(wd.astype(f32)) @ w_w2.astype(f32)
    log_w = -jnp.exp(-jax.nn.softplus(-w_raw) - 0.5)
    a = jax.nn.sigmoid(a0.astype(f32) + ad.astype(f32) @ w_a2.astype(f32))
    g = jax.nn.sigmoid(gd.astype(f32)) @ w_g2.astype(f32)
    hs = (B, S, RWKV_HEADS, RWKV_HD)
    kk = (k * k_k.astype(f32)).reshape(hs)
    kk = kk / jnp.maximum(jnp.linalg.norm(kk, axis=-1, keepdims=True), 1e-12)
    k = k * (1.0 + (a - 1.0) * k_a.astype(f32))
    rh, kh, vh, ah = r.reshape(hs), k.reshape(hs), v.reshape(hs), a.reshape(hs)
    decay = jnp.exp(log_w).reshape(hs)
    b = kk * ah

    def step(state, inp):
        r_t, w_t, k_t, v_t, kk_t, b_t = inp
        sa = jnp.einsum('bhvk,bhk->bhv', state, kk_t)
        state = (state * w_t[:, :, None, :] - sa[..., None] * b_t[:, :, None, :]
                 + v_t[..., None] * k_t[:, :, None, :])
        return state, jnp.einsum('bhvk,bhk->bhv', state, r_t)

    xs = tuple(jnp.moveaxis(t, 1, 0) for t in (rh, decay, kh, vh, kk, b))
    s0 = jnp.zeros((B, RWKV_HEADS, RWKV_HD, RWKV_HD), f32)
    _, y = lax.scan(step, s0, xs)
    y = jnp.moveaxis(y, 0, 1)
    mu = jnp.mean(y, axis=-1, keepdims=True)
    var = jnp.mean(jnp.square(y - mu), axis=-1, keepdims=True)
    yn = ((y - mu) * lax.rsqrt(var + GN_EPS)).reshape(B, S, RWKV_W)
    yn = yn * lnx_w.astype(f32) + lnx_b.astype(f32)
    bonus = (jnp.sum(rh * kh * r_k.astype(f32), axis=-1, keepdims=True) * vh).reshape(B, S, RWKV_W)
    return ((yn + bonus) * g).astype(r.dtype)


def hybrid_mixer(h, w_in, gla_w_a2, gla_b_a, gla_gn_w, rwkv_mu, rwkv_w0, rwkv_w_w2,
                 rwkv_a0, rwkv_w_a2, rwkv_w_g2, rwkv_k_k, rwkv_k_a, rwkv_r_k,
                 rwkv_lnx_w, rwkv_lnx_b, gate_b, w_branch, w_out):
    p = h @ w_in
    gla_p = p[..., :GLA_IN]
    rw_p = token_shift(p[..., GLA_IN:GLA_IN + RWKV_IN], rwkv_mu)
    gate_p = p[..., GLA_IN + RWKV_IN:]
    gq, gk, gv, gr, gad = _split(gla_p, GLA_SPLITS)
    rr, rk, rv, rwd, rad, rgd = _split(rw_p, RWKV_SPLITS)
    o_gla = gla_branch(gq, gk, gv, gr, gad, gla_w_a2, gla_b_a, gla_gn_w)
    o_rw = rwkv7_branch(rr, rk, rv, rwd, rad, rgd, rwkv_w0, rwkv_w_w2, rwkv_a0, rwkv_w_a2,
                        rwkv_w_g2, rwkv_k_k, rwkv_k_a, rwkv_r_k, rwkv_lnx_w, rwkv_lnx_b)
    gates = jax.nn.sigmoid((gate_p + gate_b).astype(jnp.float32))
    y_gla = (o_gla @ w_branch[:GLA_V]).astype(jnp.float32)
    y_rw = (o_rw @ w_branch[GLA_V:]).astype(jnp.float32)
    merged = gates[..., :D_MODEL] * y_gla + gates[..., D_MODEL:] * y_rw
    return merged.astype(h.dtype) @ w_out


def _fwd_setup_inputs(seed: int = 0) -> dict:
    key = jax.random.key(seed)
    ks = iter(jax.random.split(key, 40))
    L, D = DEPTH, D_MODEL

    def nrm(shape, scale):
        return scale * jax.random.normal(next(ks), shape, jnp.float32)

    def gain(shape):
        return 1.0 + nrm(shape, 0.02)

    return {
        "x": nrm((BATCH, SEQ, D), 1.0),
        "ffn1_norm": gain((L, D)),
        "ffn1_wg": nrm((L, D, D_FF), D ** -0.5),
        "ffn1_wu": nrm((L, D, D_FF), D ** -0.5),
        "ffn1_wd": nrm((L, D_FF, D), D_FF ** -0.5),
        "mix_norm": gain((L, D)),
        "w_in": nrm((L, D, D_IN), D ** -0.5),
        "gla_w_a2": nrm((L, GLA_LORA, GLA_QK), GLA_LORA ** -0.5),
        "gla_b_a": nrm((L, GLA_QK), 0.1),
        "gla_gn_w": gain((L, GLA_DV)),
        "rwkv_mu": jax.random.uniform(next(ks), (L, RWKV_IN), jnp.float32, 0.0, 1.0),
        "rwkv_w0": -2.0 + nrm((L, RWKV_W), 0.5),
        "rwkv_w_w2": nrm((L, DECAY_LORA, RWKV_W), 0.3 * DECAY_LORA ** -0.5),
        "rwkv_a0": nrm((L, RWKV_W), 0.1),
        "rwkv_w_a2": nrm((L, AAA_LORA, RWKV_W), 0.3 * AAA_LORA ** -0.5),
        "rwkv_w_g2": nrm((L, GATE_LORA, RWKV_W), GATE_LORA ** -0.5),
        "rwkv_k_k": 0.85 + nrm((L, RWKV_W), 0.05),
        "rwkv_k_a": gain((L, RWKV_W)),
        "rwkv_r_k": nrm((L, RWKV_HEADS, RWKV_HD), 0.1),
        "rwkv_lnx_w": gain((L, RWKV_W)),
        "rwkv_lnx_b": nrm((L, RWKV_W), 0.01),
        "gate_b": nrm((L, N_BRANCH * D), 0.1),
        "w_branch": nrm((L, BRANCH_IN, D), GLA_V ** -0.5),
        "w_out": nrm((L, D, D), D ** -0.5),
        "ffn2_norm": gain((L, D)),
        "ffn2_wg": nrm((L, D, D_FF), D ** -0.5),
        "ffn2_wu": nrm((L, D, D_FF), D ** -0.5),
        "ffn2_wd": nrm((L, D_FF, D), D_FF ** -0.5),
        "final_norm": gain((D,)),
    }


def _fwd_reference(x, ffn1_norm, ffn1_wg, ffn1_wu, ffn1_wd, mix_norm, w_in, gla_w_a2, gla_b_a,
              gla_gn_w, rwkv_mu, rwkv_w0, rwkv_w_w2, rwkv_a0, rwkv_w_a2, rwkv_w_g2, rwkv_k_k,
              rwkv_k_a, rwkv_r_k, rwkv_lnx_w, rwkv_lnx_b, gate_b, w_branch, w_out,
              ffn2_norm, ffn2_wg, ffn2_wu, ffn2_wd, final_norm):
    for l in range(DEPTH):
        h = rmsnorm(x, ffn1_norm[l])
        x = x + 0.5 * swiglu(h, ffn1_wg[l], ffn1_wu[l], ffn1_wd[l])
        h = rmsnorm(x, mix_norm[l])
        x = x + hybrid_mixer(h, w_in[l], gla_w_a2[l], gla_b_a[l], gla_gn_w[l], rwkv_mu[l],
                             rwkv_w0[l], rwkv_w_w2[l], rwkv_a0[l], rwkv_w_a2[l], rwkv_w_g2[l],
                             rwkv_k_k[l], rwkv_k_a[l], rwkv_r_k[l], rwkv_lnx_w[l], rwkv_lnx_b[l],
                             gate_b[l], w_branch[l], w_out[l])
        h = rmsnorm(x, ffn2_norm[l])
        x = x + 0.5 * swiglu(h, ffn2_wg[l], ffn2_wu[l], ffn2_wd[l])
    return rmsnorm(x, final_norm)


import jax as _jax
import jax.numpy as _jnp

TWIN_FORMAT = 'train_step'
FWD_PARAMS = ['x', 'ffn1_norm', 'ffn1_wg', 'ffn1_wu', 'ffn1_wd', 'mix_norm', 'w_in', 'gla_w_a2', 'gla_b_a', 'gla_gn_w', 'rwkv_mu', 'rwkv_w0', 'rwkv_w_w2', 'rwkv_a0', 'rwkv_w_a2', 'rwkv_w_g2', 'rwkv_k_k', 'rwkv_k_a', 'rwkv_r_k', 'rwkv_lnx_w', 'rwkv_lnx_b', 'gate_b', 'w_branch', 'w_out', 'ffn2_norm', 'ffn2_wg', 'ffn2_wu', 'ffn2_wd', 'final_norm']
TWIN_WEIGHTS = ['ffn1_norm', 'ffn1_wg', 'ffn1_wu', 'ffn1_wd', 'mix_norm', 'w_in', 'gla_w_a2', 'gla_b_a', 'gla_gn_w', 'rwkv_mu', 'rwkv_w0', 'rwkv_w_w2', 'rwkv_a0', 'rwkv_w_a2', 'rwkv_w_g2', 'rwkv_k_k', 'rwkv_k_a', 'rwkv_r_k', 'rwkv_lnx_w', 'rwkv_lnx_b', 'gate_b', 'w_branch', 'w_out', 'ffn2_norm', 'ffn2_wg', 'ffn2_wu', 'ffn2_wd', 'final_norm']
TWIN_DIFF_INPUT = 'x'
TWIN_INPUTS = ['x', 'ffn1_norm', 'ffn1_wg', 'ffn1_wu', 'ffn1_wd', 'mix_norm', 'w_in', 'gla_w_a2', 'gla_b_a', 'gla_gn_w', 'rwkv_mu', 'rwkv_w0', 'rwkv_w_w2', 'rwkv_a0', 'rwkv_w_a2', 'rwkv_w_g2', 'rwkv_k_k', 'rwkv_k_a', 'rwkv_r_k', 'rwkv_lnx_w', 'rwkv_lnx_b', 'gate_b', 'w_branch', 'w_out', 'ffn2_norm', 'ffn2_wg', 'ffn2_wu', 'ffn2_wd', 'final_norm', 'loss_target', 'm_ffn1_norm', 'm_ffn1_wg', 'm_ffn1_wu', 'm_ffn1_wd', 'm_mix_norm', 'm_w_in', 'm_gla_w_a2', 'm_gla_b_a', 'm_gla_gn_w', 'm_rwkv_mu', 'm_rwkv_w0', 'm_rwkv_w_w2', 'm_rwkv_a0', 'm_rwkv_w_a2', 'm_rwkv_w_g2', 'm_rwkv_k_k', 'm_rwkv_k_a', 'm_rwkv_r_k', 'm_rwkv_lnx_w', 'm_rwkv_lnx_b', 'm_gate_b', 'm_w_branch', 'm_w_out', 'm_ffn2_norm', 'm_ffn2_wg', 'm_ffn2_wu', 'm_ffn2_wd', 'm_final_norm', 'v_ffn1_norm', 'v_ffn1_wg', 'v_ffn1_wu', 'v_ffn1_wd', 'v_mix_norm', 'v_w_in', 'v_gla_w_a2', 'v_gla_b_a', 'v_gla_gn_w', 'v_rwkv_mu', 'v_rwkv_w0', 'v_rwkv_w_w2', 'v_rwkv_a0', 'v_rwkv_w_a2', 'v_rwkv_w_g2', 'v_rwkv_k_k', 'v_rwkv_k_a', 'v_rwkv_r_k', 'v_rwkv_lnx_w', 'v_rwkv_lnx_b', 'v_gate_b', 'v_w_branch', 'v_w_out', 'v_ffn2_norm', 'v_ffn2_wg', 'v_ffn2_wu', 'v_ffn2_wd', 'v_final_norm']
TWIN_OUTPUTS = ['loss', 'grad_x', 'grad_ffn1_norm', 'grad_ffn1_wg', 'grad_ffn1_wu', 'grad_ffn1_wd', 'grad_mix_norm', 'grad_w_in', 'grad_gla_w_a2', 'grad_gla_b_a', 'grad_gla_gn_w', 'grad_rwkv_mu', 'grad_rwkv_w0', 'grad_rwkv_w_w2', 'grad_rwkv_a0', 'grad_rwkv_w_a2', 'grad_rwkv_w_g2', 'grad_rwkv_k_k', 'grad_rwkv_k_a', 'grad_rwkv_r_k', 'grad_rwkv_lnx_w', 'grad_rwkv_lnx_b', 'grad_gate_b', 'grad_w_branch', 'grad_w_out', 'grad_ffn2_norm', 'grad_ffn2_wg', 'grad_ffn2_wu', 'grad_ffn2_wd', 'grad_final_norm', 'delta_ffn1_norm', 'delta_ffn1_wg', 'delta_ffn1_wu', 'delta_ffn1_wd', 'delta_mix_norm', 'delta_w_in', 'delta_gla_w_a2', 'delta_gla_b_a', 'delta_gla_gn_w', 'delta_rwkv_mu', 'delta_rwkv_w0', 'delta_rwkv_w_w2', 'delta_rwkv_a0', 'delta_rwkv_w_a2', 'delta_rwkv_w_g2', 'delta_rwkv_k_k', 'delta_rwkv_k_a', 'delta_rwkv_r_k', 'delta_rwkv_lnx_w', 'delta_rwkv_lnx_b', 'delta_gate_b', 'delta_w_branch', 'delta_w_out', 'delta_ffn2_norm', 'delta_ffn2_wg', 'delta_ffn2_wu', 'delta_ffn2_wd', 'delta_final_norm', 'new_m_ffn1_norm', 'new_m_ffn1_wg', 'new_m_ffn1_wu', 'new_m_ffn1_wd', 'new_m_mix_norm', 'new_m_w_in', 'new_m_gla_w_a2', 'new_m_gla_b_a', 'new_m_gla_gn_w', 'new_m_rwkv_mu', 'new_m_rwkv_w0', 'new_m_rwkv_w_w2', 'new_m_rwkv_a0', 'new_m_rwkv_w_a2', 'new_m_rwkv_w_g2', 'new_m_rwkv_k_k', 'new_m_rwkv_k_a', 'new_m_rwkv_r_k', 'new_m_rwkv_lnx_w', 'new_m_rwkv_lnx_b', 'new_m_gate_b', 'new_m_w_branch', 'new_m_w_out', 'new_m_ffn2_norm', 'new_m_ffn2_wg', 'new_m_ffn2_wu', 'new_m_ffn2_wd', 'new_m_final_norm', 'new_v_ffn1_norm', 'new_v_ffn1_wg', 'new_v_ffn1_wu', 'new_v_ffn1_wd', 'new_v_mix_norm', 'new_v_w_in', 'new_v_gla_w_a2', 'new_v_gla_b_a', 'new_v_gla_gn_w', 'new_v_rwkv_mu', 'new_v_rwkv_w0', 'new_v_rwkv_w_w2', 'new_v_rwkv_a0', 'new_v_rwkv_w_a2', 'new_v_rwkv_w_g2', 'new_v_rwkv_k_k', 'new_v_rwkv_k_a', 'new_v_rwkv_r_k', 'new_v_rwkv_lnx_w', 'new_v_rwkv_lnx_b', 'new_v_gate_b', 'new_v_w_branch', 'new_v_w_out', 'new_v_ffn2_norm', 'new_v_ffn2_wg', 'new_v_ffn2_wu', 'new_v_ffn2_wd', 'new_v_final_norm']
TWIN_LEAF_KINDS = {'loss': 'loss', 'grad_x': 'grad_x', 'grad_ffn1_norm': 'grad_w', 'grad_ffn1_wg': 'grad_w', 'grad_ffn1_wu': 'grad_w', 'grad_ffn1_wd': 'grad_w', 'grad_mix_norm': 'grad_w', 'grad_w_in': 'grad_w', 'grad_gla_w_a2': 'grad_w', 'grad_gla_b_a': 'grad_w', 'grad_gla_gn_w': 'grad_w', 'grad_rwkv_mu': 'grad_w', 'grad_rwkv_w0': 'grad_w', 'grad_rwkv_w_w2': 'grad_w', 'grad_rwkv_a0': 'grad_w', 'grad_rwkv_w_a2': 'grad_w', 'grad_rwkv_w_g2': 'grad_w', 'grad_rwkv_k_k': 'grad_w', 'grad_rwkv_k_a': 'grad_w', 'grad_rwkv_r_k': 'grad_w', 'grad_rwkv_lnx_w': 'grad_w', 'grad_rwkv_lnx_b': 'grad_w', 'grad_gate_b': 'grad_w', 'grad_w_branch': 'grad_w', 'grad_w_out': 'grad_w', 'grad_ffn2_norm': 'grad_w', 'grad_ffn2_wg': 'grad_w', 'grad_ffn2_wu': 'grad_w', 'grad_ffn2_wd': 'grad_w', 'grad_final_norm': 'grad_w', 'delta_ffn1_norm': 'delta_w', 'delta_ffn1_wg': 'delta_w', 'delta_ffn1_wu': 'delta_w', 'delta_ffn1_wd': 'delta_w', 'delta_mix_norm': 'delta_w', 'delta_w_in': 'delta_w', 'delta_gla_w_a2': 'delta_w', 'delta_gla_b_a': 'delta_w', 'delta_gla_gn_w': 'delta_w', 'delta_rwkv_mu': 'delta_w', 'delta_rwkv_w0': 'delta_w', 'delta_rwkv_w_w2': 'delta_w', 'delta_rwkv_a0': 'delta_w', 'delta_rwkv_w_a2': 'delta_w', 'delta_rwkv_w_g2': 'delta_w', 'delta_rwkv_k_k': 'delta_w', 'delta_rwkv_k_a': 'delta_w', 'delta_rwkv_r_k': 'delta_w', 'delta_rwkv_lnx_w': 'delta_w', 'delta_rwkv_lnx_b': 'delta_w', 'delta_gate_b': 'delta_w', 'delta_w_branch': 'delta_w', 'delta_w_out': 'delta_w', 'delta_ffn2_norm': 'delta_w', 'delta_ffn2_wg': 'delta_w', 'delta_ffn2_wu': 'delta_w', 'delta_ffn2_wd': 'delta_w', 'delta_final_norm': 'delta_w', 'new_m_ffn1_norm': 'new_m', 'new_m_ffn1_wg': 'new_m', 'new_m_ffn1_wu': 'new_m', 'new_m_ffn1_wd': 'new_m', 'new_m_mix_norm': 'new_m', 'new_m_w_in': 'new_m', 'new_m_gla_w_a2': 'new_m', 'new_m_gla_b_a': 'new_m', 'new_m_gla_gn_w': 'new_m', 'new_m_rwkv_mu': 'new_m', 'new_m_rwkv_w0': 'new_m', 'new_m_rwkv_w_w2': 'new_m', 'new_m_rwkv_a0': 'new_m', 'new_m_rwkv_w_a2': 'new_m', 'new_m_rwkv_w_g2': 'new_m', 'new_m_rwkv_k_k': 'new_m', 'new_m_rwkv_k_a': 'new_m', 'new_m_rwkv_r_k': 'new_m', 'new_m_rwkv_lnx_w': 'new_m', 'new_m_rwkv_lnx_b': 'new_m', 'new_m_gate_b': 'new_m', 'new_m_w_branch': 'new_m', 'new_m_w_out': 'new_m', 'new_m_ffn2_norm': 'new_m', 'new_m_ffn2_wg': 'new_m', 'new_m_ffn2_wu': 'new_m', 'new_m_ffn2_wd': 'new_m', 'new_m_final_norm': 'new_m', 'new_v_ffn1_norm': 'new_v', 'new_v_ffn1_wg': 'new_v', 'new_v_ffn1_wu': 'new_v', 'new_v_ffn1_wd': 'new_v', 'new_v_mix_norm': 'new_v', 'new_v_w_in': 'new_v', 'new_v_gla_w_a2': 'new_v', 'new_v_gla_b_a': 'new_v', 'new_v_gla_gn_w': 'new_v', 'new_v_rwkv_mu': 'new_v', 'new_v_rwkv_w0': 'new_v', 'new_v_rwkv_w_w2': 'new_v', 'new_v_rwkv_a0': 'new_v', 'new_v_rwkv_w_a2': 'new_v', 'new_v_rwkv_w_g2': 'new_v', 'new_v_rwkv_k_k': 'new_v', 'new_v_rwkv_k_a': 'new_v', 'new_v_rwkv_r_k': 'new_v', 'new_v_rwkv_lnx_w': 'new_v', 'new_v_rwkv_lnx_b': 'new_v', 'new_v_gate_b': 'new_v', 'new_v_w_branch': 'new_v', 'new_v_w_out': 'new_v', 'new_v_ffn2_norm': 'new_v', 'new_v_ffn2_wg': 'new_v', 'new_v_ffn2_wu': 'new_v', 'new_v_ffn2_wd': 'new_v', 'new_v_final_norm': 'new_v'}


def _forward(args):
    return _fwd_reference(*[args[k] for k in FWD_PARAMS])


def _output_shape():
    def fwd():
        inp = _fwd_setup_inputs(0)
        return _fwd_reference(*[inp[k] for k in FWD_PARAMS])
    out = _jax.eval_shape(fwd)
    return out.shape, out.dtype

N_MICROBATCH = 1
ADAM_LR = 0.001
ADAM_B1 = 0.9
ADAM_B2 = 0.999
ADAM_EPS = 1e-08
ADAM_WD = 0.01
ADAM_STEP = 10
PER_EXAMPLE_BATCH_AXIS = {'x': 0, 'loss_target': 0}
SHARED_INPUTS = []
_WEIGHT_DTYPES = {'ffn1_norm': _jnp.float32, 'ffn1_wg': _jnp.float32, 'ffn1_wu': _jnp.float32, 'ffn1_wd': _jnp.float32, 'mix_norm': _jnp.float32, 'w_in': _jnp.float32, 'gla_w_a2': _jnp.float32, 'gla_b_a': _jnp.float32, 'gla_gn_w': _jnp.float32, 'rwkv_mu': _jnp.float32, 'rwkv_w0': _jnp.float32, 'rwkv_w_w2': _jnp.float32, 'rwkv_a0': _jnp.float32, 'rwkv_w_a2': _jnp.float32, 'rwkv_w_g2': _jnp.float32, 'rwkv_k_k': _jnp.float32, 'rwkv_k_a': _jnp.float32, 'rwkv_r_k': _jnp.float32, 'rwkv_lnx_w': _jnp.float32, 'rwkv_lnx_b': _jnp.float32, 'gate_b': _jnp.float32, 'w_branch': _jnp.float32, 'w_out': _jnp.float32, 'ffn2_norm': _jnp.float32, 'ffn2_wg': _jnp.float32, 'ffn2_wu': _jnp.float32, 'ffn2_wd': _jnp.float32, 'final_norm': _jnp.float32}
MOMENT_SCALE = {'ffn1_norm': 4.668469e-02, 'ffn1_wg': 1.964677e-02, 'ffn1_wu': 1.902524e-02, 'ffn1_wd': 3.154874e-02, 'mix_norm': 7.211200e-02, 'w_in': 3.133767e-02, 'gla_w_a2': 5.997190e-03, 'gla_b_a': 2.451363e-02, 'gla_gn_w': 7.512690e-02, 'rwkv_mu': 5.546973e-02, 'rwkv_w0': 1.950615e-02, 'rwkv_w_w2': 3.601707e-03, 'rwkv_a0': 1.644874e-02, 'rwkv_w_a2': 1.434213e-02, 'rwkv_w_g2': 3.482749e-02, 'rwkv_k_k': 1.244988e-02, 'rwkv_k_a': 3.631174e-02, 'rwkv_r_k': 7.512357e-02, 'rwkv_lnx_w': 3.497673e-02, 'rwkv_lnx_b': 3.260745e-02, 'gate_b': 9.892728e-03, 'w_branch': 2.527451e-02, 'w_out': 3.577747e-02, 'ffn2_norm': 3.252997e-02, 'ffn2_wg': 1.404831e-02, 'ffn2_wu': 1.361400e-02, 'ffn2_wd': 2.253560e-02, 'final_norm': 1.598047e+01}


def _to_microbatches(a, axis):
    t = _jnp.moveaxis(a, axis, 0)
    t = t.reshape((N_MICROBATCH, t.shape[0] // N_MICROBATCH) + t.shape[1:])
    return _jnp.moveaxis(t, 1, axis + 1)


def setup_inputs(seed: int = 0) -> dict:
    inp = _fwd_setup_inputs(seed)
    key = _jax.random.fold_in(_jax.random.key(seed), 7919)
    shape, _ = _output_shape()
    out = dict(inp)
    out["loss_target"] = _jax.random.normal(_jax.random.fold_in(key, 0), shape, _jnp.float32)
    for i, name in enumerate(TWIN_WEIGHTS):
        w = inp[name].astype(_jnp.float32)
        if MOMENT_SCALE is None:
            s = _jnp.sqrt(_jnp.mean(_jnp.square(w)) + 1e-30)
        else:
            s = MOMENT_SCALE[name]
        km, kv = _jax.random.split(_jax.random.fold_in(key, i + 1))
        out[name] = w
        out["m_" + name] = s * _jax.random.normal(km, w.shape, _jnp.float32)
        out["v_" + name] = (s * s) * _jax.random.uniform(kv, w.shape, _jnp.float32, 0.5, 1.5)
    if N_MICROBATCH > 1:
        for name, axis in PER_EXAMPLE_BATCH_AXIS.items():
            out[name] = _to_microbatches(out[name], axis)
    return {'x': out['x'], 'ffn1_norm': out['ffn1_norm'], 'ffn1_wg': out['ffn1_wg'], 'ffn1_wu': out['ffn1_wu'], 'ffn1_wd': out['ffn1_wd'], 'mix_norm': out['mix_norm'], 'w_in': out['w_in'], 'gla_w_a2': out['gla_w_a2'], 'gla_b_a': out['gla_b_a'], 'gla_gn_w': out['gla_gn_w'], 'rwkv_mu': out['rwkv_mu'], 'rwkv_w0': out['rwkv_w0'], 'rwkv_w_w2': out['rwkv_w_w2'], 'rwkv_a0': out['rwkv_a0'], 'rwkv_w_a2': out['rwkv_w_a2'], 'rwkv_w_g2': out['rwkv_w_g2'], 'rwkv_k_k': out['rwkv_k_k'], 'rwkv_k_a': out['rwkv_k_a'], 'rwkv_r_k': out['rwkv_r_k'], 'rwkv_lnx_w': out['rwkv_lnx_w'], 'rwkv_lnx_b': out['rwkv_lnx_b'], 'gate_b': out['gate_b'], 'w_branch': out['w_branch'], 'w_out': out['w_out'], 'ffn2_norm': out['ffn2_norm'], 'ffn2_wg': out['ffn2_wg'], 'ffn2_wu': out['ffn2_wu'], 'ffn2_wd': out['ffn2_wd'], 'final_norm': out['final_norm'], 'loss_target': out['loss_target'], 'm_ffn1_norm': out['m_ffn1_norm'], 'm_ffn1_wg': out['m_ffn1_wg'], 'm_ffn1_wu': out['m_ffn1_wu'], 'm_ffn1_wd': out['m_ffn1_wd'], 'm_mix_norm': out['m_mix_norm'], 'm_w_in': out['m_w_in'], 'm_gla_w_a2': out['m_gla_w_a2'], 'm_gla_b_a': out['m_gla_b_a'], 'm_gla_gn_w': out['m_gla_gn_w'], 'm_rwkv_mu': out['m_rwkv_mu'], 'm_rwkv_w0': out['m_rwkv_w0'], 'm_rwkv_w_w2': out['m_rwkv_w_w2'], 'm_rwkv_a0': out['m_rwkv_a0'], 'm_rwkv_w_a2': out['m_rwkv_w_a2'], 'm_rwkv_w_g2': out['m_rwkv_w_g2'], 'm_rwkv_k_k': out['m_rwkv_k_k'], 'm_rwkv_k_a': out['m_rwkv_k_a'], 'm_rwkv_r_k': out['m_rwkv_r_k'], 'm_rwkv_lnx_w': out['m_rwkv_lnx_w'], 'm_rwkv_lnx_b': out['m_rwkv_lnx_b'], 'm_gate_b': out['m_gate_b'], 'm_w_branch': out['m_w_branch'], 'm_w_out': out['m_w_out'], 'm_ffn2_norm': out['m_ffn2_norm'], 'm_ffn2_wg': out['m_ffn2_wg'], 'm_ffn2_wu': out['m_ffn2_wu'], 'm_ffn2_wd': out['m_ffn2_wd'], 'm_final_norm': out['m_final_norm'], 'v_ffn1_norm': out['v_ffn1_norm'], 'v_ffn1_wg': out['v_ffn1_wg'], 'v_ffn1_wu': out['v_ffn1_wu'], 'v_ffn1_wd': out['v_ffn1_wd'], 'v_mix_norm': out['v_mix_norm'], 'v_w_in': out['v_w_in'], 'v_gla_w_a2': out['v_gla_w_a2'], 'v_gla_b_a': out['v_gla_b_a'], 'v_gla_gn_w': out['v_gla_gn_w'], 'v_rwkv_mu': out['v_rwkv_mu'], 'v_rwkv_w0': out['v_rwkv_w0'], 'v_rwkv_w_w2': out['v_rwkv_w_w2'], 'v_rwkv_a0': out['v_rwkv_a0'], 'v_rwkv_w_a2': out['v_rwkv_w_a2'], 'v_rwkv_w_g2': out['v_rwkv_w_g2'], 'v_rwkv_k_k': out['v_rwkv_k_k'], 'v_rwkv_k_a': out['v_rwkv_k_a'], 'v_rwkv_r_k': out['v_rwkv_r_k'], 'v_rwkv_lnx_w': out['v_rwkv_lnx_w'], 'v_rwkv_lnx_b': out['v_rwkv_lnx_b'], 'v_gate_b': out['v_gate_b'], 'v_w_branch': out['v_w_branch'], 'v_w_out': out['v_w_out'], 'v_ffn2_norm': out['v_ffn2_norm'], 'v_ffn2_wg': out['v_ffn2_wg'], 'v_ffn2_wu': out['v_ffn2_wu'], 'v_ffn2_wd': out['v_ffn2_wd'], 'v_final_norm': out['v_final_norm']}


def _loss(weights, diff, rest, loss_target):
    with _jax.named_scope("forward"):
        args = {**rest, TWIN_DIFF_INPUT: diff, **{k: w.astype(_WEIGHT_DTYPES[k]) for k, w in weights.items()}}
        y = _forward(args)
    with _jax.named_scope("loss_head"):
        err = _jnp.square(y.astype(_jnp.float32) - loss_target)
        return 0.5 * _jnp.sum(_jnp.mean(err, axis=-1)) if err.ndim else 0.5 * err


def _adamw(w, g, m, v):
    m = ADAM_B1 * m + (1.0 - ADAM_B1) * g
    v = ADAM_B2 * v + (1.0 - ADAM_B2) * _jnp.square(g)
    m_hat = m / (1.0 - ADAM_B1 ** ADAM_STEP)
    v_hat = v / (1.0 - ADAM_B2 ** ADAM_STEP)
    delta = -ADAM_LR * (m_hat / (_jnp.sqrt(v_hat) + ADAM_EPS) + ADAM_WD * w)
    return delta, m, v


def reference(x, ffn1_norm, ffn1_wg, ffn1_wu, ffn1_wd, mix_norm, w_in, gla_w_a2, gla_b_a, gla_gn_w, rwkv_mu, rwkv_w0, rwkv_w_w2, rwkv_a0, rwkv_w_a2, rwkv_w_g2, rwkv_k_k, rwkv_k_a, rwkv_r_k, rwkv_lnx_w, rwkv_lnx_b, gate_b, w_branch, w_out, ffn2_norm, ffn2_wg, ffn2_wu, ffn2_wd, final_norm, loss_target, m_ffn1_norm, m_ffn1_wg, m_ffn1_wu, m_ffn1_wd, m_mix_norm, m_w_in, m_gla_w_a2, m_gla_b_a, m_gla_gn_w, m_rwkv_mu, m_rwkv_w0, m_rwkv_w_w2, m_rwkv_a0, m_rwkv_w_a2, m_rwkv_w_g2, m_rwkv_k_k, m_rwkv_k_a, m_rwkv_r_k, m_rwkv_lnx_w, m_rwkv_lnx_b, m_gate_b, m_w_branch, m_w_out, m_ffn2_norm, m_ffn2_wg, m_ffn2_wu, m_ffn2_wd, m_final_norm, v_ffn1_norm, v_ffn1_wg, v_ffn1_wu, v_ffn1_wd, v_mix_norm, v_w_in, v_gla_w_a2, v_gla_b_a, v_gla_gn_w, v_rwkv_mu, v_rwkv_w0, v_rwkv_w_w2, v_rwkv_a0, v_rwkv_w_a2, v_rwkv_w_g2, v_rwkv_k_k, v_rwkv_k_a, v_rwkv_r_k, v_rwkv_lnx_w, v_rwkv_lnx_b, v_gate_b, v_w_branch, v_w_out, v_ffn2_norm, v_ffn2_wg, v_ffn2_wu, v_ffn2_wd, v_final_norm):
    given = dict(x=x, ffn1_norm=ffn1_norm, ffn1_wg=ffn1_wg, ffn1_wu=ffn1_wu, ffn1_wd=ffn1_wd, mix_norm=mix_norm, w_in=w_in, gla_w_a2=gla_w_a2, gla_b_a=gla_b_a, gla_gn_w=gla_gn_w, rwkv_mu=rwkv_mu, rwkv_w0=rwkv_w0, rwkv_w_w2=rwkv_w_w2, rwkv_a0=rwkv_a0, rwkv_w_a2=rwkv_w_a2, rwkv_w_g2=rwkv_w_g2, rwkv_k_k=rwkv_k_k, rwkv_k_a=rwkv_k_a, rwkv_r_k=rwkv_r_k, rwkv_lnx_w=rwkv_lnx_w, rwkv_lnx_b=rwkv_lnx_b, gate_b=gate_b, w_branch=w_branch, w_out=w_out, ffn2_norm=ffn2_norm, ffn2_wg=ffn2_wg, ffn2_wu=ffn2_wu, ffn2_wd=ffn2_wd, final_norm=final_norm, loss_target=loss_target, m_ffn1_norm=m_ffn1_norm, m_ffn1_wg=m_ffn1_wg, m_ffn1_wu=m_ffn1_wu, m_ffn1_wd=m_ffn1_wd, m_mix_norm=m_mix_norm, m_w_in=m_w_in, m_gla_w_a2=m_gla_w_a2, m_gla_b_a=m_gla_b_a, m_gla_gn_w=m_gla_gn_w, m_rwkv_mu=m_rwkv_mu, m_rwkv_w0=m_rwkv_w0, m_rwkv_w_w2=m_rwkv_w_w2, m_rwkv_a0=m_rwkv_a0, m_rwkv_w_a2=m_rwkv_w_a2, m_rwkv_w_g2=m_rwkv_w_g2, m_rwkv_k_k=m_rwkv_k_k, m_rwkv_k_a=m_rwkv_k_a, m_rwkv_r_k=m_rwkv_r_k, m_rwkv_lnx_w=m_rwkv_lnx_w, m_rwkv_lnx_b=m_rwkv_lnx_b, m_gate_b=m_gate_b, m_w_branch=m_w_branch, m_w_out=m_w_out, m_ffn2_norm=m_ffn2_norm, m_ffn2_wg=m_ffn2_wg, m_ffn2_wu=m_ffn2_wu, m_ffn2_wd=m_ffn2_wd, m_final_norm=m_final_norm, v_ffn1_norm=v_ffn1_norm, v_ffn1_wg=v_ffn1_wg, v_ffn1_wu=v_ffn1_wu, v_ffn1_wd=v_ffn1_wd, v_mix_norm=v_mix_norm, v_w_in=v_w_in, v_gla_w_a2=v_gla_w_a2, v_gla_b_a=v_gla_b_a, v_gla_gn_w=v_gla_gn_w, v_rwkv_mu=v_rwkv_mu, v_rwkv_w0=v_rwkv_w0, v_rwkv_w_w2=v_rwkv_w_w2, v_rwkv_a0=v_rwkv_a0, v_rwkv_w_a2=v_rwkv_w_a2, v_rwkv_w_g2=v_rwkv_w_g2, v_rwkv_k_k=v_rwkv_k_k, v_rwkv_k_a=v_rwkv_k_a, v_rwkv_r_k=v_rwkv_r_k, v_rwkv_lnx_w=v_rwkv_lnx_w, v_rwkv_lnx_b=v_rwkv_lnx_b, v_gate_b=v_gate_b, v_w_branch=v_w_branch, v_w_out=v_w_out, v_ffn2_norm=v_ffn2_norm, v_ffn2_wg=v_ffn2_wg, v_ffn2_wu=v_ffn2_wu, v_ffn2_wd=v_ffn2_wd, v_final_norm=v_final_norm)
    weights = {n: given[n] for n in TWIN_WEIGHTS}
    shared = {n: given[n] for n in SHARED_INPUTS}
    per_example = {n: given[n] for n in ['x']}
    grad_fn = _jax.value_and_grad(_loss, argnums=(0, 1))

    def one_microbatch(ex, loss_target):
        ex = dict(ex)
        diff = ex.pop(TWIN_DIFF_INPUT)
        return grad_fn(weights, diff, {**shared, **ex}, loss_target)

    if N_MICROBATCH == 1:
        loss, (grad_w, grad_x) = one_microbatch(per_example, given["loss_target"])
    else:
        def body(carry, xs):
            loss_sum, grad_sum = carry
            l_k, (gw_k, gx_k) = one_microbatch(xs[0], xs[1])
            with _jax.named_scope("update"):
                return (loss_sum + l_k, _jax.tree.map(_jnp.add, grad_sum, gw_k)), gx_k

        init = (_jnp.zeros((), _jnp.float32), _jax.tree.map(_jnp.zeros_like, weights))
        (loss, grad_w), grad_x = _jax.lax.scan(body, init, (per_example, given["loss_target"]))
    with _jax.named_scope("update"):
        delta_w, new_m, new_v = {}, {}, {}
        for n in TWIN_WEIGHTS:
            delta_w[n], new_m[n], new_v[n] = _adamw(weights[n], grad_w[n], given["m_" + n], given["v_" + n])
    return (loss, grad_x, *[grad_w[n] for n in TWIN_WEIGHTS], *[delta_w[n] for n in TWIN_WEIGHTS],
            *[new_m[n] for n in TWIN_WEIGHTS], *[new_v[n] for n in TWIN_WEIGHTS])
```

```python
import functools

import jax
import jax.numpy as jnp
from jax import lax
from jax.experimental import pallas as pl
from jax.experimental.pallas import tpu as pltpu

F32 = jnp.float32
BF16 = jnp.bfloat16
HI = lax.Precision.HIGHEST

NDEV = 8
D = 2048
DFF = 5632
FSH = DFF // NDEV
CHUNK = 64
GLA_HEADS, GLA_DK, GLA_DV = 4, 128, 256
GLA_QK, GLA_V, GLA_LORA, GLA_TAU = 512, 1024, 16, 16.0
RW_HEADS, RW_HD, RW_W = 16, 64, 1024
DECAY_LORA, AAA_LORA, GATE_LORA = 96, 96, 256
GN_EPS = 64e-5
NORM_EPS = 1e-6
GLA_IN = 2 * GLA_QK + 2 * GLA_V + GLA_LORA
RW_IN = 3 * RW_W + DECAY_LORA + AAA_LORA + GATE_LORA
D_IN = GLA_IN + RW_IN + 2 * D
DIN_SH = D_IN // NDEV
PG_W = 2 * D
PR_W = 3584
PA_W = 3584
PA_USED = 2 * GLA_QK + 2 * GLA_V + 128
DIN_P = PG_W + PR_W + PA_W
LORA_P = 128

ADAM_LR, ADAM_B1, ADAM_B2, ADAM_EPS, ADAM_WD, ADAM_STEP = 0.001, 0.9, 0.999, 1e-08, 0.01, 10

VMEM_LIMIT = 56 * 1024 * 1024
RW_TB = 128
RW_G = 16


def _params(sem=None, vmem=VMEM_LIMIT):
    return pltpu.CompilerParams(dimension_semantics=sem, vmem_limit_bytes=vmem)


def _pair_mask():
    return lax.broadcasted_iota(jnp.int32, (RW_HD, 2 * RW_HD), 1) < RW_HD


def _pair_rowsum(p, mask):
    tot = jnp.sum(p, axis=1, keepdims=True)
    first = jnp.sum(jnp.where(mask, p, 0.0), axis=1, keepdims=True)
    return first, tot - first


def _cols_to_tile(tile, i, col_a, col_b):
    lane = lax.broadcasted_iota(jnp.int32, (RW_HD, 2 * RW_HD), 1) == i
    top = jnp.where(lane, col_a, tile[:RW_HD])
    bot = jnp.where(lane, col_b, tile[RW_HD:])
    return jnp.concatenate([top, bot], axis=0)


def _split_transposed(x_ref, dst_ref):
    xt = x_ref[...].T
    for g in range(RW_TB // RW_G):
        dst_ref[g, :, 0:RW_G] = xt[:, g * RW_G:(g + 1) * RW_G]


def _rw_core_fwd(rw, w, k2, kk, b):
    T = rw.shape[0]
    nb = T // RW_TB
    ng = RW_TB // RW_G

    def body(r_ref, v_ref, w_ref, k_ref, kk_ref, b_ref, y_ref, st_ref, s_scr, vt_scr, yt_scr):
        @pl.when(pl.program_id(1) == 0)
        def _():
            s_scr[...] = jnp.zeros_like(s_scr)

        mask = _pair_mask()
        _split_transposed(v_ref, vt_scr)

        def group(g, S):
            vt = vt_scr[g]
            ytile = jnp.zeros((2 * RW_HD, 2 * RW_HD), F32)
            for i in range(RW_G):
                t = g * RW_G + i
                row = lambda ref: ref[pl.ds(t, 1), :]
                sa_a, sa_b = _pair_rowsum(S * row(kk_ref), mask)
                sa = jnp.where(mask, sa_a, sa_b)
                vcol = jnp.where(mask, vt[:RW_HD, i:i + 1], vt[RW_HD:, i:i + 1])
                S = S * row(w_ref) - sa * row(b_ref) + vcol * row(k_ref)
                st_ref[0, t] = S
                y_a, y_b = _pair_rowsum(S * row(r_ref), mask)
                ytile = _cols_to_tile(ytile, i, y_a, y_b)
            yt_scr[g] = ytile
            return S

        s_scr[...] = lax.fori_loop(0, ng, group, s_scr[...])
        for g in range(ng):
            y_ref[g * RW_G:(g + 1) * RW_G, :] = yt_scr[g].T[0:RW_G, :]

    blk = lambda cb: pl.BlockSpec((RW_TB, 128), lambda p, i, cb=cb: (i, cb + p))
    return pl.pallas_call(
        body, name="rw_core_fwd", grid=(RW_HEADS // 2, nb),
        in_specs=[blk(0), blk(2 * RW_W // 128), blk(0), blk(0), blk(0), blk(0)],
        out_specs=[blk(0), pl.BlockSpec((1, RW_TB, RW_HD, 128), lambda p, i: (p, i, 0, 0))],
        out_shape=(jax.ShapeDtypeStruct((T, RW_W), F32), jax.ShapeDtypeStruct((RW_HEADS // 2, T, RW_HD, 128), F32)),
        scratch_shapes=[pltpu.VMEM((RW_HD, 128), F32), pltpu.VMEM((ng, 128, 128), F32), pltpu.VMEM((ng, 128, 128), F32)],
        compiler_params=_params(("arbitrary", "arbitrary")),
    )(rw, rw, w, k2, kk, b)


def _rw_core_bwd(rw, w, k2, kk, b, states, dy):
    T = rw.shape[0]
    nb = T // RW_TB
    ng = RW_TB // RW_G

    def body(r_ref, v_ref, w_ref, k_ref, kk_ref, b_ref, dy_ref, st_ref, sp_ref,
             dr_ref, dw_ref, dk_ref, dv_ref, dkk_ref, db_ref, ds_scr, vt_scr, dyt_scr, dvt_scr):
        step = pl.program_id(1)

        @pl.when(step == 0)
        def _():
            ds_scr[...] = jnp.zeros_like(ds_scr)

        mask = _pair_mask()
        _split_transposed(v_ref, vt_scr)
        _split_transposed(dy_ref, dyt_scr)
        s_before = jnp.where(step == nb - 1, 0.0, sp_ref[0, 0])

        def group(gg, dS):
            g = ng - 1 - gg
            vt = vt_scr[g]
            dyt = dyt_scr[g]
            dvtile = jnp.zeros((2 * RW_HD, 2 * RW_HD), F32)
            for i in reversed(range(RW_G)):
                t = g * RW_G + i
                row = lambda ref: ref[pl.ds(t, 1), :]
                s_new = st_ref[0, t]
                s_old = st_ref[0, jnp.maximum(t - 1, 0)]
                if i == 0:
                    s_old = jnp.where(g == 0, s_before, s_old)
                vcol = jnp.where(mask, vt[:RW_HD, i:i + 1], vt[RW_HD:, i:i + 1])
                dycol = jnp.where(mask, dyt[:RW_HD, i:i + 1], dyt[RW_HD:, i:i + 1])
                kk_row, b_row, k_row = row(kk_ref), row(b_ref), row(k_ref)
                dS = dS + dycol * row(r_ref)
                dr_ref[pl.ds(t, 1), :] = jnp.sum(s_new * dycol, axis=0, keepdims=True)
                dw_ref[pl.ds(t, 1), :] = jnp.sum(dS * s_old, axis=0, keepdims=True)
                m_a, m_b = _pair_rowsum(dS * b_row, mask)
                dsa = -jnp.where(mask, m_a, m_b)
                sa_a, sa_b = _pair_rowsum(s_old * kk_row, mask)
                sa = jnp.where(mask, sa_a, sa_b)
                db_ref[pl.ds(t, 1), :] = -jnp.sum(dS * sa, axis=0, keepdims=True)
                dv_a, dv_b = _pair_rowsum(dS * k_row, mask)
                dvtile = _cols_to_tile(dvtile, i, dv_a, dv_b)
                dk_ref[pl.ds(t, 1), :] = jnp.sum(dS * vcol, axis=0, keepdims=True)
                dkk_ref[pl.ds(t, 1), :] = jnp.sum(s_old * dsa, axis=0, keepdims=True)
                dS = dS * row(w_ref) + dsa * kk_row
            dvt_scr[g] = dvtile
            return dS

        ds_scr[...] = lax.fori_loop(0, ng, group, ds_scr[...])
        for g in range(ng):
            dv_ref[g * RW_G:(g + 1) * RW_G, :] = dvt_scr[g].T[0:RW_G, :]

    blk = lambda cb: pl.BlockSpec((RW_TB, 128), lambda p, i, cb=cb: (nb - 1 - i, cb + p))
    st_spec = pl.BlockSpec((1, RW_TB, RW_HD, 128), lambda p, i: (p, nb - 1 - i, 0, 0))
    sp_spec = pl.BlockSpec((1, 1, RW_HD, 128), lambda p, i: (p, jnp.maximum((nb - 1 - i) * RW_TB - 1, 0), 0, 0))
    out = jax.ShapeDtypeStruct((T, RW_W), F32)
    return pl.pallas_call(
        body, name="rw_core_bwd", grid=(RW_HEADS // 2, nb),
        in_specs=[blk(0), blk(2 * RW_W // 128), blk(0), blk(0), blk(0), blk(0), blk(0), st_spec, sp_spec],
        out_specs=[blk(0)] * 6,
        out_shape=(out,) * 6,
        scratch_shapes=[pltpu.VMEM((RW_HD, 128), F32)] + [pltpu.VMEM((ng, 128, 128), F32)] * 3,
        compiler_params=_params(("arbitrary", "arbitrary")),
    )(rw, rw, w, k2, kk, b, dy, states, states)


GLA_CB = 8


def _gla_chunk(s_t, q, k, v, la, ltri):
    cum = jnp.dot(ltri, la, precision=HI, preferred_element_type=F32)
    total = jnp.sum(la, axis=0, keepdims=True)
    kdec = k * jnp.exp(total - cum)
    u_t = lax.dot_general(v, kdec, (((0,), (0,)), ((), ())), precision=HI, preferred_element_type=F32)
    s_t = jnp.exp(total) * s_t + u_t
    o = lax.dot_general(q * (GLA_DK ** -0.5), s_t, (((1,), (1,)), ((), ())), precision=HI, preferred_element_type=F32)
    return s_t, o


def _gla_core_fwd(pa, la, ltri):
    T = pa.shape[0]
    cb = min(GLA_CB, T // CHUNK)
    rows = cb * CHUNK
    nsteps = T // rows

    def body(q_ref, k_ref, v_ref, la_ref, ltri_ref, o_ref, st_ref, s_scr):
        @pl.when(pl.program_id(1) == 0)
        def _():
            s_scr[...] = jnp.zeros_like(s_scr)

        def chunk(c, s_t):
            sl = pl.ds(pl.multiple_of(c * CHUNK, CHUNK), CHUNK)
            s_t, o = _gla_chunk(s_t, q_ref[sl, :], k_ref[sl, :], v_ref[sl, :], la_ref[sl, :], ltri_ref[...])
            o_ref[sl, :] = o
            st_ref[0, c] = s_t
            return s_t

        s_scr[...] = lax.fori_loop(0, cb, chunk, s_scr[...])

    qk = lambda off: pl.BlockSpec((rows, GLA_DK), lambda h, i, off=off: (i, off + h))
    vspec = pl.BlockSpec((rows, GLA_DV), lambda h, i: (i, 2 * GLA_QK // GLA_DV + h))
    return pl.pallas_call(
        body, name="gla_core_fwd", grid=(GLA_HEADS, nsteps),
        in_specs=[qk(0), qk(GLA_HEADS), vspec, qk(0), pl.BlockSpec((CHUNK, CHUNK), lambda h, i: (0, 0))],
        out_specs=[pl.BlockSpec((rows, GLA_DV), lambda h, i: (i, h)),
                   pl.BlockSpec((1, cb, GLA_DV, GLA_DK), lambda h, i: (h, i, 0, 0))],
        out_shape=(jax.ShapeDtypeStruct((T, GLA_V), F32),
                   jax.ShapeDtypeStruct((GLA_HEADS, T // CHUNK, GLA_DV, GLA_DK), F32)),
        scratch_shapes=[pltpu.VMEM((GLA_DV, GLA_DK), F32)],
        compiler_params=_params(("arbitrary", "arbitrary")),
    )(pa, pa, pa, la, ltri)


def _gla_core_bwd(pa, la, ltri, states, do):
    T = pa.shape[0]
    cb = min(GLA_CB, T // CHUNK)
    rows = cb * CHUNK
    nsteps = T // rows

    def body(q_ref, k_ref, v_ref, la_ref, ltri_ref, st_ref, sp_ref, do_ref,
             dq_ref, dk_ref, dv_ref, dla_ref, ds_scr):
        step = pl.program_id(1)

        @pl.when(step == 0)
        def _():
            ds_scr[...] = jnp.zeros_like(ds_scr)

        s_before = jnp.where(step == nsteps - 1, 0.0, sp_ref[0, 0])

        def chunk(cc, ds_t):
            c = cb - 1 - cc
            sl = pl.ds(pl.multiple_of(c * CHUNK, CHUNK), CHUNK)
            s_prev = jnp.where(c == 0, s_before, st_ref[0, jnp.maximum(c - 1, 0)])
            _, vjp = jax.vjp(functools.partial(_gla_chunk, ltri=ltri_ref[...]),
                             s_prev, q_ref[sl, :], k_ref[sl, :], v_ref[sl, :], la_ref[sl, :])
            ds_prev, dq, dk, dv, dla = vjp((ds_t, do_ref[sl, :]))
            dq_ref[sl, :] = dq
            dk_ref[sl, :] = dk
            dv_ref[sl, :] = dv
            dla_ref[sl, :] = dla
            return ds_prev

        ds_scr[...] = lax.fori_loop(0, cb, chunk, ds_scr[...])

    r = lambda i: nsteps - 1 - i
    qk = lambda off: pl.BlockSpec((rows, GLA_DK), lambda h, i, off=off: (r(i), off + h))
    vspec = pl.BlockSpec((rows, GLA_DV), lambda h, i: (r(i), 2 * GLA_QK // GLA_DV + h))
    o128 = pl.BlockSpec((rows, GLA_DK), lambda h, i: (r(i), h))
    o256 = pl.BlockSpec((rows, GLA_DV), lambda h, i: (r(i), h))
    return pl.pallas_call(
        body, name="gla_core_bwd", grid=(GLA_HEADS, nsteps),
        in_specs=[qk(0), qk(GLA_HEADS), vspec, qk(0), pl.BlockSpec((CHUNK, CHUNK), lambda h, i: (0, 0)),
                  pl.BlockSpec((1, cb, GLA_DV, GLA_DK), lambda h, i: (h, r(i), 0, 0)),
                  pl.BlockSpec((1, 1, GLA_DV, GLA_DK), lambda h, i: (h, jnp.maximum(r(i) * cb - 1, 0), 0, 0)),
                  o256],
        out_specs=[o128, o128, o256, o128],
        out_shape=(jax.ShapeDtypeStruct((T, GLA_QK), F32), jax.ShapeDtypeStruct((T, GLA_QK), F32),
                   jax.ShapeDtypeStruct((T, GLA_V), F32), jax.ShapeDtypeStruct((T, GLA_QK), F32)),
        scratch_shapes=[pltpu.VMEM((GLA_DV, GLA_DK), F32)],
        compiler_params=_params(("arbitrary", "arbitrary")),
    )(pa, pa, pa, la, ltri, states, states, do)


def _rowwise(fn, name, T, tm, rows, pars, row_outs, acc_outs):
    nr, npar, nro = len(rows), len(pars), len(row_outs)
    tm = min(tm, T)
    nsteps = T // tm

    def body(*refs):
        i = pl.program_id(0)
        ins = [r[...] for r in refs[:nr + npar]]
        outs, accs = fn(i, *ins)
        for r, o in zip(refs[nr + npar:nr + npar + nro], outs):
            r[...] = o.astype(r.dtype)
        for r, a in zip(refs[nr + npar + nro:], accs):
            @pl.when(i == 0)
            def _(r=r, a=a):
                r[...] = a

            @pl.when(i > 0)
            def _(r=r, a=a):
                r[...] += a

    def rspec(width, cb, kind):
        if kind == "cur":
            return pl.BlockSpec((tm, width), lambda i: (i, cb))
        if kind == "prev":
            return pl.BlockSpec((8, width), lambda i: (jnp.maximum(i * (tm // 8) - 1, 0), cb))
        return pl.BlockSpec((8, width), lambda i: (jnp.minimum((i + 1) * (tm // 8), T // 8 - 1), cb))

    in_specs = [rspec(w, cb, kind) for (_, w, cb, kind) in rows]
    in_specs += [pl.BlockSpec(p.shape, lambda i, nd=p.ndim: (0,) * nd) for p in pars]
    out_specs = [pl.BlockSpec((tm, w), lambda i: (i, 0)) for (w, _) in row_outs]
    out_specs += [pl.BlockSpec(s, lambda i, nd=len(s): (0,) * nd) for s in acc_outs]
    out_shape = [jax.ShapeDtypeStruct((T, w), dt) for (w, dt) in row_outs]
    out_shape += [jax.ShapeDtypeStruct(s, F32) for s in acc_outs]
    res = pl.pallas_call(
        body, name=name, grid=(nsteps,), in_specs=in_specs, out_specs=out_specs, out_shape=out_shape,
        compiler_params=_params(("arbitrary",)),
    )(*[r[0] for r in rows], *pars)
    return res


def _cur(a, width=None, cb=0):
    return (a, a.shape[1] if width is None else width, cb, "cur")


def _sigmoid(x):
    return 1.0 / (1.0 + jnp.exp(-x))


def _silu(x):
    return x * _sigmoid(x)


def _softplus(x):
    return jnp.maximum(x, 0.0) + jnp.log(1.0 + jnp.exp(-jnp.abs(x)))


def _rms(x, g):
    return x * lax.rsqrt(jnp.mean(x * x, axis=-1, keepdims=True) + NORM_EPS) * g


def _dot_hi(a, b):
    return jnp.dot(a, b, precision=HI, preferred_element_type=F32)


def _rms_fwd(x, g, name):
    T = x.shape[0]
    fn = lambda i, xb, gb: ((_rms(xb, gb),), ())
    return _rowwise(fn, name, T, 256, [_cur(x)], [g], [(D, BF16)], [])[0]


def _rms_bwd(x, g, dh, dres, name):
    T = x.shape[0]

    def fn(i, xb, dhb, drb, gb):
        _, vjp = jax.vjp(_rms, xb, gb)
        dx, dg = vjp(dhb)
        return (drb + dx,), (dg,)

    return _rowwise(fn, name, T, 256, [_cur(x), _cur(dh), _cur(dres)], [g], [(D, F32)], [(1, D)])


def _loss_bwd(x, target, g):
    T = x.shape[0]

    def loss(xb, gb, tb):
        err = _rms(xb, gb) - tb
        return 0.5 * jnp.sum(jnp.mean(err * err, axis=-1, keepdims=True))

    def fn(i, xb, tb, gb):
        val, (dx, dg) = jax.value_and_grad(loss, argnums=(0, 1))(xb, gb, tb)
        return (dx,), (jnp.full((1, 128), val, F32), dg)

    return _rowwise(fn, "loss_bwd", T, 256, [_cur(x), _cur(target)], [g], [(D, F32)], [(1, 128), (1, D)])


def _gla_la(a_down, w_a2, b_a):
    return -_softplus(-(_dot_hi(a_down, w_a2) + b_a)) * (1.0 / GLA_TAU)


def _gla_prep(pa, w_a2, b_a):
    T = pa.shape[0]
    fn = lambda i, ab, wb, bb: ((_gla_la(ab, wb, bb),), ())
    return _rowwise(fn, "gla_prep", T, 512, [_cur(pa, LORA_P, (2 * GLA_QK + 2 * GLA_V) // LORA_P)], [w_a2, b_a],
                    [(GLA_QK, F32)], [])[0]


def _gla_prep_bwd(pa, w_a2, b_a, dla):
    T = pa.shape[0]

    def fn(i, ab, dlab, wb, bb):
        _, vjp = jax.vjp(_gla_la, ab, wb, bb)
        da, dw, db = vjp(dlab)
        return (da,), (dw, db)

    return _rowwise(fn, "gla_prep_bwd", T, 512, [_cur(pa, LORA_P, (2 * GLA_QK + 2 * GLA_V) // LORA_P), _cur(dla)],
                    [w_a2, b_a], [(LORA_P, BF16)], [(LORA_P, GLA_QK), (1, GLA_QK)])


def _gla_out(o, r, gn, ind, ind_t):
    ms = _dot_hi(_dot_hi(o * o, ind) * (1.0 / GLA_DV), ind_t)
    return o * lax.rsqrt(ms + NORM_EPS) * gn * _silu(r)


def _gla_post(o_raw, pa, gn, ind, ind_t):
    T = pa.shape[0]
    fn = lambda i, ob, rb, gb, a, b: ((_gla_out(ob, rb, gb, a, b),), ())
    return _rowwise(fn, "gla_post", T, 256, [_cur(o_raw), _cur(pa, GLA_V, 2)], [gn, ind, ind_t], [(GLA_V, BF16)], [])[0]


def _gla_post_bwd(o_raw, pa, gn, ind, ind_t, do):
    T = pa.shape[0]

    def fn(i, ob, rb, dob, gb, a, b):
        _, vjp = jax.vjp(lambda o, r, g: _gla_out(o, r, g, a, b), ob, rb, gb)
        d_o, d_r, d_g = vjp(dob)
        return (d_o, d_r), (d_g,)

    return _rowwise(fn, "gla_post_bwd", T, 256, [_cur(o_raw), _cur(pa, GLA_V, 2), _cur(do)], [gn, ind, ind_t],
                    [(GLA_V, F32), (GLA_V, BF16)], [(1, GLA_V)])


def _shift_rows(cur, prev8, i):
    first = jnp.where(i == 0, 0.0, prev8[7:8, :])
    rolled = pltpu.roll(cur, 1, 0)
    return jnp.where(lax.broadcasted_iota(jnp.int32, cur.shape, 0) == 0, first, rolled)


def _rw_gates(rw, w0, w_w2, a0, w_a2, w_g2, k_k, k_a, ind, ind_t):
    rk = rw[:, RW_W:2 * RW_W]
    wd = rw[:, 3 * RW_W:3 * RW_W + LORA_P]
    ad = rw[:, 3 * RW_W + LORA_P:3 * RW_W + 2 * LORA_P]
    gd = rw[:, 3 * RW_W + 2 * LORA_P:]
    w_raw = w0 + _dot_hi(jnp.tanh(wd), w_w2)
    w = jnp.exp(-jnp.exp(-_softplus(-w_raw) - 0.5))
    a = _sigmoid(a0 + _dot_hi(ad, w_a2))
    g = _dot_hi(_sigmoid(gd), w_g2)
    kk = rk * k_k
    kk = kk * _dot_hi(lax.rsqrt(jnp.maximum(_dot_hi(kk * kk, ind), 1e-24)), ind_t)
    k2 = rk * (1.0 + (a - 1.0) * k_a)
    return w, k2, kk, kk * a, g


def _rw_prep(pr, mu, gate_pars):
    T = pr.shape[0]

    def fn(i, cur, prev8, mub, *gp):
        rw = cur + mub * (_shift_rows(cur, prev8, i) - cur)
        return (rw,) + _rw_gates(rw, *gp), ()

    return _rowwise(fn, "rw_prep", T, 256, [_cur(pr), (pr, PR_W, 0, "prev")], [mu, *gate_pars],
                    [(PR_W, F32)] + [(RW_W, F32)] * 5, [])


def _rw_prep_bwd(pr, mu, gate_pars, d_r, d_v, d_w, d_k2, d_kk, d_b, d_g):
    T = pr.shape[0]
    rows = [_cur(pr), (pr, PR_W, 0, "prev")] + [_cur(x) for x in (*d_r, *d_v, d_w, *d_k2, d_kk, d_b, d_g)]
    acc = [(1, PR_W)] + [tuple(p.shape) for p in gate_pars[:-2]]

    def fn(i, cur, prev8, dr1, dr2, dv1, dv2, dw, dk1, dk2, dkk, db, dg, mub, *gp):
        sh = _shift_rows(cur, prev8, i)
        rw = cur + mub * (sh - cur)
        _, vjp = jax.vjp(lambda x, *p: _rw_gates(x, *p, gp[-2], gp[-1]), rw, *gp[:-2])
        grads = vjp((dw, dk1 + dk2, dkk, db, dg))
        zeros = jnp.zeros((cur.shape[0], PR_W - 3 * RW_W), F32)
        drw = grads[0] + jnp.concatenate([dr1 + dr2, jnp.zeros_like(dr1), dv1 + dv2, zeros], axis=1)
        dmu = jnp.sum(drw * (sh - cur), axis=0, keepdims=True)
        return (drw,), (dmu, *grads[1:])

    return _rowwise(fn, "rw_prep_bwd", T, 128, rows, [mu, *gate_pars], [(PR_W, F32)], acc)


def _shift_bwd(drw, mu):
    T = drw.shape[0]
    tm = min(256, T)

    def fn(i, cur, next8, mub):
        last = jnp.where(i == T // tm - 1, 0.0, next8[0:1, :])
        rolled = pltpu.roll(cur, cur.shape[0] - 1, 0)
        nxt = jnp.where(lax.broadcasted_iota(jnp.int32, cur.shape, 0) == cur.shape[0] - 1, last, rolled)
        return ((1.0 - mub) * cur + mub * nxt,), ()

    return _rowwise(fn, "shift_bwd", T, tm, [_cur(drw), (drw, PR_W, 0, "next")], [mu], [(PR_W, BF16)], [])[0]


def _rw_out(y, r, v, k2, g, lnx_w, lnx_b, r_k, ind, ind_t):
    mean = _dot_hi(_dot_hi(y, ind) * (1.0 / RW_HD), ind_t)
    yc = y - mean
    var = _dot_hi(_dot_hi(yc * yc, ind) * (1.0 / RW_HD), ind_t)
    yn = yc * lax.rsqrt(var + GN_EPS) * lnx_w + lnx_b
    bonus = _dot_hi(_dot_hi(r * k2 * r_k, ind), ind_t) * v
    return (yn + bonus) * g


def _rw_post(y, rw, k2, g, pars):
    T = y.shape[0]
    fn = lambda i, *a: ((_rw_out(*a),), ())
    return _rowwise(fn, "rw_post", T, 256, [_cur(y), _cur(rw, RW_W, 0), _cur(rw, RW_W, 2), _cur(k2), _cur(g)], pars,
                    [(RW_W, BF16)], [])[0]


def _rw_post_bwd(y, rw, k2, g, pars, do):
    T = y.shape[0]

    def fn(i, yb, rb, vb, kb, gb, dob, lw, lb, rk, ind, ind_t):
        _, vjp = jax.vjp(lambda *a: _rw_out(*a, ind, ind_t), yb, rb, vb, kb, gb, lw, lb, rk)
        gr = vjp(dob)
        return gr[:5], gr[5:]

    return _rowwise(fn, "rw_post_bwd", T, 256,
                    [_cur(y), _cur(rw, RW_W, 0), _cur(rw, RW_W, 2), _cur(k2), _cur(g), _cur(do)], pars,
                    [(RW_W, F32)] * 5, [(1, RW_W)] * 3)


def _merge_bwd(dm, y_gla, y_rw, pg, gate_b):
    T = dm.shape[0]

    def fn(i, dmb, ya, yr, p1, p2, gb):
        g1 = _sigmoid(p1 + gb[:, :D])
        g2 = _sigmoid(p2 + gb[:, D:])
        dp1 = dmb * ya * g1 * (1.0 - g1)
        dp2 = dmb * yr * g2 * (1.0 - g2)
        dp = jnp.concatenate([dp1, dp2], axis=1)
        return (dmb * g1, dmb * g2, dp), (jnp.sum(dp, axis=0, keepdims=True),)

    return _rowwise(fn, "merge_bwd", T, 256, [_cur(dm), _cur(y_gla), _cur(y_rw), _cur(pg, D, 0), _cur(pg, D, 1)],
                    [gate_b], [(D, BF16), (D, BF16), (PG_W, BF16)], [(1, PG_W)])


_NN = (((1,), (0,)), ((), ()))
_NT = (((1,), (1,)), ((), ()))
_TN = (((0,), (0,)), ((), ()))


def _bdot(a, b, dims):
    return lax.dot_general(a.astype(BF16), b.astype(BF16), dims, preferred_element_type=F32)


def _accumulate(k, nk, acc, part, finish):
    if nk == 1:
        finish(part)
        return

    @pl.when(k == 0)
    def _():
        acc[...] = part

    @pl.when(k > 0)
    def _():
        acc[...] += part

    @pl.when(k == nk - 1)
    def _():
        finish(acc[...])


def _matmul(a, b, mode, M, N, K, tm, tn, tk, name, a_off=(0, 0), b_off=(0, 0), res=None, scale=1.0, out_dtype=F32):
    tm, tn, tk = min(tm, M), min(tn, N), min(tk, K)
    nk = K // tk
    if mode == "nn":
        a_spec = pl.BlockSpec((tm, tk), lambda i, j, k: (i + a_off[0], k + a_off[1]))
        b_spec = pl.BlockSpec((tk, tn), lambda i, j, k: (k + b_off[0], j + b_off[1]))
        dims = _NN
    elif mode == "nt":
        a_spec = pl.BlockSpec((tm, tk), lambda i, j, k: (i + a_off[0], k + a_off[1]))
        b_spec = pl.BlockSpec((tn, tk), lambda i, j, k: (j + b_off[0], k + b_off[1]))
        dims = _NT
    else:
        a_spec = pl.BlockSpec((tk, tm), lambda i, j, k: (k + a_off[0], i + a_off[1]))
        b_spec = pl.BlockSpec((tk, tn), lambda i, j, k: (k + b_off[0], j + b_off[1]))
        dims = _TN
    o_spec = pl.BlockSpec((tm, tn), lambda i, j, k: (i, j))

    def body(a_ref, b_ref, *rest):
        r_ref = rest[0] if res is not None else None
        o_ref = rest[1] if res is not None else rest[0]
        acc = rest[-1] if nk > 1 else None

        def finish(total):
            total = total * scale if scale != 1.0 else total
            if r_ref is not None:
                total = r_ref[...] + total
            o_ref[...] = total.astype(out_dtype)

        _accumulate(pl.program_id(2), nk, acc, _bdot(a_ref[...], b_ref[...], dims), finish)

    return pl.pallas_call(
        body, name=name, grid=(M // tm, N // tn, nk),
        in_specs=[a_spec, b_spec] + ([o_spec] if res is not None else []),
        out_specs=o_spec, out_shape=jax.ShapeDtypeStruct((M, N), out_dtype),
        scratch_shapes=[pltpu.VMEM((tm, tn), F32)] if nk > 1 else [],
        compiler_params=_params(("parallel", "parallel", "arbitrary")),
    )(a, b, *([res] if res is not None else []))


def _ffn_up(h, wg, wu, name):
    T = h.shape[0]
    tm = min(1024, T)

    def body(h_ref, wg_ref, wu_ref, a_ref, u_ref, s_ref):
        hb = h_ref[...]
        a = _bdot(hb, wg_ref[...], _NN)
        u = _bdot(hb, wu_ref[...], _NN)
        a_ref[...] = a
        u_ref[...] = u
        s_ref[...] = (_silu(a) * u).astype(BF16)

    w_spec = pl.BlockSpec((None, D, FSH), lambda i, j: (j, 0, 0))
    o_spec = pl.BlockSpec((None, tm, FSH), lambda i, j: (j, i, 0))
    sh = lambda dt: jax.ShapeDtypeStruct((NDEV, T, FSH), dt)
    return pl.pallas_call(
        body, name=name, grid=(T // tm, NDEV),
        in_specs=[pl.BlockSpec((tm, D), lambda i, j: (i, 0)), w_spec, w_spec],
        out_specs=[o_spec] * 3, out_shape=(sh(F32), sh(F32), sh(BF16)),
        compiler_params=_params(("parallel", "arbitrary")),
    )(h, wg, wu)


def _ffn_down(s, wd, x, name):
    T = x.shape[0]
    tm, tn = min(1024, T), 1024

    def body(s_ref, wd_ref, x_ref, o_ref, acc):
        def finish(total):
            o_ref[...] = x_ref[...] + 0.5 * total

        _accumulate(pl.program_id(2), NDEV, acc, _bdot(s_ref[...], wd_ref[...], _NN), finish)

    xo = pl.BlockSpec((tm, tn), lambda i, n, j: (i, n))
    return pl.pallas_call(
        body, name=name, grid=(T // tm, D // tn, NDEV),
        in_specs=[pl.BlockSpec((None, tm, FSH), lambda i, n, j: (j, i, 0)),
                  pl.BlockSpec((None, FSH, tn), lambda i, n, j: (j, 0, n)), xo],
        out_specs=xo, out_shape=jax.ShapeDtypeStruct((T, D), F32),
        scratch_shapes=[pltpu.VMEM((tm, tn), F32)],
        compiler_params=_params(("parallel", "parallel", "arbitrary")),
    )(s, wd, x)


def _ffn_bwd_hidden(dx, wd, a, u, name):
    T = dx.shape[0]
    tm = min(1024, T)

    def body(dx_ref, wd_ref, a_ref, u_ref, da_ref, du_ref):
        ds = 0.5 * _bdot(dx_ref[...], wd_ref[...], _NT)
        av = a_ref[...]
        sg = _sigmoid(av)
        da_ref[...] = (ds * u_ref[...] * (sg * (1.0 + av * (1.0 - sg)))).astype(BF16)
        du_ref[...] = (ds * (av * sg)).astype(BF16)

    act = pl.BlockSpec((None, tm, FSH), lambda i, j: (j, i, 0))
    sh = jax.ShapeDtypeStruct((NDEV, T, FSH), BF16)
    return pl.pallas_call(
        body, name=name, grid=(T // tm, NDEV),
        in_specs=[pl.BlockSpec((tm, D), lambda i, j: (i, 0)), pl.BlockSpec((None, FSH, D), lambda i, j: (j, 0, 0)),
                  act, act],
        out_specs=[act, act], out_shape=(sh, sh),
        compiler_params=_params(("parallel", "arbitrary")),
    )(dx, wd, a, u)


def _ffn_bwd_input(da, du, wg, wu, name):
    T = da.shape[1]
    tm, tn = min(1024, T), 1024

    def body(da_ref, du_ref, wg_ref, wu_ref, o_ref, acc):
        part = _bdot(da_ref[...], wg_ref[...], _NT) + _bdot(du_ref[...], wu_ref[...], _NT)

        def finish(total):
            o_ref[...] = total

        _accumulate(pl.program_id(2), NDEV, acc, part, finish)

    act = pl.BlockSpec((None, tm, FSH), lambda i, n, j: (j, i, 0))
    wsp = pl.BlockSpec((None, tn, FSH), lambda i, n, j: (j, n, 0))
    return pl.pallas_call(
        body, name=name, grid=(T // tm, D // tn, NDEV),
        in_specs=[act, act, wsp, wsp],
        out_specs=pl.BlockSpec((tm, tn), lambda i, n, j: (i, n)), out_shape=jax.ShapeDtypeStruct((T, D), F32),
        scratch_shapes=[pltpu.VMEM((tm, tn), F32)],
        compiler_params=_params(("parallel", "parallel", "arbitrary")),
    )(da, du, wg, wu)


def _ffn_grad_up(h, da, name):
    T = h.shape[0]
    tm, tk = 1024, min(1024, T)
    nk = T // tk

    def body(h_ref, da_ref, o_ref, acc):
        def finish(total):
            o_ref[...] = total

        _accumulate(pl.program_id(2), nk, acc, _bdot(h_ref[...], da_ref[...], _TN), finish)

    return pl.pallas_call(
        body, name=name, grid=(NDEV, D // tm, nk),
        in_specs=[pl.BlockSpec((tk, tm), lambda j, i, t: (t, i)), pl.BlockSpec((None, tk, FSH), lambda j, i, t: (j, t, 0))],
        out_specs=pl.BlockSpec((None, tm, FSH), lambda j, i, t: (j, i, 0)),
        out_shape=jax.ShapeDtypeStruct((NDEV, D, FSH), F32),
        scratch_shapes=[pltpu.VMEM((tm, FSH), F32)],
        compiler_params=_params(("parallel", "parallel", "arbitrary")),
    )(h, da)


def _ffn_grad_down(s, dx, name):
    T = dx.shape[0]
    tn, tk = 1024, min(1024, T)
    nk = T // tk

    def body(s_ref, dx_ref, o_ref, acc):
        def finish(total):
            o_ref[...] = 0.5 * total

        _accumulate(pl.program_id(2), nk, acc, _bdot(s_ref[...], dx_ref[...], _TN), finish)

    return pl.pallas_call(
        body, name=name, grid=(NDEV, D // tn, nk),
        in_specs=[pl.BlockSpec((None, tk, FSH), lambda j, n, t: (j, t, 0)), pl.BlockSpec((tk, tn), lambda j, n, t: (t, n))],
        out_specs=pl.BlockSpec((None, FSH, tn), lambda j, n, t: (j, 0, n)),
        out_shape=jax.ShapeDtypeStruct((NDEV, DFF // NDEV, D), F32),
        scratch_shapes=[pltpu.VMEM((FSH, tn), F32)],
        compiler_params=_params(("parallel", "parallel", "arbitrary")),
    )(s, dx)


def _branch_merge(o_gla, o_rw, wb, pg, gate_b):
    T = o_gla.shape[0]
    tm, tn = min(1024, T), 512

    def body(og_ref, or_ref, w1_ref, w2_ref, p1_ref, p2_ref, b1_ref, b2_ref, yg_ref, yr_ref, m_ref):
        yg = _bdot(og_ref[...], w1_ref[...], _NN)
        yr = _bdot(or_ref[...], w2_ref[...], _NN)
        yg_ref[...] = yg
        yr_ref[...] = yr
        m_ref[...] = (_sigmoid(p1_ref[...] + b1_ref[...]) * yg + _sigmoid(p2_ref[...] + b2_ref[...]) * yr).astype(BF16)

    nj = D // tn
    act = pl.BlockSpec((tm, GLA_V), lambda i, j: (i, 0))
    out = pl.BlockSpec((tm, tn), lambda i, j: (i, j))
    return pl.pallas_call(
        body, name="branch_merge", grid=(T // tm, nj),
        in_specs=[act, act, pl.BlockSpec((GLA_V, tn), lambda i, j: (0, j)), pl.BlockSpec((RW_W, tn), lambda i, j: (1, j)),
                  out, pl.BlockSpec((tm, tn), lambda i, j: (i, nj + j)),
                  pl.BlockSpec((1, tn), lambda i, j: (0, j)), pl.BlockSpec((1, tn), lambda i, j: (0, nj + j))],
        out_specs=[out, out, out],
        out_shape=(jax.ShapeDtypeStruct((T, D), F32), jax.ShapeDtypeStruct((T, D), F32), jax.ShapeDtypeStruct((T, D), BF16)),
        compiler_params=_params(("parallel", "arbitrary")),
    )(o_gla, o_rw, wb, wb, pg, pg, gate_b, gate_b)


def _head_indicator(width, heads):
    col = lax.broadcasted_iota(jnp.int32, (width, 128), 0) // (width // heads)
    ind = (col == lax.broadcasted_iota(jnp.int32, (width, 128), 1)).astype(F32)
    return ind, ind.T


def _ffn_fwd(x, g, wg, wu, wd, tag):
    h = _rms_fwd(x, g, "rms_" + tag)
    a, u, s = _ffn_up(h, wg, wu, "ffn_up_" + tag)
    return _ffn_down(s, wd, x, "ffn_down_" + tag), (h, a, u, s)


def _ffn_bwd(dy, x, g, wg, wu, wd, saved, tag):
    h, a, u, s = saved
    da, du = _ffn_bwd_hidden(dy, wd, a, u, "ffn_bwd_hidden_" + tag)
    dh = _ffn_bwd_input(da, du, wg, wu, "ffn_bwd_input_" + tag)
    dwg = _ffn_grad_up(h, da, "ffn_grad_gate_" + tag)
    dwu = _ffn_grad_up(h, du, "ffn_grad_up_" + tag)
    dwd = _ffn_grad_down(s, dy, "ffn_grad_down_" + tag)
    dx, dg = _rms_bwd(x, g, dh, dy, "rms_bwd_" + tag)
    return dx, dg, dwg, dwu, dwd


def _local_step(x, target, w):
    T = x.shape[0]
    ind16, ind16_t = _head_indicator(RW_W, RW_HEADS)
    ind4, ind4_t = _head_indicator(GLA_V, GLA_HEADS)
    ltri = jnp.tril(jnp.ones((CHUNK, CHUNK), F32))
    gate_pars = [w["w0"], w["w_w2"], w["a0"], w["w_a2"], w["w_g2"], w["k_k"], w["k_a"], ind16, ind16_t]
    post_pars = [w["lnx_w"], w["lnx_b"], w["r_k"], ind16, ind16_t]

    x1, ffn1 = _ffn_fwd(x, w["g1"], w["wg1"], w["wu1"], w["wd1"], "1")
    h2 = _rms_fwd(x1, w["g2"], "rms_mix")
    proj = lambda n, off, name: _matmul(h2, w["win"], "nn", T, n, D, 1024, 512, D, name, b_off=(0, off // 512))
    pg = proj(PG_W, 0, "proj_gate")
    pr = proj(PR_W, PG_W, "proj_rwkv")
    pa = proj(PA_W, PG_W + PR_W, "proj_gla")
    la = _gla_prep(pa, w["gla_w_a2"], w["gla_b_a"])
    o_raw, gla_states = _gla_core_fwd(pa, la, ltri)
    o_gla = _gla_post(o_raw, pa, w["gn"], ind4, ind4_t)
    rw, dec, k2, kk, b, g = _rw_prep(pr, w["mu"], gate_pars)
    y, rw_states = _rw_core_fwd(rw, dec, k2, kk, b)
    o_rw = _rw_post(y, rw, k2, g, post_pars)
    y_gla, y_rw, merged = _branch_merge(o_gla, o_rw, w["wb"], pg, w["gate_b"])
    x2 = _matmul(merged, w["wo"], "nn", T, D, D, 1024, 1024, D, "out_proj", res=x1)
    x3, ffn2 = _ffn_fwd(x2, w["g3"], w["wg2"], w["wu2"], w["wd2"], "2")
    dx3, loss, d_gf = _loss_bwd(x3, target, w["gf"])

    grads = {"gf": d_gf}
    dx2, grads["g3"], grads["wg2"], grads["wu2"], grads["wd2"] = _ffn_bwd(
        dx3, x2, w["g3"], w["wg2"], w["wu2"], w["wd2"], ffn2, "2")
    dm = _matmul(dx2, w["wo"], "nt", T, D, D, 1024, 1024, D, "out_proj_bwd")
    grads["wo"] = _matmul(merged, dx2, "tn", D, D, T, 1024, 1024, 1024, "out_proj_grad")
    dy_gla, dy_rw, dpg, grads["gate_b"] = _merge_bwd(dm, y_gla, y_rw, pg, w["gate_b"])
    do_gla = _matmul(dy_gla, w["wb"], "nt", T, GLA_V, D, 1024, 1024, D, "branch_gla_bwd")
    do_rw = _matmul(dy_rw, w["wb"], "nt", T, RW_W, D, 1024, 1024, D, "branch_rwkv_bwd", b_off=(1, 0))
    grads["wb"] = jnp.concatenate([
        _matmul(o_gla, dy_gla, "tn", GLA_V, D, T, 1024, 1024, 1024, "branch_gla_grad"),
        _matmul(o_rw, dy_rw, "tn", RW_W, D, T, 1024, 1024, 1024, "branch_rwkv_grad")], axis=0)
    dy, dr2, dv2, dk2b, dg, grads["lnx_w"], grads["lnx_b"], grads["r_k"] = _rw_post_bwd(y, rw, k2, g, post_pars, do_rw)
    dr1, dw, dk2a, dv1, dkk, db = _rw_core_bwd(rw, dec, k2, kk, b, rw_states, dy)
    drw, grads["mu"], grads["w0"], grads["w_w2"], grads["a0"], grads["w_a2"], grads["w_g2"], grads["k_k"], grads["k_a"] = (
        _rw_prep_bwd(pr, w["mu"], gate_pars, (dr1, dr2), (dv1, dv2), dw, (dk2a, dk2b), dkk, db, dg))
    dpr = _shift_bwd(drw, w["mu"])
    do_raw, dr_gla, grads["gn"] = _gla_post_bwd(o_raw, pa, w["gn"], ind4, ind4_t, do_gla)
    dq, dk, dv, dla = _gla_core_bwd(pa, la, ltri, gla_states, do_raw)
    da_down, grads["gla_w_a2"], grads["gla_b_a"] = _gla_prep_bwd(pa, w["gla_w_a2"], w["gla_b_a"], dla)
    dpa = jnp.concatenate([dq.astype(BF16), dk.astype(BF16), dv.astype(BF16), dr_gla, da_down,
                           jnp.zeros((T, PA_W - PA_USED), BF16)], axis=1)
    dp = jnp.concatenate([dpg, dpr, dpa], axis=1)
    dh2 = _matmul(dp, w["win"], "nt", T, D, DIN_P, 1024, 1024, 1024, "proj_bwd")
    grads["win"] = _matmul(h2, dp, "tn", D, DIN_P, T, 1024, 1024, 1024, "proj_grad")
    dx1, grads["g2"] = _rms_bwd(x1, w["g2"], dh2, dx2, "rms_bwd_mix")
    dx, grads["g1"], grads["wg1"], grads["wu1"], grads["wd1"] = _ffn_bwd(
        dx1, x, w["g1"], w["wg1"], w["wu1"], w["wd1"], ffn1, "1")
    return loss, dx, grads


BIG = ("ffn1_wg", "ffn1_wu", "ffn1_wd", "w_in", "w_branch", "w_out", "ffn2_wg", "ffn2_wu", "ffn2_wd")
SMALL_SHARDED = ("gla_w_a2", "rwkv_w_w2", "rwkv_w_a2", "rwkv_w_g2")
REPLICATED = ("ffn1_norm", "mix_norm", "gla_b_a", "gla_gn_w", "rwkv_mu", "rwkv_w0", "rwkv_a0", "rwkv_k_k", "rwkv_k_a",
              "rwkv_r_k", "rwkv_lnx_w", "rwkv_lnx_b", "gate_b", "ffn2_norm", "final_norm")
WEIGHTS = ("ffn1_norm", "ffn1_wg", "ffn1_wu", "ffn1_wd", "mix_norm", "w_in", "gla_w_a2", "gla_b_a", "gla_gn_w",
           "rwkv_mu", "rwkv_w0", "rwkv_w_w2", "rwkv_a0", "rwkv_w_a2", "rwkv_w_g2", "rwkv_k_k", "rwkv_k_a", "rwkv_r_k",
           "rwkv_lnx_w", "rwkv_lnx_b", "gate_b", "w_branch", "w_out", "ffn2_norm", "ffn2_wg", "ffn2_wu", "ffn2_wd",
           "final_norm")


def _unshard_cols(g):
    return jnp.transpose(g, (1, 0, 2)).reshape(g.shape[1], NDEV * g.shape[2])


def _shard_cols(a):
    return jnp.transpose(a.reshape(a.shape[0], NDEV, a.shape[1] // NDEV), (1, 0, 2))


def _pad_rows(a, rows):
    return jnp.pad(a, ((0, rows - a.shape[0]), (0, 0)))


def _align_rw(a):
    c = 3 * RW_W
    z = jnp.zeros((a.shape[0], LORA_P - DECAY_LORA), a.dtype)
    return jnp.concatenate([a[:, :c], a[:, c:c + DECAY_LORA], z, a[:, c + DECAY_LORA:c + 2 * DECAY_LORA], z,
                            a[:, c + 2 * DECAY_LORA:]], axis=1)


def _unalign_rw(a):
    c = 3 * RW_W
    return jnp.concatenate([a[:, :c + DECAY_LORA], a[:, c + LORA_P:c + LORA_P + AAA_LORA], a[:, c + 2 * LORA_P:]], axis=1)


def _align_proj(a):
    gla = jnp.pad(a[:, :GLA_IN], ((0, 0), (0, PA_W - GLA_IN)))
    return jnp.concatenate([a[:, GLA_IN + RW_IN:], _align_rw(a[:, GLA_IN:GLA_IN + RW_IN]), gla], axis=1)


def _unalign_proj(a):
    return jnp.concatenate([a[:, PG_W + PR_W:PG_W + PR_W + GLA_IN], _unalign_rw(a[:, PG_W:PG_W + PR_W]), a[:, :PG_W]], axis=1)


def _layout_weights(gb, gs, rep):
    row = lambda n: rep[n].reshape(1, -1)
    return {
        "wg1": gb["ffn1_wg"], "wu1": gb["ffn1_wu"], "wd1": gb["ffn1_wd"],
        "wg2": gb["ffn2_wg"], "wu2": gb["ffn2_wu"], "wd2": gb["ffn2_wd"],
        "win": _align_proj(_unshard_cols(gb["w_in"])),
        "wb": gb["w_branch"].reshape(GLA_V + RW_W, D), "wo": gb["w_out"].reshape(D, D),
        "g1": row("ffn1_norm"), "g2": row("mix_norm"), "g3": row("ffn2_norm"), "gf": row("final_norm"),
        "gla_w_a2": _pad_rows(_unshard_cols(gs["gla_w_a2"]), LORA_P), "gla_b_a": row("gla_b_a"),
        "gn": jnp.tile(row("gla_gn_w"), (1, GLA_HEADS)),
        "mu": _align_rw(row("rwkv_mu")), "w0": row("rwkv_w0"), "a0": row("rwkv_a0"),
        "w_w2": _pad_rows(_unshard_cols(gs["rwkv_w_w2"]), LORA_P),
        "w_a2": _pad_rows(_unshard_cols(gs["rwkv_w_a2"]), LORA_P),
        "w_g2": _unshard_cols(gs["rwkv_w_g2"]),
        "k_k": row("rwkv_k_k"), "k_a": row("rwkv_k_a"), "r_k": row("rwkv_r_k"),
        "lnx_w": row("rwkv_lnx_w"), "lnx_b": row("rwkv_lnx_b"), "gate_b": row("gate_b"),
    }


def _layout_grads(g):
    big = {
        "ffn1_wg": g["wg1"], "ffn1_wu": g["wu1"], "ffn1_wd": g["wd1"],
        "ffn2_wg": g["wg2"], "ffn2_wu": g["wu2"], "ffn2_wd": g["wd2"],
        "w_in": _shard_cols(_unalign_proj(g["win"])),
        "w_branch": g["wb"].reshape(NDEV, (GLA_V + RW_W) // NDEV, D), "w_out": g["wo"].reshape(NDEV, D // NDEV, D),
    }
    small = {
        "ffn1_norm": g["g1"], "mix_norm": g["g2"], "ffn2_norm": g["g3"], "final_norm": g["gf"],
        "gla_w_a2": g["gla_w_a2"][:GLA_LORA], "gla_b_a": g["gla_b_a"],
        "gla_gn_w": jnp.sum(g["gn"].reshape(GLA_HEADS, GLA_DV), axis=0, keepdims=True),
        "rwkv_mu": _unalign_rw(g["mu"]), "rwkv_w0": g["w0"], "rwkv_a0": g["a0"],
        "rwkv_w_w2": g["w_w2"][:DECAY_LORA], "rwkv_w_a2": g["w_a2"][:AAA_LORA], "rwkv_w_g2": g["w_g2"],
        "rwkv_k_k": g["k_k"], "rwkv_k_a": g["k_a"], "rwkv_r_k": g["r_k"],
        "rwkv_lnx_w": g["lnx_w"], "rwkv_lnx_b": g["lnx_b"], "gate_b": g["gate_b"],
    }
    return big, small


_MESH = pl.DeviceIdType.MESH
_ANY = pl.BlockSpec(memory_space=pl.ANY)


def _position():
    return lax.axis_index("x"), lax.axis_index("y"), lax.axis_index("c")


def _slot(p):
    return 4 * p[0] + 2 * p[1] + p[2]


def _all_gather(arrays, name):
    n = len(arrays)

    def body(*refs):
        ins, outs = refs[:n], refs[n:2 * n]
        send_sems, recv_sems, local_sems = refs[2 * n:]
        x, y, c = _position()
        me, sibling = (x, y, c), (x, y, 1 - c)
        chips = [(1 - x, y), (x, 1 - y), (1 - x, 1 - y)]

        def copy(a, k, block, to, src=None):
            dst = outs[a].at[_slot(block)]
            return pltpu.make_async_remote_copy(
                src_ref=dst if src is None else src, dst_ref=dst, send_sem=send_sems.at[7 * a + k],
                recv_sem=recv_sems.at[7 * a + k], device_id=to, device_id_type=_MESH)

        local, sends = [], []
        for a in range(n):
            mine = pltpu.make_async_copy(ins[a], outs[a].at[_slot(me)], local_sems.at[a])
            mine.start()
            local.append(mine)
            first = [copy(a, 0, me, sibling, src=ins[a])]
            first += [copy(a, 1 + j, me, (*chip, c), src=ins[a]) for j, chip in enumerate(chips)]
            for cp in first:
                cp.start()
            sends += first
        for a in range(n):
            for j, chip in enumerate(chips):
                copy(a, 1 + j, (*chip, c), me).wait_recv()
                passed = copy(a, 4 + j, (*chip, c), sibling)
                passed.start()
                sends.append(passed)
        for a in range(n):
            copy(a, 0, sibling, me).wait_recv()
            for j, chip in enumerate(chips):
                copy(a, 4 + j, (*chip, 1 - c), me).wait_recv()
        for cp in sends:
            cp.wait_send()
        for mine in local:
            mine.wait()

    return pl.pallas_call(
        body, name=name, in_specs=[_ANY] * n, out_specs=[_ANY] * n,
        out_shape=[jax.ShapeDtypeStruct((NDEV,) + a.shape, a.dtype) for a in arrays],
        scratch_shapes=[pltpu.SemaphoreType.DMA((7 * n,)), pltpu.SemaphoreType.DMA((7 * n,)), pltpu.SemaphoreType.DMA((n,))],
    )(*arrays)


def _exchange(arrays, name):
    n = len(arrays)

    def body(*refs):
        ins, outs = refs[:n], refs[n:2 * n]
        send_sems, recv_sems, local_sems = refs[2 * n:]
        x, y, c = _position()
        me = (x, y, c)
        flip = lambda v, f: 1 - v if f else v
        peers = [(flip(x, fx), flip(y, fy), flip(c, fc))
                 for fx, fy, fc in ((0, 0, 1), (1, 0, 0), (0, 1, 0), (1, 1, 0), (1, 0, 1), (0, 1, 1), (1, 1, 1))]

        def copy(a, k, src_slot, dst_slot):
            return pltpu.make_async_remote_copy(
                src_ref=ins[a].at[src_slot], dst_ref=outs[a].at[dst_slot], send_sem=send_sems.at[7 * a + k],
                recv_sem=recv_sems.at[7 * a + k], device_id=peers[k], device_id_type=_MESH)

        local, sends = [], []
        for a in range(n):
            mine = pltpu.make_async_copy(ins[a].at[_slot(me)], outs[a].at[_slot(me)], local_sems.at[a])
            mine.start()
            local.append(mine)
            for k, peer in enumerate(peers):
                cp = copy(a, k, _slot(peer), _slot(me))
                cp.start()
                sends.append(cp)
        for a in range(n):
            for k, peer in enumerate(peers):
                copy(a, k, _slot(peer), _slot(peer)).wait_recv()
        for cp in sends:
            cp.wait_send()
        for mine in local:
            mine.wait()

    return pl.pallas_call(
        body, name=name, in_specs=[_ANY] * n, out_specs=[_ANY] * n,
        out_shape=[jax.ShapeDtypeStruct(a.shape, a.dtype) for a in arrays],
        scratch_shapes=[pltpu.SemaphoreType.DMA((7 * n,)), pltpu.SemaphoreType.DMA((7 * n,)), pltpu.SemaphoreType.DMA((n,))],
    )(*arrays)


def _adamw_math(w, g, m, v):
    m = ADAM_B1 * m + (1.0 - ADAM_B1) * g
    v = ADAM_B2 * v + (1.0 - ADAM_B2) * (g * g)
    m_hat = m / (1.0 - ADAM_B1 ** ADAM_STEP)
    v_hat = v / (1.0 - ADAM_B2 ** ADAM_STEP)
    delta = -ADAM_LR * (m_hat / (jnp.sqrt(v_hat) + ADAM_EPS) + ADAM_WD * w)
    return delta, m, v


def _sum_slots(ref):
    total = ref[0].astype(F32)
    for s in range(1, NDEV):
        total = total + ref[s].astype(F32)
    return total


def _adamw(parts, w, m, v, tr, name, stack_index=None, row_block_offset=0):
    R, C = w.shape

    def body(p_ref, w_ref, m_ref, v_ref, g_ref, d_ref, nm_ref, nv_ref):
        g = _sum_slots(p_ref)
        g_ref[...] = g
        d_ref[...], nm_ref[...], nv_ref[...] = _adamw_math(w_ref[...], g, m_ref[...], v_ref[...])

    if stack_index is None:
        p_spec = pl.BlockSpec((NDEV, tr, C), lambda r: (0, row_block_offset + r, 0))
    else:
        p_spec = pl.BlockSpec((NDEV, None, tr, C), lambda r: (0, stack_index, r, 0))
    blk = pl.BlockSpec((tr, C), lambda r: (r, 0))
    out = jax.ShapeDtypeStruct((R, C), F32)
    return pl.pallas_call(
        body, name=name, grid=(R // tr,), in_specs=[p_spec, blk, blk, blk], out_specs=[blk] * 4, out_shape=(out,) * 4,
        compiler_params=_params(("parallel",)),
    )(parts, w, m, v)


def _sum_gathered(parts):
    _, R, C = parts.shape

    def body(p_ref, o_ref):
        o_ref[...] = _sum_slots(p_ref)

    return pl.pallas_call(body, name="small_grad_sum", out_shape=jax.ShapeDtypeStruct((R, C), F32),
                          compiler_params=_params())(parts)


def _adamw_small(w, g, m, v):
    def body(w_ref, g_ref, m_ref, v_ref, d_ref, nm_ref, nv_ref):
        d_ref[...], nm_ref[...], nv_ref[...] = _adamw_math(w_ref[...], g_ref[...], m_ref[...], v_ref[...])

    out = jax.ShapeDtypeStruct(w.shape, F32)
    return pl.pallas_call(body, name="adamw_small", out_shape=(out,) * 3, compiler_params=_params())(w, g, m, v)


def _pack(pieces, rows):
    flat = jnp.concatenate([p.reshape(-1) for p in pieces])
    return jnp.pad(flat, (0, rows * 128 - flat.shape[0])).reshape(rows, 128)


def _unpack(packed, shapes):
    flat = packed.reshape(-1)
    out, off = [], 0
    for s in shapes:
        size = 1
        for d in s:
            size *= d
        out.append(flat[off:off + size].reshape(s))
        off += size
    return out


def _rows_for(shapes, extra=0):
    total = extra
    for s in shapes:
        size = 1
        for d in s:
            size *= d
        total += size
    return -(-total // 1024) * 8


def kernel(x, ffn1_norm, ffn1_wg, ffn1_wu, ffn1_wd, mix_norm, w_in, gla_w_a2, gla_b_a, gla_gn_w, rwkv_mu, rwkv_w0, rwkv_w_w2, rwkv_a0, rwkv_w_a2, rwkv_w_g2, rwkv_k_k, rwkv_k_a, rwkv_r_k, rwkv_lnx_w, rwkv_lnx_b, gate_b, w_branch, w_out, ffn2_norm, ffn2_wg, ffn2_wu, ffn2_wd, final_norm, loss_target, m_ffn1_norm, m_ffn1_wg, m_ffn1_wu, m_ffn1_wd, m_mix_norm, m_w_in, m_gla_w_a2, m_gla_b_a, m_gla_gn_w, m_rwkv_mu, m_rwkv_w0, m_rwkv_w_w2, m_rwkv_a0, m_rwkv_w_a2, m_rwkv_w_g2, m_rwkv_k_k, m_rwkv_k_a, m_rwkv_r_k, m_rwkv_lnx_w, m_rwkv_lnx_b, m_gate_b, m_w_branch, m_w_out, m_ffn2_norm, m_ffn2_wg, m_ffn2_wu, m_ffn2_wd, m_final_norm, v_ffn1_norm, v_ffn1_wg, v_ffn1_wu, v_ffn1_wd, v_mix_norm, v_w_in, v_gla_w_a2, v_gla_b_a, v_gla_gn_w, v_rwkv_mu, v_rwkv_w0, v_rwkv_w_w2, v_rwkv_a0, v_rwkv_w_a2, v_rwkv_w_g2, v_rwkv_k_k, v_rwkv_k_a, v_rwkv_r_k, v_rwkv_lnx_w, v_rwkv_lnx_b, v_gate_b, v_w_branch, v_w_out, v_ffn2_norm, v_ffn2_wg, v_ffn2_wu, v_ffn2_wd, v_final_norm):
    wts = dict(zip(WEIGHTS, (ffn1_norm, ffn1_wg, ffn1_wu, ffn1_wd, mix_norm, w_in, gla_w_a2, gla_b_a, gla_gn_w, rwkv_mu, rwkv_w0, rwkv_w_w2, rwkv_a0, rwkv_w_a2, rwkv_w_g2, rwkv_k_k, rwkv_k_a, rwkv_r_k, rwkv_lnx_w, rwkv_lnx_b, gate_b, w_branch, w_out, ffn2_norm, ffn2_wg, ffn2_wu, ffn2_wd, final_norm)))
    mom = dict(zip(WEIGHTS, (m_ffn1_norm, m_ffn1_wg, m_ffn1_wu, m_ffn1_wd, m_mix_norm, m_w_in, m_gla_w_a2, m_gla_b_a, m_gla_gn_w, m_rwkv_mu, m_rwkv_w0, m_rwkv_w_w2, m_rwkv_a0, m_rwkv_w_a2, m_rwkv_w_g2, m_rwkv_k_k, m_rwkv_k_a, m_rwkv_r_k, m_rwkv_lnx_w, m_rwkv_lnx_b, m_gate_b, m_w_branch, m_w_out, m_ffn2_norm, m_ffn2_wg, m_ffn2_wu, m_ffn2_wd, m_final_norm)))
    var = dict(zip(WEIGHTS, (v_ffn1_norm, v_ffn1_wg, v_ffn1_wu, v_ffn1_wd, v_mix_norm, v_w_in, v_gla_w_a2, v_gla_b_a, v_gla_gn_w, v_rwkv_mu, v_rwkv_w0, v_rwkv_w_w2, v_rwkv_a0, v_rwkv_w_a2, v_rwkv_w_g2, v_rwkv_k_k, v_rwkv_k_a, v_rwkv_r_k, v_rwkv_lnx_w, v_rwkv_lnx_b, v_gate_b, v_w_branch, v_w_out, v_ffn2_norm, v_ffn2_wg, v_ffn2_wu, v_ffn2_wd, v_final_norm)))
    two = lambda a: a.reshape(a.shape[-2:])

    up = jnp.stack([two(wts[n]) for n in ("ffn1_wg", "ffn1_wu", "ffn2_wg", "ffn2_wu")]).astype(BF16)
    down_names = ("ffn1_wd", "ffn2_wd", "w_branch", "w_out")
    down = jnp.concatenate([two(wts[n]) for n in down_names], axis=0).astype(BF16)
    proj = two(w_in).astype(BF16)
    lora = jnp.concatenate([jnp.pad(two(gla_w_a2), ((0, 0), (0, 128 - GLA_QK // NDEV)))] +
                           [two(wts[n]) for n in SMALL_SHARDED[1:]], axis=0)
    g_up, g_down, g_proj, g_lora = _all_gather([up, down, proj, lora], "gather_weights")
    gb = {"ffn1_wg": g_up[:, 0], "ffn1_wu": g_up[:, 1], "ffn2_wg": g_up[:, 2], "ffn2_wu": g_up[:, 3], "w_in": g_proj}
    row = 0
    for n in down_names:
        gb[n] = g_down[:, row:row + wts[n].shape[1]]
        row += wts[n].shape[1]
    gs = {"gla_w_a2": g_lora[:, :GLA_LORA, :GLA_QK // NDEV]}
    row = GLA_LORA
    for n in SMALL_SHARDED[1:]:
        gs[n] = g_lora[:, row:row + wts[n].shape[1]]
        row += wts[n].shape[1]
    w = _layout_weights(gb, gs, {n: wts[n] for n in REPLICATED})

    loss_part, grad_x, grads = _local_step(x[0], loss_target[0], w)
    big, small = _layout_grads(grads)

    p_up = jnp.stack([big[n] for n in ("ffn1_wg", "ffn1_wu", "ffn2_wg", "ffn2_wu")], axis=1).astype(BF16)
    p_down = jnp.concatenate([big[n] for n in down_names], axis=1).astype(BF16)
    p_proj = big["w_in"].astype(BF16)
    r_up, r_down, r_proj = _exchange([p_up, p_down, p_proj], "exchange_grads")
    result = {}
    for i, n in enumerate(("ffn1_wg", "ffn1_wu", "ffn2_wg", "ffn2_wu")):
        result[n] = _adamw(r_up, two(wts[n]), two(mom[n]), two(var[n]), 256, "adamw_" + n, stack_index=i)
    row = 0
    for n in down_names:
        result[n] = _adamw(r_down, two(wts[n]), two(mom[n]), two(var[n]), 64, "adamw_" + n, row_block_offset=row // 64)
        row += wts[n].shape[1]
    result["w_in"] = _adamw(r_proj, two(w_in), two(mom["w_in"]), two(var["w_in"]), 256, "adamw_w_in")

    small_names = [n for n in WEIGHTS if n not in BIG]
    full_shapes = [small[n].shape for n in small_names]
    rows_full = _rows_for(full_shapes, extra=128)
    packed = _pack([small[n] for n in small_names] + [loss_part], rows_full)
    (gathered,) = _all_gather([packed], "gather_small_grads")
    total = _sum_gathered(gathered)
    *full_grads, loss_row = _unpack(total, full_shapes + [(1, 128)])
    me = _slot(_position())
    own = {}
    for n, g in zip(small_names, full_grads):
        if n in SMALL_SHARDED:
            cols = wts[n].shape[-1]
            g = lax.dynamic_slice_in_dim(g, me * cols, cols, axis=1)
        own[n] = g.reshape(wts[n].shape)
    own_shapes = [wts[n].shape for n in small_names]
    rows_own = _rows_for(own_shapes)
    pk = lambda d: _pack([d[n] for n in small_names], rows_own)
    d_s, m_s, v_s = _adamw_small(pk(wts), pk(own), pk(mom), pk(var))
    for n, d, m, v in zip(small_names, _unpack(d_s, own_shapes), _unpack(m_s, own_shapes), _unpack(v_s, own_shapes)):
        result[n] = (own[n], d, m, v)

    shaped = lambda n, k: result[n][k].reshape(wts[n].shape)
    return (loss_row[0, 0], grad_x[None],
            *[shaped(n, 0) for n in WEIGHTS], *[shaped(n, 1) for n in WEIGHTS],
            *[shaped(n, 2) for n in WEIGHTS], *[shaped(n, 3) for n in WEIGHTS])
```

```python
import functools

import jax
import jax.numpy as jnp
from jax import lax
from jax.experimental import pallas as pl
from jax.experimental.pallas import tpu as pltpu

F32 = jnp.float32
BF16 = jnp.bfloat16
HI = lax.Precision.HIGHEST

NDEV = 8
D = 2048
DFF = 5632
FSH = DFF // NDEV
CHUNK = 64
GLA_HEADS, GLA_DK, GLA_DV = 4, 128, 256
GLA_QK, GLA_V, GLA_LORA, GLA_TAU = 512, 1024, 16, 16.0
RW_HEADS, RW_HD, RW_W = 16, 64, 1024
DECAY_LORA, AAA_LORA, GATE_LORA = 96, 96, 256
GN_EPS = 64e-5
NORM_EPS = 1e-6
GLA_IN = 2 * GLA_QK + 2 * GLA_V + GLA_LORA
RW_IN = 3 * RW_W + DECAY_LORA + AAA_LORA + GATE_LORA
D_IN = GLA_IN + RW_IN + 2 * D
DIN_SH = D_IN // NDEV
PG_W = 2 * D
PR_W = 3584
PA_W = 3584
PA_USED = 2 * GLA_QK + 2 * GLA_V + 128
DIN_P = PG_W + PR_W + PA_W
LORA_P = 128

ADAM_LR, ADAM_B1, ADAM_B2, ADAM_EPS, ADAM_WD, ADAM_STEP = 0.001, 0.9, 0.999, 1e-08, 0.01, 10

VMEM_LIMIT = 56 * 1024 * 1024
RW_TB = 128
RW_G = 16
RW_NP = 4


def _params(sem=None, vmem=VMEM_LIMIT):
    return pltpu.CompilerParams(dimension_semantics=sem, vmem_limit_bytes=vmem)


def _pair_mask():
    return lax.broadcasted_iota(jnp.int32, (RW_HD, 2 * RW_HD), 1) < RW_HD


def _pair_rowsum(p, mask):
    tot = jnp.sum(p, axis=1, keepdims=True)
    first = jnp.sum(jnp.where(mask, p, 0.0), axis=1, keepdims=True)
    return first, tot - first


def _split_transposed(x_ref, q, dst_ref, base):
    xt = x_ref[:, 128 * q:128 * (q + 1)].T
    for g in range(RW_TB // RW_G):
        dst_ref[base + g, :, 0:RW_G] = xt[:, g * RW_G:(g + 1) * RW_G]


def _pair_column(tile_ref, idx, i, mask):
    return jnp.where(mask, tile_ref[idx, 0:RW_HD, i:i + 1], tile_ref[idx, RW_HD:, i:i + 1])


def _rw_core_fwd(rw, w, k2, kk, b):
    T = rw.shape[0]
    nb = T // RW_TB
    ng = RW_TB // RW_G
    NP = RW_NP

    def body(r_ref, v_ref, w_ref, k_ref, kk_ref, b_ref, y_ref, st_ref, s_scr, vt_scr, yt_scr, rows_scr):
        @pl.when(pl.program_id(1) == 0)
        def _():
            s_scr[...] = jnp.zeros_like(s_scr)
            yt_scr[...] = jnp.zeros_like(yt_scr)

        mask = _pair_mask()
        for q in range(NP):
            _split_transposed(v_ref, q, vt_scr, q * ng)
        R_, W_, K_, KK_, B_ = range(5)
        for a, ref in enumerate((r_ref, w_ref, k_ref, kk_ref, b_ref)):
            for q in range(NP):
                rows_scr[a * NP + q] = ref[:, 128 * q:128 * (q + 1)]

        def group(g, states):
            states = list(states)
            for i in range(RW_G):
                t = g * RW_G + i
                row = lambda a, q: rows_scr[a * NP + q, pl.ds(t, 1), :]
                sums = [_pair_rowsum(states[q] * row(KK_, q), mask) for q in range(NP)]
                for q in range(NP):
                    sa = jnp.where(mask, *sums[q])
                    states[q] = (states[q] * row(W_, q) - sa * row(B_, q)
                                 + _pair_column(vt_scr, q * ng + g, i, mask) * row(K_, q))
                    st_ref[q, t] = states[q]
                outs = [_pair_rowsum(states[q] * row(R_, q), mask) for q in range(NP)]
                for q in range(NP):
                    yt_scr[q * ng + g, 0:RW_HD, i:i + 1] = outs[q][0]
                    yt_scr[q * ng + g, RW_HD:, i:i + 1] = outs[q][1]
            return tuple(states)

        states = lax.fori_loop(0, ng, group, tuple(s_scr[q] for q in range(NP)))
        for q in range(NP):
            s_scr[q] = states[q]
            for g in range(ng):
                y_ref[g * RW_G:(g + 1) * RW_G, 128 * q:128 * (q + 1)] = yt_scr[q * ng + g].T[0:RW_G, :]

    blk = lambda cb: pl.BlockSpec((RW_TB, 128 * NP), lambda p, i, cb=cb: (i, cb + p))
    tiles = pltpu.VMEM((NP * ng, 128, 128), F32)
    return pl.pallas_call(
        body, name="rw_core_fwd", grid=(RW_HEADS // 2 // NP, nb),
        in_specs=[blk(0), blk(2 * RW_W // (128 * NP)), blk(0), blk(0), blk(0), blk(0)],
        out_specs=[blk(0), pl.BlockSpec((NP, RW_TB, RW_HD, 128), lambda p, i: (p, i, 0, 0))],
        out_shape=(jax.ShapeDtypeStruct((T, RW_W), F32), jax.ShapeDtypeStruct((RW_HEADS // 2, T, RW_HD, 128), F32)),
        scratch_shapes=[pltpu.VMEM((NP, RW_HD, 128), F32), tiles, tiles, pltpu.VMEM((5 * NP, RW_TB, 128), F32)],
        compiler_params=_params(("arbitrary", "arbitrary")),
    )(rw, rw, w, k2, kk, b)


def _rw_core_bwd(rw, w, k2, kk, b, states, dy):
    T = rw.shape[0]
    nb = T // RW_TB
    ng = RW_TB // RW_G
    NP = RW_NP

    def body(r_ref, v_ref, w_ref, k_ref, kk_ref, b_ref, dy_ref, st_ref, sp_ref,
             dr_ref, dw_ref, dk_ref, dv_ref, dkk_ref, db_ref, ds_scr, vt_scr, dyt_scr, dvt_scr, rows_scr, out_scr):
        step = pl.program_id(1)

        @pl.when(step == 0)
        def _():
            ds_scr[...] = jnp.zeros_like(ds_scr)
            dvt_scr[...] = jnp.zeros_like(dvt_scr)

        mask = _pair_mask()
        for q in range(NP):
            _split_transposed(v_ref, q, vt_scr, q * ng)
            _split_transposed(dy_ref, q, dyt_scr, q * ng)
        R_, W_, K_, KK_, B_ = range(5)
        for a, ref in enumerate((r_ref, w_ref, k_ref, kk_ref, b_ref)):
            for q in range(NP):
                rows_scr[a * NP + q] = ref[:, 128 * q:128 * (q + 1)]

        def group(gg, grads):
            g = ng - 1 - gg
            grads = list(grads)
            pairs = range(NP)
            for i in reversed(range(RW_G)):
                t = g * RW_G + i
                row = lambda a, q: rows_scr[a * NP + q, pl.ds(t, 1), :]

                def put(a, q, value):
                    out_scr[a * NP + q, pl.ds(t, 1), :] = value

                s_old = [st_ref[q, jnp.maximum(t - 1, 0)] for q in pairs]
                if i == 0:
                    s_old = [jnp.where(g == 0, jnp.where(step == nb - 1, 0.0, sp_ref[q, 0]), s_old[q]) for q in pairs]
                dycol = [_pair_column(dyt_scr, q * ng + g, i, mask) for q in pairs]
                dS = [grads[q] + dycol[q] * row(R_, q) for q in pairs]
                m = [_pair_rowsum(dS[q] * row(B_, q), mask) for q in pairs]
                sa = [_pair_rowsum(s_old[q] * row(KK_, q), mask) for q in pairs]
                dv = [_pair_rowsum(dS[q] * row(K_, q), mask) for q in pairs]
                for q in pairs:
                    put(R_, q, jnp.sum(st_ref[q, t] * dycol[q], axis=0, keepdims=True))
                    put(W_, q, jnp.sum(dS[q] * s_old[q], axis=0, keepdims=True))
                    put(K_, q, jnp.sum(dS[q] * _pair_column(vt_scr, q * ng + g, i, mask), axis=0, keepdims=True))
                for q in pairs:
                    dsa = -jnp.where(mask, *m[q])
                    grads[q] = dS[q] * row(W_, q) + dsa * row(KK_, q)
                    put(KK_, q, jnp.sum(s_old[q] * dsa, axis=0, keepdims=True))
                    put(B_, q, -jnp.sum(dS[q] * jnp.where(mask, *sa[q]), axis=0, keepdims=True))
                    dvt_scr[q * ng + g, 0:RW_HD, i:i + 1] = dv[q][0]
                    dvt_scr[q * ng + g, RW_HD:, i:i + 1] = dv[q][1]
            return tuple(grads)

        grads = lax.fori_loop(0, ng, group, tuple(ds_scr[q] for q in range(NP)))
        for q in range(NP):
            ds_scr[q] = grads[q]
            for a, ref in enumerate((dr_ref, dw_ref, dk_ref, dkk_ref, db_ref)):
                ref[:, 128 * q:128 * (q + 1)] = out_scr[a * NP + q]
            for g in range(ng):
                dv_ref[g * RW_G:(g + 1) * RW_G, 128 * q:128 * (q + 1)] = dvt_scr[q * ng + g].T[0:RW_G, :]

    blk = lambda cb: pl.BlockSpec((RW_TB, 128 * NP), lambda p, i, cb=cb: (nb - 1 - i, cb + p))
    st_spec = pl.BlockSpec((NP, RW_TB, RW_HD, 128), lambda p, i: (p, nb - 1 - i, 0, 0))
    sp_spec = pl.BlockSpec((NP, 1, RW_HD, 128), lambda p, i: (p, jnp.maximum((nb - 1 - i) * RW_TB - 1, 0), 0, 0))
    out = jax.ShapeDtypeStruct((T, RW_W), F32)
    tiles = pltpu.VMEM((NP * ng, 128, 128), F32)
    return pl.pallas_call(
        body, name="rw_core_bwd", grid=(RW_HEADS // 2 // NP, nb),
        in_specs=[blk(0), blk(2 * RW_W // (128 * NP)), blk(0), blk(0), blk(0), blk(0), blk(0), st_spec, sp_spec],
        out_specs=[blk(0)] * 6,
        out_shape=(out,) * 6,
        scratch_shapes=[pltpu.VMEM((NP, RW_HD, 128), F32), tiles, tiles, tiles,
                        pltpu.VMEM((5 * NP, RW_TB, 128), F32), pltpu.VMEM((5 * NP, RW_TB, 128), F32)],
        compiler_params=_params(("arbitrary", "arbitrary")),
    )(rw, rw, w, k2, kk, b, dy, states, states)


GLA_CB = 8


def _gla_chunk(s_t, q, k, v, la, ltri):
    cum = jnp.dot(ltri, la, precision=HI, preferred_element_type=F32)
    total = jnp.sum(la, axis=0, keepdims=True)
    kdec = k * jnp.exp(total - cum)
    u_t = lax.dot_general(v, kdec, (((0,), (0,)), ((), ())), precision=HI, preferred_element_type=F32)
    s_t = jnp.exp(total) * s_t + u_t
    o = lax.dot_general(q * (GLA_DK ** -0.5), s_t, (((1,), (1,)), ((), ())), precision=HI, preferred_element_type=F32)
    return s_t, o


def _gla_core_fwd(pa, la, ltri):
    T = pa.shape[0]
    cb = min(GLA_CB, T // CHUNK)
    rows = cb * CHUNK
    nsteps = T // rows

    def body(q_ref, k_ref, v_ref, la_ref, ltri_ref, o_ref, st_ref, s_scr):
        @pl.when(pl.program_id(1) == 0)
        def _():
            s_scr[...] = jnp.zeros_like(s_scr)

        def chunk(c, s_t):
            sl = pl.ds(pl.multiple_of(c * CHUNK, CHUNK), CHUNK)
            s_t, o = _gla_chunk(s_t, q_ref[sl, :], k_ref[sl, :], v_ref[sl, :], la_ref[sl, :], ltri_ref[...])
            o_ref[sl, :] = o
            st_ref[0, c] = s_t
            return s_t

        s_scr[...] = lax.fori_loop(0, cb, chunk, s_scr[...])

    qk = lambda off: pl.BlockSpec((rows, GLA_DK), lambda h, i, off=off: (i, off + h))
    vspec = pl.BlockSpec((rows, GLA_DV), lambda h, i: (i, 2 * GLA_QK // GLA_DV + h))
    return pl.pallas_call(
        body, name="gla_core_fwd", grid=(GLA_HEADS, nsteps),
        in_specs=[qk(0), qk(GLA_HEADS), vspec, qk(0), pl.BlockSpec((CHUNK, CHUNK), lambda h, i: (0, 0))],
        out_specs=[pl.BlockSpec((rows, GLA_DV), lambda h, i: (i, h)),
                   pl.BlockSpec((1, cb, GLA_DV, GLA_DK), lambda h, i: (h, i, 0, 0))],
        out_shape=(jax.ShapeDtypeStruct((T, GLA_V), F32),
                   jax.ShapeDtypeStruct((GLA_HEADS, T // CHUNK, GLA_DV, GLA_DK), F32)),
        scratch_shapes=[pltpu.VMEM((GLA_DV, GLA_DK), F32)],
        compiler_params=_params(("arbitrary", "arbitrary")),
    )(pa, pa, pa, la, ltri)


def _gla_core_bwd(pa, la, ltri, states, do):
    T = pa.shape[0]
    cb = min(GLA_CB, T // CHUNK)
    rows = cb * CHUNK
    nsteps = T // rows

    def body(q_ref, k_ref, v_ref, la_ref, ltri_ref, st_ref, sp_ref, do_ref,
             dq_ref, dk_ref, dv_ref, dla_ref, ds_scr):
        step = pl.program_id(1)

        @pl.when(step == 0)
        def _():
            ds_scr[...] = jnp.zeros_like(ds_scr)

        s_before = jnp.where(step == nsteps - 1, 0.0, sp_ref[0, 0])

        def chunk(cc, ds_t):
            c = cb - 1 - cc
            sl = pl.ds(pl.multiple_of(c * CHUNK, CHUNK), CHUNK)
            s_prev = jnp.where(c == 0, s_before, st_ref[0, jnp.maximum(c - 1, 0)])
            _, vjp = jax.vjp(functools.partial(_gla_chunk, ltri=ltri_ref[...]),
                             s_prev, q_ref[sl, :], k_ref[sl, :], v_ref[sl, :], la_ref[sl, :])
            ds_prev, dq, dk, dv, dla = vjp((ds_t, do_ref[sl, :]))
            dq_ref[sl, :] = dq
            dk_ref[sl, :] = dk
            dv_ref[sl, :] = dv
            dla_ref[sl, :] = dla
            return ds_prev

        ds_scr[...] = lax.fori_loop(0, cb, chunk, ds_scr[...])

    r = lambda i: nsteps - 1 - i
    qk = lambda off: pl.BlockSpec((rows, GLA_DK), lambda h, i, off=off: (r(i), off + h))
    vspec = pl.BlockSpec((rows, GLA_DV), lambda h, i: (r(i), 2 * GLA_QK // GLA_DV + h))
    o128 = pl.BlockSpec((rows, GLA_DK), lambda h, i: (r(i), h))
    o256 = pl.BlockSpec((rows, GLA_DV), lambda h, i: (r(i), h))
    return pl.pallas_call(
        body, name="gla_core_bwd", grid=(GLA_HEADS, nsteps),
        in_specs=[qk(0), qk(GLA_HEADS), vspec, qk(0), pl.BlockSpec((CHUNK, CHUNK), lambda h, i: (0, 0)),
                  pl.BlockSpec((1, cb, GLA_DV, GLA_DK), lambda h, i: (h, r(i), 0, 0)),
                  pl.BlockSpec((1, 1, GLA_DV, GLA_DK), lambda h, i: (h, jnp.maximum(r(i) * cb - 1, 0), 0, 0)),
                  o256],
        out_specs=[o128, o128, o256, o128],
        out_shape=(jax.ShapeDtypeStruct((T, GLA_QK), F32), jax.ShapeDtypeStruct((T, GLA_QK), F32),
                   jax.ShapeDtypeStruct((T, GLA_V), F32), jax.ShapeDtypeStruct((T, GLA_QK), F32)),
        scratch_shapes=[pltpu.VMEM((GLA_DV, GLA_DK), F32)],
        compiler_params=_params(("arbitrary", "arbitrary")),
    )(pa, pa, pa, la, ltri, states, states, do)


def _rowwise(fn, name, T, tm, rows, pars, row_outs, acc_outs):
    nr, npar, nro = len(rows), len(pars), len(row_outs)
    tm = min(tm, T)
    nsteps = T // tm

    def body(*refs):
        i = pl.program_id(0)
        ins = [r[...] for r in refs[:nr + npar]]
        outs, accs = fn(i, *ins)
        for r, o in zip(refs[nr + npar:nr + npar + nro], outs):
            r[...] = o.astype(r.dtype)
        for r, a in zip(refs[nr + npar + nro:], accs):
            @pl.when(i == 0)
            def _(r=r, a=a):
                r[...] = a

            @pl.when(i > 0)
            def _(r=r, a=a):
                r[...] += a

    def rspec(width, cb, kind):
        if kind == "cur":
            return pl.BlockSpec((tm, width), lambda i: (i, cb))
        if kind == "prev":
            return pl.BlockSpec((8, width), lambda i: (jnp.maximum(i * (tm // 8) - 1, 0), cb))
        return pl.BlockSpec((8, width), lambda i: (jnp.minimum((i + 1) * (tm // 8), T // 8 - 1), cb))

    in_specs = [rspec(w, cb, kind) for (_, w, cb, kind) in rows]
    in_specs += [pl.BlockSpec(p.shape, lambda i, nd=p.ndim: (0,) * nd) for p in pars]
    out_specs = [pl.BlockSpec((tm, w), lambda i: (i, 0)) for (w, _) in row_outs]
    out_specs += [pl.BlockSpec(s, lambda i, nd=len(s): (0,) * nd) for s in acc_outs]
    out_shape = [jax.ShapeDtypeStruct((T, w), dt) for (w, dt) in row_outs]
    out_shape += [jax.ShapeDtypeStruct(s, F32) for s in acc_outs]
    res = pl.pallas_call(
        body, name=name, grid=(nsteps,), in_specs=in_specs, out_specs=out_specs, out_shape=out_shape,
        compiler_params=_params(("arbitrary",)),
    )(*[r[0] for r in rows], *pars)
    return res


def _cur(a, width=None, cb=0):
    return (a, a.shape[1] if width is None else width, cb, "cur")


def _sigmoid(x):
    return 1.0 / (1.0 + jnp.exp(-x))


def _silu(x):
    return x * _sigmoid(x)


def _softplus(x):
    return jnp.maximum(x, 0.0) + jnp.log(1.0 + jnp.exp(-jnp.abs(x)))


def _rms(x, g):
    return x * lax.rsqrt(jnp.mean(x * x, axis=-1, keepdims=True) + NORM_EPS) * g


def _dot_hi(a, b):
    return jnp.dot(a, b, precision=HI, preferred_element_type=F32)


def _rms_fwd(x, g, name):
    T = x.shape[0]
    fn = lambda i, xb, gb: ((_rms(xb, gb),), ())
    return _rowwise(fn, name, T, 256, [_cur(x)], [g], [(D, BF16)], [])[0]


def _rms_bwd(x, g, dh, dres, name):
    T = x.shape[0]

    def fn(i, xb, dhb, drb, gb):
        _, vjp = jax.vjp(_rms, xb, gb)
        dx, dg = vjp(dhb)
        return (drb + dx,), (dg,)

    return _rowwise(fn, name, T, 256, [_cur(x), _cur(dh), _cur(dres)], [g], [(D, F32)], [(1, D)])


def _loss_bwd(x, target, g):
    T = x.shape[0]

    def loss(xb, gb, tb):
        err = _rms(xb, gb) - tb
        return 0.5 * jnp.sum(jnp.mean(err * err, axis=-1, keepdims=True))

    def fn(i, xb, tb, gb):
        val, (dx, dg) = jax.value_and_grad(loss, argnums=(0, 1))(xb, gb, tb)
        return (dx,), (jnp.full((1, 128), val, F32), dg)

    return _rowwise(fn, "loss_bwd", T, 256, [_cur(x), _cur(target)], [g], [(D, F32)], [(1, 128), (1, D)])


def _gla_la(a_down, w_a2, b_a):
    return -_softplus(-(_dot_hi(a_down, w_a2) + b_a)) * (1.0 / GLA_TAU)


def _gla_prep(pa, w_a2, b_a):
    T = pa.shape[0]
    fn = lambda i, ab, wb, bb: ((_gla_la(ab, wb, bb),), ())
    return _rowwise(fn, "gla_prep", T, 512, [_cur(pa, LORA_P, (2 * GLA_QK + 2 * GLA_V) // LORA_P)], [w_a2, b_a],
                    [(GLA_QK, F32)], [])[0]


def _gla_prep_bwd(pa, w_a2, b_a, dla):
    T = pa.shape[0]

    def fn(i, ab, dlab, wb, bb):
        _, vjp = jax.vjp(_gla_la, ab, wb, bb)
        da, dw, db = vjp(dlab)
        return (da,), (dw, db)

    return _rowwise(fn, "gla_prep_bwd", T, 512, [_cur(pa, LORA_P, (2 * GLA_QK + 2 * GLA_V) // LORA_P), _cur(dla)],
                    [w_a2, b_a], [(LORA_P, BF16)], [(LORA_P, GLA_QK), (1, GLA_QK)])


def _gla_out(o, r, gn, ind, ind_t):
    ms = _dot_hi(_dot_hi(o * o, ind) * (1.0 / GLA_DV), ind_t)
    return o * lax.rsqrt(ms + NORM_EPS) * gn * _silu(r)


def _gla_post(o_raw, pa, gn, ind, ind_t):
    T = pa.shape[0]
    fn = lambda i, ob, rb, gb, a, b: ((_gla_out(ob, rb, gb, a, b),), ())
    return _rowwise(fn, "gla_post", T, 256, [_cur(o_raw), _cur(pa, GLA_V, 2)], [gn, ind, ind_t], [(GLA_V, BF16)], [])[0]


def _gla_post_bwd(o_raw, pa, gn, ind, ind_t, do):
    T = pa.shape[0]

    def fn(i, ob, rb, dob, gb, a, b):
        _, vjp = jax.vjp(lambda o, r, g: _gla_out(o, r, g, a, b), ob, rb, gb)
        d_o, d_r, d_g = vjp(dob)
        return (d_o, d_r), (d_g,)

    return _rowwise(fn, "gla_post_bwd", T, 256, [_cur(o_raw), _cur(pa, GLA_V, 2), _cur(do)], [gn, ind, ind_t],
                    [(GLA_V, F32), (GLA_V, BF16)], [(1, GLA_V)])


def _shift_rows(cur, prev8, i):
    first = jnp.where(i == 0, 0.0, prev8[7:8, :])
    rolled = pltpu.roll(cur, 1, 0)
    return jnp.where(lax.broadcasted_iota(jnp.int32, cur.shape, 0) == 0, first, rolled)


def _rw_gates(rw, w0, w_w2, a0, w_a2, w_g2, k_k, k_a, ind, ind_t):
    rk = rw[:, RW_W:2 * RW_W]
    wd = rw[:, 3 * RW_W:3 * RW_W + LORA_P]
    ad = rw[:, 3 * RW_W + LORA_P:3 * RW_W + 2 * LORA_P]
    gd = rw[:, 3 * RW_W + 2 * LORA_P:]
    w_raw = w0 + _dot_hi(jnp.tanh(wd), w_w2)
    w = jnp.exp(-jnp.exp(-_softplus(-w_raw) - 0.5))
    a = _sigmoid(a0 + _dot_hi(ad, w_a2))
    g = _dot_hi(_sigmoid(gd), w_g2)
    kk = rk * k_k
    kk = kk * _dot_hi(lax.rsqrt(jnp.maximum(_dot_hi(kk * kk, ind), 1e-24)), ind_t)
    k2 = rk * (1.0 + (a - 1.0) * k_a)
    return w, k2, kk, kk * a, g


def _rw_prep(pr, mu, gate_pars):
    T = pr.shape[0]

    def fn(i, cur, prev8, mub, *gp):
        rw = cur + mub * (_shift_rows(cur, prev8, i) - cur)
        return (rw,) + _rw_gates(rw, *gp), ()

    return _rowwise(fn, "rw_prep", T, 256, [_cur(pr), (pr, PR_W, 0, "prev")], [mu, *gate_pars],
                    [(PR_W, F32)] + [(RW_W, F32)] * 5, [])


def _rw_prep_bwd(pr, mu, gate_pars, d_r, d_v, d_w, d_k2, d_kk, d_b, d_g):
    T = pr.shape[0]
    rows = [_cur(pr), (pr, PR_W, 0, "prev")] + [_cur(x) for x in (*d_r, *d_v, d_w, *d_k2, d_kk, d_b, d_g)]
    acc = [(1, PR_W)] + [tuple(p.shape) for p in gate_pars[:-2]]

    def fn(i, cur, prev8, dr1, dr2, dv1, dv2, dw, dk1, dk2, dkk, db, dg, mub, *gp):
        sh = _shift_rows(cur, prev8, i)
        rw = cur + mub * (sh - cur)
        _, vjp = jax.vjp(lambda x, *p: _rw_gates(x, *p, gp[-2], gp[-1]), rw, *gp[:-2])
        grads = vjp((dw, dk1 + dk2, dkk, db, dg))
        zeros = jnp.zeros((cur.shape[0], PR_W - 3 * RW_W), F32)
        drw = grads[0] + jnp.concatenate([dr1 + dr2, jnp.zeros_like(dr1), dv1 + dv2, zeros], axis=1)
        dmu = jnp.sum(drw * (sh - cur), axis=0, keepdims=True)
        return (drw,), (dmu, *grads[1:])

    return _rowwise(fn, "rw_prep_bwd", T, 128, rows, [mu, *gate_pars], [(PR_W, F32)], acc)


def _shift_bwd(drw, mu):
    T = drw.shape[0]
    tm = min(256, T)

    def fn(i, cur, next8, mub):
        last = jnp.where(i == T // tm - 1, 0.0, next8[0:1, :])
        rolled = pltpu.roll(cur, cur.shape[0] - 1, 0)
        nxt = jnp.where(lax.broadcasted_iota(jnp.int32, cur.shape, 0) == cur.shape[0] - 1, last, rolled)
        return ((1.0 - mub) * cur + mub * nxt,), ()

    return _rowwise(fn, "shift_bwd", T, tm, [_cur(drw), (drw, PR_W, 0, "next")], [mu], [(PR_W, BF16)], [])[0]


def _rw_out(y, r, v, k2, g, lnx_w, lnx_b, r_k, ind, ind_t):
    mean = _dot_hi(_dot_hi(y, ind) * (1.0 / RW_HD), ind_t)
    yc = y - mean
    var = _dot_hi(_dot_hi(yc * yc, ind) * (1.0 / RW_HD), ind_t)
    yn = yc * lax.rsqrt(var + GN_EPS) * lnx_w + lnx_b
    bonus = _dot_hi(_dot_hi(r * k2 * r_k, ind), ind_t) * v
    return (yn + bonus) * g


def _rw_post(y, rw, k2, g, pars):
    T = y.shape[0]
    fn = lambda i, *a: ((_rw_out(*a),), ())
    return _rowwise(fn, "rw_post", T, 256, [_cur(y), _cur(rw, RW_W, 0), _cur(rw, RW_W, 2), _cur(k2), _cur(g)], pars,
                    [(RW_W, BF16)], [])[0]


def _rw_post_bwd(y, rw, k2, g, pars, do):
    T = y.shape[0]

    def fn(i, yb, rb, vb, kb, gb, dob, lw, lb, rk, ind, ind_t):
        _, vjp = jax.vjp(lambda *a: _rw_out(*a, ind, ind_t), yb, rb, vb, kb, gb, lw, lb, rk)
        gr = vjp(dob)
        return gr[:5], gr[5:]

    return _rowwise(fn, "rw_post_bwd", T, 256,
                    [_cur(y), _cur(rw, RW_W, 0), _cur(rw, RW_W, 2), _cur(k2), _cur(g), _cur(do)], pars,
                    [(RW_W, F32)] * 5, [(1, RW_W)] * 3)


def _merge_bwd(dm, y_gla, y_rw, pg, gate_b):
    T = dm.shape[0]

    def fn(i, dmb, ya, yr, p1, p2, gb):
        g1 = _sigmoid(p1 + gb[:, :D])
        g2 = _sigmoid(p2 + gb[:, D:])
        dp1 = dmb * ya * g1 * (1.0 - g1)
        dp2 = dmb * yr * g2 * (1.0 - g2)
        dp = jnp.concatenate([dp1, dp2], axis=1)
        return (dmb * g1, dmb * g2, dp), (jnp.sum(dp, axis=0, keepdims=True),)

    return _rowwise(fn, "merge_bwd", T, 256, [_cur(dm), _cur(y_gla), _cur(y_rw), _cur(pg, D, 0), _cur(pg, D, 1)],
                    [gate_b], [(D, BF16), (D, BF16), (PG_W, BF16)], [(1, PG_W)])


_NN = (((1,), (0,)), ((), ()))
_NT = (((1,), (1,)), ((), ()))
_TN = (((0,), (0,)), ((), ()))


def _bdot(a, b, dims):
    return lax.dot_general(a.astype(BF16), b.astype(BF16), dims, preferred_element_type=F32)


def _accumulate(k, nk, acc, part, finish):
    if nk == 1:
        finish(part)
        return

    @pl.when(k == 0)
    def _():
        acc[...] = part

    @pl.when(k > 0)
    def _():
        acc[...] += part

    @pl.when(k == nk - 1)
    def _():
        finish(acc[...])


def _matmul(a, b, mode, M, N, K, tm, tn, tk, name, a_off=(0, 0), b_off=(0, 0), res=None, scale=1.0, out_dtype=F32):
    tm, tn, tk = min(tm, M), min(tn, N), min(tk, K)
    nk = K // tk
    if mode == "nn":
        a_spec = pl.BlockSpec((tm, tk), lambda i, j, k: (i + a_off[0], k + a_off[1]))
        b_spec = pl.BlockSpec((tk, tn), lambda i, j, k: (k + b_off[0], j + b_off[1]))
        dims = _NN
    elif mode == "nt":
        a_spec = pl.BlockSpec((tm, tk), lambda i, j, k: (i + a_off[0], k + a_off[1]))
        b_spec = pl.BlockSpec((tn, tk), lambda i, j, k: (j + b_off[0], k + b_off[1]))
        dims = _NT
    else:
        a_spec = pl.BlockSpec((tk, tm), lambda i, j, k: (k + a_off[0], i + a_off[1]))
        b_spec = pl.BlockSpec((tk, tn), lambda i, j, k: (k + b_off[0], j + b_off[1]))
        dims = _TN
    o_spec = pl.BlockSpec((tm, tn), lambda i, j, k: (i, j))

    def body(a_ref, b_ref, *rest):
        r_ref = rest[0] if res is not None else None
        o_ref = rest[1] if res is not None else rest[0]
        acc = rest[-1] if nk > 1 else None

        def finish(total):
            total = total * scale if scale != 1.0 else total
            if r_ref is not None:
                total = r_ref[...] + total
            o_ref[...] = total.astype(out_dtype)

        _accumulate(pl.program_id(2), nk, acc, _bdot(a_ref[...], b_ref[...], dims), finish)

    return pl.pallas_call(
        body, name=name, grid=(M // tm, N // tn, nk),
        in_specs=[a_spec, b_spec] + ([o_spec] if res is not None else []),
        out_specs=o_spec, out_shape=jax.ShapeDtypeStruct((M, N), out_dtype),
        scratch_shapes=[pltpu.VMEM((tm, tn), F32)] if nk > 1 else [],
        compiler_params=_params(("parallel", "parallel", "arbitrary")),
    )(a, b, *([res] if res is not None else []))


def _ffn_up(h, wg, wu, name):
    T = h.shape[0]
    tm = min(1024, T)

    def body(h_ref, wg_ref, wu_ref, a_ref, u_ref, s_ref):
        hb = h_ref[...]
        a = _bdot(hb, wg_ref[...], _NN)
        u = _bdot(hb, wu_ref[...], _NN)
        a_ref[...] = a
        u_ref[...] = u
        s_ref[...] = (_silu(a) * u).astype(BF16)

    w_spec = pl.BlockSpec((None, D, FSH), lambda i, j: (j, 0, 0))
    o_spec = pl.BlockSpec((None, tm, FSH), lambda i, j: (j, i, 0))
    sh = lambda dt: jax.ShapeDtypeStruct((NDEV, T, FSH), dt)
    return pl.pallas_call(
        body, name=name, grid=(T // tm, NDEV),
        in_specs=[pl.BlockSpec((tm, D), lambda i, j: (i, 0)), w_spec, w_spec],
        out_specs=[o_spec] * 3, out_shape=(sh(F32), sh(F32), sh(BF16)),
        compiler_params=_params(("parallel", "arbitrary")),
    )(h, wg, wu)


def _ffn_down(s, wd, x, name):
    T = x.shape[0]
    tm, tn = min(1024, T), 1024

    def body(s_ref, wd_ref, x_ref, o_ref, acc):
        def finish(total):
            o_ref[...] = x_ref[...] + 0.5 * total

        _accumulate(pl.program_id(2), NDEV, acc, _bdot(s_ref[...], wd_ref[...], _NN), finish)

    xo = pl.BlockSpec((tm, tn), lambda i, n, j: (i, n))
    return pl.pallas_call(
        body, name=name, grid=(T // tm, D // tn, NDEV),
        in_specs=[pl.BlockSpec((None, tm, FSH), lambda i, n, j: (j, i, 0)),
                  pl.BlockSpec((None, FSH, tn), lambda i, n, j: (j, 0, n)), xo],
        out_specs=xo, out_shape=jax.ShapeDtypeStruct((T, D), F32),
        scratch_shapes=[pltpu.VMEM((tm, tn), F32)],
        compiler_params=_params(("parallel", "parallel", "arbitrary")),
    )(s, wd, x)


def _ffn_bwd_hidden(dx, wd, a, u, name):
    T = dx.shape[0]
    tm = min(1024, T)

    def body(dx_ref, wd_ref, a_ref, u_ref, da_ref, du_ref):
        ds = 0.5 * _bdot(dx_ref[...], wd_ref[...], _NT)
        av = a_ref[...]
        sg = _sigmoid(av)
        da_ref[...] = (ds * u_ref[...] * (sg * (1.0 + av * (1.0 - sg)))).astype(BF16)
        du_ref[...] = (ds * (av * sg)).astype(BF16)

    act = pl.BlockSpec((None, tm, FSH), lambda i, j: (j, i, 0))
    sh = jax.ShapeDtypeStruct((NDEV, T, FSH), BF16)
    return pl.pallas_call(
        body, name=name, grid=(T // tm, NDEV),
        in_specs=[pl.BlockSpec((tm, D), lambda i, j: (i, 0)), pl.BlockSpec((None, FSH, D), lambda i, j: (j, 0, 0)),
                  act, act],
        out_specs=[act, act], out_shape=(sh, sh),
        compiler_params=_params(("parallel", "arbitrary")),
    )(dx, wd, a, u)


def _ffn_bwd_input(da, du, wg, wu, name):
    T = da.shape[1]
    tm, tn = min(1024, T), 1024

    def body(da_ref, du_ref, wg_ref, wu_ref, o_ref, acc):
        part = _bdot(da_ref[...], wg_ref[...], _NT) + _bdot(du_ref[...], wu_ref[...], _NT)

        def finish(total):
            o_ref[...] = total

        _accumulate(pl.program_id(2), NDEV, acc, part, finish)

    act = pl.BlockSpec((None, tm, FSH), lambda i, n, j: (j, i, 0))
    wsp = pl.BlockSpec((None, tn, FSH), lambda i, n, j: (j, n, 0))
    return pl.pallas_call(
        body, name=name, grid=(T // tm, D // tn, NDEV),
        in_specs=[act, act, wsp, wsp],
        out_specs=pl.BlockSpec((tm, tn), lambda i, n, j: (i, n)), out_shape=jax.ShapeDtypeStruct((T, D), F32),
        scratch_shapes=[pltpu.VMEM((tm, tn), F32)],
        compiler_params=_params(("parallel", "parallel", "arbitrary")),
    )(da, du, wg, wu)


def _ffn_grad_up(h, da, name):
    T = h.shape[0]
    tm, tk = 1024, min(1024, T)
    nk = T // tk

    def body(h_ref, da_ref, o_ref, acc):
        def finish(total):
            o_ref[...] = total

        _accumulate(pl.program_id(2), nk, acc, _bdot(h_ref[...], da_ref[...], _TN), finish)

    return pl.pallas_call(
        body, name=name, grid=(NDEV, D // tm, nk),
        in_specs=[pl.BlockSpec((tk, tm), lambda j, i, t: (t, i)), pl.BlockSpec((None, tk, FSH), lambda j, i, t: (j, t, 0))],
        out_specs=pl.BlockSpec((None, tm, FSH), lambda j, i, t: (j, i, 0)),
        out_shape=jax.ShapeDtypeStruct((NDEV, D, FSH), F32),
        scratch_shapes=[pltpu.VMEM((tm, FSH), F32)],
        compiler_params=_params(("parallel", "parallel", "arbitrary")),
    )(h, da)


def _ffn_grad_down(s, dx, name):
    T = dx.shape[0]
    tn, tk = 1024, min(1024, T)
    nk = T // tk

    def body(s_ref, dx_ref, o_ref, acc):
        def finish(total):
            o_ref[...] = 0.5 * total

        _accumulate(pl.program_id(2), nk, acc, _bdot(s_ref[...], dx_ref[...], _TN), finish)

    return pl.pallas_call(
        body, name=name, grid=(NDEV, D // tn, nk),
        in_specs=[pl.BlockSpec((None, tk, FSH), lambda j, n, t: (j, t, 0)), pl.BlockSpec((tk, tn), lambda j, n, t: (t, n))],
        out_specs=pl.BlockSpec((None, FSH, tn), lambda j, n, t: (j, 0, n)),
        out_shape=jax.ShapeDtypeStruct((NDEV, DFF // NDEV, D), F32),
        scratch_shapes=[pltpu.VMEM((FSH, tn), F32)],
        compiler_params=_params(("parallel", "parallel", "arbitrary")),
    )(s, dx)


def _branch_merge(o_gla, o_rw, wb, pg, gate_b):
    T = o_gla.shape[0]
    tm, tn = min(1024, T), 512

    def body(og_ref, or_ref, w1_ref, w2_ref, p1_ref, p2_ref, b1_ref, b2_ref, yg_ref, yr_ref, m_ref):
        yg = _bdot(og_ref[...], w1_ref[...], _NN)
        yr = _bdot(or_ref[...], w2_ref[...], _NN)
        yg_ref[...] = yg
        yr_ref[...] = yr
        m_ref[...] = (_sigmoid(p1_ref[...] + b1_ref[...]) * yg + _sigmoid(p2_ref[...] + b2_ref[...]) * yr).astype(BF16)

    nj = D // tn
    act = pl.BlockSpec((tm, GLA_V), lambda i, j: (i, 0))
    out = pl.BlockSpec((tm, tn), lambda i, j: (i, j))
    return pl.pallas_call(
        body, name="branch_merge", grid=(T // tm, nj),
        in_specs=[act, act, pl.BlockSpec((GLA_V, tn), lambda i, j: (0, j)), pl.BlockSpec((RW_W, tn), lambda i, j: (1, j)),
                  out, pl.BlockSpec((tm, tn), lambda i, j: (i, nj + j)),
                  pl.BlockSpec((1, tn), lambda i, j: (0, j)), pl.BlockSpec((1, tn), lambda i, j: (0, nj + j))],
        out_specs=[out, out, out],
        out_shape=(jax.ShapeDtypeStruct((T, D), F32), jax.ShapeDtypeStruct((T, D), F32), jax.ShapeDtypeStruct((T, D), BF16)),
        compiler_params=_params(("parallel", "arbitrary")),
    )(o_gla, o_rw, wb, wb, pg, pg, gate_b, gate_b)


def _head_indicator(width, heads):
    col = lax.broadcasted_iota(jnp.int32, (width, 128), 0) // (width // heads)
    ind = (col == lax.broadcasted_iota(jnp.int32, (width, 128), 1)).astype(F32)
    return ind, ind.T


def _ffn_fwd(x, g, wg, wu, wd, tag):
    h = _rms_fwd(x, g, "rms_" + tag)
    a, u, s = _ffn_up(h, wg, wu, "ffn_up_" + tag)
    return _ffn_down(s, wd, x, "ffn_down_" + tag), (h, a, u, s)


def _ffn_bwd(dy, x, g, wg, wu, wd, saved, tag):
    h, a, u, s = saved
    da, du = _ffn_bwd_hidden(dy, wd, a, u, "ffn_bwd_hidden_" + tag)
    dh = _ffn_bwd_input(da, du, wg, wu, "ffn_bwd_input_" + tag)
    dwg = _ffn_grad_up(h, da, "ffn_grad_gate_" + tag)
    dwu = _ffn_grad_up(h, du, "ffn_grad_up_" + tag)
    dwd = _ffn_grad_down(s, dy, "ffn_grad_down_" + tag)
    dx, dg = _rms_bwd(x, g, dh, dy, "rms_bwd_" + tag)
    return dx, dg, dwg, dwu, dwd


def _local_step(x, target, w):
    T = x.shape[0]
    ind16, ind16_t = _head_indicator(RW_W, RW_HEADS)
    ind4, ind4_t = _head_indicator(GLA_V, GLA_HEADS)
    ltri = jnp.tril(jnp.ones((CHUNK, CHUNK), F32))
    gate_pars = [w["w0"], w["w_w2"], w["a0"], w["w_a2"], w["w_g2"], w["k_k"], w["k_a"], ind16, ind16_t]
    post_pars = [w["lnx_w"], w["lnx_b"], w["r_k"], ind16, ind16_t]

    x1, ffn1 = _ffn_fwd(x, w["g1"], w["wg1"], w["wu1"], w["wd1"], "1")
    h2 = _rms_fwd(x1, w["g2"], "rms_mix")
    proj = lambda n, off, name: _matmul(h2, w["win"], "nn", T, n, D, 1024, 512, D, name, b_off=(0, off // 512))
    pg = proj(PG_W, 0, "proj_gate")
    pr = proj(PR_W, PG_W, "proj_rwkv")
    pa = proj(PA_W, PG_W + PR_W, "proj_gla")
    la = _gla_prep(pa, w["gla_w_a2"], w["gla_b_a"])
    o_raw, gla_states = _gla_core_fwd(pa, la, ltri)
    o_gla = _gla_post(o_raw, pa, w["gn"], ind4, ind4_t)
    rw, dec, k2, kk, b, g = _rw_prep(pr, w["mu"], gate_pars)
    y, rw_states = _rw_core_fwd(rw, dec, k2, kk, b)
    o_rw = _rw_post(y, rw, k2, g, post_pars)
    y_gla, y_rw, merged = _branch_merge(o_gla, o_rw, w["wb"], pg, w["gate_b"])
    x2 = _matmul(merged, w["wo"], "nn", T, D, D, 1024, 1024, D, "out_proj", res=x1)
    x3, ffn2 = _ffn_fwd(x2, w["g3"], w["wg2"], w["wu2"], w["wd2"], "2")
    dx3, loss, d_gf = _loss_bwd(x3, target, w["gf"])

    grads = {"gf": d_gf}
    dx2, grads["g3"], grads["wg2"], grads["wu2"], grads["wd2"] = _ffn_bwd(
        dx3, x2, w["g3"], w["wg2"], w["wu2"], w["wd2"], ffn2, "2")
    dm = _matmul(dx2, w["wo"], "nt", T, D, D, 1024, 1024, D, "out_proj_bwd")
    grads["wo"] = _matmul(merged, dx2, "tn", D, D, T, 1024, 1024, 1024, "out_proj_grad")
    dy_gla, dy_rw, dpg, grads["gate_b"] = _merge_bwd(dm, y_gla, y_rw, pg, w["gate_b"])
    do_gla = _matmul(dy_gla, w["wb"], "nt", T, GLA_V, D, 1024, 1024, D, "branch_gla_bwd")
    do_rw = _matmul(dy_rw, w["wb"], "nt", T, RW_W, D, 1024, 1024, D, "branch_rwkv_bwd", b_off=(1, 0))
    grads["wb"] = jnp.concatenate([
        _matmul(o_gla, dy_gla, "tn", GLA_V, D, T, 1024, 1024, 1024, "branch_gla_grad"),
        _matmul(o_rw, dy_rw, "tn", RW_W, D, T, 1024, 1024, 1024, "branch_rwkv_grad")], axis=0)
    dy, dr2, dv2, dk2b, dg, grads["lnx_w"], grads["lnx_b"], grads["r_k"] = _rw_post_bwd(y, rw, k2, g, post_pars, do_rw)
    dr1, dw, dk2a, dv1, dkk, db = _rw_core_bwd(rw, dec, k2, kk, b, rw_states, dy)
    drw, grads["mu"], grads["w0"], grads["w_w2"], grads["a0"], grads["w_a2"], grads["w_g2"], grads["k_k"], grads["k_a"] = (
        _rw_prep_bwd(pr, w["mu"], gate_pars, (dr1, dr2), (dv1, dv2), dw, (dk2a, dk2b), dkk, db, dg))
    dpr = _shift_bwd(drw, w["mu"])
    do_raw, dr_gla, grads["gn"] = _gla_post_bwd(o_raw, pa, w["gn"], ind4, ind4_t, do_gla)
    dq, dk, dv, dla = _gla_core_bwd(pa, la, ltri, gla_states, do_raw)
    da_down, grads["gla_w_a2"], grads["gla_b_a"] = _gla_prep_bwd(pa, w["gla_w_a2"], w["gla_b_a"], dla)
    dpa = jnp.concatenate([dq.astype(BF16), dk.astype(BF16), dv.astype(BF16), dr_gla, da_down,
                           jnp.zeros((T, PA_W - PA_USED), BF16)], axis=1)
    dp = jnp.concatenate([dpg, dpr, dpa], axis=1)
    dh2 = _matmul(dp, w["win"], "nt", T, D, DIN_P, 1024, 1024, 1024, "proj_bwd")
    grads["win"] = _matmul(h2, dp, "tn", D, DIN_P, T, 1024, 1024, 1024, "proj_grad")
    dx1, grads["g2"] = _rms_bwd(x1, w["g2"], dh2, dx2, "rms_bwd_mix")
    dx, grads["g1"], grads["wg1"], grads["wu1"], grads["wd1"] = _ffn_bwd(
        dx1, x, w["g1"], w["wg1"], w["wu1"], w["wd1"], ffn1, "1")
    return loss, dx, grads


BIG = ("ffn1_wg", "ffn1_wu", "ffn1_wd", "w_in", "w_branch", "w_out", "ffn2_wg", "ffn2_wu", "ffn2_wd")
SMALL_SHARDED = ("gla_w_a2", "rwkv_w_w2", "rwkv_w_a2", "rwkv_w_g2")
REPLICATED = ("ffn1_norm", "mix_norm", "gla_b_a", "gla_gn_w", "rwkv_mu", "rwkv_w0", "rwkv_a0", "rwkv_k_k", "rwkv_k_a",
              "rwkv_r_k", "rwkv_lnx_w", "rwkv_lnx_b", "gate_b", "ffn2_norm", "final_norm")
WEIGHTS = ("ffn1_norm", "ffn1_wg", "ffn1_wu", "ffn1_wd", "mix_norm", "w_in", "gla_w_a2", "gla_b_a", "gla_gn_w",
           "rwkv_mu", "rwkv_w0", "rwkv_w_w2", "rwkv_a0", "rwkv_w_a2", "rwkv_w_g2", "rwkv_k_k", "rwkv_k_a", "rwkv_r_k",
           "rwkv_lnx_w", "rwkv_lnx_b", "gate_b", "w_branch", "w_out", "ffn2_norm", "ffn2_wg", "ffn2_wu", "ffn2_wd",
           "final_norm")


def _unshard_cols(g):
    return jnp.transpose(g, (1, 0, 2)).reshape(g.shape[1], NDEV * g.shape[2])


def _shard_cols(a):
    return jnp.transpose(a.reshape(a.shape[0], NDEV, a.shape[1] // NDEV), (1, 0, 2))


def _pad_rows(a, rows):
    return jnp.pad(a, ((0, rows - a.shape[0]), (0, 0)))


def _align_rw(a):
    c = 3 * RW_W
    z = jnp.zeros((a.shape[0], LORA_P - DECAY_LORA), a.dtype)
    return jnp.concatenate([a[:, :c], a[:, c:c + DECAY_LORA], z, a[:, c + DECAY_LORA:c + 2 * DECAY_LORA], z,
                            a[:, c + 2 * DECAY_LORA:]], axis=1)


def _unalign_rw(a):
    c = 3 * RW_W
    return jnp.concatenate([a[:, :c + DECAY_LORA], a[:, c + LORA_P:c + LORA_P + AAA_LORA], a[:, c + 2 * LORA_P:]], axis=1)


def _align_proj(a):
    gla = jnp.pad(a[:, :GLA_IN], ((0, 0), (0, PA_W - GLA_IN)))
    return jnp.concatenate([a[:, GLA_IN + RW_IN:], _align_rw(a[:, GLA_IN:GLA_IN + RW_IN]), gla], axis=1)


def _unalign_proj(a):
    return jnp.concatenate([a[:, PG_W + PR_W:PG_W + PR_W + GLA_IN], _unalign_rw(a[:, PG_W:PG_W + PR_W]), a[:, :PG_W]], axis=1)


def _layout_weights(gb, gs, rep):
    row = lambda n: rep[n].reshape(1, -1)
    return {
        "wg1": gb["ffn1_wg"], "wu1": gb["ffn1_wu"], "wd1": gb["ffn1_wd"],
        "wg2": gb["ffn2_wg"], "wu2": gb["ffn2_wu"], "wd2": gb["ffn2_wd"],
        "win": _align_proj(_unshard_cols(gb["w_in"])),
        "wb": gb["w_branch"].reshape(GLA_V + RW_W, D), "wo": gb["w_out"].reshape(D, D),
        "g1": row("ffn1_norm"), "g2": row("mix_norm"), "g3": row("ffn2_norm"), "gf": row("final_norm"),
        "gla_w_a2": _pad_rows(_unshard_cols(gs["gla_w_a2"]), LORA_P), "gla_b_a": row("gla_b_a"),
        "gn": jnp.tile(row("gla_gn_w"), (1, GLA_HEADS)),
        "mu": _align_rw(row("rwkv_mu")), "w0": row("rwkv_w0"), "a0": row("rwkv_a0"),
        "w_w2": _pad_rows(_unshard_cols(gs["rwkv_w_w2"]), LORA_P),
        "w_a2": _pad_rows(_unshard_cols(gs["rwkv_w_a2"]), LORA_P),
        "w_g2": _unshard_cols(gs["rwkv_w_g2"]),
        "k_k": row("rwkv_k_k"), "k_a": row("rwkv_k_a"), "r_k": row("rwkv_r_k"),
        "lnx_w": row("rwkv_lnx_w"), "lnx_b": row("rwkv_lnx_b"), "gate_b": row("gate_b"),
    }


def _layout_grads(g):
    big = {
        "ffn1_wg": g["wg1"], "ffn1_wu": g["wu1"], "ffn1_wd": g["wd1"],
        "ffn2_wg": g["wg2"], "ffn2_wu": g["wu2"], "ffn2_wd": g["wd2"],
        "w_in": _shard_cols(_unalign_proj(g["win"])),
        "w_branch": g["wb"].reshape(NDEV, (GLA_V + RW_W) // NDEV, D), "w_out": g["wo"].reshape(NDEV, D // NDEV, D),
    }
    small = {
        "ffn1_norm": g["g1"], "mix_norm": g["g2"], "ffn2_norm": g["g3"], "final_norm": g["gf"],
        "gla_w_a2": g["gla_w_a2"][:GLA_LORA], "gla_b_a": g["gla_b_a"],
        "gla_gn_w": jnp.sum(g["gn"].reshape(GLA_HEADS, GLA_DV), axis=0, keepdims=True),
        "rwkv_mu": _unalign_rw(g["mu"]), "rwkv_w0": g["w0"], "rwkv_a0": g["a0"],
        "rwkv_w_w2": g["w_w2"][:DECAY_LORA], "rwkv_w_a2": g["w_a2"][:AAA_LORA], "rwkv_w_g2": g["w_g2"],
        "rwkv_k_k": g["k_k"], "rwkv_k_a": g["k_a"], "rwkv_r_k": g["r_k"],
        "rwkv_lnx_w": g["lnx_w"], "rwkv_lnx_b": g["lnx_b"], "gate_b": g["gate_b"],
    }
    return big, small


_MESH = pl.DeviceIdType.MESH
_ANY = pl.BlockSpec(memory_space=pl.ANY)


def _position():
    return lax.axis_index("x"), lax.axis_index("y"), lax.axis_index("c")


def _slot(p):
    return 4 * p[0] + 2 * p[1] + p[2]


def _all_gather(arrays, name):
    n = len(arrays)

    def body(*refs):
        ins, outs = refs[:n], refs[n:2 * n]
        send_sems, recv_sems, local_sems = refs[2 * n:]
        x, y, c = _position()
        me, sibling = (x, y, c), (x, y, 1 - c)
        chips = [(1 - x, y), (x, 1 - y), (1 - x, 1 - y)]

        def copy(a, k, block, to, src=None):
            dst = outs[a].at[_slot(block)]
            return pltpu.make_async_remote_copy(
                src_ref=dst if src is None else src, dst_ref=dst, send_sem=send_sems.at[7 * a + k],
                recv_sem=recv_sems.at[7 * a + k], device_id=to, device_id_type=_MESH)

        local, sends = [], []
        for a in range(n):
            mine = pltpu.make_async_copy(ins[a], outs[a].at[_slot(me)], local_sems.at[a])
            mine.start()
            local.append(mine)
            first = [copy(a, 0, me, sibling, src=ins[a])]
            first += [copy(a, 1 + j, me, (*chip, c), src=ins[a]) for j, chip in enumerate(chips)]
            for cp in first:
                cp.start()
            sends += first
        for a in range(n):
            for j, chip in enumerate(chips):
                copy(a, 1 + j, (*chip, c), me).wait_recv()
                passed = copy(a, 4 + j, (*chip, c), sibling)
                passed.start()
                sends.append(passed)
        for a in range(n):
            copy(a, 0, sibling, me).wait_recv()
            for j, chip in enumerate(chips):
                copy(a, 4 + j, (*chip, 1 - c), me).wait_recv()
        for cp in sends:
            cp.wait_send()
        for mine in local:
            mine.wait()

    return pl.pallas_call(
        body, name=name, in_specs=[_ANY] * n, out_specs=[_ANY] * n,
        out_shape=[jax.ShapeDtypeStruct((NDEV,) + a.shape, a.dtype) for a in arrays],
        scratch_shapes=[pltpu.SemaphoreType.DMA((7 * n,)), pltpu.SemaphoreType.DMA((7 * n,)), pltpu.SemaphoreType.DMA((n,))],
    )(*arrays)


def _exchange(arrays, name):
    n = len(arrays)

    def body(*refs):
        ins, outs = refs[:n], refs[n:2 * n]
        send_sems, recv_sems, local_sems = refs[2 * n:]
        x, y, c = _position()
        me = (x, y, c)
        flip = lambda v, f: 1 - v if f else v
        peers = [(flip(x, fx), flip(y, fy), flip(c, fc))
                 for fx, fy, fc in ((0, 0, 1), (1, 0, 0), (0, 1, 0), (1, 1, 0), (1, 0, 1), (0, 1, 1), (1, 1, 1))]

        def copy(a, k, src_slot, dst_slot):
            return pltpu.make_async_remote_copy(
                src_ref=ins[a].at[src_slot], dst_ref=outs[a].at[dst_slot], send_sem=send_sems.at[7 * a + k],
                recv_sem=recv_sems.at[7 * a + k], device_id=peers[k], device_id_type=_MESH)

        local, sends = [], []
        for a in range(n):
            mine = pltpu.make_async_copy(ins[a].at[_slot(me)], outs[a].at[_slot(me)], local_sems.at[a])
            mine.start()
            local.append(mine)
            for k, peer in enumerate(peers):
                cp = copy(a, k, _slot(peer), _slot(me))
                cp.start()
                sends.append(cp)
        for a in range(n):
            for k, peer in enumerate(peers):
                copy(a, k, _slot(peer), _slot(peer)).wait_recv()
        for cp in sends:
            cp.wait_send()
        for mine in local:
            mine.wait()

    return pl.pallas_call(
        body, name=name, in_specs=[_ANY] * n, out_specs=[_ANY] * n,
        out_shape=[jax.ShapeDtypeStruct(a.shape, a.dtype) for a in arrays],
        scratch_shapes=[pltpu.SemaphoreType.DMA((7 * n,)), pltpu.SemaphoreType.DMA((7 * n,)), pltpu.SemaphoreType.DMA((n,))],
    )(*arrays)


def _adamw_math(w, g, m, v):
    m = ADAM_B1 * m + (1.0 - ADAM_B1) * g
    v = ADAM_B2 * v + (1.0 - ADAM_B2) * (g * g)
    m_hat = m / (1.0 - ADAM_B1 ** ADAM_STEP)
    v_hat = v / (1.0 - ADAM_B2 ** ADAM_STEP)
    delta = -ADAM_LR * (m_hat / (jnp.sqrt(v_hat) + ADAM_EPS) + ADAM_WD * w)
    return delta, m, v


def _sum_slots(ref):
    total = ref[0].astype(F32)
    for s in range(1, NDEV):
        total = total + ref[s].astype(F32)
    return total


def _adamw(parts, w, m, v, tr, name, stack_index=None, row_block_offset=0):
    R, C = w.shape

    def body(p_ref, w_ref, m_ref, v_ref, g_ref, d_ref, nm_ref, nv_ref):
        g = _sum_slots(p_ref)
        g_ref[...] = g
        d_ref[...], nm_ref[...], nv_ref[...] = _adamw_math(w_ref[...], g, m_ref[...], v_ref[...])

    if stack_index is None:
        p_spec = pl.BlockSpec((NDEV, tr, C), lambda r: (0, row_block_offset + r, 0))
    else:
        p_spec = pl.BlockSpec((NDEV, None, tr, C), lambda r: (0, stack_index, r, 0))
    blk = pl.BlockSpec((tr, C), lambda r: (r, 0))
    out = jax.ShapeDtypeStruct((R, C), F32)
    return pl.pallas_call(
        body, name=name, grid=(R // tr,), in_specs=[p_spec, blk, blk, blk], out_specs=[blk] * 4, out_shape=(out,) * 4,
        compiler_params=_params(("parallel",)),
    )(parts, w, m, v)


def _sum_gathered(parts):
    _, R, C = parts.shape

    def body(p_ref, o_ref):
        o_ref[...] = _sum_slots(p_ref)

    return pl.pallas_call(body, name="small_grad_sum", out_shape=jax.ShapeDtypeStruct((R, C), F32),
                          compiler_params=_params())(parts)


def _adamw_small(w, g, m, v):
    def body(w_ref, g_ref, m_ref, v_ref, d_ref, nm_ref, nv_ref):
        d_ref[...], nm_ref[...], nv_ref[...] = _adamw_math(w_ref[...], g_ref[...], m_ref[...], v_ref[...])

    out = jax.ShapeDtypeStruct(w.shape, F32)
    return pl.pallas_call(body, name="adamw_small", out_shape=(out,) * 3, compiler_params=_params())(w, g, m, v)


def _pack(pieces, rows):
    flat = jnp.concatenate([p.reshape(-1) for p in pieces])
    return jnp.pad(flat, (0, rows * 128 - flat.shape[0])).reshape(rows, 128)


def _unpack(packed, shapes):
    flat = packed.reshape(-1)
    out, off = [], 0
    for s in shapes:
        size = 1
        for d in s:
            size *= d
        out.append(flat[off:off + size].reshape(s))
        off += size
    return out


def _rows_for(shapes, extra=0):
    total = extra
    for s in shapes:
        size = 1
        for d in s:
            size *= d
        total += size
    return -(-total // 1024) * 8


def kernel(x, ffn1_norm, ffn1_wg, ffn1_wu, ffn1_wd, mix_norm, w_in, gla_w_a2, gla_b_a, gla_gn_w, rwkv_mu, rwkv_w0, rwkv_w_w2, rwkv_a0, rwkv_w_a2, rwkv_w_g2, rwkv_k_k, rwkv_k_a, rwkv_r_k, rwkv_lnx_w, rwkv_lnx_b, gate_b, w_branch, w_out, ffn2_norm, ffn2_wg, ffn2_wu, ffn2_wd, final_norm, loss_target, m_ffn1_norm, m_ffn1_wg, m_ffn1_wu, m_ffn1_wd, m_mix_norm, m_w_in, m_gla_w_a2, m_gla_b_a, m_gla_gn_w, m_rwkv_mu, m_rwkv_w0, m_rwkv_w_w2, m_rwkv_a0, m_rwkv_w_a2, m_rwkv_w_g2, m_rwkv_k_k, m_rwkv_k_a, m_rwkv_r_k, m_rwkv_lnx_w, m_rwkv_lnx_b, m_gate_b, m_w_branch, m_w_out, m_ffn2_norm, m_ffn2_wg, m_ffn2_wu, m_ffn2_wd, m_final_norm, v_ffn1_norm, v_ffn1_wg, v_ffn1_wu, v_ffn1_wd, v_mix_norm, v_w_in, v_gla_w_a2, v_gla_b_a, v_gla_gn_w, v_rwkv_mu, v_rwkv_w0, v_rwkv_w_w2, v_rwkv_a0, v_rwkv_w_a2, v_rwkv_w_g2, v_rwkv_k_k, v_rwkv_k_a, v_rwkv_r_k, v_rwkv_lnx_w, v_rwkv_lnx_b, v_gate_b, v_w_branch, v_w_out, v_ffn2_norm, v_ffn2_wg, v_ffn2_wu, v_ffn2_wd, v_final_norm):
    wts = dict(zip(WEIGHTS, (ffn1_norm, ffn1_wg, ffn1_wu, ffn1_wd, mix_norm, w_in, gla_w_a2, gla_b_a, gla_gn_w, rwkv_mu, rwkv_w0, rwkv_w_w2, rwkv_a0, rwkv_w_a2, rwkv_w_g2, rwkv_k_k, rwkv_k_a, rwkv_r_k, rwkv_lnx_w, rwkv_lnx_b, gate_b, w_branch, w_out, ffn2_norm, ffn2_wg, ffn2_wu, ffn2_wd, final_norm)))
    mom = dict(zip(WEIGHTS, (m_ffn1_norm, m_ffn1_wg, m_ffn1_wu, m_ffn1_wd, m_mix_norm, m_w_in, m_gla_w_a2, m_gla_b_a, m_gla_gn_w, m_rwkv_mu, m_rwkv_w0, m_rwkv_w_w2, m_rwkv_a0, m_rwkv_w_a2, m_rwkv_w_g2, m_rwkv_k_k, m_rwkv_k_a, m_rwkv_r_k, m_rwkv_lnx_w, m_rwkv_lnx_b, m_gate_b, m_w_branch, m_w_out, m_ffn2_norm, m_ffn2_wg, m_ffn2_wu, m_ffn2_wd, m_final_norm)))
    var = dict(zip(WEIGHTS, (v_ffn1_norm, v_ffn1_wg, v_ffn1_wu, v_ffn1_wd, v_mix_norm, v_w_in, v_gla_w_a2, v_gla_b_a, v_gla_gn_w, v_rwkv_mu, v_rwkv_w0, v_rwkv_w_w2, v_rwkv_a0, v_rwkv_w_a2, v_rwkv_w_g2, v_rwkv_k_k, v_rwkv_k_a, v_rwkv_r_k, v_rwkv_lnx_w, v_rwkv_lnx_b, v_gate_b, v_w_branch, v_w_out, v_ffn2_norm, v_ffn2_wg, v_ffn2_wu, v_ffn2_wd, v_final_norm)))
    two = lambda a: a.reshape(a.shape[-2:])

    up = jnp.stack([two(wts[n]) for n in ("ffn1_wg", "ffn1_wu", "ffn2_wg", "ffn2_wu")]).astype(BF16)
    down_names = ("ffn1_wd", "ffn2_wd", "w_branch", "w_out")
    down = jnp.concatenate([two(wts[n]) for n in down_names], axis=0).astype(BF16)
    proj = two(w_in).astype(BF16)
    lora = jnp.concatenate([jnp.pad(two(gla_w_a2), ((0, 0), (0, 128 - GLA_QK // NDEV)))] +
                           [two(wts[n]) for n in SMALL_SHARDED[1:]], axis=0)
    g_up, g_down, g_proj, g_lora = _all_gather([up, down, proj, lora], "gather_weights")
    gb = {"ffn1_wg": g_up[:, 0], "ffn1_wu": g_up[:, 1], "ffn2_wg": g_up[:, 2], "ffn2_wu": g_up[:, 3], "w_in": g_proj}
    row = 0
    for n in down_names:
        gb[n] = g_down[:, row:row + wts[n].shape[1]]
        row += wts[n].shape[1]
    gs = {"gla_w_a2": g_lora[:, :GLA_LORA, :GLA_QK // NDEV]}
    row = GLA_LORA
    for n in SMALL_SHARDED[1:]:
        gs[n] = g_lora[:, row:row + wts[n].shape[1]]
        row += wts[n].shape[1]
    w = _layout_weights(gb, gs, {n: wts[n] for n in REPLICATED})

    loss_part, grad_x, grads = _local_step(x[0], loss_target[0], w)
    big, small = _layout_grads(grads)

    p_up = jnp.stack([big[n] for n in ("ffn1_wg", "ffn1_wu", "ffn2_wg", "ffn2_wu")], axis=1).astype(BF16)
    p_down = jnp.concatenate([big[n] for n in down_names], axis=1).astype(BF16)
    p_proj = big["w_in"].astype(BF16)
    r_up, r_down, r_proj = _exchange([p_up, p_down, p_proj], "exchange_grads")
    result = {}
    for i, n in enumerate(("ffn1_wg", "ffn1_wu", "ffn2_wg", "ffn2_wu")):
        result[n] = _adamw(r_up, two(wts[n]), two(mom[n]), two(var[n]), 256, "adamw_" + n, stack_index=i)
    row = 0
    for n in down_names:
        result[n] = _adamw(r_down, two(wts[n]), two(mom[n]), two(var[n]), 64, "adamw_" + n, row_block_offset=row // 64)
        row += wts[n].shape[1]
    result["w_in"] = _adamw(r_proj, two(w_in), two(mom["w_in"]), two(var["w_in"]), 256, "adamw_w_in")

    small_names = [n for n in WEIGHTS if n not in BIG]
    full_shapes = [small[n].shape for n in small_names]
    rows_full = _rows_for(full_shapes, extra=128)
    packed = _pack([small[n] for n in small_names] + [loss_part], rows_full)
    (gathered,) = _all_gather([packed], "gather_small_grads")
    total = _sum_gathered(gathered)
    *full_grads, loss_row = _unpack(total, full_shapes + [(1, 128)])
    me = _slot(_position())
    own = {}
    for n, g in zip(small_names, full_grads):
        if n in SMALL_SHARDED:
            cols = wts[n].shape[-1]
            g = lax.dynamic_slice_in_dim(g, me * cols, cols, axis=1)
        own[n] = g.reshape(wts[n].shape)
    own_shapes = [wts[n].shape for n in small_names]
    rows_own = _rows_for(own_shapes)
    pk = lambda d: _pack([d[n] for n in small_names], rows_own)
    d_s, m_s, v_s = _adamw_small(pk(wts), pk(own), pk(mom), pk(var))
    for n, d, m, v in zip(small_names, _unpack(d_s, own_shapes), _unpack(m_s, own_shapes), _unpack(v_s, own_shapes)):
        result[n] = (own[n], d, m, v)

    shaped = lambda n, k: result[n][k].reshape(wts[n].shape)
    return (loss_row[0, 0], grad_x[None],
            *[shaped(n, 0) for n in WEIGHTS], *[shaped(n, 1) for n in WEIGHTS],
            *[shaped(n, 2) for n in WEIGHTS], *[shaped(n, 3) for n in WEIGHTS])
```

```python
import functools

import jax
import jax.numpy as jnp
from jax import lax
from jax.experimental import pallas as pl
from jax.experimental.pallas import tpu as pltpu

F32 = jnp.float32
BF16 = jnp.bfloat16
HI = lax.Precision.HIGHEST

NDEV = 8
D = 2048
DFF = 5632
FSH = DFF // NDEV
CHUNK = 64
GLA_HEADS, GLA_DK, GLA_DV = 4, 128, 256
GLA_QK, GLA_V, GLA_LORA, GLA_TAU = 512, 1024, 16, 16.0
RW_HEADS, RW_HD, RW_W = 16, 64, 1024
DECAY_LORA, AAA_LORA, GATE_LORA = 96, 96, 256
GN_EPS = 64e-5
NORM_EPS = 1e-6
GLA_IN = 2 * GLA_QK + 2 * GLA_V + GLA_LORA
RW_IN = 3 * RW_W + DECAY_LORA + AAA_LORA + GATE_LORA
D_IN = GLA_IN + RW_IN + 2 * D
DIN_SH = D_IN // NDEV
PG_W = 2 * D
PR_W = 3584
PA_W = 3584
PA_USED = 2 * GLA_QK + 2 * GLA_V + 128
DIN_P = PG_W + PR_W + PA_W
LORA_P = 128

ADAM_LR, ADAM_B1, ADAM_B2, ADAM_EPS, ADAM_WD, ADAM_STEP = 0.001, 0.9, 0.999, 1e-08, 0.01, 10

VMEM_LIMIT = 56 * 1024 * 1024
RW_TB = 128
RW_G = 16
RW_NP = 4


def _params(sem=None, vmem=VMEM_LIMIT):
    return pltpu.CompilerParams(dimension_semantics=sem, vmem_limit_bytes=vmem)


def _pair_mask():
    return lax.broadcasted_iota(jnp.int32, (RW_HD, 2 * RW_HD), 1) < RW_HD


def _pair_rowsum(p, mask):
    tot = jnp.sum(p, axis=1, keepdims=True)
    first = jnp.sum(jnp.where(mask, p, 0.0), axis=1, keepdims=True)
    return first, tot - first


def _split_transposed(x_ref, q, dst_ref, base):
    xt = x_ref[:, 128 * q:128 * (q + 1)].T
    for g in range(RW_TB // RW_G):
        dst_ref[base + g, :, 0:RW_G] = xt[:, g * RW_G:(g + 1) * RW_G]


def _pair_column(tile_ref, idx, i, mask):
    return jnp.where(mask, tile_ref[idx, 0:RW_HD, i:i + 1], tile_ref[idx, RW_HD:, i:i + 1])


def _rw_core_fwd(rw, w, k2, kk, b, gather=()):
    T = rw.shape[0]
    nb = T // RW_TB
    ng = RW_TB // RW_G
    NP = RW_NP
    nc = len(gather)
    npair = RW_HEADS // 2 // NP

    def body(r_ref, v_ref, w_ref, k_ref, kk_ref, b_ref, *rest):
        g_in, (y_ref, st_ref), g_out = rest[:nc], rest[nc:nc + 2], rest[nc + 2:2 * nc + 2]
        s_scr, vt_scr, yt_scr, rows_scr = rest[2 * nc + 2:2 * nc + 6]
        pair, blk_i = pl.program_id(0), pl.program_id(1)
        if nc:
            start, forward, finish = _gather_plan(g_in, g_out, *rest[2 * nc + 6:])
            pl.when((pair == 0) & (blk_i == 0))(start)
            pl.when((pair == 0) & (blk_i == nb // 2))(forward)

        @pl.when(pl.program_id(1) == 0)
        def _():
            s_scr[...] = jnp.zeros_like(s_scr)
            yt_scr[...] = jnp.zeros_like(yt_scr)

        mask = _pair_mask()
        for q in range(NP):
            _split_transposed(v_ref, q, vt_scr, q * ng)
        R_, W_, K_, KK_, B_ = range(5)
        for a, ref in enumerate((r_ref, w_ref, k_ref, kk_ref, b_ref)):
            for q in range(NP):
                rows_scr[a * NP + q] = ref[:, 128 * q:128 * (q + 1)]

        def group(g, states):
            states = list(states)
            for i in range(RW_G):
                t = g * RW_G + i
                row = lambda a, q: rows_scr[a * NP + q, pl.ds(t, 1), :]
                sums = [_pair_rowsum(states[q] * row(KK_, q), mask) for q in range(NP)]
                for q in range(NP):
                    sa = jnp.where(mask, *sums[q])
                    states[q] = (states[q] * row(W_, q) - sa * row(B_, q)
                                 + _pair_column(vt_scr, q * ng + g, i, mask) * row(K_, q))
                    st_ref[q, t] = states[q]
                outs = [_pair_rowsum(states[q] * row(R_, q), mask) for q in range(NP)]
                for q in range(NP):
                    yt_scr[q * ng + g, 0:RW_HD, i:i + 1] = outs[q][0]
                    yt_scr[q * ng + g, RW_HD:, i:i + 1] = outs[q][1]
            return tuple(states)

        states = lax.fori_loop(0, ng, group, tuple(s_scr[q] for q in range(NP)))
        for q in range(NP):
            s_scr[q] = states[q]
            for g in range(ng):
                y_ref[g * RW_G:(g + 1) * RW_G, 128 * q:128 * (q + 1)] = yt_scr[q * ng + g].T[0:RW_G, :]
        if nc:
            pl.when((pair == npair - 1) & (blk_i == nb - 1))(finish)

    blk = lambda cb: pl.BlockSpec((RW_TB, 128 * NP), lambda p, i, cb=cb: (i, cb + p))
    tiles = pltpu.VMEM((NP * ng, 128, 128), F32)
    return pl.pallas_call(
        body, name="rw_core_fwd", grid=(npair, nb),
        in_specs=[blk(0), blk(2 * RW_W // (128 * NP)), blk(0), blk(0), blk(0), blk(0)] + [_ANY] * nc,
        out_specs=[blk(0), pl.BlockSpec((NP, RW_TB, RW_HD, 128), lambda p, i: (p, i, 0, 0))] + [_ANY] * nc,
        out_shape=[jax.ShapeDtypeStruct((T, RW_W), F32), jax.ShapeDtypeStruct((RW_HEADS // 2, T, RW_HD, 128), F32)]
        + [jax.ShapeDtypeStruct((NDEV,) + a.shape, a.dtype) for a in gather],
        scratch_shapes=[pltpu.VMEM((NP, RW_HD, 128), F32), tiles, tiles, pltpu.VMEM((5 * NP, RW_TB, 128), F32)]
        + _comm_sems(nc),
        compiler_params=_params(("arbitrary", "arbitrary")),
    )(rw, rw, w, k2, kk, b, *gather)


def _rw_core_bwd(rw, w, k2, kk, b, states, dy, exchange=()):
    T = rw.shape[0]
    nb = T // RW_TB
    ng = RW_TB // RW_G
    NP = RW_NP
    nc = len(exchange)
    npair = RW_HEADS // 2 // NP

    def body(r_ref, v_ref, w_ref, k_ref, kk_ref, b_ref, dy_ref, st_ref, sp_ref, *rest):
        e_in, e_out = rest[:nc], rest[nc + 6:2 * nc + 6]
        dr_ref, dw_ref, dk_ref, dv_ref, dkk_ref, db_ref = rest[nc:nc + 6]
        ds_scr, vt_scr, dyt_scr, dvt_scr, rows_scr, out_scr = rest[2 * nc + 6:2 * nc + 12]
        step = pl.program_id(1)
        if nc:
            start, finish = _exchange_plan(e_in, e_out, *rest[2 * nc + 12:])
            pl.when((pl.program_id(0) == 0) & (step == 0))(start)

        @pl.when(step == 0)
        def _():
            ds_scr[...] = jnp.zeros_like(ds_scr)
            dvt_scr[...] = jnp.zeros_like(dvt_scr)

        mask = _pair_mask()
        for q in range(NP):
            _split_transposed(v_ref, q, vt_scr, q * ng)
            _split_transposed(dy_ref, q, dyt_scr, q * ng)
        R_, W_, K_, KK_, B_ = range(5)
        for a, ref in enumerate((r_ref, w_ref, k_ref, kk_ref, b_ref)):
            for q in range(NP):
                rows_scr[a * NP + q] = ref[:, 128 * q:128 * (q + 1)]

        def group(gg, grads):
            g = ng - 1 - gg
            grads = list(grads)
            pairs = range(NP)
            for i in reversed(range(RW_G)):
                t = g * RW_G + i
                row = lambda a, q: rows_scr[a * NP + q, pl.ds(t, 1), :]

                def put(a, q, value):
                    out_scr[a * NP + q, pl.ds(t, 1), :] = value

                s_old = [st_ref[q, jnp.maximum(t - 1, 0)] for q in pairs]
                if i == 0:
                    s_old = [jnp.where(g == 0, jnp.where(step == nb - 1, 0.0, sp_ref[q, 0]), s_old[q]) for q in pairs]
                dycol = [_pair_column(dyt_scr, q * ng + g, i, mask) for q in pairs]
                dS = [grads[q] + dycol[q] * row(R_, q) for q in pairs]
                m = [_pair_rowsum(dS[q] * row(B_, q), mask) for q in pairs]
                sa = [_pair_rowsum(s_old[q] * row(KK_, q), mask) for q in pairs]
                dv = [_pair_rowsum(dS[q] * row(K_, q), mask) for q in pairs]
                for q in pairs:
                    put(R_, q, jnp.sum(st_ref[q, t] * dycol[q], axis=0, keepdims=True))
                    put(W_, q, jnp.sum(dS[q] * s_old[q], axis=0, keepdims=True))
                    put(K_, q, jnp.sum(dS[q] * _pair_column(vt_scr, q * ng + g, i, mask), axis=0, keepdims=True))
                for q in pairs:
                    dsa = -jnp.where(mask, *m[q])
                    grads[q] = dS[q] * row(W_, q) + dsa * row(KK_, q)
                    put(KK_, q, jnp.sum(s_old[q] * dsa, axis=0, keepdims=True))
                    put(B_, q, -jnp.sum(dS[q] * jnp.where(mask, *sa[q]), axis=0, keepdims=True))
                    dvt_scr[q * ng + g, 0:RW_HD, i:i + 1] = dv[q][0]
                    dvt_scr[q * ng + g, RW_HD:, i:i + 1] = dv[q][1]
            return tuple(grads)

        grads = lax.fori_loop(0, ng, group, tuple(ds_scr[q] for q in range(NP)))
        for q in range(NP):
            ds_scr[q] = grads[q]
            for a, ref in enumerate((dr_ref, dw_ref, dk_ref, dkk_ref, db_ref)):
                ref[:, 128 * q:128 * (q + 1)] = out_scr[a * NP + q]
            for g in range(ng):
                dv_ref[g * RW_G:(g + 1) * RW_G, 128 * q:128 * (q + 1)] = dvt_scr[q * ng + g].T[0:RW_G, :]
        if nc:
            pl.when((pl.program_id(0) == npair - 1) & (step == nb - 1))(finish)

    blk = lambda cb: pl.BlockSpec((RW_TB, 128 * NP), lambda p, i, cb=cb: (nb - 1 - i, cb + p))
    st_spec = pl.BlockSpec((NP, RW_TB, RW_HD, 128), lambda p, i: (p, nb - 1 - i, 0, 0))
    sp_spec = pl.BlockSpec((NP, 1, RW_HD, 128), lambda p, i: (p, jnp.maximum((nb - 1 - i) * RW_TB - 1, 0), 0, 0))
    out = jax.ShapeDtypeStruct((T, RW_W), F32)
    tiles = pltpu.VMEM((NP * ng, 128, 128), F32)
    return pl.pallas_call(
        body, name="rw_core_bwd", grid=(npair, nb),
        in_specs=[blk(0), blk(2 * RW_W // (128 * NP)), blk(0), blk(0), blk(0), blk(0), blk(0), st_spec, sp_spec]
        + [_ANY] * nc,
        out_specs=[blk(0)] * 6 + [_ANY] * nc,
        out_shape=[out] * 6 + [jax.ShapeDtypeStruct(a.shape, a.dtype) for a in exchange],
        scratch_shapes=[pltpu.VMEM((NP, RW_HD, 128), F32), tiles, tiles, tiles,
                        pltpu.VMEM((5 * NP, RW_TB, 128), F32), pltpu.VMEM((5 * NP, RW_TB, 128), F32)] + _comm_sems(nc),
        compiler_params=_params(("arbitrary", "arbitrary")),
    )(rw, rw, w, k2, kk, b, dy, states, states, *exchange)


GLA_CB = 8


def _gla_chunk(s_t, q, k, v, la, ltri):
    cum = jnp.dot(ltri, la, precision=HI, preferred_element_type=F32)
    total = jnp.sum(la, axis=0, keepdims=True)
    kdec = k * jnp.exp(total - cum)
    u_t = lax.dot_general(v, kdec, (((0,), (0,)), ((), ())), precision=HI, preferred_element_type=F32)
    s_t = jnp.exp(total) * s_t + u_t
    o = lax.dot_general(q * (GLA_DK ** -0.5), s_t, (((1,), (1,)), ((), ())), precision=HI, preferred_element_type=F32)
    return s_t, o


def _gla_core_fwd(pa, la, ltri):
    T = pa.shape[0]
    cb = min(GLA_CB, T // CHUNK)
    rows = cb * CHUNK
    nsteps = T // rows

    def body(q_ref, k_ref, v_ref, la_ref, ltri_ref, o_ref, st_ref, s_scr):
        @pl.when(pl.program_id(1) == 0)
        def _():
            s_scr[...] = jnp.zeros_like(s_scr)

        def chunk(c, s_t):
            sl = pl.ds(pl.multiple_of(c * CHUNK, CHUNK), CHUNK)
            s_t, o = _gla_chunk(s_t, q_ref[sl, :], k_ref[sl, :], v_ref[sl, :], la_ref[sl, :], ltri_ref[...])
            o_ref[sl, :] = o
            st_ref[0, c] = s_t
            return s_t

        s_scr[...] = lax.fori_loop(0, cb, chunk, s_scr[...])

    qk = lambda off: pl.BlockSpec((rows, GLA_DK), lambda h, i, off=off: (i, off + h))
    vspec = pl.BlockSpec((rows, GLA_DV), lambda h, i: (i, 2 * GLA_QK // GLA_DV + h))
    return pl.pallas_call(
        body, name="gla_core_fwd", grid=(GLA_HEADS, nsteps),
        in_specs=[qk(0), qk(GLA_HEADS), vspec, qk(0), pl.BlockSpec((CHUNK, CHUNK), lambda h, i: (0, 0))],
        out_specs=[pl.BlockSpec((rows, GLA_DV), lambda h, i: (i, h)),
                   pl.BlockSpec((1, cb, GLA_DV, GLA_DK), lambda h, i: (h, i, 0, 0))],
        out_shape=(jax.ShapeDtypeStruct((T, GLA_V), F32),
                   jax.ShapeDtypeStruct((GLA_HEADS, T // CHUNK, GLA_DV, GLA_DK), F32)),
        scratch_shapes=[pltpu.VMEM((GLA_DV, GLA_DK), F32)],
        compiler_params=_params(("arbitrary", "arbitrary")),
    )(pa, pa, pa, la, ltri)


def _gla_core_bwd(pa, la, ltri, states, do):
    T = pa.shape[0]
    cb = min(GLA_CB, T // CHUNK)
    rows = cb * CHUNK
    nsteps = T // rows

    def body(q_ref, k_ref, v_ref, la_ref, ltri_ref, st_ref, sp_ref, do_ref,
             dq_ref, dk_ref, dv_ref, dla_ref, ds_scr):
        step = pl.program_id(1)

        @pl.when(step == 0)
        def _():
            ds_scr[...] = jnp.zeros_like(ds_scr)

        s_before = jnp.where(step == nsteps - 1, 0.0, sp_ref[0, 0])

        def chunk(cc, ds_t):
            c = cb - 1 - cc
            sl = pl.ds(pl.multiple_of(c * CHUNK, CHUNK), CHUNK)
            s_prev = jnp.where(c == 0, s_before, st_ref[0, jnp.maximum(c - 1, 0)])
            _, vjp = jax.vjp(functools.partial(_gla_chunk, ltri=ltri_ref[...]),
                             s_prev, q_ref[sl, :], k_ref[sl, :], v_ref[sl, :], la_ref[sl, :])
            ds_prev, dq, dk, dv, dla = vjp((ds_t, do_ref[sl, :]))
            dq_ref[sl, :] = dq
            dk_ref[sl, :] = dk
            dv_ref[sl, :] = dv
            dla_ref[sl, :] = dla
            return ds_prev

        ds_scr[...] = lax.fori_loop(0, cb, chunk, ds_scr[...])

    r = lambda i: nsteps - 1 - i
    qk = lambda off: pl.BlockSpec((rows, GLA_DK), lambda h, i, off=off: (r(i), off + h))
    vspec = pl.BlockSpec((rows, GLA_DV), lambda h, i: (r(i), 2 * GLA_QK // GLA_DV + h))
    o128 = pl.BlockSpec((rows, GLA_DK), lambda h, i: (r(i), h))
    o256 = pl.BlockSpec((rows, GLA_DV), lambda h, i: (r(i), h))
    return pl.pallas_call(
        body, name="gla_core_bwd", grid=(GLA_HEADS, nsteps),
        in_specs=[qk(0), qk(GLA_HEADS), vspec, qk(0), pl.BlockSpec((CHUNK, CHUNK), lambda h, i: (0, 0)),
                  pl.BlockSpec((1, cb, GLA_DV, GLA_DK), lambda h, i: (h, r(i), 0, 0)),
                  pl.BlockSpec((1, 1, GLA_DV, GLA_DK), lambda h, i: (h, jnp.maximum(r(i) * cb - 1, 0), 0, 0)),
                  o256],
        out_specs=[o128, o128, o256, o128],
        out_shape=(jax.ShapeDtypeStruct((T, GLA_QK), F32), jax.ShapeDtypeStruct((T, GLA_QK), F32),
                   jax.ShapeDtypeStruct((T, GLA_V), F32), jax.ShapeDtypeStruct((T, GLA_QK), F32)),
        scratch_shapes=[pltpu.VMEM((GLA_DV, GLA_DK), F32)],
        compiler_params=_params(("arbitrary", "arbitrary")),
    )(pa, pa, pa, la, ltri, states, states, do)


def _rowwise(fn, name, T, tm, rows, pars, row_outs, acc_outs):
    nr, npar, nro = len(rows), len(pars), len(row_outs)
    tm = min(tm, T)
    nsteps = T // tm

    def body(*refs):
        i = pl.program_id(0)
        ins = [r[...] for r in refs[:nr + npar]]
        outs, accs = fn(i, *ins)
        for r, o in zip(refs[nr + npar:nr + npar + nro], outs):
            r[...] = o.astype(r.dtype)
        for r, a in zip(refs[nr + npar + nro:], accs):
            @pl.when(i == 0)
            def _(r=r, a=a):
                r[...] = a

            @pl.when(i > 0)
            def _(r=r, a=a):
                r[...] += a

    def rspec(width, cb, kind):
        if kind == "cur":
            return pl.BlockSpec((tm, width), lambda i: (i, cb))
        if kind == "prev":
            return pl.BlockSpec((8, width), lambda i: (jnp.maximum(i * (tm // 8) - 1, 0), cb))
        return pl.BlockSpec((8, width), lambda i: (jnp.minimum((i + 1) * (tm // 8), T // 8 - 1), cb))

    in_specs = [rspec(w, cb, kind) for (_, w, cb, kind) in rows]
    in_specs += [pl.BlockSpec(p.shape, lambda i, nd=p.ndim: (0,) * nd) for p in pars]
    out_specs = [pl.BlockSpec((tm, w), lambda i: (i, 0)) for (w, _) in row_outs]
    out_specs += [pl.BlockSpec(s, lambda i, nd=len(s): (0,) * nd) for s in acc_outs]
    out_shape = [jax.ShapeDtypeStruct((T, w), dt) for (w, dt) in row_outs]
    out_shape += [jax.ShapeDtypeStruct(s, F32) for s in acc_outs]
    res = pl.pallas_call(
        body, name=name, grid=(nsteps,), in_specs=in_specs, out_specs=out_specs, out_shape=out_shape,
        compiler_params=_params(("arbitrary",)),
    )(*[r[0] for r in rows], *pars)
    return res


def _cur(a, width=None, cb=0):
    return (a, a.shape[1] if width is None else width, cb, "cur")


def _sigmoid(x):
    return 1.0 / (1.0 + jnp.exp(-x))


def _silu(x):
    return x * _sigmoid(x)


def _softplus(x):
    return jnp.maximum(x, 0.0) + jnp.log(1.0 + jnp.exp(-jnp.abs(x)))


def _rms(x, g):
    return x * lax.rsqrt(jnp.mean(x * x, axis=-1, keepdims=True) + NORM_EPS) * g


def _dot_hi(a, b):
    return jnp.dot(a, b, precision=HI, preferred_element_type=F32)


def _rms_fwd(x, g, name):
    T = x.shape[0]
    fn = lambda i, xb, gb: ((_rms(xb, gb),), ())
    return _rowwise(fn, name, T, 256, [_cur(x)], [g], [(D, BF16)], [])[0]


def _rms_bwd(x, g, dh, dres, name):
    T = x.shape[0]

    def fn(i, xb, dhb, drb, gb):
        _, vjp = jax.vjp(_rms, xb, gb)
        dx, dg = vjp(dhb)
        return (drb + dx,), (dg,)

    return _rowwise(fn, name, T, 256, [_cur(x), _cur(dh), _cur(dres)], [g], [(D, F32)], [(1, D)])


def _loss_bwd(x, target, g):
    T = x.shape[0]

    def loss(xb, gb, tb):
        err = _rms(xb, gb) - tb
        return 0.5 * jnp.sum(jnp.mean(err * err, axis=-1, keepdims=True))

    def fn(i, xb, tb, gb):
        val, (dx, dg) = jax.value_and_grad(loss, argnums=(0, 1))(xb, gb, tb)
        return (dx,), (jnp.full((1, 128), val, F32), dg)

    return _rowwise(fn, "loss_bwd", T, 256, [_cur(x), _cur(target)], [g], [(D, F32)], [(1, 128), (1, D)])


def _gla_la(a_down, w_a2, b_a):
    return -_softplus(-(_dot_hi(a_down, w_a2) + b_a)) * (1.0 / GLA_TAU)


def _gla_prep(pa, w_a2, b_a):
    T = pa.shape[0]
    fn = lambda i, ab, wb, bb: ((_gla_la(ab, wb, bb),), ())
    return _rowwise(fn, "gla_prep", T, 512, [_cur(pa, LORA_P, (2 * GLA_QK + 2 * GLA_V) // LORA_P)], [w_a2, b_a],
                    [(GLA_QK, F32)], [])[0]


def _gla_prep_bwd(pa, w_a2, b_a, dla):
    T = pa.shape[0]

    def fn(i, ab, dlab, wb, bb):
        _, vjp = jax.vjp(_gla_la, ab, wb, bb)
        da, dw, db = vjp(dlab)
        return (da,), (dw, db)

    return _rowwise(fn, "gla_prep_bwd", T, 512, [_cur(pa, LORA_P, (2 * GLA_QK + 2 * GLA_V) // LORA_P), _cur(dla)],
                    [w_a2, b_a], [(LORA_P, BF16)], [(LORA_P, GLA_QK), (1, GLA_QK)])


def _gla_out(o, r, gn, ind, ind_t):
    ms = _dot_hi(_dot_hi(o * o, ind) * (1.0 / GLA_DV), ind_t)
    return o * lax.rsqrt(ms + NORM_EPS) * gn * _silu(r)


def _gla_post(o_raw, pa, gn, ind, ind_t):
    T = pa.shape[0]
    fn = lambda i, ob, rb, gb, a, b: ((_gla_out(ob, rb, gb, a, b),), ())
    return _rowwise(fn, "gla_post", T, 256, [_cur(o_raw), _cur(pa, GLA_V, 2)], [gn, ind, ind_t], [(GLA_V, BF16)], [])[0]


def _gla_post_bwd(o_raw, pa, gn, ind, ind_t, do):
    T = pa.shape[0]

    def fn(i, ob, rb, dob, gb, a, b):
        _, vjp = jax.vjp(lambda o, r, g: _gla_out(o, r, g, a, b), ob, rb, gb)
        d_o, d_r, d_g = vjp(dob)
        return (d_o, d_r), (d_g,)

    return _rowwise(fn, "gla_post_bwd", T, 256, [_cur(o_raw), _cur(pa, GLA_V, 2), _cur(do)], [gn, ind, ind_t],
                    [(GLA_V, F32), (GLA_V, BF16)], [(1, GLA_V)])


def _shift_rows(cur, prev8, i):
    first = jnp.where(i == 0, 0.0, prev8[7:8, :])
    rolled = pltpu.roll(cur, 1, 0)
    return jnp.where(lax.broadcasted_iota(jnp.int32, cur.shape, 0) == 0, first, rolled)


def _rw_gates(rw, w0, w_w2, a0, w_a2, w_g2, k_k, k_a, ind, ind_t):
    rk = rw[:, RW_W:2 * RW_W]
    wd = rw[:, 3 * RW_W:3 * RW_W + LORA_P]
    ad = rw[:, 3 * RW_W + LORA_P:3 * RW_W + 2 * LORA_P]
    gd = rw[:, 3 * RW_W + 2 * LORA_P:]
    w_raw = w0 + _dot_hi(jnp.tanh(wd), w_w2)
    w = jnp.exp(-jnp.exp(-_softplus(-w_raw) - 0.5))
    a = _sigmoid(a0 + _dot_hi(ad, w_a2))
    g = _dot_hi(_sigmoid(gd), w_g2)
    kk = rk * k_k
    kk = kk * _dot_hi(lax.rsqrt(jnp.maximum(_dot_hi(kk * kk, ind), 1e-24)), ind_t)
    k2 = rk * (1.0 + (a - 1.0) * k_a)
    return w, k2, kk, kk * a, g


def _rw_prep(pr, mu, gate_pars):
    T = pr.shape[0]

    def fn(i, cur, prev8, mub, *gp):
        rw = cur + mub * (_shift_rows(cur, prev8, i) - cur)
        return (rw,) + _rw_gates(rw, *gp), ()

    return _rowwise(fn, "rw_prep", T, 256, [_cur(pr), (pr, PR_W, 0, "prev")], [mu, *gate_pars],
                    [(PR_W, F32)] + [(RW_W, F32)] * 5, [])


def _rw_prep_bwd(pr, mu, gate_pars, d_r, d_v, d_w, d_k2, d_kk, d_b, d_g):
    T = pr.shape[0]
    rows = [_cur(pr), (pr, PR_W, 0, "prev")] + [_cur(x) for x in (*d_r, *d_v, d_w, *d_k2, d_kk, d_b, d_g)]
    acc = [(1, PR_W)] + [tuple(p.shape) for p in gate_pars[:-2]]

    def fn(i, cur, prev8, dr1, dr2, dv1, dv2, dw, dk1, dk2, dkk, db, dg, mub, *gp):
        sh = _shift_rows(cur, prev8, i)
        rw = cur + mub * (sh - cur)
        _, vjp = jax.vjp(lambda x, *p: _rw_gates(x, *p, gp[-2], gp[-1]), rw, *gp[:-2])
        grads = vjp((dw, dk1 + dk2, dkk, db, dg))
        zeros = jnp.zeros((cur.shape[0], PR_W - 3 * RW_W), F32)
        drw = grads[0] + jnp.concatenate([dr1 + dr2, jnp.zeros_like(dr1), dv1 + dv2, zeros], axis=1)
        dmu = jnp.sum(drw * (sh - cur), axis=0, keepdims=True)
        return (drw,), (dmu, *grads[1:])

    return _rowwise(fn, "rw_prep_bwd", T, 128, rows, [mu, *gate_pars], [(PR_W, F32)], acc)


def _shift_bwd(drw, mu):
    T = drw.shape[0]
    tm = min(256, T)

    def fn(i, cur, next8, mub):
        last = jnp.where(i == T // tm - 1, 0.0, next8[0:1, :])
        rolled = pltpu.roll(cur, cur.shape[0] - 1, 0)
        nxt = jnp.where(lax.broadcasted_iota(jnp.int32, cur.shape, 0) == cur.shape[0] - 1, last, rolled)
        return ((1.0 - mub) * cur + mub * nxt,), ()

    return _rowwise(fn, "shift_bwd", T, tm, [_cur(drw), (drw, PR_W, 0, "next")], [mu], [(PR_W, BF16)], [])[0]


def _rw_out(y, r, v, k2, g, lnx_w, lnx_b, r_k, ind, ind_t):
    mean = _dot_hi(_dot_hi(y, ind) * (1.0 / RW_HD), ind_t)
    yc = y - mean
    var = _dot_hi(_dot_hi(yc * yc, ind) * (1.0 / RW_HD), ind_t)
    yn = yc * lax.rsqrt(var + GN_EPS) * lnx_w + lnx_b
    bonus = _dot_hi(_dot_hi(r * k2 * r_k, ind), ind_t) * v
    return (yn + bonus) * g


def _rw_post(y, rw, k2, g, pars):
    T = y.shape[0]
    fn = lambda i, *a: ((_rw_out(*a),), ())
    return _rowwise(fn, "rw_post", T, 256, [_cur(y), _cur(rw, RW_W, 0), _cur(rw, RW_W, 2), _cur(k2), _cur(g)], pars,
                    [(RW_W, BF16)], [])[0]


def _rw_post_bwd(y, rw, k2, g, pars, do):
    T = y.shape[0]

    def fn(i, yb, rb, vb, kb, gb, dob, lw, lb, rk, ind, ind_t):
        _, vjp = jax.vjp(lambda *a: _rw_out(*a, ind, ind_t), yb, rb, vb, kb, gb, lw, lb, rk)
        gr = vjp(dob)
        return gr[:5], gr[5:]

    return _rowwise(fn, "rw_post_bwd", T, 256,
                    [_cur(y), _cur(rw, RW_W, 0), _cur(rw, RW_W, 2), _cur(k2), _cur(g), _cur(do)], pars,
                    [(RW_W, F32)] * 5, [(1, RW_W)] * 3)


def _merge_bwd(dm, y_gla, y_rw, pg, gate_b):
    T = dm.shape[0]

    def fn(i, dmb, ya, yr, p1, p2, gb):
        g1 = _sigmoid(p1 + gb[:, :D])
        g2 = _sigmoid(p2 + gb[:, D:])
        dp1 = dmb * ya * g1 * (1.0 - g1)
        dp2 = dmb * yr * g2 * (1.0 - g2)
        dp = jnp.concatenate([dp1, dp2], axis=1)
        return (dmb * g1, dmb * g2, dp), (jnp.sum(dp, axis=0, keepdims=True),)

    return _rowwise(fn, "merge_bwd", T, 256, [_cur(dm), _cur(y_gla), _cur(y_rw), _cur(pg, D, 0), _cur(pg, D, 1)],
                    [gate_b], [(D, BF16), (D, BF16), (PG_W, BF16)], [(1, PG_W)])


_NN = (((1,), (0,)), ((), ()))
_NT = (((1,), (1,)), ((), ()))
_TN = (((0,), (0,)), ((), ()))


def _bdot(a, b, dims):
    return lax.dot_general(a.astype(BF16), b.astype(BF16), dims, preferred_element_type=F32)


def _accumulate(k, nk, acc, part, finish):
    if nk == 1:
        finish(part)
        return

    @pl.when(k == 0)
    def _():
        acc[...] = part

    @pl.when(k > 0)
    def _():
        acc[...] += part

    @pl.when(k == nk - 1)
    def _():
        finish(acc[...])


def _matmul(a, b, mode, M, N, K, tm, tn, tk, name, a_off=(0, 0), b_off=(0, 0), res=None, scale=1.0, out_dtype=F32):
    tm, tn, tk = min(tm, M), min(tn, N), min(tk, K)
    nk = K // tk
    if mode == "nn":
        a_spec = pl.BlockSpec((tm, tk), lambda i, j, k: (i + a_off[0], k + a_off[1]))
        b_spec = pl.BlockSpec((tk, tn), lambda i, j, k: (k + b_off[0], j + b_off[1]))
        dims = _NN
    elif mode == "nt":
        a_spec = pl.BlockSpec((tm, tk), lambda i, j, k: (i + a_off[0], k + a_off[1]))
        b_spec = pl.BlockSpec((tn, tk), lambda i, j, k: (j + b_off[0], k + b_off[1]))
        dims = _NT
    else:
        a_spec = pl.BlockSpec((tk, tm), lambda i, j, k: (k + a_off[0], i + a_off[1]))
        b_spec = pl.BlockSpec((tk, tn), lambda i, j, k: (k + b_off[0], j + b_off[1]))
        dims = _TN
    o_spec = pl.BlockSpec((tm, tn), lambda i, j, k: (i, j))

    def body(a_ref, b_ref, *rest):
        r_ref = rest[0] if res is not None else None
        o_ref = rest[1] if res is not None else rest[0]
        acc = rest[-1] if nk > 1 else None

        def finish(total):
            total = total * scale if scale != 1.0 else total
            if r_ref is not None:
                total = r_ref[...] + total
            o_ref[...] = total.astype(out_dtype)

        _accumulate(pl.program_id(2), nk, acc, _bdot(a_ref[...], b_ref[...], dims), finish)

    return pl.pallas_call(
        body, name=name, grid=(M // tm, N // tn, nk),
        in_specs=[a_spec, b_spec] + ([o_spec] if res is not None else []),
        out_specs=o_spec, out_shape=jax.ShapeDtypeStruct((M, N), out_dtype),
        scratch_shapes=[pltpu.VMEM((tm, tn), F32)] if nk > 1 else [],
        compiler_params=_params(("parallel", "parallel", "arbitrary")),
    )(a, b, *([res] if res is not None else []))


def _ffn_up(h, wg, wu, name):
    T = h.shape[0]
    tm = min(1024, T)

    def body(h_ref, wg_ref, wu_ref, a_ref, u_ref, s_ref):
        hb = h_ref[...]
        a = _bdot(hb, wg_ref[...], _NN)
        u = _bdot(hb, wu_ref[...], _NN)
        a_ref[...] = a
        u_ref[...] = u
        s_ref[...] = (_silu(a) * u).astype(BF16)

    w_spec = pl.BlockSpec((None, D, FSH), lambda i, j: (j, 0, 0))
    o_spec = pl.BlockSpec((None, tm, FSH), lambda i, j: (j, i, 0))
    sh = lambda dt: jax.ShapeDtypeStruct((NDEV, T, FSH), dt)
    return pl.pallas_call(
        body, name=name, grid=(T // tm, NDEV),
        in_specs=[pl.BlockSpec((tm, D), lambda i, j: (i, 0)), w_spec, w_spec],
        out_specs=[o_spec] * 3, out_shape=(sh(F32), sh(F32), sh(BF16)),
        compiler_params=_params(("parallel", "arbitrary")),
    )(h, wg, wu)


def _ffn_down(s, wd, x, name):
    T = x.shape[0]
    tm, tn = min(1024, T), 1024

    def body(s_ref, wd_ref, x_ref, o_ref, acc):
        def finish(total):
            o_ref[...] = x_ref[...] + 0.5 * total

        _accumulate(pl.program_id(2), NDEV, acc, _bdot(s_ref[...], wd_ref[...], _NN), finish)

    xo = pl.BlockSpec((tm, tn), lambda i, n, j: (i, n))
    return pl.pallas_call(
        body, name=name, grid=(T // tm, D // tn, NDEV),
        in_specs=[pl.BlockSpec((None, tm, FSH), lambda i, n, j: (j, i, 0)),
                  pl.BlockSpec((None, FSH, tn), lambda i, n, j: (j, 0, n)), xo],
        out_specs=xo, out_shape=jax.ShapeDtypeStruct((T, D), F32),
        scratch_shapes=[pltpu.VMEM((tm, tn), F32)],
        compiler_params=_params(("parallel", "parallel", "arbitrary")),
    )(s, wd, x)


def _ffn_bwd_hidden(dx, wd, a, u, name):
    T = dx.shape[0]
    tm = min(1024, T)

    def body(dx_ref, wd_ref, a_ref, u_ref, da_ref, du_ref):
        ds = 0.5 * _bdot(dx_ref[...], wd_ref[...], _NT)
        av = a_ref[...]
        sg = _sigmoid(av)
        da_ref[...] = (ds * u_ref[...] * (sg * (1.0 + av * (1.0 - sg)))).astype(BF16)
        du_ref[...] = (ds * (av * sg)).astype(BF16)

    act = pl.BlockSpec((None, tm, FSH), lambda i, j: (j, i, 0))
    sh = jax.ShapeDtypeStruct((NDEV, T, FSH), BF16)
    return pl.pallas_call(
        body, name=name, grid=(T // tm, NDEV),
        in_specs=[pl.BlockSpec((tm, D), lambda i, j: (i, 0)), pl.BlockSpec((None, FSH, D), lambda i, j: (j, 0, 0)),
                  act, act],
        out_specs=[act, act], out_shape=(sh, sh),
        compiler_params=_params(("parallel", "arbitrary")),
    )(dx, wd, a, u)


def _ffn_bwd_input(da, du, wg, wu, name):
    T = da.shape[1]
    tm, tn = min(1024, T), 1024

    def body(da_ref, du_ref, wg_ref, wu_ref, o_ref, acc):
        part = _bdot(da_ref[...], wg_ref[...], _NT) + _bdot(du_ref[...], wu_ref[...], _NT)

        def finish(total):
            o_ref[...] = total

        _accumulate(pl.program_id(2), NDEV, acc, part, finish)

    act = pl.BlockSpec((None, tm, FSH), lambda i, n, j: (j, i, 0))
    wsp = pl.BlockSpec((None, tn, FSH), lambda i, n, j: (j, n, 0))
    return pl.pallas_call(
        body, name=name, grid=(T // tm, D // tn, NDEV),
        in_specs=[act, act, wsp, wsp],
        out_specs=pl.BlockSpec((tm, tn), lambda i, n, j: (i, n)), out_shape=jax.ShapeDtypeStruct((T, D), F32),
        scratch_shapes=[pltpu.VMEM((tm, tn), F32)],
        compiler_params=_params(("parallel", "parallel", "arbitrary")),
    )(da, du, wg, wu)


def _ffn_grad_up(h, da, name):
    T = h.shape[0]
    tm, tk = 1024, min(1024, T)
    nk = T // tk

    def body(h_ref, da_ref, o_ref, acc):
        def finish(total):
            o_ref[...] = total

        _accumulate(pl.program_id(2), nk, acc, _bdot(h_ref[...], da_ref[...], _TN), finish)

    return pl.pallas_call(
        body, name=name, grid=(NDEV, D // tm, nk),
        in_specs=[pl.BlockSpec((tk, tm), lambda j, i, t: (t, i)), pl.BlockSpec((None, tk, FSH), lambda j, i, t: (j, t, 0))],
        out_specs=pl.BlockSpec((None, tm, FSH), lambda j, i, t: (j, i, 0)),
        out_shape=jax.ShapeDtypeStruct((NDEV, D, FSH), F32),
        scratch_shapes=[pltpu.VMEM((tm, FSH), F32)],
        compiler_params=_params(("parallel", "parallel", "arbitrary")),
    )(h, da)


def _ffn_grad_down(s, dx, name):
    T = dx.shape[0]
    tn, tk = 1024, min(1024, T)
    nk = T // tk

    def body(s_ref, dx_ref, o_ref, acc):
        def finish(total):
            o_ref[...] = 0.5 * total

        _accumulate(pl.program_id(2), nk, acc, _bdot(s_ref[...], dx_ref[...], _TN), finish)

    return pl.pallas_call(
        body, name=name, grid=(NDEV, D // tn, nk),
        in_specs=[pl.BlockSpec((None, tk, FSH), lambda j, n, t: (j, t, 0)), pl.BlockSpec((tk, tn), lambda j, n, t: (t, n))],
        out_specs=pl.BlockSpec((None, FSH, tn), lambda j, n, t: (j, 0, n)),
        out_shape=jax.ShapeDtypeStruct((NDEV, DFF // NDEV, D), F32),
        scratch_shapes=[pltpu.VMEM((FSH, tn), F32)],
        compiler_params=_params(("parallel", "parallel", "arbitrary")),
    )(s, dx)


def _branch_merge(o_gla, o_rw, wb, pg, gate_b):
    T = o_gla.shape[0]
    tm, tn = min(1024, T), 512

    def body(og_ref, or_ref, w1_ref, w2_ref, p1_ref, p2_ref, b1_ref, b2_ref, yg_ref, yr_ref, m_ref):
        yg = _bdot(og_ref[...], w1_ref[...], _NN)
        yr = _bdot(or_ref[...], w2_ref[...], _NN)
        yg_ref[...] = yg
        yr_ref[...] = yr
        m_ref[...] = (_sigmoid(p1_ref[...] + b1_ref[...]) * yg + _sigmoid(p2_ref[...] + b2_ref[...]) * yr).astype(BF16)

    nj = D // tn
    act = pl.BlockSpec((tm, GLA_V), lambda i, j: (i, 0))
    out = pl.BlockSpec((tm, tn), lambda i, j: (i, j))
    return pl.pallas_call(
        body, name="branch_merge", grid=(T // tm, nj),
        in_specs=[act, act, pl.BlockSpec((GLA_V, tn), lambda i, j: (0, j)), pl.BlockSpec((RW_W, tn), lambda i, j: (1, j)),
                  out, pl.BlockSpec((tm, tn), lambda i, j: (i, nj + j)),
                  pl.BlockSpec((1, tn), lambda i, j: (0, j)), pl.BlockSpec((1, tn), lambda i, j: (0, nj + j))],
        out_specs=[out, out, out],
        out_shape=(jax.ShapeDtypeStruct((T, D), F32), jax.ShapeDtypeStruct((T, D), F32), jax.ShapeDtypeStruct((T, D), BF16)),
        compiler_params=_params(("parallel", "arbitrary")),
    )(o_gla, o_rw, wb, wb, pg, pg, gate_b, gate_b)


def _head_indicator(width, heads):
    col = lax.broadcasted_iota(jnp.int32, (width, 128), 0) // (width // heads)
    ind = (col == lax.broadcasted_iota(jnp.int32, (width, 128), 1)).astype(F32)
    return ind, ind.T


def _ffn_fwd(x, g, wg, wu, wd, tag):
    h = _rms_fwd(x, g, "rms_" + tag)
    a, u, s = _ffn_up(h, wg, wu, "ffn_up_" + tag)
    return _ffn_down(s, wd, x, "ffn_down_" + tag), (h, a, u, s)


def _ffn_bwd(dy, x, g, wg, wu, wd, saved, tag):
    h, a, u, s = saved
    da, du = _ffn_bwd_hidden(dy, wd, a, u, "ffn_bwd_hidden_" + tag)
    dh = _ffn_bwd_input(da, du, wg, wu, "ffn_bwd_input_" + tag)
    dwg = _ffn_grad_up(h, da, "ffn_grad_gate_" + tag)
    dwu = _ffn_grad_up(h, du, "ffn_grad_up_" + tag)
    dwd = _ffn_grad_down(s, dy, "ffn_grad_down_" + tag)
    dx, dg = _rms_bwd(x, g, dh, dy, "rms_bwd_" + tag)
    return dx, dg, dwg, dwu, dwd


def _local_step(x, target, w, late):
    T = x.shape[0]
    ind16, ind16_t = _head_indicator(RW_W, RW_HEADS)
    ind4, ind4_t = _head_indicator(GLA_V, GLA_HEADS)
    ltri = jnp.tril(jnp.ones((CHUNK, CHUNK), F32))
    gate_pars = [w["w0"], w["w_w2"], w["a0"], w["w_a2"], w["w_g2"], w["k_k"], w["k_a"], ind16, ind16_t]
    post_pars = [w["lnx_w"], w["lnx_b"], w["r_k"], ind16, ind16_t]

    x1, ffn1 = _ffn_fwd(x, w["g1"], w["wg1"], w["wu1"], w["wd1"], "1")
    h2 = _rms_fwd(x1, w["g2"], "rms_mix")
    proj = lambda n, off, name: _matmul(h2, w["win"], "nn", T, n, D, 1024, 512, D, name, b_off=(0, off // 512))
    pg = proj(PG_W, 0, "proj_gate")
    pr = proj(PR_W, PG_W, "proj_rwkv")
    pa = proj(PA_W, PG_W + PR_W, "proj_gla")
    la = _gla_prep(pa, w["gla_w_a2"], w["gla_b_a"])
    o_raw, gla_states = _gla_core_fwd(pa, la, ltri)
    o_gla = _gla_post(o_raw, pa, w["gn"], ind4, ind4_t)
    rw, dec, k2, kk, b, g = _rw_prep(pr, w["mu"], gate_pars)
    y, rw_states, g_up2, g_down2 = _rw_core_fwd(rw, dec, k2, kk, b, gather=late)
    w = {**w, **_late_weights(g_up2, g_down2)}
    o_rw = _rw_post(y, rw, k2, g, post_pars)
    y_gla, y_rw, merged = _branch_merge(o_gla, o_rw, w["wb"], pg, w["gate_b"])
    x2 = _matmul(merged, w["wo"], "nn", T, D, D, 1024, 1024, D, "out_proj", res=x1)
    x3, ffn2 = _ffn_fwd(x2, w["g3"], w["wg2"], w["wu2"], w["wd2"], "2")
    dx3, loss, d_gf = _loss_bwd(x3, target, w["gf"])

    grads = {"gf": d_gf}
    dx2, grads["g3"], grads["wg2"], grads["wu2"], grads["wd2"] = _ffn_bwd(
        dx3, x2, w["g3"], w["wg2"], w["wu2"], w["wd2"], ffn2, "2")
    dm = _matmul(dx2, w["wo"], "nt", T, D, D, 1024, 1024, D, "out_proj_bwd")
    grads["wo"] = _matmul(merged, dx2, "tn", D, D, T, 1024, 1024, 1024, "out_proj_grad")
    dy_gla, dy_rw, dpg, grads["gate_b"] = _merge_bwd(dm, y_gla, y_rw, pg, w["gate_b"])
    do_gla = _matmul(dy_gla, w["wb"], "nt", T, GLA_V, D, 1024, 1024, D, "branch_gla_bwd")
    do_rw = _matmul(dy_rw, w["wb"], "nt", T, RW_W, D, 1024, 1024, D, "branch_rwkv_bwd", b_off=(1, 0))
    grads["wb"] = jnp.concatenate([
        _matmul(o_gla, dy_gla, "tn", GLA_V, D, T, 1024, 1024, 1024, "branch_gla_grad"),
        _matmul(o_rw, dy_rw, "tn", RW_W, D, T, 1024, 1024, 1024, "branch_rwkv_grad")], axis=0)
    dy, dr2, dv2, dk2b, dg, grads["lnx_w"], grads["lnx_b"], grads["r_k"] = _rw_post_bwd(y, rw, k2, g, post_pars, do_rw)
    early = _late_grad_parts(grads)
    dr1, dw, dk2a, dv1, dkk, db, *received = _rw_core_bwd(rw, dec, k2, kk, b, rw_states, dy, exchange=early)
    drw, grads["mu"], grads["w0"], grads["w_w2"], grads["a0"], grads["w_a2"], grads["w_g2"], grads["k_k"], grads["k_a"] = (
        _rw_prep_bwd(pr, w["mu"], gate_pars, (dr1, dr2), (dv1, dv2), dw, (dk2a, dk2b), dkk, db, dg))
    dpr = _shift_bwd(drw, w["mu"])
    do_raw, dr_gla, grads["gn"] = _gla_post_bwd(o_raw, pa, w["gn"], ind4, ind4_t, do_gla)
    dq, dk, dv, dla = _gla_core_bwd(pa, la, ltri, gla_states, do_raw)
    da_down, grads["gla_w_a2"], grads["gla_b_a"] = _gla_prep_bwd(pa, w["gla_w_a2"], w["gla_b_a"], dla)
    dpa = jnp.concatenate([dq.astype(BF16), dk.astype(BF16), dv.astype(BF16), dr_gla, da_down,
                           jnp.zeros((T, PA_W - PA_USED), BF16)], axis=1)
    dp = jnp.concatenate([dpg, dpr, dpa], axis=1)
    dh2 = _matmul(dp, w["win"], "nt", T, D, DIN_P, 1024, 1024, 1024, "proj_bwd")
    grads["win"] = _matmul(h2, dp, "tn", D, DIN_P, T, 1024, 1024, 1024, "proj_grad")
    dx1, grads["g2"] = _rms_bwd(x1, w["g2"], dh2, dx2, "rms_bwd_mix")
    dx, grads["g1"], grads["wg1"], grads["wu1"], grads["wd1"] = _ffn_bwd(
        dx1, x, w["g1"], w["wg1"], w["wu1"], w["wd1"], ffn1, "1")
    return loss, dx, grads, received


BIG = ("ffn1_wg", "ffn1_wu", "ffn1_wd", "w_in", "w_branch", "w_out", "ffn2_wg", "ffn2_wu", "ffn2_wd")
SMALL_SHARDED = ("gla_w_a2", "rwkv_w_w2", "rwkv_w_a2", "rwkv_w_g2")
REPLICATED = ("ffn1_norm", "mix_norm", "gla_b_a", "gla_gn_w", "rwkv_mu", "rwkv_w0", "rwkv_a0", "rwkv_k_k", "rwkv_k_a",
              "rwkv_r_k", "rwkv_lnx_w", "rwkv_lnx_b", "gate_b", "ffn2_norm", "final_norm")
WEIGHTS = ("ffn1_norm", "ffn1_wg", "ffn1_wu", "ffn1_wd", "mix_norm", "w_in", "gla_w_a2", "gla_b_a", "gla_gn_w",
           "rwkv_mu", "rwkv_w0", "rwkv_w_w2", "rwkv_a0", "rwkv_w_a2", "rwkv_w_g2", "rwkv_k_k", "rwkv_k_a", "rwkv_r_k",
           "rwkv_lnx_w", "rwkv_lnx_b", "gate_b", "w_branch", "w_out", "ffn2_norm", "ffn2_wg", "ffn2_wu", "ffn2_wd",
           "final_norm")


def _unshard_cols(g):
    return jnp.transpose(g, (1, 0, 2)).reshape(g.shape[1], NDEV * g.shape[2])


def _shard_cols(a):
    return jnp.transpose(a.reshape(a.shape[0], NDEV, a.shape[1] // NDEV), (1, 0, 2))


def _pad_rows(a, rows):
    return jnp.pad(a, ((0, rows - a.shape[0]), (0, 0)))


def _align_rw(a):
    c = 3 * RW_W
    z = jnp.zeros((a.shape[0], LORA_P - DECAY_LORA), a.dtype)
    return jnp.concatenate([a[:, :c], a[:, c:c + DECAY_LORA], z, a[:, c + DECAY_LORA:c + 2 * DECAY_LORA], z,
                            a[:, c + 2 * DECAY_LORA:]], axis=1)


def _unalign_rw(a):
    c = 3 * RW_W
    return jnp.concatenate([a[:, :c + DECAY_LORA], a[:, c + LORA_P:c + LORA_P + AAA_LORA], a[:, c + 2 * LORA_P:]], axis=1)


def _align_proj(a):
    gla = jnp.pad(a[:, :GLA_IN], ((0, 0), (0, PA_W - GLA_IN)))
    return jnp.concatenate([a[:, GLA_IN + RW_IN:], _align_rw(a[:, GLA_IN:GLA_IN + RW_IN]), gla], axis=1)


def _unalign_proj(a):
    return jnp.concatenate([a[:, PG_W + PR_W:PG_W + PR_W + GLA_IN], _unalign_rw(a[:, PG_W:PG_W + PR_W]), a[:, :PG_W]], axis=1)


def _layout_weights(gb, gs, rep):
    row = lambda n: rep[n].reshape(1, -1)
    return {
        "wg1": gb["ffn1_wg"], "wu1": gb["ffn1_wu"], "wd1": gb["ffn1_wd"],
        "win": _align_proj(_unshard_cols(gb["w_in"])),
        "g1": row("ffn1_norm"), "g2": row("mix_norm"), "g3": row("ffn2_norm"), "gf": row("final_norm"),
        "gla_w_a2": _pad_rows(_unshard_cols(gs["gla_w_a2"]), LORA_P), "gla_b_a": row("gla_b_a"),
        "gn": jnp.tile(row("gla_gn_w"), (1, GLA_HEADS)),
        "mu": _align_rw(row("rwkv_mu")), "w0": row("rwkv_w0"), "a0": row("rwkv_a0"),
        "w_w2": _pad_rows(_unshard_cols(gs["rwkv_w_w2"]), LORA_P),
        "w_a2": _pad_rows(_unshard_cols(gs["rwkv_w_a2"]), LORA_P),
        "w_g2": _unshard_cols(gs["rwkv_w_g2"]),
        "k_k": row("rwkv_k_k"), "k_a": row("rwkv_k_a"), "r_k": row("rwkv_r_k"),
        "lnx_w": row("rwkv_lnx_w"), "lnx_b": row("rwkv_lnx_b"), "gate_b": row("gate_b"),
    }


LATE_ROWS = (("ffn2_wd", FSH), ("w_branch", (GLA_V + RW_W) // NDEV), ("w_out", D // NDEV))


def _late_weights(g_up, g_down):
    r1, r2 = LATE_ROWS[0][1], LATE_ROWS[0][1] + LATE_ROWS[1][1]
    return {"wg2": g_up[:, 0], "wu2": g_up[:, 1], "wd2": g_down[:, :r1],
            "wb": g_down[:, r1:r2].reshape(GLA_V + RW_W, D), "wo": g_down[:, r2:].reshape(D, D)}


def _late_grad_parts(g):
    up = jnp.stack([g["wg2"], g["wu2"]], axis=1).astype(BF16)
    down = jnp.concatenate([g["wd2"], g["wb"].reshape(NDEV, -1, D), g["wo"].reshape(NDEV, -1, D)], axis=1).astype(BF16)
    return [up, down]


def _layout_grads(g):
    big = {
        "ffn1_wg": g["wg1"], "ffn1_wu": g["wu1"], "ffn1_wd": g["wd1"],
        "w_in": _shard_cols(_unalign_proj(g["win"])),
    }
    small = {
        "ffn1_norm": g["g1"], "mix_norm": g["g2"], "ffn2_norm": g["g3"], "final_norm": g["gf"],
        "gla_w_a2": g["gla_w_a2"][:GLA_LORA], "gla_b_a": g["gla_b_a"],
        "gla_gn_w": jnp.sum(g["gn"].reshape(GLA_HEADS, GLA_DV), axis=0, keepdims=True),
        "rwkv_mu": _unalign_rw(g["mu"]), "rwkv_w0": g["w0"], "rwkv_a0": g["a0"],
        "rwkv_w_w2": g["w_w2"][:DECAY_LORA], "rwkv_w_a2": g["w_a2"][:AAA_LORA], "rwkv_w_g2": g["w_g2"],
        "rwkv_k_k": g["k_k"], "rwkv_k_a": g["k_a"], "rwkv_r_k": g["r_k"],
        "rwkv_lnx_w": g["lnx_w"], "rwkv_lnx_b": g["lnx_b"], "gate_b": g["gate_b"],
    }
    return big, small


_MESH = pl.DeviceIdType.MESH
_ANY = pl.BlockSpec(memory_space=pl.ANY)


def _position():
    return lax.axis_index("x"), lax.axis_index("y"), lax.axis_index("c")


def _slot(p):
    return 4 * p[0] + 2 * p[1] + p[2]


def _comm_sems(n):
    if not n:
        return []
    return [pltpu.SemaphoreType.DMA((7 * n,)), pltpu.SemaphoreType.DMA((7 * n,)), pltpu.SemaphoreType.DMA((n,))]


def _gather_plan(ins, outs, send_sems, recv_sems, local_sems):
    n = len(ins)
    x, y, c = _position()
    me, sibling = (x, y, c), (x, y, 1 - c)
    chips = [(1 - x, y), (x, 1 - y), (1 - x, 1 - y)]

    def copy(a, k, block, to, src=None):
        dst = outs[a].at[_slot(block)]
        return pltpu.make_async_remote_copy(
            src_ref=dst if src is None else src, dst_ref=dst, send_sem=send_sems.at[7 * a + k],
            recv_sem=recv_sems.at[7 * a + k], device_id=to, device_id_type=_MESH)

    def local(a):
        return pltpu.make_async_copy(ins[a], outs[a].at[_slot(me)], local_sems.at[a])

    def own(a):
        return [copy(a, 0, me, sibling, src=ins[a])] + [copy(a, 1 + j, me, (*chip, c), src=ins[a]) for j, chip in enumerate(chips)]

    def start():
        for a in range(n):
            local(a).start()
            for cp in own(a):
                cp.start()

    def forward():
        for a in range(n):
            for j, chip in enumerate(chips):
                copy(a, 1 + j, (*chip, c), me).wait_recv()
                copy(a, 4 + j, (*chip, c), sibling).start()

    def finish():
        for a in range(n):
            copy(a, 0, sibling, me).wait_recv()
            for j, chip in enumerate(chips):
                copy(a, 4 + j, (*chip, 1 - c), me).wait_recv()
        for a in range(n):
            for cp in own(a) + [copy(a, 4 + j, (*chip, c), sibling) for j, chip in enumerate(chips)]:
                cp.wait_send()
            local(a).wait()

    return start, forward, finish


def _exchange_plan(ins, outs, send_sems, recv_sems, local_sems):
    n = len(ins)
    x, y, c = _position()
    me = (x, y, c)
    flip = lambda v, f: 1 - v if f else v
    peers = [(flip(x, fx), flip(y, fy), flip(c, fc))
             for fx, fy, fc in ((0, 0, 1), (1, 0, 0), (0, 1, 0), (1, 1, 0), (1, 0, 1), (0, 1, 1), (1, 1, 1))]

    def copy(a, k, src_slot, dst_slot):
        return pltpu.make_async_remote_copy(
            src_ref=ins[a].at[src_slot], dst_ref=outs[a].at[dst_slot], send_sem=send_sems.at[7 * a + k],
            recv_sem=recv_sems.at[7 * a + k], device_id=peers[k], device_id_type=_MESH)

    def local(a):
        return pltpu.make_async_copy(ins[a].at[_slot(me)], outs[a].at[_slot(me)], local_sems.at[a])

    def start():
        for a in range(n):
            local(a).start()
            for k, peer in enumerate(peers):
                copy(a, k, _slot(peer), _slot(me)).start()

    def finish():
        for a in range(n):
            for k, peer in enumerate(peers):
                copy(a, k, _slot(peer), _slot(peer)).wait_recv()
        for a in range(n):
            for k, peer in enumerate(peers):
                copy(a, k, _slot(peer), _slot(me)).wait_send()
            local(a).wait()

    return start, finish


def _all_gather(arrays, name):
    n = len(arrays)

    def body(*refs):
        start, forward, finish = _gather_plan(refs[:n], refs[n:2 * n], *refs[2 * n:])
        start()
        forward()
        finish()

    return pl.pallas_call(
        body, name=name, in_specs=[_ANY] * n, out_specs=[_ANY] * n,
        out_shape=[jax.ShapeDtypeStruct((NDEV,) + a.shape, a.dtype) for a in arrays], scratch_shapes=_comm_sems(n),
    )(*arrays)


def _exchange(arrays, name):
    n = len(arrays)

    def body(*refs):
        start, finish = _exchange_plan(refs[:n], refs[n:2 * n], *refs[2 * n:])
        start()
        finish()

    return pl.pallas_call(
        body, name=name, in_specs=[_ANY] * n, out_specs=[_ANY] * n,
        out_shape=[jax.ShapeDtypeStruct(a.shape, a.dtype) for a in arrays], scratch_shapes=_comm_sems(n),
    )(*arrays)


def _adamw_math(w, g, m, v):
    m = ADAM_B1 * m + (1.0 - ADAM_B1) * g
    v = ADAM_B2 * v + (1.0 - ADAM_B2) * (g * g)
    m_hat = m / (1.0 - ADAM_B1 ** ADAM_STEP)
    v_hat = v / (1.0 - ADAM_B2 ** ADAM_STEP)
    delta = -ADAM_LR * (m_hat / (jnp.sqrt(v_hat) + ADAM_EPS) + ADAM_WD * w)
    return delta, m, v


def _sum_slots(ref):
    total = ref[0].astype(F32)
    for s in range(1, NDEV):
        total = total + ref[s].astype(F32)
    return total


def _adamw(parts, w, m, v, tr, name, stack_index=None, row_block_offset=0):
    R, C = w.shape

    def body(p_ref, w_ref, m_ref, v_ref, g_ref, d_ref, nm_ref, nv_ref):
        g = _sum_slots(p_ref)
        g_ref[...] = g
        d_ref[...], nm_ref[...], nv_ref[...] = _adamw_math(w_ref[...], g, m_ref[...], v_ref[...])

    if stack_index is None:
        p_spec = pl.BlockSpec((NDEV, tr, C), lambda r: (0, row_block_offset + r, 0))
    else:
        p_spec = pl.BlockSpec((NDEV, None, tr, C), lambda r: (0, stack_index, r, 0))
    blk = pl.BlockSpec((tr, C), lambda r: (r, 0))
    out = jax.ShapeDtypeStruct((R, C), F32)
    return pl.pallas_call(
        body, name=name, grid=(R // tr,), in_specs=[p_spec, blk, blk, blk], out_specs=[blk] * 4, out_shape=(out,) * 4,
        compiler_params=_params(("parallel",)),
    )(parts, w, m, v)


def _sum_gathered(parts):
    _, R, C = parts.shape

    def body(p_ref, o_ref):
        o_ref[...] = _sum_slots(p_ref)

    return pl.pallas_call(body, name="small_grad_sum", out_shape=jax.ShapeDtypeStruct((R, C), F32),
                          compiler_params=_params())(parts)


def _adamw_small(w, g, m, v):
    def body(w_ref, g_ref, m_ref, v_ref, d_ref, nm_ref, nv_ref):
        d_ref[...], nm_ref[...], nv_ref[...] = _adamw_math(w_ref[...], g_ref[...], m_ref[...], v_ref[...])

    out = jax.ShapeDtypeStruct(w.shape, F32)
    return pl.pallas_call(body, name="adamw_small", out_shape=(out,) * 3, compiler_params=_params())(w, g, m, v)


def _pack(pieces, rows):
    flat = jnp.concatenate([p.reshape(-1) for p in pieces])
    return jnp.pad(flat, (0, rows * 128 - flat.shape[0])).reshape(rows, 128)


def _unpack(packed, shapes):
    flat = packed.reshape(-1)
    out, off = [], 0
    for s in shapes:
        size = 1
        for d in s:
            size *= d
        out.append(flat[off:off + size].reshape(s))
        off += size
    return out


def _rows_for(shapes, extra=0):
    total = extra
    for s in shapes:
        size = 1
        for d in s:
            size *= d
        total += size
    return -(-total // 1024) * 8


def kernel(x, ffn1_norm, ffn1_wg, ffn1_wu, ffn1_wd, mix_norm, w_in, gla_w_a2, gla_b_a, gla_gn_w, rwkv_mu, rwkv_w0, rwkv_w_w2, rwkv_a0, rwkv_w_a2, rwkv_w_g2, rwkv_k_k, rwkv_k_a, rwkv_r_k, rwkv_lnx_w, rwkv_lnx_b, gate_b, w_branch, w_out, ffn2_norm, ffn2_wg, ffn2_wu, ffn2_wd, final_norm, loss_target, m_ffn1_norm, m_ffn1_wg, m_ffn1_wu, m_ffn1_wd, m_mix_norm, m_w_in, m_gla_w_a2, m_gla_b_a, m_gla_gn_w, m_rwkv_mu, m_rwkv_w0, m_rwkv_w_w2, m_rwkv_a0, m_rwkv_w_a2, m_rwkv_w_g2, m_rwkv_k_k, m_rwkv_k_a, m_rwkv_r_k, m_rwkv_lnx_w, m_rwkv_lnx_b, m_gate_b, m_w_branch, m_w_out, m_ffn2_norm, m_ffn2_wg, m_ffn2_wu, m_ffn2_wd, m_final_norm, v_ffn1_norm, v_ffn1_wg, v_ffn1_wu, v_ffn1_wd, v_mix_norm, v_w_in, v_gla_w_a2, v_gla_b_a, v_gla_gn_w, v_rwkv_mu, v_rwkv_w0, v_rwkv_w_w2, v_rwkv_a0, v_rwkv_w_a2, v_rwkv_w_g2, v_rwkv_k_k, v_rwkv_k_a, v_rwkv_r_k, v_rwkv_lnx_w, v_rwkv_lnx_b, v_gate_b, v_w_branch, v_w_out, v_ffn2_norm, v_ffn2_wg, v_ffn2_wu, v_ffn2_wd, v_final_norm):
    wts = dict(zip(WEIGHTS, (ffn1_norm, ffn1_wg, ffn1_wu, ffn1_wd, mix_norm, w_in, gla_w_a2, gla_b_a, gla_gn_w, rwkv_mu, rwkv_w0, rwkv_w_w2, rwkv_a0, rwkv_w_a2, rwkv_w_g2, rwkv_k_k, rwkv_k_a, rwkv_r_k, rwkv_lnx_w, rwkv_lnx_b, gate_b, w_branch, w_out, ffn2_norm, ffn2_wg, ffn2_wu, ffn2_wd, final_norm)))
    mom = dict(zip(WEIGHTS, (m_ffn1_norm, m_ffn1_wg, m_ffn1_wu, m_ffn1_wd, m_mix_norm, m_w_in, m_gla_w_a2, m_gla_b_a, m_gla_gn_w, m_rwkv_mu, m_rwkv_w0, m_rwkv_w_w2, m_rwkv_a0, m_rwkv_w_a2, m_rwkv_w_g2, m_rwkv_k_k, m_rwkv_k_a, m_rwkv_r_k, m_rwkv_lnx_w, m_rwkv_lnx_b, m_gate_b, m_w_branch, m_w_out, m_ffn2_norm, m_ffn2_wg, m_ffn2_wu, m_ffn2_wd, m_final_norm)))
    var = dict(zip(WEIGHTS, (v_ffn1_norm, v_ffn1_wg, v_ffn1_wu, v_ffn1_wd, v_mix_norm, v_w_in, v_gla_w_a2, v_gla_b_a, v_gla_gn_w, v_rwkv_mu, v_rwkv_w0, v_rwkv_w_w2, v_rwkv_a0, v_rwkv_w_a2, v_rwkv_w_g2, v_rwkv_k_k, v_rwkv_k_a, v_rwkv_r_k, v_rwkv_lnx_w, v_rwkv_lnx_b, v_gate_b, v_w_branch, v_w_out, v_ffn2_norm, v_ffn2_wg, v_ffn2_wu, v_ffn2_wd, v_final_norm)))
    two = lambda a: a.reshape(a.shape[-2:])

    bf = lambda n: two(wts[n]).astype(BF16)
    up1 = jnp.stack([bf("ffn1_wg"), bf("ffn1_wu")])
    lora = jnp.concatenate([jnp.pad(two(gla_w_a2), ((0, 0), (0, 128 - GLA_QK // NDEV)))] +
                           [two(wts[n]) for n in SMALL_SHARDED[1:]], axis=0)
    g_up1, g_down1, g_proj, g_lora = _all_gather([up1, bf("ffn1_wd"), bf("w_in"), lora], "gather_weights")
    gb = {"ffn1_wg": g_up1[:, 0], "ffn1_wu": g_up1[:, 1], "ffn1_wd": g_down1, "w_in": g_proj}
    gs = {"gla_w_a2": g_lora[:, :GLA_LORA, :GLA_QK // NDEV]}
    row = GLA_LORA
    for n in SMALL_SHARDED[1:]:
        gs[n] = g_lora[:, row:row + wts[n].shape[1]]
        row += wts[n].shape[1]
    w = _layout_weights(gb, gs, {n: wts[n] for n in REPLICATED})
    late = [jnp.stack([bf("ffn2_wg"), bf("ffn2_wu")]), jnp.concatenate([bf(n) for n, _ in LATE_ROWS], axis=0)]

    loss_part, grad_x, grads, (r_up2, r_down2) = _local_step(x[0], loss_target[0], w, late)
    big, small = _layout_grads(grads)

    p_up1 = jnp.stack([big["ffn1_wg"], big["ffn1_wu"]], axis=1).astype(BF16)
    r_up1, r_down1, r_proj = _exchange([p_up1, big["ffn1_wd"].astype(BF16), big["w_in"].astype(BF16)], "exchange_grads")
    result = {}
    state = lambda n: (two(wts[n]), two(mom[n]), two(var[n]))
    for parts, names in ((r_up1, ("ffn1_wg", "ffn1_wu")), (r_up2, ("ffn2_wg", "ffn2_wu"))):
        for i, n in enumerate(names):
            result[n] = _adamw(parts, *state(n), 256, "adamw_" + n, stack_index=i)
    result["ffn1_wd"] = _adamw(r_down1, *state("ffn1_wd"), 64, "adamw_ffn1_wd")
    row = 0
    for n, rows in LATE_ROWS:
        result[n] = _adamw(r_down2, *state(n), 64, "adamw_" + n, row_block_offset=row // 64)
        row += rows
    result["w_in"] = _adamw(r_proj, *state("w_in"), 256, "adamw_w_in")

    small_names = [n for n in WEIGHTS if n not in BIG]
    full_shapes = [small[n].shape for n in small_names]
    rows_full = _rows_for(full_shapes, extra=128)
    packed = _pack([small[n] for n in small_names] + [loss_part], rows_full)
    (gathered,) = _all_gather([packed], "gather_small_grads")
    total = _sum_gathered(gathered)
    *full_grads, loss_row = _unpack(total, full_shapes + [(1, 128)])
    me = _slot(_position())
    own = {}
    for n, g in zip(small_names, full_grads):
        if n in SMALL_SHARDED:
            cols = wts[n].shape[-1]
            g = lax.dynamic_slice_in_dim(g, me * cols, cols, axis=1)
        own[n] = g.reshape(wts[n].shape)
    own_shapes = [wts[n].shape for n in small_names]
    rows_own = _rows_for(own_shapes)
    pk = lambda d: _pack([d[n] for n in small_names], rows_own)
    d_s, m_s, v_s = _adamw_small(pk(wts), pk(own), pk(mom), pk(var))
    for n, d, m, v in zip(small_names, _unpack(d_s, own_shapes), _unpack(m_s, own_shapes), _unpack(v_s, own_shapes)):
        result[n] = (own[n], d, m, v)

    shaped = lambda n, k: result[n][k].reshape(wts[n].shape)
    return (loss_row[0, 0], grad_x[None],
            *[shaped(n, 0) for n in WEIGHTS], *[shaped(n, 1) for n in WEIGHTS],
            *[shaped(n, 2) for n in WEIGHTS], *[shaped(n, 3) for n in WEIGHTS])
```

```python
import functools

import jax
import jax.numpy as jnp
from jax import lax
from jax.experimental import pallas as pl
from jax.experimental.pallas import tpu as pltpu

F32 = jnp.float32
BF16 = jnp.bfloat16
HI = lax.Precision.HIGHEST

NDEV = 8
D = 2048
DFF = 5632
FSH = DFF // NDEV
CHUNK = 64
GLA_HEADS, GLA_DK, GLA_DV = 4, 128, 256
GLA_QK, GLA_V, GLA_LORA, GLA_TAU = 512, 1024, 16, 16.0
RW_HEADS, RW_HD, RW_W = 16, 64, 1024
DECAY_LORA, AAA_LORA, GATE_LORA = 96, 96, 256
GN_EPS = 64e-5
NORM_EPS = 1e-6
GLA_IN = 2 * GLA_QK + 2 * GLA_V + GLA_LORA
RW_IN = 3 * RW_W + DECAY_LORA + AAA_LORA + GATE_LORA
D_IN = GLA_IN + RW_IN + 2 * D
DIN_SH = D_IN // NDEV
PG_W = 2 * D
PR_W = 3584
PA_W = 3584
PA_USED = 2 * GLA_QK + 2 * GLA_V + 128
DIN_P = PG_W + PR_W + PA_W
LORA_P = 128

ADAM_LR, ADAM_B1, ADAM_B2, ADAM_EPS, ADAM_WD, ADAM_STEP = 0.001, 0.9, 0.999, 1e-08, 0.01, 10

VMEM_LIMIT = 56 * 1024 * 1024
RW_TB = 128
RW_G = 16
RW_NP = 4


def _params(sem=None, vmem=VMEM_LIMIT):
    return pltpu.CompilerParams(dimension_semantics=sem, vmem_limit_bytes=vmem)


def _pair_mask():
    return lax.broadcasted_iota(jnp.int32, (RW_HD, 2 * RW_HD), 1) < RW_HD


def _pair_rowsum(p, mask):
    tot = jnp.sum(p, axis=1, keepdims=True)
    first = jnp.sum(jnp.where(mask, p, 0.0), axis=1, keepdims=True)
    return first, tot - first


def _split_transposed(x_ref, q, dst_ref, base):
    xt = x_ref[:, 128 * q:128 * (q + 1)].T
    for g in range(RW_TB // RW_G):
        dst_ref[base + g, :, 0:RW_G] = xt[:, g * RW_G:(g + 1) * RW_G]


def _pair_column(tile_ref, idx, i, mask):
    return jnp.where(mask, tile_ref[idx, 0:RW_HD, i:i + 1], tile_ref[idx, RW_HD:, i:i + 1])


def _rw_core_fwd(rw, w, k2, kk, b, gather=()):
    T = rw.shape[0]
    nb = T // RW_TB
    ng = RW_TB // RW_G
    NP = RW_NP
    nc = len(gather)
    npair = RW_HEADS // 2 // NP

    def body(r_ref, v_ref, w_ref, k_ref, kk_ref, b_ref, *rest):
        g_in, (y_ref, st_ref), g_out = rest[:nc], rest[nc:nc + 2], rest[nc + 2:2 * nc + 2]
        s_scr, vt_scr, yt_scr, rows_scr = rest[2 * nc + 2:2 * nc + 6]
        pair, blk_i = pl.program_id(0), pl.program_id(1)
        if nc:
            start, forward, finish = _gather_plan(g_in, g_out, *rest[2 * nc + 6:])
            pl.when((pair == 0) & (blk_i == 0))(start)
            pl.when((pair == 0) & (blk_i == nb // 2))(forward)

        @pl.when(pl.program_id(1) == 0)
        def _():
            s_scr[...] = jnp.zeros_like(s_scr)
            yt_scr[...] = jnp.zeros_like(yt_scr)

        mask = _pair_mask()
        for q in range(NP):
            _split_transposed(v_ref, q, vt_scr, q * ng)
        R_, W_, K_, KK_, B_ = range(5)
        for a, ref in enumerate((r_ref, w_ref, k_ref, kk_ref, b_ref)):
            for q in range(NP):
                rows_scr[a * NP + q] = ref[:, 128 * q:128 * (q + 1)]

        def group(g, states):
            states = list(states)
            for i in range(RW_G):
                t = g * RW_G + i
                row = lambda a, q: rows_scr[a * NP + q, pl.ds(t, 1), :]
                sums = [_pair_rowsum(states[q] * row(KK_, q), mask) for q in range(NP)]
                for q in range(NP):
                    sa = jnp.where(mask, *sums[q])
                    states[q] = (states[q] * row(W_, q) - sa * row(B_, q)
                                 + _pair_column(vt_scr, q * ng + g, i, mask) * row(K_, q))
                    st_ref[q, t] = states[q]
                outs = [_pair_rowsum(states[q] * row(R_, q), mask) for q in range(NP)]
                for q in range(NP):
                    yt_scr[q * ng + g, 0:RW_HD, i:i + 1] = outs[q][0]
                    yt_scr[q * ng + g, RW_HD:, i:i + 1] = outs[q][1]
            return tuple(states)

        states = lax.fori_loop(0, ng, group, tuple(s_scr[q] for q in range(NP)))
        for q in range(NP):
            s_scr[q] = states[q]
            for g in range(ng):
                y_ref[g * RW_G:(g + 1) * RW_G, 128 * q:128 * (q + 1)] = yt_scr[q * ng + g].T[0:RW_G, :]
        if nc:
            pl.when((pair == npair - 1) & (blk_i == nb - 1))(finish)

    blk = lambda cb: pl.BlockSpec((RW_TB, 128 * NP), lambda p, i, cb=cb: (i, cb + p))
    tiles = pltpu.VMEM((NP * ng, 128, 128), F32)
    return pl.pallas_call(
        body, name="rw_core_fwd", grid=(npair, nb),
        in_specs=[blk(0), blk(2 * RW_W // (128 * NP)), blk(0), blk(0), blk(0), blk(0)] + [_ANY] * nc,
        out_specs=[blk(0), pl.BlockSpec((NP, RW_TB, RW_HD, 128), lambda p, i: (p, i, 0, 0))] + [_ANY] * nc,
        out_shape=[jax.ShapeDtypeStruct((T, RW_W), F32), jax.ShapeDtypeStruct((RW_HEADS // 2, T, RW_HD, 128), F32)]
        + [jax.ShapeDtypeStruct((NDEV,) + a.shape, a.dtype) for a in gather],
        scratch_shapes=[pltpu.VMEM((NP, RW_HD, 128), F32), tiles, tiles, pltpu.VMEM((5 * NP, RW_TB, 128), F32)]
        + _comm_sems(nc),
        compiler_params=_params(("arbitrary", "arbitrary")),
    )(rw, rw, w, k2, kk, b, *gather)


def _rw_core_bwd(rw, w, k2, kk, b, states, dy, exchange=()):
    T = rw.shape[0]
    nb = T // RW_TB
    ng = RW_TB // RW_G
    NP = RW_NP
    nc = len(exchange)
    npair = RW_HEADS // 2 // NP

    def body(r_ref, v_ref, w_ref, k_ref, kk_ref, b_ref, dy_ref, st_ref, sp_ref, *rest):
        e_in, e_out = rest[:nc], rest[nc + 6:2 * nc + 6]
        dr_ref, dw_ref, dk_ref, dv_ref, dkk_ref, db_ref = rest[nc:nc + 6]
        ds_scr, vt_scr, dyt_scr, dvt_scr, rows_scr, out_scr = rest[2 * nc + 6:2 * nc + 12]
        step = pl.program_id(1)
        if nc:
            start, finish = _exchange_plan(e_in, e_out, *rest[2 * nc + 12:])
            pl.when((pl.program_id(0) == 0) & (step == 0))(start)

        @pl.when(step == 0)
        def _():
            ds_scr[...] = jnp.zeros_like(ds_scr)
            dvt_scr[...] = jnp.zeros_like(dvt_scr)

        mask = _pair_mask()
        for q in range(NP):
            _split_transposed(v_ref, q, vt_scr, q * ng)
            _split_transposed(dy_ref, q, dyt_scr, q * ng)
        R_, W_, K_, KK_, B_ = range(5)
        for a, ref in enumerate((r_ref, w_ref, k_ref, kk_ref, b_ref)):
            for q in range(NP):
                rows_scr[a * NP + q] = ref[:, 128 * q:128 * (q + 1)]

        def group(gg, grads):
            g = ng - 1 - gg
            grads = list(grads)
            pairs = range(NP)
            for i in reversed(range(RW_G)):
                t = g * RW_G + i
                row = lambda a, q: rows_scr[a * NP + q, pl.ds(t, 1), :]

                def put(a, q, value):
                    out_scr[a * NP + q, pl.ds(t, 1), :] = value

                s_old = [st_ref[q, jnp.maximum(t - 1, 0)] for q in pairs]
                if i == 0:
                    s_old = [jnp.where(g == 0, jnp.where(step == nb - 1, 0.0, sp_ref[q, 0]), s_old[q]) for q in pairs]
                dycol = [_pair_column(dyt_scr, q * ng + g, i, mask) for q in pairs]
                dS = [grads[q] + dycol[q] * row(R_, q) for q in pairs]
                m = [_pair_rowsum(dS[q] * row(B_, q), mask) for q in pairs]
                sa = [_pair_rowsum(s_old[q] * row(KK_, q), mask) for q in pairs]
                dv = [_pair_rowsum(dS[q] * row(K_, q), mask) for q in pairs]
                for q in pairs:
                    put(R_, q, jnp.sum(st_ref[q, t] * dycol[q], axis=0, keepdims=True))
                    put(W_, q, jnp.sum(dS[q] * s_old[q], axis=0, keepdims=True))
                    put(K_, q, jnp.sum(dS[q] * _pair_column(vt_scr, q * ng + g, i, mask), axis=0, keepdims=True))
                for q in pairs:
                    dsa = -jnp.where(mask, *m[q])
                    grads[q] = dS[q] * row(W_, q) + dsa * row(KK_, q)
                    put(KK_, q, jnp.sum(s_old[q] * dsa, axis=0, keepdims=True))
                    put(B_, q, -jnp.sum(dS[q] * jnp.where(mask, *sa[q]), axis=0, keepdims=True))
                    dvt_scr[q * ng + g, 0:RW_HD, i:i + 1] = dv[q][0]
                    dvt_scr[q * ng + g, RW_HD:, i:i + 1] = dv[q][1]
            return tuple(grads)

        grads = lax.fori_loop(0, ng, group, tuple(ds_scr[q] for q in range(NP)))
        for q in range(NP):
            ds_scr[q] = grads[q]
            for a, ref in enumerate((dr_ref, dw_ref, dk_ref, dkk_ref, db_ref)):
                ref[:, 128 * q:128 * (q + 1)] = out_scr[a * NP + q]
            for g in range(ng):
                dv_ref[g * RW_G:(g + 1) * RW_G, 128 * q:128 * (q + 1)] = dvt_scr[q * ng + g].T[0:RW_G, :]
        if nc:
            pl.when((pl.program_id(0) == npair - 1) & (step == nb - 1))(finish)

    blk = lambda cb: pl.BlockSpec((RW_TB, 128 * NP), lambda p, i, cb=cb: (nb - 1 - i, cb + p))
    st_spec = pl.BlockSpec((NP, RW_TB, RW_HD, 128), lambda p, i: (p, nb - 1 - i, 0, 0))
    sp_spec = pl.BlockSpec((NP, 1, RW_HD, 128), lambda p, i: (p, jnp.maximum((nb - 1 - i) * RW_TB - 1, 0), 0, 0))
    out = jax.ShapeDtypeStruct((T, RW_W), F32)
    tiles = pltpu.VMEM((NP * ng, 128, 128), F32)
    return pl.pallas_call(
        body, name="rw_core_bwd", grid=(npair, nb),
        in_specs=[blk(0), blk(2 * RW_W // (128 * NP)), blk(0), blk(0), blk(0), blk(0), blk(0), st_spec, sp_spec]
        + [_ANY] * nc,
        out_specs=[blk(0)] * 6 + [_ANY] * nc,
        out_shape=[out] * 6 + [jax.ShapeDtypeStruct(a.shape, a.dtype) for a in exchange],
        scratch_shapes=[pltpu.VMEM((NP, RW_HD, 128), F32), tiles, tiles, tiles,
                        pltpu.VMEM((5 * NP, RW_TB, 128), F32), pltpu.VMEM((5 * NP, RW_TB, 128), F32)] + _comm_sems(nc),
        compiler_params=_params(("arbitrary", "arbitrary")),
    )(rw, rw, w, k2, kk, b, dy, states, states, *exchange)


GLA_CB = 8


def _gla_chunk(s_t, q, k, v, la, ltri):
    cum = jnp.dot(ltri, la, precision=HI, preferred_element_type=F32)
    total = jnp.sum(la, axis=0, keepdims=True)
    kdec = k * jnp.exp(total - cum)
    u_t = _bdot(v, kdec, _TN)
    s_t = jnp.exp(total) * s_t + u_t
    o = _bdot(q * (GLA_DK ** -0.5), s_t, _NT)
    return s_t, o


def _gla_core_fwd(pa, la, ltri):
    T = pa.shape[0]
    cb = min(GLA_CB, T // CHUNK)
    rows = cb * CHUNK
    nsteps = T // rows

    def body(q_ref, k_ref, v_ref, la_ref, ltri_ref, o_ref, st_ref, s_scr):
        @pl.when(pl.program_id(1) == 0)
        def _():
            s_scr[...] = jnp.zeros_like(s_scr)

        def chunk(c, s_t):
            sl = pl.ds(pl.multiple_of(c * CHUNK, CHUNK), CHUNK)
            s_t, o = _gla_chunk(s_t, q_ref[sl, :], k_ref[sl, :], v_ref[sl, :], la_ref[sl, :], ltri_ref[...])
            o_ref[sl, :] = o
            st_ref[0, c] = s_t
            return s_t

        s_scr[...] = lax.fori_loop(0, cb, chunk, s_scr[...])

    qk = lambda off: pl.BlockSpec((rows, GLA_DK), lambda h, i, off=off: (i, off + h))
    vspec = pl.BlockSpec((rows, GLA_DV), lambda h, i: (i, 2 * GLA_QK // GLA_DV + h))
    return pl.pallas_call(
        body, name="gla_core_fwd", grid=(GLA_HEADS, nsteps),
        in_specs=[qk(0), qk(GLA_HEADS), vspec, qk(0), pl.BlockSpec((CHUNK, CHUNK), lambda h, i: (0, 0))],
        out_specs=[pl.BlockSpec((rows, GLA_DV), lambda h, i: (i, h)),
                   pl.BlockSpec((1, cb, GLA_DV, GLA_DK), lambda h, i: (h, i, 0, 0))],
        out_shape=(jax.ShapeDtypeStruct((T, GLA_V), F32),
                   jax.ShapeDtypeStruct((GLA_HEADS, T // CHUNK, GLA_DV, GLA_DK), F32)),
        scratch_shapes=[pltpu.VMEM((GLA_DV, GLA_DK), F32)],
        compiler_params=_params(("arbitrary", "arbitrary")),
    )(pa, pa, pa, la, ltri)


def _gla_core_bwd(pa, la, ltri, states, do):
    T = pa.shape[0]
    cb = min(GLA_CB, T // CHUNK)
    rows = cb * CHUNK
    nsteps = T // rows

    def body(q_ref, k_ref, v_ref, la_ref, ltri_ref, st_ref, sp_ref, do_ref,
             dq_ref, dk_ref, dv_ref, dla_ref, ds_scr):
        step = pl.program_id(1)

        @pl.when(step == 0)
        def _():
            ds_scr[...] = jnp.zeros_like(ds_scr)

        s_before = jnp.where(step == nsteps - 1, 0.0, sp_ref[0, 0])

        def chunk(cc, ds_t):
            c = cb - 1 - cc
            sl = pl.ds(pl.multiple_of(c * CHUNK, CHUNK), CHUNK)
            s_prev = jnp.where(c == 0, s_before, st_ref[0, jnp.maximum(c - 1, 0)])
            _, vjp = jax.vjp(functools.partial(_gla_chunk, ltri=ltri_ref[...]),
                             s_prev, q_ref[sl, :], k_ref[sl, :], v_ref[sl, :], la_ref[sl, :])
            ds_prev, dq, dk, dv, dla = vjp((ds_t, do_ref[sl, :]))
            dq_ref[sl, :] = dq
            dk_ref[sl, :] = dk
            dv_ref[sl, :] = dv
            dla_ref[sl, :] = dla
            return ds_prev

        ds_scr[...] = lax.fori_loop(0, cb, chunk, ds_scr[...])

    r = lambda i: nsteps - 1 - i
    qk = lambda off: pl.BlockSpec((rows, GLA_DK), lambda h, i, off=off: (r(i), off + h))
    vspec = pl.BlockSpec((rows, GLA_DV), lambda h, i: (r(i), 2 * GLA_QK // GLA_DV + h))
    o128 = pl.BlockSpec((rows, GLA_DK), lambda h, i: (r(i), h))
    o256 = pl.BlockSpec((rows, GLA_DV), lambda h, i: (r(i), h))
    return pl.pallas_call(
        body, name="gla_core_bwd", grid=(GLA_HEADS, nsteps),
        in_specs=[qk(0), qk(GLA_HEADS), vspec, qk(0), pl.BlockSpec((CHUNK, CHUNK), lambda h, i: (0, 0)),
                  pl.BlockSpec((1, cb, GLA_DV, GLA_DK), lambda h, i: (h, r(i), 0, 0)),
                  pl.BlockSpec((1, 1, GLA_DV, GLA_DK), lambda h, i: (h, jnp.maximum(r(i) * cb - 1, 0), 0, 0)),
                  o256],
        out_specs=[o128, o128, o256, o128],
        out_shape=(jax.ShapeDtypeStruct((T, GLA_QK), F32), jax.ShapeDtypeStruct((T, GLA_QK), F32),
                   jax.ShapeDtypeStruct((T, GLA_V), F32), jax.ShapeDtypeStruct((T, GLA_QK), F32)),
        scratch_shapes=[pltpu.VMEM((GLA_DV, GLA_DK), F32)],
        compiler_params=_params(("arbitrary", "arbitrary")),
    )(pa, pa, pa, la, ltri, states, states, do)


def _rowwise(fn, name, T, tm, rows, pars, row_outs, acc_outs):
    nr, npar, nro = len(rows), len(pars), len(row_outs)
    tm = min(tm, T)
    nsteps = T // tm

    def body(*refs):
        i = pl.program_id(0)
        ins = [r[...] for r in refs[:nr + npar]]
        outs, accs = fn(i, *ins)
        for r, o in zip(refs[nr + npar:nr + npar + nro], outs):
            r[...] = o.astype(r.dtype)
        for r, a in zip(refs[nr + npar + nro:], accs):
            @pl.when(i == 0)
            def _(r=r, a=a):
                r[...] = a

            @pl.when(i > 0)
            def _(r=r, a=a):
                r[...] += a

    def rspec(width, cb, kind):
        if kind == "cur":
            return pl.BlockSpec((tm, width), lambda i: (i, cb))
        if kind == "prev":
            return pl.BlockSpec((8, width), lambda i: (jnp.maximum(i * (tm // 8) - 1, 0), cb))
        return pl.BlockSpec((8, width), lambda i: (jnp.minimum((i + 1) * (tm // 8), T // 8 - 1), cb))

    in_specs = [rspec(w, cb, kind) for (_, w, cb, kind) in rows]
    in_specs += [pl.BlockSpec(p.shape, lambda i, nd=p.ndim: (0,) * nd) for p in pars]
    out_specs = [pl.BlockSpec((tm, w), lambda i: (i, 0)) for (w, _) in row_outs]
    out_specs += [pl.BlockSpec(s, lambda i, nd=len(s): (0,) * nd) for s in acc_outs]
    out_shape = [jax.ShapeDtypeStruct((T, w), dt) for (w, dt) in row_outs]
    out_shape += [jax.ShapeDtypeStruct(s, F32) for s in acc_outs]
    res = pl.pallas_call(
        body, name=name, grid=(nsteps,), in_specs=in_specs, out_specs=out_specs, out_shape=out_shape,
        compiler_params=_params(("arbitrary",)),
    )(*[r[0] for r in rows], *pars)
    return res


def _cur(a, width=None, cb=0):
    return (a, a.shape[1] if width is None else width, cb, "cur")


def _sigmoid(x):
    return 1.0 / (1.0 + jnp.exp(-x))


def _silu(x):
    return x * _sigmoid(x)


def _softplus(x):
    return jnp.maximum(x, 0.0) + jnp.log(1.0 + jnp.exp(-jnp.abs(x)))


def _rms(x, g):
    return x * lax.rsqrt(jnp.mean(x * x, axis=-1, keepdims=True) + NORM_EPS) * g


def _dot_hi(a, b):
    return jnp.dot(a, b, precision=HI, preferred_element_type=F32)


def _rms_fwd(x, g, name):
    T = x.shape[0]
    fn = lambda i, xb, gb: ((_rms(xb, gb),), ())
    return _rowwise(fn, name, T, 256, [_cur(x)], [g], [(D, BF16)], [])[0]


def _rms_bwd(x, g, dh, dres, name):
    T = x.shape[0]

    def fn(i, xb, dhb, drb, gb):
        _, vjp = jax.vjp(_rms, xb, gb)
        dx, dg = vjp(dhb)
        return (drb + dx,), (dg,)

    return _rowwise(fn, name, T, 256, [_cur(x), _cur(dh), _cur(dres)], [g], [(D, F32)], [(1, D)])


def _loss_bwd(x, target, g):
    T = x.shape[0]

    def loss(xb, gb, tb):
        err = _rms(xb, gb) - tb
        return 0.5 * jnp.sum(jnp.mean(err * err, axis=-1, keepdims=True))

    def fn(i, xb, tb, gb):
        val, (dx, dg) = jax.value_and_grad(loss, argnums=(0, 1))(xb, gb, tb)
        return (dx,), (jnp.full((1, 128), val, F32), dg)

    return _rowwise(fn, "loss_bwd", T, 256, [_cur(x), _cur(target)], [g], [(D, F32)], [(1, 128), (1, D)])


def _gla_la(a_down, w_a2, b_a):
    return -_softplus(-(_bdot(a_down, w_a2, _NN) + b_a)) * (1.0 / GLA_TAU)


def _gla_prep(pa, w_a2, b_a):
    T = pa.shape[0]
    fn = lambda i, ab, wb, bb: ((_gla_la(ab, wb, bb),), ())
    return _rowwise(fn, "gla_prep", T, 512, [_cur(pa, LORA_P, (2 * GLA_QK + 2 * GLA_V) // LORA_P)], [w_a2, b_a],
                    [(GLA_QK, F32)], [])[0]


def _gla_prep_bwd(pa, w_a2, b_a, dla):
    T = pa.shape[0]

    def fn(i, ab, dlab, wb, bb):
        _, vjp = jax.vjp(_gla_la, ab, wb, bb)
        da, dw, db = vjp(dlab)
        return (da,), (dw, db)

    return _rowwise(fn, "gla_prep_bwd", T, 512, [_cur(pa, LORA_P, (2 * GLA_QK + 2 * GLA_V) // LORA_P), _cur(dla)],
                    [w_a2, b_a], [(LORA_P, BF16)], [(LORA_P, GLA_QK), (1, GLA_QK)])


def _gla_out(o, r, gn, ind, ind_t):
    ms = _dot_hi(_dot_hi(o * o, ind) * (1.0 / GLA_DV), ind_t)
    return o * lax.rsqrt(ms + NORM_EPS) * gn * _silu(r)


def _gla_post(o_raw, pa, gn, ind, ind_t):
    T = pa.shape[0]
    fn = lambda i, ob, rb, gb, a, b: ((_gla_out(ob, rb, gb, a, b),), ())
    return _rowwise(fn, "gla_post", T, 256, [_cur(o_raw), _cur(pa, GLA_V, 2)], [gn, ind, ind_t], [(GLA_V, BF16)], [])[0]


def _gla_post_bwd(o_raw, pa, gn, ind, ind_t, do):
    T = pa.shape[0]

    def fn(i, ob, rb, dob, gb, a, b):
        _, vjp = jax.vjp(lambda o, r, g: _gla_out(o, r, g, a, b), ob, rb, gb)
        d_o, d_r, d_g = vjp(dob)
        return (d_o, d_r), (d_g,)

    return _rowwise(fn, "gla_post_bwd", T, 256, [_cur(o_raw), _cur(pa, GLA_V, 2), _cur(do)], [gn, ind, ind_t],
                    [(GLA_V, F32), (GLA_V, BF16)], [(1, GLA_V)])


def _shift_rows(cur, prev8, i):
    first = jnp.where(i == 0, 0.0, prev8[7:8, :])
    rolled = pltpu.roll(cur, 1, 0)
    return jnp.where(lax.broadcasted_iota(jnp.int32, cur.shape, 0) == 0, first, rolled)


def _rw_gates(rw, w0, w_w2, a0, w_a2, w_g2, k_k, k_a, ind, ind_t):
    rk = rw[:, RW_W:2 * RW_W]
    wd = rw[:, 3 * RW_W:3 * RW_W + LORA_P]
    ad = rw[:, 3 * RW_W + LORA_P:3 * RW_W + 2 * LORA_P]
    gd = rw[:, 3 * RW_W + 2 * LORA_P:]
    w_raw = w0 + _bdot(jnp.tanh(wd), w_w2, _NN)
    w = jnp.exp(-jnp.exp(-_softplus(-w_raw) - 0.5))
    a = _sigmoid(a0 + _bdot(ad, w_a2, _NN))
    g = _bdot(_sigmoid(gd), w_g2, _NN)
    kk = rk * k_k
    kk = kk * _dot_hi(lax.rsqrt(jnp.maximum(_dot_hi(kk * kk, ind), 1e-24)), ind_t)
    k2 = rk * (1.0 + (a - 1.0) * k_a)
    return w, k2, kk, kk * a, g


def _rw_prep(pr, mu, gate_pars):
    T = pr.shape[0]

    def fn(i, cur, prev8, mub, *gp):
        rw = cur + mub * (_shift_rows(cur, prev8, i) - cur)
        return (rw,) + _rw_gates(rw, *gp), ()

    return _rowwise(fn, "rw_prep", T, 256, [_cur(pr), (pr, PR_W, 0, "prev")], [mu, *gate_pars],
                    [(PR_W, F32)] + [(RW_W, F32)] * 5, [])


def _rw_prep_bwd(pr, mu, gate_pars, d_r, d_v, d_w, d_k2, d_kk, d_b, d_g):
    T = pr.shape[0]
    rows = [_cur(pr), (pr, PR_W, 0, "prev")] + [_cur(x) for x in (*d_r, *d_v, d_w, *d_k2, d_kk, d_b, d_g)]
    acc = [(1, PR_W)] + [tuple(p.shape) for p in gate_pars[:-2]]

    def fn(i, cur, prev8, dr1, dr2, dv1, dv2, dw, dk1, dk2, dkk, db, dg, mub, *gp):
        sh = _shift_rows(cur, prev8, i)
        rw = cur + mub * (sh - cur)
        _, vjp = jax.vjp(lambda x, *p: _rw_gates(x, *p, gp[-2], gp[-1]), rw, *gp[:-2])
        grads = vjp((dw, dk1 + dk2, dkk, db, dg))
        zeros = jnp.zeros((cur.shape[0], PR_W - 3 * RW_W), F32)
        drw = grads[0] + jnp.concatenate([dr1 + dr2, jnp.zeros_like(dr1), dv1 + dv2, zeros], axis=1)
        dmu = jnp.sum(drw * (sh - cur), axis=0, keepdims=True)
        return (drw,), (dmu, *grads[1:])

    return _rowwise(fn, "rw_prep_bwd", T, 128, rows, [mu, *gate_pars], [(PR_W, F32)], acc)


def _shift_bwd(drw, mu):
    T = drw.shape[0]
    tm = min(256, T)

    def fn(i, cur, next8, mub):
        last = jnp.where(i == T // tm - 1, 0.0, next8[0:1, :])
        rolled = pltpu.roll(cur, cur.shape[0] - 1, 0)
        nxt = jnp.where(lax.broadcasted_iota(jnp.int32, cur.shape, 0) == cur.shape[0] - 1, last, rolled)
        return ((1.0 - mub) * cur + mub * nxt,), ()

    return _rowwise(fn, "shift_bwd", T, tm, [_cur(drw), (drw, PR_W, 0, "next")], [mu], [(PR_W, BF16)], [])[0]


def _rw_out(y, r, v, k2, g, lnx_w, lnx_b, r_k, ind, ind_t):
    mean = _dot_hi(_dot_hi(y, ind) * (1.0 / RW_HD), ind_t)
    yc = y - mean
    var = _dot_hi(_dot_hi(yc * yc, ind) * (1.0 / RW_HD), ind_t)
    yn = yc * lax.rsqrt(var + GN_EPS) * lnx_w + lnx_b
    bonus = _dot_hi(_dot_hi(r * k2 * r_k, ind), ind_t) * v
    return (yn + bonus) * g


def _rw_post(y, rw, k2, g, pars):
    T = y.shape[0]
    fn = lambda i, *a: ((_rw_out(*a),), ())
    return _rowwise(fn, "rw_post", T, 256, [_cur(y), _cur(rw, RW_W, 0), _cur(rw, RW_W, 2), _cur(k2), _cur(g)], pars,
                    [(RW_W, BF16)], [])[0]


def _rw_post_bwd(y, rw, k2, g, pars, do):
    T = y.shape[0]

    def fn(i, yb, rb, vb, kb, gb, dob, lw, lb, rk, ind, ind_t):
        _, vjp = jax.vjp(lambda *a: _rw_out(*a, ind, ind_t), yb, rb, vb, kb, gb, lw, lb, rk)
        gr = vjp(dob)
        return gr[:5], gr[5:]

    return _rowwise(fn, "rw_post_bwd", T, 256,
                    [_cur(y), _cur(rw, RW_W, 0), _cur(rw, RW_W, 2), _cur(k2), _cur(g), _cur(do)], pars,
                    [(RW_W, F32)] * 5, [(1, RW_W)] * 3)


def _merge_bwd(dm, y_gla, y_rw, pg, gate_b):
    T = dm.shape[0]

    def fn(i, dmb, ya, yr, p1, p2, gb):
        g1 = _sigmoid(p1 + gb[:, :D])
        g2 = _sigmoid(p2 + gb[:, D:])
        dp1 = dmb * ya * g1 * (1.0 - g1)
        dp2 = dmb * yr * g2 * (1.0 - g2)
        dp = jnp.concatenate([dp1, dp2], axis=1)
        return (dmb * g1, dmb * g2, dp), (jnp.sum(dp, axis=0, keepdims=True),)

    return _rowwise(fn, "merge_bwd", T, 256, [_cur(dm), _cur(y_gla), _cur(y_rw), _cur(pg, D, 0), _cur(pg, D, 1)],
                    [gate_b], [(D, BF16), (D, BF16), (PG_W, BF16)], [(1, PG_W)])


_NN = (((1,), (0,)), ((), ()))
_NT = (((1,), (1,)), ((), ()))
_TN = (((0,), (0,)), ((), ()))


def _bdot(a, b, dims):
    return lax.dot_general(a.astype(BF16), b.astype(BF16), dims, preferred_element_type=F32)


def _accumulate(k, nk, acc, part, finish):
    if nk == 1:
        finish(part)
        return

    @pl.when(k == 0)
    def _():
        acc[...] = part

    @pl.when(k > 0)
    def _():
        acc[...] += part

    @pl.when(k == nk - 1)
    def _():
        finish(acc[...])


def _matmul(a, b, mode, M, N, K, tm, tn, tk, name, a_off=(0, 0), b_off=(0, 0), res=None, scale=1.0, out_dtype=F32):
    tm, tn, tk = min(tm, M), min(tn, N), min(tk, K)
    nk = K // tk
    if mode == "nn":
        a_spec = pl.BlockSpec((tm, tk), lambda i, j, k: (i + a_off[0], k + a_off[1]))
        b_spec = pl.BlockSpec((tk, tn), lambda i, j, k: (k + b_off[0], j + b_off[1]))
        dims = _NN
    elif mode == "nt":
        a_spec = pl.BlockSpec((tm, tk), lambda i, j, k: (i + a_off[0], k + a_off[1]))
        b_spec = pl.BlockSpec((tn, tk), lambda i, j, k: (j + b_off[0], k + b_off[1]))
        dims = _NT
    else:
        a_spec = pl.BlockSpec((tk, tm), lambda i, j, k: (k + a_off[0], i + a_off[1]))
        b_spec = pl.BlockSpec((tk, tn), lambda i, j, k: (k + b_off[0], j + b_off[1]))
        dims = _TN
    o_spec = pl.BlockSpec((tm, tn), lambda i, j, k: (i, j))

    def body(a_ref, b_ref, *rest):
        r_ref = rest[0] if res is not None else None
        o_ref = rest[1] if res is not None else rest[0]
        acc = rest[-1] if nk > 1 else None

        def finish(total):
            total = total * scale if scale != 1.0 else total
            if r_ref is not None:
                total = r_ref[...] + total
            o_ref[...] = total.astype(out_dtype)

        _accumulate(pl.program_id(2), nk, acc, _bdot(a_ref[...], b_ref[...], dims), finish)

    return pl.pallas_call(
        body, name=name, grid=(M // tm, N // tn, nk),
        in_specs=[a_spec, b_spec] + ([o_spec] if res is not None else []),
        out_specs=o_spec, out_shape=jax.ShapeDtypeStruct((M, N), out_dtype),
        scratch_shapes=[pltpu.VMEM((tm, tn), F32)] if nk > 1 else [],
        compiler_params=_params(("parallel", "parallel", "arbitrary")),
    )(a, b, *([res] if res is not None else []))


def _ffn_up(h, wg, wu, name):
    T = h.shape[0]
    tm = min(1024, T)

    def body(h_ref, wg_ref, wu_ref, a_ref, u_ref, s_ref):
        hb = h_ref[...]
        a = _bdot(hb, wg_ref[...], _NN)
        u = _bdot(hb, wu_ref[...], _NN)
        a_ref[...] = a
        u_ref[...] = u
        s_ref[...] = (_silu(a) * u).astype(BF16)

    w_spec = pl.BlockSpec((None, D, FSH), lambda i, j: (j, 0, 0))
    o_spec = pl.BlockSpec((None, tm, FSH), lambda i, j: (j, i, 0))
    sh = lambda dt: jax.ShapeDtypeStruct((NDEV, T, FSH), dt)
    return pl.pallas_call(
        body, name=name, grid=(T // tm, NDEV),
        in_specs=[pl.BlockSpec((tm, D), lambda i, j: (i, 0)), w_spec, w_spec],
        out_specs=[o_spec] * 3, out_shape=(sh(F32), sh(F32), sh(BF16)),
        compiler_params=_params(("parallel", "arbitrary")),
    )(h, wg, wu)


def _ffn_down(s, wd, x, name):
    T = x.shape[0]
    tm, tn = min(1024, T), 1024

    def body(s_ref, wd_ref, x_ref, o_ref, acc):
        def finish(total):
            o_ref[...] = x_ref[...] + 0.5 * total

        _accumulate(pl.program_id(2), NDEV, acc, _bdot(s_ref[...], wd_ref[...], _NN), finish)

    xo = pl.BlockSpec((tm, tn), lambda i, n, j: (i, n))
    return pl.pallas_call(
        body, name=name, grid=(T // tm, D // tn, NDEV),
        in_specs=[pl.BlockSpec((None, tm, FSH), lambda i, n, j: (j, i, 0)),
                  pl.BlockSpec((None, FSH, tn), lambda i, n, j: (j, 0, n)), xo],
        out_specs=xo, out_shape=jax.ShapeDtypeStruct((T, D), F32),
        scratch_shapes=[pltpu.VMEM((tm, tn), F32)],
        compiler_params=_params(("parallel", "parallel", "arbitrary")),
    )(s, wd, x)


def _ffn_bwd_hidden(dx, wd, a, u, name):
    T = dx.shape[0]
    tm = min(1024, T)

    def body(dx_ref, wd_ref, a_ref, u_ref, da_ref, du_ref):
        ds = 0.5 * _bdot(dx_ref[...], wd_ref[...], _NT)
        av = a_ref[...]
        sg = _sigmoid(av)
        da_ref[...] = (ds * u_ref[...] * (sg * (1.0 + av * (1.0 - sg)))).astype(BF16)
        du_ref[...] = (ds * (av * sg)).astype(BF16)

    act = pl.BlockSpec((None, tm, FSH), lambda i, j: (j, i, 0))
    sh = jax.ShapeDtypeStruct((NDEV, T, FSH), BF16)
    return pl.pallas_call(
        body, name=name, grid=(T // tm, NDEV),
        in_specs=[pl.BlockSpec((tm, D), lambda i, j: (i, 0)), pl.BlockSpec((None, FSH, D), lambda i, j: (j, 0, 0)),
                  act, act],
        out_specs=[act, act], out_shape=(sh, sh),
        compiler_params=_params(("parallel", "arbitrary")),
    )(dx, wd, a, u)


def _ffn_bwd_input(da, du, wg, wu, name):
    T = da.shape[1]
    tm, tn = min(1024, T), 1024

    def body(da_ref, du_ref, wg_ref, wu_ref, o_ref, acc):
        part = _bdot(da_ref[...], wg_ref[...], _NT) + _bdot(du_ref[...], wu_ref[...], _NT)

        def finish(total):
            o_ref[...] = total

        _accumulate(pl.program_id(2), NDEV, acc, part, finish)

    act = pl.BlockSpec((None, tm, FSH), lambda i, n, j: (j, i, 0))
    wsp = pl.BlockSpec((None, tn, FSH), lambda i, n, j: (j, n, 0))
    return pl.pallas_call(
        body, name=name, grid=(T // tm, D // tn, NDEV),
        in_specs=[act, act, wsp, wsp],
        out_specs=pl.BlockSpec((tm, tn), lambda i, n, j: (i, n)), out_shape=jax.ShapeDtypeStruct((T, D), F32),
        scratch_shapes=[pltpu.VMEM((tm, tn), F32)],
        compiler_params=_params(("parallel", "parallel", "arbitrary")),
    )(da, du, wg, wu)


def _ffn_grad_up(h, da, du, name):
    T = h.shape[0]
    tm, tk = 1024, min(1024, T)
    nk = T // tk

    def body(h_ref, da_ref, du_ref, o_ref, acc_a, acc_u):
        k = pl.program_id(2)
        hb = h_ref[...]
        for acc, ref, slot in ((acc_a, da_ref, 0), (acc_u, du_ref, 1)):
            def finish(total, slot=slot):
                o_ref[slot] = total.astype(BF16)

            _accumulate(k, nk, acc, _bdot(hb, ref[...], _TN), finish)

    act = pl.BlockSpec((None, tk, FSH), lambda j, i, t: (j, t, 0))
    return pl.pallas_call(
        body, name=name, grid=(NDEV, D // tm, nk),
        in_specs=[pl.BlockSpec((tk, tm), lambda j, i, t: (t, i)), act, act],
        out_specs=pl.BlockSpec((None, 2, tm, FSH), lambda j, i, t: (j, 0, i, 0)),
        out_shape=jax.ShapeDtypeStruct((NDEV, 2, D, FSH), BF16),
        scratch_shapes=[pltpu.VMEM((tm, FSH), F32), pltpu.VMEM((tm, FSH), F32)],
        compiler_params=_params(("parallel", "parallel", "arbitrary")),
    )(h, da, du)


def _ffn_grad_down(s, dx, name):
    T = dx.shape[0]
    tn, tk = 1024, min(1024, T)
    nk = T // tk

    def body(s_ref, dx_ref, o_ref, acc):
        def finish(total):
            o_ref[...] = (0.5 * total).astype(BF16)

        _accumulate(pl.program_id(2), nk, acc, _bdot(s_ref[...], dx_ref[...], _TN), finish)

    return pl.pallas_call(
        body, name=name, grid=(NDEV, D // tn, nk),
        in_specs=[pl.BlockSpec((None, tk, FSH), lambda j, n, t: (j, t, 0)), pl.BlockSpec((tk, tn), lambda j, n, t: (t, n))],
        out_specs=pl.BlockSpec((None, FSH, tn), lambda j, n, t: (j, 0, n)),
        out_shape=jax.ShapeDtypeStruct((NDEV, DFF // NDEV, D), BF16),
        scratch_shapes=[pltpu.VMEM((FSH, tn), F32)],
        compiler_params=_params(("parallel", "parallel", "arbitrary")),
    )(s, dx)


def _branch_merge(o_gla, o_rw, wb, pg, gate_b):
    T = o_gla.shape[0]
    tm, tn = min(1024, T), 512

    def body(og_ref, or_ref, w1_ref, w2_ref, p1_ref, p2_ref, b1_ref, b2_ref, yg_ref, yr_ref, m_ref):
        yg = _bdot(og_ref[...], w1_ref[...], _NN)
        yr = _bdot(or_ref[...], w2_ref[...], _NN)
        yg_ref[...] = yg
        yr_ref[...] = yr
        m_ref[...] = (_sigmoid(p1_ref[...] + b1_ref[...]) * yg + _sigmoid(p2_ref[...] + b2_ref[...]) * yr).astype(BF16)

    nj = D // tn
    act = pl.BlockSpec((tm, GLA_V), lambda i, j: (i, 0))
    out = pl.BlockSpec((tm, tn), lambda i, j: (i, j))
    return pl.pallas_call(
        body, name="branch_merge", grid=(T // tm, nj),
        in_specs=[act, act, pl.BlockSpec((GLA_V, tn), lambda i, j: (0, j)), pl.BlockSpec((RW_W, tn), lambda i, j: (1, j)),
                  out, pl.BlockSpec((tm, tn), lambda i, j: (i, nj + j)),
                  pl.BlockSpec((1, tn), lambda i, j: (0, j)), pl.BlockSpec((1, tn), lambda i, j: (0, nj + j))],
        out_specs=[out, out, out],
        out_shape=(jax.ShapeDtypeStruct((T, D), F32), jax.ShapeDtypeStruct((T, D), F32), jax.ShapeDtypeStruct((T, D), BF16)),
        compiler_params=_params(("parallel", "arbitrary")),
    )(o_gla, o_rw, wb, wb, pg, pg, gate_b, gate_b)


def _head_indicator(width, heads):
    col = lax.broadcasted_iota(jnp.int32, (width, 128), 0) // (width // heads)
    ind = (col == lax.broadcasted_iota(jnp.int32, (width, 128), 1)).astype(F32)
    return ind, ind.T


def _ffn_fwd(x, g, wg, wu, wd, tag):
    h = _rms_fwd(x, g, "rms_" + tag)
    a, u, s = _ffn_up(h, wg, wu, "ffn_up_" + tag)
    return _ffn_down(s, wd, x, "ffn_down_" + tag), (h, a, u, s)


def _ffn_bwd(dy, x, g, wg, wu, wd, saved, tag):
    h, a, u, s = saved
    da, du = _ffn_bwd_hidden(dy, wd, a, u, "ffn_bwd_hidden_" + tag)
    dh = _ffn_bwd_input(da, du, wg, wu, "ffn_bwd_input_" + tag)
    dw_up = _ffn_grad_up(h, da, du, "ffn_grad_up_" + tag)
    dwd = _ffn_grad_down(s, dy, "ffn_grad_down_" + tag)
    dx, dg = _rms_bwd(x, g, dh, dy, "rms_bwd_" + tag)
    return dx, dg, dw_up, dwd


def _local_step(x, target, w, late):
    T = x.shape[0]
    ind16, ind16_t = _head_indicator(RW_W, RW_HEADS)
    ind4, ind4_t = _head_indicator(GLA_V, GLA_HEADS)
    ltri = jnp.tril(jnp.ones((CHUNK, CHUNK), F32))
    gate_pars = [w["w0"], w["w_w2"], w["a0"], w["w_a2"], w["w_g2"], w["k_k"], w["k_a"], ind16, ind16_t]
    post_pars = [w["lnx_w"], w["lnx_b"], w["r_k"], ind16, ind16_t]

    x1, ffn1 = _ffn_fwd(x, w["g1"], w["wg1"], w["wu1"], w["wd1"], "1")
    h2 = _rms_fwd(x1, w["g2"], "rms_mix")
    proj = lambda n, off, name: _matmul(h2, w["win"], "nn", T, n, D, 1024, 512, D, name, b_off=(0, off // 512))
    pg = proj(PG_W, 0, "proj_gate")
    pr = proj(PR_W, PG_W, "proj_rwkv")
    pa = proj(PA_W, PG_W + PR_W, "proj_gla")
    la = _gla_prep(pa, w["gla_w_a2"], w["gla_b_a"])
    o_raw, gla_states = _gla_core_fwd(pa, la, ltri)
    o_gla = _gla_post(o_raw, pa, w["gn"], ind4, ind4_t)
    rw, dec, k2, kk, b, g = _rw_prep(pr, w["mu"], gate_pars)
    y, rw_states, g_up2, g_down2 = _rw_core_fwd(rw, dec, k2, kk, b, gather=late)
    w = {**w, **_late_weights(g_up2, g_down2)}
    o_rw = _rw_post(y, rw, k2, g, post_pars)
    y_gla, y_rw, merged = _branch_merge(o_gla, o_rw, w["wb"], pg, w["gate_b"])
    x2 = _matmul(merged, w["wo"], "nn", T, D, D, 1024, 1024, D, "out_proj", res=x1)
    x3, ffn2 = _ffn_fwd(x2, w["g3"], w["wg2"], w["wu2"], w["wd2"], "2")
    dx3, loss, d_gf = _loss_bwd(x3, target, w["gf"])

    grads = {"gf": d_gf}
    dx2, grads["g3"], grads["up2"], grads["wd2"] = _ffn_bwd(
        dx3, x2, w["g3"], w["wg2"], w["wu2"], w["wd2"], ffn2, "2")
    dm = _matmul(dx2, w["wo"], "nt", T, D, D, 1024, 1024, D, "out_proj_bwd")
    grads["wo"] = _matmul(merged, dx2, "tn", D, D, T, 1024, 1024, 1024, "out_proj_grad", out_dtype=BF16)
    dy_gla, dy_rw, dpg, grads["gate_b"] = _merge_bwd(dm, y_gla, y_rw, pg, w["gate_b"])
    do_gla = _matmul(dy_gla, w["wb"], "nt", T, GLA_V, D, 1024, 1024, D, "branch_gla_bwd")
    do_rw = _matmul(dy_rw, w["wb"], "nt", T, RW_W, D, 1024, 1024, D, "branch_rwkv_bwd", b_off=(1, 0))
    grads["wb"] = jnp.concatenate([
        _matmul(o_gla, dy_gla, "tn", GLA_V, D, T, 1024, 1024, 1024, "branch_gla_grad", out_dtype=BF16),
        _matmul(o_rw, dy_rw, "tn", RW_W, D, T, 1024, 1024, 1024, "branch_rwkv_grad", out_dtype=BF16)], axis=0)
    dy, dr2, dv2, dk2b, dg, grads["lnx_w"], grads["lnx_b"], grads["r_k"] = _rw_post_bwd(y, rw, k2, g, post_pars, do_rw)
    early = _late_grad_parts(grads)
    dr1, dw, dk2a, dv1, dkk, db, *received = _rw_core_bwd(rw, dec, k2, kk, b, rw_states, dy, exchange=early)
    drw, grads["mu"], grads["w0"], grads["w_w2"], grads["a0"], grads["w_a2"], grads["w_g2"], grads["k_k"], grads["k_a"] = (
        _rw_prep_bwd(pr, w["mu"], gate_pars, (dr1, dr2), (dv1, dv2), dw, (dk2a, dk2b), dkk, db, dg))
    dpr = _shift_bwd(drw, w["mu"])
    do_raw, dr_gla, grads["gn"] = _gla_post_bwd(o_raw, pa, w["gn"], ind4, ind4_t, do_gla)
    dq, dk, dv, dla = _gla_core_bwd(pa, la, ltri, gla_states, do_raw)
    da_down, grads["gla_w_a2"], grads["gla_b_a"] = _gla_prep_bwd(pa, w["gla_w_a2"], w["gla_b_a"], dla)
    dpa = jnp.concatenate([dq.astype(BF16), dk.astype(BF16), dv.astype(BF16), dr_gla, da_down,
                           jnp.zeros((T, PA_W - PA_USED), BF16)], axis=1)
    dp = jnp.concatenate([dpg, dpr, dpa], axis=1)
    dh2 = _matmul(dp, w["win"], "nt", T, D, DIN_P, 1024, 1024, 1024, "proj_bwd")
    grads["win"] = _matmul(h2, dp, "tn", D, DIN_P, T, 1024, 1024, 1024, "proj_grad", out_dtype=BF16)
    dx1, grads["g2"] = _rms_bwd(x1, w["g2"], dh2, dx2, "rms_bwd_mix")
    dx, grads["g1"], grads["up1"], grads["wd1"] = _ffn_bwd(
        dx1, x, w["g1"], w["wg1"], w["wu1"], w["wd1"], ffn1, "1")
    return loss, dx, grads, received


BIG = ("ffn1_wg", "ffn1_wu", "ffn1_wd", "w_in", "w_branch", "w_out", "ffn2_wg", "ffn2_wu", "ffn2_wd")
SMALL_SHARDED = ("gla_w_a2", "rwkv_w_w2", "rwkv_w_a2", "rwkv_w_g2")
REPLICATED = ("ffn1_norm", "mix_norm", "gla_b_a", "gla_gn_w", "rwkv_mu", "rwkv_w0", "rwkv_a0", "rwkv_k_k", "rwkv_k_a",
              "rwkv_r_k", "rwkv_lnx_w", "rwkv_lnx_b", "gate_b", "ffn2_norm", "final_norm")
WEIGHTS = ("ffn1_norm", "ffn1_wg", "ffn1_wu", "ffn1_wd", "mix_norm", "w_in", "gla_w_a2", "gla_b_a", "gla_gn_w",
           "rwkv_mu", "rwkv_w0", "rwkv_w_w2", "rwkv_a0", "rwkv_w_a2", "rwkv_w_g2", "rwkv_k_k", "rwkv_k_a", "rwkv_r_k",
           "rwkv_lnx_w", "rwkv_lnx_b", "gate_b", "w_branch", "w_out", "ffn2_norm", "ffn2_wg", "ffn2_wu", "ffn2_wd",
           "final_norm")


def _unshard_cols(g):
    return jnp.transpose(g, (1, 0, 2)).reshape(g.shape[1], NDEV * g.shape[2])


def _shard_cols(a):
    return jnp.transpose(a.reshape(a.shape[0], NDEV, a.shape[1] // NDEV), (1, 0, 2))


def _pad_rows(a, rows):
    return jnp.pad(a, ((0, rows - a.shape[0]), (0, 0)))


def _align_rw(a):
    c = 3 * RW_W
    z = jnp.zeros((a.shape[0], LORA_P - DECAY_LORA), a.dtype)
    return jnp.concatenate([a[:, :c], a[:, c:c + DECAY_LORA], z, a[:, c + DECAY_LORA:c + 2 * DECAY_LORA], z,
                            a[:, c + 2 * DECAY_LORA:]], axis=1)


def _unalign_rw(a):
    c = 3 * RW_W
    return jnp.concatenate([a[:, :c + DECAY_LORA], a[:, c + LORA_P:c + LORA_P + AAA_LORA], a[:, c + 2 * LORA_P:]], axis=1)


def _align_proj(a):
    gla = jnp.pad(a[:, :GLA_IN], ((0, 0), (0, PA_W - GLA_IN)))
    return jnp.concatenate([a[:, GLA_IN + RW_IN:], _align_rw(a[:, GLA_IN:GLA_IN + RW_IN]), gla], axis=1)


def _unalign_proj(a):
    return jnp.concatenate([a[:, PG_W + PR_W:PG_W + PR_W + GLA_IN], _unalign_rw(a[:, PG_W:PG_W + PR_W]), a[:, :PG_W]], axis=1)


def _layout_weights(gb, gs, rep):
    row = lambda n: rep[n].reshape(1, -1)
    return {
        "wg1": gb["ffn1_wg"], "wu1": gb["ffn1_wu"], "wd1": gb["ffn1_wd"],
        "win": _align_proj(_unshard_cols(gb["w_in"])),
        "g1": row("ffn1_norm"), "g2": row("mix_norm"), "g3": row("ffn2_norm"), "gf": row("final_norm"),
        "gla_w_a2": _pad_rows(_unshard_cols(gs["gla_w_a2"]), LORA_P), "gla_b_a": row("gla_b_a"),
        "gn": jnp.tile(row("gla_gn_w"), (1, GLA_HEADS)),
        "mu": _align_rw(row("rwkv_mu")), "w0": row("rwkv_w0"), "a0": row("rwkv_a0"),
        "w_w2": _pad_rows(_unshard_cols(gs["rwkv_w_w2"]), LORA_P),
        "w_a2": _pad_rows(_unshard_cols(gs["rwkv_w_a2"]), LORA_P),
        "w_g2": _unshard_cols(gs["rwkv_w_g2"]),
        "k_k": row("rwkv_k_k"), "k_a": row("rwkv_k_a"), "r_k": row("rwkv_r_k"),
        "lnx_w": row("rwkv_lnx_w"), "lnx_b": row("rwkv_lnx_b"), "gate_b": row("gate_b"),
    }


LATE_ROWS = (("ffn2_wd", FSH), ("w_branch", (GLA_V + RW_W) // NDEV), ("w_out", D // NDEV))


def _late_weights(g_up, g_down):
    r1, r2 = LATE_ROWS[0][1], LATE_ROWS[0][1] + LATE_ROWS[1][1]
    return {"wg2": g_up[:, 0], "wu2": g_up[:, 1], "wd2": g_down[:, :r1],
            "wb": g_down[:, r1:r2].reshape(GLA_V + RW_W, D), "wo": g_down[:, r2:].reshape(D, D)}


def _late_grad_parts(g):
    return [g["up2"], jnp.concatenate([g["wd2"], g["wb"].reshape(NDEV, -1, D), g["wo"].reshape(NDEV, -1, D)], axis=1)]


def _layout_grads(g):
    big = [g["up1"], g["wd1"], _shard_cols(_unalign_proj(g["win"]))]
    small = {
        "ffn1_norm": g["g1"], "mix_norm": g["g2"], "ffn2_norm": g["g3"], "final_norm": g["gf"],
        "gla_w_a2": g["gla_w_a2"][:GLA_LORA], "gla_b_a": g["gla_b_a"],
        "gla_gn_w": jnp.sum(g["gn"].reshape(GLA_HEADS, GLA_DV), axis=0, keepdims=True),
        "rwkv_mu": _unalign_rw(g["mu"]), "rwkv_w0": g["w0"], "rwkv_a0": g["a0"],
        "rwkv_w_w2": g["w_w2"][:DECAY_LORA], "rwkv_w_a2": g["w_a2"][:AAA_LORA], "rwkv_w_g2": g["w_g2"],
        "rwkv_k_k": g["k_k"], "rwkv_k_a": g["k_a"], "rwkv_r_k": g["r_k"],
        "rwkv_lnx_w": g["lnx_w"], "rwkv_lnx_b": g["lnx_b"], "gate_b": g["gate_b"],
    }
    return big, small


_MESH = pl.DeviceIdType.MESH
_ANY = pl.BlockSpec(memory_space=pl.ANY)


def _position():
    return lax.axis_index("x"), lax.axis_index("y"), lax.axis_index("c")


def _slot(p):
    return 4 * p[0] + 2 * p[1] + p[2]


def _comm_sems(n):
    if not n:
        return []
    return [pltpu.SemaphoreType.DMA((7 * n,)), pltpu.SemaphoreType.DMA((7 * n,)), pltpu.SemaphoreType.DMA((n,))]


def _gather_plan(ins, outs, send_sems, recv_sems, local_sems):
    n = len(ins)
    x, y, c = _position()
    me, sibling = (x, y, c), (x, y, 1 - c)
    chips = [(1 - x, y), (x, 1 - y), (1 - x, 1 - y)]

    def copy(a, k, block, to, src=None):
        dst = outs[a].at[_slot(block)]
        return pltpu.make_async_remote_copy(
            src_ref=dst if src is None else src, dst_ref=dst, send_sem=send_sems.at[7 * a + k],
            recv_sem=recv_sems.at[7 * a + k], device_id=to, device_id_type=_MESH)

    def local(a):
        return pltpu.make_async_copy(ins[a], outs[a].at[_slot(me)], local_sems.at[a])

    def own(a):
        return [copy(a, 0, me, sibling, src=ins[a])] + [copy(a, 1 + j, me, (*chip, c), src=ins[a]) for j, chip in enumerate(chips)]

    def start():
        for a in range(n):
            local(a).start()
            for cp in own(a):
                cp.start()

    def forward():
        for a in range(n):
            for j, chip in enumerate(chips):
                copy(a, 1 + j, (*chip, c), me).wait_recv()
                copy(a, 4 + j, (*chip, c), sibling).start()

    def finish():
        for a in range(n):
            copy(a, 0, sibling, me).wait_recv()
            for j, chip in enumerate(chips):
                copy(a, 4 + j, (*chip, 1 - c), me).wait_recv()
        for a in range(n):
            for cp in own(a) + [copy(a, 4 + j, (*chip, c), sibling) for j, chip in enumerate(chips)]:
                cp.wait_send()
            local(a).wait()

    return start, forward, finish


def _exchange_plan(ins, outs, send_sems, recv_sems, local_sems):
    n = len(ins)
    x, y, c = _position()
    me = (x, y, c)
    flip = lambda v, f: 1 - v if f else v
    peers = [(flip(x, fx), flip(y, fy), flip(c, fc))
             for fx, fy, fc in ((0, 0, 1), (1, 0, 0), (0, 1, 0), (1, 1, 0), (1, 0, 1), (0, 1, 1), (1, 1, 1))]

    def copy(a, k, src_slot, dst_slot):
        return pltpu.make_async_remote_copy(
            src_ref=ins[a].at[src_slot], dst_ref=outs[a].at[dst_slot], send_sem=send_sems.at[7 * a + k],
            recv_sem=recv_sems.at[7 * a + k], device_id=peers[k], device_id_type=_MESH)

    def local(a):
        return pltpu.make_async_copy(ins[a].at[_slot(me)], outs[a].at[_slot(me)], local_sems.at[a])

    def start():
        for a in range(n):
            local(a).start()
            for k, peer in enumerate(peers):
                copy(a, k, _slot(peer), _slot(me)).start()

    def finish():
        for a in range(n):
            for k, peer in enumerate(peers):
                copy(a, k, _slot(peer), _slot(peer)).wait_recv()
        for a in range(n):
            for k, peer in enumerate(peers):
                copy(a, k, _slot(peer), _slot(me)).wait_send()
            local(a).wait()

    return start, finish


def _all_gather(arrays, name):
    n = len(arrays)

    def body(*refs):
        start, forward, finish = _gather_plan(refs[:n], refs[n:2 * n], *refs[2 * n:])
        start()
        forward()
        finish()

    return pl.pallas_call(
        body, name=name, in_specs=[_ANY] * n, out_specs=[_ANY] * n,
        out_shape=[jax.ShapeDtypeStruct((NDEV,) + a.shape, a.dtype) for a in arrays], scratch_shapes=_comm_sems(n),
    )(*arrays)


def _exchange(arrays, name):
    n = len(arrays)

    def body(*refs):
        start, finish = _exchange_plan(refs[:n], refs[n:2 * n], *refs[2 * n:])
        start()
        finish()

    return pl.pallas_call(
        body, name=name, in_specs=[_ANY] * n, out_specs=[_ANY] * n,
        out_shape=[jax.ShapeDtypeStruct(a.shape, a.dtype) for a in arrays], scratch_shapes=_comm_sems(n),
    )(*arrays)


def _adamw_math(w, g, m, v):
    m = ADAM_B1 * m + (1.0 - ADAM_B1) * g
    v = ADAM_B2 * v + (1.0 - ADAM_B2) * (g * g)
    m_hat = m / (1.0 - ADAM_B1 ** ADAM_STEP)
    v_hat = v / (1.0 - ADAM_B2 ** ADAM_STEP)
    delta = -ADAM_LR * (m_hat / (jnp.sqrt(v_hat) + ADAM_EPS) + ADAM_WD * w)
    return delta, m, v


def _sum_slots(ref):
    total = ref[0].astype(F32)
    for s in range(1, NDEV):
        total = total + ref[s].astype(F32)
    return total


def _adamw(parts, w, m, v, tr, name, stack_index=None, row_block_offset=0):
    R, C = w.shape

    def body(p_ref, w_ref, m_ref, v_ref, g_ref, d_ref, nm_ref, nv_ref):
        g = _sum_slots(p_ref)
        g_ref[...] = g
        d_ref[...], nm_ref[...], nv_ref[...] = _adamw_math(w_ref[...], g, m_ref[...], v_ref[...])

    if stack_index is None:
        p_spec = pl.BlockSpec((NDEV, tr, C), lambda r: (0, row_block_offset + r, 0))
    else:
        p_spec = pl.BlockSpec((NDEV, None, tr, C), lambda r: (0, stack_index, r, 0))
    blk = pl.BlockSpec((tr, C), lambda r: (r, 0))
    out = jax.ShapeDtypeStruct((R, C), F32)
    return pl.pallas_call(
        body, name=name, grid=(R // tr,), in_specs=[p_spec, blk, blk, blk], out_specs=[blk] * 4, out_shape=(out,) * 4,
        compiler_params=_params(("parallel",)),
    )(parts, w, m, v)


def _sum_gathered(parts):
    _, R, C = parts.shape

    def body(p_ref, o_ref):
        o_ref[...] = _sum_slots(p_ref)

    return pl.pallas_call(body, name="small_grad_sum", out_shape=jax.ShapeDtypeStruct((R, C), F32),
                          compiler_params=_params())(parts)


def _adamw_small(w, g, m, v):
    def body(w_ref, g_ref, m_ref, v_ref, d_ref, nm_ref, nv_ref):
        d_ref[...], nm_ref[...], nv_ref[...] = _adamw_math(w_ref[...], g_ref[...], m_ref[...], v_ref[...])

    out = jax.ShapeDtypeStruct(w.shape, F32)
    return pl.pallas_call(body, name="adamw_small", out_shape=(out,) * 3, compiler_params=_params())(w, g, m, v)


def _pack(pieces, rows):
    flat = jnp.concatenate([p.reshape(-1) for p in pieces])
    return jnp.pad(flat, (0, rows * 128 - flat.shape[0])).reshape(rows, 128)


def _unpack(packed, shapes):
    flat = packed.reshape(-1)
    out, off = [], 0
    for s in shapes:
        size = 1
        for d in s:
            size *= d
        out.append(flat[off:off + size].reshape(s))
        off += size
    return out


def _rows_for(shapes, extra=0):
    total = extra
    for s in shapes:
        size = 1
        for d in s:
            size *= d
        total += size
    return -(-total // 1024) * 8


def kernel(x, ffn1_norm, ffn1_wg, ffn1_wu, ffn1_wd, mix_norm, w_in, gla_w_a2, gla_b_a, gla_gn_w, rwkv_mu, rwkv_w0, rwkv_w_w2, rwkv_a0, rwkv_w_a2, rwkv_w_g2, rwkv_k_k, rwkv_k_a, rwkv_r_k, rwkv_lnx_w, rwkv_lnx_b, gate_b, w_branch, w_out, ffn2_norm, ffn2_wg, ffn2_wu, ffn2_wd, final_norm, loss_target, m_ffn1_norm, m_ffn1_wg, m_ffn1_wu, m_ffn1_wd, m_mix_norm, m_w_in, m_gla_w_a2, m_gla_b_a, m_gla_gn_w, m_rwkv_mu, m_rwkv_w0, m_rwkv_w_w2, m_rwkv_a0, m_rwkv_w_a2, m_rwkv_w_g2, m_rwkv_k_k, m_rwkv_k_a, m_rwkv_r_k, m_rwkv_lnx_w, m_rwkv_lnx_b, m_gate_b, m_w_branch, m_w_out, m_ffn2_norm, m_ffn2_wg, m_ffn2_wu, m_ffn2_wd, m_final_norm, v_ffn1_norm, v_ffn1_wg, v_ffn1_wu, v_ffn1_wd, v_mix_norm, v_w_in, v_gla_w_a2, v_gla_b_a, v_gla_gn_w, v_rwkv_mu, v_rwkv_w0, v_rwkv_w_w2, v_rwkv_a0, v_rwkv_w_a2, v_rwkv_w_g2, v_rwkv_k_k, v_rwkv_k_a, v_rwkv_r_k, v_rwkv_lnx_w, v_rwkv_lnx_b, v_gate_b, v_w_branch, v_w_out, v_ffn2_norm, v_ffn2_wg, v_ffn2_wu, v_ffn2_wd, v_final_norm):
    wts = dict(zip(WEIGHTS, (ffn1_norm, ffn1_wg, ffn1_wu, ffn1_wd, mix_norm, w_in, gla_w_a2, gla_b_a, gla_gn_w, rwkv_mu, rwkv_w0, rwkv_w_w2, rwkv_a0, rwkv_w_a2, rwkv_w_g2, rwkv_k_k, rwkv_k_a, rwkv_r_k, rwkv_lnx_w, rwkv_lnx_b, gate_b, w_branch, w_out, ffn2_norm, ffn2_wg, ffn2_wu, ffn2_wd, final_norm)))
    mom = dict(zip(WEIGHTS, (m_ffn1_norm, m_ffn1_wg, m_ffn1_wu, m_ffn1_wd, m_mix_norm, m_w_in, m_gla_w_a2, m_gla_b_a, m_gla_gn_w, m_rwkv_mu, m_rwkv_w0, m_rwkv_w_w2, m_rwkv_a0, m_rwkv_w_a2, m_rwkv_w_g2, m_rwkv_k_k, m_rwkv_k_a, m_rwkv_r_k, m_rwkv_lnx_w, m_rwkv_lnx_b, m_gate_b, m_w_branch, m_w_out, m_ffn2_norm, m_ffn2_wg, m_ffn2_wu, m_ffn2_wd, m_final_norm)))
    var = dict(zip(WEIGHTS, (v_ffn1_norm, v_ffn1_wg, v_ffn1_wu, v_ffn1_wd, v_mix_norm, v_w_in, v_gla_w_a2, v_gla_b_a, v_gla_gn_w, v_rwkv_mu, v_rwkv_w0, v_rwkv_w_w2, v_rwkv_a0, v_rwkv_w_a2, v_rwkv_w_g2, v_rwkv_k_k, v_rwkv_k_a, v_rwkv_r_k, v_rwkv_lnx_w, v_rwkv_lnx_b, v_gate_b, v_w_branch, v_w_out, v_ffn2_norm, v_ffn2_wg, v_ffn2_wu, v_ffn2_wd, v_final_norm)))
    two = lambda a: a.reshape(a.shape[-2:])

    bf = lambda n: two(wts[n]).astype(BF16)
    up1 = jnp.stack([bf("ffn1_wg"), bf("ffn1_wu")])
    lora = jnp.concatenate([jnp.pad(two(gla_w_a2), ((0, 0), (0, 128 - GLA_QK // NDEV)))] +
                           [two(wts[n]) for n in SMALL_SHARDED[1:]], axis=0)
    g_up1, g_down1, g_proj, g_lora = _all_gather([up1, bf("ffn1_wd"), bf("w_in"), lora], "gather_weights")
    gb = {"ffn1_wg": g_up1[:, 0], "ffn1_wu": g_up1[:, 1], "ffn1_wd": g_down1, "w_in": g_proj}
    gs = {"gla_w_a2": g_lora[:, :GLA_LORA, :GLA_QK // NDEV]}
    row = GLA_LORA
    for n in SMALL_SHARDED[1:]:
        gs[n] = g_lora[:, row:row + wts[n].shape[1]]
        row += wts[n].shape[1]
    w = _layout_weights(gb, gs, {n: wts[n] for n in REPLICATED})
    late = [jnp.stack([bf("ffn2_wg"), bf("ffn2_wu")]), jnp.concatenate([bf(n) for n, _ in LATE_ROWS], axis=0)]

    loss_part, grad_x, grads, (r_up2, r_down2) = _local_step(x[0], loss_target[0], w, late)
    big, small = _layout_grads(grads)

    r_up1, r_down1, r_proj = _exchange(big, "exchange_grads")
    result = {}
    state = lambda n: (two(wts[n]), two(mom[n]), two(var[n]))
    for parts, names in ((r_up1, ("ffn1_wg", "ffn1_wu")), (r_up2, ("ffn2_wg", "ffn2_wu"))):
        for i, n in enumerate(names):
            result[n] = _adamw(parts, *state(n), 256, "adamw_" + n, stack_index=i)
    result["ffn1_wd"] = _adamw(r_down1, *state("ffn1_wd"), 64, "adamw_ffn1_wd")
    row = 0
    for n, rows in LATE_ROWS:
        result[n] = _adamw(r_down2, *state(n), 64, "adamw_" + n, row_block_offset=row // 64)
        row += rows
    result["w_in"] = _adamw(r_proj, *state("w_in"), 256, "adamw_w_in")

    small_names = [n for n in WEIGHTS if n not in BIG]
    full_shapes = [small[n].shape for n in small_names]
    rows_full = _rows_for(full_shapes, extra=128)
    packed = _pack([small[n] for n in small_names] + [loss_part], rows_full)
    (gathered,) = _all_gather([packed], "gather_small_grads")
    total = _sum_gathered(gathered)
    *full_grads, loss_row = _unpack(total, full_shapes + [(1, 128)])
    me = _slot(_position())
    own = {}
    for n, g in zip(small_names, full_grads):
        if n in SMALL_SHARDED:
            cols = wts[n].shape[-1]
            g = lax.dynamic_slice_in_dim(g, me * cols, cols, axis=1)
        own[n] = g.reshape(wts[n].shape)
    own_shapes = [wts[n].shape for n in small_names]
    rows_own = _rows_for(own_shapes)
    pk = lambda d: _pack([d[n] for n in small_names], rows_own)
    d_s, m_s, v_s = _adamw_small(pk(wts), pk(own), pk(mom), pk(var))
    for n, d, m, v in zip(small_names, _unpack(d_s, own_shapes), _unpack(m_s, own_shapes), _unpack(v_s, own_shapes)):
        result[n] = (own[n], d, m, v)

    shaped = lambda n, k: result[n][k].reshape(wts[n].shape)
    return (loss_row[0, 0], grad_x[None],
            *[shaped(n, 0) for n in WEIGHTS], *[shaped(n, 1) for n in WEIGHTS],
            *[shaped(n, 2) for n in WEIGHTS], *[shaped(n, 3) for n in WEIGHTS])
```

```python
import functools

import jax
import jax.numpy as jnp
from jax import lax
from jax.experimental import pallas as pl
from jax.experimental.pallas import tpu as pltpu

F32 = jnp.float32
BF16 = jnp.bfloat16
HI = lax.Precision.HIGHEST

NDEV = 8
D = 2048
DFF = 5632
FSH = DFF // NDEV
CHUNK = 64
GLA_HEADS, GLA_DK, GLA_DV = 4, 128, 256
GLA_QK, GLA_V, GLA_LORA, GLA_TAU = 512, 1024, 16, 16.0
RW_HEADS, RW_HD, RW_W = 16, 64, 1024
DECAY_LORA, AAA_LORA, GATE_LORA = 96, 96, 256
GN_EPS = 64e-5
NORM_EPS = 1e-6
GLA_IN = 2 * GLA_QK + 2 * GLA_V + GLA_LORA
RW_IN = 3 * RW_W + DECAY_LORA + AAA_LORA + GATE_LORA
D_IN = GLA_IN + RW_IN + 2 * D
DIN_SH = D_IN // NDEV
PG_W = 2 * D
PR_W = 3584
PA_W = 3584
PA_USED = 2 * GLA_QK + 2 * GLA_V + 128
DIN_P = PG_W + PR_W + PA_W
LORA_P = 128

ADAM_LR, ADAM_B1, ADAM_B2, ADAM_EPS, ADAM_WD, ADAM_STEP = 0.001, 0.9, 0.999, 1e-08, 0.01, 10

VMEM_LIMIT = 56 * 1024 * 1024
RW_TB = 128
RW_G = 16
RW_NP = 4


def _params(sem=None, vmem=VMEM_LIMIT):
    return pltpu.CompilerParams(dimension_semantics=sem, vmem_limit_bytes=vmem)


def _pair_mask():
    return lax.broadcasted_iota(jnp.int32, (RW_HD, 2 * RW_HD), 1) < RW_HD


def _pair_rowsum(p, mask):
    tot = jnp.sum(p, axis=1, keepdims=True)
    first = jnp.sum(jnp.where(mask, p, 0.0), axis=1, keepdims=True)
    return first, tot - first


def _split_transposed(x_ref, q, dst_ref, base):
    xt = x_ref[:, 128 * q:128 * (q + 1)].T
    for g in range(RW_TB // RW_G):
        dst_ref[base + g, :, 0:RW_G] = xt[:, g * RW_G:(g + 1) * RW_G]


def _pair_column(tile_ref, idx, i, mask):
    return jnp.where(mask, tile_ref[idx, 0:RW_HD, i:i + 1], tile_ref[idx, RW_HD:, i:i + 1])


def _rw_core_fwd(rw, w, k2, kk, b, gather=()):
    T = rw.shape[0]
    nb = T // RW_TB
    ng = RW_TB // RW_G
    NP = RW_NP
    nc = len(gather)
    npair = RW_HEADS // 2 // NP

    def body(r_ref, v_ref, w_ref, k_ref, kk_ref, b_ref, *rest):
        g_in, (y_ref, st_ref), g_out = rest[:nc], rest[nc:nc + 2], rest[nc + 2:2 * nc + 2]
        s_scr, vt_scr, yt_scr, rows_scr = rest[2 * nc + 2:2 * nc + 6]
        pair, blk_i = pl.program_id(0), pl.program_id(1)
        if nc:
            start, forward, finish = _gather_plan(g_in, g_out, *rest[2 * nc + 6:])
            pl.when((pair == 0) & (blk_i == 0))(start)
            pl.when((pair == 0) & (blk_i == nb // 2))(forward)

        @pl.when(pl.program_id(1) == 0)
        def _():
            s_scr[...] = jnp.zeros_like(s_scr)
            yt_scr[...] = jnp.zeros_like(yt_scr)

        mask = _pair_mask()
        for q in range(NP):
            _split_transposed(v_ref, q, vt_scr, q * ng)
        R_, W_, K_, KK_, B_ = range(5)
        for a, ref in enumerate((r_ref, w_ref, k_ref, kk_ref, b_ref)):
            for q in range(NP):
                rows_scr[a * NP + q] = ref[:, 128 * q:128 * (q + 1)]

        def group(g, states):
            states = list(states)
            for i in range(RW_G):
                t = g * RW_G + i
                row = lambda a, q: rows_scr[a * NP + q, pl.ds(t, 1), :]
                sums = [_pair_rowsum(states[q] * row(KK_, q), mask) for q in range(NP)]
                for q in range(NP):
                    sa = jnp.where(mask, *sums[q])
                    states[q] = (states[q] * row(W_, q) - sa * row(B_, q)
                                 + _pair_column(vt_scr, q * ng + g, i, mask) * row(K_, q))
                    st_ref[q, t] = states[q]
                outs = [_pair_rowsum(states[q] * row(R_, q), mask) for q in range(NP)]
                for q in range(NP):
                    yt_scr[q * ng + g, 0:RW_HD, i:i + 1] = outs[q][0]
                    yt_scr[q * ng + g, RW_HD:, i:i + 1] = outs[q][1]
            return tuple(states)

        states = lax.fori_loop(0, ng, group, tuple(s_scr[q] for q in range(NP)))
        for q in range(NP):
            s_scr[q] = states[q]
            for g in range(ng):
                y_ref[g * RW_G:(g + 1) * RW_G, 128 * q:128 * (q + 1)] = yt_scr[q * ng + g].T[0:RW_G, :]
        if nc:
            pl.when((pair == npair - 1) & (blk_i == nb - 1))(finish)

    blk = lambda cb: pl.BlockSpec((RW_TB, 128 * NP), lambda p, i, cb=cb: (i, cb + p))
    tiles = pltpu.VMEM((NP * ng, 128, 128), F32)
    return pl.pallas_call(
        body, name="rw_core_fwd", grid=(npair, nb),
        in_specs=[blk(0), blk(2 * RW_W // (128 * NP)), blk(0), blk(0), blk(0), blk(0)] + [_ANY] * nc,
        out_specs=[blk(0), pl.BlockSpec((NP, RW_TB, RW_HD, 128), lambda p, i: (p, i, 0, 0))] + [_ANY] * nc,
        out_shape=[jax.ShapeDtypeStruct((T, RW_W), F32), jax.ShapeDtypeStruct((RW_HEADS // 2, T, RW_HD, 128), F32)]
        + [jax.ShapeDtypeStruct((NDEV,) + a.shape, a.dtype) for a in gather],
        scratch_shapes=[pltpu.VMEM((NP, RW_HD, 128), F32), tiles, tiles, pltpu.VMEM((5 * NP, RW_TB, 128), F32)]
        + _comm_sems(nc),
        compiler_params=_params(("arbitrary", "arbitrary")),
    )(rw, rw, w, k2, kk, b, *gather)


def _rw_core_bwd(rw, w, k2, kk, b, states, dy, exchange=()):
    T = rw.shape[0]
    nb = T // RW_TB
    ng = RW_TB // RW_G
    NP = RW_NP
    nc = len(exchange)
    npair = RW_HEADS // 2 // NP

    def body(r_ref, v_ref, w_ref, k_ref, kk_ref, b_ref, dy_ref, st_ref, sp_ref, *rest):
        e_in, e_out = rest[:nc], rest[nc + 6:2 * nc + 6]
        dr_ref, dw_ref, dk_ref, dv_ref, dkk_ref, db_ref = rest[nc:nc + 6]
        ds_scr, vt_scr, dyt_scr, dvt_scr, rows_scr, out_scr = rest[2 * nc + 6:2 * nc + 12]
        step = pl.program_id(1)
        if nc:
            start, finish = _exchange_plan(e_in, e_out, *rest[2 * nc + 12:])
            pl.when((pl.program_id(0) == 0) & (step == 0))(start)

        @pl.when(step == 0)
        def _():
            ds_scr[...] = jnp.zeros_like(ds_scr)
            dvt_scr[...] = jnp.zeros_like(dvt_scr)

        mask = _pair_mask()
        for q in range(NP):
            _split_transposed(v_ref, q, vt_scr, q * ng)
            _split_transposed(dy_ref, q, dyt_scr, q * ng)
        R_, W_, K_, KK_, B_ = range(5)
        for a, ref in enumerate((r_ref, w_ref, k_ref, kk_ref, b_ref)):
            for q in range(NP):
                rows_scr[a * NP + q] = ref[:, 128 * q:128 * (q + 1)]

        def group(gg, grads):
            g = ng - 1 - gg
            grads = list(grads)
            pairs = range(NP)
            for i in reversed(range(RW_G)):
                t = g * RW_G + i
                row = lambda a, q: rows_scr[a * NP + q, pl.ds(t, 1), :]

                def put(a, q, value):
                    out_scr[a * NP + q, pl.ds(t, 1), :] = value

                s_old = [st_ref[q, jnp.maximum(t - 1, 0)] for q in pairs]
                if i == 0:
                    s_old = [jnp.where(g == 0, jnp.where(step == nb - 1, 0.0, sp_ref[q, 0]), s_old[q]) for q in pairs]
                dycol = [_pair_column(dyt_scr, q * ng + g, i, mask) for q in pairs]
                dS = [grads[q] + dycol[q] * row(R_, q) for q in pairs]
                m = [_pair_rowsum(dS[q] * row(B_, q), mask) for q in pairs]
                sa = [_pair_rowsum(s_old[q] * row(KK_, q), mask) for q in pairs]
                dv = [_pair_rowsum(dS[q] * row(K_, q), mask) for q in pairs]
                for q in pairs:
                    put(R_, q, jnp.sum(st_ref[q, t] * dycol[q], axis=0, keepdims=True))
                    put(W_, q, jnp.sum(dS[q] * s_old[q], axis=0, keepdims=True))
                    put(K_, q, jnp.sum(dS[q] * _pair_column(vt_scr, q * ng + g, i, mask), axis=0, keepdims=True))
                for q in pairs:
                    dsa = -jnp.where(mask, *m[q])
                    grads[q] = dS[q] * row(W_, q) + dsa * row(KK_, q)
                    put(KK_, q, jnp.sum(s_old[q] * dsa, axis=0, keepdims=True))
                    put(B_, q, -jnp.sum(dS[q] * jnp.where(mask, *sa[q]), axis=0, keepdims=True))
                    dvt_scr[q * ng + g, 0:RW_HD, i:i + 1] = dv[q][0]
                    dvt_scr[q * ng + g, RW_HD:, i:i + 1] = dv[q][1]
            return tuple(grads)

        grads = lax.fori_loop(0, ng, group, tuple(ds_scr[q] for q in range(NP)))
        for q in range(NP):
            ds_scr[q] = grads[q]
            for a, ref in enumerate((dr_ref, dw_ref, dk_ref, dkk_ref, db_ref)):
                ref[:, 128 * q:128 * (q + 1)] = out_scr[a * NP + q]
            for g in range(ng):
                dv_ref[g * RW_G:(g + 1) * RW_G, 128 * q:128 * (q + 1)] = dvt_scr[q * ng + g].T[0:RW_G, :]
        if nc:
            pl.when((pl.program_id(0) == npair - 1) & (step == nb - 1))(finish)

    blk = lambda cb: pl.BlockSpec((RW_TB, 128 * NP), lambda p, i, cb=cb: (nb - 1 - i, cb + p))
    st_spec = pl.BlockSpec((NP, RW_TB, RW_HD, 128), lambda p, i: (p, nb - 1 - i, 0, 0))
    sp_spec = pl.BlockSpec((NP, 1, RW_HD, 128), lambda p, i: (p, jnp.maximum((nb - 1 - i) * RW_TB - 1, 0), 0, 0))
    out = jax.ShapeDtypeStruct((T, RW_W), F32)
    tiles = pltpu.VMEM((NP * ng, 128, 128), F32)
    return pl.pallas_call(
        body, name="rw_core_bwd", grid=(npair, nb),
        in_specs=[blk(0), blk(2 * RW_W // (128 * NP)), blk(0), blk(0), blk(0), blk(0), blk(0), st_spec, sp_spec]
        + [_ANY] * nc,
        out_specs=[blk(0)] * 6 + [_ANY] * nc,
        out_shape=[out] * 6 + [jax.ShapeDtypeStruct(a.shape, a.dtype) for a in exchange],
        scratch_shapes=[pltpu.VMEM((NP, RW_HD, 128), F32), tiles, tiles, tiles,
                        pltpu.VMEM((5 * NP, RW_TB, 128), F32), pltpu.VMEM((5 * NP, RW_TB, 128), F32)] + _comm_sems(nc),
        compiler_params=_params(("arbitrary", "arbitrary")),
    )(rw, rw, w, k2, kk, b, dy, states, states, *exchange)


GLA_CB = 8


def _gla_chunk(s_t, q, k, v, la, ltri):
    cum = jnp.dot(ltri, la, precision=HI, preferred_element_type=F32)
    total = jnp.sum(la, axis=0, keepdims=True)
    kdec = k * jnp.exp(total - cum)
    u_t = _bdot(v, kdec, _TN)
    s_t = jnp.exp(total) * s_t + u_t
    o = _bdot(q * (GLA_DK ** -0.5), s_t, _NT)
    return s_t, o


def _gla_core_fwd(pa, la, ltri):
    T = pa.shape[0]
    cb = min(GLA_CB, T // CHUNK)
    rows = cb * CHUNK
    nsteps = T // rows

    def body(q_ref, k_ref, v_ref, la_ref, ltri_ref, o_ref, st_ref, s_scr):
        @pl.when(pl.program_id(1) == 0)
        def _():
            s_scr[...] = jnp.zeros_like(s_scr)

        def chunk(c, s_t):
            sl = pl.ds(pl.multiple_of(c * CHUNK, CHUNK), CHUNK)
            s_t, o = _gla_chunk(s_t, q_ref[sl, :], k_ref[sl, :], v_ref[sl, :], la_ref[sl, :], ltri_ref[...])
            o_ref[sl, :] = o
            st_ref[0, c] = s_t
            return s_t

        s_scr[...] = lax.fori_loop(0, cb, chunk, s_scr[...])

    qk = lambda off: pl.BlockSpec((rows, GLA_DK), lambda h, i, off=off: (i, off + h))
    vspec = pl.BlockSpec((rows, GLA_DV), lambda h, i: (i, 2 * GLA_QK // GLA_DV + h))
    return pl.pallas_call(
        body, name="gla_core_fwd", grid=(GLA_HEADS, nsteps),
        in_specs=[qk(0), qk(GLA_HEADS), vspec, qk(0), pl.BlockSpec((CHUNK, CHUNK), lambda h, i: (0, 0))],
        out_specs=[pl.BlockSpec((rows, GLA_DV), lambda h, i: (i, h)),
                   pl.BlockSpec((1, cb, GLA_DV, GLA_DK), lambda h, i: (h, i, 0, 0))],
        out_shape=(jax.ShapeDtypeStruct((T, GLA_V), F32),
                   jax.ShapeDtypeStruct((GLA_HEADS, T // CHUNK, GLA_DV, GLA_DK), F32)),
        scratch_shapes=[pltpu.VMEM((GLA_DV, GLA_DK), F32)],
        compiler_params=_params(("arbitrary", "arbitrary")),
    )(pa, pa, pa, la, ltri)


def _gla_core_bwd(pa, la, ltri, states, do):
    T = pa.shape[0]
    cb = min(GLA_CB, T // CHUNK)
    rows = cb * CHUNK
    nsteps = T // rows

    def body(q_ref, k_ref, v_ref, la_ref, ltri_ref, st_ref, sp_ref, do_ref,
             dq_ref, dk_ref, dv_ref, dla_ref, ds_scr):
        step = pl.program_id(1)

        @pl.when(step == 0)
        def _():
            ds_scr[...] = jnp.zeros_like(ds_scr)

        s_before = jnp.where(step == nsteps - 1, 0.0, sp_ref[0, 0])

        def chunk(cc, ds_t):
            c = cb - 1 - cc
            sl = pl.ds(pl.multiple_of(c * CHUNK, CHUNK), CHUNK)
            s_prev = jnp.where(c == 0, s_before, st_ref[0, jnp.maximum(c - 1, 0)])
            _, vjp = jax.vjp(functools.partial(_gla_chunk, ltri=ltri_ref[...]),
                             s_prev, q_ref[sl, :], k_ref[sl, :], v_ref[sl, :], la_ref[sl, :])
            ds_prev, dq, dk, dv, dla = vjp((ds_t, do_ref[sl, :]))
            dq_ref[sl, :] = dq
            dk_ref[sl, :] = dk
            dv_ref[sl, :] = dv
            dla_ref[sl, :] = dla
            return ds_prev

        ds_scr[...] = lax.fori_loop(0, cb, chunk, ds_scr[...])

    r = lambda i: nsteps - 1 - i
    qk = lambda off: pl.BlockSpec((rows, GLA_DK), lambda h, i, off=off: (r(i), off + h))
    vspec = pl.BlockSpec((rows, GLA_DV), lambda h, i: (r(i), 2 * GLA_QK // GLA_DV + h))
    o128 = pl.BlockSpec((rows, GLA_DK), lambda h, i: (r(i), h))
    o256 = pl.BlockSpec((rows, GLA_DV), lambda h, i: (r(i), h))
    return pl.pallas_call(
        body, name="gla_core_bwd", grid=(GLA_HEADS, nsteps),
        in_specs=[qk(0), qk(GLA_HEADS), vspec, qk(0), pl.BlockSpec((CHUNK, CHUNK), lambda h, i: (0, 0)),
                  pl.BlockSpec((1, cb, GLA_DV, GLA_DK), lambda h, i: (h, r(i), 0, 0)),
                  pl.BlockSpec((1, 1, GLA_DV, GLA_DK), lambda h, i: (h, jnp.maximum(r(i) * cb - 1, 0), 0, 0)),
                  o256],
        out_specs=[o128, o128, o256, o128],
        out_shape=(jax.ShapeDtypeStruct((T, GLA_QK), F32), jax.ShapeDtypeStruct((T, GLA_QK), F32),
                   jax.ShapeDtypeStruct((T, GLA_V), F32), jax.ShapeDtypeStruct((T, GLA_QK), F32)),
        scratch_shapes=[pltpu.VMEM((GLA_DV, GLA_DK), F32)],
        compiler_params=_params(("arbitrary", "arbitrary")),
    )(pa, pa, pa, la, ltri, states, states, do)


def _rowwise(fn, name, T, tm, rows, pars, row_outs, acc_outs):
    nr, npar, nro = len(rows), len(pars), len(row_outs)
    tm = min(tm, T)
    nsteps = T // tm

    def body(*refs):
        i = pl.program_id(0)
        ins = [r[...] for r in refs[:nr + npar]]
        outs, accs = fn(i, *ins)
        for r, o in zip(refs[nr + npar:nr + npar + nro], outs):
            r[...] = o.astype(r.dtype)
        for r, a in zip(refs[nr + npar + nro:], accs):
            @pl.when(i == 0)
            def _(r=r, a=a):
                r[...] = a

            @pl.when(i > 0)
            def _(r=r, a=a):
                r[...] += a

    def rspec(width, cb, kind):
        if kind == "cur":
            return pl.BlockSpec((tm, width), lambda i: (i, cb))
        if kind == "prev":
            return pl.BlockSpec((8, width), lambda i: (jnp.maximum(i * (tm // 8) - 1, 0), cb))
        return pl.BlockSpec((8, width), lambda i: (jnp.minimum((i + 1) * (tm // 8), T // 8 - 1), cb))

    in_specs = [rspec(w, cb, kind) for (_, w, cb, kind) in rows]
    in_specs += [pl.BlockSpec(p.shape, lambda i, nd=p.ndim: (0,) * nd) for p in pars]
    out_specs = [pl.BlockSpec((tm, w), lambda i: (i, 0)) for (w, _) in row_outs]
    out_specs += [pl.BlockSpec(s, lambda i, nd=len(s): (0,) * nd) for s in acc_outs]
    out_shape = [jax.ShapeDtypeStruct((T, w), dt) for (w, dt) in row_outs]
    out_shape += [jax.ShapeDtypeStruct(s, F32) for s in acc_outs]
    res = pl.pallas_call(
        body, name=name, grid=(nsteps,), in_specs=in_specs, out_specs=out_specs, out_shape=out_shape,
        compiler_params=_params(("arbitrary",)),
    )(*[r[0] for r in rows], *pars)
    return res


def _cur(a, width=None, cb=0):
    return (a, a.shape[1] if width is None else width, cb, "cur")


def _sigmoid(x):
    return 1.0 / (1.0 + jnp.exp(-x))


def _silu(x):
    return x * _sigmoid(x)


def _softplus(x):
    return jnp.maximum(x, 0.0) + jnp.log(1.0 + jnp.exp(-jnp.abs(x)))


def _rms(x, g):
    return x * lax.rsqrt(jnp.mean(x * x, axis=-1, keepdims=True) + NORM_EPS) * g


def _dot_hi(a, b):
    return jnp.dot(a, b, precision=HI, preferred_element_type=F32)


def _rms_fwd(x, g, name):
    T = x.shape[0]
    fn = lambda i, xb, gb: ((_rms(xb, gb),), ())
    return _rowwise(fn, name, T, 256, [_cur(x)], [g], [(D, BF16)], [])[0]


def _rms_bwd(x, g, dh, dres, name):
    T = x.shape[0]

    def fn(i, xb, dhb, drb, gb):
        _, vjp = jax.vjp(_rms, xb, gb)
        dx, dg = vjp(dhb)
        return (drb + dx,), (dg,)

    return _rowwise(fn, name, T, 256, [_cur(x), _cur(dh), _cur(dres)], [g], [(D, F32)], [(1, D)])


def _loss_bwd(x, target, g):
    T = x.shape[0]

    def loss(xb, gb, tb):
        err = _rms(xb, gb) - tb
        return 0.5 * jnp.sum(jnp.mean(err * err, axis=-1, keepdims=True))

    def fn(i, xb, tb, gb):
        val, (dx, dg) = jax.value_and_grad(loss, argnums=(0, 1))(xb, gb, tb)
        return (dx,), (jnp.full((1, 128), val, F32), dg)

    return _rowwise(fn, "loss_bwd", T, 256, [_cur(x), _cur(target)], [g], [(D, F32)], [(1, 128), (1, D)])


def _gla_la(a_down, w_a2, b_a):
    return -_softplus(-(_bdot(a_down, w_a2, _NN) + b_a)) * (1.0 / GLA_TAU)


def _gla_prep(pa, w_a2, b_a):
    T = pa.shape[0]
    fn = lambda i, ab, wb, bb: ((_gla_la(ab, wb, bb),), ())
    return _rowwise(fn, "gla_prep", T, 512, [_cur(pa, LORA_P, (2 * GLA_QK + 2 * GLA_V) // LORA_P)], [w_a2, b_a],
                    [(GLA_QK, F32)], [])[0]


def _gla_prep_bwd(pa, w_a2, b_a, dla):
    T = pa.shape[0]

    def fn(i, ab, dlab, wb, bb):
        _, vjp = jax.vjp(_gla_la, ab, wb, bb)
        da, dw, db = vjp(dlab)
        return (da,), (dw, db)

    return _rowwise(fn, "gla_prep_bwd", T, 512, [_cur(pa, LORA_P, (2 * GLA_QK + 2 * GLA_V) // LORA_P), _cur(dla)],
                    [w_a2, b_a], [(LORA_P, BF16)], [(LORA_P, GLA_QK), (1, GLA_QK)])


def _gla_out(o, r, gn, ind, ind_t):
    ms = _dot_hi(_dot_hi(o * o, ind) * (1.0 / GLA_DV), ind_t)
    return o * lax.rsqrt(ms + NORM_EPS) * gn * _silu(r)


def _gla_post(o_raw, pa, gn, ind, ind_t):
    T = pa.shape[0]
    fn = lambda i, ob, rb, gb, a, b: ((_gla_out(ob, rb, gb, a, b),), ())
    return _rowwise(fn, "gla_post", T, 256, [_cur(o_raw), _cur(pa, GLA_V, 2)], [gn, ind, ind_t], [(GLA_V, BF16)], [])[0]


def _gla_post_bwd(o_raw, pa, gn, ind, ind_t, do):
    T = pa.shape[0]

    def fn(i, ob, rb, dob, gb, a, b):
        _, vjp = jax.vjp(lambda o, r, g: _gla_out(o, r, g, a, b), ob, rb, gb)
        d_o, d_r, d_g = vjp(dob)
        return (d_o, d_r), (d_g,)

    return _rowwise(fn, "gla_post_bwd", T, 256, [_cur(o_raw), _cur(pa, GLA_V, 2), _cur(do)], [gn, ind, ind_t],
                    [(GLA_V, F32), (GLA_V, BF16)], [(1, GLA_V)])


def _shift_rows(cur, prev8, i):
    first = jnp.where(i == 0, 0.0, prev8[7:8, :])
    rolled = pltpu.roll(cur, 1, 0)
    return jnp.where(lax.broadcasted_iota(jnp.int32, cur.shape, 0) == 0, first, rolled)


def _rw_gates(rw, w0, w_w2, a0, w_a2, w_g2, k_k, k_a, ind, ind_t):
    rk = rw[:, RW_W:2 * RW_W]
    wd = rw[:, 3 * RW_W:3 * RW_W + LORA_P]
    ad = rw[:, 3 * RW_W + LORA_P:3 * RW_W + 2 * LORA_P]
    gd = rw[:, 3 * RW_W + 2 * LORA_P:]
    w_raw = w0 + _bdot(jnp.tanh(wd), w_w2, _NN)
    w = jnp.exp(-jnp.exp(-_softplus(-w_raw) - 0.5))
    a = _sigmoid(a0 + _bdot(ad, w_a2, _NN))
    g = _bdot(_sigmoid(gd), w_g2, _NN)
    kk = rk * k_k
    kk = kk * _dot_hi(lax.rsqrt(jnp.maximum(_dot_hi(kk * kk, ind), 1e-24)), ind_t)
    k2 = rk * (1.0 + (a - 1.0) * k_a)
    return w, k2, kk, kk * a, g


def _rw_prep(pr, mu, gate_pars):
    T = pr.shape[0]

    def fn(i, cur, prev8, mub, *gp):
        rw = cur + mub * (_shift_rows(cur, prev8, i) - cur)
        return (rw,) + _rw_gates(rw, *gp), ()

    return _rowwise(fn, "rw_prep", T, 256, [_cur(pr), (pr, PR_W, 0, "prev")], [mu, *gate_pars],
                    [(PR_W, F32)] + [(RW_W, F32)] * 5, [])


def _rw_prep_bwd(pr, mu, gate_pars, d_r, d_v, d_w, d_k2, d_kk, d_b, d_g):
    T = pr.shape[0]
    rows = [_cur(pr), (pr, PR_W, 0, "prev")] + [_cur(x) for x in (*d_r, *d_v, d_w, *d_k2, d_kk, d_b, d_g)]
    acc = [(1, PR_W)] + [tuple(p.shape) for p in gate_pars[:-2]]

    def fn(i, cur, prev8, dr1, dr2, dv1, dv2, dw, dk1, dk2, dkk, db, dg, mub, *gp):
        sh = _shift_rows(cur, prev8, i)
        rw = cur + mub * (sh - cur)
        _, vjp = jax.vjp(lambda x, *p: _rw_gates(x, *p, gp[-2], gp[-1]), rw, *gp[:-2])
        grads = vjp((dw, dk1 + dk2, dkk, db, dg))
        zeros = jnp.zeros((cur.shape[0], PR_W - 3 * RW_W), F32)
        drw = grads[0] + jnp.concatenate([dr1 + dr2, jnp.zeros_like(dr1), dv1 + dv2, zeros], axis=1)
        dmu = jnp.sum(drw * (sh - cur), axis=0, keepdims=True)
        return (drw,), (dmu, *grads[1:])

    return _rowwise(fn, "rw_prep_bwd", T, 128, rows, [mu, *gate_pars], [(PR_W, F32)], acc)


def _shift_bwd(drw, mu):
    T = drw.shape[0]
    tm = min(256, T)

    def fn(i, cur, next8, mub):
        last = jnp.where(i == T // tm - 1, 0.0, next8[0:1, :])
        rolled = pltpu.roll(cur, cur.shape[0] - 1, 0)
        nxt = jnp.where(lax.broadcasted_iota(jnp.int32, cur.shape, 0) == cur.shape[0] - 1, last, rolled)
        return ((1.0 - mub) * cur + mub * nxt,), ()

    return _rowwise(fn, "shift_bwd", T, tm, [_cur(drw), (drw, PR_W, 0, "next")], [mu], [(PR_W, BF16)], [])[0]


def _rw_out(y, r, v, k2, g, lnx_w, lnx_b, r_k, ind, ind_t):
    mean = _dot_hi(_dot_hi(y, ind) * (1.0 / RW_HD), ind_t)
    yc = y - mean
    var = _dot_hi(_dot_hi(yc * yc, ind) * (1.0 / RW_HD), ind_t)
    yn = yc * lax.rsqrt(var + GN_EPS) * lnx_w + lnx_b
    bonus = _dot_hi(_dot_hi(r * k2 * r_k, ind), ind_t) * v
    return (yn + bonus) * g


def _rw_post(y, rw, k2, g, pars):
    T = y.shape[0]
    fn = lambda i, *a: ((_rw_out(*a),), ())
    return _rowwise(fn, "rw_post", T, 256, [_cur(y), _cur(rw, RW_W, 0), _cur(rw, RW_W, 2), _cur(k2), _cur(g)], pars,
                    [(RW_W, BF16)], [])[0]


def _rw_post_bwd(y, rw, k2, g, pars, do):
    T = y.shape[0]

    def fn(i, yb, rb, vb, kb, gb, dob, lw, lb, rk, ind, ind_t):
        _, vjp = jax.vjp(lambda *a: _rw_out(*a, ind, ind_t), yb, rb, vb, kb, gb, lw, lb, rk)
        gr = vjp(dob)
        return gr[:5], gr[5:]

    return _rowwise(fn, "rw_post_bwd", T, 256,
                    [_cur(y), _cur(rw, RW_W, 0), _cur(rw, RW_W, 2), _cur(k2), _cur(g), _cur(do)], pars,
                    [(RW_W, F32)] * 5, [(1, RW_W)] * 3)


def _merge_bwd(dm, y_gla, y_rw, pg, gate_b):
    T = dm.shape[0]

    def fn(i, dmb, ya, yr, p1, p2, gb):
        g1 = _sigmoid(p1 + gb[:, :D])
        g2 = _sigmoid(p2 + gb[:, D:])
        dp1 = dmb * ya * g1 * (1.0 - g1)
        dp2 = dmb * yr * g2 * (1.0 - g2)
        dp = jnp.concatenate([dp1, dp2], axis=1)
        return (dmb * g1, dmb * g2, dp), (jnp.sum(dp, axis=0, keepdims=True),)

    return _rowwise(fn, "merge_bwd", T, 256, [_cur(dm), _cur(y_gla), _cur(y_rw), _cur(pg, D, 0), _cur(pg, D, 1)],
                    [gate_b], [(D, BF16), (D, BF16), (PG_W, BF16)], [(1, PG_W)])


_NN = (((1,), (0,)), ((), ()))
_NT = (((1,), (1,)), ((), ()))
_TN = (((0,), (0,)), ((), ()))


def _bdot(a, b, dims):
    return lax.dot_general(a.astype(BF16), b.astype(BF16), dims, preferred_element_type=F32)


def _accumulate(k, nk, acc, part, finish):
    if nk == 1:
        finish(part)
        return

    @pl.when(k == 0)
    def _():
        acc[...] = part

    @pl.when(k > 0)
    def _():
        acc[...] += part

    @pl.when(k == nk - 1)
    def _():
        finish(acc[...])


def _call(body, comm, name, grid, in_specs, out_specs, out_shape, scratch_shapes, sem, operands):
    if comm is None:
        return pl.pallas_call(body, name=name, grid=grid, in_specs=in_specs, out_specs=out_specs, out_shape=out_shape,
                              scratch_shapes=scratch_shapes, compiler_params=_params(sem))(*operands)
    kind, arrays = comm
    nc, n_in, n_out, n_scr = len(arrays), len(in_specs), len(out_shape), len(scratch_shapes)
    total = 1
    for n in grid:
        total *= n

    def with_comm(*refs):
        own = refs[:n_in] + refs[n_in + nc:n_in + nc + n_out] + refs[n_in + 2 * nc + n_out:n_in + 2 * nc + n_out + n_scr]
        c_in, c_out, sems = refs[n_in:n_in + nc], refs[n_in + nc + n_out:n_in + 2 * nc + n_out], refs[-3:]
        step = 0
        for axis, n in enumerate(grid):
            step = step * n + pl.program_id(axis)
        if kind == "gather":
            start, forward, finish = _gather_plan(c_in, c_out, *sems)
            pl.when(step == 0)(start)
            pl.when(step == total // 2)(forward)
        else:
            start, finish = _exchange_plan(c_in, c_out, *sems)
            pl.when(step == 0)(start)
        body(*own)
        pl.when(step == total - 1)(finish)

    lead = (NDEV,) if kind == "gather" else ()
    return pl.pallas_call(
        with_comm, name=name, grid=grid, in_specs=list(in_specs) + [_ANY] * nc, out_specs=list(out_specs) + [_ANY] * nc,
        out_shape=list(out_shape) + [jax.ShapeDtypeStruct(lead + a.shape, a.dtype) for a in arrays],
        scratch_shapes=list(scratch_shapes) + _comm_sems(nc), compiler_params=_params(("arbitrary",) * len(grid)),
    )(*operands, *arrays)


def _matmul(a, b, mode, M, N, K, tm, tn, tk, name, a_off=(0, 0), b_off=(0, 0), res=None, scale=1.0, out_dtype=F32,
            comm=None):
    tm, tn, tk = min(tm, M), min(tn, N), min(tk, K)
    nk = K // tk
    if mode == "nn":
        a_spec = pl.BlockSpec((tm, tk), lambda i, j, k: (i + a_off[0], k + a_off[1]))
        b_spec = pl.BlockSpec((tk, tn), lambda i, j, k: (k + b_off[0], j + b_off[1]))
        dims = _NN
    elif mode == "nt":
        a_spec = pl.BlockSpec((tm, tk), lambda i, j, k: (i + a_off[0], k + a_off[1]))
        b_spec = pl.BlockSpec((tn, tk), lambda i, j, k: (j + b_off[0], k + b_off[1]))
        dims = _NT
    else:
        a_spec = pl.BlockSpec((tk, tm), lambda i, j, k: (k + a_off[0], i + a_off[1]))
        b_spec = pl.BlockSpec((tk, tn), lambda i, j, k: (k + b_off[0], j + b_off[1]))
        dims = _TN
    o_spec = pl.BlockSpec((tm, tn), lambda i, j, k: (i, j))

    def body(a_ref, b_ref, *rest):
        r_ref = rest[0] if res is not None else None
        o_ref = rest[1] if res is not None else rest[0]
        acc = rest[-1] if nk > 1 else None

        def finish(total):
            total = total * scale if scale != 1.0 else total
            if r_ref is not None:
                total = r_ref[...] + total
            o_ref[...] = total.astype(out_dtype)

        _accumulate(pl.program_id(2), nk, acc, _bdot(a_ref[...], b_ref[...], dims), finish)

    out = _call(body, comm, name, (M // tm, N // tn, nk), [a_spec, b_spec] + ([o_spec] if res is not None else []),
                [o_spec], [jax.ShapeDtypeStruct((M, N), out_dtype)], [pltpu.VMEM((tm, tn), F32)] if nk > 1 else [],
                ("parallel", "parallel", "arbitrary"), [a, b] + ([res] if res is not None else []))
    return out[0] if comm is None else out


def _ffn_up(h, wg, wu, name, comm=None):
    T = h.shape[0]
    tm = min(1024, T)

    def body(h_ref, wg_ref, wu_ref, a_ref, u_ref, s_ref):
        hb = h_ref[...]
        a = _bdot(hb, wg_ref[...], _NN)
        u = _bdot(hb, wu_ref[...], _NN)
        a_ref[...] = a
        u_ref[...] = u
        s_ref[...] = (_silu(a) * u).astype(BF16)

    w_spec = pl.BlockSpec((None, D, FSH), lambda i, j: (j, 0, 0))
    o_spec = pl.BlockSpec((None, tm, FSH), lambda i, j: (j, i, 0))
    sh = lambda dt: jax.ShapeDtypeStruct((NDEV, T, FSH), dt)
    return _call(body, comm, name, (T // tm, NDEV), [pl.BlockSpec((tm, D), lambda i, j: (i, 0)), w_spec, w_spec],
                 [o_spec] * 3, [sh(F32), sh(F32), sh(BF16)], [], ("parallel", "arbitrary"), [h, wg, wu])


def _ffn_down(s, wd, x, name, comm=None):
    T = x.shape[0]
    tm, tn = min(1024, T), 1024

    def body(s_ref, wd_ref, x_ref, o_ref, acc):
        def finish(total):
            o_ref[...] = x_ref[...] + 0.5 * total

        _accumulate(pl.program_id(2), NDEV, acc, _bdot(s_ref[...], wd_ref[...], _NN), finish)

    xo = pl.BlockSpec((tm, tn), lambda i, n, j: (i, n))
    out = _call(body, comm, name, (T // tm, D // tn, NDEV),
                [pl.BlockSpec((None, tm, FSH), lambda i, n, j: (j, i, 0)),
                 pl.BlockSpec((None, FSH, tn), lambda i, n, j: (j, 0, n)), xo],
                [xo], [jax.ShapeDtypeStruct((T, D), F32)], [pltpu.VMEM((tm, tn), F32)],
                ("parallel", "parallel", "arbitrary"), [s, wd, x])
    return out[0] if comm is None else out


def _ffn_bwd_hidden(dx, wd, a, u, name):
    T = dx.shape[0]
    tm = min(1024, T)

    def body(dx_ref, wd_ref, a_ref, u_ref, da_ref, du_ref):
        ds = 0.5 * _bdot(dx_ref[...], wd_ref[...], _NT)
        av = a_ref[...]
        sg = _sigmoid(av)
        da_ref[...] = (ds * u_ref[...] * (sg * (1.0 + av * (1.0 - sg)))).astype(BF16)
        du_ref[...] = (ds * (av * sg)).astype(BF16)

    act = pl.BlockSpec((None, tm, FSH), lambda i, j: (j, i, 0))
    sh = jax.ShapeDtypeStruct((NDEV, T, FSH), BF16)
    return pl.pallas_call(
        body, name=name, grid=(T // tm, NDEV),
        in_specs=[pl.BlockSpec((tm, D), lambda i, j: (i, 0)), pl.BlockSpec((None, FSH, D), lambda i, j: (j, 0, 0)),
                  act, act],
        out_specs=[act, act], out_shape=(sh, sh),
        compiler_params=_params(("parallel", "arbitrary")),
    )(dx, wd, a, u)


def _ffn_bwd_input(da, du, wg, wu, name, comm=None):
    T = da.shape[1]
    tm, tn = min(1024, T), 1024

    def body(da_ref, du_ref, wg_ref, wu_ref, o_ref, acc):
        part = _bdot(da_ref[...], wg_ref[...], _NT) + _bdot(du_ref[...], wu_ref[...], _NT)

        def finish(total):
            o_ref[...] = total

        _accumulate(pl.program_id(2), NDEV, acc, part, finish)

    act = pl.BlockSpec((None, tm, FSH), lambda i, n, j: (j, i, 0))
    wsp = pl.BlockSpec((None, tn, FSH), lambda i, n, j: (j, n, 0))
    out = _call(body, comm, name, (T // tm, D // tn, NDEV), [act, act, wsp, wsp],
                [pl.BlockSpec((tm, tn), lambda i, n, j: (i, n))], [jax.ShapeDtypeStruct((T, D), F32)],
                [pltpu.VMEM((tm, tn), F32)], ("parallel", "parallel", "arbitrary"), [da, du, wg, wu])
    return out[0] if comm is None else out


def _ffn_grad_up(h, da, du, name, comm=None):
    T = h.shape[0]
    tm, tk = 1024, min(1024, T)
    nk = T // tk

    def body(h_ref, da_ref, du_ref, o_ref, acc_a, acc_u):
        k = pl.program_id(2)
        hb = h_ref[...]
        for acc, ref, slot in ((acc_a, da_ref, 0), (acc_u, du_ref, 1)):
            def finish(total, slot=slot):
                o_ref[slot] = total.astype(BF16)

            _accumulate(k, nk, acc, _bdot(hb, ref[...], _TN), finish)

    act = pl.BlockSpec((None, tk, FSH), lambda j, i, t: (j, t, 0))
    out = _call(body, comm, name, (NDEV, D // tm, nk), [pl.BlockSpec((tk, tm), lambda j, i, t: (t, i)), act, act],
                [pl.BlockSpec((None, 2, tm, FSH), lambda j, i, t: (j, 0, i, 0))],
                [jax.ShapeDtypeStruct((NDEV, 2, D, FSH), BF16)],
                [pltpu.VMEM((tm, FSH), F32), pltpu.VMEM((tm, FSH), F32)], ("parallel", "parallel", "arbitrary"), [h, da, du])
    return out[0] if comm is None else out


def _ffn_grad_down(s, dx, name):
    T = dx.shape[0]
    tn, tk = 1024, min(1024, T)
    nk = T // tk

    def body(s_ref, dx_ref, o_ref, acc):
        def finish(total):
            o_ref[...] = (0.5 * total).astype(BF16)

        _accumulate(pl.program_id(2), nk, acc, _bdot(s_ref[...], dx_ref[...], _TN), finish)

    return pl.pallas_call(
        body, name=name, grid=(NDEV, D // tn, nk),
        in_specs=[pl.BlockSpec((None, tk, FSH), lambda j, n, t: (j, t, 0)), pl.BlockSpec((tk, tn), lambda j, n, t: (t, n))],
        out_specs=pl.BlockSpec((None, FSH, tn), lambda j, n, t: (j, 0, n)),
        out_shape=jax.ShapeDtypeStruct((NDEV, DFF // NDEV, D), BF16),
        scratch_shapes=[pltpu.VMEM((FSH, tn), F32)],
        compiler_params=_params(("parallel", "parallel", "arbitrary")),
    )(s, dx)


def _branch_merge(o_gla, o_rw, wb, pg, gate_b):
    T = o_gla.shape[0]
    tm, tn = min(1024, T), 512

    def body(og_ref, or_ref, w1_ref, w2_ref, p1_ref, p2_ref, b1_ref, b2_ref, yg_ref, yr_ref, m_ref):
        yg = _bdot(og_ref[...], w1_ref[...], _NN)
        yr = _bdot(or_ref[...], w2_ref[...], _NN)
        yg_ref[...] = yg
        yr_ref[...] = yr
        m_ref[...] = (_sigmoid(p1_ref[...] + b1_ref[...]) * yg + _sigmoid(p2_ref[...] + b2_ref[...]) * yr).astype(BF16)

    nj = D // tn
    act = pl.BlockSpec((tm, GLA_V), lambda i, j: (i, 0))
    out = pl.BlockSpec((tm, tn), lambda i, j: (i, j))
    return pl.pallas_call(
        body, name="branch_merge", grid=(T // tm, nj),
        in_specs=[act, act, pl.BlockSpec((GLA_V, tn), lambda i, j: (0, j)), pl.BlockSpec((RW_W, tn), lambda i, j: (1, j)),
                  out, pl.BlockSpec((tm, tn), lambda i, j: (i, nj + j)),
                  pl.BlockSpec((1, tn), lambda i, j: (0, j)), pl.BlockSpec((1, tn), lambda i, j: (0, nj + j))],
        out_specs=[out, out, out],
        out_shape=(jax.ShapeDtypeStruct((T, D), F32), jax.ShapeDtypeStruct((T, D), F32), jax.ShapeDtypeStruct((T, D), BF16)),
        compiler_params=_params(("parallel", "arbitrary")),
    )(o_gla, o_rw, wb, wb, pg, pg, gate_b, gate_b)


def _head_indicator(width, heads):
    col = lax.broadcasted_iota(jnp.int32, (width, 128), 0) // (width // heads)
    ind = (col == lax.broadcasted_iota(jnp.int32, (width, 128), 1)).astype(F32)
    return ind, ind.T


def _ffn_fwd(x, g, wg, wu, wd, tag):
    h = _rms_fwd(x, g, "rms_" + tag)
    a, u, s = _ffn_up(h, wg, wu, "ffn_up_" + tag)
    return _ffn_down(s, wd, x, "ffn_down_" + tag), (h, a, u, s)


def _ffn_fwd_gathering(x, g, wg, wu, wd_block, next_block, tag):
    h = _rms_fwd(x, g, "rms_" + tag)
    a, u, s, wd = _ffn_up(h, wg, wu, "ffn_up_" + tag, comm=("gather", [wd_block]))
    y, gathered = _ffn_down(s, wd, x, "ffn_down_" + tag, comm=("gather", [next_block]))
    return y, (h, a, u, s), wd, gathered


def _ffn_bwd(dy, x, g, wg, wu, wd, saved, tag, exchange=False):
    h, a, u, s = saved
    dwd = _ffn_grad_down(s, dy, "ffn_grad_down_" + tag)
    da, du = _ffn_bwd_hidden(dy, wd, a, u, "ffn_bwd_hidden_" + tag)
    if exchange:
        dw_up, dwd = _ffn_grad_up(h, da, du, "ffn_grad_up_" + tag, comm=("exchange", [dwd]))
        dh, dw_up = _ffn_bwd_input(da, du, wg, wu, "ffn_bwd_input_" + tag, comm=("exchange", [dw_up]))
    else:
        dw_up = _ffn_grad_up(h, da, du, "ffn_grad_up_" + tag)
        dh = _ffn_bwd_input(da, du, wg, wu, "ffn_bwd_input_" + tag)
    dx, dg = _rms_bwd(x, g, dh, dy, "rms_bwd_" + tag)
    return dx, dg, dw_up, dwd


def _local_step(x, target, w, blocks):
    T = x.shape[0]
    ind16, ind16_t = _head_indicator(RW_W, RW_HEADS)
    ind4, ind4_t = _head_indicator(GLA_V, GLA_HEADS)
    ltri = jnp.tril(jnp.ones((CHUNK, CHUNK), F32))
    gate_pars = [w["w0"], w["w_w2"], w["a0"], w["w_a2"], w["w_g2"], w["k_k"], w["k_a"], ind16, ind16_t]
    post_pars = [w["lnx_w"], w["lnx_b"], w["r_k"], ind16, ind16_t]

    x1, ffn1, wd1, g_proj = _ffn_fwd_gathering(x, w["g1"], w["wg1"], w["wu1"], blocks["wd1"], blocks["win"], "1")
    win = _align_proj(_unshard_cols(g_proj))
    h2 = _rms_fwd(x1, w["g2"], "rms_mix")
    proj = lambda n, off, name: _matmul(h2, win, "nn", T, n, D, 1024, 512, D, name, b_off=(0, off // 512))
    pg = proj(PG_W, 0, "proj_gate")
    pr = proj(PR_W, PG_W, "proj_rwkv")
    pa = proj(PA_W, PG_W + PR_W, "proj_gla")
    la = _gla_prep(pa, w["gla_w_a2"], w["gla_b_a"])
    o_raw, gla_states = _gla_core_fwd(pa, la, ltri)
    o_gla = _gla_post(o_raw, pa, w["gn"], ind4, ind4_t)
    rw, dec, k2, kk, b, g = _rw_prep(pr, w["mu"], gate_pars)
    y, rw_states, g_up2, g_down2 = _rw_core_fwd(rw, dec, k2, kk, b, gather=blocks["late"])
    w = {**w, **_late_weights(g_up2, g_down2)}
    o_rw = _rw_post(y, rw, k2, g, post_pars)
    y_gla, y_rw, merged = _branch_merge(o_gla, o_rw, w["wb"], pg, w["gate_b"])
    x2 = _matmul(merged, w["wo"], "nn", T, D, D, 1024, 1024, D, "out_proj", res=x1)
    x3, ffn2 = _ffn_fwd(x2, w["g3"], w["wg2"], w["wu2"], w["wd2"], "2")
    dx3, loss, d_gf = _loss_bwd(x3, target, w["gf"])

    grads = {"gf": d_gf}
    dx2, grads["g3"], grads["up2"], grads["wd2"] = _ffn_bwd(
        dx3, x2, w["g3"], w["wg2"], w["wu2"], w["wd2"], ffn2, "2")
    dm = _matmul(dx2, w["wo"], "nt", T, D, D, 1024, 1024, D, "out_proj_bwd")
    grads["wo"] = _matmul(merged, dx2, "tn", D, D, T, 1024, 1024, 1024, "out_proj_grad", out_dtype=BF16)
    dy_gla, dy_rw, dpg, grads["gate_b"] = _merge_bwd(dm, y_gla, y_rw, pg, w["gate_b"])
    do_gla = _matmul(dy_gla, w["wb"], "nt", T, GLA_V, D, 1024, 1024, D, "branch_gla_bwd")
    do_rw = _matmul(dy_rw, w["wb"], "nt", T, RW_W, D, 1024, 1024, D, "branch_rwkv_bwd", b_off=(1, 0))
    grads["wb"] = jnp.concatenate([
        _matmul(o_gla, dy_gla, "tn", GLA_V, D, T, 1024, 1024, 1024, "branch_gla_grad", out_dtype=BF16),
        _matmul(o_rw, dy_rw, "tn", RW_W, D, T, 1024, 1024, 1024, "branch_rwkv_grad", out_dtype=BF16)], axis=0)
    dy, dr2, dv2, dk2b, dg, grads["lnx_w"], grads["lnx_b"], grads["r_k"] = _rw_post_bwd(y, rw, k2, g, post_pars, do_rw)
    early = _late_grad_parts(grads)
    received = {}
    dr1, dw, dk2a, dv1, dkk, db, received["up2"], received["down2"] = _rw_core_bwd(
        rw, dec, k2, kk, b, rw_states, dy, exchange=early)
    drw, grads["mu"], grads["w0"], grads["w_w2"], grads["a0"], grads["w_a2"], grads["w_g2"], grads["k_k"], grads["k_a"] = (
        _rw_prep_bwd(pr, w["mu"], gate_pars, (dr1, dr2), (dv1, dv2), dw, (dk2a, dk2b), dkk, db, dg))
    dpr = _shift_bwd(drw, w["mu"])
    do_raw, dr_gla, grads["gn"] = _gla_post_bwd(o_raw, pa, w["gn"], ind4, ind4_t, do_gla)
    dq, dk, dv, dla = _gla_core_bwd(pa, la, ltri, gla_states, do_raw)
    da_down, grads["gla_w_a2"], grads["gla_b_a"] = _gla_prep_bwd(pa, w["gla_w_a2"], w["gla_b_a"], dla)
    dpa = jnp.concatenate([dq.astype(BF16), dk.astype(BF16), dv.astype(BF16), dr_gla, da_down,
                           jnp.zeros((T, PA_W - PA_USED), BF16)], axis=1)
    dp = jnp.concatenate([dpg, dpr, dpa], axis=1)
    d_win = _matmul(h2, dp, "tn", D, DIN_P, T, 1024, 1024, 1024, "proj_grad", out_dtype=BF16)
    dh2, received["win"] = _matmul(dp, win, "nt", T, D, DIN_P, 1024, 1024, 1024, "proj_bwd",
                                   comm=("exchange", [_shard_cols(_unalign_proj(d_win))]))
    dx1, grads["g2"] = _rms_bwd(x1, w["g2"], dh2, dx2, "rms_bwd_mix")
    dx, grads["g1"], received["up1"], received["wd1"] = _ffn_bwd(
        dx1, x, w["g1"], w["wg1"], w["wu1"], wd1, ffn1, "1", exchange=True)
    return loss, dx, grads, received


BIG = ("ffn1_wg", "ffn1_wu", "ffn1_wd", "w_in", "w_branch", "w_out", "ffn2_wg", "ffn2_wu", "ffn2_wd")
SMALL_SHARDED = ("gla_w_a2", "rwkv_w_w2", "rwkv_w_a2", "rwkv_w_g2")
REPLICATED = ("ffn1_norm", "mix_norm", "gla_b_a", "gla_gn_w", "rwkv_mu", "rwkv_w0", "rwkv_a0", "rwkv_k_k", "rwkv_k_a",
              "rwkv_r_k", "rwkv_lnx_w", "rwkv_lnx_b", "gate_b", "ffn2_norm", "final_norm")
WEIGHTS = ("ffn1_norm", "ffn1_wg", "ffn1_wu", "ffn1_wd", "mix_norm", "w_in", "gla_w_a2", "gla_b_a", "gla_gn_w",
           "rwkv_mu", "rwkv_w0", "rwkv_w_w2", "rwkv_a0", "rwkv_w_a2", "rwkv_w_g2", "rwkv_k_k", "rwkv_k_a", "rwkv_r_k",
           "rwkv_lnx_w", "rwkv_lnx_b", "gate_b", "w_branch", "w_out", "ffn2_norm", "ffn2_wg", "ffn2_wu", "ffn2_wd",
           "final_norm")


def _unshard_cols(g):
    return jnp.transpose(g, (1, 0, 2)).reshape(g.shape[1], NDEV * g.shape[2])


def _shard_cols(a):
    return jnp.transpose(a.reshape(a.shape[0], NDEV, a.shape[1] // NDEV), (1, 0, 2))


def _pad_rows(a, rows):
    return jnp.pad(a, ((0, rows - a.shape[0]), (0, 0)))


def _align_rw(a):
    c = 3 * RW_W
    z = jnp.zeros((a.shape[0], LORA_P - DECAY_LORA), a.dtype)
    return jnp.concatenate([a[:, :c], a[:, c:c + DECAY_LORA], z, a[:, c + DECAY_LORA:c + 2 * DECAY_LORA], z,
                            a[:, c + 2 * DECAY_LORA:]], axis=1)


def _unalign_rw(a):
    c = 3 * RW_W
    return jnp.concatenate([a[:, :c + DECAY_LORA], a[:, c + LORA_P:c + LORA_P + AAA_LORA], a[:, c + 2 * LORA_P:]], axis=1)


def _align_proj(a):
    gla = jnp.pad(a[:, :GLA_IN], ((0, 0), (0, PA_W - GLA_IN)))
    return jnp.concatenate([a[:, GLA_IN + RW_IN:], _align_rw(a[:, GLA_IN:GLA_IN + RW_IN]), gla], axis=1)


def _unalign_proj(a):
    return jnp.concatenate([a[:, PG_W + PR_W:PG_W + PR_W + GLA_IN], _unalign_rw(a[:, PG_W:PG_W + PR_W]), a[:, :PG_W]], axis=1)


def _layout_weights(gb, gs, rep):
    row = lambda n: rep[n].reshape(1, -1)
    return {
        "wg1": gb["ffn1_wg"], "wu1": gb["ffn1_wu"],
        "g1": row("ffn1_norm"), "g2": row("mix_norm"), "g3": row("ffn2_norm"), "gf": row("final_norm"),
        "gla_w_a2": _pad_rows(_unshard_cols(gs["gla_w_a2"]), LORA_P), "gla_b_a": row("gla_b_a"),
        "gn": jnp.tile(row("gla_gn_w"), (1, GLA_HEADS)),
        "mu": _align_rw(row("rwkv_mu")), "w0": row("rwkv_w0"), "a0": row("rwkv_a0"),
        "w_w2": _pad_rows(_unshard_cols(gs["rwkv_w_w2"]), LORA_P),
        "w_a2": _pad_rows(_unshard_cols(gs["rwkv_w_a2"]), LORA_P),
        "w_g2": _unshard_cols(gs["rwkv_w_g2"]),
        "k_k": row("rwkv_k_k"), "k_a": row("rwkv_k_a"), "r_k": row("rwkv_r_k"),
        "lnx_w": row("rwkv_lnx_w"), "lnx_b": row("rwkv_lnx_b"), "gate_b": row("gate_b"),
    }


LATE_ROWS = (("ffn2_wd", FSH), ("w_branch", (GLA_V + RW_W) // NDEV), ("w_out", D // NDEV))


def _late_weights(g_up, g_down):
    r1, r2 = LATE_ROWS[0][1], LATE_ROWS[0][1] + LATE_ROWS[1][1]
    return {"wg2": g_up[:, 0], "wu2": g_up[:, 1], "wd2": g_down[:, :r1],
            "wb": g_down[:, r1:r2].reshape(GLA_V + RW_W, D), "wo": g_down[:, r2:].reshape(D, D)}


def _late_grad_parts(g):
    return [g["up2"], jnp.concatenate([g["wd2"], g["wb"].reshape(NDEV, -1, D), g["wo"].reshape(NDEV, -1, D)], axis=1)]


def _layout_grads(g):
    return {
        "ffn1_norm": g["g1"], "mix_norm": g["g2"], "ffn2_norm": g["g3"], "final_norm": g["gf"],
        "gla_w_a2": g["gla_w_a2"][:GLA_LORA], "gla_b_a": g["gla_b_a"],
        "gla_gn_w": jnp.sum(g["gn"].reshape(GLA_HEADS, GLA_DV), axis=0, keepdims=True),
        "rwkv_mu": _unalign_rw(g["mu"]), "rwkv_w0": g["w0"], "rwkv_a0": g["a0"],
        "rwkv_w_w2": g["w_w2"][:DECAY_LORA], "rwkv_w_a2": g["w_a2"][:AAA_LORA], "rwkv_w_g2": g["w_g2"],
        "rwkv_k_k": g["k_k"], "rwkv_k_a": g["k_a"], "rwkv_r_k": g["r_k"],
        "rwkv_lnx_w": g["lnx_w"], "rwkv_lnx_b": g["lnx_b"], "gate_b": g["gate_b"],
    }


_MESH = pl.DeviceIdType.MESH
_ANY = pl.BlockSpec(memory_space=pl.ANY)


def _position():
    return lax.axis_index("x"), lax.axis_index("y"), lax.axis_index("c")


def _slot(p):
    return 4 * p[0] + 2 * p[1] + p[2]


def _comm_sems(n):
    if not n:
        return []
    return [pltpu.SemaphoreType.DMA((7 * n,)), pltpu.SemaphoreType.DMA((7 * n,)), pltpu.SemaphoreType.DMA((n,))]


def _gather_plan(ins, outs, send_sems, recv_sems, local_sems):
    n = len(ins)
    x, y, c = _position()
    me, sibling = (x, y, c), (x, y, 1 - c)
    chips = [(1 - x, y), (x, 1 - y), (1 - x, 1 - y)]

    def copy(a, k, block, to, src=None):
        dst = outs[a].at[_slot(block)]
        return pltpu.make_async_remote_copy(
            src_ref=dst if src is None else src, dst_ref=dst, send_sem=send_sems.at[7 * a + k],
            recv_sem=recv_sems.at[7 * a + k], device_id=to, device_id_type=_MESH)

    def local(a):
        return pltpu.make_async_copy(ins[a], outs[a].at[_slot(me)], local_sems.at[a])

    def own(a):
        return [copy(a, 0, me, sibling, src=ins[a])] + [copy(a, 1 + j, me, (*chip, c), src=ins[a]) for j, chip in enumerate(chips)]

    def start():
        for a in range(n):
            local(a).start()
            for cp in own(a):
                cp.start()

    def forward():
        for a in range(n):
            for j, chip in enumerate(chips):
                copy(a, 1 + j, (*chip, c), me).wait_recv()
                copy(a, 4 + j, (*chip, c), sibling).start()

    def finish():
        for a in range(n):
            copy(a, 0, sibling, me).wait_recv()
            for j, chip in enumerate(chips):
                copy(a, 4 + j, (*chip, 1 - c), me).wait_recv()
        for a in range(n):
            for cp in own(a) + [copy(a, 4 + j, (*chip, c), sibling) for j, chip in enumerate(chips)]:
                cp.wait_send()
            local(a).wait()

    return start, forward, finish


def _exchange_plan(ins, outs, send_sems, recv_sems, local_sems):
    n = len(ins)
    x, y, c = _position()
    me = (x, y, c)
    flip = lambda v, f: 1 - v if f else v
    peers = [(flip(x, fx), flip(y, fy), flip(c, fc))
             for fx, fy, fc in ((0, 0, 1), (1, 0, 0), (0, 1, 0), (1, 1, 0), (1, 0, 1), (0, 1, 1), (1, 1, 1))]

    def copy(a, k, src_slot, dst_slot):
        return pltpu.make_async_remote_copy(
            src_ref=ins[a].at[src_slot], dst_ref=outs[a].at[dst_slot], send_sem=send_sems.at[7 * a + k],
            recv_sem=recv_sems.at[7 * a + k], device_id=peers[k], device_id_type=_MESH)

    def local(a):
        return pltpu.make_async_copy(ins[a].at[_slot(me)], outs[a].at[_slot(me)], local_sems.at[a])

    def start():
        for a in range(n):
            local(a).start()
            for k, peer in enumerate(peers):
                copy(a, k, _slot(peer), _slot(me)).start()

    def finish():
        for a in range(n):
            for k, peer in enumerate(peers):
                copy(a, k, _slot(peer), _slot(peer)).wait_recv()
        for a in range(n):
            for k, peer in enumerate(peers):
                copy(a, k, _slot(peer), _slot(me)).wait_send()
            local(a).wait()

    return start, finish


def _all_gather(arrays, name):
    n = len(arrays)

    def body(*refs):
        start, forward, finish = _gather_plan(refs[:n], refs[n:2 * n], *refs[2 * n:])
        start()
        forward()
        finish()

    return pl.pallas_call(
        body, name=name, in_specs=[_ANY] * n, out_specs=[_ANY] * n,
        out_shape=[jax.ShapeDtypeStruct((NDEV,) + a.shape, a.dtype) for a in arrays], scratch_shapes=_comm_sems(n),
    )(*arrays)


def _adamw_math(w, g, m, v):
    m = ADAM_B1 * m + (1.0 - ADAM_B1) * g
    v = ADAM_B2 * v + (1.0 - ADAM_B2) * (g * g)
    m_hat = m / (1.0 - ADAM_B1 ** ADAM_STEP)
    v_hat = v / (1.0 - ADAM_B2 ** ADAM_STEP)
    delta = -ADAM_LR * (m_hat / (jnp.sqrt(v_hat) + ADAM_EPS) + ADAM_WD * w)
    return delta, m, v


def _sum_slots(ref):
    total = ref[0].astype(F32)
    for s in range(1, NDEV):
        total = total + ref[s].astype(F32)
    return total


def _adamw(parts, w, m, v, tr, name, stack_index=None, row_block_offset=0):
    R, C = w.shape

    def body(p_ref, w_ref, m_ref, v_ref, g_ref, d_ref, nm_ref, nv_ref):
        g = _sum_slots(p_ref)
        g_ref[...] = g
        d_ref[...], nm_ref[...], nv_ref[...] = _adamw_math(w_ref[...], g, m_ref[...], v_ref[...])

    if stack_index is None:
        p_spec = pl.BlockSpec((NDEV, tr, C), lambda r: (0, row_block_offset + r, 0))
    else:
        p_spec = pl.BlockSpec((NDEV, None, tr, C), lambda r: (0, stack_index, r, 0))
    blk = pl.BlockSpec((tr, C), lambda r: (r, 0))
    out = jax.ShapeDtypeStruct((R, C), F32)
    return pl.pallas_call(
        body, name=name, grid=(R // tr,), in_specs=[p_spec, blk, blk, blk], out_specs=[blk] * 4, out_shape=(out,) * 4,
        compiler_params=_params(("parallel",)),
    )(parts, w, m, v)


def _sum_gathered(parts):
    _, R, C = parts.shape

    def body(p_ref, o_ref):
        o_ref[...] = _sum_slots(p_ref)

    return pl.pallas_call(body, name="small_grad_sum", out_shape=jax.ShapeDtypeStruct((R, C), F32),
                          compiler_params=_params())(parts)


def _adamw_small(w, g, m, v):
    def body(w_ref, g_ref, m_ref, v_ref, d_ref, nm_ref, nv_ref):
        d_ref[...], nm_ref[...], nv_ref[...] = _adamw_math(w_ref[...], g_ref[...], m_ref[...], v_ref[...])

    out = jax.ShapeDtypeStruct(w.shape, F32)
    return pl.pallas_call(body, name="adamw_small", out_shape=(out,) * 3, compiler_params=_params())(w, g, m, v)


def _pack(pieces, rows):
    flat = jnp.concatenate([p.reshape(-1) for p in pieces])
    return jnp.pad(flat, (0, rows * 128 - flat.shape[0])).reshape(rows, 128)


def _unpack(packed, shapes):
    flat = packed.reshape(-1)
    out, off = [], 0
    for s in shapes:
        size = 1
        for d in s:
            size *= d
        out.append(flat[off:off + size].reshape(s))
        off += size
    return out


def _rows_for(shapes, extra=0):
    total = extra
    for s in shapes:
        size = 1
        for d in s:
            size *= d
        total += size
    return -(-total // 1024) * 8


def kernel(x, ffn1_norm, ffn1_wg, ffn1_wu, ffn1_wd, mix_norm, w_in, gla_w_a2, gla_b_a, gla_gn_w, rwkv_mu, rwkv_w0, rwkv_w_w2, rwkv_a0, rwkv_w_a2, rwkv_w_g2, rwkv_k_k, rwkv_k_a, rwkv_r_k, rwkv_lnx_w, rwkv_lnx_b, gate_b, w_branch, w_out, ffn2_norm, ffn2_wg, ffn2_wu, ffn2_wd, final_norm, loss_target, m_ffn1_norm, m_ffn1_wg, m_ffn1_wu, m_ffn1_wd, m_mix_norm, m_w_in, m_gla_w_a2, m_gla_b_a, m_gla_gn_w, m_rwkv_mu, m_rwkv_w0, m_rwkv_w_w2, m_rwkv_a0, m_rwkv_w_a2, m_rwkv_w_g2, m_rwkv_k_k, m_rwkv_k_a, m_rwkv_r_k, m_rwkv_lnx_w, m_rwkv_lnx_b, m_gate_b, m_w_branch, m_w_out, m_ffn2_norm, m_ffn2_wg, m_ffn2_wu, m_ffn2_wd, m_final_norm, v_ffn1_norm, v_ffn1_wg, v_ffn1_wu, v_ffn1_wd, v_mix_norm, v_w_in, v_gla_w_a2, v_gla_b_a, v_gla_gn_w, v_rwkv_mu, v_rwkv_w0, v_rwkv_w_w2, v_rwkv_a0, v_rwkv_w_a2, v_rwkv_w_g2, v_rwkv_k_k, v_rwkv_k_a, v_rwkv_r_k, v_rwkv_lnx_w, v_rwkv_lnx_b, v_gate_b, v_w_branch, v_w_out, v_ffn2_norm, v_ffn2_wg, v_ffn2_wu, v_ffn2_wd, v_final_norm):
    wts = dict(zip(WEIGHTS, (ffn1_norm, ffn1_wg, ffn1_wu, ffn1_wd, mix_norm, w_in, gla_w_a2, gla_b_a, gla_gn_w, rwkv_mu, rwkv_w0, rwkv_w_w2, rwkv_a0, rwkv_w_a2, rwkv_w_g2, rwkv_k_k, rwkv_k_a, rwkv_r_k, rwkv_lnx_w, rwkv_lnx_b, gate_b, w_branch, w_out, ffn2_norm, ffn2_wg, ffn2_wu, ffn2_wd, final_norm)))
    mom = dict(zip(WEIGHTS, (m_ffn1_norm, m_ffn1_wg, m_ffn1_wu, m_ffn1_wd, m_mix_norm, m_w_in, m_gla_w_a2, m_gla_b_a, m_gla_gn_w, m_rwkv_mu, m_rwkv_w0, m_rwkv_w_w2, m_rwkv_a0, m_rwkv_w_a2, m_rwkv_w_g2, m_rwkv_k_k, m_rwkv_k_a, m_rwkv_r_k, m_rwkv_lnx_w, m_rwkv_lnx_b, m_gate_b, m_w_branch, m_w_out, m_ffn2_norm, m_ffn2_wg, m_ffn2_wu, m_ffn2_wd, m_final_norm)))
    var = dict(zip(WEIGHTS, (v_ffn1_norm, v_ffn1_wg, v_ffn1_wu, v_ffn1_wd, v_mix_norm, v_w_in, v_gla_w_a2, v_gla_b_a, v_gla_gn_w, v_rwkv_mu, v_rwkv_w0, v_rwkv_w_w2, v_rwkv_a0, v_rwkv_w_a2, v_rwkv_w_g2, v_rwkv_k_k, v_rwkv_k_a, v_rwkv_r_k, v_rwkv_lnx_w, v_rwkv_lnx_b, v_gate_b, v_w_branch, v_w_out, v_ffn2_norm, v_ffn2_wg, v_ffn2_wu, v_ffn2_wd, v_final_norm)))
    two = lambda a: a.reshape(a.shape[-2:])

    bf = lambda n: two(wts[n]).astype(BF16)
    up1 = jnp.stack([bf("ffn1_wg"), bf("ffn1_wu")])
    lora = jnp.concatenate([jnp.pad(two(gla_w_a2), ((0, 0), (0, 128 - GLA_QK // NDEV)))] +
                           [two(wts[n]) for n in SMALL_SHARDED[1:]], axis=0)
    g_up1, g_lora = _all_gather([up1, lora], "gather_weights")
    gb = {"ffn1_wg": g_up1[:, 0], "ffn1_wu": g_up1[:, 1]}
    gs = {"gla_w_a2": g_lora[:, :GLA_LORA, :GLA_QK // NDEV]}
    row = GLA_LORA
    for n in SMALL_SHARDED[1:]:
        gs[n] = g_lora[:, row:row + wts[n].shape[1]]
        row += wts[n].shape[1]
    w = _layout_weights(gb, gs, {n: wts[n] for n in REPLICATED})
    blocks = {"wd1": bf("ffn1_wd"), "win": bf("w_in"),
              "late": [jnp.stack([bf("ffn2_wg"), bf("ffn2_wu")]), jnp.concatenate([bf(n) for n, _ in LATE_ROWS], axis=0)]}

    loss_part, grad_x, grads, parts = _local_step(x[0], loss_target[0], w, blocks)
    small = _layout_grads(grads)
    result = {}
    state = lambda n: (two(wts[n]), two(mom[n]), two(var[n]))
    for group, names in (("up1", ("ffn1_wg", "ffn1_wu")), ("up2", ("ffn2_wg", "ffn2_wu"))):
        for i, n in enumerate(names):
            result[n] = _adamw(parts[group], *state(n), 256, "adamw_" + n, stack_index=i)
    result["ffn1_wd"] = _adamw(parts["wd1"], *state("ffn1_wd"), 64, "adamw_ffn1_wd")
    row = 0
    for n, rows in LATE_ROWS:
        result[n] = _adamw(parts["down2"], *state(n), 64, "adamw_" + n, row_block_offset=row // 64)
        row += rows
    result["w_in"] = _adamw(parts["win"], *state("w_in"), 256, "adamw_w_in")

    small_names = [n for n in WEIGHTS if n not in BIG]
    full_shapes = [small[n].shape for n in small_names]
    rows_full = _rows_for(full_shapes, extra=128)
    packed = _pack([small[n] for n in small_names] + [loss_part], rows_full)
    (gathered,) = _all_gather([packed], "gather_small_grads")
    total = _sum_gathered(gathered)
    *full_grads, loss_row = _unpack(total, full_shapes + [(1, 128)])
    me = _slot(_position())
    own = {}
    for n, g in zip(small_names, full_grads):
        if n in SMALL_SHARDED:
            cols = wts[n].shape[-1]
            g = lax.dynamic_slice_in_dim(g, me * cols, cols, axis=1)
        own[n] = g.reshape(wts[n].shape)
    own_shapes = [wts[n].shape for n in small_names]
    rows_own = _rows_for(own_shapes)
    pk = lambda d: _pack([d[n] for n in small_names], rows_own)
    d_s, m_s, v_s = _adamw_small(pk(wts), pk(own), pk(mom), pk(var))
    for n, d, m, v in zip(small_names, _unpack(d_s, own_shapes), _unpack(m_s, own_shapes), _unpack(v_s, own_shapes)):
        result[n] = (own[n], d, m, v)

    shaped = lambda n, k: result[n][k].reshape(wts[n].shape)
    return (loss_row[0, 0], grad_x[None],
            *[shaped(n, 0) for n in WEIGHTS], *[shaped(n, 1) for n in WEIGHTS],
            *[shaped(n, 2) for n in WEIGHTS], *[shaped(n, 3) for n in WEIGHTS])
```

```python
import functools

import jax
import jax.numpy as jnp
from jax import lax
from jax.experimental import pallas as pl
from jax.experimental.pallas import tpu as pltpu

F32 = jnp.float32
BF16 = jnp.bfloat16
HI = lax.Precision.HIGHEST

NDEV = 8
D = 2048
DFF = 5632
FSH = DFF // NDEV
CHUNK = 64
GLA_HEADS, GLA_DK, GLA_DV = 4, 128, 256
GLA_QK, GLA_V, GLA_LORA, GLA_TAU = 512, 1024, 16, 16.0
RW_HEADS, RW_HD, RW_W = 16, 64, 1024
DECAY_LORA, AAA_LORA, GATE_LORA = 96, 96, 256
GN_EPS = 64e-5
NORM_EPS = 1e-6
GLA_IN = 2 * GLA_QK + 2 * GLA_V + GLA_LORA
RW_IN = 3 * RW_W + DECAY_LORA + AAA_LORA + GATE_LORA
D_IN = GLA_IN + RW_IN + 2 * D
DIN_SH = D_IN // NDEV
PG_W = 2 * D
PR_W = 3584
PA_W = 3584
PA_USED = 2 * GLA_QK + 2 * GLA_V + 128
DIN_P = PG_W + PR_W + PA_W
LORA_P = 128

ADAM_LR, ADAM_B1, ADAM_B2, ADAM_EPS, ADAM_WD, ADAM_STEP = 0.001, 0.9, 0.999, 1e-08, 0.01, 10

VMEM_LIMIT = 56 * 1024 * 1024
RW_TB = 64
RW_G = 16
RW_NP = 4


def _params(sem=None, vmem=VMEM_LIMIT):
    return pltpu.CompilerParams(dimension_semantics=sem, vmem_limit_bytes=vmem)


def _pair_mask():
    return lax.broadcasted_iota(jnp.int32, (RW_HD, 2 * RW_HD), 1) < RW_HD


def _pair_rowsum(p, mask):
    tot = jnp.sum(p, axis=1, keepdims=True)
    first = jnp.sum(jnp.where(mask, p, 0.0), axis=1, keepdims=True)
    return first, tot - first


def _split_transposed(x_ref, q, dst_ref, base):
    xt = x_ref[:, 128 * q:128 * (q + 1)].T
    for g in range(RW_TB // RW_G):
        dst_ref[base + g, :, 0:RW_G] = xt[:, g * RW_G:(g + 1) * RW_G]


def _pair_column(tile_ref, idx, i, mask):
    return jnp.where(mask, tile_ref[idx, 0:RW_HD, i:i + 1], tile_ref[idx, RW_HD:, i:i + 1])


def _rw_core_fwd(rw, w, k2, kk, b, gather=()):
    T = rw.shape[0]
    nb = T // RW_TB
    ng = RW_TB // RW_G
    NP = RW_NP
    nc = len(gather)
    npair = RW_HEADS // 2 // NP

    def body(r_ref, v_ref, w_ref, k_ref, kk_ref, b_ref, *rest):
        g_in, (y_ref, st_ref, sa_ref), g_out = rest[:nc], rest[nc:nc + 3], rest[nc + 3:2 * nc + 3]
        s_scr, vt_scr, yt_scr, rows_scr = rest[2 * nc + 3:2 * nc + 7]
        pair, blk_i = pl.program_id(0), pl.program_id(1)
        if nc:
            start, forward, finish = _gather_plan(g_in, g_out, *rest[2 * nc + 7:])
            pl.when((pair == 0) & (blk_i == 0))(start)
            pl.when((pair == 0) & (blk_i == nb // 2))(forward)

        @pl.when(pl.program_id(1) == 0)
        def _():
            s_scr[...] = jnp.zeros_like(s_scr)
            yt_scr[...] = jnp.zeros_like(yt_scr)

        mask = _pair_mask()
        for q in range(NP):
            _split_transposed(v_ref, q, vt_scr, q * ng)
        R_, W_, K_, KK_, B_ = range(5)
        for a, ref in enumerate((r_ref, w_ref, k_ref, kk_ref, b_ref)):
            for q in range(NP):
                rows_scr[a * NP + q] = ref[:, 128 * q:128 * (q + 1)]

        def group(g, states):
            states = list(states)
            for i in range(RW_G):
                t = g * RW_G + i
                row = lambda a, q: rows_scr[a * NP + q, pl.ds(t, 1), :]
                sums = [_pair_rowsum(states[q] * row(KK_, q), mask) for q in range(NP)]
                for q in range(NP):
                    sa = jnp.where(mask, *sums[q])
                    sa_ref[q, t] = sa
                    states[q] = (states[q] * row(W_, q) - sa * row(B_, q)
                                 + _pair_column(vt_scr, q * ng + g, i, mask) * row(K_, q))
                    st_ref[q, t] = states[q]
                outs = [_pair_rowsum(states[q] * row(R_, q), mask) for q in range(NP)]
                for q in range(NP):
                    yt_scr[q * ng + g, 0:RW_HD, i:i + 1] = outs[q][0]
                    yt_scr[q * ng + g, RW_HD:, i:i + 1] = outs[q][1]
            return tuple(states)

        states = lax.fori_loop(0, ng, group, tuple(s_scr[q] for q in range(NP)))
        for q in range(NP):
            s_scr[q] = states[q]
            for g in range(ng):
                y_ref[g * RW_G:(g + 1) * RW_G, 128 * q:128 * (q + 1)] = yt_scr[q * ng + g].T[0:RW_G, :]
        if nc:
            pl.when((pair == npair - 1) & (blk_i == nb - 1))(finish)

    blk = lambda cb: pl.BlockSpec((RW_TB, 128 * NP), lambda p, i, cb=cb: (i, cb + p))
    tiles = pltpu.VMEM((NP * ng, 128, 128), F32)
    return pl.pallas_call(
        body, name="rw_core_fwd", grid=(npair, nb),
        in_specs=[blk(0), blk(2 * RW_W // (128 * NP)), blk(0), blk(0), blk(0), blk(0)] + [_ANY] * nc,
        out_specs=[blk(0)] + [pl.BlockSpec((NP, RW_TB, RW_HD, 128), lambda p, i: (p, i, 0, 0))] * 2 + [_ANY] * nc,
        out_shape=[jax.ShapeDtypeStruct((T, RW_W), F32)] + [jax.ShapeDtypeStruct((RW_HEADS // 2, T, RW_HD, 128), F32)] * 2
        + [jax.ShapeDtypeStruct((NDEV,) + a.shape, a.dtype) for a in gather],
        scratch_shapes=[pltpu.VMEM((NP, RW_HD, 128), F32), tiles, tiles, pltpu.VMEM((5 * NP, RW_TB, 128), F32)]
        + _comm_sems(nc),
        compiler_params=_params(("arbitrary", "arbitrary")),
    )(rw, rw, w, k2, kk, b, *gather)


def _rw_core_bwd(rw, w, k2, kk, b, states, sa_tiles, dy, exchange=()):
    T = rw.shape[0]
    nb = T // RW_TB
    ng = RW_TB // RW_G
    NP = RW_NP
    nc = len(exchange)
    npair = RW_HEADS // 2 // NP

    def body(r_ref, v_ref, w_ref, k_ref, kk_ref, b_ref, dy_ref, st_ref, sp_ref, sa_ref, *rest):
        e_in, e_out = rest[:nc], rest[nc + 6:2 * nc + 6]
        dr_ref, dw_ref, dk_ref, dv_ref, dkk_ref, db_ref = rest[nc:nc + 6]
        ds_scr, vt_scr, dyt_scr, dvt_scr, rows_scr, out_scr = rest[2 * nc + 6:2 * nc + 12]
        step = pl.program_id(1)
        if nc:
            start, finish = _exchange_plan(e_in, e_out, *rest[2 * nc + 12:])
            pl.when((pl.program_id(0) == 0) & (step == 0))(start)

        @pl.when(step == 0)
        def _():
            ds_scr[...] = jnp.zeros_like(ds_scr)
            dvt_scr[...] = jnp.zeros_like(dvt_scr)

        mask = _pair_mask()
        for q in range(NP):
            _split_transposed(v_ref, q, vt_scr, q * ng)
            _split_transposed(dy_ref, q, dyt_scr, q * ng)
        R_, W_, K_, KK_, B_ = range(5)
        for a, ref in enumerate((r_ref, w_ref, k_ref, kk_ref, b_ref)):
            for q in range(NP):
                rows_scr[a * NP + q] = ref[:, 128 * q:128 * (q + 1)]

        def group(gg, grads):
            g = ng - 1 - gg
            grads = list(grads)
            pairs = range(NP)
            for i in reversed(range(RW_G)):
                t = g * RW_G + i
                row = lambda a, q: rows_scr[a * NP + q, pl.ds(t, 1), :]

                def put(a, q, value):
                    out_scr[a * NP + q, pl.ds(t, 1), :] = value

                s_old = [st_ref[q, jnp.maximum(t - 1, 0)] for q in pairs]
                if i == 0:
                    s_old = [jnp.where(g == 0, jnp.where(step == nb - 1, 0.0, sp_ref[q, 0]), s_old[q]) for q in pairs]
                dycol = [_pair_column(dyt_scr, q * ng + g, i, mask) for q in pairs]
                dS = [grads[q] + dycol[q] * row(R_, q) for q in pairs]
                m = [_pair_rowsum(dS[q] * row(B_, q), mask) for q in pairs]
                dv = [_pair_rowsum(dS[q] * row(K_, q), mask) for q in pairs]
                for q in pairs:
                    put(R_, q, jnp.sum(st_ref[q, t] * dycol[q], axis=0, keepdims=True))
                    put(W_, q, jnp.sum(dS[q] * s_old[q], axis=0, keepdims=True))
                    put(K_, q, jnp.sum(dS[q] * _pair_column(vt_scr, q * ng + g, i, mask), axis=0, keepdims=True))
                for q in pairs:
                    dsa = -jnp.where(mask, *m[q])
                    grads[q] = dS[q] * row(W_, q) + dsa * row(KK_, q)
                    put(KK_, q, jnp.sum(s_old[q] * dsa, axis=0, keepdims=True))
                    put(B_, q, -jnp.sum(dS[q] * sa_ref[q, t], axis=0, keepdims=True))
                    dvt_scr[q * ng + g, 0:RW_HD, i:i + 1] = dv[q][0]
                    dvt_scr[q * ng + g, RW_HD:, i:i + 1] = dv[q][1]
            return tuple(grads)

        grads = lax.fori_loop(0, ng, group, tuple(ds_scr[q] for q in range(NP)))
        for q in range(NP):
            ds_scr[q] = grads[q]
            for a, ref in enumerate((dr_ref, dw_ref, dk_ref, dkk_ref, db_ref)):
                ref[:, 128 * q:128 * (q + 1)] = out_scr[a * NP + q]
            for g in range(ng):
                dv_ref[g * RW_G:(g + 1) * RW_G, 128 * q:128 * (q + 1)] = dvt_scr[q * ng + g].T[0:RW_G, :]
        if nc:
            pl.when((pl.program_id(0) == npair - 1) & (step == nb - 1))(finish)

    blk = lambda cb: pl.BlockSpec((RW_TB, 128 * NP), lambda p, i, cb=cb: (nb - 1 - i, cb + p))
    st_spec = pl.BlockSpec((NP, RW_TB, RW_HD, 128), lambda p, i: (p, nb - 1 - i, 0, 0))
    sp_spec = pl.BlockSpec((NP, 1, RW_HD, 128), lambda p, i: (p, jnp.maximum((nb - 1 - i) * RW_TB - 1, 0), 0, 0))
    out = jax.ShapeDtypeStruct((T, RW_W), F32)
    tiles = pltpu.VMEM((NP * ng, 128, 128), F32)
    return pl.pallas_call(
        body, name="rw_core_bwd", grid=(npair, nb),
        in_specs=[blk(0), blk(2 * RW_W // (128 * NP)), blk(0), blk(0), blk(0), blk(0), blk(0), st_spec, sp_spec, st_spec]
        + [_ANY] * nc,
        out_specs=[blk(0)] * 6 + [_ANY] * nc,
        out_shape=[out] * 6 + [jax.ShapeDtypeStruct(a.shape, a.dtype) for a in exchange],
        scratch_shapes=[pltpu.VMEM((NP, RW_HD, 128), F32), tiles, tiles, tiles,
                        pltpu.VMEM((5 * NP, RW_TB, 128), F32), pltpu.VMEM((5 * NP, RW_TB, 128), F32)] + _comm_sems(nc),
        compiler_params=_params(("arbitrary", "arbitrary")),
    )(rw, rw, w, k2, kk, b, dy, states, states, sa_tiles, *exchange)


GLA_CB = 8


def _gla_chunk(s_t, q, k, v, la, ltri):
    cum = jnp.dot(ltri, la, precision=HI, preferred_element_type=F32)
    total = jnp.sum(la, axis=0, keepdims=True)
    kdec = k * jnp.exp(total - cum)
    u_t = _bdot(v, kdec, _TN)
    s_t = jnp.exp(total) * s_t + u_t
    o = _bdot(q * (GLA_DK ** -0.5), s_t, _NT)
    return s_t, o


def _gla_core_fwd(pa, la, ltri):
    T = pa.shape[0]
    cb = min(GLA_CB, T // CHUNK)
    rows = cb * CHUNK
    nsteps = T // rows

    def body(q_ref, k_ref, v_ref, la_ref, ltri_ref, o_ref, st_ref, s_scr):
        @pl.when(pl.program_id(1) == 0)
        def _():
            s_scr[...] = jnp.zeros_like(s_scr)

        def chunk(c, s_t):
            sl = pl.ds(pl.multiple_of(c * CHUNK, CHUNK), CHUNK)
            s_t, o = _gla_chunk(s_t, q_ref[sl, :], k_ref[sl, :], v_ref[sl, :], la_ref[sl, :], ltri_ref[...])
            o_ref[sl, :] = o
            st_ref[0, c] = s_t
            return s_t

        s_scr[...] = lax.fori_loop(0, cb, chunk, s_scr[...])

    qk = lambda off: pl.BlockSpec((rows, GLA_DK), lambda h, i, off=off: (i, off + h))
    vspec = pl.BlockSpec((rows, GLA_DV), lambda h, i: (i, 2 * GLA_QK // GLA_DV + h))
    return pl.pallas_call(
        body, name="gla_core_fwd", grid=(GLA_HEADS, nsteps),
        in_specs=[qk(0), qk(GLA_HEADS), vspec, qk(0), pl.BlockSpec((CHUNK, CHUNK), lambda h, i: (0, 0))],
        out_specs=[pl.BlockSpec((rows, GLA_DV), lambda h, i: (i, h)),
                   pl.BlockSpec((1, cb, GLA_DV, GLA_DK), lambda h, i: (h, i, 0, 0))],
        out_shape=(jax.ShapeDtypeStruct((T, GLA_V), F32),
                   jax.ShapeDtypeStruct((GLA_HEADS, T // CHUNK, GLA_DV, GLA_DK), F32)),
        scratch_shapes=[pltpu.VMEM((GLA_DV, GLA_DK), F32)],
        compiler_params=_params(("arbitrary", "arbitrary")),
    )(pa, pa, pa, la, ltri)


def _gla_core_bwd(pa, la, ltri, states, do):
    T = pa.shape[0]
    cb = min(GLA_CB, T // CHUNK)
    rows = cb * CHUNK
    nsteps = T // rows

    def body(q_ref, k_ref, v_ref, la_ref, ltri_ref, st_ref, sp_ref, do_ref,
             dq_ref, dk_ref, dv_ref, dla_ref, ds_scr):
        step = pl.program_id(1)

        @pl.when(step == 0)
        def _():
            ds_scr[...] = jnp.zeros_like(ds_scr)

        s_before = jnp.where(step == nsteps - 1, 0.0, sp_ref[0, 0])

        def chunk(cc, ds_t):
            c = cb - 1 - cc
            sl = pl.ds(pl.multiple_of(c * CHUNK, CHUNK), CHUNK)
            s_prev = jnp.where(c == 0, s_before, st_ref[0, jnp.maximum(c - 1, 0)])
            _, vjp = jax.vjp(functools.partial(_gla_chunk, ltri=ltri_ref[...]),
                             s_prev, q_ref[sl, :], k_ref[sl, :], v_ref[sl, :], la_ref[sl, :])
            ds_prev, dq, dk, dv, dla = vjp((ds_t, do_ref[sl, :]))
            dq_ref[sl, :] = dq
            dk_ref[sl, :] = dk
            dv_ref[sl, :] = dv
            dla_ref[sl, :] = dla
            return ds_prev

        ds_scr[...] = lax.fori_loop(0, cb, chunk, ds_scr[...])

    r = lambda i: nsteps - 1 - i
    qk = lambda off: pl.BlockSpec((rows, GLA_DK), lambda h, i, off=off: (r(i), off + h))
    vspec = pl.BlockSpec((rows, GLA_DV), lambda h, i: (r(i), 2 * GLA_QK // GLA_DV + h))
    o128 = pl.BlockSpec((rows, GLA_DK), lambda h, i: (r(i), h))
    o256 = pl.BlockSpec((rows, GLA_DV), lambda h, i: (r(i), h))
    return pl.pallas_call(
        body, name="gla_core_bwd", grid=(GLA_HEADS, nsteps),
        in_specs=[qk(0), qk(GLA_HEADS), vspec, qk(0), pl.BlockSpec((CHUNK, CHUNK), lambda h, i: (0, 0)),
                  pl.BlockSpec((1, cb, GLA_DV, GLA_DK), lambda h, i: (h, r(i), 0, 0)),
                  pl.BlockSpec((1, 1, GLA_DV, GLA_DK), lambda h, i: (h, jnp.maximum(r(i) * cb - 1, 0), 0, 0)),
                  o256],
        out_specs=[o128, o128, o256, o128],
        out_shape=(jax.ShapeDtypeStruct((T, GLA_QK), F32), jax.ShapeDtypeStruct((T, GLA_QK), F32),
                   jax.ShapeDtypeStruct((T, GLA_V), F32), jax.ShapeDtypeStruct((T, GLA_QK), F32)),
        scratch_shapes=[pltpu.VMEM((GLA_DV, GLA_DK), F32)],
        compiler_params=_params(("arbitrary", "arbitrary")),
    )(pa, pa, pa, la, ltri, states, states, do)


def _rowwise(fn, name, T, tm, rows, pars, row_outs, acc_outs):
    nr, npar, nro = len(rows), len(pars), len(row_outs)
    tm = min(tm, T)
    nsteps = T // tm

    def body(*refs):
        i = pl.program_id(0)
        ins = [r[...] for r in refs[:nr + npar]]
        outs, accs = fn(i, *ins)
        for r, o in zip(refs[nr + npar:nr + npar + nro], outs):
            r[...] = o.astype(r.dtype)
        for r, a in zip(refs[nr + npar + nro:], accs):
            @pl.when(i == 0)
            def _(r=r, a=a):
                r[...] = a

            @pl.when(i > 0)
            def _(r=r, a=a):
                r[...] += a

    def rspec(width, cb, kind):
        if kind == "cur":
            return pl.BlockSpec((tm, width), lambda i: (i, cb))
        if kind == "prev":
            return pl.BlockSpec((8, width), lambda i: (jnp.maximum(i * (tm // 8) - 1, 0), cb))
        return pl.BlockSpec((8, width), lambda i: (jnp.minimum((i + 1) * (tm // 8), T // 8 - 1), cb))

    in_specs = [rspec(w, cb, kind) for (_, w, cb, kind) in rows]
    in_specs += [pl.BlockSpec(p.shape, lambda i, nd=p.ndim: (0,) * nd) for p in pars]
    out_specs = [pl.BlockSpec((tm, w), lambda i: (i, 0)) for (w, _) in row_outs]
    out_specs += [pl.BlockSpec(s, lambda i, nd=len(s): (0,) * nd) for s in acc_outs]
    out_shape = [jax.ShapeDtypeStruct((T, w), dt) for (w, dt) in row_outs]
    out_shape += [jax.ShapeDtypeStruct(s, F32) for s in acc_outs]
    res = pl.pallas_call(
        body, name=name, grid=(nsteps,), in_specs=in_specs, out_specs=out_specs, out_shape=out_shape,
        compiler_params=_params(("arbitrary",)),
    )(*[r[0] for r in rows], *pars)
    return res


def _cur(a, width=None, cb=0):
    return (a, a.shape[1] if width is None else width, cb, "cur")


def _sigmoid(x):
    return 1.0 / (1.0 + jnp.exp(-x))


def _silu(x):
    return x * _sigmoid(x)


def _softplus(x):
    return jnp.maximum(x, 0.0) + jnp.log(1.0 + jnp.exp(-jnp.abs(x)))


def _rms(x, g):
    return x * lax.rsqrt(jnp.mean(x * x, axis=-1, keepdims=True) + NORM_EPS) * g


def _dot_hi(a, b):
    return jnp.dot(a, b, precision=HI, preferred_element_type=F32)


def _rms_fwd(x, g, name):
    T = x.shape[0]
    fn = lambda i, xb, gb: ((_rms(xb, gb),), ())
    return _rowwise(fn, name, T, 256, [_cur(x)], [g], [(D, BF16)], [])[0]


def _rms_bwd(x, g, dh, dres, name):
    T = x.shape[0]

    def fn(i, xb, dhb, drb, gb):
        _, vjp = jax.vjp(_rms, xb, gb)
        dx, dg = vjp(dhb)
        return (drb + dx,), (dg,)

    return _rowwise(fn, name, T, 256, [_cur(x), _cur(dh), _cur(dres)], [g], [(D, F32)], [(1, D)])


def _loss_bwd(x, target, g):
    T = x.shape[0]

    def loss(xb, gb, tb):
        err = _rms(xb, gb) - tb
        return 0.5 * jnp.sum(jnp.mean(err * err, axis=-1, keepdims=True))

    def fn(i, xb, tb, gb):
        val, (dx, dg) = jax.value_and_grad(loss, argnums=(0, 1))(xb, gb, tb)
        return (dx,), (jnp.full((1, 128), val, F32), dg)

    return _rowwise(fn, "loss_bwd", T, 256, [_cur(x), _cur(target)], [g], [(D, F32)], [(1, 128), (1, D)])


def _gla_la(a_down, w_a2, b_a):
    return -_softplus(-(_bdot(a_down, w_a2, _NN) + b_a)) * (1.0 / GLA_TAU)


def _gla_prep(pa, w_a2, b_a):
    T = pa.shape[0]
    fn = lambda i, ab, wb, bb: ((_gla_la(ab, wb, bb),), ())
    return _rowwise(fn, "gla_prep", T, 512, [_cur(pa, LORA_P, (2 * GLA_QK + 2 * GLA_V) // LORA_P)], [w_a2, b_a],
                    [(GLA_QK, F32)], [])[0]


def _gla_prep_bwd(pa, w_a2, b_a, dla):
    T = pa.shape[0]

    def fn(i, ab, dlab, wb, bb):
        _, vjp = jax.vjp(_gla_la, ab, wb, bb)
        da, dw, db = vjp(dlab)
        return (da,), (dw, db)

    return _rowwise(fn, "gla_prep_bwd", T, 512, [_cur(pa, LORA_P, (2 * GLA_QK + 2 * GLA_V) // LORA_P), _cur(dla)],
                    [w_a2, b_a], [(LORA_P, BF16)], [(LORA_P, GLA_QK), (1, GLA_QK)])


def _gla_out(o, r, gn, ind, ind_t):
    ms = _dot_hi(_dot_hi(o * o, ind) * (1.0 / GLA_DV), ind_t)
    return o * lax.rsqrt(ms + NORM_EPS) * gn * _silu(r)


def _gla_post(o_raw, pa, gn, ind, ind_t):
    T = pa.shape[0]
    fn = lambda i, ob, rb, gb, a, b: ((_gla_out(ob, rb, gb, a, b),), ())
    return _rowwise(fn, "gla_post", T, 256, [_cur(o_raw), _cur(pa, GLA_V, 2)], [gn, ind, ind_t], [(GLA_V, BF16)], [])[0]


def _gla_post_bwd(o_raw, pa, gn, ind, ind_t, do):
    T = pa.shape[0]

    def fn(i, ob, rb, dob, gb, a, b):
        _, vjp = jax.vjp(lambda o, r, g: _gla_out(o, r, g, a, b), ob, rb, gb)
        d_o, d_r, d_g = vjp(dob)
        return (d_o, d_r), (d_g,)

    return _rowwise(fn, "gla_post_bwd", T, 256, [_cur(o_raw), _cur(pa, GLA_V, 2), _cur(do)], [gn, ind, ind_t],
                    [(GLA_V, F32), (GLA_V, BF16)], [(1, GLA_V)])


def _shift_rows(cur, prev8, i):
    first = jnp.where(i == 0, 0.0, prev8[7:8, :])
    rolled = pltpu.roll(cur, 1, 0)
    return jnp.where(lax.broadcasted_iota(jnp.int32, cur.shape, 0) == 0, first, rolled)


def _rw_gates(rw, w0, w_w2, a0, w_a2, w_g2, k_k, k_a, ind, ind_t):
    rk = rw[:, RW_W:2 * RW_W]
    wd = rw[:, 3 * RW_W:3 * RW_W + LORA_P]
    ad = rw[:, 3 * RW_W + LORA_P:3 * RW_W + 2 * LORA_P]
    gd = rw[:, 3 * RW_W + 2 * LORA_P:]
    w_raw = w0 + _bdot(jnp.tanh(wd), w_w2, _NN)
    w = jnp.exp(-jnp.exp(-_softplus(-w_raw) - 0.5))
    a = _sigmoid(a0 + _bdot(ad, w_a2, _NN))
    g = _bdot(_sigmoid(gd), w_g2, _NN)
    kk = rk * k_k
    kk = kk * _dot_hi(lax.rsqrt(jnp.maximum(_dot_hi(kk * kk, ind), 1e-24)), ind_t)
    k2 = rk * (1.0 + (a - 1.0) * k_a)
    return w, k2, kk, kk * a, g


def _rw_prep(pr, mu, gate_pars):
    T = pr.shape[0]

    def fn(i, cur, prev8, mub, *gp):
        rw = cur + mub * (_shift_rows(cur, prev8, i) - cur)
        return (rw,) + _rw_gates(rw, *gp), ()

    return _rowwise(fn, "rw_prep", T, 256, [_cur(pr), (pr, PR_W, 0, "prev")], [mu, *gate_pars],
                    [(PR_W, F32)] + [(RW_W, F32)] * 5, [])


def _rw_prep_bwd(pr, mu, gate_pars, d_r, d_v, d_w, d_k2, d_kk, d_b, d_g):
    T = pr.shape[0]
    rows = [_cur(pr), (pr, PR_W, 0, "prev")] + [_cur(x) for x in (*d_r, *d_v, d_w, *d_k2, d_kk, d_b, d_g)]
    acc = [(1, PR_W)] + [tuple(p.shape) for p in gate_pars[:-2]]

    def fn(i, cur, prev8, dr1, dr2, dv1, dv2, dw, dk1, dk2, dkk, db, dg, mub, *gp):
        sh = _shift_rows(cur, prev8, i)
        rw = cur + mub * (sh - cur)
        _, vjp = jax.vjp(lambda x, *p: _rw_gates(x, *p, gp[-2], gp[-1]), rw, *gp[:-2])
        grads = vjp((dw, dk1 + dk2, dkk, db, dg))
        zeros = jnp.zeros((cur.shape[0], PR_W - 3 * RW_W), F32)
        drw = grads[0] + jnp.concatenate([dr1 + dr2, jnp.zeros_like(dr1), dv1 + dv2, zeros], axis=1)
        dmu = jnp.sum(drw * (sh - cur), axis=0, keepdims=True)
        return (drw,), (dmu, *grads[1:])

    return _rowwise(fn, "rw_prep_bwd", T, 128, rows, [mu, *gate_pars], [(PR_W, F32)], acc)


def _shift_bwd(drw, mu):
    T = drw.shape[0]
    tm = min(256, T)

    def fn(i, cur, next8, mub):
        last = jnp.where(i == T // tm - 1, 0.0, next8[0:1, :])
        rolled = pltpu.roll(cur, cur.shape[0] - 1, 0)
        nxt = jnp.where(lax.broadcasted_iota(jnp.int32, cur.shape, 0) == cur.shape[0] - 1, last, rolled)
        return ((1.0 - mub) * cur + mub * nxt,), ()

    return _rowwise(fn, "shift_bwd", T, tm, [_cur(drw), (drw, PR_W, 0, "next")], [mu], [(PR_W, BF16)], [])[0]


def _rw_out(y, r, v, k2, g, lnx_w, lnx_b, r_k, ind, ind_t):
    mean = _dot_hi(_dot_hi(y, ind) * (1.0 / RW_HD), ind_t)
    yc = y - mean
    var = _dot_hi(_dot_hi(yc * yc, ind) * (1.0 / RW_HD), ind_t)
    yn = yc * lax.rsqrt(var + GN_EPS) * lnx_w + lnx_b
    bonus = _dot_hi(_dot_hi(r * k2 * r_k, ind), ind_t) * v
    return (yn + bonus) * g


def _rw_post(y, rw, k2, g, pars):
    T = y.shape[0]
    fn = lambda i, *a: ((_rw_out(*a),), ())
    return _rowwise(fn, "rw_post", T, 256, [_cur(y), _cur(rw, RW_W, 0), _cur(rw, RW_W, 2), _cur(k2), _cur(g)], pars,
                    [(RW_W, BF16)], [])[0]


def _rw_post_bwd(y, rw, k2, g, pars, do):
    T = y.shape[0]

    def fn(i, yb, rb, vb, kb, gb, dob, lw, lb, rk, ind, ind_t):
        _, vjp = jax.vjp(lambda *a: _rw_out(*a, ind, ind_t), yb, rb, vb, kb, gb, lw, lb, rk)
        gr = vjp(dob)
        return gr[:5], gr[5:]

    return _rowwise(fn, "rw_post_bwd", T, 256,
                    [_cur(y), _cur(rw, RW_W, 0), _cur(rw, RW_W, 2), _cur(k2), _cur(g), _cur(do)], pars,
                    [(RW_W, F32)] * 5, [(1, RW_W)] * 3)


def _merge_bwd(dm, y_gla, y_rw, pg, gate_b):
    T = dm.shape[0]

    def fn(i, dmb, ya, yr, p1, p2, gb):
        g1 = _sigmoid(p1 + gb[:, :D])
        g2 = _sigmoid(p2 + gb[:, D:])
        dp1 = dmb * ya * g1 * (1.0 - g1)
        dp2 = dmb * yr * g2 * (1.0 - g2)
        dp = jnp.concatenate([dp1, dp2], axis=1)
        return (dmb * g1, dmb * g2, dp), (jnp.sum(dp, axis=0, keepdims=True),)

    return _rowwise(fn, "merge_bwd", T, 256, [_cur(dm), _cur(y_gla), _cur(y_rw), _cur(pg, D, 0), _cur(pg, D, 1)],
                    [gate_b], [(D, BF16), (D, BF16), (PG_W, BF16)], [(1, PG_W)])


_NN = (((1,), (0,)), ((), ()))
_NT = (((1,), (1,)), ((), ()))
_TN = (((0,), (0,)), ((), ()))


def _bdot(a, b, dims):
    return lax.dot_general(a.astype(BF16), b.astype(BF16), dims, preferred_element_type=F32)


def _accumulate(k, nk, acc, part, finish):
    if nk == 1:
        finish(part)
        return

    @pl.when(k == 0)
    def _():
        acc[...] = part

    @pl.when(k > 0)
    def _():
        acc[...] += part

    @pl.when(k == nk - 1)
    def _():
        finish(acc[...])


def _call(body, comm, name, grid, in_specs, out_specs, out_shape, scratch_shapes, sem, operands):
    if comm is None:
        return pl.pallas_call(body, name=name, grid=grid, in_specs=in_specs, out_specs=out_specs, out_shape=out_shape,
                              scratch_shapes=scratch_shapes, compiler_params=_params(sem))(*operands)
    kind, arrays = comm
    nc, n_in, n_out, n_scr = len(arrays), len(in_specs), len(out_shape), len(scratch_shapes)
    total = 1
    for n in grid:
        total *= n

    def with_comm(*refs):
        own = refs[:n_in] + refs[n_in + nc:n_in + nc + n_out] + refs[n_in + 2 * nc + n_out:n_in + 2 * nc + n_out + n_scr]
        c_in, c_out, sems = refs[n_in:n_in + nc], refs[n_in + nc + n_out:n_in + 2 * nc + n_out], refs[-3:]
        step = 0
        for axis, n in enumerate(grid):
            step = step * n + pl.program_id(axis)
        if kind == "gather":
            start, forward, finish = _gather_plan(c_in, c_out, *sems)
            pl.when(step == 0)(start)
            pl.when(step == total // 2)(forward)
        else:
            start, finish = _exchange_plan(c_in, c_out, *sems)
            pl.when(step == 0)(start)
        body(*own)
        pl.when(step == total - 1)(finish)

    lead = (NDEV,) if kind == "gather" else ()
    return pl.pallas_call(
        with_comm, name=name, grid=grid, in_specs=list(in_specs) + [_ANY] * nc, out_specs=list(out_specs) + [_ANY] * nc,
        out_shape=list(out_shape) + [jax.ShapeDtypeStruct(lead + a.shape, a.dtype) for a in arrays],
        scratch_shapes=list(scratch_shapes) + _comm_sems(nc), compiler_params=_params(("arbitrary",) * len(grid)),
    )(*operands, *arrays)


def _matmul(a, b, mode, M, N, K, tm, tn, tk, name, a_off=(0, 0), b_off=(0, 0), res=None, scale=1.0, out_dtype=F32,
            comm=None):
    tm, tn, tk = min(tm, M), min(tn, N), min(tk, K)
    nk = K // tk
    if mode == "nn":
        a_spec = pl.BlockSpec((tm, tk), lambda i, j, k: (i + a_off[0], k + a_off[1]))
        b_spec = pl.BlockSpec((tk, tn), lambda i, j, k: (k + b_off[0], j + b_off[1]))
        dims = _NN
    elif mode == "nt":
        a_spec = pl.BlockSpec((tm, tk), lambda i, j, k: (i + a_off[0], k + a_off[1]))
        b_spec = pl.BlockSpec((tn, tk), lambda i, j, k: (j + b_off[0], k + b_off[1]))
        dims = _NT
    else:
        a_spec = pl.BlockSpec((tk, tm), lambda i, j, k: (k + a_off[0], i + a_off[1]))
        b_spec = pl.BlockSpec((tk, tn), lambda i, j, k: (k + b_off[0], j + b_off[1]))
        dims = _TN
    o_spec = pl.BlockSpec((tm, tn), lambda i, j, k: (i, j))

    def body(a_ref, b_ref, *rest):
        r_ref = rest[0] if res is not None else None
        o_ref = rest[1] if res is not None else rest[0]
        acc = rest[-1] if nk > 1 else None

        def finish(total):
            total = total * scale if scale != 1.0 else total
            if r_ref is not None:
                total = r_ref[...] + total
            o_ref[...] = total.astype(out_dtype)

        _accumulate(pl.program_id(2), nk, acc, _bdot(a_ref[...], b_ref[...], dims), finish)

    out = _call(body, comm, name, (M // tm, N // tn, nk), [a_spec, b_spec] + ([o_spec] if res is not None else []),
                [o_spec], [jax.ShapeDtypeStruct((M, N), out_dtype)], [pltpu.VMEM((tm, tn), F32)] if nk > 1 else [],
                ("parallel", "parallel", "arbitrary"), [a, b] + ([res] if res is not None else []))
    return out[0] if comm is None else out


def _ffn_up(h, wg, wu, name, comm=None):
    T = h.shape[0]
    tm = min(1024, T)

    def body(h_ref, wg_ref, wu_ref, a_ref, u_ref, s_ref):
        hb = h_ref[...]
        a = _bdot(hb, wg_ref[...], _NN)
        u = _bdot(hb, wu_ref[...], _NN)
        a_ref[...] = a
        u_ref[...] = u
        s_ref[...] = (_silu(a) * u).astype(BF16)

    w_spec = pl.BlockSpec((None, D, FSH), lambda i, j: (j, 0, 0))
    o_spec = pl.BlockSpec((None, tm, FSH), lambda i, j: (j, i, 0))
    sh = lambda dt: jax.ShapeDtypeStruct((NDEV, T, FSH), dt)
    return _call(body, comm, name, (T // tm, NDEV), [pl.BlockSpec((tm, D), lambda i, j: (i, 0)), w_spec, w_spec],
                 [o_spec] * 3, [sh(F32), sh(F32), sh(BF16)], [], ("parallel", "arbitrary"), [h, wg, wu])


def _ffn_down(s, wd, x, name, comm=None):
    T = x.shape[0]
    tm, tn = min(1024, T), 1024

    def body(s_ref, wd_ref, x_ref, o_ref, acc):
        def finish(total):
            o_ref[...] = x_ref[...] + 0.5 * total

        _accumulate(pl.program_id(2), NDEV, acc, _bdot(s_ref[...], wd_ref[...], _NN), finish)

    xo = pl.BlockSpec((tm, tn), lambda i, n, j: (i, n))
    out = _call(body, comm, name, (T // tm, D // tn, NDEV),
                [pl.BlockSpec((None, tm, FSH), lambda i, n, j: (j, i, 0)),
                 pl.BlockSpec((None, FSH, tn), lambda i, n, j: (j, 0, n)), xo],
                [xo], [jax.ShapeDtypeStruct((T, D), F32)], [pltpu.VMEM((tm, tn), F32)],
                ("parallel", "parallel", "arbitrary"), [s, wd, x])
    return out[0] if comm is None else out


def _ffn_bwd_hidden(dx, wd, a, u, name):
    T = dx.shape[0]
    tm = min(1024, T)

    def body(dx_ref, wd_ref, a_ref, u_ref, da_ref, du_ref):
        ds = 0.5 * _bdot(dx_ref[...], wd_ref[...], _NT)
        av = a_ref[...]
        sg = _sigmoid(av)
        da_ref[...] = (ds * u_ref[...] * (sg * (1.0 + av * (1.0 - sg)))).astype(BF16)
        du_ref[...] = (ds * (av * sg)).astype(BF16)

    act = pl.BlockSpec((None, tm, FSH), lambda i, j: (j, i, 0))
    sh = jax.ShapeDtypeStruct((NDEV, T, FSH), BF16)
    return pl.pallas_call(
        body, name=name, grid=(T // tm, NDEV),
        in_specs=[pl.BlockSpec((tm, D), lambda i, j: (i, 0)), pl.BlockSpec((None, FSH, D), lambda i, j: (j, 0, 0)),
                  act, act],
        out_specs=[act, act], out_shape=(sh, sh),
        compiler_params=_params(("parallel", "arbitrary")),
    )(dx, wd, a, u)


def _ffn_bwd_input(da, du, wg, wu, name, comm=None):
    T = da.shape[1]
    tm, tn = min(1024, T), 1024

    def body(da_ref, du_ref, wg_ref, wu_ref, o_ref, acc):
        part = _bdot(da_ref[...], wg_ref[...], _NT) + _bdot(du_ref[...], wu_ref[...], _NT)

        def finish(total):
            o_ref[...] = total

        _accumulate(pl.program_id(2), NDEV, acc, part, finish)

    act = pl.BlockSpec((None, tm, FSH), lambda i, n, j: (j, i, 0))
    wsp = pl.BlockSpec((None, tn, FSH), lambda i, n, j: (j, n, 0))
    out = _call(body, comm, name, (T // tm, D // tn, NDEV), [act, act, wsp, wsp],
                [pl.BlockSpec((tm, tn), lambda i, n, j: (i, n))], [jax.ShapeDtypeStruct((T, D), F32)],
                [pltpu.VMEM((tm, tn), F32)], ("parallel", "parallel", "arbitrary"), [da, du, wg, wu])
    return out[0] if comm is None else out


def _ffn_grad_up(h, da, du, name, comm=None):
    T = h.shape[0]
    tm, tk = 1024, min(1024, T)
    nk = T // tk

    def body(h_ref, da_ref, du_ref, o_ref, acc_a, acc_u):
        k = pl.program_id(2)
        hb = h_ref[...]
        for acc, ref, slot in ((acc_a, da_ref, 0), (acc_u, du_ref, 1)):
            def finish(total, slot=slot):
                o_ref[slot] = total.astype(BF16)

            _accumulate(k, nk, acc, _bdot(hb, ref[...], _TN), finish)

    act = pl.BlockSpec((None, tk, FSH), lambda j, i, t: (j, t, 0))
    out = _call(body, comm, name, (NDEV, D // tm, nk), [pl.BlockSpec((tk, tm), lambda j, i, t: (t, i)), act, act],
                [pl.BlockSpec((None, 2, tm, FSH), lambda j, i, t: (j, 0, i, 0))],
                [jax.ShapeDtypeStruct((NDEV, 2, D, FSH), BF16)],
                [pltpu.VMEM((tm, FSH), F32), pltpu.VMEM((tm, FSH), F32)], ("parallel", "parallel", "arbitrary"), [h, da, du])
    return out[0] if comm is None else out


def _ffn_grad_down(s, dx, name):
    T = dx.shape[0]
    tn, tk = 1024, min(1024, T)
    nk = T // tk

    def body(s_ref, dx_ref, o_ref, acc):
        def finish(total):
            o_ref[...] = (0.5 * total).astype(BF16)

        _accumulate(pl.program_id(2), nk, acc, _bdot(s_ref[...], dx_ref[...], _TN), finish)

    return pl.pallas_call(
        body, name=name, grid=(NDEV, D // tn, nk),
        in_specs=[pl.BlockSpec((None, tk, FSH), lambda j, n, t: (j, t, 0)), pl.BlockSpec((tk, tn), lambda j, n, t: (t, n))],
        out_specs=pl.BlockSpec((None, FSH, tn), lambda j, n, t: (j, 0, n)),
        out_shape=jax.ShapeDtypeStruct((NDEV, DFF // NDEV, D), BF16),
        scratch_shapes=[pltpu.VMEM((FSH, tn), F32)],
        compiler_params=_params(("parallel", "parallel", "arbitrary")),
    )(s, dx)


def _branch_merge(o_gla, o_rw, wb, pg, gate_b):
    T = o_gla.shape[0]
    tm, tn = min(1024, T), 512

    def body(og_ref, or_ref, w1_ref, w2_ref, p1_ref, p2_ref, b1_ref, b2_ref, yg_ref, yr_ref, m_ref):
        yg = _bdot(og_ref[...], w1_ref[...], _NN)
        yr = _bdot(or_ref[...], w2_ref[...], _NN)
        yg_ref[...] = yg
        yr_ref[...] = yr
        m_ref[...] = (_sigmoid(p1_ref[...] + b1_ref[...]) * yg + _sigmoid(p2_ref[...] + b2_ref[...]) * yr).astype(BF16)

    nj = D // tn
    act = pl.BlockSpec((tm, GLA_V), lambda i, j: (i, 0))
    out = pl.BlockSpec((tm, tn), lambda i, j: (i, j))
    return pl.pallas_call(
        body, name="branch_merge", grid=(T // tm, nj),
        in_specs=[act, act, pl.BlockSpec((GLA_V, tn), lambda i, j: (0, j)), pl.BlockSpec((RW_W, tn), lambda i, j: (1, j)),
                  out, pl.BlockSpec((tm, tn), lambda i, j: (i, nj + j)),
                  pl.BlockSpec((1, tn), lambda i, j: (0, j)), pl.BlockSpec((1, tn), lambda i, j: (0, nj + j))],
        out_specs=[out, out, out],
        out_shape=(jax.ShapeDtypeStruct((T, D), F32), jax.ShapeDtypeStruct((T, D), F32), jax.ShapeDtypeStruct((T, D), BF16)),
        compiler_params=_params(("parallel", "arbitrary")),
    )(o_gla, o_rw, wb, wb, pg, pg, gate_b, gate_b)


def _head_indicator(width, heads):
    col = lax.broadcasted_iota(jnp.int32, (width, 128), 0) // (width // heads)
    ind = (col == lax.broadcasted_iota(jnp.int32, (width, 128), 1)).astype(F32)
    return ind, ind.T


def _ffn_fwd(x, g, wg, wu, wd, tag):
    h = _rms_fwd(x, g, "rms_" + tag)
    a, u, s = _ffn_up(h, wg, wu, "ffn_up_" + tag)
    return _ffn_down(s, wd, x, "ffn_down_" + tag), (h, a, u, s)


def _ffn_fwd_gathering(x, g, wg, wu, wd_block, next_block, tag):
    h = _rms_fwd(x, g, "rms_" + tag)
    a, u, s, wd = _ffn_up(h, wg, wu, "ffn_up_" + tag, comm=("gather", [wd_block]))
    y, gathered = _ffn_down(s, wd, x, "ffn_down_" + tag, comm=("gather", [next_block]))
    return y, (h, a, u, s), wd, gathered


def _ffn_bwd(dy, x, g, wg, wu, wd, saved, tag, exchange=False):
    h, a, u, s = saved
    dwd = _ffn_grad_down(s, dy, "ffn_grad_down_" + tag)
    da, du = _ffn_bwd_hidden(dy, wd, a, u, "ffn_bwd_hidden_" + tag)
    if exchange:
        dw_up, dwd = _ffn_grad_up(h, da, du, "ffn_grad_up_" + tag, comm=("exchange", [dwd]))
        dh, dw_up = _ffn_bwd_input(da, du, wg, wu, "ffn_bwd_input_" + tag, comm=("exchange", [dw_up]))
    else:
        dw_up = _ffn_grad_up(h, da, du, "ffn_grad_up_" + tag)
        dh = _ffn_bwd_input(da, du, wg, wu, "ffn_bwd_input_" + tag)
    dx, dg = _rms_bwd(x, g, dh, dy, "rms_bwd_" + tag)
    return dx, dg, dw_up, dwd


def _local_step(x, target, w, blocks):
    T = x.shape[0]
    ind16, ind16_t = _head_indicator(RW_W, RW_HEADS)
    ind4, ind4_t = _head_indicator(GLA_V, GLA_HEADS)
    ltri = jnp.tril(jnp.ones((CHUNK, CHUNK), F32))
    gate_pars = [w["w0"], w["w_w2"], w["a0"], w["w_a2"], w["w_g2"], w["k_k"], w["k_a"], ind16, ind16_t]
    post_pars = [w["lnx_w"], w["lnx_b"], w["r_k"], ind16, ind16_t]

    x1, ffn1, wd1, g_proj = _ffn_fwd_gathering(x, w["g1"], w["wg1"], w["wu1"], blocks["wd1"], blocks["win"], "1")
    win = _align_proj(_unshard_cols(g_proj))
    h2 = _rms_fwd(x1, w["g2"], "rms_mix")
    proj = lambda n, off, name: _matmul(h2, win, "nn", T, n, D, 1024, 512, D, name, b_off=(0, off // 512))
    pg = proj(PG_W, 0, "proj_gate")
    pr = proj(PR_W, PG_W, "proj_rwkv")
    pa = proj(PA_W, PG_W + PR_W, "proj_gla")
    la = _gla_prep(pa, w["gla_w_a2"], w["gla_b_a"])
    o_raw, gla_states = _gla_core_fwd(pa, la, ltri)
    o_gla = _gla_post(o_raw, pa, w["gn"], ind4, ind4_t)
    rw, dec, k2, kk, b, g = _rw_prep(pr, w["mu"], gate_pars)
    y, rw_states, rw_sa, g_up2, g_down2 = _rw_core_fwd(rw, dec, k2, kk, b, gather=blocks["late"])
    w = {**w, **_late_weights(g_up2, g_down2)}
    o_rw = _rw_post(y, rw, k2, g, post_pars)
    y_gla, y_rw, merged = _branch_merge(o_gla, o_rw, w["wb"], pg, w["gate_b"])
    x2 = _matmul(merged, w["wo"], "nn", T, D, D, 1024, 1024, D, "out_proj", res=x1)
    x3, ffn2 = _ffn_fwd(x2, w["g3"], w["wg2"], w["wu2"], w["wd2"], "2")
    dx3, loss, d_gf = _loss_bwd(x3, target, w["gf"])

    grads = {"gf": d_gf}
    dx2, grads["g3"], grads["up2"], grads["wd2"] = _ffn_bwd(
        dx3, x2, w["g3"], w["wg2"], w["wu2"], w["wd2"], ffn2, "2")
    dm = _matmul(dx2, w["wo"], "nt", T, D, D, 1024, 1024, D, "out_proj_bwd")
    grads["wo"] = _matmul(merged, dx2, "tn", D, D, T, 1024, 1024, 1024, "out_proj_grad", out_dtype=BF16)
    dy_gla, dy_rw, dpg, grads["gate_b"] = _merge_bwd(dm, y_gla, y_rw, pg, w["gate_b"])
    do_gla = _matmul(dy_gla, w["wb"], "nt", T, GLA_V, D, 1024, 1024, D, "branch_gla_bwd")
    do_rw = _matmul(dy_rw, w["wb"], "nt", T, RW_W, D, 1024, 1024, D, "branch_rwkv_bwd", b_off=(1, 0))
    grads["wb"] = jnp.concatenate([
        _matmul(o_gla, dy_gla, "tn", GLA_V, D, T, 1024, 1024, 1024, "branch_gla_grad", out_dtype=BF16),
        _matmul(o_rw, dy_rw, "tn", RW_W, D, T, 1024, 1024, 1024, "branch_rwkv_grad", out_dtype=BF16)], axis=0)
    dy, dr2, dv2, dk2b, dg, grads["lnx_w"], grads["lnx_b"], grads["r_k"] = _rw_post_bwd(y, rw, k2, g, post_pars, do_rw)
    early = _late_grad_parts(grads)
    received = {}
    dr1, dw, dk2a, dv1, dkk, db, received["up2"], received["down2"] = _rw_core_bwd(
        rw, dec, k2, kk, b, rw_states, rw_sa, dy, exchange=early)
    drw, grads["mu"], grads["w0"], grads["w_w2"], grads["a0"], grads["w_a2"], grads["w_g2"], grads["k_k"], grads["k_a"] = (
        _rw_prep_bwd(pr, w["mu"], gate_pars, (dr1, dr2), (dv1, dv2), dw, (dk2a, dk2b), dkk, db, dg))
    dpr = _shift_bwd(drw, w["mu"])
    do_raw, dr_gla, grads["gn"] = _gla_post_bwd(o_raw, pa, w["gn"], ind4, ind4_t, do_gla)
    dq, dk, dv, dla = _gla_core_bwd(pa, la, ltri, gla_states, do_raw)
    da_down, grads["gla_w_a2"], grads["gla_b_a"] = _gla_prep_bwd(pa, w["gla_w_a2"], w["gla_b_a"], dla)
    dpa = jnp.concatenate([dq.astype(BF16), dk.astype(BF16), dv.astype(BF16), dr_gla, da_down,
                           jnp.zeros((T, PA_W - PA_USED), BF16)], axis=1)
    dp = jnp.concatenate([dpg, dpr, dpa], axis=1)
    d_win = _matmul(h2, dp, "tn", D, DIN_P, T, 1024, 1024, 1024, "proj_grad", out_dtype=BF16)
    dh2, received["win"] = _matmul(dp, win, "nt", T, D, DIN_P, 1024, 1024, 1024, "proj_bwd",
                                   comm=("exchange", [_shard_cols(_unalign_proj(d_win))]))
    dx1, grads["g2"] = _rms_bwd(x1, w["g2"], dh2, dx2, "rms_bwd_mix")
    dx, grads["g1"], received["up1"], received["wd1"] = _ffn_bwd(
        dx1, x, w["g1"], w["wg1"], w["wu1"], wd1, ffn1, "1", exchange=True)
    return loss, dx, grads, received


BIG = ("ffn1_wg", "ffn1_wu", "ffn1_wd", "w_in", "w_branch", "w_out", "ffn2_wg", "ffn2_wu", "ffn2_wd")
SMALL_SHARDED = ("gla_w_a2", "rwkv_w_w2", "rwkv_w_a2", "rwkv_w_g2")
REPLICATED = ("ffn1_norm", "mix_norm", "gla_b_a", "gla_gn_w", "rwkv_mu", "rwkv_w0", "rwkv_a0", "rwkv_k_k", "rwkv_k_a",
              "rwkv_r_k", "rwkv_lnx_w", "rwkv_lnx_b", "gate_b", "ffn2_norm", "final_norm")
WEIGHTS = ("ffn1_norm", "ffn1_wg", "ffn1_wu", "ffn1_wd", "mix_norm", "w_in", "gla_w_a2", "gla_b_a", "gla_gn_w",
           "rwkv_mu", "rwkv_w0", "rwkv_w_w2", "rwkv_a0", "rwkv_w_a2", "rwkv_w_g2", "rwkv_k_k", "rwkv_k_a", "rwkv_r_k",
           "rwkv_lnx_w", "rwkv_lnx_b", "gate_b", "w_branch", "w_out", "ffn2_norm", "ffn2_wg", "ffn2_wu", "ffn2_wd",
           "final_norm")


def _unshard_cols(g):
    return jnp.transpose(g, (1, 0, 2)).reshape(g.shape[1], NDEV * g.shape[2])


def _shard_cols(a):
    return jnp.transpose(a.reshape(a.shape[0], NDEV, a.shape[1] // NDEV), (1, 0, 2))


def _pad_rows(a, rows):
    return jnp.pad(a, ((0, rows - a.shape[0]), (0, 0)))


def _align_rw(a):
    c = 3 * RW_W
    z = jnp.zeros((a.shape[0], LORA_P - DECAY_LORA), a.dtype)
    return jnp.concatenate([a[:, :c], a[:, c:c + DECAY_LORA], z, a[:, c + DECAY_LORA:c + 2 * DECAY_LORA], z,
                            a[:, c + 2 * DECAY_LORA:]], axis=1)


def _unalign_rw(a):
    c = 3 * RW_W
    return jnp.concatenate([a[:, :c + DECAY_LORA], a[:, c + LORA_P:c + LORA_P + AAA_LORA], a[:, c + 2 * LORA_P:]], axis=1)


def _align_proj(a):
    gla = jnp.pad(a[:, :GLA_IN], ((0, 0), (0, PA_W - GLA_IN)))
    return jnp.concatenate([a[:, GLA_IN + RW_IN:], _align_rw(a[:, GLA_IN:GLA_IN + RW_IN]), gla], axis=1)


def _unalign_proj(a):
    return jnp.concatenate([a[:, PG_W + PR_W:PG_W + PR_W + GLA_IN], _unalign_rw(a[:, PG_W:PG_W + PR_W]), a[:, :PG_W]], axis=1)


def _layout_weights(gb, gs, rep):
    row = lambda n: rep[n].reshape(1, -1)
    return {
        "wg1": gb["ffn1_wg"], "wu1": gb["ffn1_wu"],
        "g1": row("ffn1_norm"), "g2": row("mix_norm"), "g3": row("ffn2_norm"), "gf": row("final_norm"),
        "gla_w_a2": _pad_rows(_unshard_cols(gs["gla_w_a2"]), LORA_P), "gla_b_a": row("gla_b_a"),
        "gn": jnp.tile(row("gla_gn_w"), (1, GLA_HEADS)),
        "mu": _align_rw(row("rwkv_mu")), "w0": row("rwkv_w0"), "a0": row("rwkv_a0"),
        "w_w2": _pad_rows(_unshard_cols(gs["rwkv_w_w2"]), LORA_P),
        "w_a2": _pad_rows(_unshard_cols(gs["rwkv_w_a2"]), LORA_P),
        "w_g2": _unshard_cols(gs["rwkv_w_g2"]),
        "k_k": row("rwkv_k_k"), "k_a": row("rwkv_k_a"), "r_k": row("rwkv_r_k"),
        "lnx_w": row("rwkv_lnx_w"), "lnx_b": row("rwkv_lnx_b"), "gate_b": row("gate_b"),
    }


LATE_ROWS = (("ffn2_wd", FSH), ("w_branch", (GLA_V + RW_W) // NDEV), ("w_out", D // NDEV))


def _late_weights(g_up, g_down):
    r1, r2 = LATE_ROWS[0][1], LATE_ROWS[0][1] + LATE_ROWS[1][1]
    return {"wg2": g_up[:, 0], "wu2": g_up[:, 1], "wd2": g_down[:, :r1],
            "wb": g_down[:, r1:r2].reshape(GLA_V + RW_W, D), "wo": g_down[:, r2:].reshape(D, D)}


def _late_grad_parts(g):
    return [g["up2"], jnp.concatenate([g["wd2"], g["wb"].reshape(NDEV, -1, D), g["wo"].reshape(NDEV, -1, D)], axis=1)]


def _layout_grads(g):
    return {
        "ffn1_norm": g["g1"], "mix_norm": g["g2"], "ffn2_norm": g["g3"], "final_norm": g["gf"],
        "gla_w_a2": g["gla_w_a2"][:GLA_LORA], "gla_b_a": g["gla_b_a"],
        "gla_gn_w": jnp.sum(g["gn"].reshape(GLA_HEADS, GLA_DV), axis=0, keepdims=True),
        "rwkv_mu": _unalign_rw(g["mu"]), "rwkv_w0": g["w0"], "rwkv_a0": g["a0"],
        "rwkv_w_w2": g["w_w2"][:DECAY_LORA], "rwkv_w_a2": g["w_a2"][:AAA_LORA], "rwkv_w_g2": g["w_g2"],
        "rwkv_k_k": g["k_k"], "rwkv_k_a": g["k_a"], "rwkv_r_k": g["r_k"],
        "rwkv_lnx_w": g["lnx_w"], "rwkv_lnx_b": g["lnx_b"], "gate_b": g["gate_b"],
    }


_MESH = pl.DeviceIdType.MESH
_ANY = pl.BlockSpec(memory_space=pl.ANY)


def _position():
    return lax.axis_index("x"), lax.axis_index("y"), lax.axis_index("c")


def _slot(p):
    return 4 * p[0] + 2 * p[1] + p[2]


def _comm_sems(n):
    if not n:
        return []
    return [pltpu.SemaphoreType.DMA((7 * n,)), pltpu.SemaphoreType.DMA((7 * n,)), pltpu.SemaphoreType.DMA((n,))]


def _gather_plan(ins, outs, send_sems, recv_sems, local_sems):
    n = len(ins)
    x, y, c = _position()
    me, sibling = (x, y, c), (x, y, 1 - c)
    chips = [(1 - x, y), (x, 1 - y), (1 - x, 1 - y)]

    def copy(a, k, block, to, src=None):
        dst = outs[a].at[_slot(block)]
        return pltpu.make_async_remote_copy(
            src_ref=dst if src is None else src, dst_ref=dst, send_sem=send_sems.at[7 * a + k],
            recv_sem=recv_sems.at[7 * a + k], device_id=to, device_id_type=_MESH)

    def local(a):
        return pltpu.make_async_copy(ins[a], outs[a].at[_slot(me)], local_sems.at[a])

    def own(a):
        return [copy(a, 0, me, sibling, src=ins[a])] + [copy(a, 1 + j, me, (*chip, c), src=ins[a]) for j, chip in enumerate(chips)]

    def start():
        for a in range(n):
            local(a).start()
            for cp in own(a):
                cp.start()

    def forward():
        for a in range(n):
            for j, chip in enumerate(chips):
                copy(a, 1 + j, (*chip, c), me).wait_recv()
                copy(a, 4 + j, (*chip, c), sibling).start()

    def finish():
        for a in range(n):
            copy(a, 0, sibling, me).wait_recv()
            for j, chip in enumerate(chips):
                copy(a, 4 + j, (*chip, 1 - c), me).wait_recv()
        for a in range(n):
            for cp in own(a) + [copy(a, 4 + j, (*chip, c), sibling) for j, chip in enumerate(chips)]:
                cp.wait_send()
            local(a).wait()

    return start, forward, finish


def _exchange_plan(ins, outs, send_sems, recv_sems, local_sems):
    n = len(ins)
    x, y, c = _position()
    me = (x, y, c)
    flip = lambda v, f: 1 - v if f else v
    peers = [(flip(x, fx), flip(y, fy), flip(c, fc))
             for fx, fy, fc in ((0, 0, 1), (1, 0, 0), (0, 1, 0), (1, 1, 0), (1, 0, 1), (0, 1, 1), (1, 1, 1))]

    def copy(a, k, src_slot, dst_slot):
        return pltpu.make_async_remote_copy(
            src_ref=ins[a].at[src_slot], dst_ref=outs[a].at[dst_slot], send_sem=send_sems.at[7 * a + k],
            recv_sem=recv_sems.at[7 * a + k], device_id=peers[k], device_id_type=_MESH)

    def local(a):
        return pltpu.make_async_copy(ins[a].at[_slot(me)], outs[a].at[_slot(me)], local_sems.at[a])

    def start():
        for a in range(n):
            local(a).start()
            for k, peer in enumerate(peers):
                copy(a, k, _slot(peer), _slot(me)).start()

    def finish():
        for a in range(n):
            for k, peer in enumerate(peers):
                copy(a, k, _slot(peer), _slot(peer)).wait_recv()
        for a in range(n):
            for k, peer in enumerate(peers):
                copy(a, k, _slot(peer), _slot(me)).wait_send()
            local(a).wait()

    return start, finish


def _all_gather(arrays, name):
    n = len(arrays)

    def body(*refs):
        start, forward, finish = _gather_plan(refs[:n], refs[n:2 * n], *refs[2 * n:])
        start()
        forward()
        finish()

    return pl.pallas_call(
        body, name=name, in_specs=[_ANY] * n, out_specs=[_ANY] * n,
        out_shape=[jax.ShapeDtypeStruct((NDEV,) + a.shape, a.dtype) for a in arrays], scratch_shapes=_comm_sems(n),
    )(*arrays)


def _adamw_math(w, g, m, v):
    m = ADAM_B1 * m + (1.0 - ADAM_B1) * g
    v = ADAM_B2 * v + (1.0 - ADAM_B2) * (g * g)
    m_hat = m / (1.0 - ADAM_B1 ** ADAM_STEP)
    v_hat = v / (1.0 - ADAM_B2 ** ADAM_STEP)
    delta = -ADAM_LR * (m_hat / (jnp.sqrt(v_hat) + ADAM_EPS) + ADAM_WD * w)
    return delta, m, v


def _sum_slots(ref):
    total = ref[0].astype(F32)
    for s in range(1, NDEV):
        total = total + ref[s].astype(F32)
    return total


def _adamw(parts, w, m, v, tr, name, stack_index=None, row_block_offset=0):
    _, R, C = w.shape

    def body(p_ref, w_ref, m_ref, v_ref, g_ref, d_ref, nm_ref, nv_ref):
        g = _sum_slots(p_ref)
        g_ref[...] = g
        d_ref[...], nm_ref[...], nv_ref[...] = _adamw_math(w_ref[...], g, m_ref[...], v_ref[...])

    if stack_index is None:
        p_spec = pl.BlockSpec((NDEV, tr, C), lambda r: (0, row_block_offset + r, 0))
    else:
        p_spec = pl.BlockSpec((NDEV, None, tr, C), lambda r: (0, stack_index, r, 0))
    blk = pl.BlockSpec((None, tr, C), lambda r: (0, r, 0))
    out = jax.ShapeDtypeStruct((1, R, C), F32)
    return pl.pallas_call(
        body, name=name, grid=(R // tr,), in_specs=[p_spec, blk, blk, blk], out_specs=[blk] * 4, out_shape=(out,) * 4,
        compiler_params=_params(("parallel",)),
    )(parts, w, m, v)


def _sum_gathered(parts):
    _, R, C = parts.shape

    def body(p_ref, o_ref):
        o_ref[...] = _sum_slots(p_ref)

    return pl.pallas_call(body, name="small_grad_sum", out_shape=jax.ShapeDtypeStruct((R, C), F32),
                          compiler_params=_params())(parts)


def _adamw_small(w, g, m, v):
    def body(w_ref, g_ref, m_ref, v_ref, d_ref, nm_ref, nv_ref):
        d_ref[...], nm_ref[...], nv_ref[...] = _adamw_math(w_ref[...], g_ref[...], m_ref[...], v_ref[...])

    out = jax.ShapeDtypeStruct(w.shape, F32)
    return pl.pallas_call(body, name="adamw_small", out_shape=(out,) * 3, compiler_params=_params())(w, g, m, v)


def _pack(pieces, rows):
    flat = jnp.concatenate([p.reshape(-1) for p in pieces])
    return jnp.pad(flat, (0, rows * 128 - flat.shape[0])).reshape(rows, 128)


def _unpack(packed, shapes):
    flat = packed.reshape(-1)
    out, off = [], 0
    for s in shapes:
        size = 1
        for d in s:
            size *= d
        out.append(flat[off:off + size].reshape(s))
        off += size
    return out


def _rows_for(shapes, extra=0):
    total = extra
    for s in shapes:
        size = 1
        for d in s:
            size *= d
        total += size
    return -(-total // 1024) * 8


def kernel(x, ffn1_norm, ffn1_wg, ffn1_wu, ffn1_wd, mix_norm, w_in, gla_w_a2, gla_b_a, gla_gn_w, rwkv_mu, rwkv_w0, rwkv_w_w2, rwkv_a0, rwkv_w_a2, rwkv_w_g2, rwkv_k_k, rwkv_k_a, rwkv_r_k, rwkv_lnx_w, rwkv_lnx_b, gate_b, w_branch, w_out, ffn2_norm, ffn2_wg, ffn2_wu, ffn2_wd, final_norm, loss_target, m_ffn1_norm, m_ffn1_wg, m_ffn1_wu, m_ffn1_wd, m_mix_norm, m_w_in, m_gla_w_a2, m_gla_b_a, m_gla_gn_w, m_rwkv_mu, m_rwkv_w0, m_rwkv_w_w2, m_rwkv_a0, m_rwkv_w_a2, m_rwkv_w_g2, m_rwkv_k_k, m_rwkv_k_a, m_rwkv_r_k, m_rwkv_lnx_w, m_rwkv_lnx_b, m_gate_b, m_w_branch, m_w_out, m_ffn2_norm, m_ffn2_wg, m_ffn2_wu, m_ffn2_wd, m_final_norm, v_ffn1_norm, v_ffn1_wg, v_ffn1_wu, v_ffn1_wd, v_mix_norm, v_w_in, v_gla_w_a2, v_gla_b_a, v_gla_gn_w, v_rwkv_mu, v_rwkv_w0, v_rwkv_w_w2, v_rwkv_a0, v_rwkv_w_a2, v_rwkv_w_g2, v_rwkv_k_k, v_rwkv_k_a, v_rwkv_r_k, v_rwkv_lnx_w, v_rwkv_lnx_b, v_gate_b, v_w_branch, v_w_out, v_ffn2_norm, v_ffn2_wg, v_ffn2_wu, v_ffn2_wd, v_final_norm):
    wts = dict(zip(WEIGHTS, (ffn1_norm, ffn1_wg, ffn1_wu, ffn1_wd, mix_norm, w_in, gla_w_a2, gla_b_a, gla_gn_w, rwkv_mu, rwkv_w0, rwkv_w_w2, rwkv_a0, rwkv_w_a2, rwkv_w_g2, rwkv_k_k, rwkv_k_a, rwkv_r_k, rwkv_lnx_w, rwkv_lnx_b, gate_b, w_branch, w_out, ffn2_norm, ffn2_wg, ffn2_wu, ffn2_wd, final_norm)))
    mom = dict(zip(WEIGHTS, (m_ffn1_norm, m_ffn1_wg, m_ffn1_wu, m_ffn1_wd, m_mix_norm, m_w_in, m_gla_w_a2, m_gla_b_a, m_gla_gn_w, m_rwkv_mu, m_rwkv_w0, m_rwkv_w_w2, m_rwkv_a0, m_rwkv_w_a2, m_rwkv_w_g2, m_rwkv_k_k, m_rwkv_k_a, m_rwkv_r_k, m_rwkv_lnx_w, m_rwkv_lnx_b, m_gate_b, m_w_branch, m_w_out, m_ffn2_norm, m_ffn2_wg, m_ffn2_wu, m_ffn2_wd, m_final_norm)))
    var = dict(zip(WEIGHTS, (v_ffn1_norm, v_ffn1_wg, v_ffn1_wu, v_ffn1_wd, v_mix_norm, v_w_in, v_gla_w_a2, v_gla_b_a, v_gla_gn_w, v_rwkv_mu, v_rwkv_w0, v_rwkv_w_w2, v_rwkv_a0, v_rwkv_w_a2, v_rwkv_w_g2, v_rwkv_k_k, v_rwkv_k_a, v_rwkv_r_k, v_rwkv_lnx_w, v_rwkv_lnx_b, v_gate_b, v_w_branch, v_w_out, v_ffn2_norm, v_ffn2_wg, v_ffn2_wu, v_ffn2_wd, v_final_norm)))
    two = lambda a: a.reshape(a.shape[-2:])

    bf = lambda n: two(wts[n]).astype(BF16)
    up1 = jnp.stack([bf("ffn1_wg"), bf("ffn1_wu")])
    lora = jnp.concatenate([jnp.pad(two(gla_w_a2), ((0, 0), (0, 128 - GLA_QK // NDEV)))] +
                           [two(wts[n]) for n in SMALL_SHARDED[1:]], axis=0)
    g_up1, g_lora = _all_gather([up1, lora], "gather_weights")
    gb = {"ffn1_wg": g_up1[:, 0], "ffn1_wu": g_up1[:, 1]}
    gs = {"gla_w_a2": g_lora[:, :GLA_LORA, :GLA_QK // NDEV]}
    row = GLA_LORA
    for n in SMALL_SHARDED[1:]:
        gs[n] = g_lora[:, row:row + wts[n].shape[1]]
        row += wts[n].shape[1]
    w = _layout_weights(gb, gs, {n: wts[n] for n in REPLICATED})
    blocks = {"wd1": bf("ffn1_wd"), "win": bf("w_in"),
              "late": [jnp.stack([bf("ffn2_wg"), bf("ffn2_wu")]), jnp.concatenate([bf(n) for n, _ in LATE_ROWS], axis=0)]}

    loss_part, grad_x, grads, parts = _local_step(x[0], loss_target[0], w, blocks)
    small = _layout_grads(grads)
    result = {}
    state = lambda n: (wts[n], mom[n], var[n])
    for group, names in (("up1", ("ffn1_wg", "ffn1_wu")), ("up2", ("ffn2_wg", "ffn2_wu"))):
        for i, n in enumerate(names):
            result[n] = _adamw(parts[group], *state(n), 256, "adamw_" + n, stack_index=i)
    result["ffn1_wd"] = _adamw(parts["wd1"], *state("ffn1_wd"), 64, "adamw_ffn1_wd")
    row = 0
    for n, rows in LATE_ROWS:
        result[n] = _adamw(parts["down2"], *state(n), 64, "adamw_" + n, row_block_offset=row // 64)
        row += rows
    result["w_in"] = _adamw(parts["win"], *state("w_in"), 256, "adamw_w_in")

    small_names = [n for n in WEIGHTS if n not in BIG]
    full_shapes = [small[n].shape for n in small_names]
    rows_full = _rows_for(full_shapes, extra=128)
    packed = _pack([small[n] for n in small_names] + [loss_part], rows_full)
    (gathered,) = _all_gather([packed], "gather_small_grads")
    total = _sum_gathered(gathered)
    *full_grads, loss_row = _unpack(total, full_shapes + [(1, 128)])
    me = _slot(_position())
    own = {}
    for n, g in zip(small_names, full_grads):
        if n in SMALL_SHARDED:
            cols = wts[n].shape[-1]
            g = lax.dynamic_slice_in_dim(g, me * cols, cols, axis=1)
        own[n] = g.reshape(wts[n].shape)
    own_shapes = [wts[n].shape for n in small_names]
    rows_own = _rows_for(own_shapes)
    pk = lambda d: _pack([d[n] for n in small_names], rows_own)
    d_s, m_s, v_s = _adamw_small(pk(wts), pk(own), pk(mom), pk(var))
    for n, d, m, v in zip(small_names, _unpack(d_s, own_shapes), _unpack(m_s, own_shapes), _unpack(v_s, own_shapes)):
        result[n] = (own[n], d, m, v)

    shaped = lambda n, k: result[n][k].reshape(wts[n].shape)
    return (loss_row[0, 0], grad_x[None],
            *[shaped(n, 0) for n in WEIGHTS], *[shaped(n, 1) for n in WEIGHTS],
            *[shaped(n, 2) for n in WEIGHTS], *[shaped(n, 3) for n in WEIGHTS])
```

```python
import functools

import jax
import jax.numpy as jnp
from jax import lax
from jax.experimental import pallas as pl
from jax.experimental.pallas import tpu as pltpu

F32 = jnp.float32
BF16 = jnp.bfloat16
HI = lax.Precision.HIGHEST

NDEV = 8
D = 2048
DFF = 5632
FSH = DFF // NDEV
CHUNK = 64
GLA_HEADS, GLA_DK, GLA_DV = 4, 128, 256
GLA_QK, GLA_V, GLA_LORA, GLA_TAU = 512, 1024, 16, 16.0
RW_HEADS, RW_HD, RW_W = 16, 64, 1024
DECAY_LORA, AAA_LORA, GATE_LORA = 96, 96, 256
GN_EPS = 64e-5
NORM_EPS = 1e-6
GLA_IN = 2 * GLA_QK + 2 * GLA_V + GLA_LORA
RW_IN = 3 * RW_W + DECAY_LORA + AAA_LORA + GATE_LORA
D_IN = GLA_IN + RW_IN + 2 * D
DIN_SH = D_IN // NDEV
PG_W = 2 * D
PR_W = 3584
PA_W = 3584
PA_USED = 2 * GLA_QK + 2 * GLA_V + 128
DIN_P = PG_W + PR_W + PA_W
LORA_P = 128

ADAM_LR, ADAM_B1, ADAM_B2, ADAM_EPS, ADAM_WD, ADAM_STEP = 0.001, 0.9, 0.999, 1e-08, 0.01, 10

VMEM_LIMIT = 56 * 1024 * 1024
RW_TB = 64
RW_G = 16
RW_NP = 4


def _params(sem=None, vmem=VMEM_LIMIT):
    return pltpu.CompilerParams(dimension_semantics=sem, vmem_limit_bytes=vmem)


def _pair_mask():
    return lax.broadcasted_iota(jnp.int32, (RW_HD, 2 * RW_HD), 1) < RW_HD


def _pair_rowsum(p, mask):
    tot = jnp.sum(p, axis=1, keepdims=True)
    first = jnp.sum(jnp.where(mask, p, 0.0), axis=1, keepdims=True)
    return first, tot - first


def _split_transposed(x_ref, q, dst_ref, base):
    xt = x_ref[:, 128 * q:128 * (q + 1)].T
    for g in range(RW_TB // RW_G):
        dst_ref[base + g, :, 0:RW_G] = xt[:, g * RW_G:(g + 1) * RW_G]


def _pair_column(tile_ref, idx, i, mask):
    return jnp.where(mask, tile_ref[idx, 0:RW_HD, i:i + 1], tile_ref[idx, RW_HD:, i:i + 1])


def _rw_core_fwd(rw, w, k2, kk, b, gather=()):
    T = rw.shape[0]
    nb = T // RW_TB
    ng = RW_TB // RW_G
    NP = RW_NP
    nc = len(gather)
    npair = RW_HEADS // 2 // NP

    def body(r_ref, v_ref, w_ref, k_ref, kk_ref, b_ref, *rest):
        g_in, (y_ref, st_ref, sa_ref), g_out = rest[:nc], rest[nc:nc + 3], rest[nc + 3:2 * nc + 3]
        s_scr, vt_scr, yt_scr, rows_scr = rest[2 * nc + 3:2 * nc + 7]
        pair, blk_i = pl.program_id(0), pl.program_id(1)
        if nc:
            start, forward, finish = _gather_plan(g_in, g_out, *rest[2 * nc + 7:])
            pl.when((pair == 0) & (blk_i == 0))(start)
            pl.when((pair == 0) & (blk_i == nb // 2))(forward)

        @pl.when(pl.program_id(1) == 0)
        def _():
            s_scr[...] = jnp.zeros_like(s_scr)
            yt_scr[...] = jnp.zeros_like(yt_scr)

        mask = _pair_mask()
        for q in range(NP):
            _split_transposed(v_ref, q, vt_scr, q * ng)
        R_, W_, K_, KK_, B_ = range(5)
        for a, ref in enumerate((r_ref, w_ref, k_ref, kk_ref, b_ref)):
            for q in range(NP):
                rows_scr[a * NP + q] = ref[:, 128 * q:128 * (q + 1)]

        def group(g, states):
            states = list(states)
            for i in range(RW_G):
                t = g * RW_G + i
                row = lambda a, q: rows_scr[a * NP + q, pl.ds(t, 1), :]
                sums = [_pair_rowsum(states[q] * row(KK_, q), mask) for q in range(NP)]
                for q in range(NP):
                    sa = jnp.where(mask, *sums[q])
                    sa_ref[q, t] = sa
                    states[q] = (states[q] * row(W_, q) - sa * row(B_, q)
                                 + _pair_column(vt_scr, q * ng + g, i, mask) * row(K_, q))
                    st_ref[q, t] = states[q]
                outs = [_pair_rowsum(states[q] * row(R_, q), mask) for q in range(NP)]
                for q in range(NP):
                    yt_scr[q * ng + g, 0:RW_HD, i:i + 1] = outs[q][0]
                    yt_scr[q * ng + g, RW_HD:, i:i + 1] = outs[q][1]
            return tuple(states)

        states = lax.fori_loop(0, ng, group, tuple(s_scr[q] for q in range(NP)))
        for q in range(NP):
            s_scr[q] = states[q]
            for g in range(ng):
                y_ref[g * RW_G:(g + 1) * RW_G, 128 * q:128 * (q + 1)] = yt_scr[q * ng + g].T[0:RW_G, :]
        if nc:
            pl.when((pair == npair - 1) & (blk_i == nb - 1))(finish)

    blk = lambda cb: pl.BlockSpec((RW_TB, 128 * NP), lambda p, i, cb=cb: (i, cb + p))
    tiles = pltpu.VMEM((NP * ng, 128, 128), F32)
    return pl.pallas_call(
        body, name="rw_core_fwd", grid=(npair, nb),
        in_specs=[blk(0), blk(2 * RW_W // (128 * NP)), blk(0), blk(0), blk(0), blk(0)] + [_ANY] * nc,
        out_specs=[blk(0)] + [pl.BlockSpec((NP, RW_TB, RW_HD, 128), lambda p, i: (p, i, 0, 0))] * 2 + [_ANY] * nc,
        out_shape=[jax.ShapeDtypeStruct((T, RW_W), F32)] + [jax.ShapeDtypeStruct((RW_HEADS // 2, T, RW_HD, 128), F32)] * 2
        + [jax.ShapeDtypeStruct((NDEV,) + a.shape, a.dtype) for a in gather],
        scratch_shapes=[pltpu.VMEM((NP, RW_HD, 128), F32), tiles, tiles, pltpu.VMEM((5 * NP, RW_TB, 128), F32)]
        + _comm_sems(nc),
        compiler_params=_params(("arbitrary", "arbitrary")),
    )(rw, rw, w, k2, kk, b, *gather)


def _rw_core_bwd(rw, w, k2, kk, b, states, sa_tiles, dy, exchange=()):
    T = rw.shape[0]
    nb = T // RW_TB
    ng = RW_TB // RW_G
    NP = RW_NP
    nc = len(exchange)
    npair = RW_HEADS // 2 // NP

    def body(r_ref, v_ref, w_ref, k_ref, kk_ref, b_ref, dy_ref, st_ref, sp_ref, sa_ref, *rest):
        e_in, e_out = rest[:nc], rest[nc + 6:2 * nc + 6]
        dr_ref, dw_ref, dk_ref, dv_ref, dkk_ref, db_ref = rest[nc:nc + 6]
        ds_scr, vt_scr, dyt_scr, dvt_scr, rows_scr, out_scr = rest[2 * nc + 6:2 * nc + 12]
        step = pl.program_id(1)
        if nc:
            start, finish = _exchange_plan(e_in, e_out, *rest[2 * nc + 12:])
            pl.when((pl.program_id(0) == 0) & (step == 0))(start)

        @pl.when(step == 0)
        def _():
            ds_scr[...] = jnp.zeros_like(ds_scr)
            dvt_scr[...] = jnp.zeros_like(dvt_scr)

        mask = _pair_mask()
        for q in range(NP):
            _split_transposed(v_ref, q, vt_scr, q * ng)
            _split_transposed(dy_ref, q, dyt_scr, q * ng)
        R_, W_, K_, KK_, B_ = range(5)
        for a, ref in enumerate((r_ref, w_ref, k_ref, kk_ref, b_ref)):
            for q in range(NP):
                rows_scr[a * NP + q] = ref[:, 128 * q:128 * (q + 1)]

        def group(gg, grads):
            g = ng - 1 - gg
            grads = list(grads)
            pairs = range(NP)
            for i in reversed(range(RW_G)):
                t = g * RW_G + i
                row = lambda a, q: rows_scr[a * NP + q, pl.ds(t, 1), :]

                def put(a, q, value):
                    out_scr[a * NP + q, pl.ds(t, 1), :] = value

                s_old = [st_ref[q, jnp.maximum(t - 1, 0)] for q in pairs]
                if i == 0:
                    s_old = [jnp.where(g == 0, jnp.where(step == nb - 1, 0.0, sp_ref[q, 0]), s_old[q]) for q in pairs]
                dycol = [_pair_column(dyt_scr, q * ng + g, i, mask) for q in pairs]
                dS = [grads[q] + dycol[q] * row(R_, q) for q in pairs]
                m = [_pair_rowsum(dS[q] * row(B_, q), mask) for q in pairs]
                dv = [_pair_rowsum(dS[q] * row(K_, q), mask) for q in pairs]
                for q in pairs:
                    put(R_, q, jnp.sum(st_ref[q, t] * dycol[q], axis=0, keepdims=True))
                    put(W_, q, jnp.sum(dS[q] * s_old[q], axis=0, keepdims=True))
                    put(K_, q, jnp.sum(dS[q] * _pair_column(vt_scr, q * ng + g, i, mask), axis=0, keepdims=True))
                for q in pairs:
                    dsa = -jnp.where(mask, *m[q])
                    grads[q] = dS[q] * row(W_, q) + dsa * row(KK_, q)
                    put(KK_, q, jnp.sum(s_old[q] * dsa, axis=0, keepdims=True))
                    put(B_, q, -jnp.sum(dS[q] * sa_ref[q, t], axis=0, keepdims=True))
                    dvt_scr[q * ng + g, 0:RW_HD, i:i + 1] = dv[q][0]
                    dvt_scr[q * ng + g, RW_HD:, i:i + 1] = dv[q][1]
            return tuple(grads)

        grads = lax.fori_loop(0, ng, group, tuple(ds_scr[q] for q in range(NP)))
        for q in range(NP):
            ds_scr[q] = grads[q]
            for a, ref in enumerate((dr_ref, dw_ref, dk_ref, dkk_ref, db_ref)):
                ref[:, 128 * q:128 * (q + 1)] = out_scr[a * NP + q]
            for g in range(ng):
                dv_ref[g * RW_G:(g + 1) * RW_G, 128 * q:128 * (q + 1)] = dvt_scr[q * ng + g].T[0:RW_G, :]
        if nc:
            pl.when((pl.program_id(0) == npair - 1) & (step == nb - 1))(finish)

    blk = lambda cb: pl.BlockSpec((RW_TB, 128 * NP), lambda p, i, cb=cb: (nb - 1 - i, cb + p))
    st_spec = pl.BlockSpec((NP, RW_TB, RW_HD, 128), lambda p, i: (p, nb - 1 - i, 0, 0))
    sp_spec = pl.BlockSpec((NP, 1, RW_HD, 128), lambda p, i: (p, jnp.maximum((nb - 1 - i) * RW_TB - 1, 0), 0, 0))
    out = jax.ShapeDtypeStruct((T, RW_W), F32)
    tiles = pltpu.VMEM((NP * ng, 128, 128), F32)
    return pl.pallas_call(
        body, name="rw_core_bwd", grid=(npair, nb),
        in_specs=[blk(0), blk(2 * RW_W // (128 * NP)), blk(0), blk(0), blk(0), blk(0), blk(0), st_spec, sp_spec, st_spec]
        + [_ANY] * nc,
        out_specs=[blk(0)] * 6 + [_ANY] * nc,
        out_shape=[out] * 6 + [jax.ShapeDtypeStruct(a.shape, a.dtype) for a in exchange],
        scratch_shapes=[pltpu.VMEM((NP, RW_HD, 128), F32), tiles, tiles, tiles,
                        pltpu.VMEM((5 * NP, RW_TB, 128), F32), pltpu.VMEM((5 * NP, RW_TB, 128), F32)] + _comm_sems(nc),
        compiler_params=_params(("arbitrary", "arbitrary")),
    )(rw, rw, w, k2, kk, b, dy, states, states, sa_tiles, *exchange)


GLA_CB = 8


def _gla_chunk(s_t, q, k, v, la, ltri):
    cum = jnp.dot(ltri, la, precision=HI, preferred_element_type=F32)
    total = jnp.sum(la, axis=0, keepdims=True)
    kdec = k * jnp.exp(total - cum)
    u_t = _bdot(v, kdec, _TN)
    s_t = jnp.exp(total) * s_t + u_t
    o = _bdot(q * (GLA_DK ** -0.5), s_t, _NT)
    return s_t, o


def _gla_core_fwd(pa, la, ltri):
    T = pa.shape[0]
    cb = min(GLA_CB, T // CHUNK)
    rows = cb * CHUNK
    nsteps = T // rows

    def body(q_ref, k_ref, v_ref, la_ref, ltri_ref, o_ref, st_ref, s_scr):
        @pl.when(pl.program_id(1) == 0)
        def _():
            s_scr[...] = jnp.zeros_like(s_scr)

        def chunk(c, s_t):
            sl = pl.ds(pl.multiple_of(c * CHUNK, CHUNK), CHUNK)
            s_t, o = _gla_chunk(s_t, q_ref[sl, :], k_ref[sl, :], v_ref[sl, :], la_ref[sl, :], ltri_ref[...])
            o_ref[sl, :] = o
            st_ref[0, c] = s_t
            return s_t

        s_scr[...] = lax.fori_loop(0, cb, chunk, s_scr[...])

    qk = lambda off: pl.BlockSpec((rows, GLA_DK), lambda h, i, off=off: (i, off + h))
    vspec = pl.BlockSpec((rows, GLA_DV), lambda h, i: (i, 2 * GLA_QK // GLA_DV + h))
    return pl.pallas_call(
        body, name="gla_core_fwd", grid=(GLA_HEADS, nsteps),
        in_specs=[qk(0), qk(GLA_HEADS), vspec, qk(0), pl.BlockSpec((CHUNK, CHUNK), lambda h, i: (0, 0))],
        out_specs=[pl.BlockSpec((rows, GLA_DV), lambda h, i: (i, h)),
                   pl.BlockSpec((1, cb, GLA_DV, GLA_DK), lambda h, i: (h, i, 0, 0))],
        out_shape=(jax.ShapeDtypeStruct((T, GLA_V), F32),
                   jax.ShapeDtypeStruct((GLA_HEADS, T // CHUNK, GLA_DV, GLA_DK), F32)),
        scratch_shapes=[pltpu.VMEM((GLA_DV, GLA_DK), F32)],
        compiler_params=_params(("arbitrary", "arbitrary")),
    )(pa, pa, pa, la, ltri)


def _gla_core_bwd(pa, la, ltri, states, do):
    T = pa.shape[0]
    cb = min(GLA_CB, T // CHUNK)
    rows = cb * CHUNK
    nsteps = T // rows

    def body(q_ref, k_ref, v_ref, la_ref, ltri_ref, st_ref, sp_ref, do_ref,
             dq_ref, dk_ref, dv_ref, dla_ref, ds_scr):
        step = pl.program_id(1)

        @pl.when(step == 0)
        def _():
            ds_scr[...] = jnp.zeros_like(ds_scr)

        s_before = jnp.where(step == nsteps - 1, 0.0, sp_ref[0, 0])

        def chunk(cc, ds_t):
            c = cb - 1 - cc
            sl = pl.ds(pl.multiple_of(c * CHUNK, CHUNK), CHUNK)
            s_prev = jnp.where(c == 0, s_before, st_ref[0, jnp.maximum(c - 1, 0)])
            _, vjp = jax.vjp(functools.partial(_gla_chunk, ltri=ltri_ref[...]),
                             s_prev, q_ref[sl, :], k_ref[sl, :], v_ref[sl, :], la_ref[sl, :])
            ds_prev, dq, dk, dv, dla = vjp((ds_t, do_ref[sl, :]))
            dq_ref[sl, :] = dq
            dk_ref[sl, :] = dk
            dv_ref[sl, :] = dv
            dla_ref[sl, :] = dla
            return ds_prev

        ds_scr[...] = lax.fori_loop(0, cb, chunk, ds_scr[...])

    r = lambda i: nsteps - 1 - i
    qk = lambda off: pl.BlockSpec((rows, GLA_DK), lambda h, i, off=off: (r(i), off + h))
    vspec = pl.BlockSpec((rows, GLA_DV), lambda h, i: (r(i), 2 * GLA_QK // GLA_DV + h))
    o128 = pl.BlockSpec((rows, GLA_DK), lambda h, i: (r(i), h))
    o256 = pl.BlockSpec((rows, GLA_DV), lambda h, i: (r(i), h))
    return pl.pallas_call(
        body, name="gla_core_bwd", grid=(GLA_HEADS, nsteps),
        in_specs=[qk(0), qk(GLA_HEADS), vspec, qk(0), pl.BlockSpec((CHUNK, CHUNK), lambda h, i: (0, 0)),
                  pl.BlockSpec((1, cb, GLA_DV, GLA_DK), lambda h, i: (h, r(i), 0, 0)),
                  pl.BlockSpec((1, 1, GLA_DV, GLA_DK), lambda h, i: (h, jnp.maximum(r(i) * cb - 1, 0), 0, 0)),
                  o256],
        out_specs=[o128, o128, o256, o128],
        out_shape=(jax.ShapeDtypeStruct((T, GLA_QK), F32), jax.ShapeDtypeStruct((T, GLA_QK), F32),
                   jax.ShapeDtypeStruct((T, GLA_V), F32), jax.ShapeDtypeStruct((T, GLA_QK), F32)),
        scratch_shapes=[pltpu.VMEM((GLA_DV, GLA_DK), F32)],
        compiler_params=_params(("arbitrary", "arbitrary")),
    )(pa, pa, pa, la, ltri, states, states, do)


def _rowwise(fn, name, T, tm, rows, pars, row_outs, acc_outs):
    nr, npar, nro = len(rows), len(pars), len(row_outs)
    tm = min(tm, T)
    nsteps = T // tm

    def body(*refs):
        i = pl.program_id(0)
        ins = [r[...] for r in refs[:nr + npar]]
        outs, accs = fn(i, *ins)
        for r, o in zip(refs[nr + npar:nr + npar + nro], outs):
            r[...] = o.astype(r.dtype)
        for r, a in zip(refs[nr + npar + nro:], accs):
            @pl.when(i == 0)
            def _(r=r, a=a):
                r[...] = a

            @pl.when(i > 0)
            def _(r=r, a=a):
                r[...] += a

    def rspec(width, cb, kind):
        if kind == "cur":
            return pl.BlockSpec((tm, width), lambda i: (i, cb))
        if kind == "prev":
            return pl.BlockSpec((8, width), lambda i: (jnp.maximum(i * (tm // 8) - 1, 0), cb))
        return pl.BlockSpec((8, width), lambda i: (jnp.minimum((i + 1) * (tm // 8), T // 8 - 1), cb))

    in_specs = [rspec(w, cb, kind) for (_, w, cb, kind) in rows]
    in_specs += [pl.BlockSpec(p.shape, lambda i, nd=p.ndim: (0,) * nd) for p in pars]
    out_specs = [pl.BlockSpec((tm, w), lambda i: (i, 0)) for (w, _) in row_outs]
    out_specs += [pl.BlockSpec(s, lambda i, nd=len(s): (0,) * nd) for s in acc_outs]
    out_shape = [jax.ShapeDtypeStruct((T, w), dt) for (w, dt) in row_outs]
    out_shape += [jax.ShapeDtypeStruct(s, F32) for s in acc_outs]
    res = pl.pallas_call(
        body, name=name, grid=(nsteps,), in_specs=in_specs, out_specs=out_specs, out_shape=out_shape,
        compiler_params=_params(("arbitrary",)),
    )(*[r[0] for r in rows], *pars)
    return res


def _cur(a, width=None, cb=0):
    return (a, a.shape[1] if width is None else width, cb, "cur")


def _sigmoid(x):
    return 1.0 / (1.0 + jnp.exp(-x))


def _silu(x):
    return x * _sigmoid(x)


def _softplus(x):
    return jnp.maximum(x, 0.0) + jnp.log(1.0 + jnp.exp(-jnp.abs(x)))


def _rms(x, g):
    return x * lax.rsqrt(jnp.mean(x * x, axis=-1, keepdims=True) + NORM_EPS) * g


def _dot_hi(a, b):
    return jnp.dot(a, b, precision=lax.Precision.HIGH, preferred_element_type=F32)


def _rms_fwd(x, g, name):
    T = x.shape[0]
    fn = lambda i, xb, gb: ((_rms(xb, gb),), ())
    return _rowwise(fn, name, T, 256, [_cur(x)], [g], [(D, BF16)], [])[0]


def _rms_bwd(x, g, dh, dres, name):
    T = x.shape[0]

    def fn(i, xb, dhb, drb, gb):
        _, vjp = jax.vjp(_rms, xb, gb)
        dx, dg = vjp(dhb)
        return (drb + dx,), (dg,)

    return _rowwise(fn, name, T, 256, [_cur(x), _cur(dh), _cur(dres)], [g], [(D, F32)], [(1, D)])


def _loss_bwd(x, target, g):
    T = x.shape[0]

    def loss(xb, gb, tb):
        err = _rms(xb, gb) - tb
        return 0.5 * jnp.sum(jnp.mean(err * err, axis=-1, keepdims=True))

    def fn(i, xb, tb, gb):
        val, (dx, dg) = jax.value_and_grad(loss, argnums=(0, 1))(xb, gb, tb)
        return (dx,), (jnp.full((1, 128), val, F32), dg)

    return _rowwise(fn, "loss_bwd", T, 256, [_cur(x), _cur(target)], [g], [(D, F32)], [(1, 128), (1, D)])


def _gla_la(a_down, w_a2, b_a):
    return -_softplus(-(_bdot(a_down, w_a2, _NN) + b_a)) * (1.0 / GLA_TAU)


def _gla_prep(pa, w_a2, b_a):
    T = pa.shape[0]
    fn = lambda i, ab, wb, bb: ((_gla_la(ab, wb, bb),), ())
    return _rowwise(fn, "gla_prep", T, 512, [_cur(pa, LORA_P, (2 * GLA_QK + 2 * GLA_V) // LORA_P)], [w_a2, b_a],
                    [(GLA_QK, F32)], [])[0]


def _gla_prep_bwd(pa, w_a2, b_a, dla):
    T = pa.shape[0]

    def fn(i, ab, dlab, wb, bb):
        _, vjp = jax.vjp(_gla_la, ab, wb, bb)
        da, dw, db = vjp(dlab)
        return (da,), (dw, db)

    return _rowwise(fn, "gla_prep_bwd", T, 512, [_cur(pa, LORA_P, (2 * GLA_QK + 2 * GLA_V) // LORA_P), _cur(dla)],
                    [w_a2, b_a], [(LORA_P, BF16)], [(LORA_P, GLA_QK), (1, GLA_QK)])


def _gla_out(o, r, gn, ind, ind_t):
    ms = _dot_hi(_dot_hi(o * o, ind) * (1.0 / GLA_DV), ind_t)
    return o * lax.rsqrt(ms + NORM_EPS) * gn * _silu(r)


def _gla_post(o_raw, pa, gn, ind, ind_t):
    T = pa.shape[0]
    fn = lambda i, ob, rb, gb, a, b: ((_gla_out(ob, rb, gb, a, b),), ())
    return _rowwise(fn, "gla_post", T, 256, [_cur(o_raw), _cur(pa, GLA_V, 2)], [gn, ind, ind_t], [(GLA_V, BF16)], [])[0]


def _gla_post_bwd(o_raw, pa, gn, ind, ind_t, do):
    T = pa.shape[0]

    def fn(i, ob, rb, dob, gb, a, b):
        _, vjp = jax.vjp(lambda o, r, g: _gla_out(o, r, g, a, b), ob, rb, gb)
        d_o, d_r, d_g = vjp(dob)
        return (d_o, d_r), (d_g,)

    return _rowwise(fn, "gla_post_bwd", T, 256, [_cur(o_raw), _cur(pa, GLA_V, 2), _cur(do)], [gn, ind, ind_t],
                    [(GLA_V, F32), (GLA_V, BF16)], [(1, GLA_V)])


def _shift_rows(cur, prev8, i):
    first = jnp.where(i == 0, 0.0, prev8[7:8, :])
    rolled = pltpu.roll(cur, 1, 0)
    return jnp.where(lax.broadcasted_iota(jnp.int32, cur.shape, 0) == 0, first, rolled)


def _rw_gates(rw, w0, w_w2, a0, w_a2, w_g2, k_k, k_a, ind, ind_t):
    rk = rw[:, RW_W:2 * RW_W]
    wd = rw[:, 3 * RW_W:3 * RW_W + LORA_P]
    ad = rw[:, 3 * RW_W + LORA_P:3 * RW_W + 2 * LORA_P]
    gd = rw[:, 3 * RW_W + 2 * LORA_P:]
    w_raw = w0 + _bdot(jnp.tanh(wd), w_w2, _NN)
    w = jnp.exp(-jnp.exp(-_softplus(-w_raw) - 0.5))
    a = _sigmoid(a0 + _bdot(ad, w_a2, _NN))
    g = _bdot(_sigmoid(gd), w_g2, _NN)
    kk = rk * k_k
    kk = kk * _dot_hi(lax.rsqrt(jnp.maximum(_dot_hi(kk * kk, ind), 1e-24)), ind_t)
    k2 = rk * (1.0 + (a - 1.0) * k_a)
    return w, k2, kk, kk * a, g


def _rw_prep(pr, mu, gate_pars):
    T = pr.shape[0]

    def fn(i, cur, prev8, mub, *gp):
        rw = cur + mub * (_shift_rows(cur, prev8, i) - cur)
        return (rw,) + _rw_gates(rw, *gp), ()

    return _rowwise(fn, "rw_prep", T, 256, [_cur(pr), (pr, PR_W, 0, "prev")], [mu, *gate_pars],
                    [(PR_W, F32)] + [(RW_W, F32)] * 5, [])


def _rw_prep_bwd(pr, mu, gate_pars, d_r, d_v, d_w, d_k2, d_kk, d_b, d_g):
    T = pr.shape[0]
    rows = [_cur(pr), (pr, PR_W, 0, "prev")] + [_cur(x) for x in (*d_r, *d_v, d_w, *d_k2, d_kk, d_b, d_g)]
    acc = [(1, PR_W)] + [tuple(p.shape) for p in gate_pars[:-2]]

    def fn(i, cur, prev8, dr1, dr2, dv1, dv2, dw, dk1, dk2, dkk, db, dg, mub, *gp):
        sh = _shift_rows(cur, prev8, i)
        rw = cur + mub * (sh - cur)
        _, vjp = jax.vjp(lambda x, *p: _rw_gates(x, *p, gp[-2], gp[-1]), rw, *gp[:-2])
        grads = vjp((dw, dk1 + dk2, dkk, db, dg))
        zeros = jnp.zeros((cur.shape[0], PR_W - 3 * RW_W), F32)
        drw = grads[0] + jnp.concatenate([dr1 + dr2, jnp.zeros_like(dr1), dv1 + dv2, zeros], axis=1)
        dmu = jnp.sum(drw * (sh - cur), axis=0, keepdims=True)
        return (drw,), (dmu, *grads[1:])

    return _rowwise(fn, "rw_prep_bwd", T, 128, rows, [mu, *gate_pars], [(PR_W, F32)], acc)


def _shift_bwd(drw, mu):
    T = drw.shape[0]
    tm = min(256, T)

    def fn(i, cur, next8, mub):
        last = jnp.where(i == T // tm - 1, 0.0, next8[0:1, :])
        rolled = pltpu.roll(cur, cur.shape[0] - 1, 0)
        nxt = jnp.where(lax.broadcasted_iota(jnp.int32, cur.shape, 0) == cur.shape[0] - 1, last, rolled)
        return ((1.0 - mub) * cur + mub * nxt,), ()

    return _rowwise(fn, "shift_bwd", T, tm, [_cur(drw), (drw, PR_W, 0, "next")], [mu], [(PR_W, BF16)], [])[0]


def _rw_out(y, r, v, k2, g, lnx_w, lnx_b, r_k, ind, ind_t):
    mean = _dot_hi(_dot_hi(y, ind) * (1.0 / RW_HD), ind_t)
    yc = y - mean
    var = _dot_hi(_dot_hi(yc * yc, ind) * (1.0 / RW_HD), ind_t)
    yn = yc * lax.rsqrt(var + GN_EPS) * lnx_w + lnx_b
    bonus = _dot_hi(_dot_hi(r * k2 * r_k, ind), ind_t) * v
    return (yn + bonus) * g


def _rw_post(y, rw, k2, g, pars):
    T = y.shape[0]
    fn = lambda i, *a: ((_rw_out(*a),), ())
    return _rowwise(fn, "rw_post", T, 256, [_cur(y), _cur(rw, RW_W, 0), _cur(rw, RW_W, 2), _cur(k2), _cur(g)], pars,
                    [(RW_W, BF16)], [])[0]


def _rw_post_bwd(y, rw, k2, g, pars, do):
    T = y.shape[0]

    def fn(i, yb, rb, vb, kb, gb, dob, lw, lb, rk, ind, ind_t):
        _, vjp = jax.vjp(lambda *a: _rw_out(*a, ind, ind_t), yb, rb, vb, kb, gb, lw, lb, rk)
        gr = vjp(dob)
        return gr[:5], gr[5:]

    return _rowwise(fn, "rw_post_bwd", T, 256,
                    [_cur(y), _cur(rw, RW_W, 0), _cur(rw, RW_W, 2), _cur(k2), _cur(g), _cur(do)], pars,
                    [(RW_W, F32)] * 5, [(1, RW_W)] * 3)


def _merge_bwd(dm, y_gla, y_rw, pg, gate_b):
    T = dm.shape[0]

    def fn(i, dmb, ya, yr, p1, p2, gb):
        g1 = _sigmoid(p1 + gb[:, :D])
        g2 = _sigmoid(p2 + gb[:, D:])
        dp1 = dmb * ya * g1 * (1.0 - g1)
        dp2 = dmb * yr * g2 * (1.0 - g2)
        dp = jnp.concatenate([dp1, dp2], axis=1)
        return (dmb * g1, dmb * g2, dp), (jnp.sum(dp, axis=0, keepdims=True),)

    return _rowwise(fn, "merge_bwd", T, 256, [_cur(dm), _cur(y_gla), _cur(y_rw), _cur(pg, D, 0), _cur(pg, D, 1)],
                    [gate_b], [(D, BF16), (D, BF16), (PG_W, BF16)], [(1, PG_W)])


_NN = (((1,), (0,)), ((), ()))
_NT = (((1,), (1,)), ((), ()))
_TN = (((0,), (0,)), ((), ()))


def _bdot(a, b, dims):
    return lax.dot_general(a.astype(BF16), b.astype(BF16), dims, preferred_element_type=F32)


def _accumulate(k, nk, acc, part, finish):
    if nk == 1:
        finish(part)
        return

    @pl.when(k == 0)
    def _():
        acc[...] = part

    @pl.when(k > 0)
    def _():
        acc[...] += part

    @pl.when(k == nk - 1)
    def _():
        finish(acc[...])


def _call(body, comm, name, grid, in_specs, out_specs, out_shape, scratch_shapes, sem, operands):
    if comm is None:
        return pl.pallas_call(body, name=name, grid=grid, in_specs=in_specs, out_specs=out_specs, out_shape=out_shape,
                              scratch_shapes=scratch_shapes, compiler_params=_params(sem))(*operands)
    kind, arrays = comm
    nc, n_in, n_out, n_scr = len(arrays), len(in_specs), len(out_shape), len(scratch_shapes)
    total = 1
    for n in grid:
        total *= n

    def with_comm(*refs):
        own = refs[:n_in] + refs[n_in + nc:n_in + nc + n_out] + refs[n_in + 2 * nc + n_out:n_in + 2 * nc + n_out + n_scr]
        c_in, c_out, sems = refs[n_in:n_in + nc], refs[n_in + nc + n_out:n_in + 2 * nc + n_out], refs[-3:]
        step = 0
        for axis, n in enumerate(grid):
            step = step * n + pl.program_id(axis)
        if kind == "gather":
            start, forward, finish = _gather_plan(c_in, c_out, *sems)
            pl.when(step == 0)(start)
            pl.when(step == total // 2)(forward)
        else:
            start, finish = _exchange_plan(c_in, c_out, *sems)
            pl.when(step == 0)(start)
        body(*own)
        pl.when(step == total - 1)(finish)

    lead = (NDEV,) if kind == "gather" else ()
    return pl.pallas_call(
        with_comm, name=name, grid=grid, in_specs=list(in_specs) + [_ANY] * nc, out_specs=list(out_specs) + [_ANY] * nc,
        out_shape=list(out_shape) + [jax.ShapeDtypeStruct(lead + a.shape, a.dtype) for a in arrays],
        scratch_shapes=list(scratch_shapes) + _comm_sems(nc), compiler_params=_params(("arbitrary",) * len(grid)),
    )(*operands, *arrays)


def _matmul(a, b, mode, M, N, K, tm, tn, tk, name, a_off=(0, 0), b_off=(0, 0), res=None, scale=1.0, out_dtype=F32,
            comm=None):
    tm, tn, tk = min(tm, M), min(tn, N), min(tk, K)
    nk = K // tk
    if mode == "nn":
        a_spec = pl.BlockSpec((tm, tk), lambda i, j, k: (i + a_off[0], k + a_off[1]))
        b_spec = pl.BlockSpec((tk, tn), lambda i, j, k: (k + b_off[0], j + b_off[1]))
        dims = _NN
    elif mode == "nt":
        a_spec = pl.BlockSpec((tm, tk), lambda i, j, k: (i + a_off[0], k + a_off[1]))
        b_spec = pl.BlockSpec((tn, tk), lambda i, j, k: (j + b_off[0], k + b_off[1]))
        dims = _NT
    else:
        a_spec = pl.BlockSpec((tk, tm), lambda i, j, k: (k + a_off[0], i + a_off[1]))
        b_spec = pl.BlockSpec((tk, tn), lambda i, j, k: (k + b_off[0], j + b_off[1]))
        dims = _TN
    o_spec = pl.BlockSpec((tm, tn), lambda i, j, k: (i, j))

    def body(a_ref, b_ref, *rest):
        r_ref = rest[0] if res is not None else None
        o_ref = rest[1] if res is not None else rest[0]
        acc = rest[-1] if nk > 1 else None

        def finish(total):
            total = total * scale if scale != 1.0 else total
            if r_ref is not None:
                total = r_ref[...] + total
            o_ref[...] = total.astype(out_dtype)

        _accumulate(pl.program_id(2), nk, acc, _bdot(a_ref[...], b_ref[...], dims), finish)

    out = _call(body, comm, name, (M // tm, N // tn, nk), [a_spec, b_spec] + ([o_spec] if res is not None else []),
                [o_spec], [jax.ShapeDtypeStruct((M, N), out_dtype)], [pltpu.VMEM((tm, tn), F32)] if nk > 1 else [],
                ("parallel", "parallel", "arbitrary"), [a, b] + ([res] if res is not None else []))
    return out[0] if comm is None else out


def _ffn_up(h, wg, wu, name, comm=None):
    T = h.shape[0]
    tm = min(1024, T)

    def body(h_ref, wg_ref, wu_ref, a_ref, u_ref, s_ref):
        hb = h_ref[...]
        a = _bdot(hb, wg_ref[...], _NN)
        u = _bdot(hb, wu_ref[...], _NN)
        a_ref[...] = a
        u_ref[...] = u
        s_ref[...] = (_silu(a) * u).astype(BF16)

    w_spec = pl.BlockSpec((None, D, FSH), lambda i, j: (j, 0, 0))
    o_spec = pl.BlockSpec((None, tm, FSH), lambda i, j: (j, i, 0))
    sh = lambda dt: jax.ShapeDtypeStruct((NDEV, T, FSH), dt)
    return _call(body, comm, name, (T // tm, NDEV), [pl.BlockSpec((tm, D), lambda i, j: (i, 0)), w_spec, w_spec],
                 [o_spec] * 3, [sh(F32), sh(F32), sh(BF16)], [], ("parallel", "arbitrary"), [h, wg, wu])


def _ffn_down(s, wd, x, name, comm=None):
    T = x.shape[0]
    tm, tn, sh = min(1024, T), 1024, 4

    def body(s_ref, wd_ref, x_ref, o_ref, acc):
        part = _bdot(s_ref[0], wd_ref[0], _NN)
        for q in range(1, sh):
            part = part + _bdot(s_ref[q], wd_ref[q], _NN)

        def finish(total):
            o_ref[...] = x_ref[...] + 0.5 * total

        _accumulate(pl.program_id(2), NDEV // sh, acc, part, finish)

    xo = pl.BlockSpec((tm, tn), lambda i, n, j: (i, n))
    out = _call(body, comm, name, (T // tm, D // tn, NDEV // sh),
                [pl.BlockSpec((sh, tm, FSH), lambda i, n, j: (j, i, 0)),
                 pl.BlockSpec((sh, FSH, tn), lambda i, n, j: (j, 0, n)), xo],
                [xo], [jax.ShapeDtypeStruct((T, D), F32)], [pltpu.VMEM((tm, tn), F32)],
                ("parallel", "parallel", "arbitrary"), [s, wd, x])
    return out[0] if comm is None else out


def _ffn_bwd_hidden(dx, wd, a, u, name):
    T = dx.shape[0]
    tm = min(1024, T)

    def body(dx_ref, wd_ref, a_ref, u_ref, da_ref, du_ref):
        ds = 0.5 * _bdot(dx_ref[...], wd_ref[...], _NT)
        av = a_ref[...]
        sg = _sigmoid(av)
        da_ref[...] = (ds * u_ref[...] * (sg * (1.0 + av * (1.0 - sg)))).astype(BF16)
        du_ref[...] = (ds * (av * sg)).astype(BF16)

    act = pl.BlockSpec((None, tm, FSH), lambda i, j: (j, i, 0))
    sh = jax.ShapeDtypeStruct((NDEV, T, FSH), BF16)
    return pl.pallas_call(
        body, name=name, grid=(T // tm, NDEV),
        in_specs=[pl.BlockSpec((tm, D), lambda i, j: (i, 0)), pl.BlockSpec((None, FSH, D), lambda i, j: (j, 0, 0)),
                  act, act],
        out_specs=[act, act], out_shape=(sh, sh),
        compiler_params=_params(("parallel", "arbitrary")),
    )(dx, wd, a, u)


def _ffn_bwd_input(da, du, wg, wu, name, comm=None):
    T = da.shape[1]
    tm, tn, sh = min(1024, T), 1024, 2

    def body(da_ref, du_ref, wg_ref, wu_ref, o_ref, acc):
        part = _bdot(da_ref[0], wg_ref[0], _NT) + _bdot(du_ref[0], wu_ref[0], _NT)
        for q in range(1, sh):
            part = part + _bdot(da_ref[q], wg_ref[q], _NT) + _bdot(du_ref[q], wu_ref[q], _NT)

        def finish(total):
            o_ref[...] = total

        _accumulate(pl.program_id(2), NDEV // sh, acc, part, finish)

    act = pl.BlockSpec((sh, tm, FSH), lambda i, n, j: (j, i, 0))
    wsp = pl.BlockSpec((sh, tn, FSH), lambda i, n, j: (j, n, 0))
    out = _call(body, comm, name, (T // tm, D // tn, NDEV // sh), [act, act, wsp, wsp],
                [pl.BlockSpec((tm, tn), lambda i, n, j: (i, n))], [jax.ShapeDtypeStruct((T, D), F32)],
                [pltpu.VMEM((tm, tn), F32)], ("parallel", "parallel", "arbitrary"), [da, du, wg, wu])
    return out[0] if comm is None else out


def _ffn_grad_up(h, da, du, name, comm=None):
    T = h.shape[0]
    tm, tk = 1024, min(4096, T)
    nk = T // tk

    def body(h_ref, da_ref, du_ref, o_ref, acc_a, acc_u):
        k = pl.program_id(2)
        hb = h_ref[...]
        for acc, ref, slot in ((acc_a, da_ref, 0), (acc_u, du_ref, 1)):
            def finish(total, slot=slot):
                o_ref[slot] = total.astype(BF16)

            _accumulate(k, nk, acc, _bdot(hb, ref[...], _TN), finish)

    act = pl.BlockSpec((None, tk, FSH), lambda j, i, t: (j, t, 0))
    out = _call(body, comm, name, (NDEV, D // tm, nk), [pl.BlockSpec((tk, tm), lambda j, i, t: (t, i)), act, act],
                [pl.BlockSpec((None, 2, tm, FSH), lambda j, i, t: (j, 0, i, 0))],
                [jax.ShapeDtypeStruct((NDEV, 2, D, FSH), BF16)],
                [pltpu.VMEM((tm, FSH), F32), pltpu.VMEM((tm, FSH), F32)], ("parallel", "parallel", "arbitrary"), [h, da, du])
    return out[0] if comm is None else out


def _ffn_grad_down(s, dx, name):
    T = dx.shape[0]
    tn, tk = 1024, min(2048, T)
    nk = T // tk

    def body(s_ref, dx_ref, o_ref, acc):
        def finish(total):
            o_ref[...] = (0.5 * total).astype(BF16)

        _accumulate(pl.program_id(2), nk, acc, _bdot(s_ref[...], dx_ref[...], _TN), finish)

    return pl.pallas_call(
        body, name=name, grid=(NDEV, D // tn, nk),
        in_specs=[pl.BlockSpec((None, tk, FSH), lambda j, n, t: (j, t, 0)), pl.BlockSpec((tk, tn), lambda j, n, t: (t, n))],
        out_specs=pl.BlockSpec((None, FSH, tn), lambda j, n, t: (j, 0, n)),
        out_shape=jax.ShapeDtypeStruct((NDEV, DFF // NDEV, D), BF16),
        scratch_shapes=[pltpu.VMEM((FSH, tn), F32)],
        compiler_params=_params(("parallel", "parallel", "arbitrary")),
    )(s, dx)


def _branch_merge(o_gla, o_rw, wb, pg, gate_b):
    T = o_gla.shape[0]
    tm, tn = min(1024, T), 512

    def body(og_ref, or_ref, w1_ref, w2_ref, p1_ref, p2_ref, b1_ref, b2_ref, yg_ref, yr_ref, m_ref):
        yg = _bdot(og_ref[...], w1_ref[...], _NN)
        yr = _bdot(or_ref[...], w2_ref[...], _NN)
        yg_ref[...] = yg
        yr_ref[...] = yr
        m_ref[...] = (_sigmoid(p1_ref[...] + b1_ref[...]) * yg + _sigmoid(p2_ref[...] + b2_ref[...]) * yr).astype(BF16)

    nj = D // tn
    act = pl.BlockSpec((tm, GLA_V), lambda i, j: (i, 0))
    out = pl.BlockSpec((tm, tn), lambda i, j: (i, j))
    return pl.pallas_call(
        body, name="branch_merge", grid=(T // tm, nj),
        in_specs=[act, act, pl.BlockSpec((GLA_V, tn), lambda i, j: (0, j)), pl.BlockSpec((RW_W, tn), lambda i, j: (1, j)),
                  out, pl.BlockSpec((tm, tn), lambda i, j: (i, nj + j)),
                  pl.BlockSpec((1, tn), lambda i, j: (0, j)), pl.BlockSpec((1, tn), lambda i, j: (0, nj + j))],
        out_specs=[out, out, out],
        out_shape=(jax.ShapeDtypeStruct((T, D), F32), jax.ShapeDtypeStruct((T, D), F32), jax.ShapeDtypeStruct((T, D), BF16)),
        compiler_params=_params(("parallel", "arbitrary")),
    )(o_gla, o_rw, wb, wb, pg, pg, gate_b, gate_b)


def _head_indicator(width, heads):
    col = lax.broadcasted_iota(jnp.int32, (width, 128), 0) // (width // heads)
    ind = (col == lax.broadcasted_iota(jnp.int32, (width, 128), 1)).astype(F32)
    return ind, ind.T


def _ffn_fwd(x, g, wg, wu, wd, tag):
    h = _rms_fwd(x, g, "rms_" + tag)
    a, u, s = _ffn_up(h, wg, wu, "ffn_up_" + tag)
    return _ffn_down(s, wd, x, "ffn_down_" + tag), (h, a, u, s)


def _ffn_fwd_gathering(x, g, wg, wu, wd_block, next_block, tag):
    h = _rms_fwd(x, g, "rms_" + tag)
    a, u, s, wd = _ffn_up(h, wg, wu, "ffn_up_" + tag, comm=("gather", [wd_block]))
    y, gathered = _ffn_down(s, wd, x, "ffn_down_" + tag, comm=("gather", [next_block]))
    return y, (h, a, u, s), wd, gathered


def _ffn_bwd(dy, x, g, wg, wu, wd, saved, tag, exchange=False):
    h, a, u, s = saved
    dwd = _ffn_grad_down(s, dy, "ffn_grad_down_" + tag)
    da, du = _ffn_bwd_hidden(dy, wd, a, u, "ffn_bwd_hidden_" + tag)
    if exchange:
        dw_up, dwd = _ffn_grad_up(h, da, du, "ffn_grad_up_" + tag, comm=("exchange", [dwd]))
        dh, dw_up = _ffn_bwd_input(da, du, wg, wu, "ffn_bwd_input_" + tag, comm=("exchange", [dw_up]))
    else:
        dw_up = _ffn_grad_up(h, da, du, "ffn_grad_up_" + tag)
        dh = _ffn_bwd_input(da, du, wg, wu, "ffn_bwd_input_" + tag)
    dx, dg = _rms_bwd(x, g, dh, dy, "rms_bwd_" + tag)
    return dx, dg, dw_up, dwd


def _local_step(x, target, w, blocks):
    T = x.shape[0]
    ind16, ind16_t = _head_indicator(RW_W, RW_HEADS)
    ind4, ind4_t = _head_indicator(GLA_V, GLA_HEADS)
    ltri = jnp.tril(jnp.ones((CHUNK, CHUNK), F32))
    gate_pars = [w["w0"], w["w_w2"], w["a0"], w["w_a2"], w["w_g2"], w["k_k"], w["k_a"], ind16, ind16_t]
    post_pars = [w["lnx_w"], w["lnx_b"], w["r_k"], ind16, ind16_t]

    x1, ffn1, wd1, g_proj = _ffn_fwd_gathering(x, w["g1"], w["wg1"], w["wu1"], blocks["wd1"], blocks["win"], "1")
    win = _align_proj(_unshard_cols(g_proj))
    h2 = _rms_fwd(x1, w["g2"], "rms_mix")
    proj = lambda n, off, name: _matmul(h2, win, "nn", T, n, D, 1024, 512, D, name, b_off=(0, off // 512))
    pg = proj(PG_W, 0, "proj_gate")
    pr = proj(PR_W, PG_W, "proj_rwkv")
    pa = proj(PA_W, PG_W + PR_W, "proj_gla")
    la = _gla_prep(pa, w["gla_w_a2"], w["gla_b_a"])
    o_raw, gla_states = _gla_core_fwd(pa, la, ltri)
    o_gla = _gla_post(o_raw, pa, w["gn"], ind4, ind4_t)
    rw, dec, k2, kk, b, g = _rw_prep(pr, w["mu"], gate_pars)
    y, rw_states, rw_sa, g_up2, g_down2 = _rw_core_fwd(rw, dec, k2, kk, b, gather=blocks["late"])
    w = {**w, **_late_weights(g_up2, g_down2)}
    o_rw = _rw_post(y, rw, k2, g, post_pars)
    y_gla, y_rw, merged = _branch_merge(o_gla, o_rw, w["wb"], pg, w["gate_b"])
    x2 = _matmul(merged, w["wo"], "nn", T, D, D, 1024, 1024, D, "out_proj", res=x1)
    x3, ffn2 = _ffn_fwd(x2, w["g3"], w["wg2"], w["wu2"], w["wd2"], "2")
    dx3, loss, d_gf = _loss_bwd(x3, target, w["gf"])

    grads = {"gf": d_gf}
    dx2, grads["g3"], grads["up2"], grads["wd2"] = _ffn_bwd(
        dx3, x2, w["g3"], w["wg2"], w["wu2"], w["wd2"], ffn2, "2")
    dm = _matmul(dx2, w["wo"], "nt", T, D, D, 1024, 1024, D, "out_proj_bwd")
    grads["wo"] = _matmul(merged, dx2, "tn", D, D, T, 1024, 1024, 2048, "out_proj_grad", out_dtype=BF16)
    dy_gla, dy_rw, dpg, grads["gate_b"] = _merge_bwd(dm, y_gla, y_rw, pg, w["gate_b"])
    do_gla = _matmul(dy_gla, w["wb"], "nt", T, GLA_V, D, 1024, 1024, D, "branch_gla_bwd")
    do_rw = _matmul(dy_rw, w["wb"], "nt", T, RW_W, D, 1024, 1024, D, "branch_rwkv_bwd", b_off=(1, 0))
    grads["wb"] = jnp.concatenate([
        _matmul(o_gla, dy_gla, "tn", GLA_V, D, T, 1024, 1024, 4096, "branch_gla_grad", out_dtype=BF16),
        _matmul(o_rw, dy_rw, "tn", RW_W, D, T, 1024, 1024, 4096, "branch_rwkv_grad", out_dtype=BF16)], axis=0)
    dy, dr2, dv2, dk2b, dg, grads["lnx_w"], grads["lnx_b"], grads["r_k"] = _rw_post_bwd(y, rw, k2, g, post_pars, do_rw)
    early = _late_grad_parts(grads)
    received = {}
    dr1, dw, dk2a, dv1, dkk, db, received["up2"], received["down2"] = _rw_core_bwd(
        rw, dec, k2, kk, b, rw_states, rw_sa, dy, exchange=early)
    drw, grads["mu"], grads["w0"], grads["w_w2"], grads["a0"], grads["w_a2"], grads["w_g2"], grads["k_k"], grads["k_a"] = (
        _rw_prep_bwd(pr, w["mu"], gate_pars, (dr1, dr2), (dv1, dv2), dw, (dk2a, dk2b), dkk, db, dg))
    dpr = _shift_bwd(drw, w["mu"])
    do_raw, dr_gla, grads["gn"] = _gla_post_bwd(o_raw, pa, w["gn"], ind4, ind4_t, do_gla)
    dq, dk, dv, dla = _gla_core_bwd(pa, la, ltri, gla_states, do_raw)
    da_down, grads["gla_w_a2"], grads["gla_b_a"] = _gla_prep_bwd(pa, w["gla_w_a2"], w["gla_b_a"], dla)
    dpa = jnp.concatenate([dq.astype(BF16), dk.astype(BF16), dv.astype(BF16), dr_gla, da_down,
                           jnp.zeros((T, PA_W - PA_USED), BF16)], axis=1)
    dp = jnp.concatenate([dpg, dpr, dpa], axis=1)
    d_win = _matmul(h2, dp, "tn", D, DIN_P, T, 1024, 1024, 4096, "proj_grad", out_dtype=BF16)
    dh2, received["win"] = _matmul(dp, win, "nt", T, D, DIN_P, 1024, 1024, DIN_P // 4, "proj_bwd",
                                   comm=("exchange", [_shard_cols(_unalign_proj(d_win))]))
    dx1, grads["g2"] = _rms_bwd(x1, w["g2"], dh2, dx2, "rms_bwd_mix")
    dx, grads["g1"], received["up1"], received["wd1"] = _ffn_bwd(
        dx1, x, w["g1"], w["wg1"], w["wu1"], wd1, ffn1, "1", exchange=True)
    return loss, dx, grads, received


BIG = ("ffn1_wg", "ffn1_wu", "ffn1_wd", "w_in", "w_branch", "w_out", "ffn2_wg", "ffn2_wu", "ffn2_wd")
SMALL_SHARDED = ("gla_w_a2", "rwkv_w_w2", "rwkv_w_a2", "rwkv_w_g2")
REPLICATED = ("ffn1_norm", "mix_norm", "gla_b_a", "gla_gn_w", "rwkv_mu", "rwkv_w0", "rwkv_a0", "rwkv_k_k", "rwkv_k_a",
              "rwkv_r_k", "rwkv_lnx_w", "rwkv_lnx_b", "gate_b", "ffn2_norm", "final_norm")
WEIGHTS = ("ffn1_norm", "ffn1_wg", "ffn1_wu", "ffn1_wd", "mix_norm", "w_in", "gla_w_a2", "gla_b_a", "gla_gn_w",
           "rwkv_mu", "rwkv_w0", "rwkv_w_w2", "rwkv_a0", "rwkv_w_a2", "rwkv_w_g2", "rwkv_k_k", "rwkv_k_a", "rwkv_r_k",
           "rwkv_lnx_w", "rwkv_lnx_b", "gate_b", "w_branch", "w_out", "ffn2_norm", "ffn2_wg", "ffn2_wu", "ffn2_wd",
           "final_norm")


def _unshard_cols(g):
    return jnp.transpose(g, (1, 0, 2)).reshape(g.shape[1], NDEV * g.shape[2])


def _shard_cols(a):
    return jnp.transpose(a.reshape(a.shape[0], NDEV, a.shape[1] // NDEV), (1, 0, 2))


def _pad_rows(a, rows):
    return jnp.pad(a, ((0, rows - a.shape[0]), (0, 0)))


def _align_rw(a):
    c = 3 * RW_W
    z = jnp.zeros((a.shape[0], LORA_P - DECAY_LORA), a.dtype)
    return jnp.concatenate([a[:, :c], a[:, c:c + DECAY_LORA], z, a[:, c + DECAY_LORA:c + 2 * DECAY_LORA], z,
                            a[:, c + 2 * DECAY_LORA:]], axis=1)


def _unalign_rw(a):
    c = 3 * RW_W
    return jnp.concatenate([a[:, :c + DECAY_LORA], a[:, c + LORA_P:c + LORA_P + AAA_LORA], a[:, c + 2 * LORA_P:]], axis=1)


def _align_proj(a):
    gla = jnp.pad(a[:, :GLA_IN], ((0, 0), (0, PA_W - GLA_IN)))
    return jnp.concatenate([a[:, GLA_IN + RW_IN:], _align_rw(a[:, GLA_IN:GLA_IN + RW_IN]), gla], axis=1)


def _unalign_proj(a):
    return jnp.concatenate([a[:, PG_W + PR_W:PG_W + PR_W + GLA_IN], _unalign_rw(a[:, PG_W:PG_W + PR_W]), a[:, :PG_W]], axis=1)


def _layout_weights(gb, gs, rep):
    row = lambda n: rep[n].reshape(1, -1)
    return {
        "wg1": gb["ffn1_wg"], "wu1": gb["ffn1_wu"],
        "g1": row("ffn1_norm"), "g2": row("mix_norm"), "g3": row("ffn2_norm"), "gf": row("final_norm"),
        "gla_w_a2": _pad_rows(_unshard_cols(gs["gla_w_a2"]), LORA_P), "gla_b_a": row("gla_b_a"),
        "gn": jnp.tile(row("gla_gn_w"), (1, GLA_HEADS)),
        "mu": _align_rw(row("rwkv_mu")), "w0": row("rwkv_w0"), "a0": row("rwkv_a0"),
        "w_w2": _pad_rows(_unshard_cols(gs["rwkv_w_w2"]), LORA_P),
        "w_a2": _pad_rows(_unshard_cols(gs["rwkv_w_a2"]), LORA_P),
        "w_g2": _unshard_cols(gs["rwkv_w_g2"]),
        "k_k": row("rwkv_k_k"), "k_a": row("rwkv_k_a"), "r_k": row("rwkv_r_k"),
        "lnx_w": row("rwkv_lnx_w"), "lnx_b": row("rwkv_lnx_b"), "gate_b": row("gate_b"),
    }


LATE_ROWS = (("ffn2_wd", FSH), ("w_branch", (GLA_V + RW_W) // NDEV), ("w_out", D // NDEV))


def _late_weights(g_up, g_down):
    r1, r2 = LATE_ROWS[0][1], LATE_ROWS[0][1] + LATE_ROWS[1][1]
    return {"wg2": g_up[:, 0], "wu2": g_up[:, 1], "wd2": g_down[:, :r1],
            "wb": g_down[:, r1:r2].reshape(GLA_V + RW_W, D), "wo": g_down[:, r2:].reshape(D, D)}


def _late_grad_parts(g):
    return [g["up2"], jnp.concatenate([g["wd2"], g["wb"].reshape(NDEV, -1, D), g["wo"].reshape(NDEV, -1, D)], axis=1)]


def _layout_grads(g):
    return {
        "ffn1_norm": g["g1"], "mix_norm": g["g2"], "ffn2_norm": g["g3"], "final_norm": g["gf"],
        "gla_w_a2": g["gla_w_a2"][:GLA_LORA], "gla_b_a": g["gla_b_a"],
        "gla_gn_w": jnp.sum(g["gn"].reshape(GLA_HEADS, GLA_DV), axis=0, keepdims=True),
        "rwkv_mu": _unalign_rw(g["mu"]), "rwkv_w0": g["w0"], "rwkv_a0": g["a0"],
        "rwkv_w_w2": g["w_w2"][:DECAY_LORA], "rwkv_w_a2": g["w_a2"][:AAA_LORA], "rwkv_w_g2": g["w_g2"],
        "rwkv_k_k": g["k_k"], "rwkv_k_a": g["k_a"], "rwkv_r_k": g["r_k"],
        "rwkv_lnx_w": g["lnx_w"], "rwkv_lnx_b": g["lnx_b"], "gate_b": g["gate_b"],
    }


_MESH = pl.DeviceIdType.MESH
_ANY = pl.BlockSpec(memory_space=pl.ANY)


def _position():
    return lax.axis_index("x"), lax.axis_index("y"), lax.axis_index("c")


def _slot(p):
    return 4 * p[0] + 2 * p[1] + p[2]


def _comm_sems(n):
    if not n:
        return []
    return [pltpu.SemaphoreType.DMA((7 * n,)), pltpu.SemaphoreType.DMA((7 * n,)), pltpu.SemaphoreType.DMA((n,))]


def _gather_plan(ins, outs, send_sems, recv_sems, local_sems):
    n = len(ins)
    x, y, c = _position()
    me, sibling = (x, y, c), (x, y, 1 - c)
    chips = [(1 - x, y), (x, 1 - y), (1 - x, 1 - y)]

    def copy(a, k, block, to, src=None):
        dst = outs[a].at[_slot(block)]
        return pltpu.make_async_remote_copy(
            src_ref=dst if src is None else src, dst_ref=dst, send_sem=send_sems.at[7 * a + k],
            recv_sem=recv_sems.at[7 * a + k], device_id=to, device_id_type=_MESH)

    def local(a):
        return pltpu.make_async_copy(ins[a], outs[a].at[_slot(me)], local_sems.at[a])

    def own(a):
        return [copy(a, 0, me, sibling, src=ins[a])] + [copy(a, 1 + j, me, (*chip, c), src=ins[a]) for j, chip in enumerate(chips)]

    def start():
        for a in range(n):
            local(a).start()
            for cp in own(a):
                cp.start()

    def forward():
        for a in range(n):
            for j, chip in enumerate(chips):
                copy(a, 1 + j, (*chip, c), me).wait_recv()
                copy(a, 4 + j, (*chip, c), sibling).start()

    def finish():
        for a in range(n):
            copy(a, 0, sibling, me).wait_recv()
            for j, chip in enumerate(chips):
                copy(a, 4 + j, (*chip, 1 - c), me).wait_recv()
        for a in range(n):
            for cp in own(a) + [copy(a, 4 + j, (*chip, c), sibling) for j, chip in enumerate(chips)]:
                cp.wait_send()
            local(a).wait()

    return start, forward, finish


def _exchange_plan(ins, outs, send_sems, recv_sems, local_sems):
    n = len(ins)
    x, y, c = _position()
    me = (x, y, c)
    flip = lambda v, f: 1 - v if f else v
    peers = [(flip(x, fx), flip(y, fy), flip(c, fc))
             for fx, fy, fc in ((0, 0, 1), (1, 0, 0), (0, 1, 0), (1, 1, 0), (1, 0, 1), (0, 1, 1), (1, 1, 1))]

    def copy(a, k, src_slot, dst_slot):
        return pltpu.make_async_remote_copy(
            src_ref=ins[a].at[src_slot], dst_ref=outs[a].at[dst_slot], send_sem=send_sems.at[7 * a + k],
            recv_sem=recv_sems.at[7 * a + k], device_id=peers[k], device_id_type=_MESH)

    def local(a):
        return pltpu.make_async_copy(ins[a].at[_slot(me)], outs[a].at[_slot(me)], local_sems.at[a])

    def start():
        for a in range(n):
            local(a).start()
            for k, peer in enumerate(peers):
                copy(a, k, _slot(peer), _slot(me)).start()

    def finish():
        for a in range(n):
            for k, peer in enumerate(peers):
                copy(a, k, _slot(peer), _slot(peer)).wait_recv()
        for a in range(n):
            for k, peer in enumerate(peers):
                copy(a, k, _slot(peer), _slot(me)).wait_send()
            local(a).wait()

    return start, finish


def _all_gather(arrays, name):
    n = len(arrays)

    def body(*refs):
        start, forward, finish = _gather_plan(refs[:n], refs[n:2 * n], *refs[2 * n:])
        start()
        forward()
        finish()

    return pl.pallas_call(
        body, name=name, in_specs=[_ANY] * n, out_specs=[_ANY] * n,
        out_shape=[jax.ShapeDtypeStruct((NDEV,) + a.shape, a.dtype) for a in arrays], scratch_shapes=_comm_sems(n),
    )(*arrays)


def _adamw_math(w, g, m, v):
    m = ADAM_B1 * m + (1.0 - ADAM_B1) * g
    v = ADAM_B2 * v + (1.0 - ADAM_B2) * (g * g)
    m_hat = m / (1.0 - ADAM_B1 ** ADAM_STEP)
    v_hat = v / (1.0 - ADAM_B2 ** ADAM_STEP)
    delta = -ADAM_LR * (m_hat / (jnp.sqrt(v_hat) + ADAM_EPS) + ADAM_WD * w)
    return delta, m, v


def _sum_slots(ref):
    total = ref[0].astype(F32)
    for s in range(1, NDEV):
        total = total + ref[s].astype(F32)
    return total


def _adamw(parts, w, m, v, tr, name, stack_index=None, row_block_offset=0):
    _, R, C = w.shape

    def body(p_ref, w_ref, m_ref, v_ref, g_ref, d_ref, nm_ref, nv_ref):
        g = _sum_slots(p_ref)
        g_ref[...] = g
        d_ref[...], nm_ref[...], nv_ref[...] = _adamw_math(w_ref[...], g, m_ref[...], v_ref[...])

    if stack_index is None:
        p_spec = pl.BlockSpec((NDEV, tr, C), lambda r: (0, row_block_offset + r, 0))
    else:
        p_spec = pl.BlockSpec((NDEV, None, tr, C), lambda r: (0, stack_index, r, 0))
    blk = pl.BlockSpec((None, tr, C), lambda r: (0, r, 0))
    out = jax.ShapeDtypeStruct((1, R, C), F32)
    return pl.pallas_call(
        body, name=name, grid=(R // tr,), in_specs=[p_spec, blk, blk, blk], out_specs=[blk] * 4, out_shape=(out,) * 4,
        compiler_params=_params(("parallel",)),
    )(parts, w, m, v)


def _sum_gathered(parts):
    _, R, C = parts.shape

    def body(p_ref, o_ref):
        o_ref[...] = _sum_slots(p_ref)

    return pl.pallas_call(body, name="small_grad_sum", out_shape=jax.ShapeDtypeStruct((R, C), F32),
                          compiler_params=_params())(parts)


def _adamw_small(w, g, m, v):
    def body(w_ref, g_ref, m_ref, v_ref, d_ref, nm_ref, nv_ref):
        d_ref[...], nm_ref[...], nv_ref[...] = _adamw_math(w_ref[...], g_ref[...], m_ref[...], v_ref[...])

    out = jax.ShapeDtypeStruct(w.shape, F32)
    return pl.pallas_call(body, name="adamw_small", out_shape=(out,) * 3, compiler_params=_params())(w, g, m, v)


def _pack(pieces, rows):
    flat = jnp.concatenate([p.reshape(-1) for p in pieces])
    return jnp.pad(flat, (0, rows * 128 - flat.shape[0])).reshape(rows, 128)


def _unpack(packed, shapes):
    flat = packed.reshape(-1)
    out, off = [], 0
    for s in shapes:
        size = 1
        for d in s:
            size *= d
        out.append(flat[off:off + size].reshape(s))
        off += size
    return out


def _rows_for(shapes, extra=0):
    total = extra
    for s in shapes:
        size = 1
        for d in s:
            size *= d
        total += size
    return -(-total // 1024) * 8


def kernel(x, ffn1_norm, ffn1_wg, ffn1_wu, ffn1_wd, mix_norm, w_in, gla_w_a2, gla_b_a, gla_gn_w, rwkv_mu, rwkv_w0, rwkv_w_w2, rwkv_a0, rwkv_w_a2, rwkv_w_g2, rwkv_k_k, rwkv_k_a, rwkv_r_k, rwkv_lnx_w, rwkv_lnx_b, gate_b, w_branch, w_out, ffn2_norm, ffn2_wg, ffn2_wu, ffn2_wd, final_norm, loss_target, m_ffn1_norm, m_ffn1_wg, m_ffn1_wu, m_ffn1_wd, m_mix_norm, m_w_in, m_gla_w_a2, m_gla_b_a, m_gla_gn_w, m_rwkv_mu, m_rwkv_w0, m_rwkv_w_w2, m_rwkv_a0, m_rwkv_w_a2, m_rwkv_w_g2, m_rwkv_k_k, m_rwkv_k_a, m_rwkv_r_k, m_rwkv_lnx_w, m_rwkv_lnx_b, m_gate_b, m_w_branch, m_w_out, m_ffn2_norm, m_ffn2_wg, m_ffn2_wu, m_ffn2_wd, m_final_norm, v_ffn1_norm, v_ffn1_wg, v_ffn1_wu, v_ffn1_wd, v_mix_norm, v_w_in, v_gla_w_a2, v_gla_b_a, v_gla_gn_w, v_rwkv_mu, v_rwkv_w0, v_rwkv_w_w2, v_rwkv_a0, v_rwkv_w_a2, v_rwkv_w_g2, v_rwkv_k_k, v_rwkv_k_a, v_rwkv_r_k, v_rwkv_lnx_w, v_rwkv_lnx_b, v_gate_b, v_w_branch, v_w_out, v_ffn2_norm, v_ffn2_wg, v_ffn2_wu, v_ffn2_wd, v_final_norm):
    wts = dict(zip(WEIGHTS, (ffn1_norm, ffn1_wg, ffn1_wu, ffn1_wd, mix_norm, w_in, gla_w_a2, gla_b_a, gla_gn_w, rwkv_mu, rwkv_w0, rwkv_w_w2, rwkv_a0, rwkv_w_a2, rwkv_w_g2, rwkv_k_k, rwkv_k_a, rwkv_r_k, rwkv_lnx_w, rwkv_lnx_b, gate_b, w_branch, w_out, ffn2_norm, ffn2_wg, ffn2_wu, ffn2_wd, final_norm)))
    mom = dict(zip(WEIGHTS, (m_ffn1_norm, m_ffn1_wg, m_ffn1_wu, m_ffn1_wd, m_mix_norm, m_w_in, m_gla_w_a2, m_gla_b_a, m_gla_gn_w, m_rwkv_mu, m_rwkv_w0, m_rwkv_w_w2, m_rwkv_a0, m_rwkv_w_a2, m_rwkv_w_g2, m_rwkv_k_k, m_rwkv_k_a, m_rwkv_r_k, m_rwkv_lnx_w, m_rwkv_lnx_b, m_gate_b, m_w_branch, m_w_out, m_ffn2_norm, m_ffn2_wg, m_ffn2_wu, m_ffn2_wd, m_final_norm)))
    var = dict(zip(WEIGHTS, (v_ffn1_norm, v_ffn1_wg, v_ffn1_wu, v_ffn1_wd, v_mix_norm, v_w_in, v_gla_w_a2, v_gla_b_a, v_gla_gn_w, v_rwkv_mu, v_rwkv_w0, v_rwkv_w_w2, v_rwkv_a0, v_rwkv_w_a2, v_rwkv_w_g2, v_rwkv_k_k, v_rwkv_k_a, v_rwkv_r_k, v_rwkv_lnx_w, v_rwkv_lnx_b, v_gate_b, v_w_branch, v_w_out, v_ffn2_norm, v_ffn2_wg, v_ffn2_wu, v_ffn2_wd, v_final_norm)))
    two = lambda a: a.reshape(a.shape[-2:])

    bf = lambda n: two(wts[n]).astype(BF16)
    up1 = jnp.stack([bf("ffn1_wg"), bf("ffn1_wu")])
    lora = jnp.concatenate([jnp.pad(two(gla_w_a2), ((0, 0), (0, 128 - GLA_QK // NDEV)))] +
                           [two(wts[n]) for n in SMALL_SHARDED[1:]], axis=0)
    g_up1, g_lora = _all_gather([up1, lora], "gather_weights")
    gb = {"ffn1_wg": g_up1[:, 0], "ffn1_wu": g_up1[:, 1]}
    gs = {"gla_w_a2": g_lora[:, :GLA_LORA, :GLA_QK // NDEV]}
    row = GLA_LORA
    for n in SMALL_SHARDED[1:]:
        gs[n] = g_lora[:, row:row + wts[n].shape[1]]
        row += wts[n].shape[1]
    w = _layout_weights(gb, gs, {n: wts[n] for n in REPLICATED})
    blocks = {"wd1": bf("ffn1_wd"), "win": bf("w_in"),
              "late": [jnp.stack([bf("ffn2_wg"), bf("ffn2_wu")]), jnp.concatenate([bf(n) for n, _ in LATE_ROWS], axis=0)]}

    loss_part, grad_x, grads, parts = _local_step(x[0], loss_target[0], w, blocks)
    small = _layout_grads(grads)
    result = {}
    state = lambda n: (wts[n], mom[n], var[n])
    for group, names in (("up1", ("ffn1_wg", "ffn1_wu")), ("up2", ("ffn2_wg", "ffn2_wu"))):
        for i, n in enumerate(names):
            result[n] = _adamw(parts[group], *state(n), 256, "adamw_" + n, stack_index=i)
    result["ffn1_wd"] = _adamw(parts["wd1"], *state("ffn1_wd"), 64, "adamw_ffn1_wd")
    row = 0
    for n, rows in LATE_ROWS:
        result[n] = _adamw(parts["down2"], *state(n), 64, "adamw_" + n, row_block_offset=row // 64)
        row += rows
    result["w_in"] = _adamw(parts["win"], *state("w_in"), 256, "adamw_w_in")

    small_names = [n for n in WEIGHTS if n not in BIG]
    full_shapes = [small[n].shape for n in small_names]
    rows_full = _rows_for(full_shapes, extra=128)
    packed = _pack([small[n] for n in small_names] + [loss_part], rows_full)
    (gathered,) = _all_gather([packed], "gather_small_grads")
    total = _sum_gathered(gathered)
    *full_grads, loss_row = _unpack(total, full_shapes + [(1, 128)])
    me = _slot(_position())
    own = {}
    for n, g in zip(small_names, full_grads):
        if n in SMALL_SHARDED:
            cols = wts[n].shape[-1]
            g = lax.dynamic_slice_in_dim(g, me * cols, cols, axis=1)
        own[n] = g.reshape(wts[n].shape)
    own_shapes = [wts[n].shape for n in small_names]
    rows_own = _rows_for(own_shapes)
    pk = lambda d: _pack([d[n] for n in small_names], rows_own)
    d_s, m_s, v_s = _adamw_small(pk(wts), pk(own), pk(mom), pk(var))
    for n, d, m, v in zip(small_names, _unpack(d_s, own_shapes), _unpack(m_s, own_shapes), _unpack(v_s, own_shapes)):
        result[n] = (own[n], d, m, v)

    shaped = lambda n, k: result[n][k].reshape(wts[n].shape)
    return (loss_row[0, 0], grad_x[None],
            *[shaped(n, 0) for n in WEIGHTS], *[shaped(n, 1) for n in WEIGHTS],
            *[shaped(n, 2) for n in WEIGHTS], *[shaped(n, 3) for n in WEIGHTS])
```

```python
import functools

import jax
import jax.numpy as jnp
from jax import lax
from jax.experimental import pallas as pl
from jax.experimental.pallas import tpu as pltpu

F32 = jnp.float32
BF16 = jnp.bfloat16
HI = lax.Precision.HIGHEST

NDEV = 8
D = 2048
DFF = 5632
FSH = DFF // NDEV
CHUNK = 64
GLA_HEADS, GLA_DK, GLA_DV = 4, 128, 256
GLA_QK, GLA_V, GLA_LORA, GLA_TAU = 512, 1024, 16, 16.0
RW_HEADS, RW_HD, RW_W = 16, 64, 1024
DECAY_LORA, AAA_LORA, GATE_LORA = 96, 96, 256
GN_EPS = 64e-5
NORM_EPS = 1e-6
GLA_IN = 2 * GLA_QK + 2 * GLA_V + GLA_LORA
RW_IN = 3 * RW_W + DECAY_LORA + AAA_LORA + GATE_LORA
D_IN = GLA_IN + RW_IN + 2 * D
DIN_SH = D_IN // NDEV
PG_W = 2 * D
PR_W = 3584
PA_W = 3584
PA_USED = 2 * GLA_QK + 2 * GLA_V + 128
DIN_P = PG_W + PR_W + PA_W
LORA_P = 128

ADAM_LR, ADAM_B1, ADAM_B2, ADAM_EPS, ADAM_WD, ADAM_STEP = 0.001, 0.9, 0.999, 1e-08, 0.01, 10

VMEM_LIMIT = 56 * 1024 * 1024
RW_TB = 32
RW_G = 16
RW_NP = 8


def _params(sem=None, vmem=VMEM_LIMIT):
    return pltpu.CompilerParams(dimension_semantics=sem, vmem_limit_bytes=vmem)


def _pair_mask():
    return lax.broadcasted_iota(jnp.int32, (RW_HD, 2 * RW_HD), 1) < RW_HD


def _pair_rowsum(p, mask):
    tot = jnp.sum(p, axis=1, keepdims=True)
    first = jnp.sum(jnp.where(mask, p, 0.0), axis=1, keepdims=True)
    return first, tot - first


def _split_transposed(x_ref, q, dst_ref, base):
    xt = x_ref[:, 128 * q:128 * (q + 1)].T
    for g in range(RW_TB // RW_G):
        dst_ref[base + g, :, 0:RW_G] = xt[:, g * RW_G:(g + 1) * RW_G]


def _pair_column(tile_ref, idx, i, mask):
    return jnp.where(mask, tile_ref[idx, 0:RW_HD, i:i + 1], tile_ref[idx, RW_HD:, i:i + 1])


def _rw_core_fwd(rw, w, k2, kk, b, gather=()):
    T = rw.shape[0]
    nb = T // RW_TB
    ng = RW_TB // RW_G
    NP = RW_NP
    nc = len(gather)
    npair = RW_HEADS // 2 // NP

    def body(r_ref, v_ref, w_ref, k_ref, kk_ref, b_ref, *rest):
        g_in, (y_ref, st_ref, sa_ref), g_out = rest[:nc], rest[nc:nc + 3], rest[nc + 3:2 * nc + 3]
        s_scr, vt_scr, yt_scr, rows_scr = rest[2 * nc + 3:2 * nc + 7]
        pair, blk_i = pl.program_id(0), pl.program_id(1)
        if nc:
            start, forward, finish = _gather_plan(g_in, g_out, *rest[2 * nc + 7:])
            pl.when((pair == 0) & (blk_i == 0))(start)
            pl.when((pair == 0) & (blk_i == nb // 2))(forward)

        @pl.when(pl.program_id(1) == 0)
        def _():
            s_scr[...] = jnp.zeros_like(s_scr)
            yt_scr[...] = jnp.zeros_like(yt_scr)

        mask = _pair_mask()
        for q in range(NP):
            _split_transposed(v_ref, q, vt_scr, q * ng)
        R_, W_, K_, KK_, B_ = range(5)
        for a, ref in enumerate((r_ref, w_ref, k_ref, kk_ref, b_ref)):
            for q in range(NP):
                rows_scr[a * NP + q] = ref[:, 128 * q:128 * (q + 1)]

        def group(g, states):
            states = list(states)
            for i in range(RW_G):
                t = g * RW_G + i
                row = lambda a, q: rows_scr[a * NP + q, pl.ds(t, 1), :]
                sums = [_pair_rowsum(states[q] * row(KK_, q), mask) for q in range(NP)]
                for q in range(NP):
                    sa = jnp.where(mask, *sums[q])
                    sa_ref[q, t] = sa
                    states[q] = (states[q] * row(W_, q) - sa * row(B_, q)
                                 + _pair_column(vt_scr, q * ng + g, i, mask) * row(K_, q))
                    st_ref[q, t] = states[q]
                outs = [_pair_rowsum(states[q] * row(R_, q), mask) for q in range(NP)]
                for q in range(NP):
                    yt_scr[q * ng + g, 0:RW_HD, i:i + 1] = outs[q][0]
                    yt_scr[q * ng + g, RW_HD:, i:i + 1] = outs[q][1]
            return tuple(states)

        states = lax.fori_loop(0, ng, group, tuple(s_scr[q] for q in range(NP)))
        for q in range(NP):
            s_scr[q] = states[q]
            for g in range(ng):
                y_ref[g * RW_G:(g + 1) * RW_G, 128 * q:128 * (q + 1)] = yt_scr[q * ng + g].T[0:RW_G, :]
        if nc:
            pl.when((pair == npair - 1) & (blk_i == nb - 1))(finish)

    blk = lambda cb: pl.BlockSpec((RW_TB, 128 * NP), lambda p, i, cb=cb: (i, cb + p))
    tiles = pltpu.VMEM((NP * ng, 128, 128), F32)
    return pl.pallas_call(
        body, name="rw_core_fwd", grid=(npair, nb),
        in_specs=[blk(0), blk(2 * RW_W // (128 * NP)), blk(0), blk(0), blk(0), blk(0)] + [_ANY] * nc,
        out_specs=[blk(0)] + [pl.BlockSpec((NP, RW_TB, RW_HD, 128), lambda p, i: (p, i, 0, 0))] * 2 + [_ANY] * nc,
        out_shape=[jax.ShapeDtypeStruct((T, RW_W), F32)] + [jax.ShapeDtypeStruct((RW_HEADS // 2, T, RW_HD, 128), F32)] * 2
        + [jax.ShapeDtypeStruct((NDEV,) + a.shape, a.dtype) for a in gather],
        scratch_shapes=[pltpu.VMEM((NP, RW_HD, 128), F32), tiles, tiles, pltpu.VMEM((5 * NP, RW_TB, 128), F32)]
        + _comm_sems(nc),
        compiler_params=_params(("arbitrary", "arbitrary")),
    )(rw, rw, w, k2, kk, b, *gather)


def _rw_core_bwd(rw, w, k2, kk, b, states, sa_tiles, dy, exchange=()):
    T = rw.shape[0]
    nb = T // RW_TB
    ng = RW_TB // RW_G
    NP = RW_NP
    nc = len(exchange)
    npair = RW_HEADS // 2 // NP

    def body(r_ref, v_ref, w_ref, k_ref, kk_ref, b_ref, dy_ref, st_ref, sp_ref, sa_ref, *rest):
        e_in, e_out = rest[:nc], rest[nc + 6:2 * nc + 6]
        dr_ref, dw_ref, dk_ref, dv_ref, dkk_ref, db_ref = rest[nc:nc + 6]
        ds_scr, vt_scr, dyt_scr, dvt_scr, rows_scr, out_scr = rest[2 * nc + 6:2 * nc + 12]
        step = pl.program_id(1)
        if nc:
            start, finish = _exchange_plan(e_in, e_out, *rest[2 * nc + 12:])
            pl.when((pl.program_id(0) == 0) & (step == 0))(start)

        @pl.when(step == 0)
        def _():
            ds_scr[...] = jnp.zeros_like(ds_scr)
            dvt_scr[...] = jnp.zeros_like(dvt_scr)

        mask = _pair_mask()
        for q in range(NP):
            _split_transposed(v_ref, q, vt_scr, q * ng)
            _split_transposed(dy_ref, q, dyt_scr, q * ng)
        R_, W_, K_, KK_, B_ = range(5)
        for a, ref in enumerate((r_ref, w_ref, k_ref, kk_ref, b_ref)):
            for q in range(NP):
                rows_scr[a * NP + q] = ref[:, 128 * q:128 * (q + 1)]

        def group(gg, grads):
            g = ng - 1 - gg
            grads = list(grads)
            pairs = range(NP)
            for i in reversed(range(RW_G)):
                t = g * RW_G + i
                row = lambda a, q: rows_scr[a * NP + q, pl.ds(t, 1), :]

                def put(a, q, value):
                    out_scr[a * NP + q, pl.ds(t, 1), :] = value

                s_old = [st_ref[q, jnp.maximum(t - 1, 0)] for q in pairs]
                if i == 0:
                    s_old = [jnp.where(g == 0, jnp.where(step == nb - 1, 0.0, sp_ref[q, 0]), s_old[q]) for q in pairs]
                dycol = [_pair_column(dyt_scr, q * ng + g, i, mask) for q in pairs]
                dS = [grads[q] + dycol[q] * row(R_, q) for q in pairs]
                m = [_pair_rowsum(dS[q] * row(B_, q), mask) for q in pairs]
                dv = [_pair_rowsum(dS[q] * row(K_, q), mask) for q in pairs]
                for q in pairs:
                    put(R_, q, jnp.sum(st_ref[q, t] * dycol[q], axis=0, keepdims=True))
                    put(W_, q, jnp.sum(dS[q] * s_old[q], axis=0, keepdims=True))
                    put(K_, q, jnp.sum(dS[q] * _pair_column(vt_scr, q * ng + g, i, mask), axis=0, keepdims=True))
                for q in pairs:
                    dsa = -jnp.where(mask, *m[q])
                    grads[q] = dS[q] * row(W_, q) + dsa * row(KK_, q)
                    put(KK_, q, jnp.sum(s_old[q] * dsa, axis=0, keepdims=True))
                    put(B_, q, -jnp.sum(dS[q] * sa_ref[q, t], axis=0, keepdims=True))
                    dvt_scr[q * ng + g, 0:RW_HD, i:i + 1] = dv[q][0]
                    dvt_scr[q * ng + g, RW_HD:, i:i + 1] = dv[q][1]
            return tuple(grads)

        grads = lax.fori_loop(0, ng, group, tuple(ds_scr[q] for q in range(NP)))
        for q in range(NP):
            ds_scr[q] = grads[q]
            for a, ref in enumerate((dr_ref, dw_ref, dk_ref, dkk_ref, db_ref)):
                ref[:, 128 * q:128 * (q + 1)] = out_scr[a * NP + q]
            for g in range(ng):
                dv_ref[g * RW_G:(g + 1) * RW_G, 128 * q:128 * (q + 1)] = dvt_scr[q * ng + g].T[0:RW_G, :]
        if nc:
            pl.when((pl.program_id(0) == npair - 1) & (step == nb - 1))(finish)

    blk = lambda cb: pl.BlockSpec((RW_TB, 128 * NP), lambda p, i, cb=cb: (nb - 1 - i, cb + p))
    st_spec = pl.BlockSpec((NP, RW_TB, RW_HD, 128), lambda p, i: (p, nb - 1 - i, 0, 0))
    sp_spec = pl.BlockSpec((NP, 1, RW_HD, 128), lambda p, i: (p, jnp.maximum((nb - 1 - i) * RW_TB - 1, 0), 0, 0))
    out = jax.ShapeDtypeStruct((T, RW_W), F32)
    tiles = pltpu.VMEM((NP * ng, 128, 128), F32)
    return pl.pallas_call(
        body, name="rw_core_bwd", grid=(npair, nb),
        in_specs=[blk(0), blk(2 * RW_W // (128 * NP)), blk(0), blk(0), blk(0), blk(0), blk(0), st_spec, sp_spec, st_spec]
        + [_ANY] * nc,
        out_specs=[blk(0)] * 6 + [_ANY] * nc,
        out_shape=[out] * 6 + [jax.ShapeDtypeStruct(a.shape, a.dtype) for a in exchange],
        scratch_shapes=[pltpu.VMEM((NP, RW_HD, 128), F32), tiles, tiles, tiles,
                        pltpu.VMEM((5 * NP, RW_TB, 128), F32), pltpu.VMEM((5 * NP, RW_TB, 128), F32)] + _comm_sems(nc),
        compiler_params=_params(("arbitrary", "arbitrary")),
    )(rw, rw, w, k2, kk, b, dy, states, states, sa_tiles, *exchange)


GLA_CB = 8


def _gla_chunk(s_t, q, k, v, la, ltri):
    cum = jnp.dot(ltri, la, precision=HI, preferred_element_type=F32)
    total = jnp.sum(la, axis=0, keepdims=True)
    kdec = k * jnp.exp(total - cum)
    u_t = _bdot(v, kdec, _TN)
    s_t = jnp.exp(total) * s_t + u_t
    o = _bdot(q * (GLA_DK ** -0.5), s_t, _NT)
    return s_t, o


def _gla_core_fwd(pa, la, ltri):
    T = pa.shape[0]
    cb = min(GLA_CB, T // CHUNK)
    rows = cb * CHUNK
    nsteps = T // rows

    def body(q_ref, k_ref, v_ref, la_ref, ltri_ref, o_ref, st_ref, s_scr):
        @pl.when(pl.program_id(1) == 0)
        def _():
            s_scr[...] = jnp.zeros_like(s_scr)

        def chunk(c, s_t):
            sl = pl.ds(pl.multiple_of(c * CHUNK, CHUNK), CHUNK)
            s_t, o = _gla_chunk(s_t, q_ref[sl, :], k_ref[sl, :], v_ref[sl, :], la_ref[sl, :], ltri_ref[...])
            o_ref[sl, :] = o
            st_ref[0, c] = s_t
            return s_t

        s_scr[...] = lax.fori_loop(0, cb, chunk, s_scr[...])

    qk = lambda off: pl.BlockSpec((rows, GLA_DK), lambda h, i, off=off: (i, off + h))
    vspec = pl.BlockSpec((rows, GLA_DV), lambda h, i: (i, 2 * GLA_QK // GLA_DV + h))
    return pl.pallas_call(
        body, name="gla_core_fwd", grid=(GLA_HEADS, nsteps),
        in_specs=[qk(0), qk(GLA_HEADS), vspec, qk(0), pl.BlockSpec((CHUNK, CHUNK), lambda h, i: (0, 0))],
        out_specs=[pl.BlockSpec((rows, GLA_DV), lambda h, i: (i, h)),
                   pl.BlockSpec((1, cb, GLA_DV, GLA_DK), lambda h, i: (h, i, 0, 0))],
        out_shape=(jax.ShapeDtypeStruct((T, GLA_V), F32),
                   jax.ShapeDtypeStruct((GLA_HEADS, T // CHUNK, GLA_DV, GLA_DK), F32)),
        scratch_shapes=[pltpu.VMEM((GLA_DV, GLA_DK), F32)],
        compiler_params=_params(("arbitrary", "arbitrary")),
    )(pa, pa, pa, la, ltri)


def _gla_core_bwd(pa, la, ltri, states, do):
    T = pa.shape[0]
    cb = min(GLA_CB, T // CHUNK)
    rows = cb * CHUNK
    nsteps = T // rows

    def body(q_ref, k_ref, v_ref, la_ref, ltri_ref, st_ref, sp_ref, do_ref,
             dq_ref, dk_ref, dv_ref, dla_ref, ds_scr):
        step = pl.program_id(1)

        @pl.when(step == 0)
        def _():
            ds_scr[...] = jnp.zeros_like(ds_scr)

        s_before = jnp.where(step == nsteps - 1, 0.0, sp_ref[0, 0])

        def chunk(cc, ds_t):
            c = cb - 1 - cc
            sl = pl.ds(pl.multiple_of(c * CHUNK, CHUNK), CHUNK)
            s_prev = jnp.where(c == 0, s_before, st_ref[0, jnp.maximum(c - 1, 0)])
            _, vjp = jax.vjp(functools.partial(_gla_chunk, ltri=ltri_ref[...]),
                             s_prev, q_ref[sl, :], k_ref[sl, :], v_ref[sl, :], la_ref[sl, :])
            ds_prev, dq, dk, dv, dla = vjp((ds_t, do_ref[sl, :]))
            dq_ref[sl, :] = dq
            dk_ref[sl, :] = dk
            dv_ref[sl, :] = dv
            dla_ref[sl, :] = dla
            return ds_prev

        ds_scr[...] = lax.fori_loop(0, cb, chunk, ds_scr[...])

    r = lambda i: nsteps - 1 - i
    qk = lambda off: pl.BlockSpec((rows, GLA_DK), lambda h, i, off=off: (r(i), off + h))
    vspec = pl.BlockSpec((rows, GLA_DV), lambda h, i: (r(i), 2 * GLA_QK // GLA_DV + h))
    o128 = pl.BlockSpec((rows, GLA_DK), lambda h, i: (r(i), h))
    o256 = pl.BlockSpec((rows, GLA_DV), lambda h, i: (r(i), h))
    return pl.pallas_call(
        body, name="gla_core_bwd", grid=(GLA_HEADS, nsteps),
        in_specs=[qk(0), qk(GLA_HEADS), vspec, qk(0), pl.BlockSpec((CHUNK, CHUNK), lambda h, i: (0, 0)),
                  pl.BlockSpec((1, cb, GLA_DV, GLA_DK), lambda h, i: (h, r(i), 0, 0)),
                  pl.BlockSpec((1, 1, GLA_DV, GLA_DK), lambda h, i: (h, jnp.maximum(r(i) * cb - 1, 0), 0, 0)),
                  o256],
        out_specs=[o128, o128, o256, o128],
        out_shape=(jax.ShapeDtypeStruct((T, GLA_QK), F32), jax.ShapeDtypeStruct((T, GLA_QK), F32),
                   jax.ShapeDtypeStruct((T, GLA_V), F32), jax.ShapeDtypeStruct((T, GLA_QK), F32)),
        scratch_shapes=[pltpu.VMEM((GLA_DV, GLA_DK), F32)],
        compiler_params=_params(("arbitrary", "arbitrary")),
    )(pa, pa, pa, la, ltri, states, states, do)


def _rowwise(fn, name, T, tm, rows, pars, row_outs, acc_outs):
    nr, npar, nro = len(rows), len(pars), len(row_outs)
    tm = min(tm, T)
    nsteps = T // tm

    def body(*refs):
        i = pl.program_id(0)
        ins = [r[...] for r in refs[:nr + npar]]
        outs, accs = fn(i, *ins)
        for r, o in zip(refs[nr + npar:nr + npar + nro], outs):
            r[...] = o.astype(r.dtype)
        for r, a in zip(refs[nr + npar + nro:], accs):
            @pl.when(i == 0)
            def _(r=r, a=a):
                r[...] = a

            @pl.when(i > 0)
            def _(r=r, a=a):
                r[...] += a

    def rspec(width, cb, kind):
        if kind == "cur":
            return pl.BlockSpec((tm, width), lambda i: (i, cb))
        if kind == "prev":
            return pl.BlockSpec((8, width), lambda i: (jnp.maximum(i * (tm // 8) - 1, 0), cb))
        return pl.BlockSpec((8, width), lambda i: (jnp.minimum((i + 1) * (tm // 8), T // 8 - 1), cb))

    in_specs = [rspec(w, cb, kind) for (_, w, cb, kind) in rows]
    in_specs += [pl.BlockSpec(p.shape, lambda i, nd=p.ndim: (0,) * nd) for p in pars]
    out_specs = [pl.BlockSpec((tm, w), lambda i: (i, 0)) for (w, _) in row_outs]
    out_specs += [pl.BlockSpec(s, lambda i, nd=len(s): (0,) * nd) for s in acc_outs]
    out_shape = [jax.ShapeDtypeStruct((T, w), dt) for (w, dt) in row_outs]
    out_shape += [jax.ShapeDtypeStruct(s, F32) for s in acc_outs]
    res = pl.pallas_call(
        body, name=name, grid=(nsteps,), in_specs=in_specs, out_specs=out_specs, out_shape=out_shape,
        compiler_params=_params(("arbitrary",)),
    )(*[r[0] for r in rows], *pars)
    return res


def _cur(a, width=None, cb=0):
    return (a, a.shape[1] if width is None else width, cb, "cur")


def _sigmoid(x):
    return 1.0 / (1.0 + jnp.exp(-x))


def _silu(x):
    return x * _sigmoid(x)


def _softplus(x):
    return jnp.maximum(x, 0.0) + jnp.log(1.0 + jnp.exp(-jnp.abs(x)))


def _rms(x, g):
    return x * lax.rsqrt(jnp.mean(x * x, axis=-1, keepdims=True) + NORM_EPS) * g


def _dot_hi(a, b):
    return jnp.dot(a, b, precision=lax.Precision.HIGH, preferred_element_type=F32)


def _rms_fwd(x, g, name):
    T = x.shape[0]
    fn = lambda i, xb, gb: ((_rms(xb, gb),), ())
    return _rowwise(fn, name, T, 256, [_cur(x)], [g], [(D, BF16)], [])[0]


def _rms_bwd(x, g, dh, dres, name):
    T = x.shape[0]

    def fn(i, xb, dhb, drb, gb):
        _, vjp = jax.vjp(_rms, xb, gb)
        dx, dg = vjp(dhb)
        return (drb + dx,), (dg,)

    return _rowwise(fn, name, T, 256, [_cur(x), _cur(dh), _cur(dres)], [g], [(D, F32)], [(1, D)])


def _loss_bwd(x, target, g):
    T = x.shape[0]

    def loss(xb, gb, tb):
        err = _rms(xb, gb) - tb
        return 0.5 * jnp.sum(jnp.mean(err * err, axis=-1, keepdims=True))

    def fn(i, xb, tb, gb):
        val, (dx, dg) = jax.value_and_grad(loss, argnums=(0, 1))(xb, gb, tb)
        return (dx,), (jnp.full((1, 128), val, F32), dg)

    return _rowwise(fn, "loss_bwd", T, 256, [_cur(x), _cur(target)], [g], [(D, F32)], [(1, 128), (1, D)])


def _gla_la(a_down, w_a2, b_a):
    return -_softplus(-(_bdot(a_down, w_a2, _NN) + b_a)) * (1.0 / GLA_TAU)


def _gla_prep(pa, w_a2, b_a):
    T = pa.shape[0]
    fn = lambda i, ab, wb, bb: ((_gla_la(ab, wb, bb),), ())
    return _rowwise(fn, "gla_prep", T, 512, [_cur(pa, LORA_P, (2 * GLA_QK + 2 * GLA_V) // LORA_P)], [w_a2, b_a],
                    [(GLA_QK, F32)], [])[0]


def _gla_prep_bwd(pa, w_a2, b_a, dla):
    T = pa.shape[0]

    def fn(i, ab, dlab, wb, bb):
        _, vjp = jax.vjp(_gla_la, ab, wb, bb)
        da, dw, db = vjp(dlab)
        return (da,), (dw, db)

    return _rowwise(fn, "gla_prep_bwd", T, 512, [_cur(pa, LORA_P, (2 * GLA_QK + 2 * GLA_V) // LORA_P), _cur(dla)],
                    [w_a2, b_a], [(LORA_P, BF16)], [(LORA_P, GLA_QK), (1, GLA_QK)])


def _gla_out(o, r, gn, ind, ind_t):
    ms = _dot_hi(_dot_hi(o * o, ind) * (1.0 / GLA_DV), ind_t)
    return o * lax.rsqrt(ms + NORM_EPS) * gn * _silu(r)


def _gla_post(o_raw, pa, gn, ind, ind_t):
    T = pa.shape[0]
    fn = lambda i, ob, rb, gb, a, b: ((_gla_out(ob, rb, gb, a, b),), ())
    return _rowwise(fn, "gla_post", T, 256, [_cur(o_raw), _cur(pa, GLA_V, 2)], [gn, ind, ind_t], [(GLA_V, BF16)], [])[0]


def _gla_post_bwd(o_raw, pa, gn, ind, ind_t, do):
    T = pa.shape[0]

    def fn(i, ob, rb, dob, gb, a, b):
        _, vjp = jax.vjp(lambda o, r, g: _gla_out(o, r, g, a, b), ob, rb, gb)
        d_o, d_r, d_g = vjp(dob)
        return (d_o, d_r), (d_g,)

    return _rowwise(fn, "gla_post_bwd", T, 256, [_cur(o_raw), _cur(pa, GLA_V, 2), _cur(do)], [gn, ind, ind_t],
                    [(GLA_V, F32), (GLA_V, BF16)], [(1, GLA_V)])


def _shift_rows(cur, prev8, i):
    first = jnp.where(i == 0, 0.0, prev8[7:8, :])
    rolled = pltpu.roll(cur, 1, 0)
    return jnp.where(lax.broadcasted_iota(jnp.int32, cur.shape, 0) == 0, first, rolled)


def _rw_gates(rw, w0, w_w2, a0, w_a2, w_g2, k_k, k_a, ind, ind_t):
    rk = rw[:, RW_W:2 * RW_W]
    wd = rw[:, 3 * RW_W:3 * RW_W + LORA_P]
    ad = rw[:, 3 * RW_W + LORA_P:3 * RW_W + 2 * LORA_P]
    gd = rw[:, 3 * RW_W + 2 * LORA_P:]
    w_raw = w0 + _bdot(jnp.tanh(wd), w_w2, _NN)
    w = jnp.exp(-jnp.exp(-_softplus(-w_raw) - 0.5))
    a = _sigmoid(a0 + _bdot(ad, w_a2, _NN))
    g = _bdot(_sigmoid(gd), w_g2, _NN)
    kk = rk * k_k
    kk = kk * _dot_hi(lax.rsqrt(jnp.maximum(_dot_hi(kk * kk, ind), 1e-24)), ind_t)
    k2 = rk * (1.0 + (a - 1.0) * k_a)
    return w, k2, kk, kk * a, g


def _rw_prep(pr, mu, gate_pars):
    T = pr.shape[0]

    def fn(i, cur, prev8, mub, *gp):
        rw = cur + mub * (_shift_rows(cur, prev8, i) - cur)
        return (rw,) + _rw_gates(rw, *gp), ()

    return _rowwise(fn, "rw_prep", T, 256, [_cur(pr), (pr, PR_W, 0, "prev")], [mu, *gate_pars],
                    [(PR_W, F32)] + [(RW_W, F32)] * 5, [])


def _rw_prep_bwd(pr, mu, gate_pars, d_r, d_v, d_w, d_k2, d_kk, d_b, d_g):
    T = pr.shape[0]
    rows = [_cur(pr), (pr, PR_W, 0, "prev")] + [_cur(x) for x in (*d_r, *d_v, d_w, *d_k2, d_kk, d_b, d_g)]
    acc = [(1, PR_W)] + [tuple(p.shape) for p in gate_pars[:-2]]

    def fn(i, cur, prev8, dr1, dr2, dv1, dv2, dw, dk1, dk2, dkk, db, dg, mub, *gp):
        sh = _shift_rows(cur, prev8, i)
        rw = cur + mub * (sh - cur)
        _, vjp = jax.vjp(lambda x, *p: _rw_gates(x, *p, gp[-2], gp[-1]), rw, *gp[:-2])
        grads = vjp((dw, dk1 + dk2, dkk, db, dg))
        zeros = jnp.zeros((cur.shape[0], PR_W - 3 * RW_W), F32)
        drw = grads[0] + jnp.concatenate([dr1 + dr2, jnp.zeros_like(dr1), dv1 + dv2, zeros], axis=1)
        dmu = jnp.sum(drw * (sh - cur), axis=0, keepdims=True)
        return (drw,), (dmu, *grads[1:])

    return _rowwise(fn, "rw_prep_bwd", T, 128, rows, [mu, *gate_pars], [(PR_W, F32)], acc)


def _shift_bwd(drw, mu):
    T = drw.shape[0]
    tm = min(256, T)

    def fn(i, cur, next8, mub):
        last = jnp.where(i == T // tm - 1, 0.0, next8[0:1, :])
        rolled = pltpu.roll(cur, cur.shape[0] - 1, 0)
        nxt = jnp.where(lax.broadcasted_iota(jnp.int32, cur.shape, 0) == cur.shape[0] - 1, last, rolled)
        return ((1.0 - mub) * cur + mub * nxt,), ()

    return _rowwise(fn, "shift_bwd", T, tm, [_cur(drw), (drw, PR_W, 0, "next")], [mu], [(PR_W, BF16)], [])[0]


def _rw_out(y, r, v, k2, g, lnx_w, lnx_b, r_k, ind, ind_t):
    mean = _dot_hi(_dot_hi(y, ind) * (1.0 / RW_HD), ind_t)
    yc = y - mean
    var = _dot_hi(_dot_hi(yc * yc, ind) * (1.0 / RW_HD), ind_t)
    yn = yc * lax.rsqrt(var + GN_EPS) * lnx_w + lnx_b
    bonus = _dot_hi(_dot_hi(r * k2 * r_k, ind), ind_t) * v
    return (yn + bonus) * g


def _rw_post(y, rw, k2, g, pars):
    T = y.shape[0]
    fn = lambda i, *a: ((_rw_out(*a),), ())
    return _rowwise(fn, "rw_post", T, 256, [_cur(y), _cur(rw, RW_W, 0), _cur(rw, RW_W, 2), _cur(k2), _cur(g)], pars,
                    [(RW_W, BF16)], [])[0]


def _rw_post_bwd(y, rw, k2, g, pars, do):
    T = y.shape[0]

    def fn(i, yb, rb, vb, kb, gb, dob, lw, lb, rk, ind, ind_t):
        _, vjp = jax.vjp(lambda *a: _rw_out(*a, ind, ind_t), yb, rb, vb, kb, gb, lw, lb, rk)
        gr = vjp(dob)
        return gr[:5], gr[5:]

    return _rowwise(fn, "rw_post_bwd", T, 256,
                    [_cur(y), _cur(rw, RW_W, 0), _cur(rw, RW_W, 2), _cur(k2), _cur(g), _cur(do)], pars,
                    [(RW_W, F32)] * 5, [(1, RW_W)] * 3)


def _merge_bwd(dm, y_gla, y_rw, pg, gate_b):
    T = dm.shape[0]

    def fn(i, dmb, ya, yr, p1, p2, gb):
        g1 = _sigmoid(p1 + gb[:, :D])
        g2 = _sigmoid(p2 + gb[:, D:])
        dp1 = dmb * ya * g1 * (1.0 - g1)
        dp2 = dmb * yr * g2 * (1.0 - g2)
        dp = jnp.concatenate([dp1, dp2], axis=1)
        return (dmb * g1, dmb * g2, dp), (jnp.sum(dp, axis=0, keepdims=True),)

    return _rowwise(fn, "merge_bwd", T, 256, [_cur(dm), _cur(y_gla), _cur(y_rw), _cur(pg, D, 0), _cur(pg, D, 1)],
                    [gate_b], [(D, BF16), (D, BF16), (PG_W, BF16)], [(1, PG_W)])


_NN = (((1,), (0,)), ((), ()))
_NT = (((1,), (1,)), ((), ()))
_TN = (((0,), (0,)), ((), ()))


def _bdot(a, b, dims):
    return lax.dot_general(a.astype(BF16), b.astype(BF16), dims, preferred_element_type=F32)


def _accumulate(k, nk, acc, part, finish):
    if nk == 1:
        finish(part)
        return

    @pl.when(k == 0)
    def _():
        acc[...] = part

    @pl.when(k > 0)
    def _():
        acc[...] += part

    @pl.when(k == nk - 1)
    def _():
        finish(acc[...])


def _call(body, comm, name, grid, in_specs, out_specs, out_shape, scratch_shapes, sem, operands):
    if comm is None:
        return pl.pallas_call(body, name=name, grid=grid, in_specs=in_specs, out_specs=out_specs, out_shape=out_shape,
                              scratch_shapes=scratch_shapes, compiler_params=_params(sem))(*operands)
    kind, arrays = comm
    nc, n_in, n_out, n_scr = len(arrays), len(in_specs), len(out_shape), len(scratch_shapes)
    total = 1
    for n in grid:
        total *= n

    def with_comm(*refs):
        own = refs[:n_in] + refs[n_in + nc:n_in + nc + n_out] + refs[n_in + 2 * nc + n_out:n_in + 2 * nc + n_out + n_scr]
        c_in, c_out, sems = refs[n_in:n_in + nc], refs[n_in + nc + n_out:n_in + 2 * nc + n_out], refs[-3:]
        step = 0
        for axis, n in enumerate(grid):
            step = step * n + pl.program_id(axis)
        start, *forward, finish = _PLANS[kind][0](c_in, c_out, *sems)
        pl.when(step == 0)(start)
        for stage in forward:
            pl.when(step == total // 2)(stage)
        body(*own)
        pl.when(step == total - 1)(finish)

    return pl.pallas_call(
        with_comm, name=name, grid=grid, in_specs=list(in_specs) + [_ANY] * nc, out_specs=list(out_specs) + [_ANY] * nc,
        out_shape=list(out_shape) + [jax.ShapeDtypeStruct(_PLANS[kind][1](a.shape), a.dtype) for a in arrays],
        scratch_shapes=list(scratch_shapes) + _comm_sems(nc), compiler_params=_params(("arbitrary",) * len(grid)),
    )(*operands, *arrays)


def _matmul(a, b, mode, M, N, K, tm, tn, tk, name, a_off=(0, 0), b_off=(0, 0), res=None, scale=1.0, out_dtype=F32,
            comm=None):
    tm, tn, tk = min(tm, M), min(tn, N), min(tk, K)
    nk = K // tk
    if mode == "nn":
        a_spec = pl.BlockSpec((tm, tk), lambda i, j, k: (i + a_off[0], k + a_off[1]))
        b_spec = pl.BlockSpec((tk, tn), lambda i, j, k: (k + b_off[0], j + b_off[1]))
        dims = _NN
    elif mode == "nt":
        a_spec = pl.BlockSpec((tm, tk), lambda i, j, k: (i + a_off[0], k + a_off[1]))
        b_spec = pl.BlockSpec((tn, tk), lambda i, j, k: (j + b_off[0], k + b_off[1]))
        dims = _NT
    else:
        a_spec = pl.BlockSpec((tk, tm), lambda i, j, k: (k + a_off[0], i + a_off[1]))
        b_spec = pl.BlockSpec((tk, tn), lambda i, j, k: (k + b_off[0], j + b_off[1]))
        dims = _TN
    o_spec = pl.BlockSpec((tm, tn), lambda i, j, k: (i, j))

    def body(a_ref, b_ref, *rest):
        r_ref = rest[0] if res is not None else None
        o_ref = rest[1] if res is not None else rest[0]
        acc = rest[-1] if nk > 1 else None

        def finish(total):
            total = total * scale if scale != 1.0 else total
            if r_ref is not None:
                total = r_ref[...] + total
            o_ref[...] = total.astype(out_dtype)

        _accumulate(pl.program_id(2), nk, acc, _bdot(a_ref[...], b_ref[...], dims), finish)

    out = _call(body, comm, name, (M // tm, N // tn, nk), [a_spec, b_spec] + ([o_spec] if res is not None else []),
                [o_spec], [jax.ShapeDtypeStruct((M, N), out_dtype)], [pltpu.VMEM((tm, tn), F32)] if nk > 1 else [],
                ("parallel", "parallel", "arbitrary"), [a, b] + ([res] if res is not None else []))
    return out[0] if comm is None else out


def _ffn_up(h, wg, wu, name, comm=None):
    T = h.shape[0]
    tm = min(1024, T)

    def body(h_ref, wg_ref, wu_ref, a_ref, u_ref, s_ref):
        hb = h_ref[...]
        a = _bdot(hb, wg_ref[...], _NN)
        u = _bdot(hb, wu_ref[...], _NN)
        a_ref[...] = a
        u_ref[...] = u
        s_ref[...] = (_silu(a) * u).astype(BF16)

    w_spec = pl.BlockSpec((None, D, FSH), lambda i, j: (j, 0, 0))
    o_spec = pl.BlockSpec((None, tm, FSH), lambda i, j: (j, i, 0))
    sh = lambda dt: jax.ShapeDtypeStruct((NDEV, T, FSH), dt)
    return _call(body, comm, name, (T // tm, NDEV), [pl.BlockSpec((tm, D), lambda i, j: (i, 0)), w_spec, w_spec],
                 [o_spec] * 3, [sh(F32), sh(F32), sh(BF16)], [], ("parallel", "arbitrary"), [h, wg, wu])


def _ffn_down(s, wd, x, name, comm=None):
    T = x.shape[0]
    tm, tn, sh = min(1024, T), 1024, 4

    def body(s_ref, wd_ref, x_ref, o_ref, acc):
        part = _bdot(s_ref[0], wd_ref[0], _NN)
        for q in range(1, sh):
            part = part + _bdot(s_ref[q], wd_ref[q], _NN)

        def finish(total):
            o_ref[...] = x_ref[...] + 0.5 * total

        _accumulate(pl.program_id(2), NDEV // sh, acc, part, finish)

    xo = pl.BlockSpec((tm, tn), lambda i, n, j: (i, n))
    out = _call(body, comm, name, (T // tm, D // tn, NDEV // sh),
                [pl.BlockSpec((sh, tm, FSH), lambda i, n, j: (j, i, 0)),
                 pl.BlockSpec((sh, FSH, tn), lambda i, n, j: (j, 0, n)), xo],
                [xo], [jax.ShapeDtypeStruct((T, D), F32)], [pltpu.VMEM((tm, tn), F32)],
                ("parallel", "parallel", "arbitrary"), [s, wd, x])
    return out[0] if comm is None else out


def _ffn_bwd_hidden(dx, wd, a, u, name):
    T = dx.shape[0]
    tm = min(1024, T)

    def body(dx_ref, wd_ref, a_ref, u_ref, da_ref, du_ref):
        ds = 0.5 * _bdot(dx_ref[...], wd_ref[...], _NT)
        av = a_ref[...]
        sg = _sigmoid(av)
        da_ref[...] = (ds * u_ref[...] * (sg * (1.0 + av * (1.0 - sg)))).astype(BF16)
        du_ref[...] = (ds * (av * sg)).astype(BF16)

    act = pl.BlockSpec((None, tm, FSH), lambda i, j: (j, i, 0))
    sh = jax.ShapeDtypeStruct((NDEV, T, FSH), BF16)
    return pl.pallas_call(
        body, name=name, grid=(T // tm, NDEV),
        in_specs=[pl.BlockSpec((tm, D), lambda i, j: (i, 0)), pl.BlockSpec((None, FSH, D), lambda i, j: (j, 0, 0)),
                  act, act],
        out_specs=[act, act], out_shape=(sh, sh),
        compiler_params=_params(("parallel", "arbitrary")),
    )(dx, wd, a, u)


def _ffn_bwd_input(da, du, wg, wu, name, comm=None):
    T = da.shape[1]
    tm, tn, sh = min(1024, T), 1024, 2

    def body(da_ref, du_ref, wg_ref, wu_ref, o_ref, acc):
        part = _bdot(da_ref[0], wg_ref[0], _NT) + _bdot(du_ref[0], wu_ref[0], _NT)
        for q in range(1, sh):
            part = part + _bdot(da_ref[q], wg_ref[q], _NT) + _bdot(du_ref[q], wu_ref[q], _NT)

        def finish(total):
            o_ref[...] = total

        _accumulate(pl.program_id(2), NDEV // sh, acc, part, finish)

    act = pl.BlockSpec((sh, tm, FSH), lambda i, n, j: (j, i, 0))
    wsp = pl.BlockSpec((sh, tn, FSH), lambda i, n, j: (j, n, 0))
    out = _call(body, comm, name, (T // tm, D // tn, NDEV // sh), [act, act, wsp, wsp],
                [pl.BlockSpec((tm, tn), lambda i, n, j: (i, n))], [jax.ShapeDtypeStruct((T, D), F32)],
                [pltpu.VMEM((tm, tn), F32)], ("parallel", "parallel", "arbitrary"), [da, du, wg, wu])
    return out[0] if comm is None else out


def _ffn_grad_up(h, da, du, name, comm=None, core_major=False):
    T = h.shape[0]
    tm, tk = 1024, min(4096, T)
    nk = T // tk

    def body(h_ref, da_ref, du_ref, o_ref, acc_a, acc_u):
        k = pl.program_id(2)
        hb = h_ref[...]
        for acc, ref, slot in ((acc_a, da_ref, 0), (acc_u, du_ref, 1)):
            def finish(total, slot=slot):
                o_ref[slot] = total.astype(BF16)

            _accumulate(k, nk, acc, _bdot(hb, ref[...], _TN), finish)

    act = pl.BlockSpec((None, tk, FSH), lambda j, i, t: (j, t, 0))
    if core_major:
        o_spec = pl.BlockSpec((None, None, 2, tm, FSH), lambda j, i, t: (j % 2, j // 2, 0, i, 0))
        o_shape = jax.ShapeDtypeStruct((2, NDEV // 2, 2, D, FSH), BF16)
    else:
        o_spec = pl.BlockSpec((None, 2, tm, FSH), lambda j, i, t: (j, 0, i, 0))
        o_shape = jax.ShapeDtypeStruct((NDEV, 2, D, FSH), BF16)
    out = _call(body, comm, name, (NDEV, D // tm, nk), [pl.BlockSpec((tk, tm), lambda j, i, t: (t, i)), act, act],
                [o_spec], [o_shape],
                [pltpu.VMEM((tm, FSH), F32), pltpu.VMEM((tm, FSH), F32)], ("parallel", "parallel", "arbitrary"), [h, da, du])
    return out[0] if comm is None else out


def _ffn_grad_down(s, dx, name):
    T = dx.shape[0]
    tn, tk = 1024, min(2048, T)
    nk = T // tk

    def body(s_ref, dx_ref, o_ref, acc):
        def finish(total):
            o_ref[...] = (0.5 * total).astype(BF16)

        _accumulate(pl.program_id(2), nk, acc, _bdot(s_ref[...], dx_ref[...], _TN), finish)

    return pl.pallas_call(
        body, name=name, grid=(NDEV, D // tn, nk),
        in_specs=[pl.BlockSpec((None, tk, FSH), lambda j, n, t: (j, t, 0)), pl.BlockSpec((tk, tn), lambda j, n, t: (t, n))],
        out_specs=pl.BlockSpec((None, FSH, tn), lambda j, n, t: (j, 0, n)),
        out_shape=jax.ShapeDtypeStruct((NDEV, DFF // NDEV, D), BF16),
        scratch_shapes=[pltpu.VMEM((FSH, tn), F32)],
        compiler_params=_params(("parallel", "parallel", "arbitrary")),
    )(s, dx)


def _branch_merge(o_gla, o_rw, wb, pg, gate_b):
    T = o_gla.shape[0]
    tm, tn = min(1024, T), 512

    def body(og_ref, or_ref, w1_ref, w2_ref, p1_ref, p2_ref, b1_ref, b2_ref, yg_ref, yr_ref, m_ref):
        yg = _bdot(og_ref[...], w1_ref[...], _NN)
        yr = _bdot(or_ref[...], w2_ref[...], _NN)
        yg_ref[...] = yg
        yr_ref[...] = yr
        m_ref[...] = (_sigmoid(p1_ref[...] + b1_ref[...]) * yg + _sigmoid(p2_ref[...] + b2_ref[...]) * yr).astype(BF16)

    nj = D // tn
    act = pl.BlockSpec((tm, GLA_V), lambda i, j: (i, 0))
    out = pl.BlockSpec((tm, tn), lambda i, j: (i, j))
    return pl.pallas_call(
        body, name="branch_merge", grid=(T // tm, nj),
        in_specs=[act, act, pl.BlockSpec((GLA_V, tn), lambda i, j: (0, j)), pl.BlockSpec((RW_W, tn), lambda i, j: (1, j)),
                  out, pl.BlockSpec((tm, tn), lambda i, j: (i, nj + j)),
                  pl.BlockSpec((1, tn), lambda i, j: (0, j)), pl.BlockSpec((1, tn), lambda i, j: (0, nj + j))],
        out_specs=[out, out, out],
        out_shape=(jax.ShapeDtypeStruct((T, D), F32), jax.ShapeDtypeStruct((T, D), F32), jax.ShapeDtypeStruct((T, D), BF16)),
        compiler_params=_params(("parallel", "arbitrary")),
    )(o_gla, o_rw, wb, wb, pg, pg, gate_b, gate_b)


def _head_indicator(width, heads):
    col = lax.broadcasted_iota(jnp.int32, (width, 128), 0) // (width // heads)
    ind = (col == lax.broadcasted_iota(jnp.int32, (width, 128), 1)).astype(F32)
    return ind, ind.T


def _ffn_fwd(x, g, wg, wu, wd, tag):
    h = _rms_fwd(x, g, "rms_" + tag)
    a, u, s = _ffn_up(h, wg, wu, "ffn_up_" + tag)
    return _ffn_down(s, wd, x, "ffn_down_" + tag), (h, a, u, s)


def _ffn_fwd_gathering(x, g, wg, wu, wd_block, next_block, tag):
    h = _rms_fwd(x, g, "rms_" + tag)
    a, u, s, wd = _ffn_up(h, wg, wu, "ffn_up_" + tag, comm=("gather", [wd_block]))
    y, gathered = _ffn_down(s, wd, x, "ffn_down_" + tag, comm=("gather", [next_block]))
    return y, (h, a, u, s), wd, gathered


def _ffn_bwd(dy, x, g, wg, wu, wd, saved, tag, exchange=False):
    h, a, u, s = saved
    dwd = _ffn_grad_down(s, dy, "ffn_grad_down_" + tag)
    da, du = _ffn_bwd_hidden(dy, wd, a, u, "ffn_bwd_hidden_" + tag)
    if exchange:
        dw_up, dwd = _ffn_grad_up(h, da, du, "ffn_grad_up_" + tag, comm=("exchange", [dwd]), core_major=True)
        dh, dw_up = _ffn_bwd_input(da, du, wg, wu, "ffn_bwd_input_" + tag, comm=("quad", [_chip_sum(dw_up, "up_" + tag)]))
    else:
        dw_up = _ffn_grad_up(h, da, du, "ffn_grad_up_" + tag)
        dh = _ffn_bwd_input(da, du, wg, wu, "ffn_bwd_input_" + tag)
    dx, dg = _rms_bwd(x, g, dh, dy, "rms_bwd_" + tag)
    return dx, dg, dw_up, dwd


def _local_step(x, target, w, blocks):
    T = x.shape[0]
    ind16, ind16_t = _head_indicator(RW_W, RW_HEADS)
    ind4, ind4_t = _head_indicator(GLA_V, GLA_HEADS)
    ltri = jnp.tril(jnp.ones((CHUNK, CHUNK), F32))
    gate_pars = [w["w0"], w["w_w2"], w["a0"], w["w_a2"], w["w_g2"], w["k_k"], w["k_a"], ind16, ind16_t]
    post_pars = [w["lnx_w"], w["lnx_b"], w["r_k"], ind16, ind16_t]

    x1, ffn1, wd1, g_proj = _ffn_fwd_gathering(x, w["g1"], w["wg1"], w["wu1"], blocks["wd1"], blocks["win"], "1")
    win = _align_proj(_unshard_cols(g_proj))
    h2 = _rms_fwd(x1, w["g2"], "rms_mix")
    proj = lambda n, off, name: _matmul(h2, win, "nn", T, n, D, 1024, 512, D, name, b_off=(0, off // 512))
    pg = proj(PG_W, 0, "proj_gate")
    pr = proj(PR_W, PG_W, "proj_rwkv")
    pa = proj(PA_W, PG_W + PR_W, "proj_gla")
    la = _gla_prep(pa, w["gla_w_a2"], w["gla_b_a"])
    o_raw, gla_states = _gla_core_fwd(pa, la, ltri)
    o_gla = _gla_post(o_raw, pa, w["gn"], ind4, ind4_t)
    rw, dec, k2, kk, b, g = _rw_prep(pr, w["mu"], gate_pars)
    y, rw_states, rw_sa, g_up2, g_down2 = _rw_core_fwd(rw, dec, k2, kk, b, gather=blocks["late"])
    w = {**w, **_late_weights(g_up2, g_down2)}
    o_rw = _rw_post(y, rw, k2, g, post_pars)
    y_gla, y_rw, merged = _branch_merge(o_gla, o_rw, w["wb"], pg, w["gate_b"])
    x2 = _matmul(merged, w["wo"], "nn", T, D, D, 1024, 1024, D, "out_proj", res=x1)
    x3, ffn2 = _ffn_fwd(x2, w["g3"], w["wg2"], w["wu2"], w["wd2"], "2")
    dx3, loss, d_gf = _loss_bwd(x3, target, w["gf"])

    grads = {"gf": d_gf}
    dx2, grads["g3"], grads["up2"], grads["wd2"] = _ffn_bwd(
        dx3, x2, w["g3"], w["wg2"], w["wu2"], w["wd2"], ffn2, "2")
    dm = _matmul(dx2, w["wo"], "nt", T, D, D, 1024, 1024, D, "out_proj_bwd")
    grads["wo"] = _matmul(merged, dx2, "tn", D, D, T, 1024, 1024, 2048, "out_proj_grad", out_dtype=BF16)
    dy_gla, dy_rw, dpg, grads["gate_b"] = _merge_bwd(dm, y_gla, y_rw, pg, w["gate_b"])
    do_gla = _matmul(dy_gla, w["wb"], "nt", T, GLA_V, D, 1024, 1024, D, "branch_gla_bwd")
    do_rw = _matmul(dy_rw, w["wb"], "nt", T, RW_W, D, 1024, 1024, D, "branch_rwkv_bwd", b_off=(1, 0))
    grads["wb"] = jnp.concatenate([
        _matmul(o_gla, dy_gla, "tn", GLA_V, D, T, 1024, 1024, 4096, "branch_gla_grad", out_dtype=BF16),
        _matmul(o_rw, dy_rw, "tn", RW_W, D, T, 1024, 1024, 4096, "branch_rwkv_grad", out_dtype=BF16)], axis=0)
    dy, dr2, dv2, dk2b, dg, grads["lnx_w"], grads["lnx_b"], grads["r_k"] = _rw_post_bwd(y, rw, k2, g, post_pars, do_rw)
    early = _late_grad_parts(grads)
    received = {}
    dr1, dw, dk2a, dv1, dkk, db, received["up2"], received["down2"] = _rw_core_bwd(
        rw, dec, k2, kk, b, rw_states, rw_sa, dy, exchange=early)
    drw, grads["mu"], grads["w0"], grads["w_w2"], grads["a0"], grads["w_a2"], grads["w_g2"], grads["k_k"], grads["k_a"] = (
        _rw_prep_bwd(pr, w["mu"], gate_pars, (dr1, dr2), (dv1, dv2), dw, (dk2a, dk2b), dkk, db, dg))
    dpr = _shift_bwd(drw, w["mu"])
    do_raw, dr_gla, grads["gn"] = _gla_post_bwd(o_raw, pa, w["gn"], ind4, ind4_t, do_gla)
    dq, dk, dv, dla = _gla_core_bwd(pa, la, ltri, gla_states, do_raw)
    da_down, grads["gla_w_a2"], grads["gla_b_a"] = _gla_prep_bwd(pa, w["gla_w_a2"], w["gla_b_a"], dla)
    dpa = jnp.concatenate([dq.astype(BF16), dk.astype(BF16), dv.astype(BF16), dr_gla, da_down,
                           jnp.zeros((T, PA_W - PA_USED), BF16)], axis=1)
    dp = jnp.concatenate([dpg, dpr, dpa], axis=1)
    d_win = _matmul(h2, dp, "tn", D, DIN_P, T, 1024, 1024, 4096, "proj_grad", out_dtype=BF16)
    d_win = _shard_cols(_unalign_proj(d_win)).reshape(NDEV // 2, 2, D, DIN_SH).swapaxes(0, 1)
    dh2, received["win"] = _matmul(dp, win, "nt", T, D, DIN_P, 1024, 1024, DIN_P // 4, "proj_bwd",
                                   comm=("quad", [_chip_sum(d_win, "win")]))
    dx1, grads["g2"] = _rms_bwd(x1, w["g2"], dh2, dx2, "rms_bwd_mix")
    dx, grads["g1"], received["up1"], received["wd1"] = _ffn_bwd(
        dx1, x, w["g1"], w["wg1"], w["wu1"], wd1, ffn1, "1", exchange=True)
    return loss, dx, grads, received


BIG = ("ffn1_wg", "ffn1_wu", "ffn1_wd", "w_in", "w_branch", "w_out", "ffn2_wg", "ffn2_wu", "ffn2_wd")
SMALL_SHARDED = ("gla_w_a2", "rwkv_w_w2", "rwkv_w_a2", "rwkv_w_g2")
REPLICATED = ("ffn1_norm", "mix_norm", "gla_b_a", "gla_gn_w", "rwkv_mu", "rwkv_w0", "rwkv_a0", "rwkv_k_k", "rwkv_k_a",
              "rwkv_r_k", "rwkv_lnx_w", "rwkv_lnx_b", "gate_b", "ffn2_norm", "final_norm")
WEIGHTS = ("ffn1_norm", "ffn1_wg", "ffn1_wu", "ffn1_wd", "mix_norm", "w_in", "gla_w_a2", "gla_b_a", "gla_gn_w",
           "rwkv_mu", "rwkv_w0", "rwkv_w_w2", "rwkv_a0", "rwkv_w_a2", "rwkv_w_g2", "rwkv_k_k", "rwkv_k_a", "rwkv_r_k",
           "rwkv_lnx_w", "rwkv_lnx_b", "gate_b", "w_branch", "w_out", "ffn2_norm", "ffn2_wg", "ffn2_wu", "ffn2_wd",
           "final_norm")


def _unshard_cols(g):
    return jnp.transpose(g, (1, 0, 2)).reshape(g.shape[1], NDEV * g.shape[2])


def _shard_cols(a):
    return jnp.transpose(a.reshape(a.shape[0], NDEV, a.shape[1] // NDEV), (1, 0, 2))


def _pad_rows(a, rows):
    return jnp.pad(a, ((0, rows - a.shape[0]), (0, 0)))


def _align_rw(a):
    c = 3 * RW_W
    z = jnp.zeros((a.shape[0], LORA_P - DECAY_LORA), a.dtype)
    return jnp.concatenate([a[:, :c], a[:, c:c + DECAY_LORA], z, a[:, c + DECAY_LORA:c + 2 * DECAY_LORA], z,
                            a[:, c + 2 * DECAY_LORA:]], axis=1)


def _unalign_rw(a):
    c = 3 * RW_W
    return jnp.concatenate([a[:, :c + DECAY_LORA], a[:, c + LORA_P:c + LORA_P + AAA_LORA], a[:, c + 2 * LORA_P:]], axis=1)


def _align_proj(a):
    gla = jnp.pad(a[:, :GLA_IN], ((0, 0), (0, PA_W - GLA_IN)))
    return jnp.concatenate([a[:, GLA_IN + RW_IN:], _align_rw(a[:, GLA_IN:GLA_IN + RW_IN]), gla], axis=1)


def _unalign_proj(a):
    return jnp.concatenate([a[:, PG_W + PR_W:PG_W + PR_W + GLA_IN], _unalign_rw(a[:, PG_W:PG_W + PR_W]), a[:, :PG_W]], axis=1)


def _layout_weights(gb, gs, rep):
    row = lambda n: rep[n].reshape(1, -1)
    return {
        "wg1": gb["ffn1_wg"], "wu1": gb["ffn1_wu"],
        "g1": row("ffn1_norm"), "g2": row("mix_norm"), "g3": row("ffn2_norm"), "gf": row("final_norm"),
        "gla_w_a2": _pad_rows(_unshard_cols(gs["gla_w_a2"]), LORA_P), "gla_b_a": row("gla_b_a"),
        "gn": jnp.tile(row("gla_gn_w"), (1, GLA_HEADS)),
        "mu": _align_rw(row("rwkv_mu")), "w0": row("rwkv_w0"), "a0": row("rwkv_a0"),
        "w_w2": _pad_rows(_unshard_cols(gs["rwkv_w_w2"]), LORA_P),
        "w_a2": _pad_rows(_unshard_cols(gs["rwkv_w_a2"]), LORA_P),
        "w_g2": _unshard_cols(gs["rwkv_w_g2"]),
        "k_k": row("rwkv_k_k"), "k_a": row("rwkv_k_a"), "r_k": row("rwkv_r_k"),
        "lnx_w": row("rwkv_lnx_w"), "lnx_b": row("rwkv_lnx_b"), "gate_b": row("gate_b"),
    }


LATE_ROWS = (("ffn2_wd", FSH), ("w_branch", (GLA_V + RW_W) // NDEV), ("w_out", D // NDEV))


def _late_weights(g_up, g_down):
    r1, r2 = LATE_ROWS[0][1], LATE_ROWS[0][1] + LATE_ROWS[1][1]
    return {"wg2": g_up[:, 0], "wu2": g_up[:, 1], "wd2": g_down[:, :r1],
            "wb": g_down[:, r1:r2].reshape(GLA_V + RW_W, D), "wo": g_down[:, r2:].reshape(D, D)}


def _late_grad_parts(g):
    return [g["up2"], jnp.concatenate([g["wd2"], g["wb"].reshape(NDEV, -1, D), g["wo"].reshape(NDEV, -1, D)], axis=1)]


def _layout_grads(g):
    return {
        "ffn1_norm": g["g1"], "mix_norm": g["g2"], "ffn2_norm": g["g3"], "final_norm": g["gf"],
        "gla_w_a2": g["gla_w_a2"][:GLA_LORA], "gla_b_a": g["gla_b_a"],
        "gla_gn_w": jnp.sum(g["gn"].reshape(GLA_HEADS, GLA_DV), axis=0, keepdims=True),
        "rwkv_mu": _unalign_rw(g["mu"]), "rwkv_w0": g["w0"], "rwkv_a0": g["a0"],
        "rwkv_w_w2": g["w_w2"][:DECAY_LORA], "rwkv_w_a2": g["w_a2"][:AAA_LORA], "rwkv_w_g2": g["w_g2"],
        "rwkv_k_k": g["k_k"], "rwkv_k_a": g["k_a"], "rwkv_r_k": g["r_k"],
        "rwkv_lnx_w": g["lnx_w"], "rwkv_lnx_b": g["lnx_b"], "gate_b": g["gate_b"],
    }


_MESH = pl.DeviceIdType.MESH
_ANY = pl.BlockSpec(memory_space=pl.ANY)


def _position():
    return lax.axis_index("x"), lax.axis_index("y"), lax.axis_index("c")


def _slot(p):
    return 4 * p[0] + 2 * p[1] + p[2]


def _comm_sems(n):
    if not n:
        return []
    return [pltpu.SemaphoreType.DMA((7 * n,)), pltpu.SemaphoreType.DMA((7 * n,)), pltpu.SemaphoreType.DMA((n,))]


def _gather_plan(ins, outs, send_sems, recv_sems, local_sems):
    n = len(ins)
    x, y, c = _position()
    me, sibling = (x, y, c), (x, y, 1 - c)
    chips = [(1 - x, y), (x, 1 - y), (1 - x, 1 - y)]

    def copy(a, k, block, to, src=None):
        dst = outs[a].at[_slot(block)]
        return pltpu.make_async_remote_copy(
            src_ref=dst if src is None else src, dst_ref=dst, send_sem=send_sems.at[7 * a + k],
            recv_sem=recv_sems.at[7 * a + k], device_id=to, device_id_type=_MESH)

    def local(a):
        return pltpu.make_async_copy(ins[a], outs[a].at[_slot(me)], local_sems.at[a])

    def own(a):
        return [copy(a, 0, me, sibling, src=ins[a])] + [copy(a, 1 + j, me, (*chip, c), src=ins[a]) for j, chip in enumerate(chips)]

    def start():
        for a in range(n):
            local(a).start()
            for cp in own(a):
                cp.start()

    def forward():
        for a in range(n):
            for j, chip in enumerate(chips):
                copy(a, 1 + j, (*chip, c), me).wait_recv()
                copy(a, 4 + j, (*chip, c), sibling).start()

    def finish():
        for a in range(n):
            copy(a, 0, sibling, me).wait_recv()
            for j, chip in enumerate(chips):
                copy(a, 4 + j, (*chip, 1 - c), me).wait_recv()
        for a in range(n):
            for cp in own(a) + [copy(a, 4 + j, (*chip, c), sibling) for j, chip in enumerate(chips)]:
                cp.wait_send()
            local(a).wait()

    return start, forward, finish


def _exchange_plan(ins, outs, send_sems, recv_sems, local_sems):
    n = len(ins)
    x, y, c = _position()
    me = (x, y, c)
    flip = lambda v, f: 1 - v if f else v
    peers = [(flip(x, fx), flip(y, fy), flip(c, fc))
             for fx, fy, fc in ((0, 0, 1), (1, 0, 0), (0, 1, 0), (1, 1, 0), (1, 0, 1), (0, 1, 1), (1, 1, 1))]

    def copy(a, k, src_slot, dst_slot):
        return pltpu.make_async_remote_copy(
            src_ref=ins[a].at[src_slot], dst_ref=outs[a].at[dst_slot], send_sem=send_sems.at[7 * a + k],
            recv_sem=recv_sems.at[7 * a + k], device_id=peers[k], device_id_type=_MESH)

    def local(a):
        return pltpu.make_async_copy(ins[a].at[_slot(me)], outs[a].at[_slot(me)], local_sems.at[a])

    def start():
        for a in range(n):
            local(a).start()
            for k, peer in enumerate(peers):
                copy(a, k, _slot(peer), _slot(me)).start()

    def finish():
        for a in range(n):
            for k, peer in enumerate(peers):
                copy(a, k, _slot(peer), _slot(peer)).wait_recv()
        for a in range(n):
            for k, peer in enumerate(peers):
                copy(a, k, _slot(peer), _slot(me)).wait_send()
            local(a).wait()

    return start, finish


def _sibling_plan(ins, outs, send_sems, recv_sems, local_sems):
    x, y, c = _position()

    def copy(a):
        return pltpu.make_async_remote_copy(
            src_ref=ins[a].at[1 - c], dst_ref=outs[a], send_sem=send_sems.at[7 * a], recv_sem=recv_sems.at[7 * a],
            device_id=(x, y, 1 - c), device_id_type=_MESH)

    def start():
        for a in range(len(ins)):
            copy(a).start()

    def finish():
        for a in range(len(ins)):
            copy(a).wait()

    return start, finish


def _quad_plan(ins, outs, send_sems, recv_sems, local_sems):
    n = len(ins)
    x, y, c = _position()
    mine = 2 * x + y
    peers = [(1 - x, y), (x, 1 - y), (1 - x, 1 - y)]

    def copy(a, k, src_slot, dst_slot):
        return pltpu.make_async_remote_copy(
            src_ref=ins[a].at[src_slot], dst_ref=outs[a].at[dst_slot], send_sem=send_sems.at[7 * a + k],
            recv_sem=recv_sems.at[7 * a + k], device_id=(*peers[k], c), device_id_type=_MESH)

    def local(a):
        return pltpu.make_async_copy(ins[a].at[mine], outs[a].at[mine], local_sems.at[a])

    def start():
        for a in range(n):
            local(a).start()
            for k, (px, py) in enumerate(peers):
                copy(a, k, 2 * px + py, mine).start()

    def finish():
        for a in range(n):
            for k, (px, py) in enumerate(peers):
                copy(a, k, 2 * px + py, 2 * px + py).wait_recv()
        for a in range(n):
            for k, (px, py) in enumerate(peers):
                copy(a, k, 2 * px + py, mine).wait_send()
            local(a).wait()

    return start, finish


_PLANS = {"gather": (_gather_plan, lambda s: (NDEV,) + s), "exchange": (_exchange_plan, lambda s: s),
          "sibling": (_sibling_plan, lambda s: s[1:]), "quad": (_quad_plan, lambda s: s)}


def _exchange_now(kind, arrays, name):
    n = len(arrays)

    def body(*refs):
        for stage in _PLANS[kind][0](refs[:n], refs[n:2 * n], *refs[2 * n:]):
            stage()

    return pl.pallas_call(
        body, name=name, in_specs=[_ANY] * n, out_specs=[_ANY] * n,
        out_shape=[jax.ShapeDtypeStruct(_PLANS[kind][1](a.shape), a.dtype) for a in arrays], scratch_shapes=_comm_sems(n),
    )(*arrays)


def _chip_sum(parts, name):
    c = lax.axis_index("c")
    (theirs,) = _exchange_now("sibling", [parts], "chip_send_" + name)
    mine = lax.dynamic_index_in_dim(parts, c, axis=0, keepdims=False)
    shape = mine.shape
    rows = shape[-2]
    for d in shape[1:-2]:
        rows *= d
    flat = (4, rows, shape[-1])
    tr = 512

    def body(a_ref, b_ref, o_ref):
        o_ref[...] = (a_ref[...].astype(F32) + b_ref[...].astype(F32)).astype(BF16)

    blk = pl.BlockSpec((None, tr, shape[-1]), lambda s, r: (s, r, 0))
    out = pl.pallas_call(
        body, name="chip_sum_" + name, grid=(4, rows // tr), in_specs=[blk, blk], out_specs=blk,
        out_shape=jax.ShapeDtypeStruct(flat, BF16), compiler_params=_params(("parallel", "parallel")),
    )(mine.reshape(flat), theirs.reshape(flat))
    return out.reshape(shape)


def _adamw_math(w, g, m, v):
    m = ADAM_B1 * m + (1.0 - ADAM_B1) * g
    v = ADAM_B2 * v + (1.0 - ADAM_B2) * (g * g)
    m_hat = m / (1.0 - ADAM_B1 ** ADAM_STEP)
    v_hat = v / (1.0 - ADAM_B2 ** ADAM_STEP)
    delta = -ADAM_LR * (m_hat / (jnp.sqrt(v_hat) + ADAM_EPS) + ADAM_WD * w)
    return delta, m, v


def _sum_slots(ref):
    total = ref[0].astype(F32)
    for s in range(1, ref.shape[0]):
        total = total + ref[s].astype(F32)
    return total


def _adamw(parts, w, m, v, tr, name, stack_index=None, row_block_offset=0):
    _, R, C = w.shape

    def body(p_ref, w_ref, m_ref, v_ref, g_ref, d_ref, nm_ref, nv_ref):
        g = _sum_slots(p_ref)
        g_ref[...] = g
        d_ref[...], nm_ref[...], nv_ref[...] = _adamw_math(w_ref[...], g, m_ref[...], v_ref[...])

    if stack_index is None:
        p_spec = pl.BlockSpec((parts.shape[0], tr, C), lambda r: (0, row_block_offset + r, 0))
    else:
        p_spec = pl.BlockSpec((parts.shape[0], None, tr, C), lambda r: (0, stack_index, r, 0))
    blk = pl.BlockSpec((None, tr, C), lambda r: (0, r, 0))
    out = jax.ShapeDtypeStruct((1, R, C), F32)
    return pl.pallas_call(
        body, name=name, grid=(R // tr,), in_specs=[p_spec, blk, blk, blk], out_specs=[blk] * 4, out_shape=(out,) * 4,
        compiler_params=_params(("parallel",)),
    )(parts, w, m, v)


def _sum_gathered(parts):
    _, R, C = parts.shape

    def body(p_ref, o_ref):
        o_ref[...] = _sum_slots(p_ref)

    return pl.pallas_call(body, name="small_grad_sum", out_shape=jax.ShapeDtypeStruct((R, C), F32),
                          compiler_params=_params())(parts)


def _adamw_small(w, g, m, v):
    def body(w_ref, g_ref, m_ref, v_ref, d_ref, nm_ref, nv_ref):
        d_ref[...], nm_ref[...], nv_ref[...] = _adamw_math(w_ref[...], g_ref[...], m_ref[...], v_ref[...])

    out = jax.ShapeDtypeStruct(w.shape, F32)
    return pl.pallas_call(body, name="adamw_small", out_shape=(out,) * 3, compiler_params=_params())(w, g, m, v)


def _pack(pieces, rows):
    flat = jnp.concatenate([p.reshape(-1) for p in pieces])
    return jnp.pad(flat, (0, rows * 128 - flat.shape[0])).reshape(rows, 128)


def _unpack(packed, shapes):
    flat = packed.reshape(-1)
    out, off = [], 0
    for s in shapes:
        size = 1
        for d in s:
            size *= d
        out.append(flat[off:off + size].reshape(s))
        off += size
    return out


def _rows_for(shapes, extra=0):
    total = extra
    for s in shapes:
        size = 1
        for d in s:
            size *= d
        total += size
    return -(-total // 1024) * 8


def kernel(x, ffn1_norm, ffn1_wg, ffn1_wu, ffn1_wd, mix_norm, w_in, gla_w_a2, gla_b_a, gla_gn_w, rwkv_mu, rwkv_w0, rwkv_w_w2, rwkv_a0, rwkv_w_a2, rwkv_w_g2, rwkv_k_k, rwkv_k_a, rwkv_r_k, rwkv_lnx_w, rwkv_lnx_b, gate_b, w_branch, w_out, ffn2_norm, ffn2_wg, ffn2_wu, ffn2_wd, final_norm, loss_target, m_ffn1_norm, m_ffn1_wg, m_ffn1_wu, m_ffn1_wd, m_mix_norm, m_w_in, m_gla_w_a2, m_gla_b_a, m_gla_gn_w, m_rwkv_mu, m_rwkv_w0, m_rwkv_w_w2, m_rwkv_a0, m_rwkv_w_a2, m_rwkv_w_g2, m_rwkv_k_k, m_rwkv_k_a, m_rwkv_r_k, m_rwkv_lnx_w, m_rwkv_lnx_b, m_gate_b, m_w_branch, m_w_out, m_ffn2_norm, m_ffn2_wg, m_ffn2_wu, m_ffn2_wd, m_final_norm, v_ffn1_norm, v_ffn1_wg, v_ffn1_wu, v_ffn1_wd, v_mix_norm, v_w_in, v_gla_w_a2, v_gla_b_a, v_gla_gn_w, v_rwkv_mu, v_rwkv_w0, v_rwkv_w_w2, v_rwkv_a0, v_rwkv_w_a2, v_rwkv_w_g2, v_rwkv_k_k, v_rwkv_k_a, v_rwkv_r_k, v_rwkv_lnx_w, v_rwkv_lnx_b, v_gate_b, v_w_branch, v_w_out, v_ffn2_norm, v_ffn2_wg, v_ffn2_wu, v_ffn2_wd, v_final_norm):
    wts = dict(zip(WEIGHTS, (ffn1_norm, ffn1_wg, ffn1_wu, ffn1_wd, mix_norm, w_in, gla_w_a2, gla_b_a, gla_gn_w, rwkv_mu, rwkv_w0, rwkv_w_w2, rwkv_a0, rwkv_w_a2, rwkv_w_g2, rwkv_k_k, rwkv_k_a, rwkv_r_k, rwkv_lnx_w, rwkv_lnx_b, gate_b, w_branch, w_out, ffn2_norm, ffn2_wg, ffn2_wu, ffn2_wd, final_norm)))
    mom = dict(zip(WEIGHTS, (m_ffn1_norm, m_ffn1_wg, m_ffn1_wu, m_ffn1_wd, m_mix_norm, m_w_in, m_gla_w_a2, m_gla_b_a, m_gla_gn_w, m_rwkv_mu, m_rwkv_w0, m_rwkv_w_w2, m_rwkv_a0, m_rwkv_w_a2, m_rwkv_w_g2, m_rwkv_k_k, m_rwkv_k_a, m_rwkv_r_k, m_rwkv_lnx_w, m_rwkv_lnx_b, m_gate_b, m_w_branch, m_w_out, m_ffn2_norm, m_ffn2_wg, m_ffn2_wu, m_ffn2_wd, m_final_norm)))
    var = dict(zip(WEIGHTS, (v_ffn1_norm, v_ffn1_wg, v_ffn1_wu, v_ffn1_wd, v_mix_norm, v_w_in, v_gla_w_a2, v_gla_b_a, v_gla_gn_w, v_rwkv_mu, v_rwkv_w0, v_rwkv_w_w2, v_rwkv_a0, v_rwkv_w_a2, v_rwkv_w_g2, v_rwkv_k_k, v_rwkv_k_a, v_rwkv_r_k, v_rwkv_lnx_w, v_rwkv_lnx_b, v_gate_b, v_w_branch, v_w_out, v_ffn2_norm, v_ffn2_wg, v_ffn2_wu, v_ffn2_wd, v_final_norm)))
    two = lambda a: a.reshape(a.shape[-2:])

    bf = lambda n: two(wts[n]).astype(BF16)
    up1 = jnp.stack([bf("ffn1_wg"), bf("ffn1_wu")])
    lora = jnp.concatenate([jnp.pad(two(gla_w_a2), ((0, 0), (0, 128 - GLA_QK // NDEV)))] +
                           [two(wts[n]) for n in SMALL_SHARDED[1:]], axis=0)
    g_up1, g_lora = _exchange_now("gather", [up1, lora], "gather_weights")
    gb = {"ffn1_wg": g_up1[:, 0], "ffn1_wu": g_up1[:, 1]}
    gs = {"gla_w_a2": g_lora[:, :GLA_LORA, :GLA_QK // NDEV]}
    row = GLA_LORA
    for n in SMALL_SHARDED[1:]:
        gs[n] = g_lora[:, row:row + wts[n].shape[1]]
        row += wts[n].shape[1]
    w = _layout_weights(gb, gs, {n: wts[n] for n in REPLICATED})
    blocks = {"wd1": bf("ffn1_wd"), "win": bf("w_in"),
              "late": [jnp.stack([bf("ffn2_wg"), bf("ffn2_wu")]), jnp.concatenate([bf(n) for n, _ in LATE_ROWS], axis=0)]}

    loss_part, grad_x, grads, parts = _local_step(x[0], loss_target[0], w, blocks)
    small = _layout_grads(grads)
    result = {}
    state = lambda n: (wts[n], mom[n], var[n])
    for group, names in (("up1", ("ffn1_wg", "ffn1_wu")), ("up2", ("ffn2_wg", "ffn2_wu"))):
        for i, n in enumerate(names):
            result[n] = _adamw(parts[group], *state(n), 256, "adamw_" + n, stack_index=i)
    result["ffn1_wd"] = _adamw(parts["wd1"], *state("ffn1_wd"), 64, "adamw_ffn1_wd")
    row = 0
    for n, rows in LATE_ROWS:
        result[n] = _adamw(parts["down2"], *state(n), 64, "adamw_" + n, row_block_offset=row // 64)
        row += rows
    result["w_in"] = _adamw(parts["win"], *state("w_in"), 256, "adamw_w_in")

    small_names = [n for n in WEIGHTS if n not in BIG]
    full_shapes = [small[n].shape for n in small_names]
    rows_full = _rows_for(full_shapes, extra=128)
    packed = _pack([small[n] for n in small_names] + [loss_part], rows_full)
    (gathered,) = _exchange_now("gather", [packed], "gather_small_grads")
    total = _sum_gathered(gathered)
    *full_grads, loss_row = _unpack(total, full_shapes + [(1, 128)])
    me = _slot(_position())
    own = {}
    for n, g in zip(small_names, full_grads):
        if n in SMALL_SHARDED:
            cols = wts[n].shape[-1]
            g = lax.dynamic_slice_in_dim(g, me * cols, cols, axis=1)
        own[n] = g.reshape(wts[n].shape)
    own_shapes = [wts[n].shape for n in small_names]
    rows_own = _rows_for(own_shapes)
    pk = lambda d: _pack([d[n] for n in small_names], rows_own)
    d_s, m_s, v_s = _adamw_small(pk(wts), pk(own), pk(mom), pk(var))
    for n, d, m, v in zip(small_names, _unpack(d_s, own_shapes), _unpack(m_s, own_shapes), _unpack(v_s, own_shapes)):
        result[n] = (own[n], d, m, v)

    shaped = lambda n, k: result[n][k].reshape(wts[n].shape)
    return (loss_row[0, 0], grad_x[None],
            *[shaped(n, 0) for n in WEIGHTS], *[shaped(n, 1) for n in WEIGHTS],
            *[shaped(n, 2) for n in WEIGHTS], *[shaped(n, 3) for n in WEIGHTS])
```

```python
import functools

import jax
import jax.numpy as jnp
from jax import lax
from jax.experimental import pallas as pl
from jax.experimental.pallas import tpu as pltpu

F32 = jnp.float32
BF16 = jnp.bfloat16
HI = lax.Precision.HIGHEST

NDEV = 8
D = 2048
DFF = 5632
FSH = DFF // NDEV
CHUNK = 64
GLA_HEADS, GLA_DK, GLA_DV = 4, 128, 256
GLA_QK, GLA_V, GLA_LORA, GLA_TAU = 512, 1024, 16, 16.0
RW_HEADS, RW_HD, RW_W = 16, 64, 1024
DECAY_LORA, AAA_LORA, GATE_LORA = 96, 96, 256
GN_EPS = 64e-5
NORM_EPS = 1e-6
GLA_IN = 2 * GLA_QK + 2 * GLA_V + GLA_LORA
RW_IN = 3 * RW_W + DECAY_LORA + AAA_LORA + GATE_LORA
D_IN = GLA_IN + RW_IN + 2 * D
DIN_SH = D_IN // NDEV
PG_W = 2 * D
PR_W = 3584
PA_W = 3584
PA_USED = 2 * GLA_QK + 2 * GLA_V + 128
DIN_P = PG_W + PR_W + PA_W
LORA_P = 128

ADAM_LR, ADAM_B1, ADAM_B2, ADAM_EPS, ADAM_WD, ADAM_STEP = 0.001, 0.9, 0.999, 1e-08, 0.01, 10

VMEM_LIMIT = 56 * 1024 * 1024
RW_TB = 32
RW_G = 16
RW_NP = 8


def _params(sem=None, vmem=VMEM_LIMIT):
    return pltpu.CompilerParams(dimension_semantics=sem, vmem_limit_bytes=vmem)


def _pair_mask():
    return lax.broadcasted_iota(jnp.int32, (RW_HD, 2 * RW_HD), 1) < RW_HD


def _pair_rowsum(p, mask):
    tot = jnp.sum(p, axis=1, keepdims=True)
    first = jnp.sum(jnp.where(mask, p, 0.0), axis=1, keepdims=True)
    return first, tot - first


def _split_transposed(x_ref, q, dst_ref, base):
    xt = x_ref[:, 128 * q:128 * (q + 1)].T
    for g in range(RW_TB // RW_G):
        dst_ref[base + g, :, 0:RW_G] = xt[:, g * RW_G:(g + 1) * RW_G]


def _pair_column(tile_ref, idx, i, mask):
    return jnp.where(mask, tile_ref[idx, 0:RW_HD, i:i + 1], tile_ref[idx, RW_HD:, i:i + 1])


def _rw_core_fwd(rw, w, k2, kk, b, gather=()):
    T = rw.shape[0]
    nb = T // RW_TB
    ng = RW_TB // RW_G
    NP = RW_NP
    nc = len(gather)
    npair = RW_HEADS // 2 // NP

    def body(r_ref, v_ref, w_ref, k_ref, kk_ref, b_ref, *rest):
        g_in, (y_ref, st_ref, sa_ref), g_out = rest[:nc], rest[nc:nc + 3], rest[nc + 3:2 * nc + 3]
        s_scr, vt_scr, yt_scr, rows_scr = rest[2 * nc + 3:2 * nc + 7]
        pair, blk_i = pl.program_id(0), pl.program_id(1)
        if nc:
            start, forward, finish = _gather_plan(g_in, g_out, *rest[2 * nc + 7:])
            pl.when((pair == 0) & (blk_i == 0))(start)
            pl.when((pair == 0) & (blk_i == nb // 2))(forward)

        @pl.when(pl.program_id(1) == 0)
        def _():
            s_scr[...] = jnp.zeros_like(s_scr)
            yt_scr[...] = jnp.zeros_like(yt_scr)

        mask = _pair_mask()
        for q in range(NP):
            _split_transposed(v_ref, q, vt_scr, q * ng)
        R_, W_, K_, KK_, B_ = range(5)
        for a, ref in enumerate((r_ref, w_ref, k_ref, kk_ref, b_ref)):
            for q in range(NP):
                rows_scr[a * NP + q] = ref[:, 128 * q:128 * (q + 1)]

        def group(g, states):
            states = list(states)
            for i in range(RW_G):
                t = g * RW_G + i
                row = lambda a, q: rows_scr[a * NP + q, pl.ds(t, 1), :]
                sums = [_pair_rowsum(states[q] * row(KK_, q), mask) for q in range(NP)]
                for q in range(NP):
                    sa = jnp.where(mask, *sums[q])
                    sa_ref[q, t] = sa
                    states[q] = (states[q] * row(W_, q) - sa * row(B_, q)
                                 + _pair_column(vt_scr, q * ng + g, i, mask) * row(K_, q))
                    st_ref[q, t] = states[q]
                outs = [_pair_rowsum(states[q] * row(R_, q), mask) for q in range(NP)]
                for q in range(NP):
                    yt_scr[q * ng + g, 0:RW_HD, i:i + 1] = outs[q][0]
                    yt_scr[q * ng + g, RW_HD:, i:i + 1] = outs[q][1]
            return tuple(states)

        states = lax.fori_loop(0, ng, group, tuple(s_scr[q] for q in range(NP)))
        for q in range(NP):
            s_scr[q] = states[q]
            for g in range(ng):
                y_ref[g * RW_G:(g + 1) * RW_G, 128 * q:128 * (q + 1)] = yt_scr[q * ng + g].T[0:RW_G, :]
        if nc:
            pl.when((pair == npair - 1) & (blk_i == nb - 1))(finish)

    blk = lambda cb: pl.BlockSpec((RW_TB, 128 * NP), lambda p, i, cb=cb: (i, cb + p))
    tiles = pltpu.VMEM((NP * ng, 128, 128), F32)
    return pl.pallas_call(
        body, name="rw_core_fwd", grid=(npair, nb),
        in_specs=[blk(0), blk(2 * RW_W // (128 * NP)), blk(0), blk(0), blk(0), blk(0)] + [_ANY] * nc,
        out_specs=[blk(0)] + [pl.BlockSpec((NP, RW_TB, RW_HD, 128), lambda p, i: (p, i, 0, 0))] * 2 + [_ANY] * nc,
        out_shape=[jax.ShapeDtypeStruct((T, RW_W), F32)] + [jax.ShapeDtypeStruct((RW_HEADS // 2, T, RW_HD, 128), F32)] * 2
        + [jax.ShapeDtypeStruct((NDEV,) + a.shape, a.dtype) for a in gather],
        scratch_shapes=[pltpu.VMEM((NP, RW_HD, 128), F32), tiles, tiles, pltpu.VMEM((5 * NP, RW_TB, 128), F32)]
        + _comm_sems(nc),
        compiler_params=_params(("arbitrary", "arbitrary")),
    )(rw, rw, w, k2, kk, b, *gather)


def _rw_core_bwd(rw, w, k2, kk, b, states, sa_tiles, dy, exchange=()):
    T = rw.shape[0]
    nb = T // RW_TB
    ng = RW_TB // RW_G
    NP = RW_NP
    nc = len(exchange)
    npair = RW_HEADS // 2 // NP

    def body(r_ref, v_ref, w_ref, k_ref, kk_ref, b_ref, dy_ref, st_ref, sp_ref, sa_ref, *rest):
        e_in, e_out = rest[:nc], rest[nc + 6:2 * nc + 6]
        dr_ref, dw_ref, dk_ref, dv_ref, dkk_ref, db_ref = rest[nc:nc + 6]
        ds_scr, vt_scr, dyt_scr, dvt_scr, rows_scr, out_scr = rest[2 * nc + 6:2 * nc + 12]
        step = pl.program_id(1)
        if nc:
            start, finish = _exchange_plan(e_in, e_out, *rest[2 * nc + 12:])
            pl.when((pl.program_id(0) == 0) & (step == 0))(start)

        @pl.when(step == 0)
        def _():
            ds_scr[...] = jnp.zeros_like(ds_scr)
            dvt_scr[...] = jnp.zeros_like(dvt_scr)

        mask = _pair_mask()
        for q in range(NP):
            _split_transposed(v_ref, q, vt_scr, q * ng)
            _split_transposed(dy_ref, q, dyt_scr, q * ng)
        R_, W_, K_, KK_, B_ = range(5)
        for a, ref in enumerate((r_ref, w_ref, k_ref, kk_ref, b_ref)):
            for q in range(NP):
                rows_scr[a * NP + q] = ref[:, 128 * q:128 * (q + 1)]

        def group(gg, grads):
            g = ng - 1 - gg
            grads = list(grads)
            pairs = range(NP)
            for i in reversed(range(RW_G)):
                t = g * RW_G + i
                row = lambda a, q: rows_scr[a * NP + q, pl.ds(t, 1), :]

                def put(a, q, value):
                    out_scr[a * NP + q, pl.ds(t, 1), :] = value

                s_old = [st_ref[q, jnp.maximum(t - 1, 0)] for q in pairs]
                if i == 0:
                    s_old = [jnp.where(g == 0, jnp.where(step == nb - 1, 0.0, sp_ref[q, 0]), s_old[q]) for q in pairs]
                dycol = [_pair_column(dyt_scr, q * ng + g, i, mask) for q in pairs]
                dS = [grads[q] + dycol[q] * row(R_, q) for q in pairs]
                m = [_pair_rowsum(dS[q] * row(B_, q), mask) for q in pairs]
                dv = [_pair_rowsum(dS[q] * row(K_, q), mask) for q in pairs]
                for q in pairs:
                    put(R_, q, jnp.sum(st_ref[q, t] * dycol[q], axis=0, keepdims=True))
                    put(W_, q, jnp.sum(dS[q] * s_old[q], axis=0, keepdims=True))
                    put(K_, q, jnp.sum(dS[q] * _pair_column(vt_scr, q * ng + g, i, mask), axis=0, keepdims=True))
                for q in pairs:
                    dsa = -jnp.where(mask, *m[q])
                    grads[q] = dS[q] * row(W_, q) + dsa * row(KK_, q)
                    put(KK_, q, jnp.sum(s_old[q] * dsa, axis=0, keepdims=True))
                    put(B_, q, -jnp.sum(dS[q] * sa_ref[q, t], axis=0, keepdims=True))
                    dvt_scr[q * ng + g, 0:RW_HD, i:i + 1] = dv[q][0]
                    dvt_scr[q * ng + g, RW_HD:, i:i + 1] = dv[q][1]
            return tuple(grads)

        grads = lax.fori_loop(0, ng, group, tuple(ds_scr[q] for q in range(NP)))
        for q in range(NP):
            ds_scr[q] = grads[q]
            for a, ref in enumerate((dr_ref, dw_ref, dk_ref, dkk_ref, db_ref)):
                ref[:, 128 * q:128 * (q + 1)] = out_scr[a * NP + q]
            for g in range(ng):
                dv_ref[g * RW_G:(g + 1) * RW_G, 128 * q:128 * (q + 1)] = dvt_scr[q * ng + g].T[0:RW_G, :]
        if nc:
            pl.when((pl.program_id(0) == npair - 1) & (step == nb - 1))(finish)

    blk = lambda cb: pl.BlockSpec((RW_TB, 128 * NP), lambda p, i, cb=cb: (nb - 1 - i, cb + p))
    st_spec = pl.BlockSpec((NP, RW_TB, RW_HD, 128), lambda p, i: (p, nb - 1 - i, 0, 0))
    sp_spec = pl.BlockSpec((NP, 1, RW_HD, 128), lambda p, i: (p, jnp.maximum((nb - 1 - i) * RW_TB - 1, 0), 0, 0))
    out = jax.ShapeDtypeStruct((T, RW_W), F32)
    tiles = pltpu.VMEM((NP * ng, 128, 128), F32)
    return pl.pallas_call(
        body, name="rw_core_bwd", grid=(npair, nb),
        in_specs=[blk(0), blk(2 * RW_W // (128 * NP)), blk(0), blk(0), blk(0), blk(0), blk(0), st_spec, sp_spec, st_spec]
        + [_ANY] * nc,
        out_specs=[blk(0)] * 6 + [_ANY] * nc,
        out_shape=[out] * 6 + [jax.ShapeDtypeStruct(a.shape, a.dtype) for a in exchange],
        scratch_shapes=[pltpu.VMEM((NP, RW_HD, 128), F32), tiles, tiles, tiles,
                        pltpu.VMEM((5 * NP, RW_TB, 128), F32), pltpu.VMEM((5 * NP, RW_TB, 128), F32)] + _comm_sems(nc),
        compiler_params=_params(("arbitrary", "arbitrary")),
    )(rw, rw, w, k2, kk, b, dy, states, states, sa_tiles, *exchange)


GLA_CB = 8


def _gla_chunk(s_t, q, k, v, la, ltri):
    cum = jnp.dot(ltri, la, precision=HI, preferred_element_type=F32)
    total = jnp.sum(la, axis=0, keepdims=True)
    kdec = k * jnp.exp(total - cum)
    u_t = _bdot(v, kdec, _TN)
    s_t = jnp.exp(total) * s_t + u_t
    o = _bdot(q * (GLA_DK ** -0.5), s_t, _NT)
    return s_t, o


def _gla_core_fwd(pa, la, ltri):
    T = pa.shape[0]
    cb = min(GLA_CB, T // CHUNK)
    rows = cb * CHUNK
    nsteps = T // rows

    def body(q_ref, k_ref, v_ref, la_ref, ltri_ref, o_ref, st_ref, s_scr):
        @pl.when(pl.program_id(1) == 0)
        def _():
            s_scr[...] = jnp.zeros_like(s_scr)

        def chunk(c, s_t):
            sl = pl.ds(pl.multiple_of(c * CHUNK, CHUNK), CHUNK)
            s_t, o = _gla_chunk(s_t, q_ref[sl, :], k_ref[sl, :], v_ref[sl, :], la_ref[sl, :], ltri_ref[...])
            o_ref[sl, :] = o
            st_ref[0, c] = s_t
            return s_t

        s_scr[...] = lax.fori_loop(0, cb, chunk, s_scr[...])

    qk = lambda off: pl.BlockSpec((rows, GLA_DK), lambda h, i, off=off: (i, off + h))
    vspec = pl.BlockSpec((rows, GLA_DV), lambda h, i: (i, 2 * GLA_QK // GLA_DV + h))
    return pl.pallas_call(
        body, name="gla_core_fwd", grid=(GLA_HEADS, nsteps),
        in_specs=[qk(0), qk(GLA_HEADS), vspec, qk(0), pl.BlockSpec((CHUNK, CHUNK), lambda h, i: (0, 0))],
        out_specs=[pl.BlockSpec((rows, GLA_DV), lambda h, i: (i, h)),
                   pl.BlockSpec((1, cb, GLA_DV, GLA_DK), lambda h, i: (h, i, 0, 0))],
        out_shape=(jax.ShapeDtypeStruct((T, GLA_V), F32),
                   jax.ShapeDtypeStruct((GLA_HEADS, T // CHUNK, GLA_DV, GLA_DK), F32)),
        scratch_shapes=[pltpu.VMEM((GLA_DV, GLA_DK), F32)],
        compiler_params=_params(("arbitrary", "arbitrary")),
    )(pa, pa, pa, la, ltri)


def _gla_core_bwd(pa, la, ltri, states, do):
    T = pa.shape[0]
    cb = min(GLA_CB, T // CHUNK)
    rows = cb * CHUNK
    nsteps = T // rows

    def body(q_ref, k_ref, v_ref, la_ref, ltri_ref, st_ref, sp_ref, do_ref,
             dq_ref, dk_ref, dv_ref, dla_ref, ds_scr):
        step = pl.program_id(1)

        @pl.when(step == 0)
        def _():
            ds_scr[...] = jnp.zeros_like(ds_scr)

        s_before = jnp.where(step == nsteps - 1, 0.0, sp_ref[0, 0])

        def chunk(cc, ds_t):
            c = cb - 1 - cc
            sl = pl.ds(pl.multiple_of(c * CHUNK, CHUNK), CHUNK)
            s_prev = jnp.where(c == 0, s_before, st_ref[0, jnp.maximum(c - 1, 0)])
            _, vjp = jax.vjp(functools.partial(_gla_chunk, ltri=ltri_ref[...]),
                             s_prev, q_ref[sl, :], k_ref[sl, :], v_ref[sl, :], la_ref[sl, :])
            ds_prev, dq, dk, dv, dla = vjp((ds_t, do_ref[sl, :]))
            dq_ref[sl, :] = dq
            dk_ref[sl, :] = dk
            dv_ref[sl, :] = dv
            dla_ref[sl, :] = dla
            return ds_prev

        ds_scr[...] = lax.fori_loop(0, cb, chunk, ds_scr[...])

    r = lambda i: nsteps - 1 - i
    qk = lambda off: pl.BlockSpec((rows, GLA_DK), lambda h, i, off=off: (r(i), off + h))
    vspec = pl.BlockSpec((rows, GLA_DV), lambda h, i: (r(i), 2 * GLA_QK // GLA_DV + h))
    o128 = pl.BlockSpec((rows, GLA_DK), lambda h, i: (r(i), h))
    o256 = pl.BlockSpec((rows, GLA_DV), lambda h, i: (r(i), h))
    return pl.pallas_call(
        body, name="gla_core_bwd", grid=(GLA_HEADS, nsteps),
        in_specs=[qk(0), qk(GLA_HEADS), vspec, qk(0), pl.BlockSpec((CHUNK, CHUNK), lambda h, i: (0, 0)),
                  pl.BlockSpec((1, cb, GLA_DV, GLA_DK), lambda h, i: (h, r(i), 0, 0)),
                  pl.BlockSpec((1, 1, GLA_DV, GLA_DK), lambda h, i: (h, jnp.maximum(r(i) * cb - 1, 0), 0, 0)),
                  o256],
        out_specs=[o128, o128, o256, o128],
        out_shape=(jax.ShapeDtypeStruct((T, GLA_QK), F32), jax.ShapeDtypeStruct((T, GLA_QK), F32),
                   jax.ShapeDtypeStruct((T, GLA_V), F32), jax.ShapeDtypeStruct((T, GLA_QK), F32)),
        scratch_shapes=[pltpu.VMEM((GLA_DV, GLA_DK), F32)],
        compiler_params=_params(("arbitrary", "arbitrary")),
    )(pa, pa, pa, la, ltri, states, states, do)


def _rowwise(fn, name, T, tm, rows, pars, row_outs, acc_outs):
    nr, npar, nro = len(rows), len(pars), len(row_outs)
    tm = min(tm, T)
    nsteps = T // tm

    def body(*refs):
        i = pl.program_id(0)
        ins = [r[...] for r in refs[:nr + npar]]
        outs, accs = fn(i, *ins)
        for r, o in zip(refs[nr + npar:nr + npar + nro], outs):
            r[...] = o.astype(r.dtype)
        for r, a in zip(refs[nr + npar + nro:], accs):
            @pl.when(i == 0)
            def _(r=r, a=a):
                r[...] = a

            @pl.when(i > 0)
            def _(r=r, a=a):
                r[...] += a

    def rspec(width, cb, kind):
        if kind == "cur":
            return pl.BlockSpec((tm, width), lambda i: (i, cb))
        if kind == "prev":
            return pl.BlockSpec((8, width), lambda i: (jnp.maximum(i * (tm // 8) - 1, 0), cb))
        return pl.BlockSpec((8, width), lambda i: (jnp.minimum((i + 1) * (tm // 8), T // 8 - 1), cb))

    in_specs = [rspec(w, cb, kind) for (_, w, cb, kind) in rows]
    in_specs += [pl.BlockSpec(p.shape, lambda i, nd=p.ndim: (0,) * nd) for p in pars]
    out_specs = [pl.BlockSpec((tm, w), lambda i: (i, 0)) for (w, _) in row_outs]
    out_specs += [pl.BlockSpec(s, lambda i, nd=len(s): (0,) * nd) for s in acc_outs]
    out_shape = [jax.ShapeDtypeStruct((T, w), dt) for (w, dt) in row_outs]
    out_shape += [jax.ShapeDtypeStruct(s, F32) for s in acc_outs]
    res = pl.pallas_call(
        body, name=name, grid=(nsteps,), in_specs=in_specs, out_specs=out_specs, out_shape=out_shape,
        compiler_params=_params(("arbitrary",)),
    )(*[r[0] for r in rows], *pars)
    return res


def _cur(a, width=None, cb=0):
    return (a, a.shape[1] if width is None else width, cb, "cur")


def _sigmoid(x):
    return 1.0 / (1.0 + jnp.exp(-x))


def _silu(x):
    return x * _sigmoid(x)


def _softplus(x):
    return jnp.maximum(x, 0.0) + jnp.log(1.0 + jnp.exp(-jnp.abs(x)))


def _rms(x, g):
    return x * lax.rsqrt(jnp.mean(x * x, axis=-1, keepdims=True) + NORM_EPS) * g


def _dot_hi(a, b):
    return jnp.dot(a, b, precision=lax.Precision.HIGH, preferred_element_type=F32)


def _rms_fwd(x, g, name):
    T = x.shape[0]
    fn = lambda i, xb, gb: ((_rms(xb, gb),), ())
    return _rowwise(fn, name, T, 256, [_cur(x)], [g], [(D, BF16)], [])[0]


def _rms_bwd(x, g, dh, dres, name):
    T = x.shape[0]

    def fn(i, xb, dhb, drb, gb):
        _, vjp = jax.vjp(_rms, xb, gb)
        dx, dg = vjp(dhb)
        return (drb + dx,), (dg,)

    return _rowwise(fn, name, T, 256, [_cur(x), _cur(dh), _cur(dres)], [g], [(D, F32)], [(1, D)])


def _loss_bwd(x, target, g):
    T = x.shape[0]

    def loss(xb, gb, tb):
        err = _rms(xb, gb) - tb
        return 0.5 * jnp.sum(jnp.mean(err * err, axis=-1, keepdims=True))

    def fn(i, xb, tb, gb):
        val, (dx, dg) = jax.value_and_grad(loss, argnums=(0, 1))(xb, gb, tb)
        return (dx,), (jnp.full((1, 128), val, F32), dg)

    return _rowwise(fn, "loss_bwd", T, 256, [_cur(x), _cur(target)], [g], [(D, F32)], [(1, 128), (1, D)])


def _gla_la(a_down, w_a2, b_a):
    return -_softplus(-(_bdot(a_down, w_a2, _NN) + b_a)) * (1.0 / GLA_TAU)


def _gla_prep(pa, w_a2, b_a):
    T = pa.shape[0]
    fn = lambda i, ab, wb, bb: ((_gla_la(ab, wb, bb),), ())
    return _rowwise(fn, "gla_prep", T, 512, [_cur(pa, LORA_P, (2 * GLA_QK + 2 * GLA_V) // LORA_P)], [w_a2, b_a],
                    [(GLA_QK, F32)], [])[0]


def _gla_prep_bwd(pa, w_a2, b_a, dla):
    T = pa.shape[0]

    def fn(i, ab, dlab, wb, bb):
        _, vjp = jax.vjp(_gla_la, ab, wb, bb)
        da, dw, db = vjp(dlab)
        return (da,), (dw, db)

    return _rowwise(fn, "gla_prep_bwd", T, 512, [_cur(pa, LORA_P, (2 * GLA_QK + 2 * GLA_V) // LORA_P), _cur(dla)],
                    [w_a2, b_a], [(LORA_P, BF16)], [(LORA_P, GLA_QK), (1, GLA_QK)])


def _gla_out(o, r, gn, ind, ind_t):
    ms = _dot_hi(_dot_hi(o * o, ind) * (1.0 / GLA_DV), ind_t)
    return o * lax.rsqrt(ms + NORM_EPS) * gn * _silu(r)


def _gla_post(o_raw, pa, gn, ind, ind_t):
    T = pa.shape[0]
    fn = lambda i, ob, rb, gb, a, b: ((_gla_out(ob, rb, gb, a, b),), ())
    return _rowwise(fn, "gla_post", T, 256, [_cur(o_raw), _cur(pa, GLA_V, 2)], [gn, ind, ind_t], [(GLA_V, BF16)], [])[0]


def _gla_post_bwd(o_raw, pa, gn, ind, ind_t, do):
    T = pa.shape[0]

    def fn(i, ob, rb, dob, gb, a, b):
        _, vjp = jax.vjp(lambda o, r, g: _gla_out(o, r, g, a, b), ob, rb, gb)
        d_o, d_r, d_g = vjp(dob)
        return (d_o, d_r), (d_g,)

    return _rowwise(fn, "gla_post_bwd", T, 256, [_cur(o_raw), _cur(pa, GLA_V, 2), _cur(do)], [gn, ind, ind_t],
                    [(GLA_V, F32), (GLA_V, BF16)], [(1, GLA_V)])


def _shift_rows(cur, prev8, i):
    first = jnp.where(i == 0, 0.0, prev8[7:8, :])
    rolled = pltpu.roll(cur, 1, 0)
    return jnp.where(lax.broadcasted_iota(jnp.int32, cur.shape, 0) == 0, first, rolled)


def _rw_gates(rw, w0, w_w2, a0, w_a2, w_g2, k_k, k_a, ind, ind_t):
    rk = rw[:, RW_W:2 * RW_W]
    wd = rw[:, 3 * RW_W:3 * RW_W + LORA_P]
    ad = rw[:, 3 * RW_W + LORA_P:3 * RW_W + 2 * LORA_P]
    gd = rw[:, 3 * RW_W + 2 * LORA_P:]
    w_raw = w0 + _bdot(jnp.tanh(wd), w_w2, _NN)
    w = jnp.exp(-jnp.exp(-_softplus(-w_raw) - 0.5))
    a = _sigmoid(a0 + _bdot(ad, w_a2, _NN))
    g = _bdot(_sigmoid(gd), w_g2, _NN)
    kk = rk * k_k
    kk = kk * _dot_hi(lax.rsqrt(jnp.maximum(_dot_hi(kk * kk, ind), 1e-24)), ind_t)
    k2 = rk * (1.0 + (a - 1.0) * k_a)
    return w, k2, kk, kk * a, g


def _rw_prep(pr, mu, gate_pars):
    T = pr.shape[0]

    def fn(i, cur, prev8, mub, *gp):
        rw = cur + mub * (_shift_rows(cur, prev8, i) - cur)
        return (rw,) + _rw_gates(rw, *gp), ()

    return _rowwise(fn, "rw_prep", T, 256, [_cur(pr), (pr, PR_W, 0, "prev")], [mu, *gate_pars],
                    [(PR_W, F32)] + [(RW_W, F32)] * 5, [])


def _rw_prep_bwd(pr, mu, gate_pars, d_r, d_v, d_w, d_k2, d_kk, d_b, d_g):
    T = pr.shape[0]
    rows = [_cur(pr), (pr, PR_W, 0, "prev")] + [_cur(x) for x in (*d_r, *d_v, d_w, *d_k2, d_kk, d_b, d_g)]
    acc = [(1, PR_W)] + [tuple(p.shape) for p in gate_pars[:-2]]

    def fn(i, cur, prev8, dr1, dr2, dv1, dv2, dw, dk1, dk2, dkk, db, dg, mub, *gp):
        sh = _shift_rows(cur, prev8, i)
        rw = cur + mub * (sh - cur)
        _, vjp = jax.vjp(lambda x, *p: _rw_gates(x, *p, gp[-2], gp[-1]), rw, *gp[:-2])
        grads = vjp((dw, dk1 + dk2, dkk, db, dg))
        zeros = jnp.zeros((cur.shape[0], PR_W - 3 * RW_W), F32)
        drw = grads[0] + jnp.concatenate([dr1 + dr2, jnp.zeros_like(dr1), dv1 + dv2, zeros], axis=1)
        dmu = jnp.sum(drw * (sh - cur), axis=0, keepdims=True)
        return (drw,), (dmu, *grads[1:])

    return _rowwise(fn, "rw_prep_bwd", T, 128, rows, [mu, *gate_pars], [(PR_W, F32)], acc)


def _shift_bwd(drw, mu):
    T = drw.shape[0]
    tm = min(256, T)

    def fn(i, cur, next8, mub):
        last = jnp.where(i == T // tm - 1, 0.0, next8[0:1, :])
        rolled = pltpu.roll(cur, cur.shape[0] - 1, 0)
        nxt = jnp.where(lax.broadcasted_iota(jnp.int32, cur.shape, 0) == cur.shape[0] - 1, last, rolled)
        return ((1.0 - mub) * cur + mub * nxt,), ()

    return _rowwise(fn, "shift_bwd", T, tm, [_cur(drw), (drw, PR_W, 0, "next")], [mu], [(PR_W, BF16)], [])[0]


def _rw_out(y, r, v, k2, g, lnx_w, lnx_b, r_k, ind, ind_t):
    mean = _dot_hi(_dot_hi(y, ind) * (1.0 / RW_HD), ind_t)
    yc = y - mean
    var = _dot_hi(_dot_hi(yc * yc, ind) * (1.0 / RW_HD), ind_t)
    yn = yc * lax.rsqrt(var + GN_EPS) * lnx_w + lnx_b
    bonus = _dot_hi(_dot_hi(r * k2 * r_k, ind), ind_t) * v
    return (yn + bonus) * g


def _rw_post(y, rw, k2, g, pars):
    T = y.shape[0]
    fn = lambda i, *a: ((_rw_out(*a),), ())
    return _rowwise(fn, "rw_post", T, 256, [_cur(y), _cur(rw, RW_W, 0), _cur(rw, RW_W, 2), _cur(k2), _cur(g)], pars,
                    [(RW_W, BF16)], [])[0]


def _rw_post_bwd(y, rw, k2, g, pars, do):
    T = y.shape[0]

    def fn(i, yb, rb, vb, kb, gb, dob, lw, lb, rk, ind, ind_t):
        _, vjp = jax.vjp(lambda *a: _rw_out(*a, ind, ind_t), yb, rb, vb, kb, gb, lw, lb, rk)
        gr = vjp(dob)
        return gr[:5], gr[5:]

    return _rowwise(fn, "rw_post_bwd", T, 256,
                    [_cur(y), _cur(rw, RW_W, 0), _cur(rw, RW_W, 2), _cur(k2), _cur(g), _cur(do)], pars,
                    [(RW_W, F32)] * 5, [(1, RW_W)] * 3)


def _merge_bwd(dm, y_gla, y_rw, pg, gate_b):
    T = dm.shape[0]

    def fn(i, dmb, ya, yr, p1, p2, gb):
        g1 = _sigmoid(p1 + gb[:, :D])
        g2 = _sigmoid(p2 + gb[:, D:])
        dp1 = dmb * ya * g1 * (1.0 - g1)
        dp2 = dmb * yr * g2 * (1.0 - g2)
        dp = jnp.concatenate([dp1, dp2], axis=1)
        return (dmb * g1, dmb * g2, dp), (jnp.sum(dp, axis=0, keepdims=True),)

    return _rowwise(fn, "merge_bwd", T, 256, [_cur(dm), _cur(y_gla), _cur(y_rw), _cur(pg, D, 0), _cur(pg, D, 1)],
                    [gate_b], [(D, BF16), (D, BF16), (PG_W, BF16)], [(1, PG_W)])


_NN = (((1,), (0,)), ((), ()))
_NT = (((1,), (1,)), ((), ()))
_TN = (((0,), (0,)), ((), ()))


def _bdot(a, b, dims):
    return lax.dot_general(a.astype(BF16), b.astype(BF16), dims, preferred_element_type=F32)


def _accumulate(k, nk, acc, part, finish):
    if nk == 1:
        finish(part)
        return

    @pl.when(k == 0)
    def _():
        acc[...] = part

    @pl.when(k > 0)
    def _():
        acc[...] += part

    @pl.when(k == nk - 1)
    def _():
        finish(acc[...])


def _call(body, comm, name, grid, in_specs, out_specs, out_shape, scratch_shapes, sem, operands):
    if comm is None:
        return pl.pallas_call(body, name=name, grid=grid, in_specs=in_specs, out_specs=out_specs, out_shape=out_shape,
                              scratch_shapes=scratch_shapes, compiler_params=_params(sem))(*operands)
    kind, arrays = comm
    nc, n_in, n_out, n_scr = len(arrays), len(in_specs), len(out_shape), len(scratch_shapes)
    total = 1
    for n in grid:
        total *= n

    def with_comm(*refs):
        own = refs[:n_in] + refs[n_in + nc:n_in + nc + n_out] + refs[n_in + 2 * nc + n_out:n_in + 2 * nc + n_out + n_scr]
        c_in, c_out, sems = refs[n_in:n_in + nc], refs[n_in + nc + n_out:n_in + 2 * nc + n_out], refs[-3:]
        step = 0
        for axis, n in enumerate(grid):
            step = step * n + pl.program_id(axis)
        start, *forward, finish = _PLANS[kind][0](c_in, c_out, *sems)
        pl.when(step == 0)(start)
        for stage in forward:
            pl.when(step == total // 2)(stage)
        body(*own)
        pl.when(step == total - 1)(finish)

    return pl.pallas_call(
        with_comm, name=name, grid=grid, in_specs=list(in_specs) + [_ANY] * nc, out_specs=list(out_specs) + [_ANY] * nc,
        out_shape=list(out_shape) + [jax.ShapeDtypeStruct(_PLANS[kind][1](a.shape), a.dtype) for a in arrays],
        scratch_shapes=list(scratch_shapes) + _comm_sems(nc), compiler_params=_params(("arbitrary",) * len(grid)),
    )(*operands, *arrays)


def _matmul(a, b, mode, M, N, K, tm, tn, tk, name, a_off=(0, 0), b_off=(0, 0), res=None, scale=1.0, out_dtype=F32,
            comm=None):
    tm, tn, tk = min(tm, M), min(tn, N), min(tk, K)
    nk = K // tk
    if mode == "nn":
        a_spec = pl.BlockSpec((tm, tk), lambda i, j, k: (i + a_off[0], k + a_off[1]))
        b_spec = pl.BlockSpec((tk, tn), lambda i, j, k: (k + b_off[0], j + b_off[1]))
        dims = _NN
    elif mode == "nt":
        a_spec = pl.BlockSpec((tm, tk), lambda i, j, k: (i + a_off[0], k + a_off[1]))
        b_spec = pl.BlockSpec((tn, tk), lambda i, j, k: (j + b_off[0], k + b_off[1]))
        dims = _NT
    else:
        a_spec = pl.BlockSpec((tk, tm), lambda i, j, k: (k + a_off[0], i + a_off[1]))
        b_spec = pl.BlockSpec((tk, tn), lambda i, j, k: (k + b_off[0], j + b_off[1]))
        dims = _TN
    o_spec = pl.BlockSpec((tm, tn), lambda i, j, k: (i, j))

    def body(a_ref, b_ref, *rest):
        r_ref = rest[0] if res is not None else None
        o_ref = rest[1] if res is not None else rest[0]
        acc = rest[-1] if nk > 1 else None

        def finish(total):
            total = total * scale if scale != 1.0 else total
            if r_ref is not None:
                total = r_ref[...] + total
            o_ref[...] = total.astype(out_dtype)

        _accumulate(pl.program_id(2), nk, acc, _bdot(a_ref[...], b_ref[...], dims), finish)

    out = _call(body, comm, name, (M // tm, N // tn, nk), [a_spec, b_spec] + ([o_spec] if res is not None else []),
                [o_spec], [jax.ShapeDtypeStruct((M, N), out_dtype)], [pltpu.VMEM((tm, tn), F32)] if nk > 1 else [],
                ("parallel", "parallel", "arbitrary"), [a, b] + ([res] if res is not None else []))
    return out[0] if comm is None else out


def _ffn_up(h, wg, wu, name, comm=None):
    T = h.shape[0]
    tm = min(1024, T)

    def body(h_ref, wg_ref, wu_ref, a_ref, u_ref, s_ref):
        hb = h_ref[...]
        a = _bdot(hb, wg_ref[...], _NN)
        u = _bdot(hb, wu_ref[...], _NN)
        a_ref[...] = a
        u_ref[...] = u
        s_ref[...] = (_silu(a) * u).astype(BF16)

    w_spec = pl.BlockSpec((None, D, FSH), lambda i, j: (j, 0, 0))
    o_spec = pl.BlockSpec((None, tm, FSH), lambda i, j: (j, i, 0))
    sh = lambda dt: jax.ShapeDtypeStruct((NDEV, T, FSH), dt)
    return _call(body, comm, name, (T // tm, NDEV), [pl.BlockSpec((tm, D), lambda i, j: (i, 0)), w_spec, w_spec],
                 [o_spec] * 3, [sh(F32), sh(F32), sh(BF16)], [], ("parallel", "arbitrary"), [h, wg, wu])


def _ffn_down(s, wd, x, name, comm=None):
    T = x.shape[0]
    tm, tn, sh = min(1024, T), 1024, 4

    def body(s_ref, wd_ref, x_ref, o_ref, acc):
        part = _bdot(s_ref[0], wd_ref[0], _NN)
        for q in range(1, sh):
            part = part + _bdot(s_ref[q], wd_ref[q], _NN)

        def finish(total):
            o_ref[...] = x_ref[...] + 0.5 * total

        _accumulate(pl.program_id(2), NDEV // sh, acc, part, finish)

    xo = pl.BlockSpec((tm, tn), lambda i, n, j: (i, n))
    out = _call(body, comm, name, (T // tm, D // tn, NDEV // sh),
                [pl.BlockSpec((sh, tm, FSH), lambda i, n, j: (j, i, 0)),
                 pl.BlockSpec((sh, FSH, tn), lambda i, n, j: (j, 0, n)), xo],
                [xo], [jax.ShapeDtypeStruct((T, D), F32)], [pltpu.VMEM((tm, tn), F32)],
                ("parallel", "parallel", "arbitrary"), [s, wd, x])
    return out[0] if comm is None else out


def _ffn_bwd_hidden(dx, wd, a, u, name):
    T = dx.shape[0]
    tm = min(1024, T)

    def body(dx_ref, wd_ref, a_ref, u_ref, da_ref, du_ref):
        ds = 0.5 * _bdot(dx_ref[...], wd_ref[...], _NT)
        av = a_ref[...]
        sg = _sigmoid(av)
        da_ref[...] = (ds * u_ref[...] * (sg * (1.0 + av * (1.0 - sg)))).astype(BF16)
        du_ref[...] = (ds * (av * sg)).astype(BF16)

    act = pl.BlockSpec((None, tm, FSH), lambda i, j: (j, i, 0))
    sh = jax.ShapeDtypeStruct((NDEV, T, FSH), BF16)
    return pl.pallas_call(
        body, name=name, grid=(T // tm, NDEV),
        in_specs=[pl.BlockSpec((tm, D), lambda i, j: (i, 0)), pl.BlockSpec((None, FSH, D), lambda i, j: (j, 0, 0)),
                  act, act],
        out_specs=[act, act], out_shape=(sh, sh),
        compiler_params=_params(("parallel", "arbitrary")),
    )(dx, wd, a, u)


def _ffn_bwd_input(da, du, wg, wu, name, comm=None):
    T = da.shape[1]
    tm, tn, sh = min(1024, T), 1024, 2

    def body(da_ref, du_ref, wg_ref, wu_ref, o_ref, acc):
        part = _bdot(da_ref[0], wg_ref[0], _NT) + _bdot(du_ref[0], wu_ref[0], _NT)
        for q in range(1, sh):
            part = part + _bdot(da_ref[q], wg_ref[q], _NT) + _bdot(du_ref[q], wu_ref[q], _NT)

        def finish(total):
            o_ref[...] = total

        _accumulate(pl.program_id(2), NDEV // sh, acc, part, finish)

    act = pl.BlockSpec((sh, tm, FSH), lambda i, n, j: (j, i, 0))
    wsp = pl.BlockSpec((sh, tn, FSH), lambda i, n, j: (j, n, 0))
    out = _call(body, comm, name, (T // tm, D // tn, NDEV // sh), [act, act, wsp, wsp],
                [pl.BlockSpec((tm, tn), lambda i, n, j: (i, n))], [jax.ShapeDtypeStruct((T, D), F32)],
                [pltpu.VMEM((tm, tn), F32)], ("parallel", "parallel", "arbitrary"), [da, du, wg, wu])
    return out[0] if comm is None else out


def _ffn_grad_up(h, da, du, name, comm=None, core_major=False):
    T = h.shape[0]
    tm, tk = 1024, min(4096, T)
    nk = T // tk

    def body(h_ref, da_ref, du_ref, o_ref, acc_a, acc_u):
        k = pl.program_id(2)
        hb = h_ref[...]
        for acc, ref, slot in ((acc_a, da_ref, 0), (acc_u, du_ref, 1)):
            def finish(total, slot=slot):
                o_ref[slot] = total.astype(BF16)

            _accumulate(k, nk, acc, _bdot(hb, ref[...], _TN), finish)

    act = pl.BlockSpec((None, tk, FSH), lambda j, i, t: (j, t, 0))
    if core_major:
        o_spec = pl.BlockSpec((None, None, 2, tm, FSH), lambda j, i, t: (j % 2, j // 2, 0, i, 0))
        o_shape = jax.ShapeDtypeStruct((2, NDEV // 2, 2, D, FSH), BF16)
    else:
        o_spec = pl.BlockSpec((None, 2, tm, FSH), lambda j, i, t: (j, 0, i, 0))
        o_shape = jax.ShapeDtypeStruct((NDEV, 2, D, FSH), BF16)
    out = _call(body, comm, name, (NDEV, D // tm, nk), [pl.BlockSpec((tk, tm), lambda j, i, t: (t, i)), act, act],
                [o_spec], [o_shape],
                [pltpu.VMEM((tm, FSH), F32), pltpu.VMEM((tm, FSH), F32)], ("parallel", "parallel", "arbitrary"), [h, da, du])
    return out[0] if comm is None else out


def _ffn_grad_down(s, dx, name):
    T = dx.shape[0]
    tn, tk = 1024, min(2048, T)
    nk = T // tk

    def body(s_ref, dx_ref, o_ref, acc):
        def finish(total):
            o_ref[...] = (0.5 * total).astype(BF16)

        _accumulate(pl.program_id(2), nk, acc, _bdot(s_ref[...], dx_ref[...], _TN), finish)

    return pl.pallas_call(
        body, name=name, grid=(NDEV, D // tn, nk),
        in_specs=[pl.BlockSpec((None, tk, FSH), lambda j, n, t: (j, t, 0)), pl.BlockSpec((tk, tn), lambda j, n, t: (t, n))],
        out_specs=pl.BlockSpec((None, FSH, tn), lambda j, n, t: (j, 0, n)),
        out_shape=jax.ShapeDtypeStruct((NDEV, DFF // NDEV, D), BF16),
        scratch_shapes=[pltpu.VMEM((FSH, tn), F32)],
        compiler_params=_params(("parallel", "parallel", "arbitrary")),
    )(s, dx)


def _branch_merge(o_gla, o_rw, wb, pg, gate_b):
    T = o_gla.shape[0]
    tm, tn = min(1024, T), 512

    def body(og_ref, or_ref, w1_ref, w2_ref, p1_ref, p2_ref, b1_ref, b2_ref, yg_ref, yr_ref, m_ref):
        yg = _bdot(og_ref[...], w1_ref[...], _NN)
        yr = _bdot(or_ref[...], w2_ref[...], _NN)
        yg_ref[...] = yg
        yr_ref[...] = yr
        m_ref[...] = (_sigmoid(p1_ref[...] + b1_ref[...]) * yg + _sigmoid(p2_ref[...] + b2_ref[...]) * yr).astype(BF16)

    nj = D // tn
    act = pl.BlockSpec((tm, GLA_V), lambda i, j: (i, 0))
    out = pl.BlockSpec((tm, tn), lambda i, j: (i, j))
    return pl.pallas_call(
        body, name="branch_merge", grid=(T // tm, nj),
        in_specs=[act, act, pl.BlockSpec((GLA_V, tn), lambda i, j: (0, j)), pl.BlockSpec((RW_W, tn), lambda i, j: (1, j)),
                  out, pl.BlockSpec((tm, tn), lambda i, j: (i, nj + j)),
                  pl.BlockSpec((1, tn), lambda i, j: (0, j)), pl.BlockSpec((1, tn), lambda i, j: (0, nj + j))],
        out_specs=[out, out, out],
        out_shape=(jax.ShapeDtypeStruct((T, D), F32), jax.ShapeDtypeStruct((T, D), F32), jax.ShapeDtypeStruct((T, D), BF16)),
        compiler_params=_params(("parallel", "arbitrary")),
    )(o_gla, o_rw, wb, wb, pg, pg, gate_b, gate_b)


def _head_indicator(width, heads):
    col = lax.broadcasted_iota(jnp.int32, (width, 128), 0) // (width // heads)
    ind = (col == lax.broadcasted_iota(jnp.int32, (width, 128), 1)).astype(F32)
    return ind, ind.T


def _ffn_fwd(x, g, wg, wu, wd, tag):
    h = _rms_fwd(x, g, "rms_" + tag)
    a, u, s = _ffn_up(h, wg, wu, "ffn_up_" + tag)
    return _ffn_down(s, wd, x, "ffn_down_" + tag), (h, a, u, s)


def _ffn_fwd_gathering(x, g, wg, wu, wd_block, next_block, tag):
    h = _rms_fwd(x, g, "rms_" + tag)
    a, u, s, wd = _ffn_up(h, wg, wu, "ffn_up_" + tag, comm=("gather", [wd_block]))
    y, gathered = _ffn_down(s, wd, x, "ffn_down_" + tag, comm=("gather", [next_block]))
    return y, (h, a, u, s), wd, gathered


def _ffn_bwd(dy, x, g, wg, wu, wd, saved, tag, exchange=False):
    h, a, u, s = saved
    dwd = _ffn_grad_down(s, dy, "ffn_grad_down_" + tag)
    da, du = _ffn_bwd_hidden(dy, wd, a, u, "ffn_bwd_hidden_" + tag)
    if exchange:
        dw_up, dwd = _ffn_grad_up(h, da, du, "ffn_grad_up_" + tag, comm=("exchange", [dwd]), core_major=True)
        dh, dw_up = _ffn_bwd_input(da, du, wg, wu, "ffn_bwd_input_" + tag, comm=("quad", [_chip_sum(dw_up, "up_" + tag)]))
    else:
        dw_up = _ffn_grad_up(h, da, du, "ffn_grad_up_" + tag)
        dh = _ffn_bwd_input(da, du, wg, wu, "ffn_bwd_input_" + tag)
    dx, dg = _rms_bwd(x, g, dh, dy, "rms_bwd_" + tag)
    return dx, dg, dw_up, dwd


def _local_step(x, target, w, blocks):
    T = x.shape[0]
    ind16, ind16_t = _head_indicator(RW_W, RW_HEADS)
    ind4, ind4_t = _head_indicator(GLA_V, GLA_HEADS)
    ltri = jnp.tril(jnp.ones((CHUNK, CHUNK), F32))
    gate_pars = [w["w0"], w["w_w2"], w["a0"], w["w_a2"], w["w_g2"], w["k_k"], w["k_a"], ind16, ind16_t]
    post_pars = [w["lnx_w"], w["lnx_b"], w["r_k"], ind16, ind16_t]

    x1, ffn1, wd1, g_proj = _ffn_fwd_gathering(x, w["g1"], w["wg1"], w["wu1"], blocks["wd1"], blocks["win"], "1")
    win = _align_proj(_unshard_cols(g_proj))
    h2 = _rms_fwd(x1, w["g2"], "rms_mix")
    proj = lambda n, off, name: _matmul(h2, win, "nn", T, n, D, 1024, 512, D, name, b_off=(0, off // 512))
    pg = proj(PG_W, 0, "proj_gate")
    pr = proj(PR_W, PG_W, "proj_rwkv")
    pa = proj(PA_W, PG_W + PR_W, "proj_gla")
    la = _gla_prep(pa, w["gla_w_a2"], w["gla_b_a"])
    o_raw, gla_states = _gla_core_fwd(pa, la, ltri)
    o_gla = _gla_post(o_raw, pa, w["gn"], ind4, ind4_t)
    rw, dec, k2, kk, b, g = _rw_prep(pr, w["mu"], gate_pars)
    y, rw_states, rw_sa, g_up2, g_down2 = _rw_core_fwd(rw, dec, k2, kk, b, gather=blocks["late"])
    w = {**w, **_late_weights(g_up2, g_down2)}
    o_rw = _rw_post(y, rw, k2, g, post_pars)
    y_gla, y_rw, merged = _branch_merge(o_gla, o_rw, w["wb"], pg, w["gate_b"])
    x2 = _matmul(merged, w["wo"], "nn", T, D, D, 1024, 1024, D, "out_proj", res=x1)
    x3, ffn2 = _ffn_fwd(x2, w["g3"], w["wg2"], w["wu2"], w["wd2"], "2")
    dx3, loss, d_gf = _loss_bwd(x3, target, w["gf"])

    grads = {"gf": d_gf}
    dx2, grads["g3"], grads["up2"], grads["wd2"] = _ffn_bwd(
        dx3, x2, w["g3"], w["wg2"], w["wu2"], w["wd2"], ffn2, "2")
    dm = _matmul(dx2, w["wo"], "nt", T, D, D, 1024, 1024, D, "out_proj_bwd")
    grads["wo"] = _matmul(merged, dx2, "tn", D, D, T, 1024, 1024, 2048, "out_proj_grad", out_dtype=BF16)
    dy_gla, dy_rw, dpg, grads["gate_b"] = _merge_bwd(dm, y_gla, y_rw, pg, w["gate_b"])
    do_gla = _matmul(dy_gla, w["wb"], "nt", T, GLA_V, D, 1024, 1024, D, "branch_gla_bwd")
    do_rw = _matmul(dy_rw, w["wb"], "nt", T, RW_W, D, 1024, 1024, D, "branch_rwkv_bwd", b_off=(1, 0))
    grads["wb"] = jnp.concatenate([
        _matmul(o_gla, dy_gla, "tn", GLA_V, D, T, 1024, 1024, 4096, "branch_gla_grad", out_dtype=BF16),
        _matmul(o_rw, dy_rw, "tn", RW_W, D, T, 1024, 1024, 4096, "branch_rwkv_grad", out_dtype=BF16)], axis=0)
    dy, dr2, dv2, dk2b, dg, grads["lnx_w"], grads["lnx_b"], grads["r_k"] = _rw_post_bwd(y, rw, k2, g, post_pars, do_rw)
    early = _late_grad_parts(grads)
    received = {}
    dr1, dw, dk2a, dv1, dkk, db, received["up2"], received["down2"] = _rw_core_bwd(
        rw, dec, k2, kk, b, rw_states, rw_sa, dy, exchange=early)
    drw, grads["mu"], grads["w0"], grads["w_w2"], grads["a0"], grads["w_a2"], grads["w_g2"], grads["k_k"], grads["k_a"] = (
        _rw_prep_bwd(pr, w["mu"], gate_pars, (dr1, dr2), (dv1, dv2), dw, (dk2a, dk2b), dkk, db, dg))
    dpr = _shift_bwd(drw, w["mu"])
    do_raw, dr_gla, grads["gn"] = _gla_post_bwd(o_raw, pa, w["gn"], ind4, ind4_t, do_gla)
    dq, dk, dv, dla = _gla_core_bwd(pa, la, ltri, gla_states, do_raw)
    da_down, grads["gla_w_a2"], grads["gla_b_a"] = _gla_prep_bwd(pa, w["gla_w_a2"], w["gla_b_a"], dla)
    dpa = jnp.concatenate([dq.astype(BF16), dk.astype(BF16), dv.astype(BF16), dr_gla, da_down,
                           jnp.zeros((T, PA_W - PA_USED), BF16)], axis=1)
    dp = jnp.concatenate([dpg, dpr, dpa], axis=1)
    d_win = _matmul(h2, dp, "tn", D, DIN_P, T, 1024, 1024, 4096, "proj_grad", out_dtype=BF16)
    d_win = _shard_cols(_unalign_proj(d_win)).reshape(NDEV // 2, 2, D, DIN_SH).swapaxes(0, 1)
    dh2, received["win"] = _matmul(dp, win, "nt", T, D, DIN_P, 1024, 1024, DIN_P // 4, "proj_bwd",
                                   comm=("quad", [_chip_sum(d_win, "win")]))
    dx1, grads["g2"] = _rms_bwd(x1, w["g2"], dh2, dx2, "rms_bwd_mix")
    dx, grads["g1"], received["up1"], received["wd1"] = _ffn_bwd(
        dx1, x, w["g1"], w["wg1"], w["wu1"], wd1, ffn1, "1", exchange=True)
    return loss, dx, grads, received


BIG = ("ffn1_wg", "ffn1_wu", "ffn1_wd", "w_in", "w_branch", "w_out", "ffn2_wg", "ffn2_wu", "ffn2_wd")
SMALL_SHARDED = ("gla_w_a2", "rwkv_w_w2", "rwkv_w_a2", "rwkv_w_g2")
REPLICATED = ("ffn1_norm", "mix_norm", "gla_b_a", "gla_gn_w", "rwkv_mu", "rwkv_w0", "rwkv_a0", "rwkv_k_k", "rwkv_k_a",
              "rwkv_r_k", "rwkv_lnx_w", "rwkv_lnx_b", "gate_b", "ffn2_norm", "final_norm")
WEIGHTS = ("ffn1_norm", "ffn1_wg", "ffn1_wu", "ffn1_wd", "mix_norm", "w_in", "gla_w_a2", "gla_b_a", "gla_gn_w",
           "rwkv_mu", "rwkv_w0", "rwkv_w_w2", "rwkv_a0", "rwkv_w_a2", "rwkv_w_g2", "rwkv_k_k", "rwkv_k_a", "rwkv_r_k",
           "rwkv_lnx_w", "rwkv_lnx_b", "gate_b", "w_branch", "w_out", "ffn2_norm", "ffn2_wg", "ffn2_wu", "ffn2_wd",
           "final_norm")


def _unshard_cols(g):
    return jnp.transpose(g, (1, 0, 2)).reshape(g.shape[1], NDEV * g.shape[2])


def _shard_cols(a):
    return jnp.transpose(a.reshape(a.shape[0], NDEV, a.shape[1] // NDEV), (1, 0, 2))


def _pad_rows(a, rows):
    return jnp.pad(a, ((0, rows - a.shape[0]), (0, 0)))


def _align_rw(a):
    c = 3 * RW_W
    z = jnp.zeros((a.shape[0], LORA_P - DECAY_LORA), a.dtype)
    return jnp.concatenate([a[:, :c], a[:, c:c + DECAY_LORA], z, a[:, c + DECAY_LORA:c + 2 * DECAY_LORA], z,
                            a[:, c + 2 * DECAY_LORA:]], axis=1)


def _unalign_rw(a):
    c = 3 * RW_W
    return jnp.concatenate([a[:, :c + DECAY_LORA], a[:, c + LORA_P:c + LORA_P + AAA_LORA], a[:, c + 2 * LORA_P:]], axis=1)


def _align_proj(a):
    gla = jnp.pad(a[:, :GLA_IN], ((0, 0), (0, PA_W - GLA_IN)))
    return jnp.concatenate([a[:, GLA_IN + RW_IN:], _align_rw(a[:, GLA_IN:GLA_IN + RW_IN]), gla], axis=1)


def _unalign_proj(a):
    return jnp.concatenate([a[:, PG_W + PR_W:PG_W + PR_W + GLA_IN], _unalign_rw(a[:, PG_W:PG_W + PR_W]), a[:, :PG_W]], axis=1)


def _layout_weights(gb, gs, rep):
    row = lambda n: rep[n].reshape(1, -1)
    return {
        "wg1": gb["ffn1_wg"], "wu1": gb["ffn1_wu"],
        "g1": row("ffn1_norm"), "g2": row("mix_norm"), "g3": row("ffn2_norm"), "gf": row("final_norm"),
        "gla_w_a2": _pad_rows(_unshard_cols(gs["gla_w_a2"]), LORA_P), "gla_b_a": row("gla_b_a"),
        "gn": jnp.tile(row("gla_gn_w"), (1, GLA_HEADS)),
        "mu": _align_rw(row("rwkv_mu")), "w0": row("rwkv_w0"), "a0": row("rwkv_a0"),
        "w_w2": _pad_rows(_unshard_cols(gs["rwkv_w_w2"]), LORA_P),
        "w_a2": _pad_rows(_unshard_cols(gs["rwkv_w_a2"]), LORA_P),
        "w_g2": _unshard_cols(gs["rwkv_w_g2"]),
        "k_k": row("rwkv_k_k"), "k_a": row("rwkv_k_a"), "r_k": row("rwkv_r_k"),
        "lnx_w": row("rwkv_lnx_w"), "lnx_b": row("rwkv_lnx_b"), "gate_b": row("gate_b"),
    }


LATE_ROWS = (("ffn2_wd", FSH), ("w_branch", (GLA_V + RW_W) // NDEV), ("w_out", D // NDEV))


def _late_weights(g_up, g_down):
    r1, r2 = LATE_ROWS[0][1], LATE_ROWS[0][1] + LATE_ROWS[1][1]
    return {"wg2": g_up[:, 0], "wu2": g_up[:, 1], "wd2": g_down[:, :r1],
            "wb": g_down[:, r1:r2].reshape(GLA_V + RW_W, D), "wo": g_down[:, r2:].reshape(D, D)}


def _late_grad_parts(g):
    return [g["up2"], jnp.concatenate([g["wd2"], g["wb"].reshape(NDEV, -1, D), g["wo"].reshape(NDEV, -1, D)], axis=1)]


def _layout_grads(g):
    return {
        "ffn1_norm": g["g1"], "mix_norm": g["g2"], "ffn2_norm": g["g3"], "final_norm": g["gf"],
        "gla_w_a2": g["gla_w_a2"][:GLA_LORA], "gla_b_a": g["gla_b_a"],
        "gla_gn_w": jnp.sum(g["gn"].reshape(GLA_HEADS, GLA_DV), axis=0, keepdims=True),
        "rwkv_mu": _unalign_rw(g["mu"]), "rwkv_w0": g["w0"], "rwkv_a0": g["a0"],
        "rwkv_w_w2": g["w_w2"][:DECAY_LORA], "rwkv_w_a2": g["w_a2"][:AAA_LORA], "rwkv_w_g2": g["w_g2"],
        "rwkv_k_k": g["k_k"], "rwkv_k_a": g["k_a"], "rwkv_r_k": g["r_k"],
        "rwkv_lnx_w": g["lnx_w"], "rwkv_lnx_b": g["lnx_b"], "gate_b": g["gate_b"],
    }


_MESH = pl.DeviceIdType.MESH
_ANY = pl.BlockSpec(memory_space=pl.ANY)


def _position():
    return lax.axis_index("x"), lax.axis_index("y"), lax.axis_index("c")


def _slot(p):
    return 4 * p[0] + 2 * p[1] + p[2]


def _comm_sems(n):
    if not n:
        return []
    return [pltpu.SemaphoreType.DMA((7 * n,)), pltpu.SemaphoreType.DMA((7 * n,)), pltpu.SemaphoreType.DMA((n,))]


def _gather_plan(ins, outs, send_sems, recv_sems, local_sems):
    n = len(ins)
    x, y, c = _position()
    me, sibling = (x, y, c), (x, y, 1 - c)
    chips = [(1 - x, y), (x, 1 - y), (1 - x, 1 - y)]

    def copy(a, k, block, to, src=None):
        dst = outs[a].at[_slot(block)]
        return pltpu.make_async_remote_copy(
            src_ref=dst if src is None else src, dst_ref=dst, send_sem=send_sems.at[7 * a + k],
            recv_sem=recv_sems.at[7 * a + k], device_id=to, device_id_type=_MESH)

    def local(a):
        return pltpu.make_async_copy(ins[a], outs[a].at[_slot(me)], local_sems.at[a])

    def own(a):
        return [copy(a, 0, me, sibling, src=ins[a])] + [copy(a, 1 + j, me, (*chip, c), src=ins[a]) for j, chip in enumerate(chips)]

    def start():
        for a in range(n):
            local(a).start()
            for cp in own(a):
                cp.start()

    def forward():
        for a in range(n):
            for j, chip in enumerate(chips):
                copy(a, 1 + j, (*chip, c), me).wait_recv()
                copy(a, 4 + j, (*chip, c), sibling).start()

    def finish():
        for a in range(n):
            copy(a, 0, sibling, me).wait_recv()
            for j, chip in enumerate(chips):
                copy(a, 4 + j, (*chip, 1 - c), me).wait_recv()
        for a in range(n):
            for cp in own(a) + [copy(a, 4 + j, (*chip, c), sibling) for j, chip in enumerate(chips)]:
                cp.wait_send()
            local(a).wait()

    return start, forward, finish


def _exchange_plan(ins, outs, send_sems, recv_sems, local_sems):
    n = len(ins)
    x, y, c = _position()
    me = (x, y, c)
    flip = lambda v, f: 1 - v if f else v
    peers = [(flip(x, fx), flip(y, fy), flip(c, fc))
             for fx, fy, fc in ((0, 0, 1), (1, 0, 0), (0, 1, 0), (1, 1, 0), (1, 0, 1), (0, 1, 1), (1, 1, 1))]

    def copy(a, k, src_slot, dst_slot):
        return pltpu.make_async_remote_copy(
            src_ref=ins[a].at[src_slot], dst_ref=outs[a].at[dst_slot], send_sem=send_sems.at[7 * a + k],
            recv_sem=recv_sems.at[7 * a + k], device_id=peers[k], device_id_type=_MESH)

    def local(a):
        return pltpu.make_async_copy(ins[a].at[_slot(me)], outs[a].at[_slot(me)], local_sems.at[a])

    def start():
        for a in range(n):
            local(a).start()
            for k, peer in enumerate(peers):
                copy(a, k, _slot(peer), _slot(me)).start()

    def finish():
        for a in range(n):
            for k, peer in enumerate(peers):
                copy(a, k, _slot(peer), _slot(peer)).wait_recv()
        for a in range(n):
            for k, peer in enumerate(peers):
                copy(a, k, _slot(peer), _slot(me)).wait_send()
            local(a).wait()

    return start, finish


def _sibling_plan(ins, outs, send_sems, recv_sems, local_sems):
    x, y, c = _position()

    def copy(a):
        return pltpu.make_async_remote_copy(
            src_ref=ins[a].at[1 - c], dst_ref=outs[a], send_sem=send_sems.at[7 * a], recv_sem=recv_sems.at[7 * a],
            device_id=(x, y, 1 - c), device_id_type=_MESH)

    def start():
        for a in range(len(ins)):
            copy(a).start()

    def finish():
        for a in range(len(ins)):
            copy(a).wait()

    return start, finish


def _quad_plan(ins, outs, send_sems, recv_sems, local_sems):
    n = len(ins)
    x, y, c = _position()
    mine = 2 * x + y
    peers = [(1 - x, y), (x, 1 - y), (1 - x, 1 - y)]

    def copy(a, k, src_slot, dst_slot):
        return pltpu.make_async_remote_copy(
            src_ref=ins[a].at[src_slot], dst_ref=outs[a].at[dst_slot], send_sem=send_sems.at[7 * a + k],
            recv_sem=recv_sems.at[7 * a + k], device_id=(*peers[k], c), device_id_type=_MESH)

    def local(a):
        return pltpu.make_async_copy(ins[a].at[mine], outs[a].at[mine], local_sems.at[a])

    def start():
        for a in range(n):
            local(a).start()
            for k, (px, py) in enumerate(peers):
                copy(a, k, 2 * px + py, mine).start()

    def finish():
        for a in range(n):
            for k, (px, py) in enumerate(peers):
                copy(a, k, 2 * px + py, 2 * px + py).wait_recv()
        for a in range(n):
            for k, (px, py) in enumerate(peers):
                copy(a, k, 2 * px + py, mine).wait_send()
            local(a).wait()

    return start, finish


_PLANS = {"gather": (_gather_plan, lambda s: (NDEV,) + s), "exchange": (_exchange_plan, lambda s: s),
          "sibling": (_sibling_plan, lambda s: s[1:]), "quad": (_quad_plan, lambda s: s)}


def _exchange_now(kind, arrays, name):
    n = len(arrays)

    def body(*refs):
        for stage in _PLANS[kind][0](refs[:n], refs[n:2 * n], *refs[2 * n:]):
            stage()

    return pl.pallas_call(
        body, name=name, in_specs=[_ANY] * n, out_specs=[_ANY] * n,
        out_shape=[jax.ShapeDtypeStruct(_PLANS[kind][1](a.shape), a.dtype) for a in arrays], scratch_shapes=_comm_sems(n),
    )(*arrays)


def _chip_sum(parts, name):
    c = lax.axis_index("c")
    (theirs,) = _exchange_now("sibling", [parts], "chip_send_" + name)
    mine = lax.dynamic_index_in_dim(parts, c, axis=0, keepdims=False)
    shape = mine.shape
    rows = shape[-2]
    for d in shape[1:-2]:
        rows *= d
    flat = (4, rows, shape[-1])
    tr = 512

    def body(a_ref, b_ref, o_ref):
        o_ref[...] = (a_ref[...].astype(F32) + b_ref[...].astype(F32)).astype(BF16)

    blk = pl.BlockSpec((None, tr, shape[-1]), lambda s, r: (s, r, 0))
    out = pl.pallas_call(
        body, name="chip_sum_" + name, grid=(4, rows // tr), in_specs=[blk, blk], out_specs=blk,
        out_shape=jax.ShapeDtypeStruct(flat, BF16), compiler_params=_params(("parallel", "parallel")),
    )(mine.reshape(flat), theirs.reshape(flat))
    return out.reshape(shape)


def _adamw_math(w, g, m, v):
    m = ADAM_B1 * m + (1.0 - ADAM_B1) * g
    v = ADAM_B2 * v + (1.0 - ADAM_B2) * (g * g)
    m_hat = m / (1.0 - ADAM_B1 ** ADAM_STEP)
    v_hat = v / (1.0 - ADAM_B2 ** ADAM_STEP)
    delta = -ADAM_LR * (m_hat / (jnp.sqrt(v_hat) + ADAM_EPS) + ADAM_WD * w)
    return delta, m, v


def _sum_slots(ref):
    total = ref[0].astype(F32)
    for s in range(1, ref.shape[0]):
        total = total + ref[s].astype(F32)
    return total


def _adamw(parts, w, m, v, tr, name, stack_index=None, row_block_offset=0, transposed=False):
    _, R, C = w.shape

    def body(p_ref, w_ref, m_ref, v_ref, g_ref, d_ref, nm_ref, nv_ref):
        g = _sum_slots(p_ref)
        g = g.T if transposed else g
        g_ref[...] = g
        d_ref[...], nm_ref[...], nv_ref[...] = _adamw_math(w_ref[...], g, m_ref[...], v_ref[...])

    if stack_index is None:
        p_spec = pl.BlockSpec((parts.shape[0], tr, C), lambda r: (0, row_block_offset + r, 0))
    else:
        p_spec = pl.BlockSpec((parts.shape[0], None, tr, C), lambda r: (0, stack_index, r, 0))
    if transposed:
        blk = pl.BlockSpec((None, C, tr), lambda r: (0, 0, r))
        out = jax.ShapeDtypeStruct((1, C, R), F32)
        w, m, v = (jnp.swapaxes(a, 1, 2) for a in (w, m, v))
    else:
        blk = pl.BlockSpec((None, tr, C), lambda r: (0, r, 0))
        out = jax.ShapeDtypeStruct((1, R, C), F32)
    res = pl.pallas_call(
        body, name=name, grid=(R // tr,), in_specs=[p_spec, blk, blk, blk], out_specs=[blk] * 4, out_shape=(out,) * 4,
        compiler_params=_params(("parallel",)),
    )(parts, w, m, v)
    return [jnp.swapaxes(a, 1, 2) for a in res] if transposed else res


def _sum_gathered(parts):
    _, R, C = parts.shape

    def body(p_ref, o_ref):
        o_ref[...] = _sum_slots(p_ref)

    return pl.pallas_call(body, name="small_grad_sum", out_shape=jax.ShapeDtypeStruct((R, C), F32),
                          compiler_params=_params())(parts)


def _adamw_small(w, g, m, v):
    def body(w_ref, g_ref, m_ref, v_ref, d_ref, nm_ref, nv_ref):
        d_ref[...], nm_ref[...], nv_ref[...] = _adamw_math(w_ref[...], g_ref[...], m_ref[...], v_ref[...])

    out = jax.ShapeDtypeStruct(w.shape, F32)
    return pl.pallas_call(body, name="adamw_small", out_shape=(out,) * 3, compiler_params=_params())(w, g, m, v)


def _pack(pieces, rows):
    flat = jnp.concatenate([p.reshape(-1) for p in pieces])
    return jnp.pad(flat, (0, rows * 128 - flat.shape[0])).reshape(rows, 128)


def _unpack(packed, shapes):
    flat = packed.reshape(-1)
    out, off = [], 0
    for s in shapes:
        size = 1
        for d in s:
            size *= d
        out.append(flat[off:off + size].reshape(s))
        off += size
    return out


def _rows_for(shapes, extra=0):
    total = extra
    for s in shapes:
        size = 1
        for d in s:
            size *= d
        total += size
    return -(-total // 1024) * 8


def kernel(x, ffn1_norm, ffn1_wg, ffn1_wu, ffn1_wd, mix_norm, w_in, gla_w_a2, gla_b_a, gla_gn_w, rwkv_mu, rwkv_w0, rwkv_w_w2, rwkv_a0, rwkv_w_a2, rwkv_w_g2, rwkv_k_k, rwkv_k_a, rwkv_r_k, rwkv_lnx_w, rwkv_lnx_b, gate_b, w_branch, w_out, ffn2_norm, ffn2_wg, ffn2_wu, ffn2_wd, final_norm, loss_target, m_ffn1_norm, m_ffn1_wg, m_ffn1_wu, m_ffn1_wd, m_mix_norm, m_w_in, m_gla_w_a2, m_gla_b_a, m_gla_gn_w, m_rwkv_mu, m_rwkv_w0, m_rwkv_w_w2, m_rwkv_a0, m_rwkv_w_a2, m_rwkv_w_g2, m_rwkv_k_k, m_rwkv_k_a, m_rwkv_r_k, m_rwkv_lnx_w, m_rwkv_lnx_b, m_gate_b, m_w_branch, m_w_out, m_ffn2_norm, m_ffn2_wg, m_ffn2_wu, m_ffn2_wd, m_final_norm, v_ffn1_norm, v_ffn1_wg, v_ffn1_wu, v_ffn1_wd, v_mix_norm, v_w_in, v_gla_w_a2, v_gla_b_a, v_gla_gn_w, v_rwkv_mu, v_rwkv_w0, v_rwkv_w_w2, v_rwkv_a0, v_rwkv_w_a2, v_rwkv_w_g2, v_rwkv_k_k, v_rwkv_k_a, v_rwkv_r_k, v_rwkv_lnx_w, v_rwkv_lnx_b, v_gate_b, v_w_branch, v_w_out, v_ffn2_norm, v_ffn2_wg, v_ffn2_wu, v_ffn2_wd, v_final_norm):
    wts = dict(zip(WEIGHTS, (ffn1_norm, ffn1_wg, ffn1_wu, ffn1_wd, mix_norm, w_in, gla_w_a2, gla_b_a, gla_gn_w, rwkv_mu, rwkv_w0, rwkv_w_w2, rwkv_a0, rwkv_w_a2, rwkv_w_g2, rwkv_k_k, rwkv_k_a, rwkv_r_k, rwkv_lnx_w, rwkv_lnx_b, gate_b, w_branch, w_out, ffn2_norm, ffn2_wg, ffn2_wu, ffn2_wd, final_norm)))
    mom = dict(zip(WEIGHTS, (m_ffn1_norm, m_ffn1_wg, m_ffn1_wu, m_ffn1_wd, m_mix_norm, m_w_in, m_gla_w_a2, m_gla_b_a, m_gla_gn_w, m_rwkv_mu, m_rwkv_w0, m_rwkv_w_w2, m_rwkv_a0, m_rwkv_w_a2, m_rwkv_w_g2, m_rwkv_k_k, m_rwkv_k_a, m_rwkv_r_k, m_rwkv_lnx_w, m_rwkv_lnx_b, m_gate_b, m_w_branch, m_w_out, m_ffn2_norm, m_ffn2_wg, m_ffn2_wu, m_ffn2_wd, m_final_norm)))
    var = dict(zip(WEIGHTS, (v_ffn1_norm, v_ffn1_wg, v_ffn1_wu, v_ffn1_wd, v_mix_norm, v_w_in, v_gla_w_a2, v_gla_b_a, v_gla_gn_w, v_rwkv_mu, v_rwkv_w0, v_rwkv_w_w2, v_rwkv_a0, v_rwkv_w_a2, v_rwkv_w_g2, v_rwkv_k_k, v_rwkv_k_a, v_rwkv_r_k, v_rwkv_lnx_w, v_rwkv_lnx_b, v_gate_b, v_w_branch, v_w_out, v_ffn2_norm, v_ffn2_wg, v_ffn2_wu, v_ffn2_wd, v_final_norm)))
    two = lambda a: a.reshape(a.shape[-2:])

    bf = lambda n: two(wts[n]).astype(BF16)
    up1 = jnp.stack([bf("ffn1_wg"), bf("ffn1_wu")])
    lora = jnp.concatenate([jnp.pad(two(gla_w_a2), ((0, 0), (0, 128 - GLA_QK // NDEV)))] +
                           [two(wts[n]) for n in SMALL_SHARDED[1:]], axis=0)
    g_up1, g_lora = _exchange_now("gather", [up1, lora], "gather_weights")
    gb = {"ffn1_wg": g_up1[:, 0], "ffn1_wu": g_up1[:, 1]}
    gs = {"gla_w_a2": g_lora[:, :GLA_LORA, :GLA_QK // NDEV]}
    row = GLA_LORA
    for n in SMALL_SHARDED[1:]:
        gs[n] = g_lora[:, row:row + wts[n].shape[1]]
        row += wts[n].shape[1]
    w = _layout_weights(gb, gs, {n: wts[n] for n in REPLICATED})
    blocks = {"wd1": bf("ffn1_wd"), "win": bf("w_in"),
              "late": [jnp.stack([bf("ffn2_wg"), bf("ffn2_wu")]), jnp.concatenate([bf(n) for n, _ in LATE_ROWS], axis=0)]}

    loss_part, grad_x, grads, parts = _local_step(x[0], loss_target[0], w, blocks)
    small = _layout_grads(grads)
    result = {}
    state = lambda n: (wts[n], mom[n], var[n])
    for group, names in (("up1", ("ffn1_wg", "ffn1_wu")), ("up2", ("ffn2_wg", "ffn2_wu"))):
        for i, n in enumerate(names):
            result[n] = _adamw(parts[group], *state(n), 256, "adamw_" + n, stack_index=i, transposed=True)
    result["ffn1_wd"] = _adamw(parts["wd1"], *state("ffn1_wd"), 64, "adamw_ffn1_wd")
    row = 0
    for n, rows in LATE_ROWS:
        result[n] = _adamw(parts["down2"], *state(n), 64, "adamw_" + n, row_block_offset=row // 64)
        row += rows
    result["w_in"] = _adamw(parts["win"], *state("w_in"), 256, "adamw_w_in")

    small_names = [n for n in WEIGHTS if n not in BIG]
    full_shapes = [small[n].shape for n in small_names]
    rows_full = _rows_for(full_shapes, extra=128)
    packed = _pack([small[n] for n in small_names] + [loss_part], rows_full)
    (gathered,) = _exchange_now("gather", [packed], "gather_small_grads")
    total = _sum_gathered(gathered)
    *full_grads, loss_row = _unpack(total, full_shapes + [(1, 128)])
    me = _slot(_position())
    own = {}
    for n, g in zip(small_names, full_grads):
        if n in SMALL_SHARDED:
            cols = wts[n].shape[-1]
            g = lax.dynamic_slice_in_dim(g, me * cols, cols, axis=1)
        own[n] = g.reshape(wts[n].shape)
    own_shapes = [wts[n].shape for n in small_names]
    rows_own = _rows_for(own_shapes)
    pk = lambda d: _pack([d[n] for n in small_names], rows_own)
    d_s, m_s, v_s = _adamw_small(pk(wts), pk(own), pk(mom), pk(var))
    for n, d, m, v in zip(small_names, _unpack(d_s, own_shapes), _unpack(m_s, own_shapes), _unpack(v_s, own_shapes)):
        result[n] = (own[n], d, m, v)

    shaped = lambda n, k: result[n][k].reshape(wts[n].shape)
    return (loss_row[0, 0], grad_x[None],
            *[shaped(n, 0) for n in WEIGHTS], *[shaped(n, 1) for n in WEIGHTS],
            *[shaped(n, 2) for n in WEIGHTS], *[shaped(n, 3) for n in WEIGHTS])
```

```python
import functools

import jax
import jax.numpy as jnp
from jax import lax
from jax.experimental import pallas as pl
from jax.experimental.pallas import tpu as pltpu

F32 = jnp.float32
BF16 = jnp.bfloat16
HI = lax.Precision.HIGHEST

NDEV = 8
D = 2048
DFF = 5632
FSH = DFF // NDEV
CHUNK = 64
GLA_HEADS, GLA_DK, GLA_DV = 4, 128, 256
GLA_QK, GLA_V, GLA_LORA, GLA_TAU = 512, 1024, 16, 16.0
RW_HEADS, RW_HD, RW_W = 16, 64, 1024
DECAY_LORA, AAA_LORA, GATE_LORA = 96, 96, 256
GN_EPS = 64e-5
NORM_EPS = 1e-6
GLA_IN = 2 * GLA_QK + 2 * GLA_V + GLA_LORA
RW_IN = 3 * RW_W + DECAY_LORA + AAA_LORA + GATE_LORA
D_IN = GLA_IN + RW_IN + 2 * D
DIN_SH = D_IN // NDEV
PG_W = 2 * D
PR_W = 3584
PA_W = 3584
PA_USED = 2 * GLA_QK + 2 * GLA_V + 128
DIN_P = PG_W + PR_W + PA_W
LORA_P = 128

ADAM_LR, ADAM_B1, ADAM_B2, ADAM_EPS, ADAM_WD, ADAM_STEP = 0.001, 0.9, 0.999, 1e-08, 0.01, 10

VMEM_LIMIT = 56 * 1024 * 1024
RW_TB = 32
RW_G = 16
RW_NP = 8


def _params(sem=None, vmem=VMEM_LIMIT):
    return pltpu.CompilerParams(dimension_semantics=sem, vmem_limit_bytes=vmem)


def _pair_mask():
    return lax.broadcasted_iota(jnp.int32, (RW_HD, 2 * RW_HD), 1) < RW_HD


def _pair_rowsum(p, mask):
    tot = jnp.sum(p, axis=1, keepdims=True)
    first = jnp.sum(jnp.where(mask, p, 0.0), axis=1, keepdims=True)
    return first, tot - first


def _split_transposed(x_ref, q, dst_ref, base):
    xt = x_ref[:, 128 * q:128 * (q + 1)].T
    for g in range(RW_TB // RW_G):
        dst_ref[base + g, :, 0:RW_G] = xt[:, g * RW_G:(g + 1) * RW_G]


def _pair_column(tile_ref, idx, i, mask):
    return jnp.where(mask, tile_ref[idx, 0:RW_HD, i:i + 1], tile_ref[idx, RW_HD:, i:i + 1])


def _rw_core_fwd(rw, w, k2, kk, b, gather=()):
    T = rw.shape[0]
    nb = T // RW_TB
    ng = RW_TB // RW_G
    NP = RW_NP
    nc = len(gather)
    npair = RW_HEADS // 2 // NP

    def body(r_ref, v_ref, w_ref, k_ref, kk_ref, b_ref, *rest):
        g_in, (y_ref, st_ref, sa_ref), g_out = rest[:nc], rest[nc:nc + 3], rest[nc + 3:2 * nc + 3]
        s_scr, vt_scr, yt_scr, rows_scr = rest[2 * nc + 3:2 * nc + 7]
        pair, blk_i = pl.program_id(0), pl.program_id(1)
        if nc:
            start, forward, finish = _gather_plan(g_in, g_out, *rest[2 * nc + 7:])
            pl.when((pair == 0) & (blk_i == 0))(start)
            pl.when((pair == 0) & (blk_i == nb // 2))(forward)

        @pl.when(pl.program_id(1) == 0)
        def _():
            s_scr[...] = jnp.zeros_like(s_scr)
            yt_scr[...] = jnp.zeros_like(yt_scr)

        mask = _pair_mask()
        for q in range(NP):
            _split_transposed(v_ref, q, vt_scr, q * ng)
        R_, W_, K_, KK_, B_ = range(5)
        for a, ref in enumerate((r_ref, w_ref, k_ref, kk_ref, b_ref)):
            for q in range(NP):
                rows_scr[a * NP + q] = ref[:, 128 * q:128 * (q + 1)]

        def group(g, states):
            states = list(states)
            for i in range(RW_G):
                t = g * RW_G + i
                row = lambda a, q: rows_scr[a * NP + q, pl.ds(t, 1), :]
                sums = [_pair_rowsum(states[q] * row(KK_, q), mask) for q in range(NP)]
                for q in range(NP):
                    sa = jnp.where(mask, *sums[q])
                    sa_ref[q, t] = sa
                    states[q] = (states[q] * row(W_, q) - sa * row(B_, q)
                                 + _pair_column(vt_scr, q * ng + g, i, mask) * row(K_, q))
                    st_ref[q, t] = states[q]
                outs = [_pair_rowsum(states[q] * row(R_, q), mask) for q in range(NP)]
                for q in range(NP):
                    yt_scr[q * ng + g, 0:RW_HD, i:i + 1] = outs[q][0]
                    yt_scr[q * ng + g, RW_HD:, i:i + 1] = outs[q][1]
            return tuple(states)

        states = lax.fori_loop(0, ng, group, tuple(s_scr[q] for q in range(NP)))
        for q in range(NP):
            s_scr[q] = states[q]
            for g in range(ng):
                y_ref[g * RW_G:(g + 1) * RW_G, 128 * q:128 * (q + 1)] = yt_scr[q * ng + g].T[0:RW_G, :]
        if nc:
            pl.when((pair == npair - 1) & (blk_i == nb - 1))(finish)

    blk = lambda cb: pl.BlockSpec((RW_TB, 128 * NP), lambda p, i, cb=cb: (i, cb + p))
    tiles = pltpu.VMEM((NP * ng, 128, 128), F32)
    return pl.pallas_call(
        body, name="rw_core_fwd", grid=(npair, nb),
        in_specs=[blk(0), blk(2 * RW_W // (128 * NP)), blk(0), blk(0), blk(0), blk(0)] + [_ANY] * nc,
        out_specs=[blk(0)] + [pl.BlockSpec((NP, RW_TB, RW_HD, 128), lambda p, i: (p, i, 0, 0))] * 2 + [_ANY] * nc,
        out_shape=[jax.ShapeDtypeStruct((T, RW_W), F32)] + [jax.ShapeDtypeStruct((RW_HEADS // 2, T, RW_HD, 128), F32)] * 2
        + [jax.ShapeDtypeStruct((NDEV,) + a.shape, a.dtype) for a in gather],
        scratch_shapes=[pltpu.VMEM((NP, RW_HD, 128), F32), tiles, tiles, pltpu.VMEM((5 * NP, RW_TB, 128), F32)]
        + _comm_sems(nc),
        compiler_params=_params(("arbitrary", "arbitrary")),
    )(rw, rw, w, k2, kk, b, *gather)


def _rw_core_bwd(rw, w, k2, kk, b, states, sa_tiles, dy, exchange=()):
    T = rw.shape[0]
    nb = T // RW_TB
    ng = RW_TB // RW_G
    NP = RW_NP
    nc = len(exchange)
    npair = RW_HEADS // 2 // NP

    def body(r_ref, v_ref, w_ref, k_ref, kk_ref, b_ref, dy_ref, st_ref, sp_ref, sa_ref, *rest):
        e_in, e_out = rest[:nc], rest[nc + 6:2 * nc + 6]
        dr_ref, dw_ref, dk_ref, dv_ref, dkk_ref, db_ref = rest[nc:nc + 6]
        ds_scr, vt_scr, dyt_scr, dvt_scr, rows_scr, out_scr = rest[2 * nc + 6:2 * nc + 12]
        step = pl.program_id(1)
        if nc:
            start, finish = _exchange_plan(e_in, e_out, *rest[2 * nc + 12:])
            pl.when((pl.program_id(0) == 0) & (step == 0))(start)

        @pl.when(step == 0)
        def _():
            ds_scr[...] = jnp.zeros_like(ds_scr)
            dvt_scr[...] = jnp.zeros_like(dvt_scr)

        mask = _pair_mask()
        for q in range(NP):
            _split_transposed(v_ref, q, vt_scr, q * ng)
            _split_transposed(dy_ref, q, dyt_scr, q * ng)
        R_, W_, K_, KK_, B_ = range(5)
        for a, ref in enumerate((r_ref, w_ref, k_ref, kk_ref, b_ref)):
            for q in range(NP):
                rows_scr[a * NP + q] = ref[:, 128 * q:128 * (q + 1)]

        def group(gg, grads):
            g = ng - 1 - gg
            grads = list(grads)
            pairs = range(NP)
            for i in reversed(range(RW_G)):
                t = g * RW_G + i
                row = lambda a, q: rows_scr[a * NP + q, pl.ds(t, 1), :]

                def put(a, q, value):
                    out_scr[a * NP + q, pl.ds(t, 1), :] = value

                s_old = [st_ref[q, jnp.maximum(t - 1, 0)] for q in pairs]
                if i == 0:
                    s_old = [jnp.where(g == 0, jnp.where(step == nb - 1, 0.0, sp_ref[q, 0]), s_old[q]) for q in pairs]
                dycol = [_pair_column(dyt_scr, q * ng + g, i, mask) for q in pairs]
                dS = [grads[q] + dycol[q] * row(R_, q) for q in pairs]
                m = [_pair_rowsum(dS[q] * row(B_, q), mask) for q in pairs]
                dv = [_pair_rowsum(dS[q] * row(K_, q), mask) for q in pairs]
                for q in pairs:
                    put(R_, q, jnp.sum(st_ref[q, t] * dycol[q], axis=0, keepdims=True))
                    put(W_, q, jnp.sum(dS[q] * s_old[q], axis=0, keepdims=True))
                    put(K_, q, jnp.sum(dS[q] * _pair_column(vt_scr, q * ng + g, i, mask), axis=0, keepdims=True))
                for q in pairs:
                    dsa = -jnp.where(mask, *m[q])
                    grads[q] = dS[q] * row(W_, q) + dsa * row(KK_, q)
                    put(KK_, q, jnp.sum(s_old[q] * dsa, axis=0, keepdims=True))
                    put(B_, q, -jnp.sum(dS[q] * sa_ref[q, t], axis=0, keepdims=True))
                    dvt_scr[q * ng + g, 0:RW_HD, i:i + 1] = dv[q][0]
                    dvt_scr[q * ng + g, RW_HD:, i:i + 1] = dv[q][1]
            return tuple(grads)

        grads = lax.fori_loop(0, ng, group, tuple(ds_scr[q] for q in range(NP)))
        for q in range(NP):
            ds_scr[q] = grads[q]
            for a, ref in enumerate((dr_ref, dw_ref, dk_ref, dkk_ref, db_ref)):
                ref[:, 128 * q:128 * (q + 1)] = out_scr[a * NP + q]
            for g in range(ng):
                dv_ref[g * RW_G:(g + 1) * RW_G, 128 * q:128 * (q + 1)] = dvt_scr[q * ng + g].T[0:RW_G, :]
        if nc:
            pl.when((pl.program_id(0) == npair - 1) & (step == nb - 1))(finish)

    blk = lambda cb: pl.BlockSpec((RW_TB, 128 * NP), lambda p, i, cb=cb: (nb - 1 - i, cb + p))
    st_spec = pl.BlockSpec((NP, RW_TB, RW_HD, 128), lambda p, i: (p, nb - 1 - i, 0, 0))
    sp_spec = pl.BlockSpec((NP, 1, RW_HD, 128), lambda p, i: (p, jnp.maximum((nb - 1 - i) * RW_TB - 1, 0), 0, 0))
    out = jax.ShapeDtypeStruct((T, RW_W), F32)
    tiles = pltpu.VMEM((NP * ng, 128, 128), F32)
    return pl.pallas_call(
        body, name="rw_core_bwd", grid=(npair, nb),
        in_specs=[blk(0), blk(2 * RW_W // (128 * NP)), blk(0), blk(0), blk(0), blk(0), blk(0), st_spec, sp_spec, st_spec]
        + [_ANY] * nc,
        out_specs=[blk(0)] * 6 + [_ANY] * nc,
        out_shape=[out] * 6 + [jax.ShapeDtypeStruct(a.shape, a.dtype) for a in exchange],
        scratch_shapes=[pltpu.VMEM((NP, RW_HD, 128), F32), tiles, tiles, tiles,
                        pltpu.VMEM((5 * NP, RW_TB, 128), F32), pltpu.VMEM((5 * NP, RW_TB, 128), F32)] + _comm_sems(nc),
        compiler_params=_params(("arbitrary", "arbitrary")),
    )(rw, rw, w, k2, kk, b, dy, states, states, sa_tiles, *exchange)


GLA_CB = 8


def _gla_chunk(s_t, q, k, v, la, ltri):
    cum = jnp.dot(ltri, la, precision=HI, preferred_element_type=F32)
    total = jnp.sum(la, axis=0, keepdims=True)
    kdec = k * jnp.exp(total - cum)
    u_t = _bdot(v, kdec, _TN)
    s_t = jnp.exp(total) * s_t + u_t
    o = _bdot(q * (GLA_DK ** -0.5), s_t, _NT)
    return s_t, o


def _gla_core_fwd(pa, la, ltri):
    T = pa.shape[0]
    cb = min(GLA_CB, T // CHUNK)
    rows = cb * CHUNK
    nsteps = T // rows

    def body(q_ref, k_ref, v_ref, la_ref, ltri_ref, o_ref, st_ref, s_scr):
        @pl.when(pl.program_id(0) == 0)
        def _():
            s_scr[...] = jnp.zeros_like(s_scr)

        def chunk(c, states):
            sl = pl.ds(pl.multiple_of(c * CHUNK, CHUNK), CHUNK)
            out = []
            for h in range(GLA_HEADS):
                qk, vv = slice(GLA_DK * h, GLA_DK * (h + 1)), slice(GLA_DV * h, GLA_DV * (h + 1))
                s_t, o = _gla_chunk(states[h], q_ref[sl, qk], k_ref[sl, qk], v_ref[sl, vv], la_ref[sl, qk], ltri_ref[...])
                o_ref[sl, vv] = o
                st_ref[h, c] = s_t
                out.append(s_t)
            return tuple(out)

        states = lax.fori_loop(0, cb, chunk, tuple(s_scr[h] for h in range(GLA_HEADS)))
        for h in range(GLA_HEADS):
            s_scr[h] = states[h]

    qk = lambda cb_: pl.BlockSpec((rows, GLA_QK), lambda i, cb_=cb_: (i, cb_))
    return pl.pallas_call(
        body, name="gla_core_fwd", grid=(nsteps,),
        in_specs=[qk(0), qk(1), pl.BlockSpec((rows, GLA_V), lambda i: (i, 1)), qk(0),
                  pl.BlockSpec((CHUNK, CHUNK), lambda i: (0, 0))],
        out_specs=[pl.BlockSpec((rows, GLA_V), lambda i: (i, 0)),
                   pl.BlockSpec((GLA_HEADS, cb, GLA_DV, GLA_DK), lambda i: (0, i, 0, 0))],
        out_shape=(jax.ShapeDtypeStruct((T, GLA_V), F32),
                   jax.ShapeDtypeStruct((GLA_HEADS, T // CHUNK, GLA_DV, GLA_DK), F32)),
        scratch_shapes=[pltpu.VMEM((GLA_HEADS, GLA_DV, GLA_DK), F32)],
        compiler_params=_params(("arbitrary",)),
    )(pa, pa, pa, la, ltri)


def _gla_core_bwd(pa, la, ltri, states, do):
    T = pa.shape[0]
    cb = min(GLA_CB, T // CHUNK)
    rows = cb * CHUNK
    nsteps = T // rows

    def body(q_ref, k_ref, v_ref, la_ref, ltri_ref, st_ref, sp_ref, do_ref,
             dq_ref, dk_ref, dv_ref, dla_ref, ds_scr):
        step = pl.program_id(0)

        @pl.when(step == 0)
        def _():
            ds_scr[...] = jnp.zeros_like(ds_scr)

        def chunk(cc, grads):
            c = cb - 1 - cc
            sl = pl.ds(pl.multiple_of(c * CHUNK, CHUNK), CHUNK)
            out = []
            for h in range(GLA_HEADS):
                qk, vv = slice(GLA_DK * h, GLA_DK * (h + 1)), slice(GLA_DV * h, GLA_DV * (h + 1))
                s_before = jnp.where(step == nsteps - 1, 0.0, sp_ref[h, 0])
                s_prev = jnp.where(c == 0, s_before, st_ref[h, jnp.maximum(c - 1, 0)])
                _, vjp = jax.vjp(functools.partial(_gla_chunk, ltri=ltri_ref[...]),
                                 s_prev, q_ref[sl, qk], k_ref[sl, qk], v_ref[sl, vv], la_ref[sl, qk])
                ds_prev, dq, dk, dv, dla = vjp((grads[h], do_ref[sl, vv]))
                dq_ref[sl, qk] = dq
                dk_ref[sl, qk] = dk
                dv_ref[sl, vv] = dv
                dla_ref[sl, qk] = dla
                out.append(ds_prev)
            return tuple(out)

        grads = lax.fori_loop(0, cb, chunk, tuple(ds_scr[h] for h in range(GLA_HEADS)))
        for h in range(GLA_HEADS):
            ds_scr[h] = grads[h]

    r = lambda i: nsteps - 1 - i
    qk = lambda cb_: pl.BlockSpec((rows, GLA_QK), lambda i, cb_=cb_: (r(i), cb_))
    o512 = pl.BlockSpec((rows, GLA_QK), lambda i: (r(i), 0))
    o1024 = pl.BlockSpec((rows, GLA_V), lambda i: (r(i), 0))
    return pl.pallas_call(
        body, name="gla_core_bwd", grid=(nsteps,),
        in_specs=[qk(0), qk(1), pl.BlockSpec((rows, GLA_V), lambda i: (r(i), 1)), qk(0),
                  pl.BlockSpec((CHUNK, CHUNK), lambda i: (0, 0)),
                  pl.BlockSpec((GLA_HEADS, cb, GLA_DV, GLA_DK), lambda i: (0, r(i), 0, 0)),
                  pl.BlockSpec((GLA_HEADS, 1, GLA_DV, GLA_DK), lambda i: (0, jnp.maximum(r(i) * cb - 1, 0), 0, 0)),
                  o1024],
        out_specs=[o512, o512, o1024, o512],
        out_shape=(jax.ShapeDtypeStruct((T, GLA_QK), F32), jax.ShapeDtypeStruct((T, GLA_QK), F32),
                   jax.ShapeDtypeStruct((T, GLA_V), F32), jax.ShapeDtypeStruct((T, GLA_QK), F32)),
        scratch_shapes=[pltpu.VMEM((GLA_HEADS, GLA_DV, GLA_DK), F32)],
        compiler_params=_params(("arbitrary",)),
    )(pa, pa, pa, la, ltri, states, states, do)


def _rowwise(fn, name, T, tm, rows, pars, row_outs, acc_outs):
    nr, npar, nro = len(rows), len(pars), len(row_outs)
    tm = min(tm, T)
    nsteps = T // tm

    def body(*refs):
        i = pl.program_id(0)
        ins = [r[...] for r in refs[:nr + npar]]
        outs, accs = fn(i, *ins)
        for r, o in zip(refs[nr + npar:nr + npar + nro], outs):
            r[...] = o.astype(r.dtype)
        for r, a in zip(refs[nr + npar + nro:], accs):
            @pl.when(i == 0)
            def _(r=r, a=a):
                r[...] = a

            @pl.when(i > 0)
            def _(r=r, a=a):
                r[...] += a

    def rspec(width, cb, kind):
        if kind == "cur":
            return pl.BlockSpec((tm, width), lambda i: (i, cb))
        if kind == "prev":
            return pl.BlockSpec((8, width), lambda i: (jnp.maximum(i * (tm // 8) - 1, 0), cb))
        return pl.BlockSpec((8, width), lambda i: (jnp.minimum((i + 1) * (tm // 8), T // 8 - 1), cb))

    in_specs = [rspec(w, cb, kind) for (_, w, cb, kind) in rows]
    in_specs += [pl.BlockSpec(p.shape, lambda i, nd=p.ndim: (0,) * nd) for p in pars]
    out_specs = [pl.BlockSpec((tm, w), lambda i: (i, 0)) for (w, _) in row_outs]
    out_specs += [pl.BlockSpec(s, lambda i, nd=len(s): (0,) * nd) for s in acc_outs]
    out_shape = [jax.ShapeDtypeStruct((T, w), dt) for (w, dt) in row_outs]
    out_shape += [jax.ShapeDtypeStruct(s, F32) for s in acc_outs]
    res = pl.pallas_call(
        body, name=name, grid=(nsteps,), in_specs=in_specs, out_specs=out_specs, out_shape=out_shape,
        compiler_params=_params(("arbitrary",)),
    )(*[r[0] for r in rows], *pars)
    return res


def _cur(a, width=None, cb=0):
    return (a, a.shape[1] if width is None else width, cb, "cur")


def _sigmoid(x):
    return 1.0 / (1.0 + jnp.exp(-x))


def _silu(x):
    return x * _sigmoid(x)


def _softplus(x):
    return jnp.maximum(x, 0.0) + jnp.log(1.0 + jnp.exp(-jnp.abs(x)))


def _rms(x, g):
    return x * lax.rsqrt(jnp.mean(x * x, axis=-1, keepdims=True) + NORM_EPS) * g


def _dot_hi(a, b):
    return jnp.dot(a, b, precision=lax.Precision.HIGH, preferred_element_type=F32)


def _rms_fwd(x, g, name):
    T = x.shape[0]
    fn = lambda i, xb, gb: ((_rms(xb, gb),), ())
    return _rowwise(fn, name, T, 256, [_cur(x)], [g], [(D, BF16)], [])[0]


def _rms_bwd(x, g, dh, dres, name):
    T = x.shape[0]

    def fn(i, xb, dhb, drb, gb):
        _, vjp = jax.vjp(_rms, xb, gb)
        dx, dg = vjp(dhb)
        return (drb + dx,), (dg,)

    return _rowwise(fn, name, T, 256, [_cur(x), _cur(dh), _cur(dres)], [g], [(D, F32)], [(1, D)])


def _loss_bwd(x, target, g):
    T = x.shape[0]

    def loss(xb, gb, tb):
        err = _rms(xb, gb) - tb
        return 0.5 * jnp.sum(jnp.mean(err * err, axis=-1, keepdims=True))

    def fn(i, xb, tb, gb):
        val, (dx, dg) = jax.value_and_grad(loss, argnums=(0, 1))(xb, gb, tb)
        return (dx,), (jnp.full((1, 128), val, F32), dg)

    return _rowwise(fn, "loss_bwd", T, 256, [_cur(x), _cur(target)], [g], [(D, F32)], [(1, 128), (1, D)])


def _gla_la(a_down, w_a2, b_a):
    return -_softplus(-(_bdot(a_down, w_a2, _NN) + b_a)) * (1.0 / GLA_TAU)


def _gla_prep(pa, w_a2, b_a):
    T = pa.shape[0]
    fn = lambda i, ab, wb, bb: ((_gla_la(ab, wb, bb),), ())
    return _rowwise(fn, "gla_prep", T, 512, [_cur(pa, LORA_P, (2 * GLA_QK + 2 * GLA_V) // LORA_P)], [w_a2, b_a],
                    [(GLA_QK, F32)], [])[0]


def _gla_prep_bwd(pa, w_a2, b_a, dla):
    T = pa.shape[0]

    def fn(i, ab, dlab, wb, bb):
        _, vjp = jax.vjp(_gla_la, ab, wb, bb)
        da, dw, db = vjp(dlab)
        return (da,), (dw, db)

    return _rowwise(fn, "gla_prep_bwd", T, 512, [_cur(pa, LORA_P, (2 * GLA_QK + 2 * GLA_V) // LORA_P), _cur(dla)],
                    [w_a2, b_a], [(LORA_P, BF16)], [(LORA_P, GLA_QK), (1, GLA_QK)])


def _gla_out(o, r, gn, ind, ind_t):
    ms = _dot_hi(_dot_hi(o * o, ind) * (1.0 / GLA_DV), ind_t)
    return o * lax.rsqrt(ms + NORM_EPS) * gn * _silu(r)


def _gla_post(o_raw, pa, gn, ind, ind_t):
    T = pa.shape[0]
    fn = lambda i, ob, rb, gb, a, b: ((_gla_out(ob, rb, gb, a, b),), ())
    return _rowwise(fn, "gla_post", T, 256, [_cur(o_raw), _cur(pa, GLA_V, 2)], [gn, ind, ind_t], [(GLA_V, BF16)], [])[0]


def _gla_post_bwd(o_raw, pa, gn, ind, ind_t, do):
    T = pa.shape[0]

    def fn(i, ob, rb, dob, gb, a, b):
        _, vjp = jax.vjp(lambda o, r, g: _gla_out(o, r, g, a, b), ob, rb, gb)
        d_o, d_r, d_g = vjp(dob)
        return (d_o, d_r), (d_g,)

    return _rowwise(fn, "gla_post_bwd", T, 256, [_cur(o_raw), _cur(pa, GLA_V, 2), _cur(do)], [gn, ind, ind_t],
                    [(GLA_V, F32), (GLA_V, BF16)], [(1, GLA_V)])


def _shift_rows(cur, prev8, i):
    first = jnp.where(i == 0, 0.0, prev8[7:8, :])
    rolled = pltpu.roll(cur, 1, 0)
    return jnp.where(lax.broadcasted_iota(jnp.int32, cur.shape, 0) == 0, first, rolled)


def _rw_gates(rw, w0, w_w2, a0, w_a2, w_g2, k_k, k_a, ind, ind_t):
    rk = rw[:, RW_W:2 * RW_W]
    wd = rw[:, 3 * RW_W:3 * RW_W + LORA_P]
    ad = rw[:, 3 * RW_W + LORA_P:3 * RW_W + 2 * LORA_P]
    gd = rw[:, 3 * RW_W + 2 * LORA_P:]
    w_raw = w0 + _bdot(jnp.tanh(wd), w_w2, _NN)
    w = jnp.exp(-jnp.exp(-_softplus(-w_raw) - 0.5))
    a = _sigmoid(a0 + _bdot(ad, w_a2, _NN))
    g = _bdot(_sigmoid(gd), w_g2, _NN)
    kk = rk * k_k
    kk = kk * _dot_hi(lax.rsqrt(jnp.maximum(_dot_hi(kk * kk, ind), 1e-24)), ind_t)
    k2 = rk * (1.0 + (a - 1.0) * k_a)
    return w, k2, kk, kk * a, g


def _rw_prep(pr, mu, gate_pars):
    T = pr.shape[0]

    def fn(i, cur, prev8, mub, *gp):
        rw = cur + mub * (_shift_rows(cur, prev8, i) - cur)
        return (rw,) + _rw_gates(rw, *gp), ()

    return _rowwise(fn, "rw_prep", T, 256, [_cur(pr), (pr, PR_W, 0, "prev")], [mu, *gate_pars],
                    [(PR_W, F32)] + [(RW_W, F32)] * 5, [])


def _rw_prep_bwd(pr, mu, gate_pars, d_r, d_v, d_w, d_k2, d_kk, d_b, d_g):
    T = pr.shape[0]
    rows = [_cur(pr), (pr, PR_W, 0, "prev")] + [_cur(x) for x in (*d_r, *d_v, d_w, *d_k2, d_kk, d_b, d_g)]
    acc = [(1, PR_W)] + [tuple(p.shape) for p in gate_pars[:-2]]

    def fn(i, cur, prev8, dr1, dr2, dv1, dv2, dw, dk1, dk2, dkk, db, dg, mub, *gp):
        sh = _shift_rows(cur, prev8, i)
        rw = cur + mub * (sh - cur)
        _, vjp = jax.vjp(lambda x, *p: _rw_gates(x, *p, gp[-2], gp[-1]), rw, *gp[:-2])
        grads = vjp((dw, dk1 + dk2, dkk, db, dg))
        zeros = jnp.zeros((cur.shape[0], PR_W - 3 * RW_W), F32)
        drw = grads[0] + jnp.concatenate([dr1 + dr2, jnp.zeros_like(dr1), dv1 + dv2, zeros], axis=1)
        dmu = jnp.sum(drw * (sh - cur), axis=0, keepdims=True)
        return (drw,), (dmu, *grads[1:])

    return _rowwise(fn, "rw_prep_bwd", T, 128, rows, [mu, *gate_pars], [(PR_W, F32)], acc)


def _shift_bwd(drw, mu):
    T = drw.shape[0]
    tm = min(256, T)

    def fn(i, cur, next8, mub):
        last = jnp.where(i == T // tm - 1, 0.0, next8[0:1, :])
        rolled = pltpu.roll(cur, cur.shape[0] - 1, 0)
        nxt = jnp.where(lax.broadcasted_iota(jnp.int32, cur.shape, 0) == cur.shape[0] - 1, last, rolled)
        return ((1.0 - mub) * cur + mub * nxt,), ()

    return _rowwise(fn, "shift_bwd", T, tm, [_cur(drw), (drw, PR_W, 0, "next")], [mu], [(PR_W, BF16)], [])[0]


def _rw_out(y, r, v, k2, g, lnx_w, lnx_b, r_k, ind, ind_t):
    mean = _dot_hi(_dot_hi(y, ind) * (1.0 / RW_HD), ind_t)
    yc = y - mean
    var = _dot_hi(_dot_hi(yc * yc, ind) * (1.0 / RW_HD), ind_t)
    yn = yc * lax.rsqrt(var + GN_EPS) * lnx_w + lnx_b
    bonus = _dot_hi(_dot_hi(r * k2 * r_k, ind), ind_t) * v
    return (yn + bonus) * g


def _rw_post(y, rw, k2, g, pars):
    T = y.shape[0]
    fn = lambda i, *a: ((_rw_out(*a),), ())
    return _rowwise(fn, "rw_post", T, 256, [_cur(y), _cur(rw, RW_W, 0), _cur(rw, RW_W, 2), _cur(k2), _cur(g)], pars,
                    [(RW_W, BF16)], [])[0]


def _rw_post_bwd(y, rw, k2, g, pars, do):
    T = y.shape[0]

    def fn(i, yb, rb, vb, kb, gb, dob, lw, lb, rk, ind, ind_t):
        _, vjp = jax.vjp(lambda *a: _rw_out(*a, ind, ind_t), yb, rb, vb, kb, gb, lw, lb, rk)
        gr = vjp(dob)
        return gr[:5], gr[5:]

    return _rowwise(fn, "rw_post_bwd", T, 256,
                    [_cur(y), _cur(rw, RW_W, 0), _cur(rw, RW_W, 2), _cur(k2), _cur(g), _cur(do)], pars,
                    [(RW_W, F32)] * 5, [(1, RW_W)] * 3)


def _merge_bwd(dm, y_gla, y_rw, pg, gate_b):
    T = dm.shape[0]

    def fn(i, dmb, ya, yr, p1, p2, gb):
        g1 = _sigmoid(p1 + gb[:, :D])
        g2 = _sigmoid(p2 + gb[:, D:])
        dp1 = dmb * ya * g1 * (1.0 - g1)
        dp2 = dmb * yr * g2 * (1.0 - g2)
        dp = jnp.concatenate([dp1, dp2], axis=1)
        return (dmb * g1, dmb * g2, dp), (jnp.sum(dp, axis=0, keepdims=True),)

    return _rowwise(fn, "merge_bwd", T, 256, [_cur(dm), _cur(y_gla), _cur(y_rw), _cur(pg, D, 0), _cur(pg, D, 1)],
                    [gate_b], [(D, BF16), (D, BF16), (PG_W, BF16)], [(1, PG_W)])


_NN = (((1,), (0,)), ((), ()))
_NT = (((1,), (1,)), ((), ()))
_TN = (((0,), (0,)), ((), ()))


def _bdot(a, b, dims):
    return lax.dot_general(a.astype(BF16), b.astype(BF16), dims, preferred_element_type=F32)


def _accumulate(k, nk, acc, part, finish):
    if nk == 1:
        finish(part)
        return

    @pl.when(k == 0)
    def _():
        acc[...] = part

    @pl.when(k > 0)
    def _():
        acc[...] += part

    @pl.when(k == nk - 1)
    def _():
        finish(acc[...])


def _call(body, comm, name, grid, in_specs, out_specs, out_shape, scratch_shapes, sem, operands):
    if comm is None:
        return pl.pallas_call(body, name=name, grid=grid, in_specs=in_specs, out_specs=out_specs, out_shape=out_shape,
                              scratch_shapes=scratch_shapes, compiler_params=_params(sem))(*operands)
    kind, arrays = comm
    nc, n_in, n_out, n_scr = len(arrays), len(in_specs), len(out_shape), len(scratch_shapes)
    total = 1
    for n in grid:
        total *= n

    def with_comm(*refs):
        own = refs[:n_in] + refs[n_in + nc:n_in + nc + n_out] + refs[n_in + 2 * nc + n_out:n_in + 2 * nc + n_out + n_scr]
        c_in, c_out, sems = refs[n_in:n_in + nc], refs[n_in + nc + n_out:n_in + 2 * nc + n_out], refs[-3:]
        step = 0
        for axis, n in enumerate(grid):
            step = step * n + pl.program_id(axis)
        start, *forward, finish = _PLANS[kind][0](c_in, c_out, *sems)
        pl.when(step == 0)(start)
        for stage in forward:
            pl.when(step == total // 2)(stage)
        body(*own)
        pl.when(step == total - 1)(finish)

    return pl.pallas_call(
        with_comm, name=name, grid=grid, in_specs=list(in_specs) + [_ANY] * nc, out_specs=list(out_specs) + [_ANY] * nc,
        out_shape=list(out_shape) + [jax.ShapeDtypeStruct(_PLANS[kind][1](a.shape), a.dtype) for a in arrays],
        scratch_shapes=list(scratch_shapes) + _comm_sems(nc), compiler_params=_params(("arbitrary",) * len(grid)),
    )(*operands, *arrays)


def _matmul(a, b, mode, M, N, K, tm, tn, tk, name, a_off=(0, 0), b_off=(0, 0), res=None, scale=1.0, out_dtype=F32,
            comm=None):
    tm, tn, tk = min(tm, M), min(tn, N), min(tk, K)
    nk = K // tk
    if mode == "nn":
        a_spec = pl.BlockSpec((tm, tk), lambda i, j, k: (i + a_off[0], k + a_off[1]))
        b_spec = pl.BlockSpec((tk, tn), lambda i, j, k: (k + b_off[0], j + b_off[1]))
        dims = _NN
    elif mode == "nt":
        a_spec = pl.BlockSpec((tm, tk), lambda i, j, k: (i + a_off[0], k + a_off[1]))
        b_spec = pl.BlockSpec((tn, tk), lambda i, j, k: (j + b_off[0], k + b_off[1]))
        dims = _NT
    else:
        a_spec = pl.BlockSpec((tk, tm), lambda i, j, k: (k + a_off[0], i + a_off[1]))
        b_spec = pl.BlockSpec((tk, tn), lambda i, j, k: (k + b_off[0], j + b_off[1]))
        dims = _TN
    o_spec = pl.BlockSpec((tm, tn), lambda i, j, k: (i, j))

    def body(a_ref, b_ref, *rest):
        r_ref = rest[0] if res is not None else None
        o_ref = rest[1] if res is not None else rest[0]
        acc = rest[-1] if nk > 1 else None

        def finish(total):
            total = total * scale if scale != 1.0 else total
            if r_ref is not None:
                total = r_ref[...] + total
            o_ref[...] = total.astype(out_dtype)

        _accumulate(pl.program_id(2), nk, acc, _bdot(a_ref[...], b_ref[...], dims), finish)

    out = _call(body, comm, name, (M // tm, N // tn, nk), [a_spec, b_spec] + ([o_spec] if res is not None else []),
                [o_spec], [jax.ShapeDtypeStruct((M, N), out_dtype)], [pltpu.VMEM((tm, tn), F32)] if nk > 1 else [],
                ("parallel", "parallel", "arbitrary"), [a, b] + ([res] if res is not None else []))
    return out[0] if comm is None else out


def _ffn_up(h, wg, wu, name, comm=None):
    T = h.shape[0]
    tm = min(1024, T)

    def body(h_ref, wg_ref, wu_ref, a_ref, u_ref, s_ref):
        hb = h_ref[...]
        a = _bdot(hb, wg_ref[...], _NN)
        u = _bdot(hb, wu_ref[...], _NN)
        a_ref[...] = a
        u_ref[...] = u
        s_ref[...] = (_silu(a) * u).astype(BF16)

    w_spec = pl.BlockSpec((None, D, FSH), lambda i, j: (j, 0, 0))
    o_spec = pl.BlockSpec((None, tm, FSH), lambda i, j: (j, i, 0))
    sh = lambda dt: jax.ShapeDtypeStruct((NDEV, T, FSH), dt)
    return _call(body, comm, name, (T // tm, NDEV), [pl.BlockSpec((tm, D), lambda i, j: (i, 0)), w_spec, w_spec],
                 [o_spec] * 3, [sh(F32), sh(F32), sh(BF16)], [], ("parallel", "arbitrary"), [h, wg, wu])


def _ffn_down(s, wd, x, name, comm=None):
    T = x.shape[0]
    tm, tn, sh = min(1024, T), 1024, 4

    def body(s_ref, wd_ref, x_ref, o_ref, acc):
        part = _bdot(s_ref[0], wd_ref[0], _NN)
        for q in range(1, sh):
            part = part + _bdot(s_ref[q], wd_ref[q], _NN)

        def finish(total):
            o_ref[...] = x_ref[...] + 0.5 * total

        _accumulate(pl.program_id(2), NDEV // sh, acc, part, finish)

    xo = pl.BlockSpec((tm, tn), lambda i, n, j: (i, n))
    out = _call(body, comm, name, (T // tm, D // tn, NDEV // sh),
                [pl.BlockSpec((sh, tm, FSH), lambda i, n, j: (j, i, 0)),
                 pl.BlockSpec((sh, FSH, tn), lambda i, n, j: (j, 0, n)), xo],
                [xo], [jax.ShapeDtypeStruct((T, D), F32)], [pltpu.VMEM((tm, tn), F32)],
                ("parallel", "parallel", "arbitrary"), [s, wd, x])
    return out[0] if comm is None else out


def _ffn_bwd_hidden(dx, wd, a, u, name):
    T = dx.shape[0]
    tm = min(1024, T)

    def body(dx_ref, wd_ref, a_ref, u_ref, da_ref, du_ref):
        ds = 0.5 * _bdot(dx_ref[...], wd_ref[...], _NT)
        av = a_ref[...]
        sg = _sigmoid(av)
        da_ref[...] = (ds * u_ref[...] * (sg * (1.0 + av * (1.0 - sg)))).astype(BF16)
        du_ref[...] = (ds * (av * sg)).astype(BF16)

    act = pl.BlockSpec((None, tm, FSH), lambda i, j: (j, i, 0))
    sh = jax.ShapeDtypeStruct((NDEV, T, FSH), BF16)
    return pl.pallas_call(
        body, name=name, grid=(T // tm, NDEV),
        in_specs=[pl.BlockSpec((tm, D), lambda i, j: (i, 0)), pl.BlockSpec((None, FSH, D), lambda i, j: (j, 0, 0)),
                  act, act],
        out_specs=[act, act], out_shape=(sh, sh),
        compiler_params=_params(("parallel", "arbitrary")),
    )(dx, wd, a, u)


def _ffn_bwd_input(da, du, wg, wu, name, comm=None):
    T = da.shape[1]
    tm, tn, sh = min(1024, T), 1024, 2

    def body(da_ref, du_ref, wg_ref, wu_ref, o_ref, acc):
        part = _bdot(da_ref[0], wg_ref[0], _NT) + _bdot(du_ref[0], wu_ref[0], _NT)
        for q in range(1, sh):
            part = part + _bdot(da_ref[q], wg_ref[q], _NT) + _bdot(du_ref[q], wu_ref[q], _NT)

        def finish(total):
            o_ref[...] = total

        _accumulate(pl.program_id(2), NDEV // sh, acc, part, finish)

    act = pl.BlockSpec((sh, tm, FSH), lambda i, n, j: (j, i, 0))
    wsp = pl.BlockSpec((sh, tn, FSH), lambda i, n, j: (j, n, 0))
    out = _call(body, comm, name, (T // tm, D // tn, NDEV // sh), [act, act, wsp, wsp],
                [pl.BlockSpec((tm, tn), lambda i, n, j: (i, n))], [jax.ShapeDtypeStruct((T, D), F32)],
                [pltpu.VMEM((tm, tn), F32)], ("parallel", "parallel", "arbitrary"), [da, du, wg, wu])
    return out[0] if comm is None else out


def _ffn_grad_up(h, da, du, name, comm=None, core_major=False):
    T = h.shape[0]
    tm, tk = 1024, min(4096, T)
    nk = T // tk

    def body(h_ref, da_ref, du_ref, o_ref, acc_a, acc_u):
        k = pl.program_id(2)
        hb = h_ref[...]
        for acc, ref, slot in ((acc_a, da_ref, 0), (acc_u, du_ref, 1)):
            def finish(total, slot=slot):
                o_ref[slot] = total.astype(BF16)

            _accumulate(k, nk, acc, _bdot(hb, ref[...], _TN), finish)

    act = pl.BlockSpec((None, tk, FSH), lambda j, i, t: (j, t, 0))
    if core_major:
        o_spec = pl.BlockSpec((None, None, 2, tm, FSH), lambda j, i, t: (j % 2, j // 2, 0, i, 0))
        o_shape = jax.ShapeDtypeStruct((2, NDEV // 2, 2, D, FSH), BF16)
    else:
        o_spec = pl.BlockSpec((None, 2, tm, FSH), lambda j, i, t: (j, 0, i, 0))
        o_shape = jax.ShapeDtypeStruct((NDEV, 2, D, FSH), BF16)
    out = _call(body, comm, name, (NDEV, D // tm, nk), [pl.BlockSpec((tk, tm), lambda j, i, t: (t, i)), act, act],
                [o_spec], [o_shape],
                [pltpu.VMEM((tm, FSH), F32), pltpu.VMEM((tm, FSH), F32)], ("parallel", "parallel", "arbitrary"), [h, da, du])
    return out[0] if comm is None else out


def _ffn_grad_down(s, dx, name):
    T = dx.shape[0]
    tn, tk = 1024, min(2048, T)
    nk = T // tk

    def body(s_ref, dx_ref, o_ref, acc):
        def finish(total):
            o_ref[...] = (0.5 * total).astype(BF16)

        _accumulate(pl.program_id(2), nk, acc, _bdot(s_ref[...], dx_ref[...], _TN), finish)

    return pl.pallas_call(
        body, name=name, grid=(NDEV, D // tn, nk),
        in_specs=[pl.BlockSpec((None, tk, FSH), lambda j, n, t: (j, t, 0)), pl.BlockSpec((tk, tn), lambda j, n, t: (t, n))],
        out_specs=pl.BlockSpec((None, FSH, tn), lambda j, n, t: (j, 0, n)),
        out_shape=jax.ShapeDtypeStruct((NDEV, DFF // NDEV, D), BF16),
        scratch_shapes=[pltpu.VMEM((FSH, tn), F32)],
        compiler_params=_params(("parallel", "parallel", "arbitrary")),
    )(s, dx)


def _branch_merge(o_gla, o_rw, wb, pg, gate_b):
    T = o_gla.shape[0]
    tm, tn = min(1024, T), 512

    def body(og_ref, or_ref, w1_ref, w2_ref, p1_ref, p2_ref, b1_ref, b2_ref, yg_ref, yr_ref, m_ref):
        yg = _bdot(og_ref[...], w1_ref[...], _NN)
        yr = _bdot(or_ref[...], w2_ref[...], _NN)
        yg_ref[...] = yg
        yr_ref[...] = yr
        m_ref[...] = (_sigmoid(p1_ref[...] + b1_ref[...]) * yg + _sigmoid(p2_ref[...] + b2_ref[...]) * yr).astype(BF16)

    nj = D // tn
    act = pl.BlockSpec((tm, GLA_V), lambda i, j: (i, 0))
    out = pl.BlockSpec((tm, tn), lambda i, j: (i, j))
    return pl.pallas_call(
        body, name="branch_merge", grid=(T // tm, nj),
        in_specs=[act, act, pl.BlockSpec((GLA_V, tn), lambda i, j: (0, j)), pl.BlockSpec((RW_W, tn), lambda i, j: (1, j)),
                  out, pl.BlockSpec((tm, tn), lambda i, j: (i, nj + j)),
                  pl.BlockSpec((1, tn), lambda i, j: (0, j)), pl.BlockSpec((1, tn), lambda i, j: (0, nj + j))],
        out_specs=[out, out, out],
        out_shape=(jax.ShapeDtypeStruct((T, D), F32), jax.ShapeDtypeStruct((T, D), F32), jax.ShapeDtypeStruct((T, D), BF16)),
        compiler_params=_params(("parallel", "arbitrary")),
    )(o_gla, o_rw, wb, wb, pg, pg, gate_b, gate_b)


def _head_indicator(width, heads):
    col = lax.broadcasted_iota(jnp.int32, (width, 128), 0) // (width // heads)
    ind = (col == lax.broadcasted_iota(jnp.int32, (width, 128), 1)).astype(F32)
    return ind, ind.T


def _ffn_fwd(x, g, wg, wu, wd, tag):
    h = _rms_fwd(x, g, "rms_" + tag)
    a, u, s = _ffn_up(h, wg, wu, "ffn_up_" + tag)
    return _ffn_down(s, wd, x, "ffn_down_" + tag), (h, a, u, s)


def _ffn_fwd_gathering(x, g, wg, wu, wd_block, next_block, tag):
    h = _rms_fwd(x, g, "rms_" + tag)
    a, u, s, wd = _ffn_up(h, wg, wu, "ffn_up_" + tag, comm=("gather", [wd_block]))
    y, gathered = _ffn_down(s, wd, x, "ffn_down_" + tag, comm=("gather", [next_block]))
    return y, (h, a, u, s), wd, gathered


def _ffn_bwd(dy, x, g, wg, wu, wd, saved, tag, exchange=False):
    h, a, u, s = saved
    dwd = _ffn_grad_down(s, dy, "ffn_grad_down_" + tag)
    da, du = _ffn_bwd_hidden(dy, wd, a, u, "ffn_bwd_hidden_" + tag)
    if exchange:
        dw_up, dwd = _ffn_grad_up(h, da, du, "ffn_grad_up_" + tag, comm=("exchange", [dwd]), core_major=True)
        dh, dw_up = _ffn_bwd_input(da, du, wg, wu, "ffn_bwd_input_" + tag, comm=("quad", [_chip_sum(dw_up, "up_" + tag)]))
    else:
        dw_up = _ffn_grad_up(h, da, du, "ffn_grad_up_" + tag)
        dh = _ffn_bwd_input(da, du, wg, wu, "ffn_bwd_input_" + tag)
    dx, dg = _rms_bwd(x, g, dh, dy, "rms_bwd_" + tag)
    return dx, dg, dw_up, dwd


def _local_step(x, target, w, blocks):
    T = x.shape[0]
    ind16, ind16_t = _head_indicator(RW_W, RW_HEADS)
    ind4, ind4_t = _head_indicator(GLA_V, GLA_HEADS)
    ltri = jnp.tril(jnp.ones((CHUNK, CHUNK), F32))
    gate_pars = [w["w0"], w["w_w2"], w["a0"], w["w_a2"], w["w_g2"], w["k_k"], w["k_a"], ind16, ind16_t]
    post_pars = [w["lnx_w"], w["lnx_b"], w["r_k"], ind16, ind16_t]

    x1, ffn1, wd1, g_proj = _ffn_fwd_gathering(x, w["g1"], w["wg1"], w["wu1"], blocks["wd1"], blocks["win"], "1")
    win = _align_proj(_unshard_cols(g_proj))
    h2 = _rms_fwd(x1, w["g2"], "rms_mix")
    proj = lambda n, off, name: _matmul(h2, win, "nn", T, n, D, 1024, 512, D, name, b_off=(0, off // 512))
    pg = proj(PG_W, 0, "proj_gate")
    pr = proj(PR_W, PG_W, "proj_rwkv")
    pa = proj(PA_W, PG_W + PR_W, "proj_gla")
    la = _gla_prep(pa, w["gla_w_a2"], w["gla_b_a"])
    o_raw, gla_states = _gla_core_fwd(pa, la, ltri)
    o_gla = _gla_post(o_raw, pa, w["gn"], ind4, ind4_t)
    rw, dec, k2, kk, b, g = _rw_prep(pr, w["mu"], gate_pars)
    y, rw_states, rw_sa, g_up2, g_down2 = _rw_core_fwd(rw, dec, k2, kk, b, gather=blocks["late"])
    w = {**w, **_late_weights(g_up2, g_down2)}
    o_rw = _rw_post(y, rw, k2, g, post_pars)
    y_gla, y_rw, merged = _branch_merge(o_gla, o_rw, w["wb"], pg, w["gate_b"])
    x2 = _matmul(merged, w["wo"], "nn", T, D, D, 1024, 1024, D, "out_proj", res=x1)
    x3, ffn2 = _ffn_fwd(x2, w["g3"], w["wg2"], w["wu2"], w["wd2"], "2")
    dx3, loss, d_gf = _loss_bwd(x3, target, w["gf"])

    grads = {"gf": d_gf}
    dx2, grads["g3"], grads["up2"], grads["wd2"] = _ffn_bwd(
        dx3, x2, w["g3"], w["wg2"], w["wu2"], w["wd2"], ffn2, "2")
    dm = _matmul(dx2, w["wo"], "nt", T, D, D, 1024, 1024, D, "out_proj_bwd")
    grads["wo"] = _matmul(merged, dx2, "tn", D, D, T, 1024, 1024, 2048, "out_proj_grad", out_dtype=BF16)
    dy_gla, dy_rw, dpg, grads["gate_b"] = _merge_bwd(dm, y_gla, y_rw, pg, w["gate_b"])
    do_gla = _matmul(dy_gla, w["wb"], "nt", T, GLA_V, D, 1024, 1024, D, "branch_gla_bwd")
    do_rw = _matmul(dy_rw, w["wb"], "nt", T, RW_W, D, 1024, 1024, D, "branch_rwkv_bwd", b_off=(1, 0))
    grads["wb"] = jnp.concatenate([
        _matmul(o_gla, dy_gla, "tn", GLA_V, D, T, 1024, 1024, 4096, "branch_gla_grad", out_dtype=BF16),
        _matmul(o_rw, dy_rw, "tn", RW_W, D, T, 1024, 1024, 4096, "branch_rwkv_grad", out_dtype=BF16)], axis=0)
    dy, dr2, dv2, dk2b, dg, grads["lnx_w"], grads["lnx_b"], grads["r_k"] = _rw_post_bwd(y, rw, k2, g, post_pars, do_rw)
    early = _late_grad_parts(grads)
    received = {}
    dr1, dw, dk2a, dv1, dkk, db, received["up2"], received["down2"] = _rw_core_bwd(
        rw, dec, k2, kk, b, rw_states, rw_sa, dy, exchange=early)
    drw, grads["mu"], grads["w0"], grads["w_w2"], grads["a0"], grads["w_a2"], grads["w_g2"], grads["k_k"], grads["k_a"] = (
        _rw_prep_bwd(pr, w["mu"], gate_pars, (dr1, dr2), (dv1, dv2), dw, (dk2a, dk2b), dkk, db, dg))
    dpr = _shift_bwd(drw, w["mu"])
    do_raw, dr_gla, grads["gn"] = _gla_post_bwd(o_raw, pa, w["gn"], ind4, ind4_t, do_gla)
    dq, dk, dv, dla = _gla_core_bwd(pa, la, ltri, gla_states, do_raw)
    da_down, grads["gla_w_a2"], grads["gla_b_a"] = _gla_prep_bwd(pa, w["gla_w_a2"], w["gla_b_a"], dla)
    dpa = jnp.concatenate([dq.astype(BF16), dk.astype(BF16), dv.astype(BF16), dr_gla, da_down,
                           jnp.zeros((T, PA_W - PA_USED), BF16)], axis=1)
    dp = jnp.concatenate([dpg, dpr, dpa], axis=1)
    d_win = _matmul(h2, dp, "tn", D, DIN_P, T, 1024, 1024, 4096, "proj_grad", out_dtype=BF16)
    d_win = _shard_cols(_unalign_proj(d_win)).reshape(NDEV // 2, 2, D, DIN_SH).swapaxes(0, 1)
    dh2, received["win"] = _matmul(dp, win, "nt", T, D, DIN_P, 1024, 1024, DIN_P // 4, "proj_bwd",
                                   comm=("quad", [_chip_sum(d_win, "win")]))
    dx1, grads["g2"] = _rms_bwd(x1, w["g2"], dh2, dx2, "rms_bwd_mix")
    dx, grads["g1"], received["up1"], received["wd1"] = _ffn_bwd(
        dx1, x, w["g1"], w["wg1"], w["wu1"], wd1, ffn1, "1", exchange=True)
    return loss, dx, grads, received


BIG = ("ffn1_wg", "ffn1_wu", "ffn1_wd", "w_in", "w_branch", "w_out", "ffn2_wg", "ffn2_wu", "ffn2_wd")
SMALL_SHARDED = ("gla_w_a2", "rwkv_w_w2", "rwkv_w_a2", "rwkv_w_g2")
REPLICATED = ("ffn1_norm", "mix_norm", "gla_b_a", "gla_gn_w", "rwkv_mu", "rwkv_w0", "rwkv_a0", "rwkv_k_k", "rwkv_k_a",
              "rwkv_r_k", "rwkv_lnx_w", "rwkv_lnx_b", "gate_b", "ffn2_norm", "final_norm")
WEIGHTS = ("ffn1_norm", "ffn1_wg", "ffn1_wu", "ffn1_wd", "mix_norm", "w_in", "gla_w_a2", "gla_b_a", "gla_gn_w",
           "rwkv_mu", "rwkv_w0", "rwkv_w_w2", "rwkv_a0", "rwkv_w_a2", "rwkv_w_g2", "rwkv_k_k", "rwkv_k_a", "rwkv_r_k",
           "rwkv_lnx_w", "rwkv_lnx_b", "gate_b", "w_branch", "w_out", "ffn2_norm", "ffn2_wg", "ffn2_wu", "ffn2_wd",
           "final_norm")


def _unshard_cols(g):
    return jnp.transpose(g, (1, 0, 2)).reshape(g.shape[1], NDEV * g.shape[2])


def _shard_cols(a):
    return jnp.transpose(a.reshape(a.shape[0], NDEV, a.shape[1] // NDEV), (1, 0, 2))


def _pad_rows(a, rows):
    return jnp.pad(a, ((0, rows - a.shape[0]), (0, 0)))


def _align_rw(a):
    c = 3 * RW_W
    z = jnp.zeros((a.shape[0], LORA_P - DECAY_LORA), a.dtype)
    return jnp.concatenate([a[:, :c], a[:, c:c + DECAY_LORA], z, a[:, c + DECAY_LORA:c + 2 * DECAY_LORA], z,
                            a[:, c + 2 * DECAY_LORA:]], axis=1)


def _unalign_rw(a):
    c = 3 * RW_W
    return jnp.concatenate([a[:, :c + DECAY_LORA], a[:, c + LORA_P:c + LORA_P + AAA_LORA], a[:, c + 2 * LORA_P:]], axis=1)


def _align_proj(a):
    gla = jnp.pad(a[:, :GLA_IN], ((0, 0), (0, PA_W - GLA_IN)))
    return jnp.concatenate([a[:, GLA_IN + RW_IN:], _align_rw(a[:, GLA_IN:GLA_IN + RW_IN]), gla], axis=1)


def _unalign_proj(a):
    return jnp.concatenate([a[:, PG_W + PR_W:PG_W + PR_W + GLA_IN], _unalign_rw(a[:, PG_W:PG_W + PR_W]), a[:, :PG_W]], axis=1)


def _layout_weights(gb, gs, rep):
    row = lambda n: rep[n].reshape(1, -1)
    return {
        "wg1": gb["ffn1_wg"], "wu1": gb["ffn1_wu"],
        "g1": row("ffn1_norm"), "g2": row("mix_norm"), "g3": row("ffn2_norm"), "gf": row("final_norm"),
        "gla_w_a2": _pad_rows(_unshard_cols(gs["gla_w_a2"]), LORA_P), "gla_b_a": row("gla_b_a"),
        "gn": jnp.tile(row("gla_gn_w"), (1, GLA_HEADS)),
        "mu": _align_rw(row("rwkv_mu")), "w0": row("rwkv_w0"), "a0": row("rwkv_a0"),
        "w_w2": _pad_rows(_unshard_cols(gs["rwkv_w_w2"]), LORA_P),
        "w_a2": _pad_rows(_unshard_cols(gs["rwkv_w_a2"]), LORA_P),
        "w_g2": _unshard_cols(gs["rwkv_w_g2"]),
        "k_k": row("rwkv_k_k"), "k_a": row("rwkv_k_a"), "r_k": row("rwkv_r_k"),
        "lnx_w": row("rwkv_lnx_w"), "lnx_b": row("rwkv_lnx_b"), "gate_b": row("gate_b"),
    }


LATE_ROWS = (("ffn2_wd", FSH), ("w_branch", (GLA_V + RW_W) // NDEV), ("w_out", D // NDEV))


def _late_weights(g_up, g_down):
    r1, r2 = LATE_ROWS[0][1], LATE_ROWS[0][1] + LATE_ROWS[1][1]
    return {"wg2": g_up[:, 0], "wu2": g_up[:, 1], "wd2": g_down[:, :r1],
            "wb": g_down[:, r1:r2].reshape(GLA_V + RW_W, D), "wo": g_down[:, r2:].reshape(D, D)}


def _late_grad_parts(g):
    return [g["up2"], jnp.concatenate([g["wd2"], g["wb"].reshape(NDEV, -1, D), g["wo"].reshape(NDEV, -1, D)], axis=1)]


def _layout_grads(g):
    return {
        "ffn1_norm": g["g1"], "mix_norm": g["g2"], "ffn2_norm": g["g3"], "final_norm": g["gf"],
        "gla_w_a2": g["gla_w_a2"][:GLA_LORA], "gla_b_a": g["gla_b_a"],
        "gla_gn_w": jnp.sum(g["gn"].reshape(GLA_HEADS, GLA_DV), axis=0, keepdims=True),
        "rwkv_mu": _unalign_rw(g["mu"]), "rwkv_w0": g["w0"], "rwkv_a0": g["a0"],
        "rwkv_w_w2": g["w_w2"][:DECAY_LORA], "rwkv_w_a2": g["w_a2"][:AAA_LORA], "rwkv_w_g2": g["w_g2"],
        "rwkv_k_k": g["k_k"], "rwkv_k_a": g["k_a"], "rwkv_r_k": g["r_k"],
        "rwkv_lnx_w": g["lnx_w"], "rwkv_lnx_b": g["lnx_b"], "gate_b": g["gate_b"],
    }


_MESH = pl.DeviceIdType.MESH
_ANY = pl.BlockSpec(memory_space=pl.ANY)


def _position():
    return lax.axis_index("x"), lax.axis_index("y"), lax.axis_index("c")


def _slot(p):
    return 4 * p[0] + 2 * p[1] + p[2]


def _comm_sems(n):
    if not n:
        return []
    return [pltpu.SemaphoreType.DMA((7 * n,)), pltpu.SemaphoreType.DMA((7 * n,)), pltpu.SemaphoreType.DMA((n,))]


def _gather_plan(ins, outs, send_sems, recv_sems, local_sems):
    n = len(ins)
    x, y, c = _position()
    me, sibling = (x, y, c), (x, y, 1 - c)
    chips = [(1 - x, y), (x, 1 - y), (1 - x, 1 - y)]

    def copy(a, k, block, to, src=None):
        dst = outs[a].at[_slot(block)]
        return pltpu.make_async_remote_copy(
            src_ref=dst if src is None else src, dst_ref=dst, send_sem=send_sems.at[7 * a + k],
            recv_sem=recv_sems.at[7 * a + k], device_id=to, device_id_type=_MESH)

    def local(a):
        return pltpu.make_async_copy(ins[a], outs[a].at[_slot(me)], local_sems.at[a])

    def own(a):
        return [copy(a, 0, me, sibling, src=ins[a])] + [copy(a, 1 + j, me, (*chip, c), src=ins[a]) for j, chip in enumerate(chips)]

    def start():
        for a in range(n):
            local(a).start()
            for cp in own(a):
                cp.start()

    def forward():
        for a in range(n):
            for j, chip in enumerate(chips):
                copy(a, 1 + j, (*chip, c), me).wait_recv()
                copy(a, 4 + j, (*chip, c), sibling).start()

    def finish():
        for a in range(n):
            copy(a, 0, sibling, me).wait_recv()
            for j, chip in enumerate(chips):
                copy(a, 4 + j, (*chip, 1 - c), me).wait_recv()
        for a in range(n):
            for cp in own(a) + [copy(a, 4 + j, (*chip, c), sibling) for j, chip in enumerate(chips)]:
                cp.wait_send()
            local(a).wait()

    return start, forward, finish


def _exchange_plan(ins, outs, send_sems, recv_sems, local_sems):
    n = len(ins)
    x, y, c = _position()
    me = (x, y, c)
    flip = lambda v, f: 1 - v if f else v
    peers = [(flip(x, fx), flip(y, fy), flip(c, fc))
             for fx, fy, fc in ((0, 0, 1), (1, 0, 0), (0, 1, 0), (1, 1, 0), (1, 0, 1), (0, 1, 1), (1, 1, 1))]

    def copy(a, k, src_slot, dst_slot):
        return pltpu.make_async_remote_copy(
            src_ref=ins[a].at[src_slot], dst_ref=outs[a].at[dst_slot], send_sem=send_sems.at[7 * a + k],
            recv_sem=recv_sems.at[7 * a + k], device_id=peers[k], device_id_type=_MESH)

    def local(a):
        return pltpu.make_async_copy(ins[a].at[_slot(me)], outs[a].at[_slot(me)], local_sems.at[a])

    def start():
        for a in range(n):
            local(a).start()
            for k, peer in enumerate(peers):
                copy(a, k, _slot(peer), _slot(me)).start()

    def finish():
        for a in range(n):
            for k, peer in enumerate(peers):
                copy(a, k, _slot(peer), _slot(peer)).wait_recv()
        for a in range(n):
            for k, peer in enumerate(peers):
                copy(a, k, _slot(peer), _slot(me)).wait_send()
            local(a).wait()

    return start, finish


def _sibling_plan(ins, outs, send_sems, recv_sems, local_sems):
    x, y, c = _position()

    def copy(a):
        return pltpu.make_async_remote_copy(
            src_ref=ins[a].at[1 - c], dst_ref=outs[a], send_sem=send_sems.at[7 * a], recv_sem=recv_sems.at[7 * a],
            device_id=(x, y, 1 - c), device_id_type=_MESH)

    def start():
        for a in range(len(ins)):
            copy(a).start()

    def finish():
        for a in range(len(ins)):
            copy(a).wait()

    return start, finish


def _quad_plan(ins, outs, send_sems, recv_sems, local_sems):
    n = len(ins)
    x, y, c = _position()
    mine = 2 * x + y
    peers = [(1 - x, y), (x, 1 - y), (1 - x, 1 - y)]

    def copy(a, k, src_slot, dst_slot):
        return pltpu.make_async_remote_copy(
            src_ref=ins[a].at[src_slot], dst_ref=outs[a].at[dst_slot], send_sem=send_sems.at[7 * a + k],
            recv_sem=recv_sems.at[7 * a + k], device_id=(*peers[k], c), device_id_type=_MESH)

    def local(a):
        return pltpu.make_async_copy(ins[a].at[mine], outs[a].at[mine], local_sems.at[a])

    def start():
        for a in range(n):
            local(a).start()
            for k, (px, py) in enumerate(peers):
                copy(a, k, 2 * px + py, mine).start()

    def finish():
        for a in range(n):
            for k, (px, py) in enumerate(peers):
                copy(a, k, 2 * px + py, 2 * px + py).wait_recv()
        for a in range(n):
            for k, (px, py) in enumerate(peers):
                copy(a, k, 2 * px + py, mine).wait_send()
            local(a).wait()

    return start, finish


_PLANS = {"gather": (_gather_plan, lambda s: (NDEV,) + s), "exchange": (_exchange_plan, lambda s: s),
          "sibling": (_sibling_plan, lambda s: s[1:]), "quad": (_quad_plan, lambda s: s)}


def _exchange_now(kind, arrays, name):
    n = len(arrays)

    def body(*refs):
        for stage in _PLANS[kind][0](refs[:n], refs[n:2 * n], *refs[2 * n:]):
            stage()

    return pl.pallas_call(
        body, name=name, in_specs=[_ANY] * n, out_specs=[_ANY] * n,
        out_shape=[jax.ShapeDtypeStruct(_PLANS[kind][1](a.shape), a.dtype) for a in arrays], scratch_shapes=_comm_sems(n),
    )(*arrays)


def _chip_sum(parts, name):
    (theirs,) = _exchange_now("sibling", [parts], "chip_send_" + name)
    shape = theirs.shape
    rows = shape[-2]
    for d in shape[1:-2]:
        rows *= d
    flat = (4, rows, shape[-1])
    tr = 512

    def body(both_ref, b_ref, o_ref):
        mine = jnp.where(lax.axis_index("c") == 0, both_ref[0], both_ref[1])
        o_ref[...] = (mine.astype(F32) + b_ref[...].astype(F32)).astype(BF16)

    blk = pl.BlockSpec((None, tr, shape[-1]), lambda s, r: (s, r, 0))
    out = pl.pallas_call(
        body, name="chip_sum_" + name, grid=(4, rows // tr),
        in_specs=[pl.BlockSpec((2, None, tr, shape[-1]), lambda s, r: (0, s, r, 0)), blk], out_specs=blk,
        out_shape=jax.ShapeDtypeStruct(flat, BF16), compiler_params=_params(("parallel", "parallel")),
    )(parts.reshape((2,) + flat), theirs.reshape(flat))
    return out.reshape(shape)


def _adamw_math(w, g, m, v):
    m = ADAM_B1 * m + (1.0 - ADAM_B1) * g
    v = ADAM_B2 * v + (1.0 - ADAM_B2) * (g * g)
    m_hat = m / (1.0 - ADAM_B1 ** ADAM_STEP)
    v_hat = v / (1.0 - ADAM_B2 ** ADAM_STEP)
    delta = -ADAM_LR * (m_hat / (jnp.sqrt(v_hat) + ADAM_EPS) + ADAM_WD * w)
    return delta, m, v


def _sum_slots(ref):
    total = ref[0].astype(F32)
    for s in range(1, ref.shape[0]):
        total = total + ref[s].astype(F32)
    return total


def _adamw(parts, w, m, v, tr, name, stack_index=None, row_block_offset=0, transposed=False):
    _, R, C = w.shape

    def body(p_ref, w_ref, m_ref, v_ref, g_ref, d_ref, nm_ref, nv_ref):
        g = _sum_slots(p_ref)
        g = g.T if transposed else g
        g_ref[...] = g
        d_ref[...], nm_ref[...], nv_ref[...] = _adamw_math(w_ref[...], g, m_ref[...], v_ref[...])

    if stack_index is None:
        p_spec = pl.BlockSpec((parts.shape[0], tr, C), lambda r: (0, row_block_offset + r, 0))
    else:
        p_spec = pl.BlockSpec((parts.shape[0], None, tr, C), lambda r: (0, stack_index, r, 0))
    if transposed:
        blk = pl.BlockSpec((None, C, tr), lambda r: (0, 0, r))
        out = jax.ShapeDtypeStruct((1, C, R), F32)
        w, m, v = (jnp.swapaxes(a, 1, 2) for a in (w, m, v))
    else:
        blk = pl.BlockSpec((None, tr, C), lambda r: (0, r, 0))
        out = jax.ShapeDtypeStruct((1, R, C), F32)
    res = pl.pallas_call(
        body, name=name, grid=(R // tr,), in_specs=[p_spec, blk, blk, blk], out_specs=[blk] * 4, out_shape=(out,) * 4,
        compiler_params=_params(("parallel",)),
    )(parts, w, m, v)
    return [jnp.swapaxes(a, 1, 2) for a in res] if transposed else res


def _sum_gathered(parts):
    _, R, C = parts.shape

    def body(p_ref, o_ref):
        o_ref[...] = _sum_slots(p_ref)

    return pl.pallas_call(body, name="small_grad_sum", out_shape=jax.ShapeDtypeStruct((R, C), F32),
                          compiler_params=_params())(parts)


def _adamw_small(w, g, m, v):
    def body(w_ref, g_ref, m_ref, v_ref, d_ref, nm_ref, nv_ref):
        d_ref[...], nm_ref[...], nv_ref[...] = _adamw_math(w_ref[...], g_ref[...], m_ref[...], v_ref[...])

    out = jax.ShapeDtypeStruct(w.shape, F32)
    return pl.pallas_call(body, name="adamw_small", out_shape=(out,) * 3, compiler_params=_params())(w, g, m, v)


def _pack(pieces, rows):
    flat = jnp.concatenate([p.reshape(-1) for p in pieces])
    return jnp.pad(flat, (0, rows * 128 - flat.shape[0])).reshape(rows, 128)


def _unpack(packed, shapes):
    flat = packed.reshape(-1)
    out, off = [], 0
    for s in shapes:
        size = 1
        for d in s:
            size *= d
        out.append(flat[off:off + size].reshape(s))
        off += size
    return out


def _rows_for(shapes, extra=0):
    total = extra
    for s in shapes:
        size = 1
        for d in s:
            size *= d
        total += size
    return -(-total // 1024) * 8


def kernel(x, ffn1_norm, ffn1_wg, ffn1_wu, ffn1_wd, mix_norm, w_in, gla_w_a2, gla_b_a, gla_gn_w, rwkv_mu, rwkv_w0, rwkv_w_w2, rwkv_a0, rwkv_w_a2, rwkv_w_g2, rwkv_k_k, rwkv_k_a, rwkv_r_k, rwkv_lnx_w, rwkv_lnx_b, gate_b, w_branch, w_out, ffn2_norm, ffn2_wg, ffn2_wu, ffn2_wd, final_norm, loss_target, m_ffn1_norm, m_ffn1_wg, m_ffn1_wu, m_ffn1_wd, m_mix_norm, m_w_in, m_gla_w_a2, m_gla_b_a, m_gla_gn_w, m_rwkv_mu, m_rwkv_w0, m_rwkv_w_w2, m_rwkv_a0, m_rwkv_w_a2, m_rwkv_w_g2, m_rwkv_k_k, m_rwkv_k_a, m_rwkv_r_k, m_rwkv_lnx_w, m_rwkv_lnx_b, m_gate_b, m_w_branch, m_w_out, m_ffn2_norm, m_ffn2_wg, m_ffn2_wu, m_ffn2_wd, m_final_norm, v_ffn1_norm, v_ffn1_wg, v_ffn1_wu, v_ffn1_wd, v_mix_norm, v_w_in, v_gla_w_a2, v_gla_b_a, v_gla_gn_w, v_rwkv_mu, v_rwkv_w0, v_rwkv_w_w2, v_rwkv_a0, v_rwkv_w_a2, v_rwkv_w_g2, v_rwkv_k_k, v_rwkv_k_a, v_rwkv_r_k, v_rwkv_lnx_w, v_rwkv_lnx_b, v_gate_b, v_w_branch, v_w_out, v_ffn2_norm, v_ffn2_wg, v_ffn2_wu, v_ffn2_wd, v_final_norm):
    wts = dict(zip(WEIGHTS, (ffn1_norm, ffn1_wg, ffn1_wu, ffn1_wd, mix_norm, w_in, gla_w_a2, gla_b_a, gla_gn_w, rwkv_mu, rwkv_w0, rwkv_w_w2, rwkv_a0, rwkv_w_a2, rwkv_w_g2, rwkv_k_k, rwkv_k_a, rwkv_r_k, rwkv_lnx_w, rwkv_lnx_b, gate_b, w_branch, w_out, ffn2_norm, ffn2_wg, ffn2_wu, ffn2_wd, final_norm)))
    mom = dict(zip(WEIGHTS, (m_ffn1_norm, m_ffn1_wg, m_ffn1_wu, m_ffn1_wd, m_mix_norm, m_w_in, m_gla_w_a2, m_gla_b_a, m_gla_gn_w, m_rwkv_mu, m_rwkv_w0, m_rwkv_w_w2, m_rwkv_a0, m_rwkv_w_a2, m_rwkv_w_g2, m_rwkv_k_k, m_rwkv_k_a, m_rwkv_r_k, m_rwkv_lnx_w, m_rwkv_lnx_b, m_gate_b, m_w_branch, m_w_out, m_ffn2_norm, m_ffn2_wg, m_ffn2_wu, m_ffn2_wd, m_final_norm)))
    var = dict(zip(WEIGHTS, (v_ffn1_norm, v_ffn1_wg, v_ffn1_wu, v_ffn1_wd, v_mix_norm, v_w_in, v_gla_w_a2, v_gla_b_a, v_gla_gn_w, v_rwkv_mu, v_rwkv_w0, v_rwkv_w_w2, v_rwkv_a0, v_rwkv_w_a2, v_rwkv_w_g2, v_rwkv_k_k, v_rwkv_k_a, v_rwkv_r_k, v_rwkv_lnx_w, v_rwkv_lnx_b, v_gate_b, v_w_branch, v_w_out, v_ffn2_norm, v_ffn2_wg, v_ffn2_wu, v_ffn2_wd, v_final_norm)))
    two = lambda a: a.reshape(a.shape[-2:])

    bf = lambda n: two(wts[n]).astype(BF16)
    up1 = jnp.stack([bf("ffn1_wg"), bf("ffn1_wu")])
    lora = jnp.concatenate([jnp.pad(two(gla_w_a2), ((0, 0), (0, 128 - GLA_QK // NDEV)))] +
                           [two(wts[n]) for n in SMALL_SHARDED[1:]], axis=0)
    g_up1, g_lora = _exchange_now("gather", [up1, lora], "gather_weights")
    gb = {"ffn1_wg": g_up1[:, 0], "ffn1_wu": g_up1[:, 1]}
    gs = {"gla_w_a2": g_lora[:, :GLA_LORA, :GLA_QK // NDEV]}
    row = GLA_LORA
    for n in SMALL_SHARDED[1:]:
        gs[n] = g_lora[:, row:row + wts[n].shape[1]]
        row += wts[n].shape[1]
    w = _layout_weights(gb, gs, {n: wts[n] for n in REPLICATED})
    blocks = {"wd1": bf("ffn1_wd"), "win": bf("w_in"),
              "late": [jnp.stack([bf("ffn2_wg"), bf("ffn2_wu")]), jnp.concatenate([bf(n) for n, _ in LATE_ROWS], axis=0)]}

    loss_part, grad_x, grads, parts = _local_step(x[0], loss_target[0], w, blocks)
    small = _layout_grads(grads)
    result = {}
    state = lambda n: (wts[n], mom[n], var[n])
    for group, names in (("up1", ("ffn1_wg", "ffn1_wu")), ("up2", ("ffn2_wg", "ffn2_wu"))):
        for i, n in enumerate(names):
            result[n] = _adamw(parts[group], *state(n), 256, "adamw_" + n, stack_index=i, transposed=True)
    result["ffn1_wd"] = _adamw(parts["wd1"], *state("ffn1_wd"), 64, "adamw_ffn1_wd")
    row = 0
    for n, rows in LATE_ROWS:
        result[n] = _adamw(parts["down2"], *state(n), 64, "adamw_" + n, row_block_offset=row // 64)
        row += rows
    result["w_in"] = _adamw(parts["win"], *state("w_in"), 256, "adamw_w_in")

    small_names = [n for n in WEIGHTS if n not in BIG]
    full_shapes = [small[n].shape for n in small_names]
    rows_full = _rows_for(full_shapes, extra=128)
    packed = _pack([small[n] for n in small_names] + [loss_part], rows_full)
    (gathered,) = _exchange_now("gather", [packed], "gather_small_grads")
    total = _sum_gathered(gathered)
    *full_grads, loss_row = _unpack(total, full_shapes + [(1, 128)])
    me = _slot(_position())
    own = {}
    for n, g in zip(small_names, full_grads):
        if n in SMALL_SHARDED:
            cols = wts[n].shape[-1]
            g = lax.dynamic_slice_in_dim(g, me * cols, cols, axis=1)
        own[n] = g.reshape(wts[n].shape)
    own_shapes = [wts[n].shape for n in small_names]
    rows_own = _rows_for(own_shapes)
    pk = lambda d: _pack([d[n] for n in small_names], rows_own)
    d_s, m_s, v_s = _adamw_small(pk(wts), pk(own), pk(mom), pk(var))
    for n, d, m, v in zip(small_names, _unpack(d_s, own_shapes), _unpack(m_s, own_shapes), _unpack(v_s, own_shapes)):
        result[n] = (own[n], d, m, v)

    shaped = lambda n, k: result[n][k].reshape(wts[n].shape)
    return (loss_row[0, 0], grad_x[None],
            *[shaped(n, 0) for n in WEIGHTS], *[shaped(n, 1) for n in WEIGHTS],
            *[shaped(n, 2) for n in WEIGHTS], *[shaped(n, 3) for n in WEIGHTS])
```

```python
import functools

import jax
import jax.numpy as jnp
from jax import lax
from jax.experimental import pallas as pl
from jax.experimental.pallas import tpu as pltpu

F32 = jnp.float32
BF16 = jnp.bfloat16
HI = lax.Precision.HIGHEST

NDEV = 8
D = 2048
DFF = 5632
FSH = DFF // NDEV
CHUNK = 64
GLA_HEADS, GLA_DK, GLA_DV = 4, 128, 256
GLA_QK, GLA_V, GLA_LORA, GLA_TAU = 512, 1024, 16, 16.0
RW_HEADS, RW_HD, RW_W = 16, 64, 1024
DECAY_LORA, AAA_LORA, GATE_LORA = 96, 96, 256
GN_EPS = 64e-5
NORM_EPS = 1e-6
GLA_IN = 2 * GLA_QK + 2 * GLA_V + GLA_LORA
RW_IN = 3 * RW_W + DECAY_LORA + AAA_LORA + GATE_LORA
D_IN = GLA_IN + RW_IN + 2 * D
DIN_SH = D_IN // NDEV
PG_W = 2 * D
PR_W = 3584
PA_W = 3584
PA_USED = 2 * GLA_QK + 2 * GLA_V + 128
DIN_P = PG_W + PR_W + PA_W
LORA_P = 128

ADAM_LR, ADAM_B1, ADAM_B2, ADAM_EPS, ADAM_WD, ADAM_STEP = 0.001, 0.9, 0.999, 1e-08, 0.01, 10

VMEM_LIMIT = 56 * 1024 * 1024
RW_TB = 32
RW_G = 16
RW_NP = 8


def _params(sem=None, vmem=VMEM_LIMIT):
    return pltpu.CompilerParams(dimension_semantics=sem, vmem_limit_bytes=vmem)


def _pair_mask():
    return lax.broadcasted_iota(jnp.int32, (RW_HD, 2 * RW_HD), 1) < RW_HD


def _pair_rowsum(p, mask):
    tot = jnp.sum(p, axis=1, keepdims=True)
    first = jnp.sum(jnp.where(mask, p, 0.0), axis=1, keepdims=True)
    return first, tot - first


def _split_transposed(x_ref, q, dst_ref, base):
    xt = x_ref[:, 128 * q:128 * (q + 1)].T
    for g in range(RW_TB // RW_G):
        dst_ref[base + g, :, 0:RW_G] = xt[:, g * RW_G:(g + 1) * RW_G]


def _pair_column(tile_ref, idx, i, mask):
    return jnp.where(mask, tile_ref[idx, 0:RW_HD, i:i + 1], tile_ref[idx, RW_HD:, i:i + 1])


def _rw_core_fwd(rw, w, k2, kk, b, gather=()):
    T = rw.shape[0]
    nb = T // RW_TB
    ng = RW_TB // RW_G
    NP = RW_NP
    nc = len(gather)
    npair = RW_HEADS // 2 // NP

    def body(r_ref, v_ref, w_ref, k_ref, kk_ref, b_ref, *rest):
        g_in, (y_ref, st_ref, sa_ref), g_out = rest[:nc], rest[nc:nc + 3], rest[nc + 3:2 * nc + 3]
        s_scr, vt_scr, yt_scr, rows_scr = rest[2 * nc + 3:2 * nc + 7]
        pair, blk_i = pl.program_id(0), pl.program_id(1)
        if nc:
            start, forward, finish = _gather_plan(g_in, g_out, *rest[2 * nc + 7:])
            pl.when((pair == 0) & (blk_i == 0))(start)
            pl.when((pair == 0) & (blk_i == nb // 2))(forward)

        @pl.when(pl.program_id(1) == 0)
        def _():
            s_scr[...] = jnp.zeros_like(s_scr)
            yt_scr[...] = jnp.zeros_like(yt_scr)

        mask = _pair_mask()
        for q in range(NP):
            _split_transposed(v_ref, q, vt_scr, q * ng)
        R_, W_, K_, KK_, B_ = range(5)
        for a, ref in enumerate((r_ref, w_ref, k_ref, kk_ref, b_ref)):
            for q in range(NP):
                rows_scr[a * NP + q] = ref[:, 128 * q:128 * (q + 1)]

        def group(g, states):
            states = list(states)
            for i in range(RW_G):
                t = g * RW_G + i
                row = lambda a, q: rows_scr[a * NP + q, pl.ds(t, 1), :]
                sums = [_pair_rowsum(states[q] * row(KK_, q), mask) for q in range(NP)]
                for q in range(NP):
                    sa = jnp.where(mask, *sums[q])
                    sa_ref[q, t] = sa
                    states[q] = (states[q] * row(W_, q) - sa * row(B_, q)
                                 + _pair_column(vt_scr, q * ng + g, i, mask) * row(K_, q))
                    st_ref[q, t] = states[q]
                outs = [_pair_rowsum(states[q] * row(R_, q), mask) for q in range(NP)]
                for q in range(NP):
                    yt_scr[q * ng + g, 0:RW_HD, i:i + 1] = outs[q][0]
                    yt_scr[q * ng + g, RW_HD:, i:i + 1] = outs[q][1]
            return tuple(states)

        states = lax.fori_loop(0, ng, group, tuple(s_scr[q] for q in range(NP)))
        for q in range(NP):
            s_scr[q] = states[q]
            for g in range(ng):
                y_ref[g * RW_G:(g + 1) * RW_G, 128 * q:128 * (q + 1)] = yt_scr[q * ng + g].T[0:RW_G, :]
        if nc:
            pl.when((pair == npair - 1) & (blk_i == nb - 1))(finish)

    blk = lambda cb: pl.BlockSpec((RW_TB, 128 * NP), lambda p, i, cb=cb: (i, cb + p))
    tiles = pltpu.VMEM((NP * ng, 128, 128), F32)
    return pl.pallas_call(
        body, name="rw_core_fwd", grid=(npair, nb),
        in_specs=[blk(0), blk(2 * RW_W // (128 * NP)), blk(0), blk(0), blk(0), blk(0)] + [_ANY] * nc,
        out_specs=[blk(0)] + [pl.BlockSpec((NP, RW_TB, RW_HD, 128), lambda p, i: (p, i, 0, 0))] * 2 + [_ANY] * nc,
        out_shape=[jax.ShapeDtypeStruct((T, RW_W), F32)] + [jax.ShapeDtypeStruct((RW_HEADS // 2, T, RW_HD, 128), F32)] * 2
        + [jax.ShapeDtypeStruct((NDEV,) + a.shape, a.dtype) for a in gather],
        scratch_shapes=[pltpu.VMEM((NP, RW_HD, 128), F32), tiles, tiles, pltpu.VMEM((5 * NP, RW_TB, 128), F32)]
        + _comm_sems(nc),
        compiler_params=_params(("arbitrary", "arbitrary")),
    )(rw, rw, w, k2, kk, b, *gather)


def _rw_core_bwd(rw, w, k2, kk, b, states, sa_tiles, dy, exchange=()):
    T = rw.shape[0]
    nb = T // RW_TB
    ng = RW_TB // RW_G
    NP = RW_NP
    nc = len(exchange)
    npair = RW_HEADS // 2 // NP

    def body(r_ref, v_ref, w_ref, k_ref, kk_ref, b_ref, dy_ref, st_ref, sp_ref, sa_ref, *rest):
        e_in, e_out = rest[:nc], rest[nc + 6:2 * nc + 6]
        dr_ref, dw_ref, dk_ref, dv_ref, dkk_ref, db_ref = rest[nc:nc + 6]
        ds_scr, vt_scr, dyt_scr, dvt_scr, rows_scr, out_scr = rest[2 * nc + 6:2 * nc + 12]
        step = pl.program_id(1)
        if nc:
            start, finish = _exchange_plan(e_in, e_out, *rest[2 * nc + 12:])
            pl.when((pl.program_id(0) == 0) & (step == 0))(start)

        @pl.when(step == 0)
        def _():
            ds_scr[...] = jnp.zeros_like(ds_scr)
            dvt_scr[...] = jnp.zeros_like(dvt_scr)

        mask = _pair_mask()
        for q in range(NP):
            _split_transposed(v_ref, q, vt_scr, q * ng)
            _split_transposed(dy_ref, q, dyt_scr, q * ng)
        R_, W_, K_, KK_, B_ = range(5)
        for a, ref in enumerate((r_ref, w_ref, k_ref, kk_ref, b_ref)):
            for q in range(NP):
                rows_scr[a * NP + q] = ref[:, 128 * q:128 * (q + 1)]

        def group(gg, grads):
            g = ng - 1 - gg
            grads = list(grads)
            pairs = range(NP)
            for i in reversed(range(RW_G)):
                t = g * RW_G + i
                row = lambda a, q: rows_scr[a * NP + q, pl.ds(t, 1), :]

                def put(a, q, value):
                    out_scr[a * NP + q, pl.ds(t, 1), :] = value

                s_old = [st_ref[q, jnp.maximum(t - 1, 0)] for q in pairs]
                if i == 0:
                    s_old = [jnp.where(g == 0, jnp.where(step == nb - 1, 0.0, sp_ref[q, 0]), s_old[q]) for q in pairs]
                dycol = [_pair_column(dyt_scr, q * ng + g, i, mask) for q in pairs]
                dS = [grads[q] + dycol[q] * row(R_, q) for q in pairs]
                m = [_pair_rowsum(dS[q] * row(B_, q), mask) for q in pairs]
                dv = [_pair_rowsum(dS[q] * row(K_, q), mask) for q in pairs]
                for q in pairs:
                    put(R_, q, jnp.sum(st_ref[q, t] * dycol[q], axis=0, keepdims=True))
                    put(W_, q, jnp.sum(dS[q] * s_old[q], axis=0, keepdims=True))
                    put(K_, q, jnp.sum(dS[q] * _pair_column(vt_scr, q * ng + g, i, mask), axis=0, keepdims=True))
                for q in pairs:
                    dsa = -jnp.where(mask, *m[q])
                    grads[q] = dS[q] * row(W_, q) + dsa * row(KK_, q)
                    put(KK_, q, jnp.sum(s_old[q] * dsa, axis=0, keepdims=True))
                    put(B_, q, -jnp.sum(dS[q] * sa_ref[q, t], axis=0, keepdims=True))
                    dvt_scr[q * ng + g, 0:RW_HD, i:i + 1] = dv[q][0]
                    dvt_scr[q * ng + g, RW_HD:, i:i + 1] = dv[q][1]
            return tuple(grads)

        grads = lax.fori_loop(0, ng, group, tuple(ds_scr[q] for q in range(NP)))
        for q in range(NP):
            ds_scr[q] = grads[q]
            for a, ref in enumerate((dr_ref, dw_ref, dk_ref, dkk_ref, db_ref)):
                ref[:, 128 * q:128 * (q + 1)] = out_scr[a * NP + q]
            for g in range(ng):
                dv_ref[g * RW_G:(g + 1) * RW_G, 128 * q:128 * (q + 1)] = dvt_scr[q * ng + g].T[0:RW_G, :]
        if nc:
            pl.when((pl.program_id(0) == npair - 1) & (step == nb - 1))(finish)

    blk = lambda cb: pl.BlockSpec((RW_TB, 128 * NP), lambda p, i, cb=cb: (nb - 1 - i, cb + p))
    st_spec = pl.BlockSpec((NP, RW_TB, RW_HD, 128), lambda p, i: (p, nb - 1 - i, 0, 0))
    sp_spec = pl.BlockSpec((NP, 1, RW_HD, 128), lambda p, i: (p, jnp.maximum((nb - 1 - i) * RW_TB - 1, 0), 0, 0))
    out = jax.ShapeDtypeStruct((T, RW_W), F32)
    tiles = pltpu.VMEM((NP * ng, 128, 128), F32)
    return pl.pallas_call(
        body, name="rw_core_bwd", grid=(npair, nb),
        in_specs=[blk(0), blk(2 * RW_W // (128 * NP)), blk(0), blk(0), blk(0), blk(0), blk(0), st_spec, sp_spec, st_spec]
        + [_ANY] * nc,
        out_specs=[blk(0)] * 6 + [_ANY] * nc,
        out_shape=[out] * 6 + [jax.ShapeDtypeStruct(a.shape, a.dtype) for a in exchange],
        scratch_shapes=[pltpu.VMEM((NP, RW_HD, 128), F32), tiles, tiles, tiles,
                        pltpu.VMEM((5 * NP, RW_TB, 128), F32), pltpu.VMEM((5 * NP, RW_TB, 128), F32)] + _comm_sems(nc),
        compiler_params=_params(("arbitrary", "arbitrary")),
    )(rw, rw, w, k2, kk, b, dy, states, states, sa_tiles, *exchange)


GLA_CB = 8


def _gla_chunk(s_t, q, k, v, la, ltri):
    cum = jnp.dot(ltri, la, precision=HI, preferred_element_type=F32)
    total = jnp.sum(la, axis=0, keepdims=True)
    kdec = k * jnp.exp(total - cum)
    u_t = _bdot(v, kdec, _TN)
    s_t = jnp.exp(total) * s_t + u_t
    o = _bdot(q * (GLA_DK ** -0.5), s_t, _NT)
    return s_t, o


def _gla_core_fwd(pa, la, ltri):
    T = pa.shape[0]
    cb = min(GLA_CB, T // CHUNK)
    rows = cb * CHUNK
    nsteps = T // rows

    def body(q_ref, k_ref, v_ref, la_ref, ltri_ref, o_ref, st_ref, s_scr):
        @pl.when(pl.program_id(0) == 0)
        def _():
            s_scr[...] = jnp.zeros_like(s_scr)

        def chunk(c, states):
            sl = pl.ds(pl.multiple_of(c * CHUNK, CHUNK), CHUNK)
            out = []
            for h in range(GLA_HEADS):
                qk, vv = slice(GLA_DK * h, GLA_DK * (h + 1)), slice(GLA_DV * h, GLA_DV * (h + 1))
                s_t, o = _gla_chunk(states[h], q_ref[sl, qk], k_ref[sl, qk], v_ref[sl, vv], la_ref[sl, qk], ltri_ref[...])
                o_ref[sl, vv] = o
                st_ref[h, c] = s_t
                out.append(s_t)
            return tuple(out)

        states = lax.fori_loop(0, cb, chunk, tuple(s_scr[h] for h in range(GLA_HEADS)))
        for h in range(GLA_HEADS):
            s_scr[h] = states[h]

    qk = lambda cb_: pl.BlockSpec((rows, GLA_QK), lambda i, cb_=cb_: (i, cb_))
    return pl.pallas_call(
        body, name="gla_core_fwd", grid=(nsteps,),
        in_specs=[qk(0), qk(1), pl.BlockSpec((rows, GLA_V), lambda i: (i, 1)), qk(0),
                  pl.BlockSpec((CHUNK, CHUNK), lambda i: (0, 0))],
        out_specs=[pl.BlockSpec((rows, GLA_V), lambda i: (i, 0)),
                   pl.BlockSpec((GLA_HEADS, cb, GLA_DV, GLA_DK), lambda i: (0, i, 0, 0))],
        out_shape=(jax.ShapeDtypeStruct((T, GLA_V), F32),
                   jax.ShapeDtypeStruct((GLA_HEADS, T // CHUNK, GLA_DV, GLA_DK), F32)),
        scratch_shapes=[pltpu.VMEM((GLA_HEADS, GLA_DV, GLA_DK), F32)],
        compiler_params=_params(("arbitrary",)),
    )(pa, pa, pa, la, ltri)


def _gla_core_bwd(pa, la, ltri, states, do):
    T = pa.shape[0]
    cb = min(GLA_CB, T // CHUNK)
    rows = cb * CHUNK
    nsteps = T // rows

    def body(q_ref, k_ref, v_ref, la_ref, ltri_ref, st_ref, sp_ref, do_ref,
             dq_ref, dk_ref, dv_ref, dla_ref, ds_scr):
        step = pl.program_id(0)

        @pl.when(step == 0)
        def _():
            ds_scr[...] = jnp.zeros_like(ds_scr)

        def chunk(cc, grads):
            c = cb - 1 - cc
            sl = pl.ds(pl.multiple_of(c * CHUNK, CHUNK), CHUNK)
            out = []
            for h in range(GLA_HEADS):
                qk, vv = slice(GLA_DK * h, GLA_DK * (h + 1)), slice(GLA_DV * h, GLA_DV * (h + 1))
                s_before = jnp.where(step == nsteps - 1, 0.0, sp_ref[h, 0])
                s_prev = jnp.where(c == 0, s_before, st_ref[h, jnp.maximum(c - 1, 0)])
                _, vjp = jax.vjp(functools.partial(_gla_chunk, ltri=ltri_ref[...]),
                                 s_prev, q_ref[sl, qk], k_ref[sl, qk], v_ref[sl, vv], la_ref[sl, qk])
                ds_prev, dq, dk, dv, dla = vjp((grads[h], do_ref[sl, vv]))
                dq_ref[sl, qk] = dq
                dk_ref[sl, qk] = dk
                dv_ref[sl, vv] = dv
                dla_ref[sl, qk] = dla
                out.append(ds_prev)
            return tuple(out)

        grads = lax.fori_loop(0, cb, chunk, tuple(ds_scr[h] for h in range(GLA_HEADS)))
        for h in range(GLA_HEADS):
            ds_scr[h] = grads[h]

    r = lambda i: nsteps - 1 - i
    qk = lambda cb_: pl.BlockSpec((rows, GLA_QK), lambda i, cb_=cb_: (r(i), cb_))
    o512 = pl.BlockSpec((rows, GLA_QK), lambda i: (r(i), 0))
    o1024 = pl.BlockSpec((rows, GLA_V), lambda i: (r(i), 0))
    return pl.pallas_call(
        body, name="gla_core_bwd", grid=(nsteps,),
        in_specs=[qk(0), qk(1), pl.BlockSpec((rows, GLA_V), lambda i: (r(i), 1)), qk(0),
                  pl.BlockSpec((CHUNK, CHUNK), lambda i: (0, 0)),
                  pl.BlockSpec((GLA_HEADS, cb, GLA_DV, GLA_DK), lambda i: (0, r(i), 0, 0)),
                  pl.BlockSpec((GLA_HEADS, 1, GLA_DV, GLA_DK), lambda i: (0, jnp.maximum(r(i) * cb - 1, 0), 0, 0)),
                  o1024],
        out_specs=[o512, o512, o1024, o512],
        out_shape=(jax.ShapeDtypeStruct((T, GLA_QK), F32), jax.ShapeDtypeStruct((T, GLA_QK), F32),
                   jax.ShapeDtypeStruct((T, GLA_V), F32), jax.ShapeDtypeStruct((T, GLA_QK), F32)),
        scratch_shapes=[pltpu.VMEM((GLA_HEADS, GLA_DV, GLA_DK), F32)],
        compiler_params=_params(("arbitrary",)),
    )(pa, pa, pa, la, ltri, states, states, do)


def _rowwise(fn, name, T, tm, rows, pars, row_outs, acc_outs):
    nr, npar, nro = len(rows), len(pars), len(row_outs)
    tm = min(tm, T)
    nsteps = T // tm

    def body(*refs):
        i = pl.program_id(0)
        ins = [r[...] for r in refs[:nr + npar]]
        outs, accs = fn(i, *ins)
        for r, o in zip(refs[nr + npar:nr + npar + nro], outs):
            r[...] = o.astype(r.dtype)
        for r, a in zip(refs[nr + npar + nro:], accs):
            @pl.when(i == 0)
            def _(r=r, a=a):
                r[...] = a

            @pl.when(i > 0)
            def _(r=r, a=a):
                r[...] += a

    def rspec(width, cb, kind):
        if kind == "cur":
            return pl.BlockSpec((tm, width), lambda i: (i, cb))
        if kind == "prev":
            return pl.BlockSpec((8, width), lambda i: (jnp.maximum(i * (tm // 8) - 1, 0), cb))
        return pl.BlockSpec((8, width), lambda i: (jnp.minimum((i + 1) * (tm // 8), T // 8 - 1), cb))

    in_specs = [rspec(w, cb, kind) for (_, w, cb, kind) in rows]
    in_specs += [pl.BlockSpec(p.shape, lambda i, nd=p.ndim: (0,) * nd) for p in pars]
    out_specs = [pl.BlockSpec((tm, w), lambda i: (i, 0)) for (w, _) in row_outs]
    out_specs += [pl.BlockSpec(s, lambda i, nd=len(s): (0,) * nd) for s in acc_outs]
    out_shape = [jax.ShapeDtypeStruct((T, w), dt) for (w, dt) in row_outs]
    out_shape += [jax.ShapeDtypeStruct(s, F32) for s in acc_outs]
    res = pl.pallas_call(
        body, name=name, grid=(nsteps,), in_specs=in_specs, out_specs=out_specs, out_shape=out_shape,
        compiler_params=_params(("arbitrary",)),
    )(*[r[0] for r in rows], *pars)
    return res


def _cur(a, width=None, cb=0):
    return (a, a.shape[1] if width is None else width, cb, "cur")


def _sigmoid(x):
    return 1.0 / (1.0 + jnp.exp(-x))


def _silu(x):
    return x * _sigmoid(x)


def _softplus(x):
    return jnp.maximum(x, 0.0) + jnp.log(1.0 + jnp.exp(-jnp.abs(x)))


def _rms(x, g):
    return x * lax.rsqrt(jnp.mean(x * x, axis=-1, keepdims=True) + NORM_EPS) * g


def _dot_hi(a, b):
    return jnp.dot(a, b, precision=lax.Precision.HIGH, preferred_element_type=F32)


def _rms_fwd(x, g, name):
    T = x.shape[0]
    fn = lambda i, xb, gb: ((_rms(xb, gb),), ())
    return _rowwise(fn, name, T, 256, [_cur(x)], [g], [(D, BF16)], [])[0]


def _rms_bwd(x, g, dh, dres, name):
    T = x.shape[0]

    def fn(i, xb, dhb, drb, gb):
        _, vjp = jax.vjp(_rms, xb, gb)
        dx, dg = vjp(dhb)
        return (drb + dx,), (dg,)

    return _rowwise(fn, name, T, 256, [_cur(x), _cur(dh), _cur(dres)], [g], [(D, F32)], [(1, D)])


def _loss_bwd(x, target, g):
    T = x.shape[0]

    def loss(xb, gb, tb):
        err = _rms(xb, gb) - tb
        return 0.5 * jnp.sum(jnp.mean(err * err, axis=-1, keepdims=True))

    def fn(i, xb, tb, gb):
        val, (dx, dg) = jax.value_and_grad(loss, argnums=(0, 1))(xb, gb, tb)
        return (dx,), (jnp.full((1, 128), val, F32), dg)

    return _rowwise(fn, "loss_bwd", T, 256, [_cur(x), _cur(target)], [g], [(D, F32)], [(1, 128), (1, D)])


def _gla_la(a_down, w_a2, b_a):
    return -_softplus(-(_bdot(a_down, w_a2, _NN) + b_a)) * (1.0 / GLA_TAU)


def _gla_prep(pa, w_a2, b_a):
    T = pa.shape[0]
    fn = lambda i, ab, wb, bb: ((_gla_la(ab, wb, bb),), ())
    return _rowwise(fn, "gla_prep", T, 512, [_cur(pa, LORA_P, (2 * GLA_QK + 2 * GLA_V) // LORA_P)], [w_a2, b_a],
                    [(GLA_QK, F32)], [])[0]


def _gla_prep_bwd(pa, w_a2, b_a, dla):
    T = pa.shape[0]

    def fn(i, ab, dlab, wb, bb):
        _, vjp = jax.vjp(_gla_la, ab, wb, bb)
        da, dw, db = vjp(dlab)
        return (da,), (dw, db)

    return _rowwise(fn, "gla_prep_bwd", T, 512, [_cur(pa, LORA_P, (2 * GLA_QK + 2 * GLA_V) // LORA_P), _cur(dla)],
                    [w_a2, b_a], [(LORA_P, BF16)], [(LORA_P, GLA_QK), (1, GLA_QK)])


def _gla_out(o, r, gn, ind, ind_t):
    ms = _dot_hi(_dot_hi(o * o, ind) * (1.0 / GLA_DV), ind_t)
    return o * lax.rsqrt(ms + NORM_EPS) * gn * _silu(r)


def _gla_post(o_raw, pa, gn, ind, ind_t):
    T = pa.shape[0]
    fn = lambda i, ob, rb, gb, a, b: ((_gla_out(ob, rb, gb, a, b),), ())
    return _rowwise(fn, "gla_post", T, 256, [_cur(o_raw), _cur(pa, GLA_V, 2)], [gn, ind, ind_t], [(GLA_V, BF16)], [])[0]


def _gla_post_bwd(o_raw, pa, gn, ind, ind_t, do):
    T = pa.shape[0]

    def fn(i, ob, rb, dob, gb, a, b):
        _, vjp = jax.vjp(lambda o, r, g: _gla_out(o, r, g, a, b), ob, rb, gb)
        d_o, d_r, d_g = vjp(dob)
        return (d_o, d_r), (d_g,)

    return _rowwise(fn, "gla_post_bwd", T, 256, [_cur(o_raw), _cur(pa, GLA_V, 2), _cur(do)], [gn, ind, ind_t],
                    [(GLA_V, F32), (GLA_V, BF16)], [(1, GLA_V)])


def _shift_rows(cur, prev8, i):
    first = jnp.where(i == 0, 0.0, prev8[7:8, :])
    rolled = pltpu.roll(cur, 1, 0)
    return jnp.where(lax.broadcasted_iota(jnp.int32, cur.shape, 0) == 0, first, rolled)


def _rw_gates(rw, w0, w_w2, a0, w_a2, w_g2, k_k, k_a, ind, ind_t):
    rk = rw[:, RW_W:2 * RW_W]
    wd = rw[:, 3 * RW_W:3 * RW_W + LORA_P]
    ad = rw[:, 3 * RW_W + LORA_P:3 * RW_W + 2 * LORA_P]
    gd = rw[:, 3 * RW_W + 2 * LORA_P:]
    w_raw = w0 + _bdot(jnp.tanh(wd), w_w2, _NN)
    w = jnp.exp(-jnp.exp(-_softplus(-w_raw) - 0.5))
    a = _sigmoid(a0 + _bdot(ad, w_a2, _NN))
    g = _bdot(_sigmoid(gd), w_g2, _NN)
    kk = rk * k_k
    kk = kk * _dot_hi(lax.rsqrt(jnp.maximum(_dot_hi(kk * kk, ind), 1e-24)), ind_t)
    k2 = rk * (1.0 + (a - 1.0) * k_a)
    return w, k2, kk, kk * a, g


def _rw_prep(pr, mu, gate_pars):
    T = pr.shape[0]

    def fn(i, cur, prev8, mub, *gp):
        rw = cur + mub * (_shift_rows(cur, prev8, i) - cur)
        return (rw,) + _rw_gates(rw, *gp), ()

    return _rowwise(fn, "rw_prep", T, 256, [_cur(pr), (pr, PR_W, 0, "prev")], [mu, *gate_pars],
                    [(PR_W, F32)] + [(RW_W, F32)] * 5, [])


def _rw_prep_bwd(pr, mu, gate_pars, d_r, d_v, d_w, d_k2, d_kk, d_b, d_g):
    T = pr.shape[0]
    rows = [_cur(pr), (pr, PR_W, 0, "prev")] + [_cur(x) for x in (*d_r, *d_v, d_w, *d_k2, d_kk, d_b, d_g)]
    acc = [(1, PR_W)] + [tuple(p.shape) for p in gate_pars[:-2]]

    def fn(i, cur, prev8, dr1, dr2, dv1, dv2, dw, dk1, dk2, dkk, db, dg, mub, *gp):
        sh = _shift_rows(cur, prev8, i)
        rw = cur + mub * (sh - cur)
        _, vjp = jax.vjp(lambda x, *p: _rw_gates(x, *p, gp[-2], gp[-1]), rw, *gp[:-2])
        grads = vjp((dw, dk1 + dk2, dkk, db, dg))
        zeros = jnp.zeros((cur.shape[0], PR_W - 3 * RW_W), F32)
        drw = grads[0] + jnp.concatenate([dr1 + dr2, jnp.zeros_like(dr1), dv1 + dv2, zeros], axis=1)
        dmu = jnp.sum(drw * (sh - cur), axis=0, keepdims=True)
        return (drw,), (dmu, *grads[1:])

    return _rowwise(fn, "rw_prep_bwd", T, 128, rows, [mu, *gate_pars], [(PR_W, F32)], acc)


def _shift_bwd(drw, mu):
    T = drw.shape[0]
    tm = min(256, T)

    def fn(i, cur, next8, mub):
        last = jnp.where(i == T // tm - 1, 0.0, next8[0:1, :])
        rolled = pltpu.roll(cur, cur.shape[0] - 1, 0)
        nxt = jnp.where(lax.broadcasted_iota(jnp.int32, cur.shape, 0) == cur.shape[0] - 1, last, rolled)
        return ((1.0 - mub) * cur + mub * nxt,), ()

    return _rowwise(fn, "shift_bwd", T, tm, [_cur(drw), (drw, PR_W, 0, "next")], [mu], [(PR_W, BF16)], [])[0]


def _rw_out(y, r, v, k2, g, lnx_w, lnx_b, r_k, ind, ind_t):
    mean = _dot_hi(_dot_hi(y, ind) * (1.0 / RW_HD), ind_t)
    yc = y - mean
    var = _dot_hi(_dot_hi(yc * yc, ind) * (1.0 / RW_HD), ind_t)
    yn = yc * lax.rsqrt(var + GN_EPS) * lnx_w + lnx_b
    bonus = _dot_hi(_dot_hi(r * k2 * r_k, ind), ind_t) * v
    return (yn + bonus) * g


def _rw_post(y, rw, k2, g, pars):
    T = y.shape[0]
    fn = lambda i, *a: ((_rw_out(*a),), ())
    return _rowwise(fn, "rw_post", T, 256, [_cur(y), _cur(rw, RW_W, 0), _cur(rw, RW_W, 2), _cur(k2), _cur(g)], pars,
                    [(RW_W, BF16)], [])[0]


def _rw_post_bwd(y, rw, k2, g, pars, do):
    T = y.shape[0]

    def fn(i, yb, rb, vb, kb, gb, dob, lw, lb, rk, ind, ind_t):
        _, vjp = jax.vjp(lambda *a: _rw_out(*a, ind, ind_t), yb, rb, vb, kb, gb, lw, lb, rk)
        gr = vjp(dob)
        return gr[:5], gr[5:]

    return _rowwise(fn, "rw_post_bwd", T, 256,
                    [_cur(y), _cur(rw, RW_W, 0), _cur(rw, RW_W, 2), _cur(k2), _cur(g), _cur(do)], pars,
                    [(RW_W, F32)] * 5, [(1, RW_W)] * 3)


def _merge_bwd(dm, y_gla, y_rw, pg, gate_b):
    T = dm.shape[0]

    def fn(i, dmb, ya, yr, p1, p2, gb):
        g1 = _sigmoid(p1 + gb[:, :D])
        g2 = _sigmoid(p2 + gb[:, D:])
        dp1 = dmb * ya * g1 * (1.0 - g1)
        dp2 = dmb * yr * g2 * (1.0 - g2)
        dp = jnp.concatenate([dp1, dp2], axis=1)
        return (dmb * g1, dmb * g2, dp), (jnp.sum(dp, axis=0, keepdims=True),)

    return _rowwise(fn, "merge_bwd", T, 256, [_cur(dm), _cur(y_gla), _cur(y_rw), _cur(pg, D, 0), _cur(pg, D, 1)],
                    [gate_b], [(D, BF16), (D, BF16), (PG_W, BF16)], [(1, PG_W)])


_NN = (((1,), (0,)), ((), ()))
_NT = (((1,), (1,)), ((), ()))
_TN = (((0,), (0,)), ((), ()))


def _bdot(a, b, dims):
    return lax.dot_general(a.astype(BF16), b.astype(BF16), dims, preferred_element_type=F32)


def _accumulate(k, nk, acc, part, finish):
    if nk == 1:
        finish(part)
        return

    @pl.when(k == 0)
    def _():
        acc[...] = part

    @pl.when(k > 0)
    def _():
        acc[...] += part

    @pl.when(k == nk - 1)
    def _():
        finish(acc[...])


def _call(body, comm, name, grid, in_specs, out_specs, out_shape, scratch_shapes, sem, operands):
    if comm is None:
        return pl.pallas_call(body, name=name, grid=grid, in_specs=in_specs, out_specs=out_specs, out_shape=out_shape,
                              scratch_shapes=scratch_shapes, compiler_params=_params(sem))(*operands)
    kind, arrays = comm
    nc, n_in, n_out, n_scr = len(arrays), len(in_specs), len(out_shape), len(scratch_shapes)
    total = 1
    for n in grid:
        total *= n

    def with_comm(*refs):
        own = refs[:n_in] + refs[n_in + nc:n_in + nc + n_out] + refs[n_in + 2 * nc + n_out:n_in + 2 * nc + n_out + n_scr]
        c_in, c_out, sems = refs[n_in:n_in + nc], refs[n_in + nc + n_out:n_in + 2 * nc + n_out], refs[-3:]
        step = 0
        for axis, n in enumerate(grid):
            step = step * n + pl.program_id(axis)
        start, *forward, finish = _PLANS[kind][0](c_in, c_out, *sems)
        pl.when(step == 0)(start)
        for stage in forward:
            pl.when(step == (total - 1 if kind == "gather_late" else total // 2))(stage)
        body(*own)
        pl.when(step == total - 1)(finish)

    return pl.pallas_call(
        with_comm, name=name, grid=grid, in_specs=list(in_specs) + [_ANY] * nc, out_specs=list(out_specs) + [_ANY] * nc,
        out_shape=list(out_shape) + [jax.ShapeDtypeStruct(_PLANS[kind][1](a.shape), a.dtype) for a in arrays],
        scratch_shapes=list(scratch_shapes) + _comm_sems(nc), compiler_params=_params(("arbitrary",) * len(grid)),
    )(*operands, *arrays)


def _matmul(a, b, mode, M, N, K, tm, tn, tk, name, a_off=(0, 0), b_off=(0, 0), res=None, scale=1.0, out_dtype=F32,
            comm=None):
    tm, tn, tk = min(tm, M), min(tn, N), min(tk, K)
    nk = K // tk
    if mode == "nn":
        a_spec = pl.BlockSpec((tm, tk), lambda i, j, k: (i + a_off[0], k + a_off[1]))
        b_spec = pl.BlockSpec((tk, tn), lambda i, j, k: (k + b_off[0], j + b_off[1]))
        dims = _NN
    elif mode == "nt":
        a_spec = pl.BlockSpec((tm, tk), lambda i, j, k: (i + a_off[0], k + a_off[1]))
        b_spec = pl.BlockSpec((tn, tk), lambda i, j, k: (j + b_off[0], k + b_off[1]))
        dims = _NT
    else:
        a_spec = pl.BlockSpec((tk, tm), lambda i, j, k: (k + a_off[0], i + a_off[1]))
        b_spec = pl.BlockSpec((tk, tn), lambda i, j, k: (k + b_off[0], j + b_off[1]))
        dims = _TN
    o_spec = pl.BlockSpec((tm, tn), lambda i, j, k: (i, j))

    def body(a_ref, b_ref, *rest):
        r_ref = rest[0] if res is not None else None
        o_ref = rest[1] if res is not None else rest[0]
        acc = rest[-1] if nk > 1 else None

        def finish(total):
            total = total * scale if scale != 1.0 else total
            if r_ref is not None:
                total = r_ref[...] + total
            o_ref[...] = total.astype(out_dtype)

        _accumulate(pl.program_id(2), nk, acc, _bdot(a_ref[...], b_ref[...], dims), finish)

    out = _call(body, comm, name, (M // tm, N // tn, nk), [a_spec, b_spec] + ([o_spec] if res is not None else []),
                [o_spec], [jax.ShapeDtypeStruct((M, N), out_dtype)], [pltpu.VMEM((tm, tn), F32)] if nk > 1 else [],
                ("parallel", "parallel", "arbitrary"), [a, b] + ([res] if res is not None else []))
    return out[0] if comm is None else out


def _ffn_up(h, wg, wu, name, comm=None):
    T = h.shape[0]
    tm = min(1024, T)

    def body(h_ref, wg_ref, wu_ref, a_ref, u_ref, s_ref):
        hb = h_ref[...]
        a = _bdot(hb, wg_ref[...], _NN)
        u = _bdot(hb, wu_ref[...], _NN)
        a_ref[...] = a
        u_ref[...] = u
        s_ref[...] = (_silu(a) * u).astype(BF16)

    w_spec = pl.BlockSpec((None, D, FSH), lambda i, j: (j, 0, 0))
    o_spec = pl.BlockSpec((None, tm, FSH), lambda i, j: (j, i, 0))
    sh = lambda dt: jax.ShapeDtypeStruct((NDEV, T, FSH), dt)
    return _call(body, comm, name, (T // tm, NDEV), [pl.BlockSpec((tm, D), lambda i, j: (i, 0)), w_spec, w_spec],
                 [o_spec] * 3, [sh(F32), sh(F32), sh(BF16)], [], ("parallel", "arbitrary"), [h, wg, wu])


def _ffn_gate(h, wg, name, comm=None):
    T = h.shape[0]
    tm = min(1024, T)

    def body(h_ref, wg_ref, a_ref):
        a_ref[...] = _bdot(h_ref[...], wg_ref[...], _NN)

    return _call(body, comm, name, (T // tm, NDEV),
                 [pl.BlockSpec((tm, D), lambda i, j: (i, 0)), pl.BlockSpec((None, D, FSH), lambda i, j: (j, 0, 0))],
                 [pl.BlockSpec((None, tm, FSH), lambda i, j: (j, i, 0))], [jax.ShapeDtypeStruct((NDEV, T, FSH), F32)], [],
                 ("parallel", "arbitrary"), [h, wg])


def _ffn_up_after_gate(h, wu, a, name, comm=None):
    T = h.shape[0]
    tm = min(1024, T)

    def body(h_ref, wu_ref, a_ref, u_ref, s_ref):
        u = _bdot(h_ref[...], wu_ref[...], _NN)
        u_ref[...] = u
        s_ref[...] = (_silu(a_ref[...]) * u).astype(BF16)

    act = pl.BlockSpec((None, tm, FSH), lambda i, j: (j, i, 0))
    sh = lambda dt: jax.ShapeDtypeStruct((NDEV, T, FSH), dt)
    return _call(body, comm, name, (T // tm, NDEV),
                 [pl.BlockSpec((tm, D), lambda i, j: (i, 0)), pl.BlockSpec((None, D, FSH), lambda i, j: (j, 0, 0)), act],
                 [act, act], [sh(F32), sh(BF16)], [], ("parallel", "arbitrary"), [h, wu, a])


def _ffn_down(s, wd, x, name, comm=None):
    T = x.shape[0]
    tm, tn, sh = min(1024, T), 1024, 4

    def body(s_ref, wd_ref, x_ref, o_ref, acc):
        part = _bdot(s_ref[0], wd_ref[0], _NN)
        for q in range(1, sh):
            part = part + _bdot(s_ref[q], wd_ref[q], _NN)

        def finish(total):
            o_ref[...] = x_ref[...] + 0.5 * total

        _accumulate(pl.program_id(2), NDEV // sh, acc, part, finish)

    xo = pl.BlockSpec((tm, tn), lambda i, n, j: (i, n))
    out = _call(body, comm, name, (T // tm, D // tn, NDEV // sh),
                [pl.BlockSpec((sh, tm, FSH), lambda i, n, j: (j, i, 0)),
                 pl.BlockSpec((sh, FSH, tn), lambda i, n, j: (j, 0, n)), xo],
                [xo], [jax.ShapeDtypeStruct((T, D), F32)], [pltpu.VMEM((tm, tn), F32)],
                ("parallel", "parallel", "arbitrary"), [s, wd, x])
    return out[0] if comm is None else out


def _ffn_bwd_hidden(dx, wd, a, u, name):
    T = dx.shape[0]
    tm = min(1024, T)

    def body(dx_ref, wd_ref, a_ref, u_ref, da_ref, du_ref):
        ds = 0.5 * _bdot(dx_ref[...], wd_ref[...], _NT)
        av = a_ref[...]
        sg = _sigmoid(av)
        da_ref[...] = (ds * u_ref[...] * (sg * (1.0 + av * (1.0 - sg)))).astype(BF16)
        du_ref[...] = (ds * (av * sg)).astype(BF16)

    act = pl.BlockSpec((None, tm, FSH), lambda i, j: (j, i, 0))
    sh = jax.ShapeDtypeStruct((NDEV, T, FSH), BF16)
    return pl.pallas_call(
        body, name=name, grid=(T // tm, NDEV),
        in_specs=[pl.BlockSpec((tm, D), lambda i, j: (i, 0)), pl.BlockSpec((None, FSH, D), lambda i, j: (j, 0, 0)),
                  act, act],
        out_specs=[act, act], out_shape=(sh, sh),
        compiler_params=_params(("parallel", "arbitrary")),
    )(dx, wd, a, u)


def _ffn_bwd_input(da, du, wg, wu, name, comm=None):
    T = da.shape[1]
    tm, tn, sh = min(1024, T), 1024, 2

    def body(da_ref, du_ref, wg_ref, wu_ref, o_ref, acc):
        part = _bdot(da_ref[0], wg_ref[0], _NT) + _bdot(du_ref[0], wu_ref[0], _NT)
        for q in range(1, sh):
            part = part + _bdot(da_ref[q], wg_ref[q], _NT) + _bdot(du_ref[q], wu_ref[q], _NT)

        def finish(total):
            o_ref[...] = total

        _accumulate(pl.program_id(2), NDEV // sh, acc, part, finish)

    act = pl.BlockSpec((sh, tm, FSH), lambda i, n, j: (j, i, 0))
    wsp = pl.BlockSpec((sh, tn, FSH), lambda i, n, j: (j, n, 0))
    out = _call(body, comm, name, (T // tm, D // tn, NDEV // sh), [act, act, wsp, wsp],
                [pl.BlockSpec((tm, tn), lambda i, n, j: (i, n))], [jax.ShapeDtypeStruct((T, D), F32)],
                [pltpu.VMEM((tm, tn), F32)], ("parallel", "parallel", "arbitrary"), [da, du, wg, wu])
    return out[0] if comm is None else out


def _ffn_grad_up(h, da, du, name, comm=None, core_major=False):
    T = h.shape[0]
    tm, tk = 1024, min(4096, T)
    nk = T // tk

    def body(h_ref, da_ref, du_ref, o_ref, acc_a, acc_u):
        k = pl.program_id(2)
        hb = h_ref[...]
        for acc, ref, slot in ((acc_a, da_ref, 0), (acc_u, du_ref, 1)):
            def finish(total, slot=slot):
                o_ref[slot] = total.astype(BF16)

            _accumulate(k, nk, acc, _bdot(hb, ref[...], _TN), finish)

    act = pl.BlockSpec((None, tk, FSH), lambda j, i, t: (j, t, 0))
    if core_major:
        o_spec = pl.BlockSpec((None, None, 2, tm, FSH), lambda j, i, t: (j % 2, j // 2, 0, i, 0))
        o_shape = jax.ShapeDtypeStruct((2, NDEV // 2, 2, D, FSH), BF16)
    else:
        o_spec = pl.BlockSpec((None, 2, tm, FSH), lambda j, i, t: (j, 0, i, 0))
        o_shape = jax.ShapeDtypeStruct((NDEV, 2, D, FSH), BF16)
    out = _call(body, comm, name, (NDEV, D // tm, nk), [pl.BlockSpec((tk, tm), lambda j, i, t: (t, i)), act, act],
                [o_spec], [o_shape],
                [pltpu.VMEM((tm, FSH), F32), pltpu.VMEM((tm, FSH), F32)], ("parallel", "parallel", "arbitrary"), [h, da, du])
    return out[0] if comm is None else out


def _ffn_grad_down(s, dx, name):
    T = dx.shape[0]
    tn, tk = 1024, min(2048, T)
    nk = T // tk

    def body(s_ref, dx_ref, o_ref, acc):
        def finish(total):
            o_ref[...] = (0.5 * total).astype(BF16)

        _accumulate(pl.program_id(2), nk, acc, _bdot(s_ref[...], dx_ref[...], _TN), finish)

    return pl.pallas_call(
        body, name=name, grid=(NDEV, D // tn, nk),
        in_specs=[pl.BlockSpec((None, tk, FSH), lambda j, n, t: (j, t, 0)), pl.BlockSpec((tk, tn), lambda j, n, t: (t, n))],
        out_specs=pl.BlockSpec((None, FSH, tn), lambda j, n, t: (j, 0, n)),
        out_shape=jax.ShapeDtypeStruct((NDEV, DFF // NDEV, D), BF16),
        scratch_shapes=[pltpu.VMEM((FSH, tn), F32)],
        compiler_params=_params(("parallel", "parallel", "arbitrary")),
    )(s, dx)


def _branch_merge(o_gla, o_rw, wb, pg, gate_b):
    T = o_gla.shape[0]
    tm, tn = min(1024, T), 512

    def body(og_ref, or_ref, w1_ref, w2_ref, p1_ref, p2_ref, b1_ref, b2_ref, yg_ref, yr_ref, m_ref):
        yg = _bdot(og_ref[...], w1_ref[...], _NN)
        yr = _bdot(or_ref[...], w2_ref[...], _NN)
        yg_ref[...] = yg
        yr_ref[...] = yr
        m_ref[...] = (_sigmoid(p1_ref[...] + b1_ref[...]) * yg + _sigmoid(p2_ref[...] + b2_ref[...]) * yr).astype(BF16)

    nj = D // tn
    act = pl.BlockSpec((tm, GLA_V), lambda i, j: (i, 0))
    out = pl.BlockSpec((tm, tn), lambda i, j: (i, j))
    return pl.pallas_call(
        body, name="branch_merge", grid=(T // tm, nj),
        in_specs=[act, act, pl.BlockSpec((GLA_V, tn), lambda i, j: (0, j)), pl.BlockSpec((RW_W, tn), lambda i, j: (1, j)),
                  out, pl.BlockSpec((tm, tn), lambda i, j: (i, nj + j)),
                  pl.BlockSpec((1, tn), lambda i, j: (0, j)), pl.BlockSpec((1, tn), lambda i, j: (0, nj + j))],
        out_specs=[out, out, out],
        out_shape=(jax.ShapeDtypeStruct((T, D), F32), jax.ShapeDtypeStruct((T, D), F32), jax.ShapeDtypeStruct((T, D), BF16)),
        compiler_params=_params(("parallel", "arbitrary")),
    )(o_gla, o_rw, wb, wb, pg, pg, gate_b, gate_b)


def _head_indicator(width, heads):
    col = lax.broadcasted_iota(jnp.int32, (width, 128), 0) // (width // heads)
    ind = (col == lax.broadcasted_iota(jnp.int32, (width, 128), 1)).astype(F32)
    return ind, ind.T


def _ffn_fwd(x, g, wg, wu, wd, tag):
    h = _rms_fwd(x, g, "rms_" + tag)
    a, u, s = _ffn_up(h, wg, wu, "ffn_up_" + tag)
    return _ffn_down(s, wd, x, "ffn_down_" + tag), (h, a, u, s)


def _ffn_fwd_gathering(x, g, wg, wu_block, wd_block, next_block, tag):
    h = _rms_fwd(x, g, "rms_" + tag)
    a, wu = _ffn_gate(h, wg, "ffn_gate_" + tag, comm=("gather", [wu_block]))
    u, s, wd = _ffn_up_after_gate(h, wu, a, "ffn_up_" + tag, comm=("gather", [wd_block]))
    y, gathered = _ffn_down(s, wd, x, "ffn_down_" + tag, comm=("gather_late", [next_block]))
    return y, (h, a, u, s), wu, wd, gathered


def _ffn_bwd(dy, x, g, wg, wu, wd, saved, tag, exchange=False):
    h, a, u, s = saved
    dwd = _ffn_grad_down(s, dy, "ffn_grad_down_" + tag)
    da, du = _ffn_bwd_hidden(dy, wd, a, u, "ffn_bwd_hidden_" + tag)
    if exchange:
        dw_up, dwd = _ffn_grad_up(h, da, du, "ffn_grad_up_" + tag, comm=("exchange", [dwd]), core_major=True)
        dh, dw_up = _ffn_bwd_input(da, du, wg, wu, "ffn_bwd_input_" + tag, comm=("quad", [_chip_sum(dw_up, "up_" + tag)]))
    else:
        dw_up = _ffn_grad_up(h, da, du, "ffn_grad_up_" + tag)
        dh = _ffn_bwd_input(da, du, wg, wu, "ffn_bwd_input_" + tag)
    dx, dg = _rms_bwd(x, g, dh, dy, "rms_bwd_" + tag)
    return dx, dg, dw_up, dwd


def _local_step(x, target, w, blocks):
    T = x.shape[0]
    ind16, ind16_t = _head_indicator(RW_W, RW_HEADS)
    ind4, ind4_t = _head_indicator(GLA_V, GLA_HEADS)
    ltri = jnp.tril(jnp.ones((CHUNK, CHUNK), F32))
    gate_pars = [w["w0"], w["w_w2"], w["a0"], w["w_a2"], w["w_g2"], w["k_k"], w["k_a"], ind16, ind16_t]
    post_pars = [w["lnx_w"], w["lnx_b"], w["r_k"], ind16, ind16_t]

    x1, ffn1, wu1, wd1, g_proj = _ffn_fwd_gathering(x, w["g1"], w["wg1"], blocks["wu1"], blocks["wd1"], blocks["win"], "1")
    win = _align_proj(_unshard_cols(g_proj))
    h2 = _rms_fwd(x1, w["g2"], "rms_mix")
    proj = lambda n, off, name: _matmul(h2, win, "nn", T, n, D, 1024, 512, D, name, b_off=(0, off // 512))
    pg = proj(PG_W, 0, "proj_gate")
    pr = proj(PR_W, PG_W, "proj_rwkv")
    pa = proj(PA_W, PG_W + PR_W, "proj_gla")
    la = _gla_prep(pa, w["gla_w_a2"], w["gla_b_a"])
    o_raw, gla_states = _gla_core_fwd(pa, la, ltri)
    o_gla = _gla_post(o_raw, pa, w["gn"], ind4, ind4_t)
    rw, dec, k2, kk, b, g = _rw_prep(pr, w["mu"], gate_pars)
    y, rw_states, rw_sa, g_up2, g_down2 = _rw_core_fwd(rw, dec, k2, kk, b, gather=blocks["late"])
    w = {**w, **_late_weights(g_up2, g_down2)}
    o_rw = _rw_post(y, rw, k2, g, post_pars)
    y_gla, y_rw, merged = _branch_merge(o_gla, o_rw, w["wb"], pg, w["gate_b"])
    x2 = _matmul(merged, w["wo"], "nn", T, D, D, 1024, 1024, D, "out_proj", res=x1)
    x3, ffn2 = _ffn_fwd(x2, w["g3"], w["wg2"], w["wu2"], w["wd2"], "2")
    dx3, loss, d_gf = _loss_bwd(x3, target, w["gf"])

    grads = {"gf": d_gf}
    dx2, grads["g3"], grads["up2"], grads["wd2"] = _ffn_bwd(
        dx3, x2, w["g3"], w["wg2"], w["wu2"], w["wd2"], ffn2, "2")
    dm = _matmul(dx2, w["wo"], "nt", T, D, D, 1024, 1024, D, "out_proj_bwd")
    grads["wo"] = _matmul(merged, dx2, "tn", D, D, T, 1024, 1024, 2048, "out_proj_grad", out_dtype=BF16)
    dy_gla, dy_rw, dpg, grads["gate_b"] = _merge_bwd(dm, y_gla, y_rw, pg, w["gate_b"])
    do_gla = _matmul(dy_gla, w["wb"], "nt", T, GLA_V, D, 1024, 1024, D, "branch_gla_bwd")
    do_rw = _matmul(dy_rw, w["wb"], "nt", T, RW_W, D, 1024, 1024, D, "branch_rwkv_bwd", b_off=(1, 0))
    grads["wb"] = jnp.concatenate([
        _matmul(o_gla, dy_gla, "tn", GLA_V, D, T, 1024, 1024, 4096, "branch_gla_grad", out_dtype=BF16),
        _matmul(o_rw, dy_rw, "tn", RW_W, D, T, 1024, 1024, 4096, "branch_rwkv_grad", out_dtype=BF16)], axis=0)
    dy, dr2, dv2, dk2b, dg, grads["lnx_w"], grads["lnx_b"], grads["r_k"] = _rw_post_bwd(y, rw, k2, g, post_pars, do_rw)
    early = _late_grad_parts(grads)
    received = {}
    dr1, dw, dk2a, dv1, dkk, db, received["up2"], received["down2"] = _rw_core_bwd(
        rw, dec, k2, kk, b, rw_states, rw_sa, dy, exchange=early)
    drw, grads["mu"], grads["w0"], grads["w_w2"], grads["a0"], grads["w_a2"], grads["w_g2"], grads["k_k"], grads["k_a"] = (
        _rw_prep_bwd(pr, w["mu"], gate_pars, (dr1, dr2), (dv1, dv2), dw, (dk2a, dk2b), dkk, db, dg))
    dpr = _shift_bwd(drw, w["mu"])
    do_raw, dr_gla, grads["gn"] = _gla_post_bwd(o_raw, pa, w["gn"], ind4, ind4_t, do_gla)
    dq, dk, dv, dla = _gla_core_bwd(pa, la, ltri, gla_states, do_raw)
    da_down, grads["gla_w_a2"], grads["gla_b_a"] = _gla_prep_bwd(pa, w["gla_w_a2"], w["gla_b_a"], dla)
    dpa = jnp.concatenate([dq.astype(BF16), dk.astype(BF16), dv.astype(BF16), dr_gla, da_down,
                           jnp.zeros((T, PA_W - PA_USED), BF16)], axis=1)
    dp = jnp.concatenate([dpg, dpr, dpa], axis=1)
    d_win = _matmul(h2, dp, "tn", D, DIN_P, T, 1024, 1024, 4096, "proj_grad", out_dtype=BF16)
    d_win = _shard_cols(_unalign_proj(d_win)).reshape(NDEV // 2, 2, D, DIN_SH).swapaxes(0, 1)
    dh2, received["win"] = _matmul(dp, win, "nt", T, D, DIN_P, 1024, 1024, DIN_P // 4, "proj_bwd",
                                   comm=("quad", [_chip_sum(d_win, "win")]))
    dx1, grads["g2"] = _rms_bwd(x1, w["g2"], dh2, dx2, "rms_bwd_mix")
    dx, grads["g1"], received["up1"], received["wd1"] = _ffn_bwd(
        dx1, x, w["g1"], w["wg1"], wu1, wd1, ffn1, "1", exchange=True)
    return loss, dx, grads, received


BIG = ("ffn1_wg", "ffn1_wu", "ffn1_wd", "w_in", "w_branch", "w_out", "ffn2_wg", "ffn2_wu", "ffn2_wd")
SMALL_SHARDED = ("gla_w_a2", "rwkv_w_w2", "rwkv_w_a2", "rwkv_w_g2")
REPLICATED = ("ffn1_norm", "mix_norm", "gla_b_a", "gla_gn_w", "rwkv_mu", "rwkv_w0", "rwkv_a0", "rwkv_k_k", "rwkv_k_a",
              "rwkv_r_k", "rwkv_lnx_w", "rwkv_lnx_b", "gate_b", "ffn2_norm", "final_norm")
WEIGHTS = ("ffn1_norm", "ffn1_wg", "ffn1_wu", "ffn1_wd", "mix_norm", "w_in", "gla_w_a2", "gla_b_a", "gla_gn_w",
           "rwkv_mu", "rwkv_w0", "rwkv_w_w2", "rwkv_a0", "rwkv_w_a2", "rwkv_w_g2", "rwkv_k_k", "rwkv_k_a", "rwkv_r_k",
           "rwkv_lnx_w", "rwkv_lnx_b", "gate_b", "w_branch", "w_out", "ffn2_norm", "ffn2_wg", "ffn2_wu", "ffn2_wd",
           "final_norm")


def _unshard_cols(g):
    return jnp.transpose(g, (1, 0, 2)).reshape(g.shape[1], NDEV * g.shape[2])


def _shard_cols(a):
    return jnp.transpose(a.reshape(a.shape[0], NDEV, a.shape[1] // NDEV), (1, 0, 2))


def _pad_rows(a, rows):
    return jnp.pad(a, ((0, rows - a.shape[0]), (0, 0)))


def _align_rw(a):
    c = 3 * RW_W
    z = jnp.zeros((a.shape[0], LORA_P - DECAY_LORA), a.dtype)
    return jnp.concatenate([a[:, :c], a[:, c:c + DECAY_LORA], z, a[:, c + DECAY_LORA:c + 2 * DECAY_LORA], z,
                            a[:, c + 2 * DECAY_LORA:]], axis=1)


def _unalign_rw(a):
    c = 3 * RW_W
    return jnp.concatenate([a[:, :c + DECAY_LORA], a[:, c + LORA_P:c + LORA_P + AAA_LORA], a[:, c + 2 * LORA_P:]], axis=1)


def _align_proj(a):
    gla = jnp.pad(a[:, :GLA_IN], ((0, 0), (0, PA_W - GLA_IN)))
    return jnp.concatenate([a[:, GLA_IN + RW_IN:], _align_rw(a[:, GLA_IN:GLA_IN + RW_IN]), gla], axis=1)


def _unalign_proj(a):
    return jnp.concatenate([a[:, PG_W + PR_W:PG_W + PR_W + GLA_IN], _unalign_rw(a[:, PG_W:PG_W + PR_W]), a[:, :PG_W]], axis=1)


def _layout_weights(gb, gs, rep):
    row = lambda n: rep[n].reshape(1, -1)
    return {
        "wg1": gb["ffn1_wg"],
        "g1": row("ffn1_norm"), "g2": row("mix_norm"), "g3": row("ffn2_norm"), "gf": row("final_norm"),
        "gla_w_a2": _pad_rows(_unshard_cols(gs["gla_w_a2"]), LORA_P), "gla_b_a": row("gla_b_a"),
        "gn": jnp.tile(row("gla_gn_w"), (1, GLA_HEADS)),
        "mu": _align_rw(row("rwkv_mu")), "w0": row("rwkv_w0"), "a0": row("rwkv_a0"),
        "w_w2": _pad_rows(_unshard_cols(gs["rwkv_w_w2"]), LORA_P),
        "w_a2": _pad_rows(_unshard_cols(gs["rwkv_w_a2"]), LORA_P),
        "w_g2": _unshard_cols(gs["rwkv_w_g2"]),
        "k_k": row("rwkv_k_k"), "k_a": row("rwkv_k_a"), "r_k": row("rwkv_r_k"),
        "lnx_w": row("rwkv_lnx_w"), "lnx_b": row("rwkv_lnx_b"), "gate_b": row("gate_b"),
    }


LATE_ROWS = (("ffn2_wd", FSH), ("w_branch", (GLA_V + RW_W) // NDEV), ("w_out", D // NDEV))


def _late_weights(g_up, g_down):
    r1, r2 = LATE_ROWS[0][1], LATE_ROWS[0][1] + LATE_ROWS[1][1]
    return {"wg2": g_up[:, 0], "wu2": g_up[:, 1], "wd2": g_down[:, :r1],
            "wb": g_down[:, r1:r2].reshape(GLA_V + RW_W, D), "wo": g_down[:, r2:].reshape(D, D)}


def _late_grad_parts(g):
    return [g["up2"], jnp.concatenate([g["wd2"], g["wb"].reshape(NDEV, -1, D), g["wo"].reshape(NDEV, -1, D)], axis=1)]


def _layout_grads(g):
    return {
        "ffn1_norm": g["g1"], "mix_norm": g["g2"], "ffn2_norm": g["g3"], "final_norm": g["gf"],
        "gla_w_a2": g["gla_w_a2"][:GLA_LORA], "gla_b_a": g["gla_b_a"],
        "gla_gn_w": jnp.sum(g["gn"].reshape(GLA_HEADS, GLA_DV), axis=0, keepdims=True),
        "rwkv_mu": _unalign_rw(g["mu"]), "rwkv_w0": g["w0"], "rwkv_a0": g["a0"],
        "rwkv_w_w2": g["w_w2"][:DECAY_LORA], "rwkv_w_a2": g["w_a2"][:AAA_LORA], "rwkv_w_g2": g["w_g2"],
        "rwkv_k_k": g["k_k"], "rwkv_k_a": g["k_a"], "rwkv_r_k": g["r_k"],
        "rwkv_lnx_w": g["lnx_w"], "rwkv_lnx_b": g["lnx_b"], "gate_b": g["gate_b"],
    }


_MESH = pl.DeviceIdType.MESH
_ANY = pl.BlockSpec(memory_space=pl.ANY)


def _position():
    return lax.axis_index("x"), lax.axis_index("y"), lax.axis_index("c")


def _slot(p):
    return 4 * p[0] + 2 * p[1] + p[2]


def _comm_sems(n):
    if not n:
        return []
    return [pltpu.SemaphoreType.DMA((7 * n,)), pltpu.SemaphoreType.DMA((7 * n,)), pltpu.SemaphoreType.DMA((n,))]


def _gather_plan(ins, outs, send_sems, recv_sems, local_sems):
    n = len(ins)
    x, y, c = _position()
    me, sibling = (x, y, c), (x, y, 1 - c)
    chips = [(1 - x, y), (x, 1 - y), (1 - x, 1 - y)]

    def copy(a, k, block, to, src=None):
        dst = outs[a].at[_slot(block)]
        return pltpu.make_async_remote_copy(
            src_ref=dst if src is None else src, dst_ref=dst, send_sem=send_sems.at[7 * a + k],
            recv_sem=recv_sems.at[7 * a + k], device_id=to, device_id_type=_MESH)

    def local(a):
        return pltpu.make_async_copy(ins[a], outs[a].at[_slot(me)], local_sems.at[a])

    def own(a):
        return [copy(a, 0, me, sibling, src=ins[a])] + [copy(a, 1 + j, me, (*chip, c), src=ins[a]) for j, chip in enumerate(chips)]

    def start():
        for a in range(n):
            local(a).start()
            for cp in own(a):
                cp.start()

    def forward():
        for a in range(n):
            for j, chip in enumerate(chips):
                copy(a, 1 + j, (*chip, c), me).wait_recv()
                copy(a, 4 + j, (*chip, c), sibling).start()

    def finish():
        for a in range(n):
            copy(a, 0, sibling, me).wait_recv()
            for j, chip in enumerate(chips):
                copy(a, 4 + j, (*chip, 1 - c), me).wait_recv()
        for a in range(n):
            for cp in own(a) + [copy(a, 4 + j, (*chip, c), sibling) for j, chip in enumerate(chips)]:
                cp.wait_send()
            local(a).wait()

    return start, forward, finish


def _exchange_plan(ins, outs, send_sems, recv_sems, local_sems):
    n = len(ins)
    x, y, c = _position()
    me = (x, y, c)
    flip = lambda v, f: 1 - v if f else v
    peers = [(flip(x, fx), flip(y, fy), flip(c, fc))
             for fx, fy, fc in ((0, 0, 1), (1, 0, 0), (0, 1, 0), (1, 1, 0), (1, 0, 1), (0, 1, 1), (1, 1, 1))]

    def copy(a, k, src_slot, dst_slot):
        return pltpu.make_async_remote_copy(
            src_ref=ins[a].at[src_slot], dst_ref=outs[a].at[dst_slot], send_sem=send_sems.at[7 * a + k],
            recv_sem=recv_sems.at[7 * a + k], device_id=peers[k], device_id_type=_MESH)

    def local(a):
        return pltpu.make_async_copy(ins[a].at[_slot(me)], outs[a].at[_slot(me)], local_sems.at[a])

    def start():
        for a in range(n):
            local(a).start()
            for k, peer in enumerate(peers):
                copy(a, k, _slot(peer), _slot(me)).start()

    def finish():
        for a in range(n):
            for k, peer in enumerate(peers):
                copy(a, k, _slot(peer), _slot(peer)).wait_recv()
        for a in range(n):
            for k, peer in enumerate(peers):
                copy(a, k, _slot(peer), _slot(me)).wait_send()
            local(a).wait()

    return start, finish


def _sibling_plan(ins, outs, send_sems, recv_sems, local_sems):
    x, y, c = _position()

    def copy(a):
        return pltpu.make_async_remote_copy(
            src_ref=ins[a].at[1 - c], dst_ref=outs[a], send_sem=send_sems.at[7 * a], recv_sem=recv_sems.at[7 * a],
            device_id=(x, y, 1 - c), device_id_type=_MESH)

    def start():
        for a in range(len(ins)):
            copy(a).start()

    def finish():
        for a in range(len(ins)):
            copy(a).wait()

    return start, finish


def _quad_plan(ins, outs, send_sems, recv_sems, local_sems):
    n = len(ins)
    x, y, c = _position()
    mine = 2 * x + y
    peers = [(1 - x, y), (x, 1 - y), (1 - x, 1 - y)]

    def copy(a, k, src_slot, dst_slot):
        return pltpu.make_async_remote_copy(
            src_ref=ins[a].at[src_slot], dst_ref=outs[a].at[dst_slot], send_sem=send_sems.at[7 * a + k],
            recv_sem=recv_sems.at[7 * a + k], device_id=(*peers[k], c), device_id_type=_MESH)

    def local(a):
        return pltpu.make_async_copy(ins[a].at[mine], outs[a].at[mine], local_sems.at[a])

    def start():
        for a in range(n):
            local(a).start()
            for k, (px, py) in enumerate(peers):
                copy(a, k, 2 * px + py, mine).start()

    def finish():
        for a in range(n):
            for k, (px, py) in enumerate(peers):
                copy(a, k, 2 * px + py, 2 * px + py).wait_recv()
        for a in range(n):
            for k, (px, py) in enumerate(peers):
                copy(a, k, 2 * px + py, mine).wait_send()
            local(a).wait()

    return start, finish


_PLANS = {"gather": (_gather_plan, lambda s: (NDEV,) + s), "gather_late": (_gather_plan, lambda s: (NDEV,) + s),
          "exchange": (_exchange_plan, lambda s: s),
          "sibling": (_sibling_plan, lambda s: s[1:]), "quad": (_quad_plan, lambda s: s)}


def _exchange_now(kind, arrays, name):
    n = len(arrays)

    def body(*refs):
        for stage in _PLANS[kind][0](refs[:n], refs[n:2 * n], *refs[2 * n:]):
            stage()

    return pl.pallas_call(
        body, name=name, in_specs=[_ANY] * n, out_specs=[_ANY] * n,
        out_shape=[jax.ShapeDtypeStruct(_PLANS[kind][1](a.shape), a.dtype) for a in arrays], scratch_shapes=_comm_sems(n),
    )(*arrays)


def _chip_sum(parts, name):
    (theirs,) = _exchange_now("sibling", [parts], "chip_send_" + name)
    shape = theirs.shape
    rows = shape[-2]
    for d in shape[1:-2]:
        rows *= d
    flat = (4, rows, shape[-1])
    tr = 512

    def body(both_ref, b_ref, o_ref):
        mine = jnp.where(lax.axis_index("c") == 0, both_ref[0], both_ref[1])
        o_ref[...] = (mine.astype(F32) + b_ref[...].astype(F32)).astype(BF16)

    blk = pl.BlockSpec((None, tr, shape[-1]), lambda s, r: (s, r, 0))
    out = pl.pallas_call(
        body, name="chip_sum_" + name, grid=(4, rows // tr),
        in_specs=[pl.BlockSpec((2, None, tr, shape[-1]), lambda s, r: (0, s, r, 0)), blk], out_specs=blk,
        out_shape=jax.ShapeDtypeStruct(flat, BF16), compiler_params=_params(("parallel", "parallel")),
    )(parts.reshape((2,) + flat), theirs.reshape(flat))
    return out.reshape(shape)


def _adamw_math(w, g, m, v):
    m = ADAM_B1 * m + (1.0 - ADAM_B1) * g
    v = ADAM_B2 * v + (1.0 - ADAM_B2) * (g * g)
    m_hat = m / (1.0 - ADAM_B1 ** ADAM_STEP)
    v_hat = v / (1.0 - ADAM_B2 ** ADAM_STEP)
    delta = -ADAM_LR * (m_hat / (jnp.sqrt(v_hat) + ADAM_EPS) + ADAM_WD * w)
    return delta, m, v


def _sum_slots(ref):
    total = ref[0].astype(F32)
    for s in range(1, ref.shape[0]):
        total = total + ref[s].astype(F32)
    return total


def _adamw(parts, w, m, v, tr, name, stack_index=None, row_block_offset=0, transposed=False):
    _, R, C = w.shape

    def body(p_ref, w_ref, m_ref, v_ref, g_ref, d_ref, nm_ref, nv_ref):
        g = _sum_slots(p_ref)
        g = g.T if transposed else g
        g_ref[...] = g
        d_ref[...], nm_ref[...], nv_ref[...] = _adamw_math(w_ref[...], g, m_ref[...], v_ref[...])

    if stack_index is None:
        p_spec = pl.BlockSpec((parts.shape[0], tr, C), lambda r: (0, row_block_offset + r, 0))
    else:
        p_spec = pl.BlockSpec((parts.shape[0], None, tr, C), lambda r: (0, stack_index, r, 0))
    if transposed:
        blk = pl.BlockSpec((None, C, tr), lambda r: (0, 0, r))
        out = jax.ShapeDtypeStruct((1, C, R), F32)
        w, m, v = (jnp.swapaxes(a, 1, 2) for a in (w, m, v))
    else:
        blk = pl.BlockSpec((None, tr, C), lambda r: (0, r, 0))
        out = jax.ShapeDtypeStruct((1, R, C), F32)
    res = pl.pallas_call(
        body, name=name, grid=(R // tr,), in_specs=[p_spec, blk, blk, blk], out_specs=[blk] * 4, out_shape=(out,) * 4,
        compiler_params=_params(("parallel",)),
    )(parts, w, m, v)
    return [jnp.swapaxes(a, 1, 2) for a in res] if transposed else res


def _sum_gathered(parts):
    _, R, C = parts.shape

    def body(p_ref, o_ref):
        o_ref[...] = _sum_slots(p_ref)

    return pl.pallas_call(body, name="small_grad_sum", out_shape=jax.ShapeDtypeStruct((R, C), F32),
                          compiler_params=_params())(parts)


def _adamw_small(w, g, m, v):
    def body(w_ref, g_ref, m_ref, v_ref, d_ref, nm_ref, nv_ref):
        d_ref[...], nm_ref[...], nv_ref[...] = _adamw_math(w_ref[...], g_ref[...], m_ref[...], v_ref[...])

    out = jax.ShapeDtypeStruct(w.shape, F32)
    return pl.pallas_call(body, name="adamw_small", out_shape=(out,) * 3, compiler_params=_params())(w, g, m, v)


def _pack(pieces, rows):
    flat = jnp.concatenate([p.reshape(-1) for p in pieces])
    return jnp.pad(flat, (0, rows * 128 - flat.shape[0])).reshape(rows, 128)


def _unpack(packed, shapes):
    flat = packed.reshape(-1)
    out, off = [], 0
    for s in shapes:
        size = 1
        for d in s:
            size *= d
        out.append(flat[off:off + size].reshape(s))
        off += size
    return out


def _rows_for(shapes, extra=0):
    total = extra
    for s in shapes:
        size = 1
        for d in s:
            size *= d
        total += size
    return -(-total // 1024) * 8


def kernel(x, ffn1_norm, ffn1_wg, ffn1_wu, ffn1_wd, mix_norm, w_in, gla_w_a2, gla_b_a, gla_gn_w, rwkv_mu, rwkv_w0, rwkv_w_w2, rwkv_a0, rwkv_w_a2, rwkv_w_g2, rwkv_k_k, rwkv_k_a, rwkv_r_k, rwkv_lnx_w, rwkv_lnx_b, gate_b, w_branch, w_out, ffn2_norm, ffn2_wg, ffn2_wu, ffn2_wd, final_norm, loss_target, m_ffn1_norm, m_ffn1_wg, m_ffn1_wu, m_ffn1_wd, m_mix_norm, m_w_in, m_gla_w_a2, m_gla_b_a, m_gla_gn_w, m_rwkv_mu, m_rwkv_w0, m_rwkv_w_w2, m_rwkv_a0, m_rwkv_w_a2, m_rwkv_w_g2, m_rwkv_k_k, m_rwkv_k_a, m_rwkv_r_k, m_rwkv_lnx_w, m_rwkv_lnx_b, m_gate_b, m_w_branch, m_w_out, m_ffn2_norm, m_ffn2_wg, m_ffn2_wu, m_ffn2_wd, m_final_norm, v_ffn1_norm, v_ffn1_wg, v_ffn1_wu, v_ffn1_wd, v_mix_norm, v_w_in, v_gla_w_a2, v_gla_b_a, v_gla_gn_w, v_rwkv_mu, v_rwkv_w0, v_rwkv_w_w2, v_rwkv_a0, v_rwkv_w_a2, v_rwkv_w_g2, v_rwkv_k_k, v_rwkv_k_a, v_rwkv_r_k, v_rwkv_lnx_w, v_rwkv_lnx_b, v_gate_b, v_w_branch, v_w_out, v_ffn2_norm, v_ffn2_wg, v_ffn2_wu, v_ffn2_wd, v_final_norm):
    wts = dict(zip(WEIGHTS, (ffn1_norm, ffn1_wg, ffn1_wu, ffn1_wd, mix_norm, w_in, gla_w_a2, gla_b_a, gla_gn_w, rwkv_mu, rwkv_w0, rwkv_w_w2, rwkv_a0, rwkv_w_a2, rwkv_w_g2, rwkv_k_k, rwkv_k_a, rwkv_r_k, rwkv_lnx_w, rwkv_lnx_b, gate_b, w_branch, w_out, ffn2_norm, ffn2_wg, ffn2_wu, ffn2_wd, final_norm)))
    mom = dict(zip(WEIGHTS, (m_ffn1_norm, m_ffn1_wg, m_ffn1_wu, m_ffn1_wd, m_mix_norm, m_w_in, m_gla_w_a2, m_gla_b_a, m_gla_gn_w, m_rwkv_mu, m_rwkv_w0, m_rwkv_w_w2, m_rwkv_a0, m_rwkv_w_a2, m_rwkv_w_g2, m_rwkv_k_k, m_rwkv_k_a, m_rwkv_r_k, m_rwkv_lnx_w, m_rwkv_lnx_b, m_gate_b, m_w_branch, m_w_out, m_ffn2_norm, m_ffn2_wg, m_ffn2_wu, m_ffn2_wd, m_final_norm)))
    var = dict(zip(WEIGHTS, (v_ffn1_norm, v_ffn1_wg, v_ffn1_wu, v_ffn1_wd, v_mix_norm, v_w_in, v_gla_w_a2, v_gla_b_a, v_gla_gn_w, v_rwkv_mu, v_rwkv_w0, v_rwkv_w_w2, v_rwkv_a0, v_rwkv_w_a2, v_rwkv_w_g2, v_rwkv_k_k, v_rwkv_k_a, v_rwkv_r_k, v_rwkv_lnx_w, v_rwkv_lnx_b, v_gate_b, v_w_branch, v_w_out, v_ffn2_norm, v_ffn2_wg, v_ffn2_wu, v_ffn2_wd, v_final_norm)))
    two = lambda a: a.reshape(a.shape[-2:])

    bf = lambda n: two(wts[n]).astype(BF16)
    lora = jnp.concatenate([jnp.pad(two(gla_w_a2), ((0, 0), (0, 128 - GLA_QK // NDEV)))] +
                           [two(wts[n]) for n in SMALL_SHARDED[1:]], axis=0)
    g_wg1, g_lora = _exchange_now("gather", [bf("ffn1_wg"), lora], "gather_weights")
    gb = {"ffn1_wg": g_wg1}
    gs = {"gla_w_a2": g_lora[:, :GLA_LORA, :GLA_QK // NDEV]}
    row = GLA_LORA
    for n in SMALL_SHARDED[1:]:
        gs[n] = g_lora[:, row:row + wts[n].shape[1]]
        row += wts[n].shape[1]
    w = _layout_weights(gb, gs, {n: wts[n] for n in REPLICATED})
    blocks = {"wu1": bf("ffn1_wu"), "wd1": bf("ffn1_wd"), "win": bf("w_in"),
              "late": [jnp.stack([bf("ffn2_wg"), bf("ffn2_wu")]), jnp.concatenate([bf(n) for n, _ in LATE_ROWS], axis=0)]}

    loss_part, grad_x, grads, parts = _local_step(x[0], loss_target[0], w, blocks)
    small = _layout_grads(grads)
    result = {}
    state = lambda n: (wts[n], mom[n], var[n])
    for group, names in (("up1", ("ffn1_wg", "ffn1_wu")), ("up2", ("ffn2_wg", "ffn2_wu"))):
        for i, n in enumerate(names):
            result[n] = _adamw(parts[group], *state(n), 256, "adamw_" + n, stack_index=i, transposed=True)
    result["ffn1_wd"] = _adamw(parts["wd1"], *state("ffn1_wd"), 64, "adamw_ffn1_wd")
    row = 0
    for n, rows in LATE_ROWS:
        result[n] = _adamw(parts["down2"], *state(n), 64, "adamw_" + n, row_block_offset=row // 64)
        row += rows
    result["w_in"] = _adamw(parts["win"], *state("w_in"), 256, "adamw_w_in")

    small_names = [n for n in WEIGHTS if n not in BIG]
    full_shapes = [small[n].shape for n in small_names]
    rows_full = _rows_for(full_shapes, extra=128)
    packed = _pack([small[n] for n in small_names] + [loss_part], rows_full)
    (gathered,) = _exchange_now("gather", [packed], "gather_small_grads")
    total = _sum_gathered(gathered)
    *full_grads, loss_row = _unpack(total, full_shapes + [(1, 128)])
    me = _slot(_position())
    own = {}
    for n, g in zip(small_names, full_grads):
        if n in SMALL_SHARDED:
            cols = wts[n].shape[-1]
            g = lax.dynamic_slice_in_dim(g, me * cols, cols, axis=1)
        own[n] = g.reshape(wts[n].shape)
    own_shapes = [wts[n].shape for n in small_names]
    rows_own = _rows_for(own_shapes)
    pk = lambda d: _pack([d[n] for n in small_names], rows_own)
    d_s, m_s, v_s = _adamw_small(pk(wts), pk(own), pk(mom), pk(var))
    for n, d, m, v in zip(small_names, _unpack(d_s, own_shapes), _unpack(m_s, own_shapes), _unpack(v_s, own_shapes)):
        result[n] = (own[n], d, m, v)

    shaped = lambda n, k: result[n][k].reshape(wts[n].shape)
    return (loss_row[0, 0], grad_x[None],
            *[shaped(n, 0) for n in WEIGHTS], *[shaped(n, 1) for n in WEIGHTS],
            *[shaped(n, 2) for n in WEIGHTS], *[shaped(n, 3) for n in WEIGHTS])
```

```python
import functools

import jax
import jax.numpy as jnp
from jax import lax
from jax.experimental import pallas as pl
from jax.experimental.pallas import tpu as pltpu

F32 = jnp.float32
BF16 = jnp.bfloat16
HI = lax.Precision.HIGHEST

NDEV = 8
D = 2048
DFF = 5632
FSH = DFF // NDEV
CHUNK = 64
GLA_HEADS, GLA_DK, GLA_DV = 4, 128, 256
GLA_QK, GLA_V, GLA_LORA, GLA_TAU = 512, 1024, 16, 16.0
RW_HEADS, RW_HD, RW_W = 16, 64, 1024
DECAY_LORA, AAA_LORA, GATE_LORA = 96, 96, 256
GN_EPS = 64e-5
NORM_EPS = 1e-6
GLA_IN = 2 * GLA_QK + 2 * GLA_V + GLA_LORA
RW_IN = 3 * RW_W + DECAY_LORA + AAA_LORA + GATE_LORA
D_IN = GLA_IN + RW_IN + 2 * D
DIN_SH = D_IN // NDEV
PG_W = 2 * D
PR_W = 3584
PA_W = 3584
PA_USED = 2 * GLA_QK + 2 * GLA_V + 128
DIN_P = PG_W + PR_W + PA_W
LORA_P = 128

ADAM_LR, ADAM_B1, ADAM_B2, ADAM_EPS, ADAM_WD, ADAM_STEP = 0.001, 0.9, 0.999, 1e-08, 0.01, 10

VMEM_LIMIT = 56 * 1024 * 1024
RW_TB = 32
RW_G = 16
RW_NP = 8


def _params(sem=None, vmem=VMEM_LIMIT):
    return pltpu.CompilerParams(dimension_semantics=sem, vmem_limit_bytes=vmem)


def _pair_mask():
    return lax.broadcasted_iota(jnp.int32, (RW_HD, 2 * RW_HD), 1) < RW_HD


def _pair_rowsum(p, mask):
    tot = jnp.sum(p, axis=1, keepdims=True)
    first = jnp.sum(jnp.where(mask, p, 0.0), axis=1, keepdims=True)
    return first, tot - first


def _split_transposed(x_ref, q, dst_ref, base):
    xt = x_ref[:, 128 * q:128 * (q + 1)].T
    for g in range(RW_TB // RW_G):
        dst_ref[base + g, :, 0:RW_G] = xt[:, g * RW_G:(g + 1) * RW_G]


def _pair_column(tile_ref, idx, i, mask):
    return jnp.where(mask, tile_ref[idx, 0:RW_HD, i:i + 1], tile_ref[idx, RW_HD:, i:i + 1])


def _rw_core_fwd(rw, w, k2, kk, b, gather=()):
    T = rw.shape[0]
    nb = T // RW_TB
    ng = RW_TB // RW_G
    NP = RW_NP
    nc = len(gather)
    npair = RW_HEADS // 2 // NP

    def body(r_ref, v_ref, w_ref, k_ref, kk_ref, b_ref, *rest):
        g_in, (y_ref, st_ref, sa_ref), g_out = rest[:nc], rest[nc:nc + 3], rest[nc + 3:2 * nc + 3]
        s_scr, vt_scr, yt_scr, rows_scr = rest[2 * nc + 3:2 * nc + 7]
        pair, blk_i = pl.program_id(0), pl.program_id(1)
        if nc:
            start, forward, finish = _gather_plan(g_in, g_out, *rest[2 * nc + 7:])
            pl.when((pair == 0) & (blk_i == 0))(start)
            pl.when((pair == 0) & (blk_i == nb // 2))(forward)

        @pl.when(pl.program_id(1) == 0)
        def _():
            s_scr[...] = jnp.zeros_like(s_scr)
            yt_scr[...] = jnp.zeros_like(yt_scr)

        mask = _pair_mask()
        for q in range(NP):
            _split_transposed(v_ref, q, vt_scr, q * ng)
        R_, W_, K_, KK_, B_ = range(5)
        for a, ref in enumerate((r_ref, w_ref, k_ref, kk_ref, b_ref)):
            for q in range(NP):
                rows_scr[a * NP + q] = ref[:, 128 * q:128 * (q + 1)]

        def group(g, states):
            states = list(states)
            for i in range(RW_G):
                t = g * RW_G + i
                row = lambda a, q: rows_scr[a * NP + q, pl.ds(t, 1), :]
                sums = [_pair_rowsum(states[q] * row(KK_, q), mask) for q in range(NP)]
                for q in range(NP):
                    sa = jnp.where(mask, *sums[q])
                    sa_ref[q, t] = sa
                    states[q] = (states[q] * row(W_, q) - sa * row(B_, q)
                                 + _pair_column(vt_scr, q * ng + g, i, mask) * row(K_, q))
                    st_ref[q, t] = states[q]
                outs = [_pair_rowsum(states[q] * row(R_, q), mask) for q in range(NP)]
                for q in range(NP):
                    yt_scr[q * ng + g, 0:RW_HD, i:i + 1] = outs[q][0]
                    yt_scr[q * ng + g, RW_HD:, i:i + 1] = outs[q][1]
            return tuple(states)

        states = lax.fori_loop(0, ng, group, tuple(s_scr[q] for q in range(NP)))
        for q in range(NP):
            s_scr[q] = states[q]
            for g in range(ng):
                y_ref[g * RW_G:(g + 1) * RW_G, 128 * q:128 * (q + 1)] = yt_scr[q * ng + g].T[0:RW_G, :]
        if nc:
            pl.when((pair == npair - 1) & (blk_i == nb - 1))(finish)

    blk = lambda cb: pl.BlockSpec((RW_TB, 128 * NP), lambda p, i, cb=cb: (i, cb + p))
    tiles = pltpu.VMEM((NP * ng, 128, 128), F32)
    return pl.pallas_call(
        body, name="rw_core_fwd", grid=(npair, nb),
        in_specs=[blk(0), blk(2 * RW_W // (128 * NP)), blk(0), blk(0), blk(0), blk(0)] + [_ANY] * nc,
        out_specs=[blk(0)] + [pl.BlockSpec((NP, RW_TB, RW_HD, 128), lambda p, i: (p, i, 0, 0))] * 2 + [_ANY] * nc,
        out_shape=[jax.ShapeDtypeStruct((T, RW_W), F32)] + [jax.ShapeDtypeStruct((RW_HEADS // 2, T, RW_HD, 128), F32)] * 2
        + [jax.ShapeDtypeStruct((NDEV,) + a.shape, a.dtype) for a in gather],
        scratch_shapes=[pltpu.VMEM((NP, RW_HD, 128), F32), tiles, tiles, pltpu.VMEM((5 * NP, RW_TB, 128), F32)]
        + _comm_sems(nc),
        compiler_params=_params(("arbitrary", "arbitrary")),
    )(rw, rw, w, k2, kk, b, *gather)


def _rw_core_bwd(rw, w, k2, kk, b, states, sa_tiles, dy, exchange=()):
    T = rw.shape[0]
    nb = T // RW_TB
    ng = RW_TB // RW_G
    NP = RW_NP
    nc = len(exchange)
    npair = RW_HEADS // 2 // NP

    def body(r_ref, v_ref, w_ref, k_ref, kk_ref, b_ref, dy_ref, st_ref, sp_ref, sa_ref, *rest):
        e_in, e_out = rest[:nc], rest[nc + 6:2 * nc + 6]
        dr_ref, dw_ref, dk_ref, dv_ref, dkk_ref, db_ref = rest[nc:nc + 6]
        ds_scr, vt_scr, dyt_scr, dvt_scr, rows_scr, out_scr = rest[2 * nc + 6:2 * nc + 12]
        step = pl.program_id(1)
        if nc:
            start, finish = _exchange_plan(e_in, e_out, *rest[2 * nc + 12:])
            pl.when((pl.program_id(0) == 0) & (step == 0))(start)

        @pl.when(step == 0)
        def _():
            ds_scr[...] = jnp.zeros_like(ds_scr)
            dvt_scr[...] = jnp.zeros_like(dvt_scr)

        mask = _pair_mask()
        for q in range(NP):
            _split_transposed(v_ref, q, vt_scr, q * ng)
            _split_transposed(dy_ref, q, dyt_scr, q * ng)
        R_, W_, K_, KK_, B_ = range(5)
        for a, ref in enumerate((r_ref, w_ref, k_ref, kk_ref, b_ref)):
            for q in range(NP):
                rows_scr[a * NP + q] = ref[:, 128 * q:128 * (q + 1)]

        def group(gg, grads):
            g = ng - 1 - gg
            grads = list(grads)
            pairs = range(NP)
            for i in reversed(range(RW_G)):
                t = g * RW_G + i
                row = lambda a, q: rows_scr[a * NP + q, pl.ds(t, 1), :]

                def put(a, q, value):
                    out_scr[a * NP + q, pl.ds(t, 1), :] = value

                s_old = [st_ref[q, jnp.maximum(t - 1, 0)] for q in pairs]
                if i == 0:
                    s_old = [jnp.where(g == 0, jnp.where(step == nb - 1, 0.0, sp_ref[q, 0]), s_old[q]) for q in pairs]
                dycol = [_pair_column(dyt_scr, q * ng + g, i, mask) for q in pairs]
                dS = [grads[q] + dycol[q] * row(R_, q) for q in pairs]
                m = [_pair_rowsum(dS[q] * row(B_, q), mask) for q in pairs]
                dv = [_pair_rowsum(dS[q] * row(K_, q), mask) for q in pairs]
                for q in pairs:
                    put(R_, q, jnp.sum(st_ref[q, t] * dycol[q], axis=0, keepdims=True))
                    put(W_, q, jnp.sum(dS[q] * s_old[q], axis=0, keepdims=True))
                    put(K_, q, jnp.sum(dS[q] * _pair_column(vt_scr, q * ng + g, i, mask), axis=0, keepdims=True))
                for q in pairs:
                    dsa = -jnp.where(mask, *m[q])
                    grads[q] = dS[q] * row(W_, q) + dsa * row(KK_, q)
                    put(KK_, q, jnp.sum(s_old[q] * dsa, axis=0, keepdims=True))
                    put(B_, q, -jnp.sum(dS[q] * sa_ref[q, t], axis=0, keepdims=True))
                    dvt_scr[q * ng + g, 0:RW_HD, i:i + 1] = dv[q][0]
                    dvt_scr[q * ng + g, RW_HD:, i:i + 1] = dv[q][1]
            return tuple(grads)

        grads = lax.fori_loop(0, ng, group, tuple(ds_scr[q] for q in range(NP)))
        for q in range(NP):
            ds_scr[q] = grads[q]
            for a, ref in enumerate((dr_ref, dw_ref, dk_ref, dkk_ref, db_ref)):
                ref[:, 128 * q:128 * (q + 1)] = out_scr[a * NP + q]
            for g in range(ng):
                dv_ref[g * RW_G:(g + 1) * RW_G, 128 * q:128 * (q + 1)] = dvt_scr[q * ng + g].T[0:RW_G, :]
        if nc:
            pl.when((pl.program_id(0) == npair - 1) & (step == nb - 1))(finish)

    blk = lambda cb: pl.BlockSpec((RW_TB, 128 * NP), lambda p, i, cb=cb: (nb - 1 - i, cb + p))
    st_spec = pl.BlockSpec((NP, RW_TB, RW_HD, 128), lambda p, i: (p, nb - 1 - i, 0, 0))
    sp_spec = pl.BlockSpec((NP, 1, RW_HD, 128), lambda p, i: (p, jnp.maximum((nb - 1 - i) * RW_TB - 1, 0), 0, 0))
    out = jax.ShapeDtypeStruct((T, RW_W), F32)
    tiles = pltpu.VMEM((NP * ng, 128, 128), F32)
    return pl.pallas_call(
        body, name="rw_core_bwd", grid=(npair, nb),
        in_specs=[blk(0), blk(2 * RW_W // (128 * NP)), blk(0), blk(0), blk(0), blk(0), blk(0), st_spec, sp_spec, st_spec]
        + [_ANY] * nc,
        out_specs=[blk(0)] * 6 + [_ANY] * nc,
        out_shape=[out] * 6 + [jax.ShapeDtypeStruct(a.shape, a.dtype) for a in exchange],
        scratch_shapes=[pltpu.VMEM((NP, RW_HD, 128), F32), tiles, tiles, tiles,
                        pltpu.VMEM((5 * NP, RW_TB, 128), F32), pltpu.VMEM((5 * NP, RW_TB, 128), F32)] + _comm_sems(nc),
        compiler_params=_params(("arbitrary", "arbitrary")),
    )(rw, rw, w, k2, kk, b, dy, states, states, sa_tiles, *exchange)


GLA_CB = 8


def _gla_chunk(s_t, q, k, v, la, ltri):
    cum = jnp.dot(ltri, la, precision=HI, preferred_element_type=F32)
    total = jnp.sum(la, axis=0, keepdims=True)
    kdec = k * jnp.exp(total - cum)
    u_t = _bdot(v, kdec, _TN)
    s_t = jnp.exp(total) * s_t + u_t
    o = _bdot(q * (GLA_DK ** -0.5), s_t, _NT)
    return s_t, o


def _gla_core_fwd(pa, la, ltri):
    T = pa.shape[0]
    cb = min(GLA_CB, T // CHUNK)
    rows = cb * CHUNK
    nsteps = T // rows

    def body(q_ref, k_ref, v_ref, la_ref, ltri_ref, o_ref, st_ref, s_scr):
        @pl.when(pl.program_id(0) == 0)
        def _():
            s_scr[...] = jnp.zeros_like(s_scr)

        def chunk(c, states):
            sl = pl.ds(pl.multiple_of(c * CHUNK, CHUNK), CHUNK)
            out = []
            for h in range(GLA_HEADS):
                qk, vv = slice(GLA_DK * h, GLA_DK * (h + 1)), slice(GLA_DV * h, GLA_DV * (h + 1))
                s_t, o = _gla_chunk(states[h], q_ref[sl, qk], k_ref[sl, qk], v_ref[sl, vv], la_ref[sl, qk], ltri_ref[...])
                o_ref[sl, vv] = o
                st_ref[h, c] = s_t
                out.append(s_t)
            return tuple(out)

        states = lax.fori_loop(0, cb, chunk, tuple(s_scr[h] for h in range(GLA_HEADS)))
        for h in range(GLA_HEADS):
            s_scr[h] = states[h]

    qk = lambda cb_: pl.BlockSpec((rows, GLA_QK), lambda i, cb_=cb_: (i, cb_))
    return pl.pallas_call(
        body, name="gla_core_fwd", grid=(nsteps,),
        in_specs=[qk(0), qk(1), pl.BlockSpec((rows, GLA_V), lambda i: (i, 1)), qk(0),
                  pl.BlockSpec((CHUNK, CHUNK), lambda i: (0, 0))],
        out_specs=[pl.BlockSpec((rows, GLA_V), lambda i: (i, 0)),
                   pl.BlockSpec((GLA_HEADS, cb, GLA_DV, GLA_DK), lambda i: (0, i, 0, 0))],
        out_shape=(jax.ShapeDtypeStruct((T, GLA_V), F32),
                   jax.ShapeDtypeStruct((GLA_HEADS, T // CHUNK, GLA_DV, GLA_DK), F32)),
        scratch_shapes=[pltpu.VMEM((GLA_HEADS, GLA_DV, GLA_DK), F32)],
        compiler_params=_params(("arbitrary",)),
    )(pa, pa, pa, la, ltri)


def _gla_core_bwd(pa, la, ltri, states, do):
    T = pa.shape[0]
    cb = min(GLA_CB, T // CHUNK)
    rows = cb * CHUNK
    nsteps = T // rows

    def body(q_ref, k_ref, v_ref, la_ref, ltri_ref, st_ref, sp_ref, do_ref,
             dq_ref, dk_ref, dv_ref, dla_ref, ds_scr):
        step = pl.program_id(0)

        @pl.when(step == 0)
        def _():
            ds_scr[...] = jnp.zeros_like(ds_scr)

        def chunk(cc, grads):
            c = cb - 1 - cc
            sl = pl.ds(pl.multiple_of(c * CHUNK, CHUNK), CHUNK)
            out = []
            for h in range(GLA_HEADS):
                qk, vv = slice(GLA_DK * h, GLA_DK * (h + 1)), slice(GLA_DV * h, GLA_DV * (h + 1))
                s_before = jnp.where(step == nsteps - 1, 0.0, sp_ref[h, 0])
                s_prev = jnp.where(c == 0, s_before, st_ref[h, jnp.maximum(c - 1, 0)])
                _, vjp = jax.vjp(functools.partial(_gla_chunk, ltri=ltri_ref[...]),
                                 s_prev, q_ref[sl, qk], k_ref[sl, qk], v_ref[sl, vv], la_ref[sl, qk])
                ds_prev, dq, dk, dv, dla = vjp((grads[h], do_ref[sl, vv]))
                dq_ref[sl, qk] = dq
                dk_ref[sl, qk] = dk
                dv_ref[sl, vv] = dv
                dla_ref[sl, qk] = dla
                out.append(ds_prev)
            return tuple(out)

        grads = lax.fori_loop(0, cb, chunk, tuple(ds_scr[h] for h in range(GLA_HEADS)))
        for h in range(GLA_HEADS):
            ds_scr[h] = grads[h]

    r = lambda i: nsteps - 1 - i
    qk = lambda cb_: pl.BlockSpec((rows, GLA_QK), lambda i, cb_=cb_: (r(i), cb_))
    o512 = pl.BlockSpec((rows, GLA_QK), lambda i: (r(i), 0))
    o1024 = pl.BlockSpec((rows, GLA_V), lambda i: (r(i), 0))
    return pl.pallas_call(
        body, name="gla_core_bwd", grid=(nsteps,),
        in_specs=[qk(0), qk(1), pl.BlockSpec((rows, GLA_V), lambda i: (r(i), 1)), qk(0),
                  pl.BlockSpec((CHUNK, CHUNK), lambda i: (0, 0)),
                  pl.BlockSpec((GLA_HEADS, cb, GLA_DV, GLA_DK), lambda i: (0, r(i), 0, 0)),
                  pl.BlockSpec((GLA_HEADS, 1, GLA_DV, GLA_DK), lambda i: (0, jnp.maximum(r(i) * cb - 1, 0), 0, 0)),
                  o1024],
        out_specs=[o512, o512, o1024, o512],
        out_shape=(jax.ShapeDtypeStruct((T, GLA_QK), F32), jax.ShapeDtypeStruct((T, GLA_QK), F32),
                   jax.ShapeDtypeStruct((T, GLA_V), F32), jax.ShapeDtypeStruct((T, GLA_QK), F32)),
        scratch_shapes=[pltpu.VMEM((GLA_HEADS, GLA_DV, GLA_DK), F32)],
        compiler_params=_params(("arbitrary",)),
    )(pa, pa, pa, la, ltri, states, states, do)


def _rowwise(fn, name, T, tm, rows, pars, row_outs, acc_outs):
    nr, npar, nro = len(rows), len(pars), len(row_outs)
    tm = min(tm, T)
    nsteps = T // tm

    def body(*refs):
        i = pl.program_id(0)
        ins = [r[...] for r in refs[:nr + npar]]
        outs, accs = fn(i, *ins)
        for r, o in zip(refs[nr + npar:nr + npar + nro], outs):
            r[...] = o.astype(r.dtype)
        for r, a in zip(refs[nr + npar + nro:], accs):
            @pl.when(i == 0)
            def _(r=r, a=a):
                r[...] = a

            @pl.when(i > 0)
            def _(r=r, a=a):
                r[...] += a

    def rspec(width, cb, kind):
        if kind == "cur":
            return pl.BlockSpec((tm, width), lambda i: (i, cb))
        if kind == "prev":
            return pl.BlockSpec((8, width), lambda i: (jnp.maximum(i * (tm // 8) - 1, 0), cb))
        return pl.BlockSpec((8, width), lambda i: (jnp.minimum((i + 1) * (tm // 8), T // 8 - 1), cb))

    in_specs = [rspec(w, cb, kind) for (_, w, cb, kind) in rows]
    in_specs += [pl.BlockSpec(p.shape, lambda i, nd=p.ndim: (0,) * nd) for p in pars]
    out_specs = [pl.BlockSpec((tm, w), lambda i: (i, 0)) for (w, _) in row_outs]
    out_specs += [pl.BlockSpec(s, lambda i, nd=len(s): (0,) * nd) for s in acc_outs]
    out_shape = [jax.ShapeDtypeStruct((T, w), dt) for (w, dt) in row_outs]
    out_shape += [jax.ShapeDtypeStruct(s, F32) for s in acc_outs]
    res = pl.pallas_call(
        body, name=name, grid=(nsteps,), in_specs=in_specs, out_specs=out_specs, out_shape=out_shape,
        compiler_params=_params(("arbitrary",)),
    )(*[r[0] for r in rows], *pars)
    return res


def _cur(a, width=None, cb=0):
    return (a, a.shape[1] if width is None else width, cb, "cur")


def _sigmoid(x):
    return 1.0 / (1.0 + jnp.exp(-x))


def _silu(x):
    return x * _sigmoid(x)


def _softplus(x):
    return jnp.maximum(x, 0.0) + jnp.log(1.0 + jnp.exp(-jnp.abs(x)))


def _rms(x, g):
    return x * lax.rsqrt(jnp.mean(x * x, axis=-1, keepdims=True) + NORM_EPS) * g


def _dot_hi(a, b):
    return jnp.dot(a, b, precision=lax.Precision.HIGH, preferred_element_type=F32)


def _rms_fwd(x, g, name):
    T = x.shape[0]
    fn = lambda i, xb, gb: ((_rms(xb, gb),), ())
    return _rowwise(fn, name, T, 256, [_cur(x)], [g], [(D, BF16)], [])[0]


def _rms_bwd(x, g, dh, dres, name):
    T = x.shape[0]

    def fn(i, xb, dhb, drb, gb):
        _, vjp = jax.vjp(_rms, xb, gb)
        dx, dg = vjp(dhb)
        return (drb + dx,), (dg,)

    return _rowwise(fn, name, T, 256, [_cur(x), _cur(dh), _cur(dres)], [g], [(D, F32)], [(1, D)])


def _loss_bwd(x, target, g):
    T = x.shape[0]

    def loss(xb, gb, tb):
        err = _rms(xb, gb) - tb
        return 0.5 * jnp.sum(jnp.mean(err * err, axis=-1, keepdims=True))

    def fn(i, xb, tb, gb):
        val, (dx, dg) = jax.value_and_grad(loss, argnums=(0, 1))(xb, gb, tb)
        return (dx,), (jnp.full((1, 128), val, F32), dg)

    return _rowwise(fn, "loss_bwd", T, 256, [_cur(x), _cur(target)], [g], [(D, F32)], [(1, 128), (1, D)])


def _gla_la(a_down, w_a2, b_a):
    return -_softplus(-(_bdot(a_down, w_a2, _NN) + b_a)) * (1.0 / GLA_TAU)


def _gla_prep(pa, w_a2, b_a):
    T = pa.shape[0]
    fn = lambda i, ab, wb, bb: ((_gla_la(ab, wb, bb),), ())
    return _rowwise(fn, "gla_prep", T, 512, [_cur(pa, LORA_P, (2 * GLA_QK + 2 * GLA_V) // LORA_P)], [w_a2, b_a],
                    [(GLA_QK, F32)], [])[0]


def _gla_prep_bwd(pa, w_a2, b_a, dla):
    T = pa.shape[0]

    def fn(i, ab, dlab, wb, bb):
        _, vjp = jax.vjp(_gla_la, ab, wb, bb)
        da, dw, db = vjp(dlab)
        return (da,), (dw, db)

    return _rowwise(fn, "gla_prep_bwd", T, 512, [_cur(pa, LORA_P, (2 * GLA_QK + 2 * GLA_V) // LORA_P), _cur(dla)],
                    [w_a2, b_a], [(LORA_P, BF16)], [(LORA_P, GLA_QK), (1, GLA_QK)])


def _gla_out(o, r, gn, ind, ind_t):
    ms = _dot_hi(_dot_hi(o * o, ind) * (1.0 / GLA_DV), ind_t)
    return o * lax.rsqrt(ms + NORM_EPS) * gn * _silu(r)


def _gla_post(o_raw, pa, gn, ind, ind_t):
    T = pa.shape[0]
    fn = lambda i, ob, rb, gb, a, b: ((_gla_out(ob, rb, gb, a, b),), ())
    return _rowwise(fn, "gla_post", T, 256, [_cur(o_raw), _cur(pa, GLA_V, 2)], [gn, ind, ind_t], [(GLA_V, BF16)], [])[0]


def _gla_post_bwd(o_raw, pa, gn, ind, ind_t, do):
    T = pa.shape[0]

    def fn(i, ob, rb, dob, gb, a, b):
        _, vjp = jax.vjp(lambda o, r, g: _gla_out(o, r, g, a, b), ob, rb, gb)
        d_o, d_r, d_g = vjp(dob)
        return (d_o, d_r), (d_g,)

    return _rowwise(fn, "gla_post_bwd", T, 256, [_cur(o_raw), _cur(pa, GLA_V, 2), _cur(do)], [gn, ind, ind_t],
                    [(GLA_V, F32), (GLA_V, BF16)], [(1, GLA_V)])


def _shift_rows(cur, prev8, i):
    first = jnp.where(i == 0, 0.0, prev8[7:8, :])
    rolled = pltpu.roll(cur, 1, 0)
    return jnp.where(lax.broadcasted_iota(jnp.int32, cur.shape, 0) == 0, first, rolled)


def _rw_gates(rw, w0, w_w2, a0, w_a2, w_g2, k_k, k_a, ind, ind_t):
    rk = rw[:, RW_W:2 * RW_W]
    wd = rw[:, 3 * RW_W:3 * RW_W + LORA_P]
    ad = rw[:, 3 * RW_W + LORA_P:3 * RW_W + 2 * LORA_P]
    gd = rw[:, 3 * RW_W + 2 * LORA_P:]
    w_raw = w0 + _bdot(jnp.tanh(wd), w_w2, _NN)
    w = jnp.exp(-jnp.exp(-_softplus(-w_raw) - 0.5))
    a = _sigmoid(a0 + _bdot(ad, w_a2, _NN))
    g = _bdot(_sigmoid(gd), w_g2, _NN)
    kk = rk * k_k
    kk = kk * _dot_hi(lax.rsqrt(jnp.maximum(_dot_hi(kk * kk, ind), 1e-24)), ind_t)
    k2 = rk * (1.0 + (a - 1.0) * k_a)
    return w, k2, kk, kk * a, g


def _rw_prep(pr, mu, gate_pars):
    T = pr.shape[0]

    def fn(i, cur, prev8, mub, *gp):
        rw = cur + mub * (_shift_rows(cur, prev8, i) - cur)
        return (rw,) + _rw_gates(rw, *gp), ()

    return _rowwise(fn, "rw_prep", T, 256, [_cur(pr), (pr, PR_W, 0, "prev")], [mu, *gate_pars],
                    [(PR_W, F32)] + [(RW_W, F32)] * 5, [])


def _rw_prep_bwd(pr, mu, gate_pars, d_r, d_v, d_w, d_k2, d_kk, d_b, d_g):
    T = pr.shape[0]
    rows = [_cur(pr), (pr, PR_W, 0, "prev")] + [_cur(x) for x in (*d_r, *d_v, d_w, *d_k2, d_kk, d_b, d_g)]
    acc = [(1, PR_W)] + [tuple(p.shape) for p in gate_pars[:-2]]

    def fn(i, cur, prev8, dr1, dr2, dv1, dv2, dw, dk1, dk2, dkk, db, dg, mub, *gp):
        sh = _shift_rows(cur, prev8, i)
        rw = cur + mub * (sh - cur)
        _, vjp = jax.vjp(lambda x, *p: _rw_gates(x, *p, gp[-2], gp[-1]), rw, *gp[:-2])
        grads = vjp((dw, dk1 + dk2, dkk, db, dg))
        zeros = jnp.zeros((cur.shape[0], PR_W - 3 * RW_W), F32)
        drw = grads[0] + jnp.concatenate([dr1 + dr2, jnp.zeros_like(dr1), dv1 + dv2, zeros], axis=1)
        dmu = jnp.sum(drw * (sh - cur), axis=0, keepdims=True)
        return (drw,), (dmu, *grads[1:])

    return _rowwise(fn, "rw_prep_bwd", T, 128, rows, [mu, *gate_pars], [(PR_W, F32)], acc)


def _shift_bwd(drw, mu):
    T = drw.shape[0]
    tm = min(256, T)

    def fn(i, cur, next8, mub):
        last = jnp.where(i == T // tm - 1, 0.0, next8[0:1, :])
        rolled = pltpu.roll(cur, cur.shape[0] - 1, 0)
        nxt = jnp.where(lax.broadcasted_iota(jnp.int32, cur.shape, 0) == cur.shape[0] - 1, last, rolled)
        return ((1.0 - mub) * cur + mub * nxt,), ()

    return _rowwise(fn, "shift_bwd", T, tm, [_cur(drw), (drw, PR_W, 0, "next")], [mu], [(PR_W, BF16)], [])[0]


def _rw_out(y, r, v, k2, g, lnx_w, lnx_b, r_k, ind, ind_t):
    mean = _dot_hi(_dot_hi(y, ind) * (1.0 / RW_HD), ind_t)
    yc = y - mean
    var = _dot_hi(_dot_hi(yc * yc, ind) * (1.0 / RW_HD), ind_t)
    yn = yc * lax.rsqrt(var + GN_EPS) * lnx_w + lnx_b
    bonus = _dot_hi(_dot_hi(r * k2 * r_k, ind), ind_t) * v
    return (yn + bonus) * g


def _rw_post(y, rw, k2, g, pars):
    T = y.shape[0]
    fn = lambda i, *a: ((_rw_out(*a),), ())
    return _rowwise(fn, "rw_post", T, 256, [_cur(y), _cur(rw, RW_W, 0), _cur(rw, RW_W, 2), _cur(k2), _cur(g)], pars,
                    [(RW_W, BF16)], [])[0]


def _rw_post_bwd(y, rw, k2, g, pars, do):
    T = y.shape[0]

    def fn(i, yb, rb, vb, kb, gb, dob, lw, lb, rk, ind, ind_t):
        _, vjp = jax.vjp(lambda *a: _rw_out(*a, ind, ind_t), yb, rb, vb, kb, gb, lw, lb, rk)
        gr = vjp(dob)
        return gr[:5], gr[5:]

    return _rowwise(fn, "rw_post_bwd", T, 256,
                    [_cur(y), _cur(rw, RW_W, 0), _cur(rw, RW_W, 2), _cur(k2), _cur(g), _cur(do)], pars,
                    [(RW_W, F32)] * 5, [(1, RW_W)] * 3)


def _merge_bwd(dm, y_gla, y_rw, pg, gate_b):
    T = dm.shape[0]

    def fn(i, dmb, ya, yr, p1, p2, gb):
        g1 = _sigmoid(p1 + gb[:, :D])
        g2 = _sigmoid(p2 + gb[:, D:])
        dp1 = dmb * ya * g1 * (1.0 - g1)
        dp2 = dmb * yr * g2 * (1.0 - g2)
        dp = jnp.concatenate([dp1, dp2], axis=1)
        return (dmb * g1, dmb * g2, dp), (jnp.sum(dp, axis=0, keepdims=True),)

    return _rowwise(fn, "merge_bwd", T, 256, [_cur(dm), _cur(y_gla), _cur(y_rw), _cur(pg, D, 0), _cur(pg, D, 1)],
                    [gate_b], [(D, BF16), (D, BF16), (PG_W, BF16)], [(1, PG_W)])


_NN = (((1,), (0,)), ((), ()))
_NT = (((1,), (1,)), ((), ()))
_TN = (((0,), (0,)), ((), ()))


def _bdot(a, b, dims):
    return lax.dot_general(a.astype(BF16), b.astype(BF16), dims, preferred_element_type=F32)


def _accumulate(k, nk, acc, part, finish):
    if nk == 1:
        finish(part)
        return

    @pl.when(k == 0)
    def _():
        acc[...] = part

    @pl.when(k > 0)
    def _():
        acc[...] += part

    @pl.when(k == nk - 1)
    def _():
        finish(acc[...])


def _call(body, comm, name, grid, in_specs, out_specs, out_shape, scratch_shapes, sem, operands):
    if comm is None:
        return pl.pallas_call(body, name=name, grid=grid, in_specs=in_specs, out_specs=out_specs, out_shape=out_shape,
                              scratch_shapes=scratch_shapes, compiler_params=_params(sem))(*operands)
    kind, arrays = comm
    nc, n_in, n_out, n_scr = len(arrays), len(in_specs), len(out_shape), len(scratch_shapes)
    total = 1
    for n in grid:
        total *= n

    def with_comm(*refs):
        own = refs[:n_in] + refs[n_in + nc:n_in + nc + n_out] + refs[n_in + 2 * nc + n_out:n_in + 2 * nc + n_out + n_scr]
        c_in, c_out, sems = refs[n_in:n_in + nc], refs[n_in + nc + n_out:n_in + 2 * nc + n_out], refs[-3:]
        step = 0
        for axis, n in enumerate(grid):
            step = step * n + pl.program_id(axis)
        start, *forward, finish = _PLANS[kind][0](c_in, c_out, *sems)
        pl.when(step == 0)(start)
        for stage in forward:
            pl.when(step == (total - 1 if kind == "gather_late" else total // 2))(stage)
        body(*own)
        pl.when(step == total - 1)(finish)

    return pl.pallas_call(
        with_comm, name=name, grid=grid, in_specs=list(in_specs) + [_ANY] * nc, out_specs=list(out_specs) + [_ANY] * nc,
        out_shape=list(out_shape) + [jax.ShapeDtypeStruct(_PLANS[kind][1](a.shape), a.dtype) for a in arrays],
        scratch_shapes=list(scratch_shapes) + _comm_sems(nc), compiler_params=_params(("arbitrary",) * len(grid)),
    )(*operands, *arrays)


def _matmul(a, b, mode, M, N, K, tm, tn, tk, name, a_off=(0, 0), b_off=(0, 0), res=None, scale=1.0, out_dtype=F32,
            comm=None):
    tm, tn, tk = min(tm, M), min(tn, N), min(tk, K)
    nk = K // tk
    if mode == "nn":
        a_spec = pl.BlockSpec((tm, tk), lambda i, j, k: (i + a_off[0], k + a_off[1]))
        b_spec = pl.BlockSpec((tk, tn), lambda i, j, k: (k + b_off[0], j + b_off[1]))
        dims = _NN
    elif mode == "nt":
        a_spec = pl.BlockSpec((tm, tk), lambda i, j, k: (i + a_off[0], k + a_off[1]))
        b_spec = pl.BlockSpec((tn, tk), lambda i, j, k: (j + b_off[0], k + b_off[1]))
        dims = _NT
    else:
        a_spec = pl.BlockSpec((tk, tm), lambda i, j, k: (k + a_off[0], i + a_off[1]))
        b_spec = pl.BlockSpec((tk, tn), lambda i, j, k: (k + b_off[0], j + b_off[1]))
        dims = _TN
    o_spec = pl.BlockSpec((tm, tn), lambda i, j, k: (i, j))

    def body(a_ref, b_ref, *rest):
        r_ref = rest[0] if res is not None else None
        o_ref = rest[1] if res is not None else rest[0]
        acc = rest[-1] if nk > 1 else None

        def finish(total):
            total = total * scale if scale != 1.0 else total
            if r_ref is not None:
                total = r_ref[...] + total
            o_ref[...] = total.astype(out_dtype)

        _accumulate(pl.program_id(2), nk, acc, _bdot(a_ref[...], b_ref[...], dims), finish)

    out = _call(body, comm, name, (M // tm, N // tn, nk), [a_spec, b_spec] + ([o_spec] if res is not None else []),
                [o_spec], [jax.ShapeDtypeStruct((M, N), out_dtype)], [pltpu.VMEM((tm, tn), F32)] if nk > 1 else [],
                ("parallel", "parallel", "arbitrary"), [a, b] + ([res] if res is not None else []))
    return out[0] if comm is None else out


def _ffn_up(h, wg, wu, name, comm=None):
    T = h.shape[0]
    tm = min(1024, T)

    def body(h_ref, wg_ref, wu_ref, a_ref, u_ref, s_ref):
        hb = h_ref[...]
        a = _bdot(hb, wg_ref[...], _NN)
        u = _bdot(hb, wu_ref[...], _NN)
        a_ref[...] = a
        u_ref[...] = u
        s_ref[...] = (_silu(a) * u).astype(BF16)

    w_spec = pl.BlockSpec((None, D, FSH), lambda i, j: (j, 0, 0))
    o_spec = pl.BlockSpec((None, tm, FSH), lambda i, j: (j, i, 0))
    sh = lambda dt: jax.ShapeDtypeStruct((NDEV, T, FSH), dt)
    return _call(body, comm, name, (T // tm, NDEV), [pl.BlockSpec((tm, D), lambda i, j: (i, 0)), w_spec, w_spec],
                 [o_spec] * 3, [sh(F32), sh(F32), sh(BF16)], [], ("parallel", "arbitrary"), [h, wg, wu])


def _ffn_gate(h, wg, name, comm=None):
    T = h.shape[0]
    tm = min(1024, T)

    def body(h_ref, wg_ref, a_ref):
        a_ref[...] = _bdot(h_ref[...], wg_ref[...], _NN)

    return _call(body, comm, name, (T // tm, NDEV),
                 [pl.BlockSpec((tm, D), lambda i, j: (i, 0)), pl.BlockSpec((None, D, FSH), lambda i, j: (j, 0, 0))],
                 [pl.BlockSpec((None, tm, FSH), lambda i, j: (j, i, 0))], [jax.ShapeDtypeStruct((NDEV, T, FSH), F32)], [],
                 ("parallel", "arbitrary"), [h, wg])


def _ffn_up_after_gate(h, wu, a, name, comm=None):
    T = h.shape[0]
    tm = min(1024, T)

    def body(h_ref, wu_ref, a_ref, u_ref, s_ref):
        u = _bdot(h_ref[...], wu_ref[...], _NN)
        u_ref[...] = u
        s_ref[...] = (_silu(a_ref[...]) * u).astype(BF16)

    act = pl.BlockSpec((None, tm, FSH), lambda i, j: (j, i, 0))
    sh = lambda dt: jax.ShapeDtypeStruct((NDEV, T, FSH), dt)
    return _call(body, comm, name, (T // tm, NDEV),
                 [pl.BlockSpec((tm, D), lambda i, j: (i, 0)), pl.BlockSpec((None, D, FSH), lambda i, j: (j, 0, 0)), act],
                 [act, act], [sh(F32), sh(BF16)], [], ("parallel", "arbitrary"), [h, wu, a])


def _ffn_down(s, wd, x, name, comm=None):
    T = x.shape[0]
    tm, tn, sh = min(1024, T), 1024, 4

    def body(s_ref, wd_ref, x_ref, o_ref, acc):
        part = _bdot(s_ref[0], wd_ref[0], _NN)
        for q in range(1, sh):
            part = part + _bdot(s_ref[q], wd_ref[q], _NN)

        def finish(total):
            o_ref[...] = x_ref[...] + 0.5 * total

        _accumulate(pl.program_id(2), NDEV // sh, acc, part, finish)

    xo = pl.BlockSpec((tm, tn), lambda i, n, j: (i, n))
    out = _call(body, comm, name, (T // tm, D // tn, NDEV // sh),
                [pl.BlockSpec((sh, tm, FSH), lambda i, n, j: (j, i, 0)),
                 pl.BlockSpec((sh, FSH, tn), lambda i, n, j: (j, 0, n)), xo],
                [xo], [jax.ShapeDtypeStruct((T, D), F32)], [pltpu.VMEM((tm, tn), F32)],
                ("parallel", "parallel", "arbitrary"), [s, wd, x])
    return out[0] if comm is None else out


def _ffn_bwd_hidden(dx, wd, a, u, name):
    T = dx.shape[0]
    tm = min(1024, T)

    def body(dx_ref, wd_ref, a_ref, u_ref, da_ref, du_ref):
        ds = 0.5 * _bdot(dx_ref[...], wd_ref[...], _NT)
        av = a_ref[...]
        sg = _sigmoid(av)
        da_ref[...] = (ds * u_ref[...] * (sg * (1.0 + av * (1.0 - sg)))).astype(BF16)
        du_ref[...] = (ds * (av * sg)).astype(BF16)

    act = pl.BlockSpec((None, tm, FSH), lambda i, j: (j, i, 0))
    sh = jax.ShapeDtypeStruct((NDEV, T, FSH), BF16)
    return pl.pallas_call(
        body, name=name, grid=(T // tm, NDEV),
        in_specs=[pl.BlockSpec((tm, D), lambda i, j: (i, 0)), pl.BlockSpec((None, FSH, D), lambda i, j: (j, 0, 0)),
                  act, act],
        out_specs=[act, act], out_shape=(sh, sh),
        compiler_params=_params(("parallel", "arbitrary")),
    )(dx, wd, a, u)


def _ffn_bwd_input(da, du, wg, wu, name, comm=None):
    T = da.shape[1]
    tm, tn, sh = min(1024, T), 1024, 2

    def body(da_ref, du_ref, wg_ref, wu_ref, o_ref, acc):
        part = _bdot(da_ref[0], wg_ref[0], _NT) + _bdot(du_ref[0], wu_ref[0], _NT)
        for q in range(1, sh):
            part = part + _bdot(da_ref[q], wg_ref[q], _NT) + _bdot(du_ref[q], wu_ref[q], _NT)

        def finish(total):
            o_ref[...] = total

        _accumulate(pl.program_id(2), NDEV // sh, acc, part, finish)

    act = pl.BlockSpec((sh, tm, FSH), lambda i, n, j: (j, i, 0))
    wsp = pl.BlockSpec((sh, tn, FSH), lambda i, n, j: (j, n, 0))
    out = _call(body, comm, name, (T // tm, D // tn, NDEV // sh), [act, act, wsp, wsp],
                [pl.BlockSpec((tm, tn), lambda i, n, j: (i, n))], [jax.ShapeDtypeStruct((T, D), F32)],
                [pltpu.VMEM((tm, tn), F32)], ("parallel", "parallel", "arbitrary"), [da, du, wg, wu])
    return out[0] if comm is None else out


def _ffn_grad_up(h, da, du, name, comm=None, core_major=False):
    T = h.shape[0]
    tm, tk = 1024, min(4096, T)
    nk = T // tk

    def body(h_ref, da_ref, du_ref, o_ref, acc_a, acc_u):
        k = pl.program_id(2)
        hb = h_ref[...]
        for acc, ref, slot in ((acc_a, da_ref, 0), (acc_u, du_ref, 1)):
            def finish(total, slot=slot):
                o_ref[slot] = total.astype(BF16)

            _accumulate(k, nk, acc, _bdot(hb, ref[...], _TN), finish)

    act = pl.BlockSpec((None, tk, FSH), lambda j, i, t: (j, t, 0))
    if core_major:
        o_spec = pl.BlockSpec((None, None, 2, tm, FSH), lambda j, i, t: (j % 2, j // 2, 0, i, 0))
        o_shape = jax.ShapeDtypeStruct((2, NDEV // 2, 2, D, FSH), BF16)
    else:
        o_spec = pl.BlockSpec((None, 2, tm, FSH), lambda j, i, t: (j, 0, i, 0))
        o_shape = jax.ShapeDtypeStruct((NDEV, 2, D, FSH), BF16)
    out = _call(body, comm, name, (NDEV, D // tm, nk), [pl.BlockSpec((tk, tm), lambda j, i, t: (t, i)), act, act],
                [o_spec], [o_shape],
                [pltpu.VMEM((tm, FSH), F32), pltpu.VMEM((tm, FSH), F32)], ("parallel", "parallel", "arbitrary"), [h, da, du])
    return out[0] if comm is None else out


def _ffn_grad_down(s, dx, name):
    T = dx.shape[0]
    tn, tk = 1024, min(2048, T)
    nk = T // tk

    def body(s_ref, dx_ref, o_ref, acc):
        def finish(total):
            o_ref[...] = (0.5 * total).astype(BF16)

        _accumulate(pl.program_id(2), nk, acc, _bdot(s_ref[...], dx_ref[...], _TN), finish)

    return pl.pallas_call(
        body, name=name, grid=(NDEV, D // tn, nk),
        in_specs=[pl.BlockSpec((None, tk, FSH), lambda j, n, t: (j, t, 0)), pl.BlockSpec((tk, tn), lambda j, n, t: (t, n))],
        out_specs=pl.BlockSpec((None, FSH, tn), lambda j, n, t: (j, 0, n)),
        out_shape=jax.ShapeDtypeStruct((NDEV, DFF // NDEV, D), BF16),
        scratch_shapes=[pltpu.VMEM((FSH, tn), F32)],
        compiler_params=_params(("parallel", "parallel", "arbitrary")),
    )(s, dx)


def _branch_merge(o_gla, o_rw, wb, pg, gate_b):
    T = o_gla.shape[0]
    tm, tn = min(1024, T), 512

    def body(og_ref, or_ref, w1_ref, w2_ref, p1_ref, p2_ref, b1_ref, b2_ref, yg_ref, yr_ref, m_ref):
        yg = _bdot(og_ref[...], w1_ref[...], _NN)
        yr = _bdot(or_ref[...], w2_ref[...], _NN)
        yg_ref[...] = yg
        yr_ref[...] = yr
        m_ref[...] = (_sigmoid(p1_ref[...] + b1_ref[...]) * yg + _sigmoid(p2_ref[...] + b2_ref[...]) * yr).astype(BF16)

    nj = D // tn
    act = pl.BlockSpec((tm, GLA_V), lambda i, j: (i, 0))
    out = pl.BlockSpec((tm, tn), lambda i, j: (i, j))
    return pl.pallas_call(
        body, name="branch_merge", grid=(T // tm, nj),
        in_specs=[act, act, pl.BlockSpec((GLA_V, tn), lambda i, j: (0, j)), pl.BlockSpec((RW_W, tn), lambda i, j: (1, j)),
                  out, pl.BlockSpec((tm, tn), lambda i, j: (i, nj + j)),
                  pl.BlockSpec((1, tn), lambda i, j: (0, j)), pl.BlockSpec((1, tn), lambda i, j: (0, nj + j))],
        out_specs=[out, out, out],
        out_shape=(jax.ShapeDtypeStruct((T, D), F32), jax.ShapeDtypeStruct((T, D), F32), jax.ShapeDtypeStruct((T, D), BF16)),
        compiler_params=_params(("parallel", "arbitrary")),
    )(o_gla, o_rw, wb, wb, pg, pg, gate_b, gate_b)


def _head_indicator(width, heads):
    col = lax.broadcasted_iota(jnp.int32, (width, 128), 0) // (width // heads)
    ind = (col == lax.broadcasted_iota(jnp.int32, (width, 128), 1)).astype(F32)
    return ind, ind.T


def _ffn_fwd(x, g, wg, wu, wd, tag):
    h = _rms_fwd(x, g, "rms_" + tag)
    a, u, s = _ffn_up(h, wg, wu, "ffn_up_" + tag)
    return _ffn_down(s, wd, x, "ffn_down_" + tag), (h, a, u, s)


def _ffn_fwd_gathering(x, g, wg, wu_block, wd_block, next_block, tag):
    h = _rms_fwd(x, g, "rms_" + tag)
    a, wu = _ffn_gate(h, wg, "ffn_gate_" + tag, comm=("gather_late", [wu_block]))
    u, s, wd = _ffn_up_after_gate(h, wu, a, "ffn_up_" + tag, comm=("gather_late", [wd_block]))
    y, gathered = _ffn_down(s, wd, x, "ffn_down_" + tag, comm=("gather_late", [next_block]))
    return y, (h, a, u, s), wu, wd, gathered


def _ffn_bwd(dy, x, g, wg, wu, wd, saved, tag, exchange=False):
    h, a, u, s = saved
    dwd = _ffn_grad_down(s, dy, "ffn_grad_down_" + tag)
    da, du = _ffn_bwd_hidden(dy, wd, a, u, "ffn_bwd_hidden_" + tag)
    if exchange:
        dw_up, dwd = _ffn_grad_up(h, da, du, "ffn_grad_up_" + tag, comm=("exchange", [dwd]), core_major=True)
        dh, dw_up = _ffn_bwd_input(da, du, wg, wu, "ffn_bwd_input_" + tag, comm=("quad", [_chip_sum(dw_up, "up_" + tag)]))
    else:
        dw_up = _ffn_grad_up(h, da, du, "ffn_grad_up_" + tag)
        dh = _ffn_bwd_input(da, du, wg, wu, "ffn_bwd_input_" + tag)
    dx, dg = _rms_bwd(x, g, dh, dy, "rms_bwd_" + tag)
    return dx, dg, dw_up, dwd


def _local_step(x, target, w, blocks):
    T = x.shape[0]
    ind16, ind16_t = _head_indicator(RW_W, RW_HEADS)
    ind4, ind4_t = _head_indicator(GLA_V, GLA_HEADS)
    ltri = jnp.tril(jnp.ones((CHUNK, CHUNK), F32))
    gate_pars = [w["w0"], w["w_w2"], w["a0"], w["w_a2"], w["w_g2"], w["k_k"], w["k_a"], ind16, ind16_t]
    post_pars = [w["lnx_w"], w["lnx_b"], w["r_k"], ind16, ind16_t]

    x1, ffn1, wu1, wd1, g_proj = _ffn_fwd_gathering(x, w["g1"], w["wg1"], blocks["wu1"], blocks["wd1"], blocks["win"], "1")
    win = _align_proj(_unshard_cols(g_proj))
    h2 = _rms_fwd(x1, w["g2"], "rms_mix")
    proj = lambda n, off, name: _matmul(h2, win, "nn", T, n, D, 1024, 512, D, name, b_off=(0, off // 512))
    pg = proj(PG_W, 0, "proj_gate")
    pr = proj(PR_W, PG_W, "proj_rwkv")
    pa = proj(PA_W, PG_W + PR_W, "proj_gla")
    la = _gla_prep(pa, w["gla_w_a2"], w["gla_b_a"])
    o_raw, gla_states = _gla_core_fwd(pa, la, ltri)
    o_gla = _gla_post(o_raw, pa, w["gn"], ind4, ind4_t)
    rw, dec, k2, kk, b, g = _rw_prep(pr, w["mu"], gate_pars)
    y, rw_states, rw_sa, g_up2, g_down2 = _rw_core_fwd(rw, dec, k2, kk, b, gather=blocks["late"])
    w = {**w, **_late_weights(g_up2, g_down2)}
    o_rw = _rw_post(y, rw, k2, g, post_pars)
    y_gla, y_rw, merged = _branch_merge(o_gla, o_rw, w["wb"], pg, w["gate_b"])
    x2 = _matmul(merged, w["wo"], "nn", T, D, D, 1024, 1024, D, "out_proj", res=x1)
    x3, ffn2 = _ffn_fwd(x2, w["g3"], w["wg2"], w["wu2"], w["wd2"], "2")
    dx3, loss, d_gf = _loss_bwd(x3, target, w["gf"])

    grads = {"gf": d_gf}
    dx2, grads["g3"], grads["up2"], grads["wd2"] = _ffn_bwd(
        dx3, x2, w["g3"], w["wg2"], w["wu2"], w["wd2"], ffn2, "2")
    dm = _matmul(dx2, w["wo"], "nt", T, D, D, 1024, 1024, D, "out_proj_bwd")
    grads["wo"] = _matmul(merged, dx2, "tn", D, D, T, 1024, 1024, 2048, "out_proj_grad", out_dtype=BF16)
    dy_gla, dy_rw, dpg, grads["gate_b"] = _merge_bwd(dm, y_gla, y_rw, pg, w["gate_b"])
    do_gla = _matmul(dy_gla, w["wb"], "nt", T, GLA_V, D, 1024, 1024, D, "branch_gla_bwd")
    do_rw = _matmul(dy_rw, w["wb"], "nt", T, RW_W, D, 1024, 1024, D, "branch_rwkv_bwd", b_off=(1, 0))
    grads["wb"] = jnp.concatenate([
        _matmul(o_gla, dy_gla, "tn", GLA_V, D, T, 1024, 1024, 4096, "branch_gla_grad", out_dtype=BF16),
        _matmul(o_rw, dy_rw, "tn", RW_W, D, T, 1024, 1024, 4096, "branch_rwkv_grad", out_dtype=BF16)], axis=0)
    dy, dr2, dv2, dk2b, dg, grads["lnx_w"], grads["lnx_b"], grads["r_k"] = _rw_post_bwd(y, rw, k2, g, post_pars, do_rw)
    early = _late_grad_parts(grads)
    received = {}
    dr1, dw, dk2a, dv1, dkk, db, received["up2"], received["down2"] = _rw_core_bwd(
        rw, dec, k2, kk, b, rw_states, rw_sa, dy, exchange=early)
    drw, grads["mu"], grads["w0"], grads["w_w2"], grads["a0"], grads["w_a2"], grads["w_g2"], grads["k_k"], grads["k_a"] = (
        _rw_prep_bwd(pr, w["mu"], gate_pars, (dr1, dr2), (dv1, dv2), dw, (dk2a, dk2b), dkk, db, dg))
    dpr = _shift_bwd(drw, w["mu"])
    do_raw, dr_gla, grads["gn"] = _gla_post_bwd(o_raw, pa, w["gn"], ind4, ind4_t, do_gla)
    dq, dk, dv, dla = _gla_core_bwd(pa, la, ltri, gla_states, do_raw)
    da_down, grads["gla_w_a2"], grads["gla_b_a"] = _gla_prep_bwd(pa, w["gla_w_a2"], w["gla_b_a"], dla)
    dpa = jnp.concatenate([dq.astype(BF16), dk.astype(BF16), dv.astype(BF16), dr_gla, da_down,
                           jnp.zeros((T, PA_W - PA_USED), BF16)], axis=1)
    dp = jnp.concatenate([dpg, dpr, dpa], axis=1)
    d_win = _matmul(h2, dp, "tn", D, DIN_P, T, 1024, 1024, 4096, "proj_grad", out_dtype=BF16)
    d_win = _shard_cols(_unalign_proj(d_win)).reshape(NDEV // 2, 2, D, DIN_SH).swapaxes(0, 1)
    dh2, received["win"] = _matmul(dp, win, "nt", T, D, DIN_P, 1024, 1024, DIN_P // 4, "proj_bwd",
                                   comm=("quad", [_chip_sum(d_win, "win")]))
    dx1, grads["g2"] = _rms_bwd(x1, w["g2"], dh2, dx2, "rms_bwd_mix")
    dx, grads["g1"], received["up1"], received["wd1"] = _ffn_bwd(
        dx1, x, w["g1"], w["wg1"], wu1, wd1, ffn1, "1", exchange=True)
    return loss, dx, grads, received


BIG = ("ffn1_wg", "ffn1_wu", "ffn1_wd", "w_in", "w_branch", "w_out", "ffn2_wg", "ffn2_wu", "ffn2_wd")
SMALL_SHARDED = ("gla_w_a2", "rwkv_w_w2", "rwkv_w_a2", "rwkv_w_g2")
REPLICATED = ("ffn1_norm", "mix_norm", "gla_b_a", "gla_gn_w", "rwkv_mu", "rwkv_w0", "rwkv_a0", "rwkv_k_k", "rwkv_k_a",
              "rwkv_r_k", "rwkv_lnx_w", "rwkv_lnx_b", "gate_b", "ffn2_norm", "final_norm")
WEIGHTS = ("ffn1_norm", "ffn1_wg", "ffn1_wu", "ffn1_wd", "mix_norm", "w_in", "gla_w_a2", "gla_b_a", "gla_gn_w",
           "rwkv_mu", "rwkv_w0", "rwkv_w_w2", "rwkv_a0", "rwkv_w_a2", "rwkv_w_g2", "rwkv_k_k", "rwkv_k_a", "rwkv_r_k",
           "rwkv_lnx_w", "rwkv_lnx_b", "gate_b", "w_branch", "w_out", "ffn2_norm", "ffn2_wg", "ffn2_wu", "ffn2_wd",
           "final_norm")


def _unshard_cols(g):
    return jnp.transpose(g, (1, 0, 2)).reshape(g.shape[1], NDEV * g.shape[2])


def _shard_cols(a):
    return jnp.transpose(a.reshape(a.shape[0], NDEV, a.shape[1] // NDEV), (1, 0, 2))


def _pad_rows(a, rows):
    return jnp.pad(a, ((0, rows - a.shape[0]), (0, 0)))


def _align_rw(a):
    c = 3 * RW_W
    z = jnp.zeros((a.shape[0], LORA_P - DECAY_LORA), a.dtype)
    return jnp.concatenate([a[:, :c], a[:, c:c + DECAY_LORA], z, a[:, c + DECAY_LORA:c + 2 * DECAY_LORA], z,
                            a[:, c + 2 * DECAY_LORA:]], axis=1)


def _unalign_rw(a):
    c = 3 * RW_W
    return jnp.concatenate([a[:, :c + DECAY_LORA], a[:, c + LORA_P:c + LORA_P + AAA_LORA], a[:, c + 2 * LORA_P:]], axis=1)


def _align_proj(a):
    gla = jnp.pad(a[:, :GLA_IN], ((0, 0), (0, PA_W - GLA_IN)))
    return jnp.concatenate([a[:, GLA_IN + RW_IN:], _align_rw(a[:, GLA_IN:GLA_IN + RW_IN]), gla], axis=1)


def _unalign_proj(a):
    return jnp.concatenate([a[:, PG_W + PR_W:PG_W + PR_W + GLA_IN], _unalign_rw(a[:, PG_W:PG_W + PR_W]), a[:, :PG_W]], axis=1)


def _layout_weights(gb, gs, rep):
    row = lambda n: rep[n].reshape(1, -1)
    return {
        "wg1": gb["ffn1_wg"],
        "g1": row("ffn1_norm"), "g2": row("mix_norm"), "g3": row("ffn2_norm"), "gf": row("final_norm"),
        "gla_w_a2": _pad_rows(_unshard_cols(gs["gla_w_a2"]), LORA_P), "gla_b_a": row("gla_b_a"),
        "gn": jnp.tile(row("gla_gn_w"), (1, GLA_HEADS)),
        "mu": _align_rw(row("rwkv_mu")), "w0": row("rwkv_w0"), "a0": row("rwkv_a0"),
        "w_w2": _pad_rows(_unshard_cols(gs["rwkv_w_w2"]), LORA_P),
        "w_a2": _pad_rows(_unshard_cols(gs["rwkv_w_a2"]), LORA_P),
        "w_g2": _unshard_cols(gs["rwkv_w_g2"]),
        "k_k": row("rwkv_k_k"), "k_a": row("rwkv_k_a"), "r_k": row("rwkv_r_k"),
        "lnx_w": row("rwkv_lnx_w"), "lnx_b": row("rwkv_lnx_b"), "gate_b": row("gate_b"),
    }


LATE_ROWS = (("ffn2_wd", FSH), ("w_branch", (GLA_V + RW_W) // NDEV), ("w_out", D // NDEV))


def _late_weights(g_up, g_down):
    r1, r2 = LATE_ROWS[0][1], LATE_ROWS[0][1] + LATE_ROWS[1][1]
    return {"wg2": g_up[:, 0], "wu2": g_up[:, 1], "wd2": g_down[:, :r1],
            "wb": g_down[:, r1:r2].reshape(GLA_V + RW_W, D), "wo": g_down[:, r2:].reshape(D, D)}


def _late_grad_parts(g):
    return [g["up2"], jnp.concatenate([g["wd2"], g["wb"].reshape(NDEV, -1, D), g["wo"].reshape(NDEV, -1, D)], axis=1)]


def _layout_grads(g):
    return {
        "ffn1_norm": g["g1"], "mix_norm": g["g2"], "ffn2_norm": g["g3"], "final_norm": g["gf"],
        "gla_w_a2": g["gla_w_a2"][:GLA_LORA], "gla_b_a": g["gla_b_a"],
        "gla_gn_w": jnp.sum(g["gn"].reshape(GLA_HEADS, GLA_DV), axis=0, keepdims=True),
        "rwkv_mu": _unalign_rw(g["mu"]), "rwkv_w0": g["w0"], "rwkv_a0": g["a0"],
        "rwkv_w_w2": g["w_w2"][:DECAY_LORA], "rwkv_w_a2": g["w_a2"][:AAA_LORA], "rwkv_w_g2": g["w_g2"],
        "rwkv_k_k": g["k_k"], "rwkv_k_a": g["k_a"], "rwkv_r_k": g["r_k"],
        "rwkv_lnx_w": g["lnx_w"], "rwkv_lnx_b": g["lnx_b"], "gate_b": g["gate_b"],
    }


_MESH = pl.DeviceIdType.MESH
_ANY = pl.BlockSpec(memory_space=pl.ANY)


def _position():
    return lax.axis_index("x"), lax.axis_index("y"), lax.axis_index("c")


def _slot(p):
    return 4 * p[0] + 2 * p[1] + p[2]


def _comm_sems(n):
    if not n:
        return []
    return [pltpu.SemaphoreType.DMA((7 * n,)), pltpu.SemaphoreType.DMA((7 * n,)), pltpu.SemaphoreType.DMA((n,))]


def _gather_plan(ins, outs, send_sems, recv_sems, local_sems):
    n = len(ins)
    x, y, c = _position()
    me, sibling = (x, y, c), (x, y, 1 - c)
    chips = [(1 - x, y), (x, 1 - y), (1 - x, 1 - y)]

    def copy(a, k, block, to, src=None):
        dst = outs[a].at[_slot(block)]
        return pltpu.make_async_remote_copy(
            src_ref=dst if src is None else src, dst_ref=dst, send_sem=send_sems.at[7 * a + k],
            recv_sem=recv_sems.at[7 * a + k], device_id=to, device_id_type=_MESH)

    def local(a):
        return pltpu.make_async_copy(ins[a], outs[a].at[_slot(me)], local_sems.at[a])

    def own(a):
        return [copy(a, 0, me, sibling, src=ins[a])] + [copy(a, 1 + j, me, (*chip, c), src=ins[a]) for j, chip in enumerate(chips)]

    def start():
        for a in range(n):
            local(a).start()
            for cp in own(a):
                cp.start()

    def forward():
        for a in range(n):
            for j, chip in enumerate(chips):
                copy(a, 1 + j, (*chip, c), me).wait_recv()
                copy(a, 4 + j, (*chip, c), sibling).start()

    def finish():
        for a in range(n):
            copy(a, 0, sibling, me).wait_recv()
            for j, chip in enumerate(chips):
                copy(a, 4 + j, (*chip, 1 - c), me).wait_recv()
        for a in range(n):
            for cp in own(a) + [copy(a, 4 + j, (*chip, c), sibling) for j, chip in enumerate(chips)]:
                cp.wait_send()
            local(a).wait()

    return start, forward, finish


def _exchange_plan(ins, outs, send_sems, recv_sems, local_sems):
    n = len(ins)
    x, y, c = _position()
    me = (x, y, c)
    flip = lambda v, f: 1 - v if f else v
    peers = [(flip(x, fx), flip(y, fy), flip(c, fc))
             for fx, fy, fc in ((0, 0, 1), (1, 0, 0), (0, 1, 0), (1, 1, 0), (1, 0, 1), (0, 1, 1), (1, 1, 1))]

    def copy(a, k, src_slot, dst_slot):
        return pltpu.make_async_remote_copy(
            src_ref=ins[a].at[src_slot], dst_ref=outs[a].at[dst_slot], send_sem=send_sems.at[7 * a + k],
            recv_sem=recv_sems.at[7 * a + k], device_id=peers[k], device_id_type=_MESH)

    def local(a):
        return pltpu.make_async_copy(ins[a].at[_slot(me)], outs[a].at[_slot(me)], local_sems.at[a])

    def start():
        for a in range(n):
            local(a).start()
            for k, peer in enumerate(peers):
                copy(a, k, _slot(peer), _slot(me)).start()

    def finish():
        for a in range(n):
            for k, peer in enumerate(peers):
                copy(a, k, _slot(peer), _slot(peer)).wait_recv()
        for a in range(n):
            for k, peer in enumerate(peers):
                copy(a, k, _slot(peer), _slot(me)).wait_send()
            local(a).wait()

    return start, finish


def _sibling_plan(ins, outs, send_sems, recv_sems, local_sems):
    x, y, c = _position()

    def copy(a):
        return pltpu.make_async_remote_copy(
            src_ref=ins[a].at[1 - c], dst_ref=outs[a], send_sem=send_sems.at[7 * a], recv_sem=recv_sems.at[7 * a],
            device_id=(x, y, 1 - c), device_id_type=_MESH)

    def start():
        for a in range(len(ins)):
            copy(a).start()

    def finish():
        for a in range(len(ins)):
            copy(a).wait()

    return start, finish


def _quad_plan(ins, outs, send_sems, recv_sems, local_sems):
    n = len(ins)
    x, y, c = _position()
    mine = 2 * x + y
    peers = [(1 - x, y), (x, 1 - y), (1 - x, 1 - y)]

    def copy(a, k, src_slot, dst_slot):
        return pltpu.make_async_remote_copy(
            src_ref=ins[a].at[src_slot], dst_ref=outs[a].at[dst_slot], send_sem=send_sems.at[7 * a + k],
            recv_sem=recv_sems.at[7 * a + k], device_id=(*peers[k], c), device_id_type=_MESH)

    def local(a):
        return pltpu.make_async_copy(ins[a].at[mine], outs[a].at[mine], local_sems.at[a])

    def start():
        for a in range(n):
            local(a).start()
            for k, (px, py) in enumerate(peers):
                copy(a, k, 2 * px + py, mine).start()

    def finish():
        for a in range(n):
            for k, (px, py) in enumerate(peers):
                copy(a, k, 2 * px + py, 2 * px + py).wait_recv()
        for a in range(n):
            for k, (px, py) in enumerate(peers):
                copy(a, k, 2 * px + py, mine).wait_send()
            local(a).wait()

    return start, finish


_PLANS = {"gather": (_gather_plan, lambda s: (NDEV,) + s), "gather_late": (_gather_plan, lambda s: (NDEV,) + s),
          "exchange": (_exchange_plan, lambda s: s),
          "sibling": (_sibling_plan, lambda s: s[1:]), "quad": (_quad_plan, lambda s: s)}


def _exchange_now(kind, arrays, name):
    n = len(arrays)

    def body(*refs):
        for stage in _PLANS[kind][0](refs[:n], refs[n:2 * n], *refs[2 * n:]):
            stage()

    return pl.pallas_call(
        body, name=name, in_specs=[_ANY] * n, out_specs=[_ANY] * n,
        out_shape=[jax.ShapeDtypeStruct(_PLANS[kind][1](a.shape), a.dtype) for a in arrays], scratch_shapes=_comm_sems(n),
    )(*arrays)


def _chip_sum(parts, name):
    (theirs,) = _exchange_now("sibling", [parts], "chip_send_" + name)
    shape = theirs.shape
    rows = shape[-2]
    for d in shape[1:-2]:
        rows *= d
    flat = (4, rows, shape[-1])
    tr = 512

    def body(both_ref, b_ref, o_ref):
        mine = jnp.where(lax.axis_index("c") == 0, both_ref[0], both_ref[1])
        o_ref[...] = (mine.astype(F32) + b_ref[...].astype(F32)).astype(BF16)

    blk = pl.BlockSpec((None, tr, shape[-1]), lambda s, r: (s, r, 0))
    out = pl.pallas_call(
        body, name="chip_sum_" + name, grid=(4, rows // tr),
        in_specs=[pl.BlockSpec((2, None, tr, shape[-1]), lambda s, r: (0, s, r, 0)), blk], out_specs=blk,
        out_shape=jax.ShapeDtypeStruct(flat, BF16), compiler_params=_params(("parallel", "parallel")),
    )(parts.reshape((2,) + flat), theirs.reshape(flat))
    return out.reshape(shape)


def _adamw_math(w, g, m, v):
    m = ADAM_B1 * m + (1.0 - ADAM_B1) * g
    v = ADAM_B2 * v + (1.0 - ADAM_B2) * (g * g)
    m_hat = m / (1.0 - ADAM_B1 ** ADAM_STEP)
    v_hat = v / (1.0 - ADAM_B2 ** ADAM_STEP)
    delta = -ADAM_LR * (m_hat / (jnp.sqrt(v_hat) + ADAM_EPS) + ADAM_WD * w)
    return delta, m, v


def _sum_slots(ref):
    total = ref[0].astype(F32)
    for s in range(1, ref.shape[0]):
        total = total + ref[s].astype(F32)
    return total


def _adamw(parts, w, m, v, tr, name, stack_index=None, row_block_offset=0, transposed=False):
    _, R, C = w.shape

    def body(p_ref, w_ref, m_ref, v_ref, g_ref, d_ref, nm_ref, nv_ref):
        g = _sum_slots(p_ref)
        g = g.T if transposed else g
        g_ref[...] = g
        d_ref[...], nm_ref[...], nv_ref[...] = _adamw_math(w_ref[...], g, m_ref[...], v_ref[...])

    if stack_index is None:
        p_spec = pl.BlockSpec((parts.shape[0], tr, C), lambda r: (0, row_block_offset + r, 0))
    else:
        p_spec = pl.BlockSpec((parts.shape[0], None, tr, C), lambda r: (0, stack_index, r, 0))
    if transposed:
        blk = pl.BlockSpec((None, C, tr), lambda r: (0, 0, r))
        out = jax.ShapeDtypeStruct((1, C, R), F32)
        w, m, v = (jnp.swapaxes(a, 1, 2) for a in (w, m, v))
    else:
        blk = pl.BlockSpec((None, tr, C), lambda r: (0, r, 0))
        out = jax.ShapeDtypeStruct((1, R, C), F32)
    res = pl.pallas_call(
        body, name=name, grid=(R // tr,), in_specs=[p_spec, blk, blk, blk], out_specs=[blk] * 4, out_shape=(out,) * 4,
        compiler_params=_params(("parallel",)),
    )(parts, w, m, v)
    return [jnp.swapaxes(a, 1, 2) for a in res] if transposed else res


def _sum_gathered(parts):
    _, R, C = parts.shape

    def body(p_ref, o_ref):
        o_ref[...] = _sum_slots(p_ref)

    return pl.pallas_call(body, name="small_grad_sum", out_shape=jax.ShapeDtypeStruct((R, C), F32),
                          compiler_params=_params())(parts)


def _adamw_small(w, g, m, v):
    def body(w_ref, g_ref, m_ref, v_ref, d_ref, nm_ref, nv_ref):
        d_ref[...], nm_ref[...], nv_ref[...] = _adamw_math(w_ref[...], g_ref[...], m_ref[...], v_ref[...])

    out = jax.ShapeDtypeStruct(w.shape, F32)
    return pl.pallas_call(body, name="adamw_small", out_shape=(out,) * 3, compiler_params=_params())(w, g, m, v)


def _pack(pieces, rows):
    flat = jnp.concatenate([p.reshape(-1) for p in pieces])
    return jnp.pad(flat, (0, rows * 128 - flat.shape[0])).reshape(rows, 128)


def _unpack(packed, shapes):
    flat = packed.reshape(-1)
    out, off = [], 0
    for s in shapes:
        size = 1
        for d in s:
            size *= d
        out.append(flat[off:off + size].reshape(s))
        off += size
    return out


def _rows_for(shapes, extra=0):
    total = extra
    for s in shapes:
        size = 1
        for d in s:
            size *= d
        total += size
    return -(-total // 1024) * 8


def kernel(x, ffn1_norm, ffn1_wg, ffn1_wu, ffn1_wd, mix_norm, w_in, gla_w_a2, gla_b_a, gla_gn_w, rwkv_mu, rwkv_w0, rwkv_w_w2, rwkv_a0, rwkv_w_a2, rwkv_w_g2, rwkv_k_k, rwkv_k_a, rwkv_r_k, rwkv_lnx_w, rwkv_lnx_b, gate_b, w_branch, w_out, ffn2_norm, ffn2_wg, ffn2_wu, ffn2_wd, final_norm, loss_target, m_ffn1_norm, m_ffn1_wg, m_ffn1_wu, m_ffn1_wd, m_mix_norm, m_w_in, m_gla_w_a2, m_gla_b_a, m_gla_gn_w, m_rwkv_mu, m_rwkv_w0, m_rwkv_w_w2, m_rwkv_a0, m_rwkv_w_a2, m_rwkv_w_g2, m_rwkv_k_k, m_rwkv_k_a, m_rwkv_r_k, m_rwkv_lnx_w, m_rwkv_lnx_b, m_gate_b, m_w_branch, m_w_out, m_ffn2_norm, m_ffn2_wg, m_ffn2_wu, m_ffn2_wd, m_final_norm, v_ffn1_norm, v_ffn1_wg, v_ffn1_wu, v_ffn1_wd, v_mix_norm, v_w_in, v_gla_w_a2, v_gla_b_a, v_gla_gn_w, v_rwkv_mu, v_rwkv_w0, v_rwkv_w_w2, v_rwkv_a0, v_rwkv_w_a2, v_rwkv_w_g2, v_rwkv_k_k, v_rwkv_k_a, v_rwkv_r_k, v_rwkv_lnx_w, v_rwkv_lnx_b, v_gate_b, v_w_branch, v_w_out, v_ffn2_norm, v_ffn2_wg, v_ffn2_wu, v_ffn2_wd, v_final_norm):
    wts = dict(zip(WEIGHTS, (ffn1_norm, ffn1_wg, ffn1_wu, ffn1_wd, mix_norm, w_in, gla_w_a2, gla_b_a, gla_gn_w, rwkv_mu, rwkv_w0, rwkv_w_w2, rwkv_a0, rwkv_w_a2, rwkv_w_g2, rwkv_k_k, rwkv_k_a, rwkv_r_k, rwkv_lnx_w, rwkv_lnx_b, gate_b, w_branch, w_out, ffn2_norm, ffn2_wg, ffn2_wu, ffn2_wd, final_norm)))
    mom = dict(zip(WEIGHTS, (m_ffn1_norm, m_ffn1_wg, m_ffn1_wu, m_ffn1_wd, m_mix_norm, m_w_in, m_gla_w_a2, m_gla_b_a, m_gla_gn_w, m_rwkv_mu, m_rwkv_w0, m_rwkv_w_w2, m_rwkv_a0, m_rwkv_w_a2, m_rwkv_w_g2, m_rwkv_k_k, m_rwkv_k_a, m_rwkv_r_k, m_rwkv_lnx_w, m_rwkv_lnx_b, m_gate_b, m_w_branch, m_w_out, m_ffn2_norm, m_ffn2_wg, m_ffn2_wu, m_ffn2_wd, m_final_norm)))
    var = dict(zip(WEIGHTS, (v_ffn1_norm, v_ffn1_wg, v_ffn1_wu, v_ffn1_wd, v_mix_norm, v_w_in, v_gla_w_a2, v_gla_b_a, v_gla_gn_w, v_rwkv_mu, v_rwkv_w0, v_rwkv_w_w2, v_rwkv_a0, v_rwkv_w_a2, v_rwkv_w_g2, v_rwkv_k_k, v_rwkv_k_a, v_rwkv_r_k, v_rwkv_lnx_w, v_rwkv_lnx_b, v_gate_b, v_w_branch, v_w_out, v_ffn2_norm, v_ffn2_wg, v_ffn2_wu, v_ffn2_wd, v_final_norm)))
    two = lambda a: a.reshape(a.shape[-2:])

    bf = lambda n: two(wts[n]).astype(BF16)
    lora = jnp.concatenate([jnp.pad(two(gla_w_a2), ((0, 0), (0, 128 - GLA_QK // NDEV)))] +
                           [two(wts[n]) for n in SMALL_SHARDED[1:]], axis=0)
    g_wg1, g_lora = _exchange_now("gather", [bf("ffn1_wg"), lora], "gather_weights")
    gb = {"ffn1_wg": g_wg1}
    gs = {"gla_w_a2": g_lora[:, :GLA_LORA, :GLA_QK // NDEV]}
    row = GLA_LORA
    for n in SMALL_SHARDED[1:]:
        gs[n] = g_lora[:, row:row + wts[n].shape[1]]
        row += wts[n].shape[1]
    w = _layout_weights(gb, gs, {n: wts[n] for n in REPLICATED})
    blocks = {"wu1": bf("ffn1_wu"), "wd1": bf("ffn1_wd"), "win": bf("w_in"),
              "late": [jnp.stack([bf("ffn2_wg"), bf("ffn2_wu")]), jnp.concatenate([bf(n) for n, _ in LATE_ROWS], axis=0)]}

    loss_part, grad_x, grads, parts = _local_step(x[0], loss_target[0], w, blocks)
    small = _layout_grads(grads)
    result = {}
    state = lambda n: (wts[n], mom[n], var[n])
    for group, names in (("up1", ("ffn1_wg", "ffn1_wu")), ("up2", ("ffn2_wg", "ffn2_wu"))):
        for i, n in enumerate(names):
            result[n] = _adamw(parts[group], *state(n), 256, "adamw_" + n, stack_index=i, transposed=True)
    result["ffn1_wd"] = _adamw(parts["wd1"], *state("ffn1_wd"), 64, "adamw_ffn1_wd")
    row = 0
    for n, rows in LATE_ROWS:
        result[n] = _adamw(parts["down2"], *state(n), 64, "adamw_" + n, row_block_offset=row // 64)
        row += rows
    result["w_in"] = _adamw(parts["win"], *state("w_in"), 256, "adamw_w_in")

    small_names = [n for n in WEIGHTS if n not in BIG]
    full_shapes = [small[n].shape for n in small_names]
    rows_full = _rows_for(full_shapes, extra=128)
    packed = _pack([small[n] for n in small_names] + [loss_part], rows_full)
    (gathered,) = _exchange_now("gather", [packed], "gather_small_grads")
    total = _sum_gathered(gathered)
    *full_grads, loss_row = _unpack(total, full_shapes + [(1, 128)])
    me = _slot(_position())
    own = {}
    for n, g in zip(small_names, full_grads):
        if n in SMALL_SHARDED:
            cols = wts[n].shape[-1]
            g = lax.dynamic_slice_in_dim(g, me * cols, cols, axis=1)
        own[n] = g.reshape(wts[n].shape)
    own_shapes = [wts[n].shape for n in small_names]
    rows_own = _rows_for(own_shapes)
    pk = lambda d: _pack([d[n] for n in small_names], rows_own)
    d_s, m_s, v_s = _adamw_small(pk(wts), pk(own), pk(mom), pk(var))
    for n, d, m, v in zip(small_names, _unpack(d_s, own_shapes), _unpack(m_s, own_shapes), _unpack(v_s, own_shapes)):
        result[n] = (own[n], d, m, v)

    shaped = lambda n, k: result[n][k].reshape(wts[n].shape)
    return (loss_row[0, 0], grad_x[None],
            *[shaped(n, 0) for n in WEIGHTS], *[shaped(n, 1) for n in WEIGHTS],
            *[shaped(n, 2) for n in WEIGHTS], *[shaped(n, 3) for n in WEIGHTS])
```

```python
import functools

import jax
import jax.numpy as jnp
from jax import lax
from jax.experimental import pallas as pl
from jax.experimental.pallas import tpu as pltpu

F32 = jnp.float32
BF16 = jnp.bfloat16
HI = lax.Precision.HIGHEST

NDEV = 8
D = 2048
DFF = 5632
FSH = DFF // NDEV
CHUNK = 64
GLA_HEADS, GLA_DK, GLA_DV = 4, 128, 256
GLA_QK, GLA_V, GLA_LORA, GLA_TAU = 512, 1024, 16, 16.0
RW_HEADS, RW_HD, RW_W = 16, 64, 1024
DECAY_LORA, AAA_LORA, GATE_LORA = 96, 96, 256
GN_EPS = 64e-5
NORM_EPS = 1e-6
GLA_IN = 2 * GLA_QK + 2 * GLA_V + GLA_LORA
RW_IN = 3 * RW_W + DECAY_LORA + AAA_LORA + GATE_LORA
D_IN = GLA_IN + RW_IN + 2 * D
DIN_SH = D_IN // NDEV
PG_W = 2 * D
PR_W = 3584
PA_W = 3584
PA_USED = 2 * GLA_QK + 2 * GLA_V + 128
DIN_P = PG_W + PR_W + PA_W
LORA_P = 128

ADAM_LR, ADAM_B1, ADAM_B2, ADAM_EPS, ADAM_WD, ADAM_STEP = 0.001, 0.9, 0.999, 1e-08, 0.01, 10

VMEM_LIMIT = 56 * 1024 * 1024
RW_TB = 32
RW_G = 32
RW_NP = 8


def _params(sem=None, vmem=VMEM_LIMIT):
    return pltpu.CompilerParams(dimension_semantics=sem, vmem_limit_bytes=vmem)


def _pair_mask():
    return lax.broadcasted_iota(jnp.int32, (RW_HD, 2 * RW_HD), 1) < RW_HD


def _pair_rowsum(p, mask):
    tot = jnp.sum(p, axis=1, keepdims=True)
    first = jnp.sum(jnp.where(mask, p, 0.0), axis=1, keepdims=True)
    return first, tot - first


def _split_transposed(x_ref, q, dst_ref, base):
    xt = x_ref[:, 128 * q:128 * (q + 1)].T
    for g in range(RW_TB // RW_G):
        dst_ref[base + g, :, 0:RW_G] = xt[:, g * RW_G:(g + 1) * RW_G]


def _pair_column(tile_ref, idx, i, mask):
    return jnp.where(mask, tile_ref[idx, 0:RW_HD, i:i + 1], tile_ref[idx, RW_HD:, i:i + 1])


def _rw_core_fwd(rw, w, k2, kk, b, gather=()):
    T = rw.shape[0]
    nb = T // RW_TB
    ng = RW_TB // RW_G
    NP = RW_NP
    nc = len(gather)
    npair = RW_HEADS // 2 // NP

    def body(r_ref, v_ref, w_ref, k_ref, kk_ref, b_ref, *rest):
        g_in, (y_ref, st_ref, sa_ref), g_out = rest[:nc], rest[nc:nc + 3], rest[nc + 3:2 * nc + 3]
        s_scr, vt_scr, yt_scr, rows_scr = rest[2 * nc + 3:2 * nc + 7]
        pair, blk_i = pl.program_id(0), pl.program_id(1)
        if nc:
            start, forward, finish = _gather_plan(g_in, g_out, *rest[2 * nc + 7:])
            pl.when((pair == 0) & (blk_i == 0))(start)
            pl.when((pair == 0) & (blk_i == nb // 2))(forward)

        @pl.when(pl.program_id(1) == 0)
        def _():
            s_scr[...] = jnp.zeros_like(s_scr)
            yt_scr[...] = jnp.zeros_like(yt_scr)

        mask = _pair_mask()
        for q in range(NP):
            _split_transposed(v_ref, q, vt_scr, q * ng)
        R_, W_, K_, KK_, B_ = range(5)
        for a, ref in enumerate((r_ref, w_ref, k_ref, kk_ref, b_ref)):
            for q in range(NP):
                rows_scr[a * NP + q] = ref[:, 128 * q:128 * (q + 1)]

        def group(g, states):
            states = list(states)
            for i in range(RW_G):
                t = g * RW_G + i
                row = lambda a, q: rows_scr[a * NP + q, pl.ds(t, 1), :]
                sums = [_pair_rowsum(states[q] * row(KK_, q), mask) for q in range(NP)]
                for q in range(NP):
                    sa = jnp.where(mask, *sums[q])
                    sa_ref[q, t] = sa
                    states[q] = (states[q] * row(W_, q) - sa * row(B_, q)
                                 + _pair_column(vt_scr, q * ng + g, i, mask) * row(K_, q))
                    st_ref[q, t] = states[q]
                outs = [_pair_rowsum(states[q] * row(R_, q), mask) for q in range(NP)]
                for q in range(NP):
                    yt_scr[q * ng + g, 0:RW_HD, i:i + 1] = outs[q][0]
                    yt_scr[q * ng + g, RW_HD:, i:i + 1] = outs[q][1]
            return tuple(states)

        states = lax.fori_loop(0, ng, group, tuple(s_scr[q] for q in range(NP)))
        for q in range(NP):
            s_scr[q] = states[q]
            for g in range(ng):
                y_ref[g * RW_G:(g + 1) * RW_G, 128 * q:128 * (q + 1)] = yt_scr[q * ng + g].T[0:RW_G, :]
        if nc:
            pl.when((pair == npair - 1) & (blk_i == nb - 1))(finish)

    blk = lambda cb: pl.BlockSpec((RW_TB, 128 * NP), lambda p, i, cb=cb: (i, cb + p))
    tiles = pltpu.VMEM((NP * ng, 128, 128), F32)
    return pl.pallas_call(
        body, name="rw_core_fwd", grid=(npair, nb),
        in_specs=[blk(0), blk(2 * RW_W // (128 * NP)), blk(0), blk(0), blk(0), blk(0)] + [_ANY] * nc,
        out_specs=[blk(0)] + [pl.BlockSpec((NP, RW_TB, RW_HD, 128), lambda p, i: (p, i, 0, 0))] * 2 + [_ANY] * nc,
        out_shape=[jax.ShapeDtypeStruct((T, RW_W), F32)] + [jax.ShapeDtypeStruct((RW_HEADS // 2, T, RW_HD, 128), F32)] * 2
        + [jax.ShapeDtypeStruct((NDEV,) + a.shape, a.dtype) for a in gather],
        scratch_shapes=[pltpu.VMEM((NP, RW_HD, 128), F32), tiles, tiles, pltpu.VMEM((5 * NP, RW_TB, 128), F32)]
        + _comm_sems(nc),
        compiler_params=_params(("arbitrary", "arbitrary")),
    )(rw, rw, w, k2, kk, b, *gather)


def _rw_core_bwd(rw, w, k2, kk, b, states, sa_tiles, dy, exchange=()):
    T = rw.shape[0]
    nb = T // RW_TB
    ng = RW_TB // RW_G
    NP = RW_NP
    nc = len(exchange)
    npair = RW_HEADS // 2 // NP

    def body(r_ref, v_ref, w_ref, k_ref, kk_ref, b_ref, dy_ref, st_ref, sp_ref, sa_ref, *rest):
        e_in, e_out = rest[:nc], rest[nc + 6:2 * nc + 6]
        dr_ref, dw_ref, dk_ref, dv_ref, dkk_ref, db_ref = rest[nc:nc + 6]
        ds_scr, vt_scr, dyt_scr, dvt_scr, rows_scr, out_scr = rest[2 * nc + 6:2 * nc + 12]
        step = pl.program_id(1)
        if nc:
            start, finish = _exchange_plan(e_in, e_out, *rest[2 * nc + 12:])
            pl.when((pl.program_id(0) == 0) & (step == 0))(start)

        @pl.when(step == 0)
        def _():
            ds_scr[...] = jnp.zeros_like(ds_scr)
            dvt_scr[...] = jnp.zeros_like(dvt_scr)

        mask = _pair_mask()
        for q in range(NP):
            _split_transposed(v_ref, q, vt_scr, q * ng)
            _split_transposed(dy_ref, q, dyt_scr, q * ng)
        R_, W_, K_, KK_, B_ = range(5)
        for a, ref in enumerate((r_ref, w_ref, k_ref, kk_ref, b_ref)):
            for q in range(NP):
                rows_scr[a * NP + q] = ref[:, 128 * q:128 * (q + 1)]

        def group(gg, grads):
            g = ng - 1 - gg
            grads = list(grads)
            pairs = range(NP)
            for i in reversed(range(RW_G)):
                t = g * RW_G + i
                row = lambda a, q: rows_scr[a * NP + q, pl.ds(t, 1), :]

                def put(a, q, value):
                    out_scr[a * NP + q, pl.ds(t, 1), :] = value

                s_old = [st_ref[q, jnp.maximum(t - 1, 0)] for q in pairs]
                if i == 0:
                    s_old = [jnp.where(g == 0, jnp.where(step == nb - 1, 0.0, sp_ref[q, 0]), s_old[q]) for q in pairs]
                dycol = [_pair_column(dyt_scr, q * ng + g, i, mask) for q in pairs]
                dS = [grads[q] + dycol[q] * row(R_, q) for q in pairs]
                m = [_pair_rowsum(dS[q] * row(B_, q), mask) for q in pairs]
                dv = [_pair_rowsum(dS[q] * row(K_, q), mask) for q in pairs]
                for q in pairs:
                    put(R_, q, jnp.sum(st_ref[q, t] * dycol[q], axis=0, keepdims=True))
                    put(W_, q, jnp.sum(dS[q] * s_old[q], axis=0, keepdims=True))
                    put(K_, q, jnp.sum(dS[q] * _pair_column(vt_scr, q * ng + g, i, mask), axis=0, keepdims=True))
                for q in pairs:
                    dsa = -jnp.where(mask, *m[q])
                    grads[q] = dS[q] * row(W_, q) + dsa * row(KK_, q)
                    put(KK_, q, jnp.sum(s_old[q] * dsa, axis=0, keepdims=True))
                    put(B_, q, -jnp.sum(dS[q] * sa_ref[q, t], axis=0, keepdims=True))
                    dvt_scr[q * ng + g, 0:RW_HD, i:i + 1] = dv[q][0]
                    dvt_scr[q * ng + g, RW_HD:, i:i + 1] = dv[q][1]
            return tuple(grads)

        grads = lax.fori_loop(0, ng, group, tuple(ds_scr[q] for q in range(NP)))
        for q in range(NP):
            ds_scr[q] = grads[q]
            for a, ref in enumerate((dr_ref, dw_ref, dk_ref, dkk_ref, db_ref)):
                ref[:, 128 * q:128 * (q + 1)] = out_scr[a * NP + q]
            for g in range(ng):
                dv_ref[g * RW_G:(g + 1) * RW_G, 128 * q:128 * (q + 1)] = dvt_scr[q * ng + g].T[0:RW_G, :]
        if nc:
            pl.when((pl.program_id(0) == npair - 1) & (step == nb - 1))(finish)

    blk = lambda cb: pl.BlockSpec((RW_TB, 128 * NP), lambda p, i, cb=cb: (nb - 1 - i, cb + p))
    st_spec = pl.BlockSpec((NP, RW_TB, RW_HD, 128), lambda p, i: (p, nb - 1 - i, 0, 0))
    sp_spec = pl.BlockSpec((NP, 1, RW_HD, 128), lambda p, i: (p, jnp.maximum((nb - 1 - i) * RW_TB - 1, 0), 0, 0))
    out = jax.ShapeDtypeStruct((T, RW_W), F32)
    tiles = pltpu.VMEM((NP * ng, 128, 128), F32)
    return pl.pallas_call(
        body, name="rw_core_bwd", grid=(npair, nb),
        in_specs=[blk(0), blk(2 * RW_W // (128 * NP)), blk(0), blk(0), blk(0), blk(0), blk(0), st_spec, sp_spec, st_spec]
        + [_ANY] * nc,
        out_specs=[blk(0)] * 6 + [_ANY] * nc,
        out_shape=[out] * 6 + [jax.ShapeDtypeStruct(a.shape, a.dtype) for a in exchange],
        scratch_shapes=[pltpu.VMEM((NP, RW_HD, 128), F32), tiles, tiles, tiles,
                        pltpu.VMEM((5 * NP, RW_TB, 128), F32), pltpu.VMEM((5 * NP, RW_TB, 128), F32)] + _comm_sems(nc),
        compiler_params=_params(("arbitrary", "arbitrary")),
    )(rw, rw, w, k2, kk, b, dy, states, states, sa_tiles, *exchange)


GLA_CB = 8


def _gla_chunk(s_t, q, k, v, la, ltri):
    cum = jnp.dot(ltri, la, precision=HI, preferred_element_type=F32)
    total = jnp.sum(la, axis=0, keepdims=True)
    kdec = k * jnp.exp(total - cum)
    u_t = _bdot(v, kdec, _TN)
    s_t = jnp.exp(total) * s_t + u_t
    o = _bdot(q * (GLA_DK ** -0.5), s_t, _NT)
    return s_t, o


def _gla_core_fwd(pa, la, ltri):
    T = pa.shape[0]
    cb = min(GLA_CB, T // CHUNK)
    rows = cb * CHUNK
    nsteps = T // rows

    def body(q_ref, k_ref, v_ref, la_ref, ltri_ref, o_ref, st_ref, s_scr):
        @pl.when(pl.program_id(0) == 0)
        def _():
            s_scr[...] = jnp.zeros_like(s_scr)

        def chunk(c, states):
            sl = pl.ds(pl.multiple_of(c * CHUNK, CHUNK), CHUNK)
            out = []
            for h in range(GLA_HEADS):
                qk, vv = slice(GLA_DK * h, GLA_DK * (h + 1)), slice(GLA_DV * h, GLA_DV * (h + 1))
                s_t, o = _gla_chunk(states[h], q_ref[sl, qk], k_ref[sl, qk], v_ref[sl, vv], la_ref[sl, qk], ltri_ref[...])
                o_ref[sl, vv] = o
                st_ref[h, c] = s_t
                out.append(s_t)
            return tuple(out)

        states = lax.fori_loop(0, cb, chunk, tuple(s_scr[h] for h in range(GLA_HEADS)))
        for h in range(GLA_HEADS):
            s_scr[h] = states[h]

    qk = lambda cb_: pl.BlockSpec((rows, GLA_QK), lambda i, cb_=cb_: (i, cb_))
    return pl.pallas_call(
        body, name="gla_core_fwd", grid=(nsteps,),
        in_specs=[qk(0), qk(1), pl.BlockSpec((rows, GLA_V), lambda i: (i, 1)), qk(0),
                  pl.BlockSpec((CHUNK, CHUNK), lambda i: (0, 0))],
        out_specs=[pl.BlockSpec((rows, GLA_V), lambda i: (i, 0)),
                   pl.BlockSpec((GLA_HEADS, cb, GLA_DV, GLA_DK), lambda i: (0, i, 0, 0))],
        out_shape=(jax.ShapeDtypeStruct((T, GLA_V), F32),
                   jax.ShapeDtypeStruct((GLA_HEADS, T // CHUNK, GLA_DV, GLA_DK), F32)),
        scratch_shapes=[pltpu.VMEM((GLA_HEADS, GLA_DV, GLA_DK), F32)],
        compiler_params=_params(("arbitrary",)),
    )(pa, pa, pa, la, ltri)


def _gla_core_bwd(pa, la, ltri, states, do):
    T = pa.shape[0]
    cb = min(GLA_CB, T // CHUNK)
    rows = cb * CHUNK
    nsteps = T // rows

    def body(q_ref, k_ref, v_ref, la_ref, ltri_ref, st_ref, sp_ref, do_ref,
             dq_ref, dk_ref, dv_ref, dla_ref, ds_scr):
        step = pl.program_id(0)

        @pl.when(step == 0)
        def _():
            ds_scr[...] = jnp.zeros_like(ds_scr)

        def chunk(cc, grads):
            c = cb - 1 - cc
            sl = pl.ds(pl.multiple_of(c * CHUNK, CHUNK), CHUNK)
            out = []
            for h in range(GLA_HEADS):
                qk, vv = slice(GLA_DK * h, GLA_DK * (h + 1)), slice(GLA_DV * h, GLA_DV * (h + 1))
                s_before = jnp.where(step == nsteps - 1, 0.0, sp_ref[h, 0])
                s_prev = jnp.where(c == 0, s_before, st_ref[h, jnp.maximum(c - 1, 0)])
                _, vjp = jax.vjp(functools.partial(_gla_chunk, ltri=ltri_ref[...]),
                                 s_prev, q_ref[sl, qk], k_ref[sl, qk], v_ref[sl, vv], la_ref[sl, qk])
                ds_prev, dq, dk, dv, dla = vjp((grads[h], do_ref[sl, vv]))
                dq_ref[sl, qk] = dq
                dk_ref[sl, qk] = dk
                dv_ref[sl, vv] = dv
                dla_ref[sl, qk] = dla
                out.append(ds_prev)
            return tuple(out)

        grads = lax.fori_loop(0, cb, chunk, tuple(ds_scr[h] for h in range(GLA_HEADS)))
        for h in range(GLA_HEADS):
            ds_scr[h] = grads[h]

    r = lambda i: nsteps - 1 - i
    qk = lambda cb_: pl.BlockSpec((rows, GLA_QK), lambda i, cb_=cb_: (r(i), cb_))
    o512 = pl.BlockSpec((rows, GLA_QK), lambda i: (r(i), 0))
    o1024 = pl.BlockSpec((rows, GLA_V), lambda i: (r(i), 0))
    return pl.pallas_call(
        body, name="gla_core_bwd", grid=(nsteps,),
        in_specs=[qk(0), qk(1), pl.BlockSpec((rows, GLA_V), lambda i: (r(i), 1)), qk(0),
                  pl.BlockSpec((CHUNK, CHUNK), lambda i: (0, 0)),
                  pl.BlockSpec((GLA_HEADS, cb, GLA_DV, GLA_DK), lambda i: (0, r(i), 0, 0)),
                  pl.BlockSpec((GLA_HEADS, 1, GLA_DV, GLA_DK), lambda i: (0, jnp.maximum(r(i) * cb - 1, 0), 0, 0)),
                  o1024],
        out_specs=[o512, o512, o1024, o512],
        out_shape=(jax.ShapeDtypeStruct((T, GLA_QK), F32), jax.ShapeDtypeStruct((T, GLA_QK), F32),
                   jax.ShapeDtypeStruct((T, GLA_V), F32), jax.ShapeDtypeStruct((T, GLA_QK), F32)),
        scratch_shapes=[pltpu.VMEM((GLA_HEADS, GLA_DV, GLA_DK), F32)],
        compiler_params=_params(("arbitrary",)),
    )(pa, pa, pa, la, ltri, states, states, do)


def _rowwise(fn, name, T, tm, rows, pars, row_outs, acc_outs):
    nr, npar, nro = len(rows), len(pars), len(row_outs)
    tm = min(tm, T)
    nsteps = T // tm

    def body(*refs):
        i = pl.program_id(0)
        ins = [r[...] for r in refs[:nr + npar]]
        outs, accs = fn(i, *ins)
        for r, o in zip(refs[nr + npar:nr + npar + nro], outs):
            r[...] = o.astype(r.dtype)
        for r, a in zip(refs[nr + npar + nro:], accs):
            @pl.when(i == 0)
            def _(r=r, a=a):
                r[...] = a

            @pl.when(i > 0)
            def _(r=r, a=a):
                r[...] += a

    def rspec(width, cb, kind):
        if kind == "cur":
            return pl.BlockSpec((tm, width), lambda i: (i, cb))
        if kind == "prev":
            return pl.BlockSpec((8, width), lambda i: (jnp.maximum(i * (tm // 8) - 1, 0), cb))
        return pl.BlockSpec((8, width), lambda i: (jnp.minimum((i + 1) * (tm // 8), T // 8 - 1), cb))

    in_specs = [rspec(w, cb, kind) for (_, w, cb, kind) in rows]
    in_specs += [pl.BlockSpec(p.shape, lambda i, nd=p.ndim: (0,) * nd) for p in pars]
    out_specs = [pl.BlockSpec((tm, w), lambda i: (i, 0)) for (w, _) in row_outs]
    out_specs += [pl.BlockSpec(s, lambda i, nd=len(s): (0,) * nd) for s in acc_outs]
    out_shape = [jax.ShapeDtypeStruct((T, w), dt) for (w, dt) in row_outs]
    out_shape += [jax.ShapeDtypeStruct(s, F32) for s in acc_outs]
    res = pl.pallas_call(
        body, name=name, grid=(nsteps,), in_specs=in_specs, out_specs=out_specs, out_shape=out_shape,
        compiler_params=_params(("arbitrary",)),
    )(*[r[0] for r in rows], *pars)
    return res


def _cur(a, width=None, cb=0):
    return (a, a.shape[1] if width is None else width, cb, "cur")


def _sigmoid(x):
    return 1.0 / (1.0 + jnp.exp(-x))


def _silu(x):
    return x * _sigmoid(x)


def _softplus(x):
    return jnp.maximum(x, 0.0) + jnp.log(1.0 + jnp.exp(-jnp.abs(x)))


def _rms(x, g):
    return x * lax.rsqrt(jnp.mean(x * x, axis=-1, keepdims=True) + NORM_EPS) * g


def _dot_hi(a, b):
    return jnp.dot(a, b, precision=lax.Precision.HIGH, preferred_element_type=F32)


def _rms_fwd(x, g, name):
    T = x.shape[0]
    fn = lambda i, xb, gb: ((_rms(xb, gb),), ())
    return _rowwise(fn, name, T, 256, [_cur(x)], [g], [(D, BF16)], [])[0]


def _rms_bwd(x, g, dh, dres, name):
    T = x.shape[0]

    def fn(i, xb, dhb, drb, gb):
        _, vjp = jax.vjp(_rms, xb, gb)
        dx, dg = vjp(dhb)
        return (drb + dx,), (dg,)

    return _rowwise(fn, name, T, 256, [_cur(x), _cur(dh), _cur(dres)], [g], [(D, F32)], [(1, D)])


def _loss_bwd(x, target, g):
    T = x.shape[0]

    def loss(xb, gb, tb):
        err = _rms(xb, gb) - tb
        return 0.5 * jnp.sum(jnp.mean(err * err, axis=-1, keepdims=True))

    def fn(i, xb, tb, gb):
        val, (dx, dg) = jax.value_and_grad(loss, argnums=(0, 1))(xb, gb, tb)
        return (dx,), (jnp.full((1, 128), val, F32), dg)

    return _rowwise(fn, "loss_bwd", T, 256, [_cur(x), _cur(target)], [g], [(D, F32)], [(1, 128), (1, D)])


def _gla_la(a_down, w_a2, b_a):
    return -_softplus(-(_bdot(a_down, w_a2, _NN) + b_a)) * (1.0 / GLA_TAU)


def _gla_prep(pa, w_a2, b_a):
    T = pa.shape[0]
    fn = lambda i, ab, wb, bb: ((_gla_la(ab, wb, bb),), ())
    return _rowwise(fn, "gla_prep", T, 512, [_cur(pa, LORA_P, (2 * GLA_QK + 2 * GLA_V) // LORA_P)], [w_a2, b_a],
                    [(GLA_QK, F32)], [])[0]


def _gla_prep_bwd(pa, w_a2, b_a, dla):
    T = pa.shape[0]

    def fn(i, ab, dlab, wb, bb):
        _, vjp = jax.vjp(_gla_la, ab, wb, bb)
        da, dw, db = vjp(dlab)
        return (da,), (dw, db)

    return _rowwise(fn, "gla_prep_bwd", T, 512, [_cur(pa, LORA_P, (2 * GLA_QK + 2 * GLA_V) // LORA_P), _cur(dla)],
                    [w_a2, b_a], [(LORA_P, BF16)], [(LORA_P, GLA_QK), (1, GLA_QK)])


def _gla_out(o, r, gn, ind, ind_t):
    ms = _dot_hi(_dot_hi(o * o, ind) * (1.0 / GLA_DV), ind_t)
    return o * lax.rsqrt(ms + NORM_EPS) * gn * _silu(r)


def _gla_post(o_raw, pa, gn, ind, ind_t):
    T = pa.shape[0]
    fn = lambda i, ob, rb, gb, a, b: ((_gla_out(ob, rb, gb, a, b),), ())
    return _rowwise(fn, "gla_post", T, 256, [_cur(o_raw), _cur(pa, GLA_V, 2)], [gn, ind, ind_t], [(GLA_V, BF16)], [])[0]


def _gla_post_bwd(o_raw, pa, gn, ind, ind_t, do):
    T = pa.shape[0]

    def fn(i, ob, rb, dob, gb, a, b):
        _, vjp = jax.vjp(lambda o, r, g: _gla_out(o, r, g, a, b), ob, rb, gb)
        d_o, d_r, d_g = vjp(dob)
        return (d_o, d_r), (d_g,)

    return _rowwise(fn, "gla_post_bwd", T, 256, [_cur(o_raw), _cur(pa, GLA_V, 2), _cur(do)], [gn, ind, ind_t],
                    [(GLA_V, F32), (GLA_V, BF16)], [(1, GLA_V)])


def _shift_rows(cur, prev8, i):
    first = jnp.where(i == 0, 0.0, prev8[7:8, :])
    rolled = pltpu.roll(cur, 1, 0)
    return jnp.where(lax.broadcasted_iota(jnp.int32, cur.shape, 0) == 0, first, rolled)


def _rw_gates(rw, w0, w_w2, a0, w_a2, w_g2, k_k, k_a, ind, ind_t):
    rk = rw[:, RW_W:2 * RW_W]
    wd = rw[:, 3 * RW_W:3 * RW_W + LORA_P]
    ad = rw[:, 3 * RW_W + LORA_P:3 * RW_W + 2 * LORA_P]
    gd = rw[:, 3 * RW_W + 2 * LORA_P:]
    w_raw = w0 + _bdot(jnp.tanh(wd), w_w2, _NN)
    w = jnp.exp(-jnp.exp(-_softplus(-w_raw) - 0.5))
    a = _sigmoid(a0 + _bdot(ad, w_a2, _NN))
    g = _bdot(_sigmoid(gd), w_g2, _NN)
    kk = rk * k_k
    kk = kk * _dot_hi(lax.rsqrt(jnp.maximum(_dot_hi(kk * kk, ind), 1e-24)), ind_t)
    k2 = rk * (1.0 + (a - 1.0) * k_a)
    return w, k2, kk, kk * a, g


def _rw_prep(pr, mu, gate_pars):
    T = pr.shape[0]

    def fn(i, cur, prev8, mub, *gp):
        rw = cur + mub * (_shift_rows(cur, prev8, i) - cur)
        return (rw,) + _rw_gates(rw, *gp), ()

    return _rowwise(fn, "rw_prep", T, 256, [_cur(pr), (pr, PR_W, 0, "prev")], [mu, *gate_pars],
                    [(PR_W, F32)] + [(RW_W, F32)] * 5, [])


def _rw_prep_bwd(pr, mu, gate_pars, d_r, d_v, d_w, d_k2, d_kk, d_b, d_g):
    T = pr.shape[0]
    rows = [_cur(pr), (pr, PR_W, 0, "prev")] + [_cur(x) for x in (*d_r, *d_v, d_w, *d_k2, d_kk, d_b, d_g)]
    acc = [(1, PR_W)] + [tuple(p.shape) for p in gate_pars[:-2]]

    def fn(i, cur, prev8, dr1, dr2, dv1, dv2, dw, dk1, dk2, dkk, db, dg, mub, *gp):
        sh = _shift_rows(cur, prev8, i)
        rw = cur + mub * (sh - cur)
        _, vjp = jax.vjp(lambda x, *p: _rw_gates(x, *p, gp[-2], gp[-1]), rw, *gp[:-2])
        grads = vjp((dw, dk1 + dk2, dkk, db, dg))
        zeros = jnp.zeros((cur.shape[0], PR_W - 3 * RW_W), F32)
        drw = grads[0] + jnp.concatenate([dr1 + dr2, jnp.zeros_like(dr1), dv1 + dv2, zeros], axis=1)
        dmu = jnp.sum(drw * (sh - cur), axis=0, keepdims=True)
        return (drw,), (dmu, *grads[1:])

    return _rowwise(fn, "rw_prep_bwd", T, 128, rows, [mu, *gate_pars], [(PR_W, F32)], acc)


def _shift_bwd(drw, mu):
    T = drw.shape[0]
    tm = min(256, T)

    def fn(i, cur, next8, mub):
        last = jnp.where(i == T // tm - 1, 0.0, next8[0:1, :])
        rolled = pltpu.roll(cur, cur.shape[0] - 1, 0)
        nxt = jnp.where(lax.broadcasted_iota(jnp.int32, cur.shape, 0) == cur.shape[0] - 1, last, rolled)
        return ((1.0 - mub) * cur + mub * nxt,), ()

    return _rowwise(fn, "shift_bwd", T, tm, [_cur(drw), (drw, PR_W, 0, "next")], [mu], [(PR_W, BF16)], [])[0]


def _rw_out(y, r, v, k2, g, lnx_w, lnx_b, r_k, ind, ind_t):
    mean = _dot_hi(_dot_hi(y, ind) * (1.0 / RW_HD), ind_t)
    yc = y - mean
    var = _dot_hi(_dot_hi(yc * yc, ind) * (1.0 / RW_HD), ind_t)
    yn = yc * lax.rsqrt(var + GN_EPS) * lnx_w + lnx_b
    bonus = _dot_hi(_dot_hi(r * k2 * r_k, ind), ind_t) * v
    return (yn + bonus) * g


def _rw_post(y, rw, k2, g, pars):
    T = y.shape[0]
    fn = lambda i, *a: ((_rw_out(*a),), ())
    return _rowwise(fn, "rw_post", T, 256, [_cur(y), _cur(rw, RW_W, 0), _cur(rw, RW_W, 2), _cur(k2), _cur(g)], pars,
                    [(RW_W, BF16)], [])[0]


def _rw_post_bwd(y, rw, k2, g, pars, do):
    T = y.shape[0]

    def fn(i, yb, rb, vb, kb, gb, dob, lw, lb, rk, ind, ind_t):
        _, vjp = jax.vjp(lambda *a: _rw_out(*a, ind, ind_t), yb, rb, vb, kb, gb, lw, lb, rk)
        gr = vjp(dob)
        return gr[:5], gr[5:]

    return _rowwise(fn, "rw_post_bwd", T, 256,
                    [_cur(y), _cur(rw, RW_W, 0), _cur(rw, RW_W, 2), _cur(k2), _cur(g), _cur(do)], pars,
                    [(RW_W, F32)] * 5, [(1, RW_W)] * 3)


def _merge_bwd(dm, y_gla, y_rw, pg, gate_b):
    T = dm.shape[0]

    def fn(i, dmb, ya, yr, p1, p2, gb):
        g1 = _sigmoid(p1 + gb[:, :D])
        g2 = _sigmoid(p2 + gb[:, D:])
        dp1 = dmb * ya * g1 * (1.0 - g1)
        dp2 = dmb * yr * g2 * (1.0 - g2)
        dp = jnp.concatenate([dp1, dp2], axis=1)
        return (dmb * g1, dmb * g2, dp), (jnp.sum(dp, axis=0, keepdims=True),)

    return _rowwise(fn, "merge_bwd", T, 256, [_cur(dm), _cur(y_gla), _cur(y_rw), _cur(pg, D, 0), _cur(pg, D, 1)],
                    [gate_b], [(D, BF16), (D, BF16), (PG_W, BF16)], [(1, PG_W)])


_NN = (((1,), (0,)), ((), ()))
_NT = (((1,), (1,)), ((), ()))
_TN = (((0,), (0,)), ((), ()))


def _bdot(a, b, dims):
    return lax.dot_general(a.astype(BF16), b.astype(BF16), dims, preferred_element_type=F32)


def _accumulate(k, nk, acc, part, finish):
    if nk == 1:
        finish(part)
        return

    @pl.when(k == 0)
    def _():
        acc[...] = part

    @pl.when(k > 0)
    def _():
        acc[...] += part

    @pl.when(k == nk - 1)
    def _():
        finish(acc[...])


def _call(body, comm, name, grid, in_specs, out_specs, out_shape, scratch_shapes, sem, operands):
    if comm is None:
        return pl.pallas_call(body, name=name, grid=grid, in_specs=in_specs, out_specs=out_specs, out_shape=out_shape,
                              scratch_shapes=scratch_shapes, compiler_params=_params(sem))(*operands)
    kind, arrays = comm
    nc, n_in, n_out, n_scr = len(arrays), len(in_specs), len(out_shape), len(scratch_shapes)
    total = 1
    for n in grid:
        total *= n

    def with_comm(*refs):
        own = refs[:n_in] + refs[n_in + nc:n_in + nc + n_out] + refs[n_in + 2 * nc + n_out:n_in + 2 * nc + n_out + n_scr]
        c_in, c_out, sems = refs[n_in:n_in + nc], refs[n_in + nc + n_out:n_in + 2 * nc + n_out], refs[-3:]
        step = 0
        for axis, n in enumerate(grid):
            step = step * n + pl.program_id(axis)
        start, *forward, finish = _PLANS[kind][0](c_in, c_out, *sems)
        pl.when(step == 0)(start)
        for stage in forward:
            pl.when(step == (total - 1 if kind == "gather_late" else total // 2))(stage)
        body(*own)
        pl.when(step == total - 1)(finish)

    return pl.pallas_call(
        with_comm, name=name, grid=grid, in_specs=list(in_specs) + [_ANY] * nc, out_specs=list(out_specs) + [_ANY] * nc,
        out_shape=list(out_shape) + [jax.ShapeDtypeStruct(_PLANS[kind][1](a.shape), a.dtype) for a in arrays],
        scratch_shapes=list(scratch_shapes) + _comm_sems(nc), compiler_params=_params(("arbitrary",) * len(grid)),
    )(*operands, *arrays)


def _matmul(a, b, mode, M, N, K, tm, tn, tk, name, a_off=(0, 0), b_off=(0, 0), res=None, scale=1.0, out_dtype=F32,
            comm=None):
    tm, tn, tk = min(tm, M), min(tn, N), min(tk, K)
    nk = K // tk
    if mode == "nn":
        a_spec = pl.BlockSpec((tm, tk), lambda i, j, k: (i + a_off[0], k + a_off[1]))
        b_spec = pl.BlockSpec((tk, tn), lambda i, j, k: (k + b_off[0], j + b_off[1]))
        dims = _NN
    elif mode == "nt":
        a_spec = pl.BlockSpec((tm, tk), lambda i, j, k: (i + a_off[0], k + a_off[1]))
        b_spec = pl.BlockSpec((tn, tk), lambda i, j, k: (j + b_off[0], k + b_off[1]))
        dims = _NT
    else:
        a_spec = pl.BlockSpec((tk, tm), lambda i, j, k: (k + a_off[0], i + a_off[1]))
        b_spec = pl.BlockSpec((tk, tn), lambda i, j, k: (k + b_off[0], j + b_off[1]))
        dims = _TN
    o_spec = pl.BlockSpec((tm, tn), lambda i, j, k: (i, j))

    def body(a_ref, b_ref, *rest):
        r_ref = rest[0] if res is not None else None
        o_ref = rest[1] if res is not None else rest[0]
        acc = rest[-1] if nk > 1 else None

        def finish(total):
            total = total * scale if scale != 1.0 else total
            if r_ref is not None:
                total = r_ref[...] + total
            o_ref[...] = total.astype(out_dtype)

        _accumulate(pl.program_id(2), nk, acc, _bdot(a_ref[...], b_ref[...], dims), finish)

    out = _call(body, comm, name, (M // tm, N // tn, nk), [a_spec, b_spec] + ([o_spec] if res is not None else []),
                [o_spec], [jax.ShapeDtypeStruct((M, N), out_dtype)], [pltpu.VMEM((tm, tn), F32)] if nk > 1 else [],
                ("parallel", "parallel", "arbitrary"), [a, b] + ([res] if res is not None else []))
    return out[0] if comm is None else out


def _ffn_up(h, wg, wu, name, comm=None):
    T = h.shape[0]
    tm = min(1024, T)

    def body(h_ref, wg_ref, wu_ref, a_ref, u_ref, s_ref):
        hb = h_ref[...]
        a = _bdot(hb, wg_ref[...], _NN)
        u = _bdot(hb, wu_ref[...], _NN)
        a_ref[...] = a
        u_ref[...] = u
        s_ref[...] = (_silu(a) * u).astype(BF16)

    w_spec = pl.BlockSpec((None, D, FSH), lambda i, j: (j, 0, 0))
    o_spec = pl.BlockSpec((None, tm, FSH), lambda i, j: (j, i, 0))
    sh = lambda dt: jax.ShapeDtypeStruct((NDEV, T, FSH), dt)
    return _call(body, comm, name, (T // tm, NDEV), [pl.BlockSpec((tm, D), lambda i, j: (i, 0)), w_spec, w_spec],
                 [o_spec] * 3, [sh(F32), sh(F32), sh(BF16)], [], ("parallel", "arbitrary"), [h, wg, wu])


def _ffn_gate(h, wg, name, comm=None):
    T = h.shape[0]
    tm = min(1024, T)

    def body(h_ref, wg_ref, a_ref):
        a_ref[...] = _bdot(h_ref[...], wg_ref[...], _NN)

    return _call(body, comm, name, (T // tm, NDEV),
                 [pl.BlockSpec((tm, D), lambda i, j: (i, 0)), pl.BlockSpec((None, D, FSH), lambda i, j: (j, 0, 0))],
                 [pl.BlockSpec((None, tm, FSH), lambda i, j: (j, i, 0))], [jax.ShapeDtypeStruct((NDEV, T, FSH), F32)], [],
                 ("parallel", "arbitrary"), [h, wg])


def _ffn_up_after_gate(h, wu, a, name, comm=None):
    T = h.shape[0]
    tm = min(1024, T)

    def body(h_ref, wu_ref, a_ref, u_ref, s_ref):
        u = _bdot(h_ref[...], wu_ref[...], _NN)
        u_ref[...] = u
        s_ref[...] = (_silu(a_ref[...]) * u).astype(BF16)

    act = pl.BlockSpec((None, tm, FSH), lambda i, j: (j, i, 0))
    sh = lambda dt: jax.ShapeDtypeStruct((NDEV, T, FSH), dt)
    return _call(body, comm, name, (T // tm, NDEV),
                 [pl.BlockSpec((tm, D), lambda i, j: (i, 0)), pl.BlockSpec((None, D, FSH), lambda i, j: (j, 0, 0)), act],
                 [act, act], [sh(F32), sh(BF16)], [], ("parallel", "arbitrary"), [h, wu, a])


def _ffn_down(s, wd, x, name, comm=None):
    T = x.shape[0]
    tm, tn, sh = min(1024, T), 1024, 4

    def body(s_ref, wd_ref, x_ref, o_ref, acc):
        part = _bdot(s_ref[0], wd_ref[0], _NN)
        for q in range(1, sh):
            part = part + _bdot(s_ref[q], wd_ref[q], _NN)

        def finish(total):
            o_ref[...] = x_ref[...] + 0.5 * total

        _accumulate(pl.program_id(2), NDEV // sh, acc, part, finish)

    xo = pl.BlockSpec((tm, tn), lambda i, n, j: (i, n))
    out = _call(body, comm, name, (T // tm, D // tn, NDEV // sh),
                [pl.BlockSpec((sh, tm, FSH), lambda i, n, j: (j, i, 0)),
                 pl.BlockSpec((sh, FSH, tn), lambda i, n, j: (j, 0, n)), xo],
                [xo], [jax.ShapeDtypeStruct((T, D), F32)], [pltpu.VMEM((tm, tn), F32)],
                ("parallel", "parallel", "arbitrary"), [s, wd, x])
    return out[0] if comm is None else out


def _ffn_bwd_hidden(dx, wd, a, u, name):
    T = dx.shape[0]
    tm = min(1024, T)

    def body(dx_ref, wd_ref, a_ref, u_ref, da_ref, du_ref):
        ds = 0.5 * _bdot(dx_ref[...], wd_ref[...], _NT)
        av = a_ref[...]
        sg = _sigmoid(av)
        da_ref[...] = (ds * u_ref[...] * (sg * (1.0 + av * (1.0 - sg)))).astype(BF16)
        du_ref[...] = (ds * (av * sg)).astype(BF16)

    act = pl.BlockSpec((None, tm, FSH), lambda i, j: (j, i, 0))
    sh = jax.ShapeDtypeStruct((NDEV, T, FSH), BF16)
    return pl.pallas_call(
        body, name=name, grid=(T // tm, NDEV),
        in_specs=[pl.BlockSpec((tm, D), lambda i, j: (i, 0)), pl.BlockSpec((None, FSH, D), lambda i, j: (j, 0, 0)),
                  act, act],
        out_specs=[act, act], out_shape=(sh, sh),
        compiler_params=_params(("parallel", "arbitrary")),
    )(dx, wd, a, u)


def _ffn_bwd_input(da, du, wg, wu, name, comm=None):
    T = da.shape[1]
    tm, tn, sh = min(1024, T), 1024, 2

    def body(da_ref, du_ref, wg_ref, wu_ref, o_ref, acc):
        part = _bdot(da_ref[0], wg_ref[0], _NT) + _bdot(du_ref[0], wu_ref[0], _NT)
        for q in range(1, sh):
            part = part + _bdot(da_ref[q], wg_ref[q], _NT) + _bdot(du_ref[q], wu_ref[q], _NT)

        def finish(total):
            o_ref[...] = total

        _accumulate(pl.program_id(2), NDEV // sh, acc, part, finish)

    act = pl.BlockSpec((sh, tm, FSH), lambda i, n, j: (j, i, 0))
    wsp = pl.BlockSpec((sh, tn, FSH), lambda i, n, j: (j, n, 0))
    out = _call(body, comm, name, (T // tm, D // tn, NDEV // sh), [act, act, wsp, wsp],
                [pl.BlockSpec((tm, tn), lambda i, n, j: (i, n))], [jax.ShapeDtypeStruct((T, D), F32)],
                [pltpu.VMEM((tm, tn), F32)], ("parallel", "parallel", "arbitrary"), [da, du, wg, wu])
    return out[0] if comm is None else out


def _ffn_grad_up(h, da, du, name, comm=None, core_major=False):
    T = h.shape[0]
    tm, tk = 1024, min(4096, T)
    nk = T // tk

    def body(h_ref, da_ref, du_ref, o_ref, acc_a, acc_u):
        k = pl.program_id(2)
        hb = h_ref[...]
        for acc, ref, slot in ((acc_a, da_ref, 0), (acc_u, du_ref, 1)):
            def finish(total, slot=slot):
                o_ref[slot] = total.astype(BF16)

            _accumulate(k, nk, acc, _bdot(hb, ref[...], _TN), finish)

    act = pl.BlockSpec((None, tk, FSH), lambda j, i, t: (j, t, 0))
    if core_major:
        o_spec = pl.BlockSpec((None, None, 2, tm, FSH), lambda j, i, t: (j % 2, j // 2, 0, i, 0))
        o_shape = jax.ShapeDtypeStruct((2, NDEV // 2, 2, D, FSH), BF16)
    else:
        o_spec = pl.BlockSpec((None, 2, tm, FSH), lambda j, i, t: (j, 0, i, 0))
        o_shape = jax.ShapeDtypeStruct((NDEV, 2, D, FSH), BF16)
    out = _call(body, comm, name, (NDEV, D // tm, nk), [pl.BlockSpec((tk, tm), lambda j, i, t: (t, i)), act, act],
                [o_spec], [o_shape],
                [pltpu.VMEM((tm, FSH), F32), pltpu.VMEM((tm, FSH), F32)], ("parallel", "parallel", "arbitrary"), [h, da, du])
    return out[0] if comm is None else out


def _ffn_grad_down(s, dx, name):
    T = dx.shape[0]
    tn, tk = 1024, min(2048, T)
    nk = T // tk

    def body(s_ref, dx_ref, o_ref, acc):
        def finish(total):
            o_ref[...] = (0.5 * total).astype(BF16)

        _accumulate(pl.program_id(2), nk, acc, _bdot(s_ref[...], dx_ref[...], _TN), finish)

    return pl.pallas_call(
        body, name=name, grid=(NDEV, D // tn, nk),
        in_specs=[pl.BlockSpec((None, tk, FSH), lambda j, n, t: (j, t, 0)), pl.BlockSpec((tk, tn), lambda j, n, t: (t, n))],
        out_specs=pl.BlockSpec((None, FSH, tn), lambda j, n, t: (j, 0, n)),
        out_shape=jax.ShapeDtypeStruct((NDEV, DFF // NDEV, D), BF16),
        scratch_shapes=[pltpu.VMEM((FSH, tn), F32)],
        compiler_params=_params(("parallel", "parallel", "arbitrary")),
    )(s, dx)


def _branch_merge(o_gla, o_rw, wb, pg, gate_b):
    T = o_gla.shape[0]
    tm, tn = min(1024, T), 512

    def body(og_ref, or_ref, w1_ref, w2_ref, p1_ref, p2_ref, b1_ref, b2_ref, yg_ref, yr_ref, m_ref):
        yg = _bdot(og_ref[...], w1_ref[...], _NN)
        yr = _bdot(or_ref[...], w2_ref[...], _NN)
        yg_ref[...] = yg
        yr_ref[...] = yr
        m_ref[...] = (_sigmoid(p1_ref[...] + b1_ref[...]) * yg + _sigmoid(p2_ref[...] + b2_ref[...]) * yr).astype(BF16)

    nj = D // tn
    act = pl.BlockSpec((tm, GLA_V), lambda i, j: (i, 0))
    out = pl.BlockSpec((tm, tn), lambda i, j: (i, j))
    return pl.pallas_call(
        body, name="branch_merge", grid=(T // tm, nj),
        in_specs=[act, act, pl.BlockSpec((GLA_V, tn), lambda i, j: (0, j)), pl.BlockSpec((RW_W, tn), lambda i, j: (1, j)),
                  out, pl.BlockSpec((tm, tn), lambda i, j: (i, nj + j)),
                  pl.BlockSpec((1, tn), lambda i, j: (0, j)), pl.BlockSpec((1, tn), lambda i, j: (0, nj + j))],
        out_specs=[out, out, out],
        out_shape=(jax.ShapeDtypeStruct((T, D), F32), jax.ShapeDtypeStruct((T, D), F32), jax.ShapeDtypeStruct((T, D), BF16)),
        compiler_params=_params(("parallel", "arbitrary")),
    )(o_gla, o_rw, wb, wb, pg, pg, gate_b, gate_b)


def _head_indicator(width, heads):
    col = lax.broadcasted_iota(jnp.int32, (width, 128), 0) // (width // heads)
    ind = (col == lax.broadcasted_iota(jnp.int32, (width, 128), 1)).astype(F32)
    return ind, ind.T


def _ffn_fwd(x, g, wg, wu, wd, tag):
    h = _rms_fwd(x, g, "rms_" + tag)
    a, u, s = _ffn_up(h, wg, wu, "ffn_up_" + tag)
    return _ffn_down(s, wd, x, "ffn_down_" + tag), (h, a, u, s)


def _ffn_fwd_gathering(x, g, wg, wu_block, wd_block, next_block, tag):
    h = _rms_fwd(x, g, "rms_" + tag)
    a, wu = _ffn_gate(h, wg, "ffn_gate_" + tag, comm=("gather_late", [wu_block]))
    u, s, wd = _ffn_up_after_gate(h, wu, a, "ffn_up_" + tag, comm=("gather_late", [wd_block]))
    y, gathered = _ffn_down(s, wd, x, "ffn_down_" + tag, comm=("gather_late", [next_block]))
    return y, (h, a, u, s), wu, wd, gathered


def _ffn_bwd(dy, x, g, wg, wu, wd, saved, tag, exchange=False):
    h, a, u, s = saved
    dwd = _ffn_grad_down(s, dy, "ffn_grad_down_" + tag)
    da, du = _ffn_bwd_hidden(dy, wd, a, u, "ffn_bwd_hidden_" + tag)
    if exchange:
        dw_up, dwd = _ffn_grad_up(h, da, du, "ffn_grad_up_" + tag, comm=("exchange", [dwd]), core_major=True)
        dh, dw_up = _ffn_bwd_input(da, du, wg, wu, "ffn_bwd_input_" + tag, comm=("quad", [_chip_sum(dw_up, "up_" + tag)]))
    else:
        dw_up = _ffn_grad_up(h, da, du, "ffn_grad_up_" + tag)
        dh = _ffn_bwd_input(da, du, wg, wu, "ffn_bwd_input_" + tag)
    dx, dg = _rms_bwd(x, g, dh, dy, "rms_bwd_" + tag)
    return dx, dg, dw_up, dwd


def _local_step(x, target, w, blocks):
    T = x.shape[0]
    ind16, ind16_t = _head_indicator(RW_W, RW_HEADS)
    ind4, ind4_t = _head_indicator(GLA_V, GLA_HEADS)
    ltri = jnp.tril(jnp.ones((CHUNK, CHUNK), F32))
    gate_pars = [w["w0"], w["w_w2"], w["a0"], w["w_a2"], w["w_g2"], w["k_k"], w["k_a"], ind16, ind16_t]
    post_pars = [w["lnx_w"], w["lnx_b"], w["r_k"], ind16, ind16_t]

    x1, ffn1, wu1, wd1, g_proj = _ffn_fwd_gathering(x, w["g1"], w["wg1"], blocks["wu1"], blocks["wd1"], blocks["win"], "1")
    win = _align_proj(_unshard_cols(g_proj))
    h2 = _rms_fwd(x1, w["g2"], "rms_mix")
    proj = lambda n, off, name: _matmul(h2, win, "nn", T, n, D, 1024, 512, D, name, b_off=(0, off // 512))
    pg = proj(PG_W, 0, "proj_gate")
    pr = proj(PR_W, PG_W, "proj_rwkv")
    pa = proj(PA_W, PG_W + PR_W, "proj_gla")
    la = _gla_prep(pa, w["gla_w_a2"], w["gla_b_a"])
    o_raw, gla_states = _gla_core_fwd(pa, la, ltri)
    o_gla = _gla_post(o_raw, pa, w["gn"], ind4, ind4_t)
    rw, dec, k2, kk, b, g = _rw_prep(pr, w["mu"], gate_pars)
    y, rw_states, rw_sa, g_up2, g_down2 = _rw_core_fwd(rw, dec, k2, kk, b, gather=blocks["late"])
    w = {**w, **_late_weights(g_up2, g_down2)}
    o_rw = _rw_post(y, rw, k2, g, post_pars)
    y_gla, y_rw, merged = _branch_merge(o_gla, o_rw, w["wb"], pg, w["gate_b"])
    x2 = _matmul(merged, w["wo"], "nn", T, D, D, 1024, 1024, D, "out_proj", res=x1)
    x3, ffn2 = _ffn_fwd(x2, w["g3"], w["wg2"], w["wu2"], w["wd2"], "2")
    dx3, loss, d_gf = _loss_bwd(x3, target, w["gf"])

    grads = {"gf": d_gf}
    dx2, grads["g3"], grads["up2"], grads["wd2"] = _ffn_bwd(
        dx3, x2, w["g3"], w["wg2"], w["wu2"], w["wd2"], ffn2, "2")
    dm = _matmul(dx2, w["wo"], "nt", T, D, D, 1024, 1024, D, "out_proj_bwd")
    grads["wo"] = _matmul(merged, dx2, "tn", D, D, T, 1024, 1024, 2048, "out_proj_grad", out_dtype=BF16)
    dy_gla, dy_rw, dpg, grads["gate_b"] = _merge_bwd(dm, y_gla, y_rw, pg, w["gate_b"])
    do_gla = _matmul(dy_gla, w["wb"], "nt", T, GLA_V, D, 1024, 1024, D, "branch_gla_bwd")
    do_rw = _matmul(dy_rw, w["wb"], "nt", T, RW_W, D, 1024, 1024, D, "branch_rwkv_bwd", b_off=(1, 0))
    grads["wb"] = jnp.concatenate([
        _matmul(o_gla, dy_gla, "tn", GLA_V, D, T, 1024, 1024, 4096, "branch_gla_grad", out_dtype=BF16),
        _matmul(o_rw, dy_rw, "tn", RW_W, D, T, 1024, 1024, 4096, "branch_rwkv_grad", out_dtype=BF16)], axis=0)
    dy, dr2, dv2, dk2b, dg, grads["lnx_w"], grads["lnx_b"], grads["r_k"] = _rw_post_bwd(y, rw, k2, g, post_pars, do_rw)
    early = _late_grad_parts(grads)
    received = {}
    dr1, dw, dk2a, dv1, dkk, db, received["up2"], received["down2"] = _rw_core_bwd(
        rw, dec, k2, kk, b, rw_states, rw_sa, dy, exchange=early)
    drw, grads["mu"], grads["w0"], grads["w_w2"], grads["a0"], grads["w_a2"], grads["w_g2"], grads["k_k"], grads["k_a"] = (
        _rw_prep_bwd(pr, w["mu"], gate_pars, (dr1, dr2), (dv1, dv2), dw, (dk2a, dk2b), dkk, db, dg))
    dpr = _shift_bwd(drw, w["mu"])
    do_raw, dr_gla, grads["gn"] = _gla_post_bwd(o_raw, pa, w["gn"], ind4, ind4_t, do_gla)
    dq, dk, dv, dla = _gla_core_bwd(pa, la, ltri, gla_states, do_raw)
    da_down, grads["gla_w_a2"], grads["gla_b_a"] = _gla_prep_bwd(pa, w["gla_w_a2"], w["gla_b_a"], dla)
    dpa = jnp.concatenate([dq.astype(BF16), dk.astype(BF16), dv.astype(BF16), dr_gla, da_down,
                           jnp.zeros((T, PA_W - PA_USED), BF16)], axis=1)
    dp = jnp.concatenate([dpg, dpr, dpa], axis=1)
    d_win = _matmul(h2, dp, "tn", D, DIN_P, T, 1024, 1024, 4096, "proj_grad", out_dtype=BF16)
    d_win = _shard_cols(_unalign_proj(d_win)).reshape(NDEV // 2, 2, D, DIN_SH).swapaxes(0, 1)
    dh2, received["win"] = _matmul(dp, win, "nt", T, D, DIN_P, 1024, 1024, DIN_P // 4, "proj_bwd",
                                   comm=("quad", [_chip_sum(d_win, "win")]))
    dx1, grads["g2"] = _rms_bwd(x1, w["g2"], dh2, dx2, "rms_bwd_mix")
    dx, grads["g1"], received["up1"], received["wd1"] = _ffn_bwd(
        dx1, x, w["g1"], w["wg1"], wu1, wd1, ffn1, "1", exchange=True)
    return loss, dx, grads, received


BIG = ("ffn1_wg", "ffn1_wu", "ffn1_wd", "w_in", "w_branch", "w_out", "ffn2_wg", "ffn2_wu", "ffn2_wd")
SMALL_SHARDED = ("gla_w_a2", "rwkv_w_w2", "rwkv_w_a2", "rwkv_w_g2")
REPLICATED = ("ffn1_norm", "mix_norm", "gla_b_a", "gla_gn_w", "rwkv_mu", "rwkv_w0", "rwkv_a0", "rwkv_k_k", "rwkv_k_a",
              "rwkv_r_k", "rwkv_lnx_w", "rwkv_lnx_b", "gate_b", "ffn2_norm", "final_norm")
WEIGHTS = ("ffn1_norm", "ffn1_wg", "ffn1_wu", "ffn1_wd", "mix_norm", "w_in", "gla_w_a2", "gla_b_a", "gla_gn_w",
           "rwkv_mu", "rwkv_w0", "rwkv_w_w2", "rwkv_a0", "rwkv_w_a2", "rwkv_w_g2", "rwkv_k_k", "rwkv_k_a", "rwkv_r_k",
           "rwkv_lnx_w", "rwkv_lnx_b", "gate_b", "w_branch", "w_out", "ffn2_norm", "ffn2_wg", "ffn2_wu", "ffn2_wd",
           "final_norm")


def _unshard_cols(g):
    return jnp.transpose(g, (1, 0, 2)).reshape(g.shape[1], NDEV * g.shape[2])


def _shard_cols(a):
    return jnp.transpose(a.reshape(a.shape[0], NDEV, a.shape[1] // NDEV), (1, 0, 2))


def _pad_rows(a, rows):
    return jnp.pad(a, ((0, rows - a.shape[0]), (0, 0)))


def _align_rw(a):
    c = 3 * RW_W
    z = jnp.zeros((a.shape[0], LORA_P - DECAY_LORA), a.dtype)
    return jnp.concatenate([a[:, :c], a[:, c:c + DECAY_LORA], z, a[:, c + DECAY_LORA:c + 2 * DECAY_LORA], z,
                            a[:, c + 2 * DECAY_LORA:]], axis=1)


def _unalign_rw(a):
    c = 3 * RW_W
    return jnp.concatenate([a[:, :c + DECAY_LORA], a[:, c + LORA_P:c + LORA_P + AAA_LORA], a[:, c + 2 * LORA_P:]], axis=1)


def _align_proj(a):
    gla = jnp.pad(a[:, :GLA_IN], ((0, 0), (0, PA_W - GLA_IN)))
    return jnp.concatenate([a[:, GLA_IN + RW_IN:], _align_rw(a[:, GLA_IN:GLA_IN + RW_IN]), gla], axis=1)


def _unalign_proj(a):
    return jnp.concatenate([a[:, PG_W + PR_W:PG_W + PR_W + GLA_IN], _unalign_rw(a[:, PG_W:PG_W + PR_W]), a[:, :PG_W]], axis=1)


def _layout_weights(gb, gs, rep):
    row = lambda n: rep[n].reshape(1, -1)
    return {
        "wg1": gb["ffn1_wg"],
        "g1": row("ffn1_norm"), "g2": row("mix_norm"), "g3": row("ffn2_norm"), "gf": row("final_norm"),
        "gla_w_a2": _pad_rows(_unshard_cols(gs["gla_w_a2"]), LORA_P), "gla_b_a": row("gla_b_a"),
        "gn": jnp.tile(row("gla_gn_w"), (1, GLA_HEADS)),
        "mu": _align_rw(row("rwkv_mu")), "w0": row("rwkv_w0"), "a0": row("rwkv_a0"),
        "w_w2": _pad_rows(_unshard_cols(gs["rwkv_w_w2"]), LORA_P),
        "w_a2": _pad_rows(_unshard_cols(gs["rwkv_w_a2"]), LORA_P),
        "w_g2": _unshard_cols(gs["rwkv_w_g2"]),
        "k_k": row("rwkv_k_k"), "k_a": row("rwkv_k_a"), "r_k": row("rwkv_r_k"),
        "lnx_w": row("rwkv_lnx_w"), "lnx_b": row("rwkv_lnx_b"), "gate_b": row("gate_b"),
    }


LATE_ROWS = (("ffn2_wd", FSH), ("w_branch", (GLA_V + RW_W) // NDEV), ("w_out", D // NDEV))


def _late_weights(g_up, g_down):
    r1, r2 = LATE_ROWS[0][1], LATE_ROWS[0][1] + LATE_ROWS[1][1]
    return {"wg2": g_up[:, 0], "wu2": g_up[:, 1], "wd2": g_down[:, :r1],
            "wb": g_down[:, r1:r2].reshape(GLA_V + RW_W, D), "wo": g_down[:, r2:].reshape(D, D)}


def _late_grad_parts(g):
    return [g["up2"], jnp.concatenate([g["wd2"], g["wb"].reshape(NDEV, -1, D), g["wo"].reshape(NDEV, -1, D)], axis=1)]


def _layout_grads(g):
    return {
        "ffn1_norm": g["g1"], "mix_norm": g["g2"], "ffn2_norm": g["g3"], "final_norm": g["gf"],
        "gla_w_a2": g["gla_w_a2"][:GLA_LORA], "gla_b_a": g["gla_b_a"],
        "gla_gn_w": jnp.sum(g["gn"].reshape(GLA_HEADS, GLA_DV), axis=0, keepdims=True),
        "rwkv_mu": _unalign_rw(g["mu"]), "rwkv_w0": g["w0"], "rwkv_a0": g["a0"],
        "rwkv_w_w2": g["w_w2"][:DECAY_LORA], "rwkv_w_a2": g["w_a2"][:AAA_LORA], "rwkv_w_g2": g["w_g2"],
        "rwkv_k_k": g["k_k"], "rwkv_k_a": g["k_a"], "rwkv_r_k": g["r_k"],
        "rwkv_lnx_w": g["lnx_w"], "rwkv_lnx_b": g["lnx_b"], "gate_b": g["gate_b"],
    }


_MESH = pl.DeviceIdType.MESH
_ANY = pl.BlockSpec(memory_space=pl.ANY)


def _position():
    return lax.axis_index("x"), lax.axis_index("y"), lax.axis_index("c")


def _slot(p):
    return 4 * p[0] + 2 * p[1] + p[2]


def _comm_sems(n):
    if not n:
        return []
    return [pltpu.SemaphoreType.DMA((7 * n,)), pltpu.SemaphoreType.DMA((7 * n,)), pltpu.SemaphoreType.DMA((n,))]


def _gather_plan(ins, outs, send_sems, recv_sems, local_sems):
    n = len(ins)
    x, y, c = _position()
    me, sibling = (x, y, c), (x, y, 1 - c)
    chips = [(1 - x, y), (x, 1 - y), (1 - x, 1 - y)]

    def copy(a, k, block, to, src=None):
        dst = outs[a].at[_slot(block)]
        return pltpu.make_async_remote_copy(
            src_ref=dst if src is None else src, dst_ref=dst, send_sem=send_sems.at[7 * a + k],
            recv_sem=recv_sems.at[7 * a + k], device_id=to, device_id_type=_MESH)

    def local(a):
        return pltpu.make_async_copy(ins[a], outs[a].at[_slot(me)], local_sems.at[a])

    def own(a):
        return [copy(a, 0, me, sibling, src=ins[a])] + [copy(a, 1 + j, me, (*chip, c), src=ins[a]) for j, chip in enumerate(chips)]

    def start():
        for a in range(n):
            local(a).start()
            for cp in own(a):
                cp.start()

    def forward():
        for a in range(n):
            for j, chip in enumerate(chips):
                copy(a, 1 + j, (*chip, c), me).wait_recv()
                copy(a, 4 + j, (*chip, c), sibling).start()

    def finish():
        for a in range(n):
            copy(a, 0, sibling, me).wait_recv()
            for j, chip in enumerate(chips):
                copy(a, 4 + j, (*chip, 1 - c), me).wait_recv()
        for a in range(n):
            for cp in own(a) + [copy(a, 4 + j, (*chip, c), sibling) for j, chip in enumerate(chips)]:
                cp.wait_send()
            local(a).wait()

    return start, forward, finish


def _exchange_plan(ins, outs, send_sems, recv_sems, local_sems):
    n = len(ins)
    x, y, c = _position()
    me = (x, y, c)
    flip = lambda v, f: 1 - v if f else v
    peers = [(flip(x, fx), flip(y, fy), flip(c, fc))
             for fx, fy, fc in ((0, 0, 1), (1, 0, 0), (0, 1, 0), (1, 1, 0), (1, 0, 1), (0, 1, 1), (1, 1, 1))]

    def copy(a, k, src_slot, dst_slot):
        return pltpu.make_async_remote_copy(
            src_ref=ins[a].at[src_slot], dst_ref=outs[a].at[dst_slot], send_sem=send_sems.at[7 * a + k],
            recv_sem=recv_sems.at[7 * a + k], device_id=peers[k], device_id_type=_MESH)

    def local(a):
        return pltpu.make_async_copy(ins[a].at[_slot(me)], outs[a].at[_slot(me)], local_sems.at[a])

    def start():
        for a in range(n):
            local(a).start()
            for k, peer in enumerate(peers):
                copy(a, k, _slot(peer), _slot(me)).start()

    def finish():
        for a in range(n):
            for k, peer in enumerate(peers):
                copy(a, k, _slot(peer), _slot(peer)).wait_recv()
        for a in range(n):
            for k, peer in enumerate(peers):
                copy(a, k, _slot(peer), _slot(me)).wait_send()
            local(a).wait()

    return start, finish


def _sibling_plan(ins, outs, send_sems, recv_sems, local_sems):
    x, y, c = _position()

    def copy(a):
        return pltpu.make_async_remote_copy(
            src_ref=ins[a].at[1 - c], dst_ref=outs[a], send_sem=send_sems.at[7 * a], recv_sem=recv_sems.at[7 * a],
            device_id=(x, y, 1 - c), device_id_type=_MESH)

    def start():
        for a in range(len(ins)):
            copy(a).start()

    def finish():
        for a in range(len(ins)):
            copy(a).wait()

    return start, finish


def _quad_plan(ins, outs, send_sems, recv_sems, local_sems):
    n = len(ins)
    x, y, c = _position()
    mine = 2 * x + y
    peers = [(1 - x, y), (x, 1 - y), (1 - x, 1 - y)]

    def copy(a, k, src_slot, dst_slot):
        return pltpu.make_async_remote_copy(
            src_ref=ins[a].at[src_slot], dst_ref=outs[a].at[dst_slot], send_sem=send_sems.at[7 * a + k],
            recv_sem=recv_sems.at[7 * a + k], device_id=(*peers[k], c), device_id_type=_MESH)

    def local(a):
        return pltpu.make_async_copy(ins[a].at[mine], outs[a].at[mine], local_sems.at[a])

    def start():
        for a in range(n):
            local(a).start()
            for k, (px, py) in enumerate(peers):
                copy(a, k, 2 * px + py, mine).start()

    def finish():
        for a in range(n):
            for k, (px, py) in enumerate(peers):
                copy(a, k, 2 * px + py, 2 * px + py).wait_recv()
        for a in range(n):
            for k, (px, py) in enumerate(peers):
                copy(a, k, 2 * px + py, mine).wait_send()
            local(a).wait()

    return start, finish


_PLANS = {"gather": (_gather_plan, lambda s: (NDEV,) + s), "gather_late": (_gather_plan, lambda s: (NDEV,) + s),
          "exchange": (_exchange_plan, lambda s: s),
          "sibling": (_sibling_plan, lambda s: s[1:]), "quad": (_quad_plan, lambda s: s)}


def _exchange_now(kind, arrays, name):
    n = len(arrays)

    def body(*refs):
        for stage in _PLANS[kind][0](refs[:n], refs[n:2 * n], *refs[2 * n:]):
            stage()

    return pl.pallas_call(
        body, name=name, in_specs=[_ANY] * n, out_specs=[_ANY] * n,
        out_shape=[jax.ShapeDtypeStruct(_PLANS[kind][1](a.shape), a.dtype) for a in arrays], scratch_shapes=_comm_sems(n),
    )(*arrays)


def _chip_sum(parts, name):
    (theirs,) = _exchange_now("sibling", [parts], "chip_send_" + name)
    shape = theirs.shape
    rows = shape[-2]
    for d in shape[1:-2]:
        rows *= d
    flat = (4, rows, shape[-1])
    tr = 512

    def body(both_ref, b_ref, o_ref):
        mine = jnp.where(lax.axis_index("c") == 0, both_ref[0], both_ref[1])
        o_ref[...] = (mine.astype(F32) + b_ref[...].astype(F32)).astype(BF16)

    blk = pl.BlockSpec((None, tr, shape[-1]), lambda s, r: (s, r, 0))
    out = pl.pallas_call(
        body, name="chip_sum_" + name, grid=(4, rows // tr),
        in_specs=[pl.BlockSpec((2, None, tr, shape[-1]), lambda s, r: (0, s, r, 0)), blk], out_specs=blk,
        out_shape=jax.ShapeDtypeStruct(flat, BF16), compiler_params=_params(("parallel", "parallel")),
    )(parts.reshape((2,) + flat), theirs.reshape(flat))
    return out.reshape(shape)


def _adamw_math(w, g, m, v):
    m = ADAM_B1 * m + (1.0 - ADAM_B1) * g
    v = ADAM_B2 * v + (1.0 - ADAM_B2) * (g * g)
    m_hat = m / (1.0 - ADAM_B1 ** ADAM_STEP)
    v_hat = v / (1.0 - ADAM_B2 ** ADAM_STEP)
    delta = -ADAM_LR * (m_hat / (jnp.sqrt(v_hat) + ADAM_EPS) + ADAM_WD * w)
    return delta, m, v


def _sum_slots(ref):
    total = ref[0].astype(F32)
    for s in range(1, ref.shape[0]):
        total = total + ref[s].astype(F32)
    return total


def _adamw(parts, w, m, v, tr, name, stack_index=None, row_block_offset=0, transposed=False):
    _, R, C = w.shape

    def body(p_ref, w_ref, m_ref, v_ref, g_ref, d_ref, nm_ref, nv_ref):
        g = _sum_slots(p_ref)
        g = g.T if transposed else g
        g_ref[...] = g
        d_ref[...], nm_ref[...], nv_ref[...] = _adamw_math(w_ref[...], g, m_ref[...], v_ref[...])

    if stack_index is None:
        p_spec = pl.BlockSpec((parts.shape[0], tr, C), lambda r: (0, row_block_offset + r, 0))
    else:
        p_spec = pl.BlockSpec((parts.shape[0], None, tr, C), lambda r: (0, stack_index, r, 0))
    if transposed:
        blk = pl.BlockSpec((None, C, tr), lambda r: (0, 0, r))
        out = jax.ShapeDtypeStruct((1, C, R), F32)
        w, m, v = (jnp.swapaxes(a, 1, 2) for a in (w, m, v))
    else:
        blk = pl.BlockSpec((None, tr, C), lambda r: (0, r, 0))
        out = jax.ShapeDtypeStruct((1, R, C), F32)
    res = pl.pallas_call(
        body, name=name, grid=(R // tr,), in_specs=[p_spec, blk, blk, blk], out_specs=[blk] * 4, out_shape=(out,) * 4,
        compiler_params=_params(("parallel",)),
    )(parts, w, m, v)
    return [jnp.swapaxes(a, 1, 2) for a in res] if transposed else res


def _sum_gathered(parts):
    _, R, C = parts.shape

    def body(p_ref, o_ref):
        o_ref[...] = _sum_slots(p_ref)

    return pl.pallas_call(body, name="small_grad_sum", out_shape=jax.ShapeDtypeStruct((R, C), F32),
                          compiler_params=_params())(parts)


def _adamw_small(w, g, m, v):
    def body(w_ref, g_ref, m_ref, v_ref, d_ref, nm_ref, nv_ref):
        d_ref[...], nm_ref[...], nv_ref[...] = _adamw_math(w_ref[...], g_ref[...], m_ref[...], v_ref[...])

    out = jax.ShapeDtypeStruct(w.shape, F32)
    return pl.pallas_call(body, name="adamw_small", out_shape=(out,) * 3, compiler_params=_params())(w, g, m, v)


def _pack(pieces, rows):
    flat = jnp.concatenate([p.reshape(-1) for p in pieces])
    return jnp.pad(flat, (0, rows * 128 - flat.shape[0])).reshape(rows, 128)


def _unpack(packed, shapes):
    flat = packed.reshape(-1)
    out, off = [], 0
    for s in shapes:
        size = 1
        for d in s:
            size *= d
        out.append(flat[off:off + size].reshape(s))
        off += size
    return out


def _rows_for(shapes, extra=0):
    total = extra
    for s in shapes:
        size = 1
        for d in s:
            size *= d
        total += size
    return -(-total // 1024) * 8


def kernel(x, ffn1_norm, ffn1_wg, ffn1_wu, ffn1_wd, mix_norm, w_in, gla_w_a2, gla_b_a, gla_gn_w, rwkv_mu, rwkv_w0, rwkv_w_w2, rwkv_a0, rwkv_w_a2, rwkv_w_g2, rwkv_k_k, rwkv_k_a, rwkv_r_k, rwkv_lnx_w, rwkv_lnx_b, gate_b, w_branch, w_out, ffn2_norm, ffn2_wg, ffn2_wu, ffn2_wd, final_norm, loss_target, m_ffn1_norm, m_ffn1_wg, m_ffn1_wu, m_ffn1_wd, m_mix_norm, m_w_in, m_gla_w_a2, m_gla_b_a, m_gla_gn_w, m_rwkv_mu, m_rwkv_w0, m_rwkv_w_w2, m_rwkv_a0, m_rwkv_w_a2, m_rwkv_w_g2, m_rwkv_k_k, m_rwkv_k_a, m_rwkv_r_k, m_rwkv_lnx_w, m_rwkv_lnx_b, m_gate_b, m_w_branch, m_w_out, m_ffn2_norm, m_ffn2_wg, m_ffn2_wu, m_ffn2_wd, m_final_norm, v_ffn1_norm, v_ffn1_wg, v_ffn1_wu, v_ffn1_wd, v_mix_norm, v_w_in, v_gla_w_a2, v_gla_b_a, v_gla_gn_w, v_rwkv_mu, v_rwkv_w0, v_rwkv_w_w2, v_rwkv_a0, v_rwkv_w_a2, v_rwkv_w_g2, v_rwkv_k_k, v_rwkv_k_a, v_rwkv_r_k, v_rwkv_lnx_w, v_rwkv_lnx_b, v_gate_b, v_w_branch, v_w_out, v_ffn2_norm, v_ffn2_wg, v_ffn2_wu, v_ffn2_wd, v_final_norm):
    wts = dict(zip(WEIGHTS, (ffn1_norm, ffn1_wg, ffn1_wu, ffn1_wd, mix_norm, w_in, gla_w_a2, gla_b_a, gla_gn_w, rwkv_mu, rwkv_w0, rwkv_w_w2, rwkv_a0, rwkv_w_a2, rwkv_w_g2, rwkv_k_k, rwkv_k_a, rwkv_r_k, rwkv_lnx_w, rwkv_lnx_b, gate_b, w_branch, w_out, ffn2_norm, ffn2_wg, ffn2_wu, ffn2_wd, final_norm)))
    mom = dict(zip(WEIGHTS, (m_ffn1_norm, m_ffn1_wg, m_ffn1_wu, m_ffn1_wd, m_mix_norm, m_w_in, m_gla_w_a2, m_gla_b_a, m_gla_gn_w, m_rwkv_mu, m_rwkv_w0, m_rwkv_w_w2, m_rwkv_a0, m_rwkv_w_a2, m_rwkv_w_g2, m_rwkv_k_k, m_rwkv_k_a, m_rwkv_r_k, m_rwkv_lnx_w, m_rwkv_lnx_b, m_gate_b, m_w_branch, m_w_out, m_ffn2_norm, m_ffn2_wg, m_ffn2_wu, m_ffn2_wd, m_final_norm)))
    var = dict(zip(WEIGHTS, (v_ffn1_norm, v_ffn1_wg, v_ffn1_wu, v_ffn1_wd, v_mix_norm, v_w_in, v_gla_w_a2, v_gla_b_a, v_gla_gn_w, v_rwkv_mu, v_rwkv_w0, v_rwkv_w_w2, v_rwkv_a0, v_rwkv_w_a2, v_rwkv_w_g2, v_rwkv_k_k, v_rwkv_k_a, v_rwkv_r_k, v_rwkv_lnx_w, v_rwkv_lnx_b, v_gate_b, v_w_branch, v_w_out, v_ffn2_norm, v_ffn2_wg, v_ffn2_wu, v_ffn2_wd, v_final_norm)))
    two = lambda a: a.reshape(a.shape[-2:])

    bf = lambda n: two(wts[n]).astype(BF16)
    lora = jnp.concatenate([jnp.pad(two(gla_w_a2), ((0, 0), (0, 128 - GLA_QK // NDEV)))] +
                           [two(wts[n]) for n in SMALL_SHARDED[1:]], axis=0)
    g_wg1, g_lora = _exchange_now("gather", [bf("ffn1_wg"), lora], "gather_weights")
    gb = {"ffn1_wg": g_wg1}
    gs = {"gla_w_a2": g_lora[:, :GLA_LORA, :GLA_QK // NDEV]}
    row = GLA_LORA
    for n in SMALL_SHARDED[1:]:
        gs[n] = g_lora[:, row:row + wts[n].shape[1]]
        row += wts[n].shape[1]
    w = _layout_weights(gb, gs, {n: wts[n] for n in REPLICATED})
    blocks = {"wu1": bf("ffn1_wu"), "wd1": bf("ffn1_wd"), "win": bf("w_in"),
              "late": [jnp.stack([bf("ffn2_wg"), bf("ffn2_wu")]), jnp.concatenate([bf(n) for n, _ in LATE_ROWS], axis=0)]}

    loss_part, grad_x, grads, parts = _local_step(x[0], loss_target[0], w, blocks)
    small = _layout_grads(grads)
    result = {}
    state = lambda n: (wts[n], mom[n], var[n])
    for group, names in (("up1", ("ffn1_wg", "ffn1_wu")), ("up2", ("ffn2_wg", "ffn2_wu"))):
        for i, n in enumerate(names):
            result[n] = _adamw(parts[group], *state(n), 256, "adamw_" + n, stack_index=i, transposed=True)
    result["ffn1_wd"] = _adamw(parts["wd1"], *state("ffn1_wd"), 64, "adamw_ffn1_wd")
    row = 0
    for n, rows in LATE_ROWS:
        result[n] = _adamw(parts["down2"], *state(n), 64, "adamw_" + n, row_block_offset=row // 64)
        row += rows
    result["w_in"] = _adamw(parts["win"], *state("w_in"), 256, "adamw_w_in")

    small_names = [n for n in WEIGHTS if n not in BIG]
    full_shapes = [small[n].shape for n in small_names]
    rows_full = _rows_for(full_shapes, extra=128)
    packed = _pack([small[n] for n in small_names] + [loss_part], rows_full)
    (gathered,) = _exchange_now("gather", [packed], "gather_small_grads")
    total = _sum_gathered(gathered)
    *full_grads, loss_row = _unpack(total, full_shapes + [(1, 128)])
    me = _slot(_position())
    own = {}
    for n, g in zip(small_names, full_grads):
        if n in SMALL_SHARDED:
            cols = wts[n].shape[-1]
            g = lax.dynamic_slice_in_dim(g, me * cols, cols, axis=1)
        own[n] = g.reshape(wts[n].shape)
    own_shapes = [wts[n].shape for n in small_names]
    rows_own = _rows_for(own_shapes)
    pk = lambda d: _pack([d[n] for n in small_names], rows_own)
    d_s, m_s, v_s = _adamw_small(pk(wts), pk(own), pk(mom), pk(var))
    for n, d, m, v in zip(small_names, _unpack(d_s, own_shapes), _unpack(m_s, own_shapes), _unpack(v_s, own_shapes)):
        result[n] = (own[n], d, m, v)

    shaped = lambda n, k: result[n][k].reshape(wts[n].shape)
    return (loss_row[0, 0], grad_x[None],
            *[shaped(n, 0) for n in WEIGHTS], *[shaped(n, 1) for n in WEIGHTS],
            *[shaped(n, 2) for n in WEIGHTS], *[shaped(n, 3) for n in WEIGHTS])
```

```python
import functools

import jax
import jax.numpy as jnp
from jax import lax
from jax.experimental import pallas as pl
from jax.experimental.pallas import tpu as pltpu

F32 = jnp.float32
BF16 = jnp.bfloat16
HI = lax.Precision.HIGHEST

NDEV = 8
D = 2048
DFF = 5632
FSH = DFF // NDEV
CHUNK = 64
GLA_HEADS, GLA_DK, GLA_DV = 4, 128, 256
GLA_QK, GLA_V, GLA_LORA, GLA_TAU = 512, 1024, 16, 16.0
RW_HEADS, RW_HD, RW_W = 16, 64, 1024
DECAY_LORA, AAA_LORA, GATE_LORA = 96, 96, 256
GN_EPS = 64e-5
NORM_EPS = 1e-6
GLA_IN = 2 * GLA_QK + 2 * GLA_V + GLA_LORA
RW_IN = 3 * RW_W + DECAY_LORA + AAA_LORA + GATE_LORA
D_IN = GLA_IN + RW_IN + 2 * D
DIN_SH = D_IN // NDEV
PG_W = 2 * D
PR_W = 3584
PA_W = 3584
PA_USED = 2 * GLA_QK + 2 * GLA_V + 128
DIN_P = PG_W + PR_W + PA_W
LORA_P = 128

ADAM_LR, ADAM_B1, ADAM_B2, ADAM_EPS, ADAM_WD, ADAM_STEP = 0.001, 0.9, 0.999, 1e-08, 0.01, 10

VMEM_LIMIT = 56 * 1024 * 1024
RW_TB = 32
RW_G = 32
RW_NP = 8


def _params(sem=None, vmem=VMEM_LIMIT):
    return pltpu.CompilerParams(dimension_semantics=sem, vmem_limit_bytes=vmem)


def _pair_mask():
    return lax.broadcasted_iota(jnp.int32, (RW_HD, 2 * RW_HD), 1) < RW_HD


def _pair_rowsum(p, mask):
    tot = jnp.sum(p, axis=1, keepdims=True)
    first = jnp.sum(jnp.where(mask, p, 0.0), axis=1, keepdims=True)
    return first, tot - first


def _split_transposed(x_ref, q, dst_ref, base):
    xt = x_ref[:, 128 * q:128 * (q + 1)].T
    for g in range(RW_TB // RW_G):
        dst_ref[base + g, :, 0:RW_G] = xt[:, g * RW_G:(g + 1) * RW_G]


def _pair_column(tile_ref, idx, i, mask):
    return jnp.where(mask, tile_ref[idx, 0:RW_HD, i:i + 1], tile_ref[idx, RW_HD:, i:i + 1])


def _rw_core_fwd(rw, w, k2, kk, b, gather=()):
    T = rw.shape[0]
    nb = T // RW_TB
    ng = RW_TB // RW_G
    NP = RW_NP
    nc = len(gather)
    npair = RW_HEADS // 2 // NP

    def body(r_ref, v_ref, w_ref, k_ref, kk_ref, b_ref, *rest):
        g_in, (y_ref, st_ref, sa_ref), g_out = rest[:nc], rest[nc:nc + 3], rest[nc + 3:2 * nc + 3]
        s_scr, vt_scr, yt_scr, rows_scr = rest[2 * nc + 3:2 * nc + 7]
        pair, blk_i = pl.program_id(0), pl.program_id(1)
        if nc:
            start, forward, finish = _gather_plan(g_in, g_out, *rest[2 * nc + 7:])
            pl.when((pair == 0) & (blk_i == 0))(start)
            pl.when((pair == 0) & (blk_i == nb // 2))(forward)

        @pl.when(pl.program_id(1) == 0)
        def _():
            s_scr[...] = jnp.zeros_like(s_scr)
            yt_scr[...] = jnp.zeros_like(yt_scr)

        mask = _pair_mask()
        for q in range(NP):
            _split_transposed(v_ref, q, vt_scr, q * ng)
        R_, W_, K_, KK_, B_ = range(5)
        for a, ref in enumerate((r_ref, w_ref, k_ref, kk_ref, b_ref)):
            for q in range(NP):
                rows_scr[a * NP + q] = ref[:, 128 * q:128 * (q + 1)]

        def group(g, states):
            states = list(states)
            for i in range(RW_G):
                t = g * RW_G + i
                row = lambda a, q: rows_scr[a * NP + q, pl.ds(t, 1), :]
                sums = [_pair_rowsum(states[q] * row(KK_, q), mask) for q in range(NP)]
                for q in range(NP):
                    sa = jnp.where(mask, *sums[q])
                    sa_ref[q, t] = sa
                    states[q] = (states[q] * row(W_, q) - sa * row(B_, q)
                                 + _pair_column(vt_scr, q * ng + g, i, mask) * row(K_, q))
                    st_ref[q, t] = states[q]
                outs = [_pair_rowsum(states[q] * row(R_, q), mask) for q in range(NP)]
                for q in range(NP):
                    yt_scr[q * ng + g, 0:RW_HD, i:i + 1] = outs[q][0]
                    yt_scr[q * ng + g, RW_HD:, i:i + 1] = outs[q][1]
            return tuple(states)

        states = lax.fori_loop(0, ng, group, tuple(s_scr[q] for q in range(NP)))
        for q in range(NP):
            s_scr[q] = states[q]
            for g in range(ng):
                y_ref[g * RW_G:(g + 1) * RW_G, 128 * q:128 * (q + 1)] = yt_scr[q * ng + g].T[0:RW_G, :]
        if nc:
            pl.when((pair == npair - 1) & (blk_i == nb - 1))(finish)

    blk = lambda cb: pl.BlockSpec((RW_TB, 128 * NP), lambda p, i, cb=cb: (i, cb + p))
    tiles = pltpu.VMEM((NP * ng, 128, 128), F32)
    return pl.pallas_call(
        body, name="rw_core_fwd", grid=(npair, nb),
        in_specs=[blk(0), blk(2 * RW_W // (128 * NP)), blk(0), blk(0), blk(0), blk(0)] + [_ANY] * nc,
        out_specs=[blk(0)] + [pl.BlockSpec((NP, RW_TB, RW_HD, 128), lambda p, i: (p, i, 0, 0))] * 2 + [_ANY] * nc,
        out_shape=[jax.ShapeDtypeStruct((T, RW_W), F32)] + [jax.ShapeDtypeStruct((RW_HEADS // 2, T, RW_HD, 128), F32)] * 2
        + [jax.ShapeDtypeStruct((NDEV,) + a.shape, a.dtype) for a in gather],
        scratch_shapes=[pltpu.VMEM((NP, RW_HD, 128), F32), tiles, tiles, pltpu.VMEM((5 * NP, RW_TB, 128), F32)]
        + _comm_sems(nc),
        compiler_params=_params(("arbitrary", "arbitrary")),
    )(rw, rw, w, k2, kk, b, *gather)


def _rw_core_bwd(rw, w, k2, kk, b, states, sa_tiles, dy, exchange=()):
    T = rw.shape[0]
    nb = T // RW_TB
    ng = RW_TB // RW_G
    NP = RW_NP
    nc = len(exchange)
    npair = RW_HEADS // 2 // NP

    def body(r_ref, v_ref, w_ref, k_ref, kk_ref, b_ref, dy_ref, st_ref, sp_ref, sa_ref, *rest):
        e_in, e_out = rest[:nc], rest[nc + 6:2 * nc + 6]
        dr_ref, dw_ref, dk_ref, dv_ref, dkk_ref, db_ref = rest[nc:nc + 6]
        ds_scr, vt_scr, dyt_scr, dvt_scr, rows_scr, out_scr = rest[2 * nc + 6:2 * nc + 12]
        step = pl.program_id(1)
        if nc:
            start, finish = _exchange_plan(e_in, e_out, *rest[2 * nc + 12:])
            pl.when((pl.program_id(0) == 0) & (step == 0))(start)

        @pl.when(step == 0)
        def _():
            ds_scr[...] = jnp.zeros_like(ds_scr)
            dvt_scr[...] = jnp.zeros_like(dvt_scr)

        mask = _pair_mask()
        for q in range(NP):
            _split_transposed(v_ref, q, vt_scr, q * ng)
            _split_transposed(dy_ref, q, dyt_scr, q * ng)
        R_, W_, K_, KK_, B_ = range(5)
        for a, ref in enumerate((r_ref, w_ref, k_ref, kk_ref, b_ref)):
            for q in range(NP):
                rows_scr[a * NP + q] = ref[:, 128 * q:128 * (q + 1)]

        def group(gg, grads):
            g = ng - 1 - gg
            grads = list(grads)
            pairs = range(NP)
            for i in reversed(range(RW_G)):
                t = g * RW_G + i
                row = lambda a, q: rows_scr[a * NP + q, pl.ds(t, 1), :]

                def put(a, q, value):
                    out_scr[a * NP + q, pl.ds(t, 1), :] = value

                s_old = [st_ref[q, jnp.maximum(t - 1, 0)] for q in pairs]
                if i == 0:
                    s_old = [jnp.where(g == 0, jnp.where(step == nb - 1, 0.0, sp_ref[q, 0]), s_old[q]) for q in pairs]
                dycol = [_pair_column(dyt_scr, q * ng + g, i, mask) for q in pairs]
                dS = [grads[q] + dycol[q] * row(R_, q) for q in pairs]
                m = [_pair_rowsum(dS[q] * row(B_, q), mask) for q in pairs]
                dv = [_pair_rowsum(dS[q] * row(K_, q), mask) for q in pairs]
                for q in pairs:
                    put(R_, q, jnp.sum(st_ref[q, t] * dycol[q], axis=0, keepdims=True))
                    put(W_, q, jnp.sum(dS[q] * s_old[q], axis=0, keepdims=True))
                    put(K_, q, jnp.sum(dS[q] * _pair_column(vt_scr, q * ng + g, i, mask), axis=0, keepdims=True))
                for q in pairs:
                    dsa = -jnp.where(mask, *m[q])
                    grads[q] = dS[q] * row(W_, q) + dsa * row(KK_, q)
                    put(KK_, q, jnp.sum(s_old[q] * dsa, axis=0, keepdims=True))
                    put(B_, q, -jnp.sum(dS[q] * sa_ref[q, t], axis=0, keepdims=True))
                    dvt_scr[q * ng + g, 0:RW_HD, i:i + 1] = dv[q][0]
                    dvt_scr[q * ng + g, RW_HD:, i:i + 1] = dv[q][1]
            return tuple(grads)

        grads = lax.fori_loop(0, ng, group, tuple(ds_scr[q] for q in range(NP)))
        for q in range(NP):
            ds_scr[q] = grads[q]
            for a, ref in enumerate((dr_ref, dw_ref, dk_ref, dkk_ref, db_ref)):
                ref[:, 128 * q:128 * (q + 1)] = out_scr[a * NP + q]
            for g in range(ng):
                dv_ref[g * RW_G:(g + 1) * RW_G, 128 * q:128 * (q + 1)] = dvt_scr[q * ng + g].T[0:RW_G, :]
        if nc:
            pl.when((pl.program_id(0) == npair - 1) & (step == nb - 1))(finish)

    blk = lambda cb: pl.BlockSpec((RW_TB, 128 * NP), lambda p, i, cb=cb: (nb - 1 - i, cb + p))
    st_spec = pl.BlockSpec((NP, RW_TB, RW_HD, 128), lambda p, i: (p, nb - 1 - i, 0, 0))
    sp_spec = pl.BlockSpec((NP, 1, RW_HD, 128), lambda p, i: (p, jnp.maximum((nb - 1 - i) * RW_TB - 1, 0), 0, 0))
    out = jax.ShapeDtypeStruct((T, RW_W), F32)
    tiles = pltpu.VMEM((NP * ng, 128, 128), F32)
    return pl.pallas_call(
        body, name="rw_core_bwd", grid=(npair, nb),
        in_specs=[blk(0), blk(2 * RW_W // (128 * NP)), blk(0), blk(0), blk(0), blk(0), blk(0), st_spec, sp_spec, st_spec]
        + [_ANY] * nc,
        out_specs=[blk(0)] * 6 + [_ANY] * nc,
        out_shape=[out] * 6 + [jax.ShapeDtypeStruct(a.shape, a.dtype) for a in exchange],
        scratch_shapes=[pltpu.VMEM((NP, RW_HD, 128), F32), tiles, tiles, tiles,
                        pltpu.VMEM((5 * NP, RW_TB, 128), F32), pltpu.VMEM((5 * NP, RW_TB, 128), F32)] + _comm_sems(nc),
        compiler_params=_params(("arbitrary", "arbitrary")),
    )(rw, rw, w, k2, kk, b, dy, states, states, sa_tiles, *exchange)


GLA_CB = 8


def _gla_chunk(s_t, q, k, v, la, ltri):
    cum = jnp.dot(ltri, la, precision=HI, preferred_element_type=F32)
    total = jnp.sum(la, axis=0, keepdims=True)
    kdec = k * jnp.exp(total - cum)
    u_t = _bdot(v, kdec, _TN)
    s_t = jnp.exp(total) * s_t + u_t
    o = _bdot(q * (GLA_DK ** -0.5), s_t, _NT)
    return s_t, o


def _gla_core_fwd(pa, la, ltri):
    T = pa.shape[0]
    cb = min(GLA_CB, T // CHUNK)
    rows = cb * CHUNK
    nsteps = T // rows

    def body(q_ref, k_ref, v_ref, la_ref, ltri_ref, o_ref, st_ref, s_scr):
        @pl.when(pl.program_id(0) == 0)
        def _():
            s_scr[...] = jnp.zeros_like(s_scr)

        def chunk(c, states):
            sl = pl.ds(pl.multiple_of(c * CHUNK, CHUNK), CHUNK)
            out = []
            for h in range(GLA_HEADS):
                qk, vv = slice(GLA_DK * h, GLA_DK * (h + 1)), slice(GLA_DV * h, GLA_DV * (h + 1))
                s_t, o = _gla_chunk(states[h], q_ref[sl, qk], k_ref[sl, qk], v_ref[sl, vv], la_ref[sl, qk], ltri_ref[...])
                o_ref[sl, vv] = o
                st_ref[h, c] = s_t
                out.append(s_t)
            return tuple(out)

        states = lax.fori_loop(0, cb, chunk, tuple(s_scr[h] for h in range(GLA_HEADS)))
        for h in range(GLA_HEADS):
            s_scr[h] = states[h]

    qk = lambda cb_: pl.BlockSpec((rows, GLA_QK), lambda i, cb_=cb_: (i, cb_))
    return pl.pallas_call(
        body, name="gla_core_fwd", grid=(nsteps,),
        in_specs=[qk(0), qk(1), pl.BlockSpec((rows, GLA_V), lambda i: (i, 1)), qk(0),
                  pl.BlockSpec((CHUNK, CHUNK), lambda i: (0, 0))],
        out_specs=[pl.BlockSpec((rows, GLA_V), lambda i: (i, 0)),
                   pl.BlockSpec((GLA_HEADS, cb, GLA_DV, GLA_DK), lambda i: (0, i, 0, 0))],
        out_shape=(jax.ShapeDtypeStruct((T, GLA_V), F32),
                   jax.ShapeDtypeStruct((GLA_HEADS, T // CHUNK, GLA_DV, GLA_DK), F32)),
        scratch_shapes=[pltpu.VMEM((GLA_HEADS, GLA_DV, GLA_DK), F32)],
        compiler_params=_params(("arbitrary",)),
    )(pa, pa, pa, la, ltri)


def _gla_core_bwd(pa, la, ltri, states, do):
    T = pa.shape[0]
    cb = min(GLA_CB, T // CHUNK)
    rows = cb * CHUNK
    nsteps = T // rows

    def body(q_ref, k_ref, v_ref, la_ref, ltri_ref, st_ref, sp_ref, do_ref,
             dq_ref, dk_ref, dv_ref, dla_ref, ds_scr):
        step = pl.program_id(0)

        @pl.when(step == 0)
        def _():
            ds_scr[...] = jnp.zeros_like(ds_scr)

        def chunk(cc, grads):
            c = cb - 1 - cc
            sl = pl.ds(pl.multiple_of(c * CHUNK, CHUNK), CHUNK)
            out = []
            for h in range(GLA_HEADS):
                qk, vv = slice(GLA_DK * h, GLA_DK * (h + 1)), slice(GLA_DV * h, GLA_DV * (h + 1))
                s_before = jnp.where(step == nsteps - 1, 0.0, sp_ref[h, 0])
                s_prev = jnp.where(c == 0, s_before, st_ref[h, jnp.maximum(c - 1, 0)])
                _, vjp = jax.vjp(functools.partial(_gla_chunk, ltri=ltri_ref[...]),
                                 s_prev, q_ref[sl, qk], k_ref[sl, qk], v_ref[sl, vv], la_ref[sl, qk])
                ds_prev, dq, dk, dv, dla = vjp((grads[h], do_ref[sl, vv]))
                dq_ref[sl, qk] = dq
                dk_ref[sl, qk] = dk
                dv_ref[sl, vv] = dv
                dla_ref[sl, qk] = dla
                out.append(ds_prev)
            return tuple(out)

        grads = lax.fori_loop(0, cb, chunk, tuple(ds_scr[h] for h in range(GLA_HEADS)))
        for h in range(GLA_HEADS):
            ds_scr[h] = grads[h]

    r = lambda i: nsteps - 1 - i
    qk = lambda cb_: pl.BlockSpec((rows, GLA_QK), lambda i, cb_=cb_: (r(i), cb_))
    o512 = pl.BlockSpec((rows, GLA_QK), lambda i: (r(i), 0))
    o1024 = pl.BlockSpec((rows, GLA_V), lambda i: (r(i), 0))
    return pl.pallas_call(
        body, name="gla_core_bwd", grid=(nsteps,),
        in_specs=[qk(0), qk(1), pl.BlockSpec((rows, GLA_V), lambda i: (r(i), 1)), qk(0),
                  pl.BlockSpec((CHUNK, CHUNK), lambda i: (0, 0)),
                  pl.BlockSpec((GLA_HEADS, cb, GLA_DV, GLA_DK), lambda i: (0, r(i), 0, 0)),
                  pl.BlockSpec((GLA_HEADS, 1, GLA_DV, GLA_DK), lambda i: (0, jnp.maximum(r(i) * cb - 1, 0), 0, 0)),
                  o1024],
        out_specs=[o512, o512, o1024, o512],
        out_shape=(jax.ShapeDtypeStruct((T, GLA_QK), F32), jax.ShapeDtypeStruct((T, GLA_QK), F32),
                   jax.ShapeDtypeStruct((T, GLA_V), F32), jax.ShapeDtypeStruct((T, GLA_QK), F32)),
        scratch_shapes=[pltpu.VMEM((GLA_HEADS, GLA_DV, GLA_DK), F32)],
        compiler_params=_params(("arbitrary",)),
    )(pa, pa, pa, la, ltri, states, states, do)


def _rowwise(fn, name, T, tm, rows, pars, row_outs, acc_outs):
    nr, npar, nro = len(rows), len(pars), len(row_outs)
    tm = min(tm, T)
    nsteps = T // tm

    def body(*refs):
        i = pl.program_id(0)
        ins = [r[...] for r in refs[:nr + npar]]
        outs, accs = fn(i, *ins)
        for r, o in zip(refs[nr + npar:nr + npar + nro], outs):
            r[...] = o.astype(r.dtype)
        for r, a in zip(refs[nr + npar + nro:], accs):
            @pl.when(i == 0)
            def _(r=r, a=a):
                r[...] = a

            @pl.when(i > 0)
            def _(r=r, a=a):
                r[...] += a

    def rspec(width, cb, kind):
        if kind == "cur":
            return pl.BlockSpec((tm, width), lambda i: (i, cb))
        if kind == "prev":
            return pl.BlockSpec((8, width), lambda i: (jnp.maximum(i * (tm // 8) - 1, 0), cb))
        return pl.BlockSpec((8, width), lambda i: (jnp.minimum((i + 1) * (tm // 8), T // 8 - 1), cb))

    in_specs = [rspec(w, cb, kind) for (_, w, cb, kind) in rows]
    in_specs += [pl.BlockSpec(p.shape, lambda i, nd=p.ndim: (0,) * nd) for p in pars]
    out_specs = [pl.BlockSpec((tm, w), lambda i: (i, 0)) for (w, _) in row_outs]
    out_specs += [pl.BlockSpec(s, lambda i, nd=len(s): (0,) * nd) for s in acc_outs]
    out_shape = [jax.ShapeDtypeStruct((T, w), dt) for (w, dt) in row_outs]
    out_shape += [jax.ShapeDtypeStruct(s, F32) for s in acc_outs]
    res = pl.pallas_call(
        body, name=name, grid=(nsteps,), in_specs=in_specs, out_specs=out_specs, out_shape=out_shape,
        compiler_params=_params(("arbitrary",)),
    )(*[r[0] for r in rows], *pars)
    return res


def _cur(a, width=None, cb=0):
    return (a, a.shape[1] if width is None else width, cb, "cur")


def _sigmoid(x):
    return 1.0 / (1.0 + jnp.exp(-x))


def _silu(x):
    return x * _sigmoid(x)


def _softplus(x):
    return jnp.maximum(x, 0.0) + jnp.log(1.0 + jnp.exp(-jnp.abs(x)))


def _rms(x, g):
    return x * lax.rsqrt(jnp.mean(x * x, axis=-1, keepdims=True) + NORM_EPS) * g


def _dot_hi(a, b):
    return jnp.dot(a, b, precision=lax.Precision.HIGH, preferred_element_type=F32)


def _rms_fwd(x, g, name):
    T = x.shape[0]
    fn = lambda i, xb, gb: ((_rms(xb, gb),), ())
    return _rowwise(fn, name, T, 256, [_cur(x)], [g], [(D, BF16)], [])[0]


def _rms_bwd(x, g, dh, dres, name):
    T = x.shape[0]

    def fn(i, xb, dhb, drb, gb):
        _, vjp = jax.vjp(_rms, xb, gb)
        dx, dg = vjp(dhb)
        return (drb + dx,), (dg,)

    return _rowwise(fn, name, T, 256, [_cur(x), _cur(dh), _cur(dres)], [g], [(D, F32)], [(1, D)])


def _loss_bwd(x, target, g):
    T = x.shape[0]

    def loss(xb, gb, tb):
        err = _rms(xb, gb) - tb
        return 0.5 * jnp.sum(jnp.mean(err * err, axis=-1, keepdims=True))

    def fn(i, xb, tb, gb):
        val, (dx, dg) = jax.value_and_grad(loss, argnums=(0, 1))(xb, gb, tb)
        return (dx,), (jnp.full((1, 128), val, F32), dg)

    return _rowwise(fn, "loss_bwd", T, 256, [_cur(x), _cur(target)], [g], [(D, F32)], [(1, 128), (1, D)])


def _gla_la(a_down, w_a2, b_a):
    return -_softplus(-(_bdot(a_down, w_a2, _NN) + b_a)) * (1.0 / GLA_TAU)


def _gla_prep(pa, w_a2, b_a):
    T = pa.shape[0]
    fn = lambda i, ab, wb, bb: ((_gla_la(ab, wb, bb),), ())
    return _rowwise(fn, "gla_prep", T, 512, [_cur(pa, LORA_P, (2 * GLA_QK + 2 * GLA_V) // LORA_P)], [w_a2, b_a],
                    [(GLA_QK, F32)], [])[0]


def _gla_prep_bwd(pa, w_a2, b_a, dla):
    T = pa.shape[0]

    def fn(i, ab, dlab, wb, bb):
        _, vjp = jax.vjp(_gla_la, ab, wb, bb)
        da, dw, db = vjp(dlab)
        return (da,), (dw, db)

    return _rowwise(fn, "gla_prep_bwd", T, 512, [_cur(pa, LORA_P, (2 * GLA_QK + 2 * GLA_V) // LORA_P), _cur(dla)],
                    [w_a2, b_a], [(LORA_P, BF16)], [(LORA_P, GLA_QK), (1, GLA_QK)])


def _gla_out(o, r, gn, ind, ind_t):
    ms = _dot_hi(_dot_hi(o * o, ind) * (1.0 / GLA_DV), ind_t)
    return o * lax.rsqrt(ms + NORM_EPS) * gn * _silu(r)


def _gla_post(o_raw, pa, gn, ind, ind_t):
    T = pa.shape[0]
    fn = lambda i, ob, rb, gb, a, b: ((_gla_out(ob, rb, gb, a, b),), ())
    return _rowwise(fn, "gla_post", T, 256, [_cur(o_raw), _cur(pa, GLA_V, 2)], [gn, ind, ind_t], [(GLA_V, BF16)], [])[0]


def _gla_post_bwd(o_raw, pa, gn, ind, ind_t, do):
    T = pa.shape[0]

    def fn(i, ob, rb, dob, gb, a, b):
        _, vjp = jax.vjp(lambda o, r, g: _gla_out(o, r, g, a, b), ob, rb, gb)
        d_o, d_r, d_g = vjp(dob)
        return (d_o, d_r), (d_g,)

    return _rowwise(fn, "gla_post_bwd", T, 256, [_cur(o_raw), _cur(pa, GLA_V, 2), _cur(do)], [gn, ind, ind_t],
                    [(GLA_V, F32), (GLA_V, BF16)], [(1, GLA_V)])


def _shift_rows(cur, prev8, i):
    first = jnp.where(i == 0, 0.0, prev8[7:8, :])
    rolled = pltpu.roll(cur, 1, 0)
    return jnp.where(lax.broadcasted_iota(jnp.int32, cur.shape, 0) == 0, first, rolled)


def _rw_gates(rw, w0, w_w2, a0, w_a2, w_g2, k_k, k_a, ind, ind_t):
    rk = rw[:, RW_W:2 * RW_W]
    wd = rw[:, 3 * RW_W:3 * RW_W + LORA_P]
    ad = rw[:, 3 * RW_W + LORA_P:3 * RW_W + 2 * LORA_P]
    gd = rw[:, 3 * RW_W + 2 * LORA_P:]
    w_raw = w0 + _bdot(jnp.tanh(wd), w_w2, _NN)
    w = jnp.exp(-jnp.exp(-_softplus(-w_raw) - 0.5))
    a = _sigmoid(a0 + _bdot(ad, w_a2, _NN))
    g = _bdot(_sigmoid(gd), w_g2, _NN)
    kk = rk * k_k
    kk = kk * _dot_hi(lax.rsqrt(jnp.maximum(_dot_hi(kk * kk, ind), 1e-24)), ind_t)
    k2 = rk * (1.0 + (a - 1.0) * k_a)
    return w, k2, kk, kk * a, g


def _rw_prep(pr, mu, gate_pars):
    T = pr.shape[0]

    def fn(i, cur, prev8, mub, *gp):
        rw = cur + mub * (_shift_rows(cur, prev8, i) - cur)
        return (rw,) + _rw_gates(rw, *gp), ()

    return _rowwise(fn, "rw_prep", T, 256, [_cur(pr), (pr, PR_W, 0, "prev")], [mu, *gate_pars],
                    [(PR_W, F32)] + [(RW_W, F32)] * 5, [])


def _rw_prep_bwd(pr, mu, gate_pars, d_r, d_v, d_w, d_k2, d_kk, d_b, d_g):
    T = pr.shape[0]
    rows = [_cur(pr), (pr, PR_W, 0, "prev")] + [_cur(x) for x in (*d_r, *d_v, d_w, *d_k2, d_kk, d_b, d_g)]
    acc = [(1, PR_W)] + [tuple(p.shape) for p in gate_pars[:-2]]

    def fn(i, cur, prev8, dr1, dr2, dv1, dv2, dw, dk1, dk2, dkk, db, dg, mub, *gp):
        sh = _shift_rows(cur, prev8, i)
        rw = cur + mub * (sh - cur)
        _, vjp = jax.vjp(lambda x, *p: _rw_gates(x, *p, gp[-2], gp[-1]), rw, *gp[:-2])
        grads = vjp((dw, dk1 + dk2, dkk, db, dg))
        zeros = jnp.zeros((cur.shape[0], PR_W - 3 * RW_W), F32)
        drw = grads[0] + jnp.concatenate([dr1 + dr2, jnp.zeros_like(dr1), dv1 + dv2, zeros], axis=1)
        dmu = jnp.sum(drw * (sh - cur), axis=0, keepdims=True)
        return (drw,), (dmu, *grads[1:])

    return _rowwise(fn, "rw_prep_bwd", T, 128, rows, [mu, *gate_pars], [(PR_W, F32)], acc)


def _shift_bwd(drw, mu):
    T = drw.shape[0]
    tm = min(256, T)

    def fn(i, cur, next8, mub):
        last = jnp.where(i == T // tm - 1, 0.0, next8[0:1, :])
        rolled = pltpu.roll(cur, cur.shape[0] - 1, 0)
        nxt = jnp.where(lax.broadcasted_iota(jnp.int32, cur.shape, 0) == cur.shape[0] - 1, last, rolled)
        return ((1.0 - mub) * cur + mub * nxt,), ()

    return _rowwise(fn, "shift_bwd", T, tm, [_cur(drw), (drw, PR_W, 0, "next")], [mu], [(PR_W, BF16)], [])[0]


def _rw_out(y, r, v, k2, g, lnx_w, lnx_b, r_k, ind, ind_t):
    mean = _dot_hi(_dot_hi(y, ind) * (1.0 / RW_HD), ind_t)
    yc = y - mean
    var = _dot_hi(_dot_hi(yc * yc, ind) * (1.0 / RW_HD), ind_t)
    yn = yc * lax.rsqrt(var + GN_EPS) * lnx_w + lnx_b
    bonus = _dot_hi(_dot_hi(r * k2 * r_k, ind), ind_t) * v
    return (yn + bonus) * g


def _rw_post(y, rw, k2, g, pars):
    T = y.shape[0]
    fn = lambda i, *a: ((_rw_out(*a),), ())
    return _rowwise(fn, "rw_post", T, 256, [_cur(y), _cur(rw, RW_W, 0), _cur(rw, RW_W, 2), _cur(k2), _cur(g)], pars,
                    [(RW_W, BF16)], [])[0]


def _rw_post_bwd(y, rw, k2, g, pars, do):
    T = y.shape[0]

    def fn(i, yb, rb, vb, kb, gb, dob, lw, lb, rk, ind, ind_t):
        _, vjp = jax.vjp(lambda *a: _rw_out(*a, ind, ind_t), yb, rb, vb, kb, gb, lw, lb, rk)
        gr = vjp(dob)
        return gr[:5], gr[5:]

    return _rowwise(fn, "rw_post_bwd", T, 256,
                    [_cur(y), _cur(rw, RW_W, 0), _cur(rw, RW_W, 2), _cur(k2), _cur(g), _cur(do)], pars,
                    [(RW_W, F32)] * 5, [(1, RW_W)] * 3)


def _merge_bwd(dm, y_gla, y_rw, pg, gate_b):
    T = dm.shape[0]

    def fn(i, dmb, ya, yr, p1, p2, gb):
        g1 = _sigmoid(p1 + gb[:, :D])
        g2 = _sigmoid(p2 + gb[:, D:])
        dp1 = dmb * ya * g1 * (1.0 - g1)
        dp2 = dmb * yr * g2 * (1.0 - g2)
        dp = jnp.concatenate([dp1, dp2], axis=1)
        return (dmb * g1, dmb * g2, dp), (jnp.sum(dp, axis=0, keepdims=True),)

    return _rowwise(fn, "merge_bwd", T, 256, [_cur(dm), _cur(y_gla), _cur(y_rw), _cur(pg, D, 0), _cur(pg, D, 1)],
                    [gate_b], [(D, BF16), (D, BF16), (PG_W, BF16)], [(1, PG_W)])


_NN = (((1,), (0,)), ((), ()))
_NT = (((1,), (1,)), ((), ()))
_TN = (((0,), (0,)), ((), ()))


def _bdot(a, b, dims):
    return lax.dot_general(a.astype(BF16), b.astype(BF16), dims, preferred_element_type=F32)


def _accumulate(k, nk, acc, part, finish):
    if nk == 1:
        finish(part)
        return

    @pl.when(k == 0)
    def _():
        acc[...] = part

    @pl.when(k > 0)
    def _():
        acc[...] += part

    @pl.when(k == nk - 1)
    def _():
        finish(acc[...])


def _call(body, comm, name, grid, in_specs, out_specs, out_shape, scratch_shapes, sem, operands):
    if comm is None:
        return pl.pallas_call(body, name=name, grid=grid, in_specs=in_specs, out_specs=out_specs, out_shape=out_shape,
                              scratch_shapes=scratch_shapes, compiler_params=_params(sem))(*operands)
    kind, arrays = comm
    nc, n_in, n_out, n_scr = len(arrays), len(in_specs), len(out_shape), len(scratch_shapes)
    total = 1
    for n in grid:
        total *= n

    def with_comm(*refs):
        own = refs[:n_in] + refs[n_in + nc:n_in + nc + n_out] + refs[n_in + 2 * nc + n_out:n_in + 2 * nc + n_out + n_scr]
        c_in, c_out, sems = refs[n_in:n_in + nc], refs[n_in + nc + n_out:n_in + 2 * nc + n_out], refs[-3:]
        step = 0
        for axis, n in enumerate(grid):
            step = step * n + pl.program_id(axis)
        start, *forward, finish = _PLANS[kind][0](c_in, c_out, *sems)
        pl.when(step == 0)(start)
        for stage in forward:
            pl.when(step == (total - 1 if kind == "gather_late" else total // 2))(stage)
        body(*own)
        pl.when(step == total - 1)(finish)

    return pl.pallas_call(
        with_comm, name=name, grid=grid, in_specs=list(in_specs) + [_ANY] * nc, out_specs=list(out_specs) + [_ANY] * nc,
        out_shape=list(out_shape) + [jax.ShapeDtypeStruct(_PLANS[kind][1](a.shape), a.dtype) for a in arrays],
        scratch_shapes=list(scratch_shapes) + _comm_sems(nc), compiler_params=_params(("arbitrary",) * len(grid)),
    )(*operands, *arrays)


def _matmul(a, b, mode, M, N, K, tm, tn, tk, name, a_off=(0, 0), b_off=(0, 0), res=None, scale=1.0, out_dtype=F32,
            comm=None):
    tm, tn, tk = min(tm, M), min(tn, N), min(tk, K)
    nk = K // tk
    if mode == "nn":
        a_spec = pl.BlockSpec((tm, tk), lambda i, j, k: (i + a_off[0], k + a_off[1]))
        b_spec = pl.BlockSpec((tk, tn), lambda i, j, k: (k + b_off[0], j + b_off[1]))
        dims = _NN
    elif mode == "nt":
        a_spec = pl.BlockSpec((tm, tk), lambda i, j, k: (i + a_off[0], k + a_off[1]))
        b_spec = pl.BlockSpec((tn, tk), lambda i, j, k: (j + b_off[0], k + b_off[1]))
        dims = _NT
    else:
        a_spec = pl.BlockSpec((tk, tm), lambda i, j, k: (k + a_off[0], i + a_off[1]))
        b_spec = pl.BlockSpec((tk, tn), lambda i, j, k: (k + b_off[0], j + b_off[1]))
        dims = _TN
    o_spec = pl.BlockSpec((tm, tn), lambda i, j, k: (i, j))

    def body(a_ref, b_ref, *rest):
        r_ref = rest[0] if res is not None else None
        o_ref = rest[1] if res is not None else rest[0]
        acc = rest[-1] if nk > 1 else None

        def finish(total):
            total = total * scale if scale != 1.0 else total
            if r_ref is not None:
                total = r_ref[...] + total
            o_ref[...] = total.astype(out_dtype)

        _accumulate(pl.program_id(2), nk, acc, _bdot(a_ref[...], b_ref[...], dims), finish)

    out = _call(body, comm, name, (M // tm, N // tn, nk), [a_spec, b_spec] + ([o_spec] if res is not None else []),
                [o_spec], [jax.ShapeDtypeStruct((M, N), out_dtype)], [pltpu.VMEM((tm, tn), F32)] if nk > 1 else [],
                ("parallel", "parallel", "arbitrary"), [a, b] + ([res] if res is not None else []))
    return out[0] if comm is None else out


def _ffn_up(h, wg, wu, name, comm=None):
    T = h.shape[0]
    tm = min(1024, T)

    def body(h_ref, wg_ref, wu_ref, a_ref, u_ref, s_ref):
        hb = h_ref[...]
        a = _bdot(hb, wg_ref[...], _NN)
        u = _bdot(hb, wu_ref[...], _NN)
        a_ref[...] = a
        u_ref[...] = u
        s_ref[...] = (_silu(a) * u).astype(BF16)

    w_spec = pl.BlockSpec((None, D, FSH), lambda i, j: (j, 0, 0))
    o_spec = pl.BlockSpec((None, tm, FSH), lambda i, j: (j, i, 0))
    sh = lambda dt: jax.ShapeDtypeStruct((NDEV, T, FSH), dt)
    return _call(body, comm, name, (T // tm, NDEV), [pl.BlockSpec((tm, D), lambda i, j: (i, 0)), w_spec, w_spec],
                 [o_spec] * 3, [sh(F32), sh(F32), sh(BF16)], [], ("parallel", "arbitrary"), [h, wg, wu])


def _ffn_gate(h, wg, name, comm=None):
    T = h.shape[0]
    tm = min(1024, T)

    def body(h_ref, wg_ref, a_ref):
        a_ref[...] = _bdot(h_ref[...], wg_ref[...], _NN)

    return _call(body, comm, name, (T // tm, NDEV),
                 [pl.BlockSpec((tm, D), lambda i, j: (i, 0)), pl.BlockSpec((None, D, FSH), lambda i, j: (j, 0, 0))],
                 [pl.BlockSpec((None, tm, FSH), lambda i, j: (j, i, 0))], [jax.ShapeDtypeStruct((NDEV, T, FSH), F32)], [],
                 ("parallel", "arbitrary"), [h, wg])


def _ffn_up_after_gate(h, wu, a, name, comm=None):
    T = h.shape[0]
    tm = min(1024, T)

    def body(h_ref, wu_ref, a_ref, u_ref, s_ref):
        u = _bdot(h_ref[...], wu_ref[...], _NN)
        u_ref[...] = u
        s_ref[...] = (_silu(a_ref[...]) * u).astype(BF16)

    act = pl.BlockSpec((None, tm, FSH), lambda i, j: (j, i, 0))
    sh = lambda dt: jax.ShapeDtypeStruct((NDEV, T, FSH), dt)
    return _call(body, comm, name, (T // tm, NDEV),
                 [pl.BlockSpec((tm, D), lambda i, j: (i, 0)), pl.BlockSpec((None, D, FSH), lambda i, j: (j, 0, 0)), act],
                 [act, act], [sh(F32), sh(BF16)], [], ("parallel", "arbitrary"), [h, wu, a])


def _ffn_down(s, wd, x, name, comm=None):
    T = x.shape[0]
    tm, tn, sh = min(1024, T), 1024, 4

    def body(s_ref, wd_ref, x_ref, o_ref, acc):
        part = _bdot(s_ref[0], wd_ref[0], _NN)
        for q in range(1, sh):
            part = part + _bdot(s_ref[q], wd_ref[q], _NN)

        def finish(total):
            o_ref[...] = x_ref[...] + 0.5 * total

        _accumulate(pl.program_id(2), NDEV // sh, acc, part, finish)

    xo = pl.BlockSpec((tm, tn), lambda i, n, j: (i, n))
    out = _call(body, comm, name, (T // tm, D // tn, NDEV // sh),
                [pl.BlockSpec((sh, tm, FSH), lambda i, n, j: (j, i, 0)),
                 pl.BlockSpec((sh, FSH, tn), lambda i, n, j: (j, 0, n)), xo],
                [xo], [jax.ShapeDtypeStruct((T, D), F32)], [pltpu.VMEM((tm, tn), F32)],
                ("parallel", "parallel", "arbitrary"), [s, wd, x])
    return out[0] if comm is None else out


def _ffn_bwd_hidden(dx, wd, a, u, name):
    T = dx.shape[0]
    tm = min(1024, T)

    def body(dx_ref, wd_ref, a_ref, u_ref, da_ref, du_ref):
        ds = 0.5 * _bdot(dx_ref[...], wd_ref[...], _NT)
        av = a_ref[...]
        sg = _sigmoid(av)
        da_ref[...] = (ds * u_ref[...] * (sg * (1.0 + av * (1.0 - sg)))).astype(BF16)
        du_ref[...] = (ds * (av * sg)).astype(BF16)

    act = pl.BlockSpec((None, tm, FSH), lambda i, j: (j, i, 0))
    sh = jax.ShapeDtypeStruct((NDEV, T, FSH), BF16)
    return pl.pallas_call(
        body, name=name, grid=(T // tm, NDEV),
        in_specs=[pl.BlockSpec((tm, D), lambda i, j: (i, 0)), pl.BlockSpec((None, FSH, D), lambda i, j: (j, 0, 0)),
                  act, act],
        out_specs=[act, act], out_shape=(sh, sh),
        compiler_params=_params(("parallel", "arbitrary")),
    )(dx, wd, a, u)


def _ffn_bwd_input(da, du, wg, wu, name, comm=None):
    T = da.shape[1]
    tm, tn, sh = min(1024, T), 1024, 2

    def body(da_ref, du_ref, wg_ref, wu_ref, o_ref, acc):
        part = _bdot(da_ref[0], wg_ref[0], _NT) + _bdot(du_ref[0], wu_ref[0], _NT)
        for q in range(1, sh):
            part = part + _bdot(da_ref[q], wg_ref[q], _NT) + _bdot(du_ref[q], wu_ref[q], _NT)

        def finish(total):
            o_ref[...] = total

        _accumulate(pl.program_id(2), NDEV // sh, acc, part, finish)

    act = pl.BlockSpec((sh, tm, FSH), lambda i, n, j: (j, i, 0))
    wsp = pl.BlockSpec((sh, tn, FSH), lambda i, n, j: (j, n, 0))
    out = _call(body, comm, name, (T // tm, D // tn, NDEV // sh), [act, act, wsp, wsp],
                [pl.BlockSpec((tm, tn), lambda i, n, j: (i, n))], [jax.ShapeDtypeStruct((T, D), F32)],
                [pltpu.VMEM((tm, tn), F32)], ("parallel", "parallel", "arbitrary"), [da, du, wg, wu])
    return out[0] if comm is None else out


def _ffn_grad_up(h, da, du, name, comm=None, core_major=False):
    T = h.shape[0]
    tm, tk = 1024, min(4096, T)
    nk = T // tk

    def body(h_ref, da_ref, du_ref, o_ref, acc_a, acc_u):
        k = pl.program_id(2)
        hb = h_ref[...]
        for acc, ref, slot in ((acc_a, da_ref, 0), (acc_u, du_ref, 1)):
            def finish(total, slot=slot):
                o_ref[slot] = total.astype(BF16)

            _accumulate(k, nk, acc, _bdot(hb, ref[...], _TN), finish)

    act = pl.BlockSpec((None, tk, FSH), lambda j, i, t: (j, t, 0))
    if core_major:
        o_spec = pl.BlockSpec((None, None, 2, tm, FSH), lambda j, i, t: (j % 2, j // 2, 0, i, 0))
        o_shape = jax.ShapeDtypeStruct((2, NDEV // 2, 2, D, FSH), BF16)
    else:
        o_spec = pl.BlockSpec((None, 2, tm, FSH), lambda j, i, t: (j, 0, i, 0))
        o_shape = jax.ShapeDtypeStruct((NDEV, 2, D, FSH), BF16)
    out = _call(body, comm, name, (NDEV, D // tm, nk), [pl.BlockSpec((tk, tm), lambda j, i, t: (t, i)), act, act],
                [o_spec], [o_shape],
                [pltpu.VMEM((tm, FSH), F32), pltpu.VMEM((tm, FSH), F32)], ("parallel", "parallel", "arbitrary"), [h, da, du])
    return out[0] if comm is None else out


def _ffn_grad_down(s, dx, name):
    T = dx.shape[0]
    tn, tk = 1024, min(4096, T)
    nk = T // tk

    def body(s_ref, dx_ref, o_ref, acc):
        def finish(total):
            o_ref[...] = (0.5 * total).astype(BF16)

        _accumulate(pl.program_id(2), nk, acc, _bdot(s_ref[...], dx_ref[...], _TN), finish)

    return pl.pallas_call(
        body, name=name, grid=(NDEV, D // tn, nk),
        in_specs=[pl.BlockSpec((None, tk, FSH), lambda j, n, t: (j, t, 0)), pl.BlockSpec((tk, tn), lambda j, n, t: (t, n))],
        out_specs=pl.BlockSpec((None, FSH, tn), lambda j, n, t: (j, 0, n)),
        out_shape=jax.ShapeDtypeStruct((NDEV, DFF // NDEV, D), BF16),
        scratch_shapes=[pltpu.VMEM((FSH, tn), F32)],
        compiler_params=_params(("parallel", "parallel", "arbitrary")),
    )(s, dx)


def _branch_merge(o_gla, o_rw, wb, pg, gate_b):
    T = o_gla.shape[0]
    tm, tn = min(1024, T), 512

    def body(og_ref, or_ref, w1_ref, w2_ref, p1_ref, p2_ref, b1_ref, b2_ref, yg_ref, yr_ref, m_ref):
        yg = _bdot(og_ref[...], w1_ref[...], _NN)
        yr = _bdot(or_ref[...], w2_ref[...], _NN)
        yg_ref[...] = yg
        yr_ref[...] = yr
        m_ref[...] = (_sigmoid(p1_ref[...] + b1_ref[...]) * yg + _sigmoid(p2_ref[...] + b2_ref[...]) * yr).astype(BF16)

    nj = D // tn
    act = pl.BlockSpec((tm, GLA_V), lambda i, j: (i, 0))
    out = pl.BlockSpec((tm, tn), lambda i, j: (i, j))
    return pl.pallas_call(
        body, name="branch_merge", grid=(T // tm, nj),
        in_specs=[act, act, pl.BlockSpec((GLA_V, tn), lambda i, j: (0, j)), pl.BlockSpec((RW_W, tn), lambda i, j: (1, j)),
                  out, pl.BlockSpec((tm, tn), lambda i, j: (i, nj + j)),
                  pl.BlockSpec((1, tn), lambda i, j: (0, j)), pl.BlockSpec((1, tn), lambda i, j: (0, nj + j))],
        out_specs=[out, out, out],
        out_shape=(jax.ShapeDtypeStruct((T, D), F32), jax.ShapeDtypeStruct((T, D), F32), jax.ShapeDtypeStruct((T, D), BF16)),
        compiler_params=_params(("parallel", "arbitrary")),
    )(o_gla, o_rw, wb, wb, pg, pg, gate_b, gate_b)


def _head_indicator(width, heads):
    col = lax.broadcasted_iota(jnp.int32, (width, 128), 0) // (width // heads)
    ind = (col == lax.broadcasted_iota(jnp.int32, (width, 128), 1)).astype(F32)
    return ind, ind.T


def _ffn_fwd(x, g, wg, wu, wd, tag):
    h = _rms_fwd(x, g, "rms_" + tag)
    a, u, s = _ffn_up(h, wg, wu, "ffn_up_" + tag)
    return _ffn_down(s, wd, x, "ffn_down_" + tag), (h, a, u, s)


def _ffn_fwd_gathering(x, g, wg, wu_block, wd_block, next_block, tag):
    h = _rms_fwd(x, g, "rms_" + tag)
    a, wu = _ffn_gate(h, wg, "ffn_gate_" + tag, comm=("gather_late", [wu_block]))
    u, s, wd = _ffn_up_after_gate(h, wu, a, "ffn_up_" + tag, comm=("gather_late", [wd_block]))
    y, gathered = _ffn_down(s, wd, x, "ffn_down_" + tag, comm=("gather_late", [next_block]))
    return y, (h, a, u, s), wu, wd, gathered


def _ffn_bwd(dy, x, g, wg, wu, wd, saved, tag, exchange=False):
    h, a, u, s = saved
    dwd = _ffn_grad_down(s, dy, "ffn_grad_down_" + tag)
    da, du = _ffn_bwd_hidden(dy, wd, a, u, "ffn_bwd_hidden_" + tag)
    if exchange:
        dw_up, dwd = _ffn_grad_up(h, da, du, "ffn_grad_up_" + tag, comm=("exchange", [dwd]), core_major=True)
        dh, dw_up = _ffn_bwd_input(da, du, wg, wu, "ffn_bwd_input_" + tag, comm=("quad", [_chip_sum(dw_up, "up_" + tag)]))
    else:
        dw_up = _ffn_grad_up(h, da, du, "ffn_grad_up_" + tag)
        dh = _ffn_bwd_input(da, du, wg, wu, "ffn_bwd_input_" + tag)
    dx, dg = _rms_bwd(x, g, dh, dy, "rms_bwd_" + tag)
    return dx, dg, dw_up, dwd


def _local_step(x, target, w, blocks):
    T = x.shape[0]
    ind16, ind16_t = _head_indicator(RW_W, RW_HEADS)
    ind4, ind4_t = _head_indicator(GLA_V, GLA_HEADS)
    ltri = jnp.tril(jnp.ones((CHUNK, CHUNK), F32))
    gate_pars = [w["w0"], w["w_w2"], w["a0"], w["w_a2"], w["w_g2"], w["k_k"], w["k_a"], ind16, ind16_t]
    post_pars = [w["lnx_w"], w["lnx_b"], w["r_k"], ind16, ind16_t]

    x1, ffn1, wu1, wd1, g_proj = _ffn_fwd_gathering(x, w["g1"], w["wg1"], blocks["wu1"], blocks["wd1"], blocks["win"], "1")
    win = _align_proj(_unshard_cols(g_proj))
    h2 = _rms_fwd(x1, w["g2"], "rms_mix")
    proj = lambda n, off, name: _matmul(h2, win, "nn", T, n, D, 1024, 512, D, name, b_off=(0, off // 512))
    pg = proj(PG_W, 0, "proj_gate")
    pr = proj(PR_W, PG_W, "proj_rwkv")
    pa = proj(PA_W, PG_W + PR_W, "proj_gla")
    la = _gla_prep(pa, w["gla_w_a2"], w["gla_b_a"])
    o_raw, gla_states = _gla_core_fwd(pa, la, ltri)
    o_gla = _gla_post(o_raw, pa, w["gn"], ind4, ind4_t)
    rw, dec, k2, kk, b, g = _rw_prep(pr, w["mu"], gate_pars)
    y, rw_states, rw_sa, g_up2, g_down2 = _rw_core_fwd(rw, dec, k2, kk, b, gather=blocks["late"])
    w = {**w, **_late_weights(g_up2, g_down2)}
    o_rw = _rw_post(y, rw, k2, g, post_pars)
    y_gla, y_rw, merged = _branch_merge(o_gla, o_rw, w["wb"], pg, w["gate_b"])
    x2 = _matmul(merged, w["wo"], "nn", T, D, D, 1024, 1024, D, "out_proj", res=x1)
    x3, ffn2 = _ffn_fwd(x2, w["g3"], w["wg2"], w["wu2"], w["wd2"], "2")
    dx3, loss, d_gf = _loss_bwd(x3, target, w["gf"])

    grads = {"gf": d_gf}
    dx2, grads["g3"], grads["up2"], grads["wd2"] = _ffn_bwd(
        dx3, x2, w["g3"], w["wg2"], w["wu2"], w["wd2"], ffn2, "2")
    dm = _matmul(dx2, w["wo"], "nt", T, D, D, 1024, 1024, D, "out_proj_bwd")
    grads["wo"] = _matmul(merged, dx2, "tn", D, D, T, 1024, 1024, 2048, "out_proj_grad", out_dtype=BF16)
    dy_gla, dy_rw, dpg, grads["gate_b"] = _merge_bwd(dm, y_gla, y_rw, pg, w["gate_b"])
    do_gla = _matmul(dy_gla, w["wb"], "nt", T, GLA_V, D, 1024, 1024, D, "branch_gla_bwd")
    do_rw = _matmul(dy_rw, w["wb"], "nt", T, RW_W, D, 1024, 1024, D, "branch_rwkv_bwd", b_off=(1, 0))
    grads["wb"] = jnp.concatenate([
        _matmul(o_gla, dy_gla, "tn", GLA_V, D, T, 1024, 1024, 4096, "branch_gla_grad", out_dtype=BF16),
        _matmul(o_rw, dy_rw, "tn", RW_W, D, T, 1024, 1024, 4096, "branch_rwkv_grad", out_dtype=BF16)], axis=0)
    dy, dr2, dv2, dk2b, dg, grads["lnx_w"], grads["lnx_b"], grads["r_k"] = _rw_post_bwd(y, rw, k2, g, post_pars, do_rw)
    early = _late_grad_parts(grads)
    received = {}
    dr1, dw, dk2a, dv1, dkk, db, received["up2"], received["down2"] = _rw_core_bwd(
        rw, dec, k2, kk, b, rw_states, rw_sa, dy, exchange=early)
    drw, grads["mu"], grads["w0"], grads["w_w2"], grads["a0"], grads["w_a2"], grads["w_g2"], grads["k_k"], grads["k_a"] = (
        _rw_prep_bwd(pr, w["mu"], gate_pars, (dr1, dr2), (dv1, dv2), dw, (dk2a, dk2b), dkk, db, dg))
    dpr = _shift_bwd(drw, w["mu"])
    do_raw, dr_gla, grads["gn"] = _gla_post_bwd(o_raw, pa, w["gn"], ind4, ind4_t, do_gla)
    dq, dk, dv, dla = _gla_core_bwd(pa, la, ltri, gla_states, do_raw)
    da_down, grads["gla_w_a2"], grads["gla_b_a"] = _gla_prep_bwd(pa, w["gla_w_a2"], w["gla_b_a"], dla)
    dpa = jnp.concatenate([dq.astype(BF16), dk.astype(BF16), dv.astype(BF16), dr_gla, da_down,
                           jnp.zeros((T, PA_W - PA_USED), BF16)], axis=1)
    dp = jnp.concatenate([dpg, dpr, dpa], axis=1)
    d_win = _matmul(h2, dp, "tn", D, DIN_P, T, 1024, 1024, 4096, "proj_grad", out_dtype=BF16)
    d_win = _shard_cols(_unalign_proj(d_win)).reshape(NDEV // 2, 2, D, DIN_SH).swapaxes(0, 1)
    dh2, received["win"] = _matmul(dp, win, "nt", T, D, DIN_P, 1024, 1024, DIN_P // 4, "proj_bwd",
                                   comm=("quad", [_chip_sum(d_win, "win")]))
    dx1, grads["g2"] = _rms_bwd(x1, w["g2"], dh2, dx2, "rms_bwd_mix")
    dx, grads["g1"], received["up1"], received["wd1"] = _ffn_bwd(
        dx1, x, w["g1"], w["wg1"], wu1, wd1, ffn1, "1", exchange=True)
    return loss, dx, grads, received


BIG = ("ffn1_wg", "ffn1_wu", "ffn1_wd", "w_in", "w_branch", "w_out", "ffn2_wg", "ffn2_wu", "ffn2_wd")
SMALL_SHARDED = ("gla_w_a2", "rwkv_w_w2", "rwkv_w_a2", "rwkv_w_g2")
REPLICATED = ("ffn1_norm", "mix_norm", "gla_b_a", "gla_gn_w", "rwkv_mu", "rwkv_w0", "rwkv_a0", "rwkv_k_k", "rwkv_k_a",
              "rwkv_r_k", "rwkv_lnx_w", "rwkv_lnx_b", "gate_b", "ffn2_norm", "final_norm")
WEIGHTS = ("ffn1_norm", "ffn1_wg", "ffn1_wu", "ffn1_wd", "mix_norm", "w_in", "gla_w_a2", "gla_b_a", "gla_gn_w",
           "rwkv_mu", "rwkv_w0", "rwkv_w_w2", "rwkv_a0", "rwkv_w_a2", "rwkv_w_g2", "rwkv_k_k", "rwkv_k_a", "rwkv_r_k",
           "rwkv_lnx_w", "rwkv_lnx_b", "gate_b", "w_branch", "w_out", "ffn2_norm", "ffn2_wg", "ffn2_wu", "ffn2_wd",
           "final_norm")


def _unshard_cols(g):
    return jnp.transpose(g, (1, 0, 2)).reshape(g.shape[1], NDEV * g.shape[2])


def _shard_cols(a):
    return jnp.transpose(a.reshape(a.shape[0], NDEV, a.shape[1] // NDEV), (1, 0, 2))


def _pad_rows(a, rows):
    return jnp.pad(a, ((0, rows - a.shape[0]), (0, 0)))


def _align_rw(a):
    c = 3 * RW_W
    z = jnp.zeros((a.shape[0], LORA_P - DECAY_LORA), a.dtype)
    return jnp.concatenate([a[:, :c], a[:, c:c + DECAY_LORA], z, a[:, c + DECAY_LORA:c + 2 * DECAY_LORA], z,
                            a[:, c + 2 * DECAY_LORA:]], axis=1)


def _unalign_rw(a):
    c = 3 * RW_W
    return jnp.concatenate([a[:, :c + DECAY_LORA], a[:, c + LORA_P:c + LORA_P + AAA_LORA], a[:, c + 2 * LORA_P:]], axis=1)


def _align_proj(a):
    gla = jnp.pad(a[:, :GLA_IN], ((0, 0), (0, PA_W - GLA_IN)))
    return jnp.concatenate([a[:, GLA_IN + RW_IN:], _align_rw(a[:, GLA_IN:GLA_IN + RW_IN]), gla], axis=1)


def _unalign_proj(a):
    return jnp.concatenate([a[:, PG_W + PR_W:PG_W + PR_W + GLA_IN], _unalign_rw(a[:, PG_W:PG_W + PR_W]), a[:, :PG_W]], axis=1)


def _layout_weights(gb, gs, rep):
    row = lambda n: rep[n].reshape(1, -1)
    return {
        "wg1": gb["ffn1_wg"],
        "g1": row("ffn1_norm"), "g2": row("mix_norm"), "g3": row("ffn2_norm"), "gf": row("final_norm"),
        "gla_w_a2": _pad_rows(_unshard_cols(gs["gla_w_a2"]), LORA_P), "gla_b_a": row("gla_b_a"),
        "gn": jnp.tile(row("gla_gn_w"), (1, GLA_HEADS)),
        "mu": _align_rw(row("rwkv_mu")), "w0": row("rwkv_w0"), "a0": row("rwkv_a0"),
        "w_w2": _pad_rows(_unshard_cols(gs["rwkv_w_w2"]), LORA_P),
        "w_a2": _pad_rows(_unshard_cols(gs["rwkv_w_a2"]), LORA_P),
        "w_g2": _unshard_cols(gs["rwkv_w_g2"]),
        "k_k": row("rwkv_k_k"), "k_a": row("rwkv_k_a"), "r_k": row("rwkv_r_k"),
        "lnx_w": row("rwkv_lnx_w"), "lnx_b": row("rwkv_lnx_b"), "gate_b": row("gate_b"),
    }


LATE_ROWS = (("ffn2_wd", FSH), ("w_branch", (GLA_V + RW_W) // NDEV), ("w_out", D // NDEV))


def _late_weights(g_up, g_down):
    r1, r2 = LATE_ROWS[0][1], LATE_ROWS[0][1] + LATE_ROWS[1][1]
    return {"wg2": g_up[:, 0], "wu2": g_up[:, 1], "wd2": g_down[:, :r1],
            "wb": g_down[:, r1:r2].reshape(GLA_V + RW_W, D), "wo": g_down[:, r2:].reshape(D, D)}


def _late_grad_parts(g):
    return [g["up2"], jnp.concatenate([g["wd2"], g["wb"].reshape(NDEV, -1, D), g["wo"].reshape(NDEV, -1, D)], axis=1)]


def _layout_grads(g):
    return {
        "ffn1_norm": g["g1"], "mix_norm": g["g2"], "ffn2_norm": g["g3"], "final_norm": g["gf"],
        "gla_w_a2": g["gla_w_a2"][:GLA_LORA], "gla_b_a": g["gla_b_a"],
        "gla_gn_w": jnp.sum(g["gn"].reshape(GLA_HEADS, GLA_DV), axis=0, keepdims=True),
        "rwkv_mu": _unalign_rw(g["mu"]), "rwkv_w0": g["w0"], "rwkv_a0": g["a0"],
        "rwkv_w_w2": g["w_w2"][:DECAY_LORA], "rwkv_w_a2": g["w_a2"][:AAA_LORA], "rwkv_w_g2": g["w_g2"],
        "rwkv_k_k": g["k_k"], "rwkv_k_a": g["k_a"], "rwkv_r_k": g["r_k"],
        "rwkv_lnx_w": g["lnx_w"], "rwkv_lnx_b": g["lnx_b"], "gate_b": g["gate_b"],
    }


_MESH = pl.DeviceIdType.MESH
_ANY = pl.BlockSpec(memory_space=pl.ANY)


def _position():
    return lax.axis_index("x"), lax.axis_index("y"), lax.axis_index("c")


def _slot(p):
    return 4 * p[0] + 2 * p[1] + p[2]


def _comm_sems(n):
    if not n:
        return []
    return [pltpu.SemaphoreType.DMA((7 * n,)), pltpu.SemaphoreType.DMA((7 * n,)), pltpu.SemaphoreType.DMA((n,))]


def _gather_plan(ins, outs, send_sems, recv_sems, local_sems):
    n = len(ins)
    x, y, c = _position()
    me, sibling = (x, y, c), (x, y, 1 - c)
    chips = [(1 - x, y), (x, 1 - y), (1 - x, 1 - y)]

    def copy(a, k, block, to, src=None):
        dst = outs[a].at[_slot(block)]
        return pltpu.make_async_remote_copy(
            src_ref=dst if src is None else src, dst_ref=dst, send_sem=send_sems.at[7 * a + k],
            recv_sem=recv_sems.at[7 * a + k], device_id=to, device_id_type=_MESH)

    def local(a):
        return pltpu.make_async_copy(ins[a], outs[a].at[_slot(me)], local_sems.at[a])

    def own(a):
        return [copy(a, 0, me, sibling, src=ins[a])] + [copy(a, 1 + j, me, (*chip, c), src=ins[a]) for j, chip in enumerate(chips)]

    def start():
        for a in range(n):
            local(a).start()
            for cp in own(a):
                cp.start()

    def forward():
        for a in range(n):
            for j, chip in enumerate(chips):
                copy(a, 1 + j, (*chip, c), me).wait_recv()
                copy(a, 4 + j, (*chip, c), sibling).start()

    def finish():
        for a in range(n):
            copy(a, 0, sibling, me).wait_recv()
            for j, chip in enumerate(chips):
                copy(a, 4 + j, (*chip, 1 - c), me).wait_recv()
        for a in range(n):
            for cp in own(a) + [copy(a, 4 + j, (*chip, c), sibling) for j, chip in enumerate(chips)]:
                cp.wait_send()
            local(a).wait()

    return start, forward, finish


def _exchange_plan(ins, outs, send_sems, recv_sems, local_sems):
    n = len(ins)
    x, y, c = _position()
    me = (x, y, c)
    flip = lambda v, f: 1 - v if f else v
    peers = [(flip(x, fx), flip(y, fy), flip(c, fc))
             for fx, fy, fc in ((0, 0, 1), (1, 0, 0), (0, 1, 0), (1, 1, 0), (1, 0, 1), (0, 1, 1), (1, 1, 1))]

    def copy(a, k, src_slot, dst_slot):
        return pltpu.make_async_remote_copy(
            src_ref=ins[a].at[src_slot], dst_ref=outs[a].at[dst_slot], send_sem=send_sems.at[7 * a + k],
            recv_sem=recv_sems.at[7 * a + k], device_id=peers[k], device_id_type=_MESH)

    def local(a):
        return pltpu.make_async_copy(ins[a].at[_slot(me)], outs[a].at[_slot(me)], local_sems.at[a])

    def start():
        for a in range(n):
            local(a).start()
            for k, peer in enumerate(peers):
                copy(a, k, _slot(peer), _slot(me)).start()

    def finish():
        for a in range(n):
            for k, peer in enumerate(peers):
                copy(a, k, _slot(peer), _slot(peer)).wait_recv()
        for a in range(n):
            for k, peer in enumerate(peers):
                copy(a, k, _slot(peer), _slot(me)).wait_send()
            local(a).wait()

    return start, finish


def _sibling_plan(ins, outs, send_sems, recv_sems, local_sems):
    x, y, c = _position()

    def copy(a):
        return pltpu.make_async_remote_copy(
            src_ref=ins[a].at[1 - c], dst_ref=outs[a], send_sem=send_sems.at[7 * a], recv_sem=recv_sems.at[7 * a],
            device_id=(x, y, 1 - c), device_id_type=_MESH)

    def start():
        for a in range(len(ins)):
            copy(a).start()

    def finish():
        for a in range(len(ins)):
            copy(a).wait()

    return start, finish


def _quad_plan(ins, outs, send_sems, recv_sems, local_sems):
    n = len(ins)
    x, y, c = _position()
    mine = 2 * x + y
    peers = [(1 - x, y), (x, 1 - y), (1 - x, 1 - y)]

    def copy(a, k, src_slot, dst_slot):
        return pltpu.make_async_remote_copy(
            src_ref=ins[a].at[src_slot], dst_ref=outs[a].at[dst_slot], send_sem=send_sems.at[7 * a + k],
            recv_sem=recv_sems.at[7 * a + k], device_id=(*peers[k], c), device_id_type=_MESH)

    def local(a):
        return pltpu.make_async_copy(ins[a].at[mine], outs[a].at[mine], local_sems.at[a])

    def start():
        for a in range(n):
            local(a).start()
            for k, (px, py) in enumerate(peers):
                copy(a, k, 2 * px + py, mine).start()

    def finish():
        for a in range(n):
            for k, (px, py) in enumerate(peers):
                copy(a, k, 2 * px + py, 2 * px + py).wait_recv()
        for a in range(n):
            for k, (px, py) in enumerate(peers):
                copy(a, k, 2 * px + py, mine).wait_send()
            local(a).wait()

    return start, finish


_PLANS = {"gather": (_gather_plan, lambda s: (NDEV,) + s), "gather_late": (_gather_plan, lambda s: (NDEV,) + s),
          "exchange": (_exchange_plan, lambda s: s),
          "sibling": (_sibling_plan, lambda s: s[1:]), "quad": (_quad_plan, lambda s: s)}


def _exchange_now(kind, arrays, name):
    n = len(arrays)

    def body(*refs):
        for stage in _PLANS[kind][0](refs[:n], refs[n:2 * n], *refs[2 * n:]):
            stage()

    return pl.pallas_call(
        body, name=name, in_specs=[_ANY] * n, out_specs=[_ANY] * n,
        out_shape=[jax.ShapeDtypeStruct(_PLANS[kind][1](a.shape), a.dtype) for a in arrays], scratch_shapes=_comm_sems(n),
    )(*arrays)


def _chip_sum(parts, name):
    (theirs,) = _exchange_now("sibling", [parts], "chip_send_" + name)
    shape = theirs.shape
    rows = shape[-2]
    for d in shape[1:-2]:
        rows *= d
    flat = (4, rows, shape[-1])
    tr = 512

    def body(both_ref, b_ref, o_ref):
        mine = jnp.where(lax.axis_index("c") == 0, both_ref[0], both_ref[1])
        o_ref[...] = (mine.astype(F32) + b_ref[...].astype(F32)).astype(BF16)

    blk = pl.BlockSpec((None, tr, shape[-1]), lambda s, r: (s, r, 0))
    out = pl.pallas_call(
        body, name="chip_sum_" + name, grid=(4, rows // tr),
        in_specs=[pl.BlockSpec((2, None, tr, shape[-1]), lambda s, r: (0, s, r, 0)), blk], out_specs=blk,
        out_shape=jax.ShapeDtypeStruct(flat, BF16), compiler_params=_params(("parallel", "parallel")),
    )(parts.reshape((2,) + flat), theirs.reshape(flat))
    return out.reshape(shape)


def _adamw_math(w, g, m, v):
    m = ADAM_B1 * m + (1.0 - ADAM_B1) * g
    v = ADAM_B2 * v + (1.0 - ADAM_B2) * (g * g)
    m_hat = m / (1.0 - ADAM_B1 ** ADAM_STEP)
    v_hat = v / (1.0 - ADAM_B2 ** ADAM_STEP)
    delta = -ADAM_LR * (m_hat / (jnp.sqrt(v_hat) + ADAM_EPS) + ADAM_WD * w)
    return delta, m, v


def _sum_slots(ref):
    total = ref[0].astype(F32)
    for s in range(1, ref.shape[0]):
        total = total + ref[s].astype(F32)
    return total


def _adamw(parts, w, m, v, tr, name, stack_index=None, row_block_offset=0, transposed=False):
    _, R, C = w.shape

    def body(p_ref, w_ref, m_ref, v_ref, g_ref, d_ref, nm_ref, nv_ref):
        g = _sum_slots(p_ref)
        g = g.T if transposed else g
        g_ref[...] = g
        d_ref[...], nm_ref[...], nv_ref[...] = _adamw_math(w_ref[...], g, m_ref[...], v_ref[...])

    if stack_index is None:
        p_spec = pl.BlockSpec((parts.shape[0], tr, C), lambda r: (0, row_block_offset + r, 0))
    else:
        p_spec = pl.BlockSpec((parts.shape[0], None, tr, C), lambda r: (0, stack_index, r, 0))
    if transposed:
        blk = pl.BlockSpec((None, C, tr), lambda r: (0, 0, r))
        out = jax.ShapeDtypeStruct((1, C, R), F32)
        w, m, v = (jnp.swapaxes(a, 1, 2) for a in (w, m, v))
    else:
        blk = pl.BlockSpec((None, tr, C), lambda r: (0, r, 0))
        out = jax.ShapeDtypeStruct((1, R, C), F32)
    res = pl.pallas_call(
        body, name=name, grid=(R // tr,), in_specs=[p_spec, blk, blk, blk], out_specs=[blk] * 4, out_shape=(out,) * 4,
        compiler_params=_params(("parallel",)),
    )(parts, w, m, v)
    return [jnp.swapaxes(a, 1, 2) for a in res] if transposed else res


def _sum_gathered(parts):
    _, R, C = parts.shape

    def body(p_ref, o_ref):
        o_ref[...] = _sum_slots(p_ref)

    return pl.pallas_call(body, name="small_grad_sum", out_shape=jax.ShapeDtypeStruct((R, C), F32),
                          compiler_params=_params())(parts)


def _adamw_small(w, g, m, v):
    def body(w_ref, g_ref, m_ref, v_ref, d_ref, nm_ref, nv_ref):
        d_ref[...], nm_ref[...], nv_ref[...] = _adamw_math(w_ref[...], g_ref[...], m_ref[...], v_ref[...])

    out = jax.ShapeDtypeStruct(w.shape, F32)
    return pl.pallas_call(body, name="adamw_small", out_shape=(out,) * 3, compiler_params=_params())(w, g, m, v)


def _pack(pieces, rows):
    flat = jnp.concatenate([p.reshape(-1) for p in pieces])
    return jnp.pad(flat, (0, rows * 128 - flat.shape[0])).reshape(rows, 128)


def _unpack(packed, shapes):
    flat = packed.reshape(-1)
    out, off = [], 0
    for s in shapes:
        size = 1
        for d in s:
            size *= d
        out.append(flat[off:off + size].reshape(s))
        off += size
    return out


def _rows_for(shapes, extra=0):
    total = extra
    for s in shapes:
        size = 1
        for d in s:
            size *= d
        total += size
    return -(-total // 1024) * 8


def kernel(x, ffn1_norm, ffn1_wg, ffn1_wu, ffn1_wd, mix_norm, w_in, gla_w_a2, gla_b_a, gla_gn_w, rwkv_mu, rwkv_w0, rwkv_w_w2, rwkv_a0, rwkv_w_a2, rwkv_w_g2, rwkv_k_k, rwkv_k_a, rwkv_r_k, rwkv_lnx_w, rwkv_lnx_b, gate_b, w_branch, w_out, ffn2_norm, ffn2_wg, ffn2_wu, ffn2_wd, final_norm, loss_target, m_ffn1_norm, m_ffn1_wg, m_ffn1_wu, m_ffn1_wd, m_mix_norm, m_w_in, m_gla_w_a2, m_gla_b_a, m_gla_gn_w, m_rwkv_mu, m_rwkv_w0, m_rwkv_w_w2, m_rwkv_a0, m_rwkv_w_a2, m_rwkv_w_g2, m_rwkv_k_k, m_rwkv_k_a, m_rwkv_r_k, m_rwkv_lnx_w, m_rwkv_lnx_b, m_gate_b, m_w_branch, m_w_out, m_ffn2_norm, m_ffn2_wg, m_ffn2_wu, m_ffn2_wd, m_final_norm, v_ffn1_norm, v_ffn1_wg, v_ffn1_wu, v_ffn1_wd, v_mix_norm, v_w_in, v_gla_w_a2, v_gla_b_a, v_gla_gn_w, v_rwkv_mu, v_rwkv_w0, v_rwkv_w_w2, v_rwkv_a0, v_rwkv_w_a2, v_rwkv_w_g2, v_rwkv_k_k, v_rwkv_k_a, v_rwkv_r_k, v_rwkv_lnx_w, v_rwkv_lnx_b, v_gate_b, v_w_branch, v_w_out, v_ffn2_norm, v_ffn2_wg, v_ffn2_wu, v_ffn2_wd, v_final_norm):
    wts = dict(zip(WEIGHTS, (ffn1_norm, ffn1_wg, ffn1_wu, ffn1_wd, mix_norm, w_in, gla_w_a2, gla_b_a, gla_gn_w, rwkv_mu, rwkv_w0, rwkv_w_w2, rwkv_a0, rwkv_w_a2, rwkv_w_g2, rwkv_k_k, rwkv_k_a, rwkv_r_k, rwkv_lnx_w, rwkv_lnx_b, gate_b, w_branch, w_out, ffn2_norm, ffn2_wg, ffn2_wu, ffn2_wd, final_norm)))
    mom = dict(zip(WEIGHTS, (m_ffn1_norm, m_ffn1_wg, m_ffn1_wu, m_ffn1_wd, m_mix_norm, m_w_in, m_gla_w_a2, m_gla_b_a, m_gla_gn_w, m_rwkv_mu, m_rwkv_w0, m_rwkv_w_w2, m_rwkv_a0, m_rwkv_w_a2, m_rwkv_w_g2, m_rwkv_k_k, m_rwkv_k_a, m_rwkv_r_k, m_rwkv_lnx_w, m_rwkv_lnx_b, m_gate_b, m_w_branch, m_w_out, m_ffn2_norm, m_ffn2_wg, m_ffn2_wu, m_ffn2_wd, m_final_norm)))
    var = dict(zip(WEIGHTS, (v_ffn1_norm, v_ffn1_wg, v_ffn1_wu, v_ffn1_wd, v_mix_norm, v_w_in, v_gla_w_a2, v_gla_b_a, v_gla_gn_w, v_rwkv_mu, v_rwkv_w0, v_rwkv_w_w2, v_rwkv_a0, v_rwkv_w_a2, v_rwkv_w_g2, v_rwkv_k_k, v_rwkv_k_a, v_rwkv_r_k, v_rwkv_lnx_w, v_rwkv_lnx_b, v_gate_b, v_w_branch, v_w_out, v_ffn2_norm, v_ffn2_wg, v_ffn2_wu, v_ffn2_wd, v_final_norm)))
    two = lambda a: a.reshape(a.shape[-2:])

    bf = lambda n: two(wts[n]).astype(BF16)
    lora = jnp.concatenate([jnp.pad(two(gla_w_a2), ((0, 0), (0, 128 - GLA_QK // NDEV)))] +
                           [two(wts[n]) for n in SMALL_SHARDED[1:]], axis=0)
    g_wg1, g_lora = _exchange_now("gather", [bf("ffn1_wg"), lora], "gather_weights")
    gb = {"ffn1_wg": g_wg1}
    gs = {"gla_w_a2": g_lora[:, :GLA_LORA, :GLA_QK // NDEV]}
    row = GLA_LORA
    for n in SMALL_SHARDED[1:]:
        gs[n] = g_lora[:, row:row + wts[n].shape[1]]
        row += wts[n].shape[1]
    w = _layout_weights(gb, gs, {n: wts[n] for n in REPLICATED})
    blocks = {"wu1": bf("ffn1_wu"), "wd1": bf("ffn1_wd"), "win": bf("w_in"),
              "late": [jnp.stack([bf("ffn2_wg"), bf("ffn2_wu")]), jnp.concatenate([bf(n) for n, _ in LATE_ROWS], axis=0)]}

    loss_part, grad_x, grads, parts = _local_step(x[0], loss_target[0], w, blocks)
    small = _layout_grads(grads)
    result = {}
    state = lambda n: (wts[n], mom[n], var[n])
    for group, names in (("up1", ("ffn1_wg", "ffn1_wu")), ("up2", ("ffn2_wg", "ffn2_wu"))):
        for i, n in enumerate(names):
            result[n] = _adamw(parts[group], *state(n), 256, "adamw_" + n, stack_index=i, transposed=True)
    result["ffn1_wd"] = _adamw(parts["wd1"], *state("ffn1_wd"), 64, "adamw_ffn1_wd")
    row = 0
    for n, rows in LATE_ROWS:
        result[n] = _adamw(parts["down2"], *state(n), 64, "adamw_" + n, row_block_offset=row // 64)
        row += rows
    result["w_in"] = _adamw(parts["win"], *state("w_in"), 256, "adamw_w_in")

    small_names = [n for n in WEIGHTS if n not in BIG]
    full_shapes = [small[n].shape for n in small_names]
    rows_full = _rows_for(full_shapes, extra=128)
    packed = _pack([small[n] for n in small_names] + [loss_part], rows_full)
    (gathered,) = _exchange_now("gather", [packed], "gather_small_grads")
    total = _sum_gathered(gathered)
    *full_grads, loss_row = _unpack(total, full_shapes + [(1, 128)])
    me = _slot(_position())
    own = {}
    for n, g in zip(small_names, full_grads):
        if n in SMALL_SHARDED:
            cols = wts[n].shape[-1]
            g = lax.dynamic_slice_in_dim(g, me * cols, cols, axis=1)
        own[n] = g.reshape(wts[n].shape)
    own_shapes = [wts[n].shape for n in small_names]
    rows_own = _rows_for(own_shapes)
    pk = lambda d: _pack([d[n] for n in small_names], rows_own)
    d_s, m_s, v_s = _adamw_small(pk(wts), pk(own), pk(mom), pk(var))
    for n, d, m, v in zip(small_names, _unpack(d_s, own_shapes), _unpack(m_s, own_shapes), _unpack(v_s, own_shapes)):
        result[n] = (own[n], d, m, v)

    shaped = lambda n, k: result[n][k].reshape(wts[n].shape)
    return (loss_row[0, 0], grad_x[None],
            *[shaped(n, 0) for n in WEIGHTS], *[shaped(n, 1) for n in WEIGHTS],
            *[shaped(n, 2) for n in WEIGHTS], *[shaped(n, 3) for n in WEIGHTS])
```

```python
import functools

import jax
import jax.numpy as jnp
from jax import lax
from jax.experimental import pallas as pl
from jax.experimental.pallas import tpu as pltpu

F32 = jnp.float32
BF16 = jnp.bfloat16
HI = lax.Precision.HIGHEST

NDEV = 8
D = 2048
DFF = 5632
FSH = DFF // NDEV
CHUNK = 64
GLA_HEADS, GLA_DK, GLA_DV = 4, 128, 256
GLA_QK, GLA_V, GLA_LORA, GLA_TAU = 512, 1024, 16, 16.0
RW_HEADS, RW_HD, RW_W = 16, 64, 1024
DECAY_LORA, AAA_LORA, GATE_LORA = 96, 96, 256
GN_EPS = 64e-5
NORM_EPS = 1e-6
GLA_IN = 2 * GLA_QK + 2 * GLA_V + GLA_LORA
RW_IN = 3 * RW_W + DECAY_LORA + AAA_LORA + GATE_LORA
D_IN = GLA_IN + RW_IN + 2 * D
DIN_SH = D_IN // NDEV
PG_W = 2 * D
PR_W = 3584
PA_W = 3584
PA_USED = 2 * GLA_QK + 2 * GLA_V + 128
DIN_P = PG_W + PR_W + PA_W
LORA_P = 128

ADAM_LR, ADAM_B1, ADAM_B2, ADAM_EPS, ADAM_WD, ADAM_STEP = 0.001, 0.9, 0.999, 1e-08, 0.01, 10

VMEM_LIMIT = 56 * 1024 * 1024
RW_TB = 32
RW_G = 32
RW_NP = 8


def _params(sem=None, vmem=VMEM_LIMIT):
    return pltpu.CompilerParams(dimension_semantics=sem, vmem_limit_bytes=vmem)


def _pair_mask():
    return lax.broadcasted_iota(jnp.int32, (RW_HD, 2 * RW_HD), 1) < RW_HD


def _pair_rowsum(p, mask):
    tot = jnp.sum(p, axis=1, keepdims=True)
    first = jnp.sum(jnp.where(mask, p, 0.0), axis=1, keepdims=True)
    return first, tot - first


def _split_transposed(x_ref, q, dst_ref, base):
    xt = x_ref[:, 128 * q:128 * (q + 1)].T
    for g in range(RW_TB // RW_G):
        dst_ref[base + g, :, 0:RW_G] = xt[:, g * RW_G:(g + 1) * RW_G]


def _pair_column(tile_ref, idx, i, mask):
    return jnp.where(mask, tile_ref[idx, 0:RW_HD, i:i + 1], tile_ref[idx, RW_HD:, i:i + 1])


def _rw_core_fwd(rw, w, k2, kk, b, gather=()):
    T = rw.shape[0]
    nb = T // RW_TB
    ng = RW_TB // RW_G
    NP = RW_NP
    nc = len(gather)
    npair = RW_HEADS // 2 // NP

    def body(r_ref, v_ref, w_ref, k_ref, kk_ref, b_ref, *rest):
        g_in, (y_ref, st_ref, sa_ref), g_out = rest[:nc], rest[nc:nc + 3], rest[nc + 3:2 * nc + 3]
        s_scr, vt_scr, yt_scr, rows_scr = rest[2 * nc + 3:2 * nc + 7]
        pair, blk_i = pl.program_id(0), pl.program_id(1)
        if nc:
            start, forward, finish = _gather_plan(g_in, g_out, *rest[2 * nc + 7:])
            pl.when((pair == 0) & (blk_i == 0))(start)
            pl.when((pair == 0) & (blk_i == nb // 2))(forward)

        @pl.when(pl.program_id(1) == 0)
        def _():
            s_scr[...] = jnp.zeros_like(s_scr)
            yt_scr[...] = jnp.zeros_like(yt_scr)

        mask = _pair_mask()
        for q in range(NP):
            _split_transposed(v_ref, q, vt_scr, q * ng)
        R_, W_, K_, KK_, B_ = range(5)
        for a, ref in enumerate((r_ref, w_ref, k_ref, kk_ref, b_ref)):
            for q in range(NP):
                rows_scr[a * NP + q] = ref[:, 128 * q:128 * (q + 1)]

        def group(g, states):
            states = list(states)
            for i in range(RW_G):
                t = g * RW_G + i
                row = lambda a, q: rows_scr[a * NP + q, pl.ds(t, 1), :]
                sums = [_pair_rowsum(states[q] * row(KK_, q), mask) for q in range(NP)]
                for q in range(NP):
                    sa = jnp.where(mask, *sums[q])
                    sa_ref[q, t] = sa
                    states[q] = (states[q] * row(W_, q) - sa * row(B_, q)
                                 + _pair_column(vt_scr, q * ng + g, i, mask) * row(K_, q))
                    st_ref[q, t] = states[q]
                outs = [_pair_rowsum(states[q] * row(R_, q), mask) for q in range(NP)]
                for q in range(NP):
                    yt_scr[q * ng + g, 0:RW_HD, i:i + 1] = outs[q][0]
                    yt_scr[q * ng + g, RW_HD:, i:i + 1] = outs[q][1]
            return tuple(states)

        states = lax.fori_loop(0, ng, group, tuple(s_scr[q] for q in range(NP)))
        for q in range(NP):
            s_scr[q] = states[q]
            for g in range(ng):
                y_ref[g * RW_G:(g + 1) * RW_G, 128 * q:128 * (q + 1)] = yt_scr[q * ng + g].T[0:RW_G, :]
        if nc:
            pl.when((pair == npair - 1) & (blk_i == nb - 1))(finish)

    blk = lambda cb: pl.BlockSpec((RW_TB, 128 * NP), lambda p, i, cb=cb: (i, cb + p))
    tiles = pltpu.VMEM((NP * ng, 128, 128), F32)
    return pl.pallas_call(
        body, name="rw_core_fwd", grid=(npair, nb),
        in_specs=[blk(0), blk(2 * RW_W // (128 * NP)), blk(0), blk(0), blk(0), blk(0)] + [_ANY] * nc,
        out_specs=[blk(0)] + [pl.BlockSpec((NP, RW_TB, RW_HD, 128), lambda p, i: (p, i, 0, 0))] * 2 + [_ANY] * nc,
        out_shape=[jax.ShapeDtypeStruct((T, RW_W), F32)] + [jax.ShapeDtypeStruct((RW_HEADS // 2, T, RW_HD, 128), F32)] * 2
        + [jax.ShapeDtypeStruct((NDEV,) + a.shape, a.dtype) for a in gather],
        scratch_shapes=[pltpu.VMEM((NP, RW_HD, 128), F32), tiles, tiles, pltpu.VMEM((5 * NP, RW_TB, 128), F32)]
        + _comm_sems(nc),
        compiler_params=_params(("arbitrary", "arbitrary")),
    )(rw, rw, w, k2, kk, b, *gather)


def _rw_core_bwd(rw, w, k2, kk, b, states, sa_tiles, dy, exchange=()):
    T = rw.shape[0]
    nb = T // RW_TB
    ng = RW_TB // RW_G
    NP = RW_NP
    nc = len(exchange)
    npair = RW_HEADS // 2 // NP

    def body(r_ref, v_ref, w_ref, k_ref, kk_ref, b_ref, dy_ref, st_ref, sp_ref, sa_ref, *rest):
        e_in, e_out = rest[:nc], rest[nc + 6:2 * nc + 6]
        dr_ref, dw_ref, dk_ref, dv_ref, dkk_ref, db_ref = rest[nc:nc + 6]
        ds_scr, vt_scr, dyt_scr, dvt_scr, rows_scr, out_scr = rest[2 * nc + 6:2 * nc + 12]
        step = pl.program_id(1)
        if nc:
            start, finish = _exchange_plan(e_in, e_out, *rest[2 * nc + 12:])
            pl.when((pl.program_id(0) == 0) & (step == 0))(start)

        @pl.when(step == 0)
        def _():
            ds_scr[...] = jnp.zeros_like(ds_scr)
            dvt_scr[...] = jnp.zeros_like(dvt_scr)

        mask = _pair_mask()
        for q in range(NP):
            _split_transposed(v_ref, q, vt_scr, q * ng)
            _split_transposed(dy_ref, q, dyt_scr, q * ng)
        R_, W_, K_, KK_, B_ = range(5)
        for a, ref in enumerate((r_ref, w_ref, k_ref, kk_ref, b_ref)):
            for q in range(NP):
                rows_scr[a * NP + q] = ref[:, 128 * q:128 * (q + 1)]

        def group(gg, grads):
            g = ng - 1 - gg
            grads = list(grads)
            pairs = range(NP)
            for i in reversed(range(RW_G)):
                t = g * RW_G + i
                row = lambda a, q: rows_scr[a * NP + q, pl.ds(t, 1), :]

                def put(a, q, value):
                    out_scr[a * NP + q, pl.ds(t, 1), :] = value

                s_old = [st_ref[q, jnp.maximum(t - 1, 0)] for q in pairs]
                if i == 0:
                    s_old = [jnp.where(g == 0, jnp.where(step == nb - 1, 0.0, sp_ref[q, 0]), s_old[q]) for q in pairs]
                dycol = [_pair_column(dyt_scr, q * ng + g, i, mask) for q in pairs]
                dS = [grads[q] + dycol[q] * row(R_, q) for q in pairs]
                m = [_pair_rowsum(dS[q] * row(B_, q), mask) for q in pairs]
                dv = [_pair_rowsum(dS[q] * row(K_, q), mask) for q in pairs]
                for q in pairs:
                    put(R_, q, jnp.sum(st_ref[q, t] * dycol[q], axis=0, keepdims=True))
                    put(W_, q, jnp.sum(dS[q] * s_old[q], axis=0, keepdims=True))
                    put(K_, q, jnp.sum(dS[q] * _pair_column(vt_scr, q * ng + g, i, mask), axis=0, keepdims=True))
                for q in pairs:
                    dsa = -jnp.where(mask, *m[q])
                    grads[q] = dS[q] * row(W_, q) + dsa * row(KK_, q)
                    put(KK_, q, jnp.sum(s_old[q] * dsa, axis=0, keepdims=True))
                    put(B_, q, -jnp.sum(dS[q] * sa_ref[q, t], axis=0, keepdims=True))
                    dvt_scr[q * ng + g, 0:RW_HD, i:i + 1] = dv[q][0]
                    dvt_scr[q * ng + g, RW_HD:, i:i + 1] = dv[q][1]
            return tuple(grads)

        grads = lax.fori_loop(0, ng, group, tuple(ds_scr[q] for q in range(NP)))
        for q in range(NP):
            ds_scr[q] = grads[q]
            for a, ref in enumerate((dr_ref, dw_ref, dk_ref, dkk_ref, db_ref)):
                ref[:, 128 * q:128 * (q + 1)] = out_scr[a * NP + q]
            for g in range(ng):
                dv_ref[g * RW_G:(g + 1) * RW_G, 128 * q:128 * (q + 1)] = dvt_scr[q * ng + g].T[0:RW_G, :]
        if nc:
            pl.when((pl.program_id(0) == npair - 1) & (step == nb - 1))(finish)

    blk = lambda cb: pl.BlockSpec((RW_TB, 128 * NP), lambda p, i, cb=cb: (nb - 1 - i, cb + p))
    st_spec = pl.BlockSpec((NP, RW_TB, RW_HD, 128), lambda p, i: (p, nb - 1 - i, 0, 0))
    sp_spec = pl.BlockSpec((NP, 1, RW_HD, 128), lambda p, i: (p, jnp.maximum((nb - 1 - i) * RW_TB - 1, 0), 0, 0))
    out = jax.ShapeDtypeStruct((T, RW_W), F32)
    tiles = pltpu.VMEM((NP * ng, 128, 128), F32)
    return pl.pallas_call(
        body, name="rw_core_bwd", grid=(npair, nb),
        in_specs=[blk(0), blk(2 * RW_W // (128 * NP)), blk(0), blk(0), blk(0), blk(0), blk(0), st_spec, sp_spec, st_spec]
        + [_ANY] * nc,
        out_specs=[blk(0)] * 6 + [_ANY] * nc,
        out_shape=[out] * 6 + [jax.ShapeDtypeStruct(a.shape, a.dtype) for a in exchange],
        scratch_shapes=[pltpu.VMEM((NP, RW_HD, 128), F32), tiles, tiles, tiles,
                        pltpu.VMEM((5 * NP, RW_TB, 128), F32), pltpu.VMEM((5 * NP, RW_TB, 128), F32)] + _comm_sems(nc),
        compiler_params=_params(("arbitrary", "arbitrary")),
    )(rw, rw, w, k2, kk, b, dy, states, states, sa_tiles, *exchange)


GLA_CB = 8


def _gla_chunk(s_t, q, k, v, la, ltri):
    cum = jnp.dot(ltri, la, precision=HI, preferred_element_type=F32)
    total = jnp.sum(la, axis=0, keepdims=True)
    kdec = k * jnp.exp(total - cum)
    u_t = _bdot(v, kdec, _TN)
    s_t = jnp.exp(total) * s_t + u_t
    o = _bdot(q * (GLA_DK ** -0.5), s_t, _NT)
    return s_t, o


def _gla_core_fwd(pa, la, ltri):
    T = pa.shape[0]
    cb = min(GLA_CB, T // CHUNK)
    rows = cb * CHUNK
    nsteps = T // rows

    def body(q_ref, k_ref, v_ref, la_ref, ltri_ref, o_ref, st_ref, s_scr):
        @pl.when(pl.program_id(0) == 0)
        def _():
            s_scr[...] = jnp.zeros_like(s_scr)

        def chunk(c, states):
            sl = pl.ds(pl.multiple_of(c * CHUNK, CHUNK), CHUNK)
            out = []
            for h in range(GLA_HEADS):
                qk, vv = slice(GLA_DK * h, GLA_DK * (h + 1)), slice(GLA_DV * h, GLA_DV * (h + 1))
                s_t, o = _gla_chunk(states[h], q_ref[sl, qk], k_ref[sl, qk], v_ref[sl, vv], la_ref[sl, qk], ltri_ref[...])
                o_ref[sl, vv] = o
                st_ref[h, c] = s_t
                out.append(s_t)
            return tuple(out)

        states = lax.fori_loop(0, cb, chunk, tuple(s_scr[h] for h in range(GLA_HEADS)))
        for h in range(GLA_HEADS):
            s_scr[h] = states[h]

    qk = lambda cb_: pl.BlockSpec((rows, GLA_QK), lambda i, cb_=cb_: (i, cb_))
    return pl.pallas_call(
        body, name="gla_core_fwd", grid=(nsteps,),
        in_specs=[qk(0), qk(1), pl.BlockSpec((rows, GLA_V), lambda i: (i, 1)), qk(0),
                  pl.BlockSpec((CHUNK, CHUNK), lambda i: (0, 0))],
        out_specs=[pl.BlockSpec((rows, GLA_V), lambda i: (i, 0)),
                   pl.BlockSpec((GLA_HEADS, cb, GLA_DV, GLA_DK), lambda i: (0, i, 0, 0))],
        out_shape=(jax.ShapeDtypeStruct((T, GLA_V), F32),
                   jax.ShapeDtypeStruct((GLA_HEADS, T // CHUNK, GLA_DV, GLA_DK), F32)),
        scratch_shapes=[pltpu.VMEM((GLA_HEADS, GLA_DV, GLA_DK), F32)],
        compiler_params=_params(("arbitrary",)),
    )(pa, pa, pa, la, ltri)


def _gla_core_bwd(pa, la, ltri, states, do):
    T = pa.shape[0]
    cb = min(GLA_CB, T // CHUNK)
    rows = cb * CHUNK
    nsteps = T // rows

    def body(q_ref, k_ref, v_ref, la_ref, ltri_ref, st_ref, sp_ref, do_ref,
             dq_ref, dk_ref, dv_ref, dla_ref, ds_scr):
        step = pl.program_id(0)

        @pl.when(step == 0)
        def _():
            ds_scr[...] = jnp.zeros_like(ds_scr)

        def chunk(cc, grads):
            c = cb - 1 - cc
            sl = pl.ds(pl.multiple_of(c * CHUNK, CHUNK), CHUNK)
            out = []
            for h in range(GLA_HEADS):
                qk, vv = slice(GLA_DK * h, GLA_DK * (h + 1)), slice(GLA_DV * h, GLA_DV * (h + 1))
                s_before = jnp.where(step == nsteps - 1, 0.0, sp_ref[h, 0])
                s_prev = jnp.where(c == 0, s_before, st_ref[h, jnp.maximum(c - 1, 0)])
                _, vjp = jax.vjp(functools.partial(_gla_chunk, ltri=ltri_ref[...]),
                                 s_prev, q_ref[sl, qk], k_ref[sl, qk], v_ref[sl, vv], la_ref[sl, qk])
                ds_prev, dq, dk, dv, dla = vjp((grads[h], do_ref[sl, vv]))
                dq_ref[sl, qk] = dq
                dk_ref[sl, qk] = dk
                dv_ref[sl, vv] = dv
                dla_ref[sl, qk] = dla
                out.append(ds_prev)
            return tuple(out)

        grads = lax.fori_loop(0, cb, chunk, tuple(ds_scr[h] for h in range(GLA_HEADS)))
        for h in range(GLA_HEADS):
            ds_scr[h] = grads[h]

    r = lambda i: nsteps - 1 - i
    qk = lambda cb_: pl.BlockSpec((rows, GLA_QK), lambda i, cb_=cb_: (r(i), cb_))
    o512 = pl.BlockSpec((rows, GLA_QK), lambda i: (r(i), 0))
    o1024 = pl.BlockSpec((rows, GLA_V), lambda i: (r(i), 0))
    return pl.pallas_call(
        body, name="gla_core_bwd", grid=(nsteps,),
        in_specs=[qk(0), qk(1), pl.BlockSpec((rows, GLA_V), lambda i: (r(i), 1)), qk(0),
                  pl.BlockSpec((CHUNK, CHUNK), lambda i: (0, 0)),
                  pl.BlockSpec((GLA_HEADS, cb, GLA_DV, GLA_DK), lambda i: (0, r(i), 0, 0)),
                  pl.BlockSpec((GLA_HEADS, 1, GLA_DV, GLA_DK), lambda i: (0, jnp.maximum(r(i) * cb - 1, 0), 0, 0)),
                  o1024],
        out_specs=[o512, o512, o1024, o512],
        out_shape=(jax.ShapeDtypeStruct((T, GLA_QK), F32), jax.ShapeDtypeStruct((T, GLA_QK), F32),
                   jax.ShapeDtypeStruct((T, GLA_V), F32), jax.ShapeDtypeStruct((T, GLA_QK), F32)),
        scratch_shapes=[pltpu.VMEM((GLA_HEADS, GLA_DV, GLA_DK), F32)],
        compiler_params=_params(("arbitrary",)),
    )(pa, pa, pa, la, ltri, states, states, do)


def _rowwise(fn, name, T, tm, rows, pars, row_outs, acc_outs):
    nr, npar, nro = len(rows), len(pars), len(row_outs)
    tm = min(tm, T)
    nsteps = T // tm

    def body(*refs):
        i = pl.program_id(0)
        ins = [r[...] for r in refs[:nr + npar]]
        outs, accs = fn(i, *ins)
        for r, o in zip(refs[nr + npar:nr + npar + nro], outs):
            r[...] = o.astype(r.dtype)
        for r, a in zip(refs[nr + npar + nro:], accs):
            @pl.when(i == 0)
            def _(r=r, a=a):
                r[...] = a

            @pl.when(i > 0)
            def _(r=r, a=a):
                r[...] += a

    def rspec(width, cb, kind):
        if kind == "cur":
            return pl.BlockSpec((tm, width), lambda i: (i, cb))
        if kind == "prev":
            return pl.BlockSpec((8, width), lambda i: (jnp.maximum(i * (tm // 8) - 1, 0), cb))
        return pl.BlockSpec((8, width), lambda i: (jnp.minimum((i + 1) * (tm // 8), T // 8 - 1), cb))

    in_specs = [rspec(w, cb, kind) for (_, w, cb, kind) in rows]
    in_specs += [pl.BlockSpec(p.shape, lambda i, nd=p.ndim: (0,) * nd) for p in pars]
    out_specs = [pl.BlockSpec((tm, w), lambda i: (i, 0)) for (w, _) in row_outs]
    out_specs += [pl.BlockSpec(s, lambda i, nd=len(s): (0,) * nd) for s in acc_outs]
    out_shape = [jax.ShapeDtypeStruct((T, w), dt) for (w, dt) in row_outs]
    out_shape += [jax.ShapeDtypeStruct(s, F32) for s in acc_outs]
    res = pl.pallas_call(
        body, name=name, grid=(nsteps,), in_specs=in_specs, out_specs=out_specs, out_shape=out_shape,
        compiler_params=_params(("arbitrary",)),
    )(*[r[0] for r in rows], *pars)
    return res


def _cur(a, width=None, cb=0):
    return (a, a.shape[1] if width is None else width, cb, "cur")


def _sigmoid(x):
    return 1.0 / (1.0 + jnp.exp(-x))


def _silu(x):
    return x * _sigmoid(x)


def _softplus(x):
    return jnp.maximum(x, 0.0) + jnp.log(1.0 + jnp.exp(-jnp.abs(x)))


def _rms(x, g):
    return x * lax.rsqrt(jnp.mean(x * x, axis=-1, keepdims=True) + NORM_EPS) * g


def _dot_hi(a, b):
    return jnp.dot(a, b, precision=lax.Precision.HIGH, preferred_element_type=F32)


def _rms_fwd(x, g, name):
    T = x.shape[0]
    fn = lambda i, xb, gb: ((_rms(xb, gb),), ())
    return _rowwise(fn, name, T, 256, [_cur(x)], [g], [(D, BF16)], [])[0]


def _rms_bwd(x, g, dh, dres, name, matmul_copy=False):
    T = x.shape[0]

    def fn(i, xb, dhb, drb, gb):
        _, vjp = jax.vjp(_rms, xb, gb)
        dx, dg = vjp(dhb)
        return (drb + dx,) * (2 if matmul_copy else 1), (dg,)

    outs = [(D, F32), (D, BF16)] if matmul_copy else [(D, F32)]
    return _rowwise(fn, name, T, 256, [_cur(x), _cur(dh), _cur(dres)], [g], outs, [(1, D)])


def _loss_bwd(x, target, g):
    T = x.shape[0]

    def loss(xb, gb, tb):
        err = _rms(xb, gb) - tb
        return 0.5 * jnp.sum(jnp.mean(err * err, axis=-1, keepdims=True))

    def fn(i, xb, tb, gb):
        val, (dx, dg) = jax.value_and_grad(loss, argnums=(0, 1))(xb, gb, tb)
        return (dx, dx), (jnp.full((1, 128), val, F32), dg)

    return _rowwise(fn, "loss_bwd", T, 256, [_cur(x), _cur(target)], [g], [(D, F32), (D, BF16)], [(1, 128), (1, D)])


def _gla_la(a_down, w_a2, b_a):
    return -_softplus(-(_bdot(a_down, w_a2, _NN) + b_a)) * (1.0 / GLA_TAU)


def _gla_prep(pa, w_a2, b_a):
    T = pa.shape[0]
    fn = lambda i, ab, wb, bb: ((_gla_la(ab, wb, bb),), ())
    return _rowwise(fn, "gla_prep", T, 512, [_cur(pa, LORA_P, (2 * GLA_QK + 2 * GLA_V) // LORA_P)], [w_a2, b_a],
                    [(GLA_QK, F32)], [])[0]


def _gla_prep_bwd(pa, w_a2, b_a, dla):
    T = pa.shape[0]

    def fn(i, ab, dlab, wb, bb):
        _, vjp = jax.vjp(_gla_la, ab, wb, bb)
        da, dw, db = vjp(dlab)
        return (da,), (dw, db)

    return _rowwise(fn, "gla_prep_bwd", T, 512, [_cur(pa, LORA_P, (2 * GLA_QK + 2 * GLA_V) // LORA_P), _cur(dla)],
                    [w_a2, b_a], [(LORA_P, BF16)], [(LORA_P, GLA_QK), (1, GLA_QK)])


def _gla_out(o, r, gn, ind, ind_t):
    ms = _dot_hi(_dot_hi(o * o, ind) * (1.0 / GLA_DV), ind_t)
    return o * lax.rsqrt(ms + NORM_EPS) * gn * _silu(r)


def _gla_post(o_raw, pa, gn, ind, ind_t):
    T = pa.shape[0]
    fn = lambda i, ob, rb, gb, a, b: ((_gla_out(ob, rb, gb, a, b),), ())
    return _rowwise(fn, "gla_post", T, 256, [_cur(o_raw), _cur(pa, GLA_V, 2)], [gn, ind, ind_t], [(GLA_V, BF16)], [])[0]


def _gla_post_bwd(o_raw, pa, gn, ind, ind_t, do):
    T = pa.shape[0]

    def fn(i, ob, rb, dob, gb, a, b):
        _, vjp = jax.vjp(lambda o, r, g: _gla_out(o, r, g, a, b), ob, rb, gb)
        d_o, d_r, d_g = vjp(dob)
        return (d_o, d_r), (d_g,)

    return _rowwise(fn, "gla_post_bwd", T, 256, [_cur(o_raw), _cur(pa, GLA_V, 2), _cur(do)], [gn, ind, ind_t],
                    [(GLA_V, F32), (GLA_V, BF16)], [(1, GLA_V)])


def _shift_rows(cur, prev8, i):
    first = jnp.where(i == 0, 0.0, prev8[7:8, :])
    rolled = pltpu.roll(cur, 1, 0)
    return jnp.where(lax.broadcasted_iota(jnp.int32, cur.shape, 0) == 0, first, rolled)


def _rw_gates(rw, w0, w_w2, a0, w_a2, w_g2, k_k, k_a, ind, ind_t):
    rk = rw[:, RW_W:2 * RW_W]
    wd = rw[:, 3 * RW_W:3 * RW_W + LORA_P]
    ad = rw[:, 3 * RW_W + LORA_P:3 * RW_W + 2 * LORA_P]
    gd = rw[:, 3 * RW_W + 2 * LORA_P:]
    w_raw = w0 + _bdot(jnp.tanh(wd), w_w2, _NN)
    w = jnp.exp(-jnp.exp(-_softplus(-w_raw) - 0.5))
    a = _sigmoid(a0 + _bdot(ad, w_a2, _NN))
    g = _bdot(_sigmoid(gd), w_g2, _NN)
    kk = rk * k_k
    kk = kk * _dot_hi(lax.rsqrt(jnp.maximum(_dot_hi(kk * kk, ind), 1e-24)), ind_t)
    k2 = rk * (1.0 + (a - 1.0) * k_a)
    return w, k2, kk, kk * a, g


def _rw_prep(pr, mu, gate_pars):
    T = pr.shape[0]

    def fn(i, cur, prev8, mub, *gp):
        rw = cur + mub * (_shift_rows(cur, prev8, i) - cur)
        return (rw,) + _rw_gates(rw, *gp), ()

    return _rowwise(fn, "rw_prep", T, 256, [_cur(pr), (pr, PR_W, 0, "prev")], [mu, *gate_pars],
                    [(PR_W, F32)] + [(RW_W, F32)] * 5, [])


def _rw_prep_bwd(pr, mu, gate_pars, d_r, d_v, d_w, d_k2, d_kk, d_b, d_g):
    T = pr.shape[0]
    rows = [_cur(pr), (pr, PR_W, 0, "prev")] + [_cur(x) for x in (*d_r, *d_v, d_w, *d_k2, d_kk, d_b, d_g)]
    acc = [(1, PR_W)] + [tuple(p.shape) for p in gate_pars[:-2]]

    def fn(i, cur, prev8, dr1, dr2, dv1, dv2, dw, dk1, dk2, dkk, db, dg, mub, *gp):
        sh = _shift_rows(cur, prev8, i)
        rw = cur + mub * (sh - cur)
        _, vjp = jax.vjp(lambda x, *p: _rw_gates(x, *p, gp[-2], gp[-1]), rw, *gp[:-2])
        grads = vjp((dw, dk1 + dk2, dkk, db, dg))
        zeros = jnp.zeros((cur.shape[0], PR_W - 3 * RW_W), F32)
        drw = grads[0] + jnp.concatenate([dr1 + dr2, jnp.zeros_like(dr1), dv1 + dv2, zeros], axis=1)
        dmu = jnp.sum(drw * (sh - cur), axis=0, keepdims=True)
        return (drw,), (dmu, *grads[1:])

    return _rowwise(fn, "rw_prep_bwd", T, 128, rows, [mu, *gate_pars], [(PR_W, F32)], acc)


def _shift_bwd(drw, mu):
    T = drw.shape[0]
    tm = min(256, T)

    def fn(i, cur, next8, mub):
        last = jnp.where(i == T // tm - 1, 0.0, next8[0:1, :])
        rolled = pltpu.roll(cur, cur.shape[0] - 1, 0)
        nxt = jnp.where(lax.broadcasted_iota(jnp.int32, cur.shape, 0) == cur.shape[0] - 1, last, rolled)
        return ((1.0 - mub) * cur + mub * nxt,), ()

    return _rowwise(fn, "shift_bwd", T, tm, [_cur(drw), (drw, PR_W, 0, "next")], [mu], [(PR_W, BF16)], [])[0]


def _rw_out(y, r, v, k2, g, lnx_w, lnx_b, r_k, ind, ind_t):
    mean = _dot_hi(_dot_hi(y, ind) * (1.0 / RW_HD), ind_t)
    yc = y - mean
    var = _dot_hi(_dot_hi(yc * yc, ind) * (1.0 / RW_HD), ind_t)
    yn = yc * lax.rsqrt(var + GN_EPS) * lnx_w + lnx_b
    bonus = _dot_hi(_dot_hi(r * k2 * r_k, ind), ind_t) * v
    return (yn + bonus) * g


def _rw_post(y, rw, k2, g, pars):
    T = y.shape[0]
    fn = lambda i, *a: ((_rw_out(*a),), ())
    return _rowwise(fn, "rw_post", T, 256, [_cur(y), _cur(rw, RW_W, 0), _cur(rw, RW_W, 2), _cur(k2), _cur(g)], pars,
                    [(RW_W, BF16)], [])[0]


def _rw_post_bwd(y, rw, k2, g, pars, do):
    T = y.shape[0]

    def fn(i, yb, rb, vb, kb, gb, dob, lw, lb, rk, ind, ind_t):
        _, vjp = jax.vjp(lambda *a: _rw_out(*a, ind, ind_t), yb, rb, vb, kb, gb, lw, lb, rk)
        gr = vjp(dob)
        return gr[:5], gr[5:]

    return _rowwise(fn, "rw_post_bwd", T, 256,
                    [_cur(y), _cur(rw, RW_W, 0), _cur(rw, RW_W, 2), _cur(k2), _cur(g), _cur(do)], pars,
                    [(RW_W, F32)] * 5, [(1, RW_W)] * 3)


def _merge_bwd(dm, y_gla, y_rw, pg, gate_b):
    T = dm.shape[0]

    def fn(i, dmb, ya, yr, p1, p2, gb):
        g1 = _sigmoid(p1 + gb[:, :D])
        g2 = _sigmoid(p2 + gb[:, D:])
        dp1 = dmb * ya * g1 * (1.0 - g1)
        dp2 = dmb * yr * g2 * (1.0 - g2)
        dp = jnp.concatenate([dp1, dp2], axis=1)
        return (dmb * g1, dmb * g2, dp), (jnp.sum(dp, axis=0, keepdims=True),)

    return _rowwise(fn, "merge_bwd", T, 256, [_cur(dm), _cur(y_gla), _cur(y_rw), _cur(pg, D, 0), _cur(pg, D, 1)],
                    [gate_b], [(D, BF16), (D, BF16), (PG_W, BF16)], [(1, PG_W)])


_NN = (((1,), (0,)), ((), ()))
_NT = (((1,), (1,)), ((), ()))
_TN = (((0,), (0,)), ((), ()))


def _bdot(a, b, dims):
    return lax.dot_general(a.astype(BF16), b.astype(BF16), dims, preferred_element_type=F32)


def _accumulate(k, nk, acc, part, finish):
    if nk == 1:
        finish(part)
        return

    @pl.when(k == 0)
    def _():
        acc[...] = part

    @pl.when(k > 0)
    def _():
        acc[...] += part

    @pl.when(k == nk - 1)
    def _():
        finish(acc[...])


def _call(body, comm, name, grid, in_specs, out_specs, out_shape, scratch_shapes, sem, operands):
    if comm is None:
        return pl.pallas_call(body, name=name, grid=grid, in_specs=in_specs, out_specs=out_specs, out_shape=out_shape,
                              scratch_shapes=scratch_shapes, compiler_params=_params(sem))(*operands)
    kind, arrays = comm
    nc, n_in, n_out, n_scr = len(arrays), len(in_specs), len(out_shape), len(scratch_shapes)
    total = 1
    for n in grid:
        total *= n

    def with_comm(*refs):
        own = refs[:n_in] + refs[n_in + nc:n_in + nc + n_out] + refs[n_in + 2 * nc + n_out:n_in + 2 * nc + n_out + n_scr]
        c_in, c_out, sems = refs[n_in:n_in + nc], refs[n_in + nc + n_out:n_in + 2 * nc + n_out], refs[-3:]
        step = 0
        for axis, n in enumerate(grid):
            step = step * n + pl.program_id(axis)
        start, *forward, finish = _PLANS[kind][0](c_in, c_out, *sems)
        pl.when(step == 0)(start)
        for stage in forward:
            pl.when(step == (total - 1 if kind == "gather_late" else total // 2))(stage)
        body(*own)
        pl.when(step == total - 1)(finish)

    return pl.pallas_call(
        with_comm, name=name, grid=grid, in_specs=list(in_specs) + [_ANY] * nc, out_specs=list(out_specs) + [_ANY] * nc,
        out_shape=list(out_shape) + [jax.ShapeDtypeStruct(_PLANS[kind][1](a.shape), a.dtype) for a in arrays],
        scratch_shapes=list(scratch_shapes) + _comm_sems(nc), compiler_params=_params(("arbitrary",) * len(grid)),
    )(*operands, *arrays)


def _matmul(a, b, mode, M, N, K, tm, tn, tk, name, a_off=(0, 0), b_off=(0, 0), res=None, scale=1.0, out_dtype=F32,
            comm=None):
    tm, tn, tk = min(tm, M), min(tn, N), min(tk, K)
    nk = K // tk
    if mode == "nn":
        a_spec = pl.BlockSpec((tm, tk), lambda i, j, k: (i + a_off[0], k + a_off[1]))
        b_spec = pl.BlockSpec((tk, tn), lambda i, j, k: (k + b_off[0], j + b_off[1]))
        dims = _NN
    elif mode == "nt":
        a_spec = pl.BlockSpec((tm, tk), lambda i, j, k: (i + a_off[0], k + a_off[1]))
        b_spec = pl.BlockSpec((tn, tk), lambda i, j, k: (j + b_off[0], k + b_off[1]))
        dims = _NT
    else:
        a_spec = pl.BlockSpec((tk, tm), lambda i, j, k: (k + a_off[0], i + a_off[1]))
        b_spec = pl.BlockSpec((tk, tn), lambda i, j, k: (k + b_off[0], j + b_off[1]))
        dims = _TN
    o_spec = pl.BlockSpec((tm, tn), lambda i, j, k: (i, j))

    def body(a_ref, b_ref, *rest):
        r_ref = rest[0] if res is not None else None
        o_ref = rest[1] if res is not None else rest[0]
        acc = rest[-1] if nk > 1 else None

        def finish(total):
            total = total * scale if scale != 1.0 else total
            if r_ref is not None:
                total = r_ref[...] + total
            o_ref[...] = total.astype(out_dtype)

        _accumulate(pl.program_id(2), nk, acc, _bdot(a_ref[...], b_ref[...], dims), finish)

    out = _call(body, comm, name, (M // tm, N // tn, nk), [a_spec, b_spec] + ([o_spec] if res is not None else []),
                [o_spec], [jax.ShapeDtypeStruct((M, N), out_dtype)], [pltpu.VMEM((tm, tn), F32)] if nk > 1 else [],
                ("parallel", "parallel", "arbitrary"), [a, b] + ([res] if res is not None else []))
    return out[0] if comm is None else out


def _ffn_up(h, wg, wu, name, comm=None):
    T = h.shape[0]
    tm = min(1024, T)

    def body(h_ref, wg_ref, wu_ref, a_ref, u_ref, s_ref):
        hb = h_ref[...]
        a = _bdot(hb, wg_ref[...], _NN)
        u = _bdot(hb, wu_ref[...], _NN)
        a_ref[...] = a
        u_ref[...] = u
        s_ref[...] = (_silu(a) * u).astype(BF16)

    w_spec = pl.BlockSpec((None, D, FSH), lambda i, j: (j, 0, 0))
    o_spec = pl.BlockSpec((None, tm, FSH), lambda i, j: (j, i, 0))
    sh = lambda dt: jax.ShapeDtypeStruct((NDEV, T, FSH), dt)
    return _call(body, comm, name, (T // tm, NDEV), [pl.BlockSpec((tm, D), lambda i, j: (i, 0)), w_spec, w_spec],
                 [o_spec] * 3, [sh(F32), sh(F32), sh(BF16)], [], ("parallel", "arbitrary"), [h, wg, wu])


def _ffn_gate(h, wg, name, comm=None):
    T = h.shape[0]
    tm = min(1024, T)

    def body(h_ref, wg_ref, a_ref):
        a_ref[...] = _bdot(h_ref[...], wg_ref[...], _NN)

    return _call(body, comm, name, (T // tm, NDEV),
                 [pl.BlockSpec((tm, D), lambda i, j: (i, 0)), pl.BlockSpec((None, D, FSH), lambda i, j: (j, 0, 0))],
                 [pl.BlockSpec((None, tm, FSH), lambda i, j: (j, i, 0))], [jax.ShapeDtypeStruct((NDEV, T, FSH), F32)], [],
                 ("parallel", "arbitrary"), [h, wg])


def _ffn_up_after_gate(h, wu, a, name, comm=None):
    T = h.shape[0]
    tm = min(1024, T)

    def body(h_ref, wu_ref, a_ref, u_ref, s_ref):
        u = _bdot(h_ref[...], wu_ref[...], _NN)
        u_ref[...] = u
        s_ref[...] = (_silu(a_ref[...]) * u).astype(BF16)

    act = pl.BlockSpec((None, tm, FSH), lambda i, j: (j, i, 0))
    sh = lambda dt: jax.ShapeDtypeStruct((NDEV, T, FSH), dt)
    return _call(body, comm, name, (T // tm, NDEV),
                 [pl.BlockSpec((tm, D), lambda i, j: (i, 0)), pl.BlockSpec((None, D, FSH), lambda i, j: (j, 0, 0)), act],
                 [act, act], [sh(F32), sh(BF16)], [], ("parallel", "arbitrary"), [h, wu, a])


def _ffn_down(s, wd, x, name, comm=None):
    T = x.shape[0]
    tm, tn, sh = min(1024, T), 1024, 4

    def body(s_ref, wd_ref, x_ref, o_ref, acc):
        part = _bdot(s_ref[0], wd_ref[0], _NN)
        for q in range(1, sh):
            part = part + _bdot(s_ref[q], wd_ref[q], _NN)

        def finish(total):
            o_ref[...] = x_ref[...] + 0.5 * total

        _accumulate(pl.program_id(2), NDEV // sh, acc, part, finish)

    xo = pl.BlockSpec((tm, tn), lambda i, n, j: (i, n))
    out = _call(body, comm, name, (T // tm, D // tn, NDEV // sh),
                [pl.BlockSpec((sh, tm, FSH), lambda i, n, j: (j, i, 0)),
                 pl.BlockSpec((sh, FSH, tn), lambda i, n, j: (j, 0, n)), xo],
                [xo], [jax.ShapeDtypeStruct((T, D), F32)], [pltpu.VMEM((tm, tn), F32)],
                ("parallel", "parallel", "arbitrary"), [s, wd, x])
    return out[0] if comm is None else out


def _ffn_bwd_hidden(dx, wd, a, u, name):
    T = dx.shape[0]
    tm = min(1024, T)

    def body(dx_ref, wd_ref, a_ref, u_ref, da_ref, du_ref):
        ds = 0.5 * _bdot(dx_ref[...], wd_ref[...], _NT)
        av = a_ref[...]
        sg = _sigmoid(av)
        da_ref[...] = (ds * u_ref[...] * (sg * (1.0 + av * (1.0 - sg)))).astype(BF16)
        du_ref[...] = (ds * (av * sg)).astype(BF16)

    act = pl.BlockSpec((None, tm, FSH), lambda i, j: (j, i, 0))
    sh = jax.ShapeDtypeStruct((NDEV, T, FSH), BF16)
    return pl.pallas_call(
        body, name=name, grid=(T // tm, NDEV),
        in_specs=[pl.BlockSpec((tm, D), lambda i, j: (i, 0)), pl.BlockSpec((None, FSH, D), lambda i, j: (j, 0, 0)),
                  act, act],
        out_specs=[act, act], out_shape=(sh, sh),
        compiler_params=_params(("parallel", "arbitrary")),
    )(dx, wd, a, u)


def _ffn_bwd_input(da, du, wg, wu, name, comm=None):
    T = da.shape[1]
    tm, tn, sh = min(1024, T), 1024, 2

    def body(da_ref, du_ref, wg_ref, wu_ref, o_ref, acc):
        part = _bdot(da_ref[0], wg_ref[0], _NT) + _bdot(du_ref[0], wu_ref[0], _NT)
        for q in range(1, sh):
            part = part + _bdot(da_ref[q], wg_ref[q], _NT) + _bdot(du_ref[q], wu_ref[q], _NT)

        def finish(total):
            o_ref[...] = total

        _accumulate(pl.program_id(2), NDEV // sh, acc, part, finish)

    act = pl.BlockSpec((sh, tm, FSH), lambda i, n, j: (j, i, 0))
    wsp = pl.BlockSpec((sh, tn, FSH), lambda i, n, j: (j, n, 0))
    out = _call(body, comm, name, (T // tm, D // tn, NDEV // sh), [act, act, wsp, wsp],
                [pl.BlockSpec((tm, tn), lambda i, n, j: (i, n))], [jax.ShapeDtypeStruct((T, D), F32)],
                [pltpu.VMEM((tm, tn), F32)], ("parallel", "parallel", "arbitrary"), [da, du, wg, wu])
    return out[0] if comm is None else out


def _ffn_grad_up(h, da, du, name, comm=None, core_major=False):
    T = h.shape[0]
    tm, tk = 1024, min(4096, T)
    nk = T // tk

    def body(h_ref, da_ref, du_ref, o_ref, acc_a, acc_u):
        k = pl.program_id(2)
        hb = h_ref[...]
        for acc, ref, slot in ((acc_a, da_ref, 0), (acc_u, du_ref, 1)):
            def finish(total, slot=slot):
                o_ref[slot] = total.astype(BF16)

            _accumulate(k, nk, acc, _bdot(hb, ref[...], _TN), finish)

    act = pl.BlockSpec((None, tk, FSH), lambda j, i, t: (j, t, 0))
    if core_major:
        o_spec = pl.BlockSpec((None, None, 2, tm, FSH), lambda j, i, t: (j % 2, j // 2, 0, i, 0))
        o_shape = jax.ShapeDtypeStruct((2, NDEV // 2, 2, D, FSH), BF16)
    else:
        o_spec = pl.BlockSpec((None, 2, tm, FSH), lambda j, i, t: (j, 0, i, 0))
        o_shape = jax.ShapeDtypeStruct((NDEV, 2, D, FSH), BF16)
    out = _call(body, comm, name, (NDEV, D // tm, nk), [pl.BlockSpec((tk, tm), lambda j, i, t: (t, i)), act, act],
                [o_spec], [o_shape],
                [pltpu.VMEM((tm, FSH), F32), pltpu.VMEM((tm, FSH), F32)], ("parallel", "parallel", "arbitrary"), [h, da, du])
    return out[0] if comm is None else out


def _ffn_grad_down(s, dx, name):
    T = dx.shape[0]
    tn, tk = 1024, min(4096, T)
    nk = T // tk

    def body(s_ref, dx_ref, o_ref, acc):
        def finish(total):
            o_ref[...] = (0.5 * total).astype(BF16)

        _accumulate(pl.program_id(2), nk, acc, _bdot(s_ref[...], dx_ref[...], _TN), finish)

    return pl.pallas_call(
        body, name=name, grid=(NDEV, D // tn, nk),
        in_specs=[pl.BlockSpec((None, tk, FSH), lambda j, n, t: (j, t, 0)), pl.BlockSpec((tk, tn), lambda j, n, t: (t, n))],
        out_specs=pl.BlockSpec((None, FSH, tn), lambda j, n, t: (j, 0, n)),
        out_shape=jax.ShapeDtypeStruct((NDEV, DFF // NDEV, D), BF16),
        scratch_shapes=[pltpu.VMEM((FSH, tn), F32)],
        compiler_params=_params(("parallel", "parallel", "arbitrary")),
    )(s, dx)


def _branch_merge(o_gla, o_rw, wb, pg, gate_b):
    T = o_gla.shape[0]
    tm, tn = min(1024, T), 512

    def body(og_ref, or_ref, w1_ref, w2_ref, p1_ref, p2_ref, b1_ref, b2_ref, yg_ref, yr_ref, m_ref):
        yg = _bdot(og_ref[...], w1_ref[...], _NN)
        yr = _bdot(or_ref[...], w2_ref[...], _NN)
        yg_ref[...] = yg
        yr_ref[...] = yr
        m_ref[...] = (_sigmoid(p1_ref[...] + b1_ref[...]) * yg + _sigmoid(p2_ref[...] + b2_ref[...]) * yr).astype(BF16)

    nj = D // tn
    act = pl.BlockSpec((tm, GLA_V), lambda i, j: (i, 0))
    out = pl.BlockSpec((tm, tn), lambda i, j: (i, j))
    return pl.pallas_call(
        body, name="branch_merge", grid=(T // tm, nj),
        in_specs=[act, act, pl.BlockSpec((GLA_V, tn), lambda i, j: (0, j)), pl.BlockSpec((RW_W, tn), lambda i, j: (1, j)),
                  out, pl.BlockSpec((tm, tn), lambda i, j: (i, nj + j)),
                  pl.BlockSpec((1, tn), lambda i, j: (0, j)), pl.BlockSpec((1, tn), lambda i, j: (0, nj + j))],
        out_specs=[out, out, out],
        out_shape=(jax.ShapeDtypeStruct((T, D), F32), jax.ShapeDtypeStruct((T, D), F32), jax.ShapeDtypeStruct((T, D), BF16)),
        compiler_params=_params(("parallel", "arbitrary")),
    )(o_gla, o_rw, wb, wb, pg, pg, gate_b, gate_b)


def _head_indicator(width, heads):
    col = lax.broadcasted_iota(jnp.int32, (width, 128), 0) // (width // heads)
    ind = (col == lax.broadcasted_iota(jnp.int32, (width, 128), 1)).astype(F32)
    return ind, ind.T


def _ffn_fwd(x, g, wg, wu, wd, tag):
    h = _rms_fwd(x, g, "rms_" + tag)
    a, u, s = _ffn_up(h, wg, wu, "ffn_up_" + tag)
    return _ffn_down(s, wd, x, "ffn_down_" + tag), (h, a, u, s)


def _ffn_fwd_gathering(x, g, wg, wu_block, wd_block, next_block, tag):
    h = _rms_fwd(x, g, "rms_" + tag)
    a, wu = _ffn_gate(h, wg, "ffn_gate_" + tag, comm=("gather_late", [wu_block]))
    u, s, wd = _ffn_up_after_gate(h, wu, a, "ffn_up_" + tag, comm=("gather_late", [wd_block]))
    y, gathered = _ffn_down(s, wd, x, "ffn_down_" + tag, comm=("gather_late", [next_block]))
    return y, (h, a, u, s), wu, wd, gathered


def _ffn_bwd(dy, dy_b, x, g, wg, wu, wd, saved, tag, exchange=False):
    h, a, u, s = saved
    dwd = _ffn_grad_down(s, dy_b, "ffn_grad_down_" + tag)
    da, du = _ffn_bwd_hidden(dy_b, wd, a, u, "ffn_bwd_hidden_" + tag)
    if exchange:
        dw_up, dwd = _ffn_grad_up(h, da, du, "ffn_grad_up_" + tag, comm=("exchange", [dwd]), core_major=True)
        dh, dw_up = _ffn_bwd_input(da, du, wg, wu, "ffn_bwd_input_" + tag, comm=("quad", [_chip_sum(dw_up, "up_" + tag)]))
    else:
        dw_up = _ffn_grad_up(h, da, du, "ffn_grad_up_" + tag)
        dh = _ffn_bwd_input(da, du, wg, wu, "ffn_bwd_input_" + tag)
    *dx, dg = _rms_bwd(x, g, dh, dy, "rms_bwd_" + tag, matmul_copy=not exchange)
    return (*dx, dg, dw_up, dwd)


def _local_step(x, target, w, blocks):
    T = x.shape[0]
    ind16, ind16_t = _head_indicator(RW_W, RW_HEADS)
    ind4, ind4_t = _head_indicator(GLA_V, GLA_HEADS)
    ltri = jnp.tril(jnp.ones((CHUNK, CHUNK), F32))
    gate_pars = [w["w0"], w["w_w2"], w["a0"], w["w_a2"], w["w_g2"], w["k_k"], w["k_a"], ind16, ind16_t]
    post_pars = [w["lnx_w"], w["lnx_b"], w["r_k"], ind16, ind16_t]

    x1, ffn1, wu1, wd1, g_proj = _ffn_fwd_gathering(x, w["g1"], w["wg1"], blocks["wu1"], blocks["wd1"], blocks["win"], "1")
    win = _align_proj(_unshard_cols(g_proj))
    h2 = _rms_fwd(x1, w["g2"], "rms_mix")
    proj = lambda n, off, name: _matmul(h2, win, "nn", T, n, D, 1024, 512, D, name, b_off=(0, off // 512))
    pg = proj(PG_W, 0, "proj_gate")
    pr = proj(PR_W, PG_W, "proj_rwkv")
    pa = proj(PA_W, PG_W + PR_W, "proj_gla")
    la = _gla_prep(pa, w["gla_w_a2"], w["gla_b_a"])
    o_raw, gla_states = _gla_core_fwd(pa, la, ltri)
    o_gla = _gla_post(o_raw, pa, w["gn"], ind4, ind4_t)
    rw, dec, k2, kk, b, g = _rw_prep(pr, w["mu"], gate_pars)
    y, rw_states, rw_sa, g_up2, g_down2 = _rw_core_fwd(rw, dec, k2, kk, b, gather=blocks["late"])
    w = {**w, **_late_weights(g_up2, g_down2)}
    o_rw = _rw_post(y, rw, k2, g, post_pars)
    y_gla, y_rw, merged = _branch_merge(o_gla, o_rw, w["wb"], pg, w["gate_b"])
    x2 = _matmul(merged, w["wo"], "nn", T, D, D, 1024, 1024, D, "out_proj", res=x1)
    x3, ffn2 = _ffn_fwd(x2, w["g3"], w["wg2"], w["wu2"], w["wd2"], "2")
    dx3, dx3_b, loss, d_gf = _loss_bwd(x3, target, w["gf"])

    grads = {"gf": d_gf}
    dx2, dx2_b, grads["g3"], grads["up2"], grads["wd2"] = _ffn_bwd(
        dx3, dx3_b, x2, w["g3"], w["wg2"], w["wu2"], w["wd2"], ffn2, "2")
    dm = _matmul(dx2_b, w["wo"], "nt", T, D, D, 1024, 1024, D, "out_proj_bwd")
    grads["wo"] = _matmul(merged, dx2_b, "tn", D, D, T, 1024, 1024, 4096, "out_proj_grad", out_dtype=BF16)
    dy_gla, dy_rw, dpg, grads["gate_b"] = _merge_bwd(dm, y_gla, y_rw, pg, w["gate_b"])
    do_gla = _matmul(dy_gla, w["wb"], "nt", T, GLA_V, D, 1024, 1024, D, "branch_gla_bwd")
    do_rw = _matmul(dy_rw, w["wb"], "nt", T, RW_W, D, 1024, 1024, D, "branch_rwkv_bwd", b_off=(1, 0))
    grads["wb"] = jnp.concatenate([
        _matmul(o_gla, dy_gla, "tn", GLA_V, D, T, 1024, 1024, 4096, "branch_gla_grad", out_dtype=BF16),
        _matmul(o_rw, dy_rw, "tn", RW_W, D, T, 1024, 1024, 4096, "branch_rwkv_grad", out_dtype=BF16)], axis=0)
    dy, dr2, dv2, dk2b, dg, grads["lnx_w"], grads["lnx_b"], grads["r_k"] = _rw_post_bwd(y, rw, k2, g, post_pars, do_rw)
    early = _late_grad_parts(grads)
    received = {}
    dr1, dw, dk2a, dv1, dkk, db, received["up2"], received["down2"] = _rw_core_bwd(
        rw, dec, k2, kk, b, rw_states, rw_sa, dy, exchange=early)
    drw, grads["mu"], grads["w0"], grads["w_w2"], grads["a0"], grads["w_a2"], grads["w_g2"], grads["k_k"], grads["k_a"] = (
        _rw_prep_bwd(pr, w["mu"], gate_pars, (dr1, dr2), (dv1, dv2), dw, (dk2a, dk2b), dkk, db, dg))
    dpr = _shift_bwd(drw, w["mu"])
    do_raw, dr_gla, grads["gn"] = _gla_post_bwd(o_raw, pa, w["gn"], ind4, ind4_t, do_gla)
    dq, dk, dv, dla = _gla_core_bwd(pa, la, ltri, gla_states, do_raw)
    da_down, grads["gla_w_a2"], grads["gla_b_a"] = _gla_prep_bwd(pa, w["gla_w_a2"], w["gla_b_a"], dla)
    dpa = jnp.concatenate([dq.astype(BF16), dk.astype(BF16), dv.astype(BF16), dr_gla, da_down,
                           jnp.zeros((T, PA_W - PA_USED), BF16)], axis=1)
    dp = jnp.concatenate([dpg, dpr, dpa], axis=1)
    d_win = _matmul(h2, dp, "tn", D, DIN_P, T, 1024, 1024, 4096, "proj_grad", out_dtype=BF16)
    d_win = _shard_cols(_unalign_proj(d_win)).reshape(NDEV // 2, 2, D, DIN_SH).swapaxes(0, 1)
    dh2, received["win"] = _matmul(dp, win, "nt", T, D, DIN_P, 1024, 1024, DIN_P // 4, "proj_bwd",
                                   comm=("quad", [_chip_sum(d_win, "win")]))
    dx1, dx1_b, grads["g2"] = _rms_bwd(x1, w["g2"], dh2, dx2, "rms_bwd_mix", matmul_copy=True)
    dx, grads["g1"], received["up1"], received["wd1"] = _ffn_bwd(
        dx1, dx1_b, x, w["g1"], w["wg1"], wu1, wd1, ffn1, "1", exchange=True)
    return loss, dx, grads, received


BIG = ("ffn1_wg", "ffn1_wu", "ffn1_wd", "w_in", "w_branch", "w_out", "ffn2_wg", "ffn2_wu", "ffn2_wd")
SMALL_SHARDED = ("gla_w_a2", "rwkv_w_w2", "rwkv_w_a2", "rwkv_w_g2")
REPLICATED = ("ffn1_norm", "mix_norm", "gla_b_a", "gla_gn_w", "rwkv_mu", "rwkv_w0", "rwkv_a0", "rwkv_k_k", "rwkv_k_a",
              "rwkv_r_k", "rwkv_lnx_w", "rwkv_lnx_b", "gate_b", "ffn2_norm", "final_norm")
WEIGHTS = ("ffn1_norm", "ffn1_wg", "ffn1_wu", "ffn1_wd", "mix_norm", "w_in", "gla_w_a2", "gla_b_a", "gla_gn_w",
           "rwkv_mu", "rwkv_w0", "rwkv_w_w2", "rwkv_a0", "rwkv_w_a2", "rwkv_w_g2", "rwkv_k_k", "rwkv_k_a", "rwkv_r_k",
           "rwkv_lnx_w", "rwkv_lnx_b", "gate_b", "w_branch", "w_out", "ffn2_norm", "ffn2_wg", "ffn2_wu", "ffn2_wd",
           "final_norm")


def _unshard_cols(g):
    return jnp.transpose(g, (1, 0, 2)).reshape(g.shape[1], NDEV * g.shape[2])


def _shard_cols(a):
    return jnp.transpose(a.reshape(a.shape[0], NDEV, a.shape[1] // NDEV), (1, 0, 2))


def _pad_rows(a, rows):
    return jnp.pad(a, ((0, rows - a.shape[0]), (0, 0)))


def _align_rw(a):
    c = 3 * RW_W
    z = jnp.zeros((a.shape[0], LORA_P - DECAY_LORA), a.dtype)
    return jnp.concatenate([a[:, :c], a[:, c:c + DECAY_LORA], z, a[:, c + DECAY_LORA:c + 2 * DECAY_LORA], z,
                            a[:, c + 2 * DECAY_LORA:]], axis=1)


def _unalign_rw(a):
    c = 3 * RW_W
    return jnp.concatenate([a[:, :c + DECAY_LORA], a[:, c + LORA_P:c + LORA_P + AAA_LORA], a[:, c + 2 * LORA_P:]], axis=1)


def _align_proj(a):
    gla = jnp.pad(a[:, :GLA_IN], ((0, 0), (0, PA_W - GLA_IN)))
    return jnp.concatenate([a[:, GLA_IN + RW_IN:], _align_rw(a[:, GLA_IN:GLA_IN + RW_IN]), gla], axis=1)


def _unalign_proj(a):
    return jnp.concatenate([a[:, PG_W + PR_W:PG_W + PR_W + GLA_IN], _unalign_rw(a[:, PG_W:PG_W + PR_W]), a[:, :PG_W]], axis=1)


def _layout_weights(gb, gs, rep):
    row = lambda n: rep[n].reshape(1, -1)
    return {
        "wg1": gb["ffn1_wg"],
        "g1": row("ffn1_norm"), "g2": row("mix_norm"), "g3": row("ffn2_norm"), "gf": row("final_norm"),
        "gla_w_a2": _pad_rows(_unshard_cols(gs["gla_w_a2"]), LORA_P), "gla_b_a": row("gla_b_a"),
        "gn": jnp.tile(row("gla_gn_w"), (1, GLA_HEADS)),
        "mu": _align_rw(row("rwkv_mu")), "w0": row("rwkv_w0"), "a0": row("rwkv_a0"),
        "w_w2": _pad_rows(_unshard_cols(gs["rwkv_w_w2"]), LORA_P),
        "w_a2": _pad_rows(_unshard_cols(gs["rwkv_w_a2"]), LORA_P),
        "w_g2": _unshard_cols(gs["rwkv_w_g2"]),
        "k_k": row("rwkv_k_k"), "k_a": row("rwkv_k_a"), "r_k": row("rwkv_r_k"),
        "lnx_w": row("rwkv_lnx_w"), "lnx_b": row("rwkv_lnx_b"), "gate_b": row("gate_b"),
    }


LATE_ROWS = (("ffn2_wd", FSH), ("w_branch", (GLA_V + RW_W) // NDEV), ("w_out", D // NDEV))


def _late_weights(g_up, g_down):
    r1, r2 = LATE_ROWS[0][1], LATE_ROWS[0][1] + LATE_ROWS[1][1]
    return {"wg2": g_up[:, 0], "wu2": g_up[:, 1], "wd2": g_down[:, :r1],
            "wb": g_down[:, r1:r2].reshape(GLA_V + RW_W, D), "wo": g_down[:, r2:].reshape(D, D)}


def _late_grad_parts(g):
    return [g["up2"], jnp.concatenate([g["wd2"], g["wb"].reshape(NDEV, -1, D), g["wo"].reshape(NDEV, -1, D)], axis=1)]


def _layout_grads(g):
    return {
        "ffn1_norm": g["g1"], "mix_norm": g["g2"], "ffn2_norm": g["g3"], "final_norm": g["gf"],
        "gla_w_a2": g["gla_w_a2"][:GLA_LORA], "gla_b_a": g["gla_b_a"],
        "gla_gn_w": jnp.sum(g["gn"].reshape(GLA_HEADS, GLA_DV), axis=0, keepdims=True),
        "rwkv_mu": _unalign_rw(g["mu"]), "rwkv_w0": g["w0"], "rwkv_a0": g["a0"],
        "rwkv_w_w2": g["w_w2"][:DECAY_LORA], "rwkv_w_a2": g["w_a2"][:AAA_LORA], "rwkv_w_g2": g["w_g2"],
        "rwkv_k_k": g["k_k"], "rwkv_k_a": g["k_a"], "rwkv_r_k": g["r_k"],
        "rwkv_lnx_w": g["lnx_w"], "rwkv_lnx_b": g["lnx_b"], "gate_b": g["gate_b"],
    }


_MESH = pl.DeviceIdType.MESH
_ANY = pl.BlockSpec(memory_space=pl.ANY)


def _position():
    return lax.axis_index("x"), lax.axis_index("y"), lax.axis_index("c")


def _slot(p):
    return 4 * p[0] + 2 * p[1] + p[2]


def _comm_sems(n):
    if not n:
        return []
    return [pltpu.SemaphoreType.DMA((7 * n,)), pltpu.SemaphoreType.DMA((7 * n,)), pltpu.SemaphoreType.DMA((n,))]


def _gather_plan(ins, outs, send_sems, recv_sems, local_sems):
    n = len(ins)
    x, y, c = _position()
    me, sibling = (x, y, c), (x, y, 1 - c)
    chips = [(1 - x, y), (x, 1 - y), (1 - x, 1 - y)]

    def copy(a, k, block, to, src=None):
        dst = outs[a].at[_slot(block)]
        return pltpu.make_async_remote_copy(
            src_ref=dst if src is None else src, dst_ref=dst, send_sem=send_sems.at[7 * a + k],
            recv_sem=recv_sems.at[7 * a + k], device_id=to, device_id_type=_MESH)

    def local(a):
        return pltpu.make_async_copy(ins[a], outs[a].at[_slot(me)], local_sems.at[a])

    def own(a):
        return [copy(a, 0, me, sibling, src=ins[a])] + [copy(a, 1 + j, me, (*chip, c), src=ins[a]) for j, chip in enumerate(chips)]

    def start():
        for a in range(n):
            local(a).start()
            for cp in own(a):
                cp.start()

    def forward():
        for a in range(n):
            for j, chip in enumerate(chips):
                copy(a, 1 + j, (*chip, c), me).wait_recv()
                copy(a, 4 + j, (*chip, c), sibling).start()

    def finish():
        for a in range(n):
            copy(a, 0, sibling, me).wait_recv()
            for j, chip in enumerate(chips):
                copy(a, 4 + j, (*chip, 1 - c), me).wait_recv()
        for a in range(n):
            for cp in own(a) + [copy(a, 4 + j, (*chip, c), sibling) for j, chip in enumerate(chips)]:
                cp.wait_send()
            local(a).wait()

    return start, forward, finish


def _exchange_plan(ins, outs, send_sems, recv_sems, local_sems):
    n = len(ins)
    x, y, c = _position()
    me = (x, y, c)
    flip = lambda v, f: 1 - v if f else v
    peers = [(flip(x, fx), flip(y, fy), flip(c, fc))
             for fx, fy, fc in ((0, 0, 1), (1, 0, 0), (0, 1, 0), (1, 1, 0), (1, 0, 1), (0, 1, 1), (1, 1, 1))]

    def copy(a, k, src_slot, dst_slot):
        return pltpu.make_async_remote_copy(
            src_ref=ins[a].at[src_slot], dst_ref=outs[a].at[dst_slot], send_sem=send_sems.at[7 * a + k],
            recv_sem=recv_sems.at[7 * a + k], device_id=peers[k], device_id_type=_MESH)

    def local(a):
        return pltpu.make_async_copy(ins[a].at[_slot(me)], outs[a].at[_slot(me)], local_sems.at[a])

    def start():
        for a in range(n):
            local(a).start()
            for k, peer in enumerate(peers):
                copy(a, k, _slot(peer), _slot(me)).start()

    def finish():
        for a in range(n):
            for k, peer in enumerate(peers):
                copy(a, k, _slot(peer), _slot(peer)).wait_recv()
        for a in range(n):
            for k, peer in enumerate(peers):
                copy(a, k, _slot(peer), _slot(me)).wait_send()
            local(a).wait()

    return start, finish


def _sibling_plan(ins, outs, send_sems, recv_sems, local_sems):
    x, y, c = _position()

    def copy(a):
        return pltpu.make_async_remote_copy(
            src_ref=ins[a].at[1 - c], dst_ref=outs[a], send_sem=send_sems.at[7 * a], recv_sem=recv_sems.at[7 * a],
            device_id=(x, y, 1 - c), device_id_type=_MESH)

    def start():
        for a in range(len(ins)):
            copy(a).start()

    def finish():
        for a in range(len(ins)):
            copy(a).wait()

    return start, finish


def _quad_plan(ins, outs, send_sems, recv_sems, local_sems):
    n = len(ins)
    x, y, c = _position()
    mine = 2 * x + y
    peers = [(1 - x, y), (x, 1 - y), (1 - x, 1 - y)]

    def copy(a, k, src_slot, dst_slot):
        return pltpu.make_async_remote_copy(
            src_ref=ins[a].at[src_slot], dst_ref=outs[a].at[dst_slot], send_sem=send_sems.at[7 * a + k],
            recv_sem=recv_sems.at[7 * a + k], device_id=(*peers[k], c), device_id_type=_MESH)

    def local(a):
        return pltpu.make_async_copy(ins[a].at[mine], outs[a].at[mine], local_sems.at[a])

    def start():
        for a in range(n):
            local(a).start()
            for k, (px, py) in enumerate(peers):
                copy(a, k, 2 * px + py, mine).start()

    def finish():
        for a in range(n):
            for k, (px, py) in enumerate(peers):
                copy(a, k, 2 * px + py, 2 * px + py).wait_recv()
        for a in range(n):
            for k, (px, py) in enumerate(peers):
                copy(a, k, 2 * px + py, mine).wait_send()
            local(a).wait()

    return start, finish


_PLANS = {"gather": (_gather_plan, lambda s: (NDEV,) + s), "gather_late": (_gather_plan, lambda s: (NDEV,) + s),
          "exchange": (_exchange_plan, lambda s: s),
          "sibling": (_sibling_plan, lambda s: s[1:]), "quad": (_quad_plan, lambda s: s)}


def _exchange_now(kind, arrays, name):
    n = len(arrays)

    def body(*refs):
        for stage in _PLANS[kind][0](refs[:n], refs[n:2 * n], *refs[2 * n:]):
            stage()

    return pl.pallas_call(
        body, name=name, in_specs=[_ANY] * n, out_specs=[_ANY] * n,
        out_shape=[jax.ShapeDtypeStruct(_PLANS[kind][1](a.shape), a.dtype) for a in arrays], scratch_shapes=_comm_sems(n),
    )(*arrays)


def _chip_sum(parts, name):
    (theirs,) = _exchange_now("sibling", [parts], "chip_send_" + name)
    shape = theirs.shape
    rows = shape[-2]
    for d in shape[1:-2]:
        rows *= d
    flat = (4, rows, shape[-1])
    tr = 512

    def body(both_ref, b_ref, o_ref):
        mine = jnp.where(lax.axis_index("c") == 0, both_ref[0], both_ref[1])
        o_ref[...] = (mine.astype(F32) + b_ref[...].astype(F32)).astype(BF16)

    blk = pl.BlockSpec((None, tr, shape[-1]), lambda s, r: (s, r, 0))
    out = pl.pallas_call(
        body, name="chip_sum_" + name, grid=(4, rows // tr),
        in_specs=[pl.BlockSpec((2, None, tr, shape[-1]), lambda s, r: (0, s, r, 0)), blk], out_specs=blk,
        out_shape=jax.ShapeDtypeStruct(flat, BF16), compiler_params=_params(("parallel", "parallel")),
    )(parts.reshape((2,) + flat), theirs.reshape(flat))
    return out.reshape(shape)


def _adamw_math(w, g, m, v):
    m = ADAM_B1 * m + (1.0 - ADAM_B1) * g
    v = ADAM_B2 * v + (1.0 - ADAM_B2) * (g * g)
    m_hat = m / (1.0 - ADAM_B1 ** ADAM_STEP)
    v_hat = v / (1.0 - ADAM_B2 ** ADAM_STEP)
    delta = -ADAM_LR * (m_hat / (jnp.sqrt(v_hat) + ADAM_EPS) + ADAM_WD * w)
    return delta, m, v


def _sum_slots(ref):
    total = ref[0].astype(F32)
    for s in range(1, ref.shape[0]):
        total = total + ref[s].astype(F32)
    return total


def _adamw(parts, w, m, v, tr, name, stack_index=None, row_block_offset=0, transposed=False):
    _, R, C = w.shape

    def body(p_ref, w_ref, m_ref, v_ref, g_ref, d_ref, nm_ref, nv_ref):
        g = _sum_slots(p_ref)
        g = g.T if transposed else g
        g_ref[...] = g
        d_ref[...], nm_ref[...], nv_ref[...] = _adamw_math(w_ref[...], g, m_ref[...], v_ref[...])

    if stack_index is None:
        p_spec = pl.BlockSpec((parts.shape[0], tr, C), lambda r: (0, row_block_offset + r, 0))
    else:
        p_spec = pl.BlockSpec((parts.shape[0], None, tr, C), lambda r: (0, stack_index, r, 0))
    if transposed:
        blk = pl.BlockSpec((None, C, tr), lambda r: (0, 0, r))
        out = jax.ShapeDtypeStruct((1, C, R), F32)
        w, m, v = (jnp.swapaxes(a, 1, 2) for a in (w, m, v))
    else:
        blk = pl.BlockSpec((None, tr, C), lambda r: (0, r, 0))
        out = jax.ShapeDtypeStruct((1, R, C), F32)
    res = pl.pallas_call(
        body, name=name, grid=(R // tr,), in_specs=[p_spec, blk, blk, blk], out_specs=[blk] * 4, out_shape=(out,) * 4,
        compiler_params=_params(("parallel",)),
    )(parts, w, m, v)
    return [jnp.swapaxes(a, 1, 2) for a in res] if transposed else res


def _sum_gathered(parts):
    _, R, C = parts.shape

    def body(p_ref, o_ref):
        o_ref[...] = _sum_slots(p_ref)

    return pl.pallas_call(body, name="small_grad_sum", out_shape=jax.ShapeDtypeStruct((R, C), F32),
                          compiler_params=_params())(parts)


def _adamw_small(w, g, m, v):
    def body(w_ref, g_ref, m_ref, v_ref, d_ref, nm_ref, nv_ref):
        d_ref[...], nm_ref[...], nv_ref[...] = _adamw_math(w_ref[...], g_ref[...], m_ref[...], v_ref[...])

    out = jax.ShapeDtypeStruct(w.shape, F32)
    return pl.pallas_call(body, name="adamw_small", out_shape=(out,) * 3, compiler_params=_params())(w, g, m, v)


def _pack(pieces, rows):
    flat = jnp.concatenate([p.reshape(-1) for p in pieces])
    return jnp.pad(flat, (0, rows * 128 - flat.shape[0])).reshape(rows, 128)


def _unpack(packed, shapes):
    flat = packed.reshape(-1)
    out, off = [], 0
    for s in shapes:
        size = 1
        for d in s:
            size *= d
        out.append(flat[off:off + size].reshape(s))
        off += size
    return out


def _rows_for(shapes, extra=0):
    total = extra
    for s in shapes:
        size = 1
        for d in s:
            size *= d
        total += size
    return -(-total // 1024) * 8


def kernel(x, ffn1_norm, ffn1_wg, ffn1_wu, ffn1_wd, mix_norm, w_in, gla_w_a2, gla_b_a, gla_gn_w, rwkv_mu, rwkv_w0, rwkv_w_w2, rwkv_a0, rwkv_w_a2, rwkv_w_g2, rwkv_k_k, rwkv_k_a, rwkv_r_k, rwkv_lnx_w, rwkv_lnx_b, gate_b, w_branch, w_out, ffn2_norm, ffn2_wg, ffn2_wu, ffn2_wd, final_norm, loss_target, m_ffn1_norm, m_ffn1_wg, m_ffn1_wu, m_ffn1_wd, m_mix_norm, m_w_in, m_gla_w_a2, m_gla_b_a, m_gla_gn_w, m_rwkv_mu, m_rwkv_w0, m_rwkv_w_w2, m_rwkv_a0, m_rwkv_w_a2, m_rwkv_w_g2, m_rwkv_k_k, m_rwkv_k_a, m_rwkv_r_k, m_rwkv_lnx_w, m_rwkv_lnx_b, m_gate_b, m_w_branch, m_w_out, m_ffn2_norm, m_ffn2_wg, m_ffn2_wu, m_ffn2_wd, m_final_norm, v_ffn1_norm, v_ffn1_wg, v_ffn1_wu, v_ffn1_wd, v_mix_norm, v_w_in, v_gla_w_a2, v_gla_b_a, v_gla_gn_w, v_rwkv_mu, v_rwkv_w0, v_rwkv_w_w2, v_rwkv_a0, v_rwkv_w_a2, v_rwkv_w_g2, v_rwkv_k_k, v_rwkv_k_a, v_rwkv_r_k, v_rwkv_lnx_w, v_rwkv_lnx_b, v_gate_b, v_w_branch, v_w_out, v_ffn2_norm, v_ffn2_wg, v_ffn2_wu, v_ffn2_wd, v_final_norm):
    wts = dict(zip(WEIGHTS, (ffn1_norm, ffn1_wg, ffn1_wu, ffn1_wd, mix_norm, w_in, gla_w_a2, gla_b_a, gla_gn_w, rwkv_mu, rwkv_w0, rwkv_w_w2, rwkv_a0, rwkv_w_a2, rwkv_w_g2, rwkv_k_k, rwkv_k_a, rwkv_r_k, rwkv_lnx_w, rwkv_lnx_b, gate_b, w_branch, w_out, ffn2_norm, ffn2_wg, ffn2_wu, ffn2_wd, final_norm)))
    mom = dict(zip(WEIGHTS, (m_ffn1_norm, m_ffn1_wg, m_ffn1_wu, m_ffn1_wd, m_mix_norm, m_w_in, m_gla_w_a2, m_gla_b_a, m_gla_gn_w, m_rwkv_mu, m_rwkv_w0, m_rwkv_w_w2, m_rwkv_a0, m_rwkv_w_a2, m_rwkv_w_g2, m_rwkv_k_k, m_rwkv_k_a, m_rwkv_r_k, m_rwkv_lnx_w, m_rwkv_lnx_b, m_gate_b, m_w_branch, m_w_out, m_ffn2_norm, m_ffn2_wg, m_ffn2_wu, m_ffn2_wd, m_final_norm)))
    var = dict(zip(WEIGHTS, (v_ffn1_norm, v_ffn1_wg, v_ffn1_wu, v_ffn1_wd, v_mix_norm, v_w_in, v_gla_w_a2, v_gla_b_a, v_gla_gn_w, v_rwkv_mu, v_rwkv_w0, v_rwkv_w_w2, v_rwkv_a0, v_rwkv_w_a2, v_rwkv_w_g2, v_rwkv_k_k, v_rwkv_k_a, v_rwkv_r_k, v_rwkv_lnx_w, v_rwkv_lnx_b, v_gate_b, v_w_branch, v_w_out, v_ffn2_norm, v_ffn2_wg, v_ffn2_wu, v_ffn2_wd, v_final_norm)))
    two = lambda a: a.reshape(a.shape[-2:])

    bf = lambda n: two(wts[n]).astype(BF16)
    lora = jnp.concatenate([jnp.pad(two(gla_w_a2), ((0, 0), (0, 128 - GLA_QK // NDEV)))] +
                           [two(wts[n]) for n in SMALL_SHARDED[1:]], axis=0)
    g_wg1, g_lora = _exchange_now("gather", [bf("ffn1_wg"), lora], "gather_weights")
    gb = {"ffn1_wg": g_wg1}
    gs = {"gla_w_a2": g_lora[:, :GLA_LORA, :GLA_QK // NDEV]}
    row = GLA_LORA
    for n in SMALL_SHARDED[1:]:
        gs[n] = g_lora[:, row:row + wts[n].shape[1]]
        row += wts[n].shape[1]
    w = _layout_weights(gb, gs, {n: wts[n] for n in REPLICATED})
    blocks = {"wu1": bf("ffn1_wu"), "wd1": bf("ffn1_wd"), "win": bf("w_in"),
              "late": [jnp.stack([bf("ffn2_wg"), bf("ffn2_wu")]), jnp.concatenate([bf(n) for n, _ in LATE_ROWS], axis=0)]}

    loss_part, grad_x, grads, parts = _local_step(x[0], loss_target[0], w, blocks)
    small = _layout_grads(grads)
    result = {}
    state = lambda n: (wts[n], mom[n], var[n])
    for group, names in (("up1", ("ffn1_wg", "ffn1_wu")), ("up2", ("ffn2_wg", "ffn2_wu"))):
        for i, n in enumerate(names):
            result[n] = _adamw(parts[group], *state(n), 256, "adamw_" + n, stack_index=i, transposed=True)
    result["ffn1_wd"] = _adamw(parts["wd1"], *state("ffn1_wd"), 64, "adamw_ffn1_wd")
    row = 0
    for n, rows in LATE_ROWS:
        result[n] = _adamw(parts["down2"], *state(n), 64, "adamw_" + n, row_block_offset=row // 64)
        row += rows
    result["w_in"] = _adamw(parts["win"], *state("w_in"), 256, "adamw_w_in")

    small_names = [n for n in WEIGHTS if n not in BIG]
    full_shapes = [small[n].shape for n in small_names]
    rows_full = _rows_for(full_shapes, extra=128)
    packed = _pack([small[n] for n in small_names] + [loss_part], rows_full)
    (gathered,) = _exchange_now("gather", [packed], "gather_small_grads")
    total = _sum_gathered(gathered)
    *full_grads, loss_row = _unpack(total, full_shapes + [(1, 128)])
    me = _slot(_position())
    own = {}
    for n, g in zip(small_names, full_grads):
        if n in SMALL_SHARDED:
            cols = wts[n].shape[-1]
            g = lax.dynamic_slice_in_dim(g, me * cols, cols, axis=1)
        own[n] = g.reshape(wts[n].shape)
    own_shapes = [wts[n].shape for n in small_names]
    rows_own = _rows_for(own_shapes)
    pk = lambda d: _pack([d[n] for n in small_names], rows_own)
    d_s, m_s, v_s = _adamw_small(pk(wts), pk(own), pk(mom), pk(var))
    for n, d, m, v in zip(small_names, _unpack(d_s, own_shapes), _unpack(m_s, own_shapes), _unpack(v_s, own_shapes)):
        result[n] = (own[n], d, m, v)

    shaped = lambda n, k: result[n][k].reshape(wts[n].shape)
    return (loss_row[0, 0], grad_x[None],
            *[shaped(n, 0) for n in WEIGHTS], *[shaped(n, 1) for n in WEIGHTS],
            *[shaped(n, 2) for n in WEIGHTS], *[shaped(n, 3) for n in WEIGHTS])
```
